```python
import jax, jax.numpy as jnp
from jax import lax
import numpy as np

D_MODEL = 1024
BATCH = 16
SEQ = 256
DEPTH = 2
DEC_BATCH = 4
DEC_SEQ = 2048
PAST_LEN = 512

GRID_W = 64
HEAD_DIM = 64
A_HEADS = 4
A_KV_HEADS = 2
WINDOW = 128
B_HEADS = 4
B_NOPE = 64
B_ROPE = 32
B_V = 128
B_Q_LORA = 384
B_KV_LORA = 256
C_HEADS = 4
C_KV_HEADS = 2
MIX_W = A_HEADS * HEAD_DIM + B_HEADS * B_V + C_HEADS * HEAD_DIM
IN_SIZES = (A_HEADS * HEAD_DIM, A_KV_HEADS * HEAD_DIM, A_KV_HEADS * HEAD_DIM,
            B_Q_LORA, B_KV_LORA, B_ROPE,
            C_HEADS * HEAD_DIM, C_KV_HEADS * HEAD_DIM, C_KV_HEADS * HEAD_DIM)
IN_COLS = (A_HEADS + 2 * A_KV_HEADS) * HEAD_DIM + B_Q_LORA + B_KV_LORA + B_ROPE + (C_HEADS + 2 * C_KV_HEADS) * HEAD_DIM
N_EXPERTS = 32
TOP_K = 4
D_FF = D_MODEL
SWIGLU_LIMIT = 7.0
SWIGLU_ALPHA = 1.702
MOE_BLOCK = 128
Q_BLOCK = 128
ROPE_THETA = 10000.0
EPS = 1e-6
NEG_INF = -1e30
F32 = jnp.float32

kernel_name = "hymba_mla_swa_moe_diffusion_step"


def rms_norm(x, g):
    xf = x.astype(F32)
    y = xf * lax.rsqrt(jnp.mean(xf * xf, axis=-1, keepdims=True) + EPS)
    return (y * g.astype(F32)).astype(x.dtype)


def adaln(cond, w, b):
    m = jnp.dot(jax.nn.silu(cond), w) + b
    return jnp.split(m[:, None, :], 6, axis=-1)


def modulate(h, shift, scale):
    return h * (1 + scale) + shift


def heads(x, n):
    return x.reshape(*x.shape[:-1], n, x.shape[-1] // n)


def split_proj(p):
    return jnp.split(p, np.cumsum(IN_SIZES)[:-1].tolist(), axis=-1)


def axial_angles(n_tok, rot_dim):
    n_rows = n_tok // GRID_W
    rows = jnp.repeat(jnp.arange(n_rows), GRID_W).astype(F32)
    cols = jnp.tile(jnp.arange(GRID_W), n_rows).astype(F32)
    axis_dim = rot_dim // 2
    inv = ROPE_THETA ** (-jnp.arange(0, axis_dim, 2, dtype=F32) / axis_dim)
    ang = jnp.concatenate([rows[:, None] * inv, cols[:, None] * inv], axis=-1)
    return jnp.cos(ang), jnp.sin(ang)


def _rotate(x, c, s):
    x1, x2 = jnp.split(x, 2, axis=-1)
    return jnp.concatenate([x1 * c - x2 * s, x2 * c + x1 * s], axis=-1)


def apply_axial_rope(x, cos, sin):
    xf = x.astype(F32)
    xr, xc = jnp.split(xf, 2, axis=-1)
    cr, cc = jnp.split(cos[None, :, None, :], 2, axis=-1)
    sr, sc = jnp.split(sin[None, :, None, :], 2, axis=-1)
    return jnp.concatenate([_rotate(xr, cr, sr), _rotate(xc, cc, sc)], axis=-1).astype(x.dtype)


def rope_tail(x, cos, sin):
    return jnp.concatenate([x[..., :B_NOPE], apply_axial_rope(x[..., B_NOPE:], cos, sin)], axis=-1)


def sink_softmax(s, sink):
    sink = sink.astype(F32)
    m = jnp.maximum(jnp.max(s, axis=-1, keepdims=True), sink)
    e = jnp.exp(s - m)
    return e / (jnp.sum(e, axis=-1, keepdims=True) + jnp.exp(sink - m))


def blocked_attention(q, k, v, sink):
    b, sq, h, dk = q.shape
    kvh = k.shape[2]
    g = h // kvh
    nb = sq // Q_BLOCK
    scale = dk ** -0.5
    qb = jnp.moveaxis(q.reshape(b, nb, Q_BLOCK, kvh, g, dk), 1, 0)

    def one_block(qblk):
        s = jnp.einsum('bqkgd,bskd->bkgqs', qblk, k, preferred_element_type=F32) * scale
        p = jax.nn.softmax(s, axis=-1) if sink is None else sink_softmax(s, sink.reshape(1, kvh, g, 1, 1))
        return jnp.einsum('bkgqs,bskd->bqkgd', p.astype(v.dtype), v)

    o = lax.map(one_block, qb)
    return jnp.moveaxis(o, 0, 1).reshape(b, sq, h, v.shape[-1])


def window_attention(q, k, v, k_ctx, v_ctx, sink):
    b, s, h, d = q.shape
    kvh = k.shape[2]
    g = h // kvh
    nb = s // Q_BLOCK
    span = Q_BLOCK + 2 * WINDOW
    pad = ((0, 0), (WINDOW, WINDOW), (0, 0), (0, 0))
    kp, vp = jnp.pad(k, pad), jnp.pad(v, pad)
    idx = jnp.arange(nb)[:, None] * Q_BLOCK + jnp.arange(span)[None, :]
    kw, vw = kp[:, idx], vp[:, idx]
    qb = q.reshape(b, nb, Q_BLOCK, kvh, g, d)
    scale = d ** -0.5
    s_loc = jnp.einsum('bnqkgd,bnskd->bnkgqs', qb, kw, preferred_element_type=F32) * scale
    qpos = jnp.arange(nb)[:, None, None] * Q_BLOCK + jnp.arange(Q_BLOCK)[None, :, None]
    kpos = idx[:, None, :] - WINDOW
    valid = (kpos >= 0) & (kpos < s) & (jnp.abs(qpos - kpos) <= WINDOW)
    s_loc = jnp.where(valid[None, :, None, None], s_loc, NEG_INF)
    s_ctx = jnp.einsum('bnqkgd,bckd->bnkgqc', qb, k_ctx, preferred_element_type=F32) * scale
    p = sink_softmax(jnp.concatenate([s_loc, s_ctx], axis=-1), sink.reshape(1, 1, kvh, g, 1, 1)).astype(v.dtype)
    o = (jnp.einsum('bnkgqs,bnskd->bnqkgd', p[..., :span], vw)
         + jnp.einsum('bnkgqc,bckd->bnqkgd', p[..., span:], v_ctx))
    return o.reshape(b, s, h, d)


def mla_queries(cq, w_uq, q_g):
    return rms_norm(heads(jnp.dot(cq, w_uq), B_HEADS), q_g)


def mla_kv(ckv, krope, w_ukv, k_g):
    kv = heads(jnp.dot(ckv, w_ukv), B_HEADS)
    k_nope, v = kv[..., :B_NOPE], kv[..., B_NOPE:]
    kr = jnp.broadcast_to(krope[:, :, None, :], (*krope.shape[:2], B_HEADS, B_ROPE))
    return rms_norm(jnp.concatenate([k_nope, kr], axis=-1), k_g), v


def context_mixers(h, w_in, a_q_g, a_k_g, a_sink, b_cq_g, b_ckv_g, w_uq, w_ukv, b_q_g, b_k_g, c_q_g, c_k_g):
    b, s, _ = h.shape
    qa, ka, va, cq, ckv, kr, qc, kc, vc = split_proj(jnp.dot(h, w_in))
    qa = rms_norm(heads(qa, A_HEADS), a_q_g)
    ka = rms_norm(heads(ka, A_KV_HEADS), a_k_g)
    va = heads(va, A_KV_HEADS)
    out_a = blocked_attention(qa, ka, va, a_sink)
    ckv = rms_norm(ckv, b_ckv_g)
    qb = mla_queries(rms_norm(cq, b_cq_g), w_uq, b_q_g)
    kb, vb = mla_kv(ckv, kr, w_ukv, b_k_g)
    out_b = blocked_attention(qb, kb, vb, None)
    qc = rms_norm(heads(qc, C_HEADS), c_q_g)
    kc = rms_norm(heads(kc, C_KV_HEADS), c_k_g)
    vc = heads(vc, C_KV_HEADS)
    out_c = blocked_attention(qc, kc, vc, None)
    mix = jnp.concatenate([out_a.reshape(b, s, -1), out_b.reshape(b, s, -1), out_c.reshape(b, s, -1)], axis=-1)
    return mix, (ka, va, ckv, kr, kc, vc)


def latent_mixers(h, ck_a, cv_a, c_ckv, c_krope, ck_c, cv_c, w_in, a_q_g, a_k_g, a_sink, b_cq_g, b_ckv_g,
                  w_uq, w_ukv, b_q_g, b_k_g, c_q_g, c_k_g, cos_hd, sin_hd, cos_b, sin_b):
    b, s, _ = h.shape
    qa, ka, va, cq, ckv, kr, qc, kc, vc = split_proj(jnp.dot(h, w_in))
    qa = apply_axial_rope(rms_norm(heads(qa, A_HEADS), a_q_g), cos_hd, sin_hd)
    ka = apply_axial_rope(rms_norm(heads(ka, A_KV_HEADS), a_k_g), cos_hd, sin_hd)
    out_a = window_attention(qa, ka, heads(va, A_KV_HEADS), ck_a, cv_a, a_sink)
    qb = rope_tail(mla_queries(rms_norm(cq, b_cq_g), w_uq, b_q_g), cos_b, sin_b)
    kb, vb = mla_kv(rms_norm(ckv, b_ckv_g), kr, w_ukv, b_k_g)
    kb = rope_tail(kb, cos_b, sin_b)
    kb_ctx, vb_ctx = mla_kv(c_ckv, c_krope, w_ukv, b_k_g)
    out_b = blocked_attention(qb, jnp.concatenate([kb_ctx, kb], axis=1), jnp.concatenate([vb_ctx, vb], axis=1), None)
    qc = apply_axial_rope(rms_norm(heads(qc, C_HEADS), c_q_g), cos_hd, sin_hd)
    kc = apply_axial_rope(rms_norm(heads(kc, C_KV_HEADS), c_k_g), cos_hd, sin_hd)
    out_c = blocked_attention(qc, jnp.concatenate([ck_c, kc], axis=1),
                              jnp.concatenate([cv_c, heads(vc, C_KV_HEADS)], axis=1), None)
    return jnp.concatenate([out_a.reshape(b, s, -1), out_b.reshape(b, s, -1), out_c.reshape(b, s, -1)], axis=-1)


def moe(x, router_w, router_b, w_gu, b_gu, w_dn, b_dn):
    b, s, d = x.shape
    tok = x.reshape(b * s, d)
    n_tok = tok.shape[0]
    n_assign = n_tok * TOP_K
    logits = jnp.dot(tok, router_w, preferred_element_type=F32) + router_b.astype(F32)
    top_logit, top_e = lax.top_k(logits, TOP_K)
    gates = jax.nn.softmax(top_logit, axis=-1)
    e_flat = top_e.reshape(-1)
    order = jnp.argsort(e_flat)
    e_sorted = e_flat[order]
    counts = jnp.bincount(e_flat, length=N_EXPERTS)
    padded = (counts + MOE_BLOCK - 1) // MOE_BLOCK * MOE_BLOCK
    start = jnp.cumsum(counts) - counts
    pend = jnp.cumsum(padded)
    pstart = pend - padded
    dest_sorted = pstart[e_sorted] + jnp.arange(n_assign) - start[e_sorted]
    dest = jnp.zeros_like(dest_sorted).at[order].set(dest_sorted)
    n_blocks = -(-n_assign // MOE_BLOCK) + N_EXPERTS
    buf = jnp.zeros((n_blocks * MOE_BLOCK, d), x.dtype).at[dest].set(jnp.repeat(tok, TOP_K, axis=0))
    block_e = jnp.minimum(jnp.searchsorted(pend, jnp.arange(n_blocks) * MOE_BLOCK, side='right'), N_EXPERTS - 1)

    def expert_block(args):
        xb, e = args
        gu = jnp.dot(xb, w_gu[e]) + b_gu[e]
        g_, up = gu[:, :D_FF], gu[:, D_FF:]
        g_ = jnp.minimum(g_, SWIGLU_LIMIT)
        up = jnp.clip(up, -SWIGLU_LIMIT, SWIGLU_LIMIT)
        hid = (up + 1) * (g_ * jax.nn.sigmoid(SWIGLU_ALPHA * g_))
        return jnp.dot(hid, w_dn[e]) + b_dn[e]

    y_buf = lax.map(expert_block, (buf.reshape(n_blocks, MOE_BLOCK, d), block_e))
    y = y_buf.reshape(-1, d)[dest].reshape(n_tok, TOP_K, d)
    return jnp.einsum('tk,tkd->td', gates.astype(x.dtype), y).reshape(b, s, d)


def setup_inputs(seed: int = 0) -> dict:
    key = jax.random.key(seed)
    ks = jax.random.split(key, 40)

    def nrm(k, shape, scale=1.0):
        return scale * jax.random.normal(k, shape, F32)

    def gain(k, shape):
        return 1.0 + 0.05 * jax.random.normal(k, shape, F32)

    return {
        "x_prompt": nrm(ks[0], (BATCH, SEQ, D_MODEL)),
        "x_sample": nrm(ks[1], (DEC_BATCH, DEC_SEQ, D_MODEL)),
        "cache_a_k": nrm(ks[2], (DEC_BATCH, DEPTH, PAST_LEN, A_KV_HEADS, HEAD_DIM)),
        "cache_a_v": nrm(ks[3], (DEC_BATCH, DEPTH, PAST_LEN, A_KV_HEADS, HEAD_DIM)),
        "cache_b_ckv": nrm(ks[4], (DEC_BATCH, DEPTH, PAST_LEN, B_KV_LORA)),
        "cache_b_krope": nrm(ks[5], (DEC_BATCH, DEPTH, PAST_LEN, B_ROPE)),
        "cache_c_k": nrm(ks[6], (DEC_BATCH, DEPTH, PAST_LEN, C_KV_HEADS, HEAD_DIM)),
        "cache_c_v": nrm(ks[7], (DEC_BATCH, DEPTH, PAST_LEN, C_KV_HEADS, HEAD_DIM)),
        "c": nrm(ks[8], (DEC_BATCH, D_MODEL)),
        "c_ctx": nrm(ks[9], (D_MODEL,)),
        "norm1_g": gain(ks[10], (DEPTH, D_MODEL)),
        "norm2_g": gain(ks[11], (DEPTH, D_MODEL)),
        "w_mod": nrm(ks[12], (DEPTH, D_MODEL, 6 * D_MODEL), D_MODEL ** -0.5),
        "b_mod": nrm(ks[13], (DEPTH, 6 * D_MODEL), 0.02),
        "w_in": nrm(ks[14], (DEPTH, D_MODEL, IN_COLS), D_MODEL ** -0.5),
        "a_q_g": gain(ks[15], (DEPTH, HEAD_DIM)),
        "a_k_g": gain(ks[16], (DEPTH, HEAD_DIM)),
        "a_sink": nrm(ks[17], (DEPTH, A_HEADS), 0.5),
        "b_cq_g": gain(ks[18], (DEPTH, B_Q_LORA)),
        "b_ckv_g": gain(ks[19], (DEPTH, B_KV_LORA)),
        "w_uq": nrm(ks[20], (DEPTH, B_Q_LORA, B_HEADS * (B_NOPE + B_ROPE)), B_Q_LORA ** -0.5),
        "w_ukv": nrm(ks[21], (DEPTH, B_KV_LORA, B_HEADS * (B_NOPE + B_V)), B_KV_LORA ** -0.5),
        "b_q_g": gain(ks[22], (DEPTH, B_NOPE + B_ROPE)),
        "b_k_g": gain(ks[23], (DEPTH, B_NOPE + B_ROPE)),
        "c_q_g": gain(ks[24], (DEPTH, HEAD_DIM)),
        "c_k_g": gain(ks[25], (DEPTH, HEAD_DIM)),
        "w_out": nrm(ks[26], (DEPTH, MIX_W, D_MODEL), MIX_W ** -0.5),
        "router_w": nrm(ks[27], (DEPTH, D_MODEL, N_EXPERTS), D_MODEL ** -0.5),
        "router_b": nrm(ks[28], (DEPTH, N_EXPERTS), 0.01),
        "w_gu": nrm(ks[29], (DEPTH, N_EXPERTS, D_MODEL, 2 * D_FF), D_MODEL ** -0.5),
        "b_gu": nrm(ks[30], (DEPTH, N_EXPERTS, 2 * D_FF), 0.02),
        "w_dn": nrm(ks[31], (DEPTH, N_EXPERTS, D_FF, D_MODEL), D_FF ** -0.5),
        "b_dn": nrm(ks[32], (DEPTH, N_EXPERTS, D_MODEL), 0.02),
    }


def reference(x_prompt, x_sample, cache_a_k, cache_a_v, cache_b_ckv, cache_b_krope, cache_c_k, cache_c_v,
              c, c_ctx, norm1_g, norm2_g, w_mod, b_mod, w_in, a_q_g, a_k_g, a_sink, b_cq_g, b_ckv_g,
              w_uq, w_ukv, b_q_g, b_k_g, c_q_g, c_k_g, w_out, router_w, router_b, w_gu, b_gu, w_dn, b_dn):
    xc = x_prompt
    ctx_layers = []
    for l in range(DEPTH):
        sh1, sc1, g1, sh2, sc2, g2 = adaln(c_ctx[None, :], w_mod[l], b_mod[l])
        h = modulate(rms_norm(xc, norm1_g[l]), sh1, sc1)
        mix, ctx_t = context_mixers(h, w_in[l], a_q_g[l], a_k_g[l], a_sink[l], b_cq_g[l], b_ckv_g[l],
                                    w_uq[l], w_ukv[l], b_q_g[l], b_k_g[l], c_q_g[l], c_k_g[l])
        xc = xc + g1 * jnp.dot(mix, w_out[l])
        h2 = modulate(rms_norm(xc, norm2_g[l]), sh2, sc2)
        xc = xc + g2 * moe(h2, router_w[l], router_b[l], w_gu[l], b_gu[l], w_dn[l], b_dn[l])
        ctx_layers.append(ctx_t)
    state_a_k = jnp.stack([t[0] for t in ctx_layers], axis=1)
    state_a_v = jnp.stack([t[1] for t in ctx_layers], axis=1)
    state_b_ckv = jnp.stack([t[2] for t in ctx_layers], axis=1)
    state_b_krope = jnp.stack([t[3] for t in ctx_layers], axis=1)
    state_c_k = jnp.stack([t[4] for t in ctx_layers], axis=1)
    state_c_v = jnp.stack([t[5] for t in ctx_layers], axis=1)

    s_lat = x_sample.shape[1]
    cos_hd, sin_hd = axial_angles(s_lat, HEAD_DIM)
    cos_b, sin_b = axial_angles(s_lat, B_ROPE)
    xs = x_sample
    for l in range(DEPTH):
        sh1, sc1, g1, sh2, sc2, g2 = adaln(c, w_mod[l], b_mod[l])
        h = modulate(rms_norm(xs, norm1_g[l]), sh1, sc1)
        mix = latent_mixers(h, cache_a_k[:, l], cache_a_v[:, l], cache_b_ckv[:, l], cache_b_krope[:, l],
                            cache_c_k[:, l], cache_c_v[:, l], w_in[l], a_q_g[l], a_k_g[l], a_sink[l],
                            b_cq_g[l], b_ckv_g[l], w_uq[l], w_ukv[l], b_q_g[l], b_k_g[l], c_q_g[l], c_k_g[l],
                            cos_hd, sin_hd, cos_b, sin_b)
        xs = xs + g1 * jnp.dot(mix, w_out[l])
        h2 = modulate(rms_norm(xs, norm2_g[l]), sh2, sc2)
        xs = xs + g2 * moe(h2, router_w[l], router_b[l], w_gu[l], b_gu[l], w_dn[l], b_dn[l])

    return (xc, xs, state_a_k, state_a_v, state_b_ckv, state_b_krope, state_c_k, state_c_v)
```

```python
import functools

import jax
import jax.numpy as jnp
import numpy as np
from jax import lax
from jax.experimental import pallas as pl
from jax.experimental.pallas import tpu as pltpu

F32 = jnp.float32
BF16 = jnp.bfloat16

D = 1024
DEPTH = 2
N_CTX_SEQ, CTX_LEN = 16, 256
N_LAT_SEQ, LAT_LEN = 4, 2048
PAST = 512
T_CTX = N_CTX_SEQ * CTX_LEN
T_LAT = N_LAT_SEQ * LAT_LEN
T_ALL = T_CTX + T_LAT
GRID_W = 64
HD = 64
WINDOW = 128
B_NOPE, B_ROPE, B_V = 64, 32, 128
B_QK = B_NOPE + B_ROPE
B_Q_LORA, B_KV_LORA = 384, 256
N_EXPERTS, TOP_K = 32, 4
D_FF = 1024
SWIGLU_LIMIT = 7.0
SWIGLU_ALPHA = 1.702
ROPE_THETA = 10000.0
EPS = 1e-6
NEG_INF = -1e30

TM = 256
LANE = 128
BP = 128
N_CTX_TILES = T_CTX // TM
N_LAT_TILES = T_LAT // TM
N_TILES = T_ALL // TM
LAT_TILES_PER_SEQ = LAT_LEN // TM
MOE_TM = 256
MOE_TILES = T_ALL * TOP_K // MOE_TM + N_EXPERTS
VMEM_LIMIT = 56 * 1024 * 1024

C_QA, C_KA, C_VA, C_CQ, C_CKV, C_QC, C_KC, C_VC, C_KR, C_END = (
    0, 256, 384, 512, 896, 1152, 1408, 1536, 1664, 1792)


def _dot(a, b):
    return jnp.dot(a, b, preferred_element_type=F32)


def _dot_t(a, b):
    return lax.dot_general(a, b, (((1,), (1,)), ((), ())), preferred_element_type=F32)


def _rms(x, g):
    return x * lax.rsqrt(jnp.mean(x * x, axis=-1, keepdims=True) + EPS) * g


def _seg_norm(x, seg, g, n):
    sq = x * x
    hi = sq.astype(BF16)
    lo = (sq - hi.astype(F32)).astype(BF16)
    ss = _dot(hi, seg) + _dot(lo, seg)
    return x * lax.rsqrt(ss * (1.0 / n) + EPS) * g


def _rope(x, cos, sin, first, sh):
    w = x.shape[1]
    fwd = pltpu.roll(x, w - sh, 1)
    bwd = pltpu.roll(x, sh, 1)
    return x * cos + jnp.where(first > 0.5, fwd, bwd) * sin


MOD_BN = 1536


def _mod_kernel(c_ref, w_ref, b_ref, o_ref):
    c = c_ref[...]
    s = (c * jax.nn.sigmoid(c)).astype(BF16)
    o_ref[...] = _dot(s, w_ref[...].astype(BF16)) + b_ref[...]


def _modulation(cond, w_mod, b_mod):
    return pl.pallas_call(
        _mod_kernel,
        grid=(DEPTH, 6 * D // MOD_BN),
        in_specs=[
            pl.BlockSpec((8, D), lambda l, j: (0, 0)),
            pl.BlockSpec((None, D, MOD_BN), lambda l, j: (l, 0, j)),
            pl.BlockSpec((None, 1, MOD_BN), lambda l, j: (l, 0, j)),
        ],
        out_specs=pl.BlockSpec((None, 8, MOD_BN), lambda l, j: (l, 0, j)),
        out_shape=jax.ShapeDtypeStruct((DEPTH, 8, 6 * D), F32),
        compiler_params=pltpu.CompilerParams(
            dimension_semantics=("arbitrary", "arbitrary"), vmem_limit_bytes=VMEM_LIMIT),
        name="modulation",
    )(cond, w_mod, b_mod.reshape(DEPTH, 1, 6 * D))


def _proj_kernel(is_lat, *refs):
    (x_ref, mod_ref, n1g_ref, win_ref, seg64_ref, seg96_ref, aqg_ref, akg_ref, cqg_ref, ckg_ref,
     bcqg_ref, bckvg_ref, bqg_ref, bkg_ref, wuq_ref, wuk_ref, wuv_ref, plc_ref) = refs[:18]
    refs = refs[18:]
    if is_lat:
        (cos64_ref, sin64_ref, cos96_ref, sin96_ref, f64_ref, f96_ref) = refs[:6]
        refs = refs[6:]
    (qa_ref, ka_ref, va_ref, qb_ref, kb_ref, vb_ref, qc_ref, kc_ref, vc_ref) = refs[:9]
    refs = refs[9:]
    if not is_lat:
        (kas_ref, vas_ref, ckvs_ref, krs_ref, kcs_ref, vcs_ref) = refs

    x = x_ref[...]
    m = mod_ref[...]
    sh1, sc1 = m[:, 0:D], m[:, D:2 * D]
    h = (_rms(x, n1g_ref[...]) * (1.0 + sc1) + sh1).astype(BF16)

    def proj(a, b):
        return _dot(h, win_ref[:, a:b])

    def rope64(t):
        wd = t.shape[1]
        return _rope(t, cos64_ref[:, :wd], sin64_ref[:, :wd], f64_ref[:, :wd], 16) if is_lat else t

    def rope96(t):
        return _rope(t, cos96_ref[...], sin96_ref[...], f96_ref[...], 8) if is_lat else t

    seg64 = seg64_ref[...]
    seg64h = seg64_ref[0:2 * HD, 0:2 * HD]
    seg96 = seg96_ref[...]

    t = _seg_norm(proj(C_QA, C_KA), seg64, aqg_ref[...], HD)
    qa_ref[...] = (rope64(t) * HD ** -0.5).astype(BF16)
    t = _seg_norm(proj(C_KA, C_VA), seg64h, akg_ref[...], HD)
    if not is_lat:
        kas_ref[...] = t
    ka_ref[...] = rope64(t).astype(BF16)
    t = proj(C_VA, C_CQ)
    if not is_lat:
        vas_ref[...] = t
    va_ref[...] = t.astype(BF16)

    cq = _rms(proj(C_CQ, C_CKV), bcqg_ref[...]).astype(BF16)
    t = _seg_norm(_dot(cq, wuq_ref[...]), seg96, bqg_ref[...], B_QK)
    qb_ref[...] = (rope96(t) * B_QK ** -0.5).astype(BF16)
    ckv = _rms(proj(C_CKV, C_QC), bckvg_ref[...])
    if not is_lat:
        ckvs_ref[...] = ckv
    ckv16 = ckv.astype(BF16)
    vb_ref[...] = _dot(ckv16, wuv_ref[...]).astype(BF16)
    kr = proj(C_KR, C_END)
    if not is_lat:
        krs_ref[...] = kr[:, 0:B_ROPE]
    kpre = _dot(ckv16, wuk_ref[...]) + _dot(kr.astype(BF16), plc_ref[...])
    t = _seg_norm(kpre, seg96, bkg_ref[...], B_QK)
    kb_ref[...] = rope96(t).astype(BF16)

    t = _seg_norm(proj(C_QC, C_KC), seg64, cqg_ref[...], HD)
    qc_ref[...] = (rope64(t) * HD ** -0.5).astype(BF16)
    t = _seg_norm(proj(C_KC, C_VC), seg64h, ckg_ref[...], HD)
    if not is_lat:
        kcs_ref[...] = t
    kc_ref[...] = rope64(t).astype(BF16)
    t = proj(C_VC, C_KR)
    if not is_lat:
        vcs_ref[...] = t
    vc_ref[...] = t.astype(BF16)


def _full(shape):
    nd = len(shape)
    return pl.BlockSpec(shape, lambda i: (0,) * nd)


def _projection(is_lat, x_all, mod_l, lw, consts):
    n_tiles = N_LAT_TILES if is_lat else N_CTX_TILES
    t_rows = n_tiles * TM
    off = N_CTX_TILES if is_lat else 0
    if is_lat:
        mod_map = lambda i: (1 + i // LAT_TILES_PER_SEQ, 0, 0)
    else:
        mod_map = lambda i: (0, 0, 0)
    ins = [x_all, mod_l, lw["n1g"], lw["w_in"], consts["seg64"], consts["seg96"], lw["aqg"], lw["akg"],
           lw["cqg"], lw["ckg"], lw["bcqg"], lw["bckvg"], lw["bqg"], lw["bkg"], lw["w_uq"], lw["w_uk"],
           lw["w_uv"], consts["place"]]
    in_specs = [pl.BlockSpec((TM, D), lambda i: (i + off, 0)),
                pl.BlockSpec((None, 1, 6 * D), mod_map)]
    in_specs += [_full(a.shape) for a in ins[2:]]
    if is_lat:
        tabs = [consts["cos64"], consts["sin64"], consts["cos96"], consts["sin96"]]
        ins += tabs + [consts["first64"], consts["first96"]]
        in_specs += [pl.BlockSpec((TM, a.shape[1]), lambda i: (i % LAT_TILES_PER_SEQ, 0)) for a in tabs]
        in_specs += [_full(consts["first64"].shape), _full(consts["first96"].shape)]
    widths = [4 * HD, 2 * HD, 2 * HD, 4 * BP, 4 * BP, 4 * B_V, 4 * HD, 2 * HD, 2 * HD]
    out_shape = [jax.ShapeDtypeStruct((t_rows, w), BF16) for w in widths]
    if not is_lat:
        out_shape += [jax.ShapeDtypeStruct((t_rows, w), F32)
                      for w in (2 * HD, 2 * HD, B_KV_LORA, B_ROPE, 2 * HD, 2 * HD)]
    out_specs = [pl.BlockSpec((TM, s.shape[1]), lambda i: (i, 0)) for s in out_shape]
    return pl.pallas_call(
        functools.partial(_proj_kernel, is_lat),
        grid=(n_tiles,),
        in_specs=in_specs,
        out_specs=out_specs,
        out_shape=out_shape,
        compiler_params=pltpu.CompilerParams(
            dimension_semantics=("arbitrary",), vmem_limit_bytes=VMEM_LIMIT),
        name="proj_lat" if is_lat else "proj_ctx",
    )(*ins)


def _cache_kv_kernel(ckv_ref, kr_ref, seg96_ref, bkg_ref, wuk_ref, wuv_ref, plc_ref, kb_ref, vb_ref):
    ckv16 = ckv_ref[...].astype(BF16)
    vb_ref[...] = _dot(ckv16, wuv_ref[...]).astype(BF16)
    kpre = _dot(ckv16, wuk_ref[...]) + _dot(kr_ref[...].astype(BF16), plc_ref[...])
    kb_ref[...] = _seg_norm(kpre, seg96_ref[...], bkg_ref[...], B_QK).astype(BF16)


def _cache_kv(c_ckv, c_kr, lw, consts):
    rows = c_ckv.shape[0]
    ins = [c_ckv, c_kr, consts["seg96"], lw["bkg"], lw["w_uk"], lw["w_uv"], consts["place"]]
    in_specs = [pl.BlockSpec((TM, B_KV_LORA), lambda i: (i, 0)), pl.BlockSpec((TM, LANE), lambda i: (i, 0))]
    in_specs += [_full(a.shape) for a in ins[2:]]
    return pl.pallas_call(
        _cache_kv_kernel,
        grid=(rows // TM,),
        in_specs=in_specs,
        out_specs=[pl.BlockSpec((TM, 4 * BP), lambda i: (i, 0)), pl.BlockSpec((TM, 4 * B_V), lambda i: (i, 0))],
        out_shape=[jax.ShapeDtypeStruct((rows, 4 * BP), BF16), jax.ShapeDtypeStruct((rows, 4 * B_V), BF16)],
        compiler_params=pltpu.CompilerParams(
            dimension_semantics=("arbitrary",), vmem_limit_bytes=VMEM_LIMIT),
        name="cache_kv",
    )(*ins)


def _attend(q, segs, sink=None):
    scores = []
    m = None
    for k, _, mask in segs:
        s = _dot_t(q, k)
        if mask is not None:
            s = jnp.where(mask, s, NEG_INF)
        scores.append(s)
        sm = jnp.max(s, axis=-1, keepdims=True)
        m = sm if m is None else jnp.maximum(m, sm)
    if sink is not None:
        m = jnp.maximum(m, sink)
    acc = None
    den = None
    for s, (_, v, _) in zip(scores, segs):
        e = jnp.exp(s - m)
        d = jnp.sum(e, axis=-1, keepdims=True)
        o = _dot(e.astype(BF16), v)
        acc = o if acc is None else acc + o
        den = d if den is None else den + d
    if sink is not None:
        den = den + jnp.exp(sink - m)
    return acc / den


def _attn_ctx_kernel(sink_ref, qa_ref, ka_ref, va_ref, qb_ref, kb_ref, vb_ref, qc_ref, kc_ref, vc_ref, o_ref):
    for h in range(4):
        kv = h // 2
        o = _attend(qa_ref[:, h * HD:(h + 1) * HD],
                    [(ka_ref[:, kv * HD:(kv + 1) * HD], va_ref[:, kv * HD:(kv + 1) * HD], None)],
                    sink=sink_ref[h])
        o_ref[:, h * HD:(h + 1) * HD] = o.astype(BF16)
    for h in range(4):
        o = _attend(qb_ref[:, h * BP:(h + 1) * BP],
                    [(kb_ref[:, h * BP:(h + 1) * BP], vb_ref[:, h * B_V:(h + 1) * B_V], None)])
        o_ref[:, 4 * HD + h * B_V:4 * HD + (h + 1) * B_V] = o.astype(BF16)
    for h in range(4):
        kv = h // 2
        o = _attend(qc_ref[:, h * HD:(h + 1) * HD],
                    [(kc_ref[:, kv * HD:(kv + 1) * HD], vc_ref[:, kv * HD:(kv + 1) * HD], None)])
        base = 4 * HD + 4 * B_V
        o_ref[:, base + h * HD:base + (h + 1) * HD] = o.astype(BF16)


def _attention_ctx(sink, p):
    names = ["qa", "ka", "va", "qb", "kb", "vb", "qc", "kc", "vc"]
    ins = [p[n] for n in names]
    in_specs = [pl.BlockSpec(memory_space=pltpu.SMEM)]
    in_specs += [pl.BlockSpec((CTX_LEN, a.shape[1]), lambda i: (i, 0)) for a in ins]
    return pl.pallas_call(
        _attn_ctx_kernel,
        grid=(N_CTX_SEQ,),
        in_specs=in_specs,
        out_specs=pl.BlockSpec((CTX_LEN, D), lambda i: (i, 0)),
        out_shape=jax.ShapeDtypeStruct((T_CTX, D), BF16),
        compiler_params=pltpu.CompilerParams(
            dimension_semantics=("arbitrary",), vmem_limit_bytes=VMEM_LIMIT),
        name="attn_ctx",
    )(sink, *ins)


WIN_SPAN = TM + 2 * WINDOW


def _attn_lat_kernel(sink_ref, qa_ref, qb_ref, qc_ref, ka_ref, va_ref, kb_ref, vb_ref, kc_ref, vc_ref,
                     cka_ref, cva_ref, ckb_ref, cvb_ref, ckc_ref, cvc_ref, o_ref):
    qi = pl.program_id(1)
    ws = pl.multiple_of(jnp.clip(qi * TM - WINDOW, 0, LAT_LEN - WIN_SPAN), WINDOW)
    qpos = qi * TM + lax.broadcasted_iota(jnp.int32, (TM, WIN_SPAN), 0)
    kpos = ws + lax.broadcasted_iota(jnp.int32, (TM, WIN_SPAN), 1)
    band = jnp.abs(qpos - kpos) <= WINDOW
    for h in range(4):
        kv = h // 2
        cs = slice(kv * HD, (kv + 1) * HD)
        o = _attend(qa_ref[:, h * HD:(h + 1) * HD],
                    [(ka_ref[pl.ds(ws, WIN_SPAN), cs], va_ref[pl.ds(ws, WIN_SPAN), cs], band),
                     (cka_ref[:, cs], cva_ref[:, cs], None)],
                    sink=sink_ref[h])
        o_ref[:, h * HD:(h + 1) * HD] = o.astype(BF16)
    for h in range(4):
        ks = slice(h * BP, (h + 1) * BP)
        vs = slice(h * B_V, (h + 1) * B_V)
        o = _attend(qb_ref[:, ks], [(ckb_ref[:, ks], cvb_ref[:, vs], None), (kb_ref[:, ks], vb_ref[:, vs], None)])
        o_ref[:, 4 * HD + h * B_V:4 * HD + (h + 1) * B_V] = o.astype(BF16)
    for h in range(4):
        kv = h // 2
        cs = slice(kv * HD, (kv + 1) * HD)
        o = _attend(qc_ref[:, h * HD:(h + 1) * HD],
                    [(ckc_ref[:, cs], cvc_ref[:, cs], None), (kc_ref[:, cs], vc_ref[:, cs], None)])
        base = 4 * HD + 4 * B_V
        o_ref[:, base + h * HD:base + (h + 1) * HD] = o.astype(BF16)


def _attention_lat(sink, p, cache):
    q_ins = [p["qa"], p["qb"], p["qc"]]
    kv_ins = [p[n] for n in ("ka", "va", "kb", "vb", "kc", "vc")]
    c_ins = [cache[n] for n in ("ka", "va", "kb", "vb", "kc", "vc")]
    in_specs = [pl.BlockSpec(memory_space=pltpu.SMEM)]
    in_specs += [pl.BlockSpec((TM, a.shape[1]), lambda b, i: (b * LAT_TILES_PER_SEQ + i, 0)) for a in q_ins]
    in_specs += [pl.BlockSpec((LAT_LEN, a.shape[1]), lambda b, i: (b, 0)) for a in kv_ins]
    in_specs += [pl.BlockSpec((PAST, a.shape[1]), lambda b, i: (b, 0)) for a in c_ins]
    return pl.pallas_call(
        _attn_lat_kernel,
        grid=(N_LAT_SEQ, LAT_TILES_PER_SEQ),
        in_specs=in_specs,
        out_specs=pl.BlockSpec((TM, D), lambda b, i: (b * LAT_TILES_PER_SEQ + i, 0)),
        out_shape=jax.ShapeDtypeStruct((T_LAT, D), BF16),
        compiler_params=pltpu.CompilerParams(
            dimension_semantics=("arbitrary", "arbitrary"), vmem_limit_bytes=VMEM_LIMIT),
        name="attn_lat",
    )(sink, *q_ins, *kv_ins, *c_ins)


def _post_kernel(mc_ref, ml_ref, x_ref, mod_ref, n2g_ref, wout_ref, rwh_ref, rwl_ref, rb_ref,
                 xo_ref, h2_ref, route_ref):
    i = pl.program_id(0)
    mix = jnp.where(i < N_CTX_TILES, mc_ref[...], ml_ref[...])
    m = mod_ref[...]
    g1, sh2, sc2 = m[:, 2 * D:3 * D], m[:, 3 * D:4 * D], m[:, 4 * D:5 * D]
    x = x_ref[...] + g1 * _dot(mix, wout_ref[...])
    xo_ref[...] = x
    h2 = _rms(x, n2g_ref[...]) * (1.0 + sc2) + sh2
    hi = h2.astype(BF16)
    h2_ref[...] = hi
    lo = (h2 - hi.astype(F32)).astype(BF16)
    logits = _dot(hi, rwh_ref[...]) + _dot(lo, rwh_ref[...]) + _dot(hi, rwl_ref[...]) + rb_ref[...]
    lane = lax.broadcasted_iota(jnp.int32, logits.shape, 1).astype(F32)
    lane_o = lax.broadcasted_iota(jnp.int32, (TM, LANE), 1)
    work = logits
    vals, idxs = [], []
    for _ in range(TOP_K):
        v = jnp.max(work, axis=-1, keepdims=True)
        idx = jnp.min(jnp.where(work == v, lane, float(N_EXPERTS)), axis=-1, keepdims=True)
        vals.append(v)
        idxs.append(idx)
        work = jnp.where(lane == idx, -jnp.inf, work)
    es = [jnp.exp(v - vals[0]) for v in vals]
    den = es[0] + es[1] + es[2] + es[3]
    slab = jnp.zeros((TM, LANE), F32)
    for k in range(TOP_K):
        slab = jnp.where(lane_o == k, es[k] / den, slab)
        slab = jnp.where(lane_o == TOP_K + k, idxs[k], slab)
    route_ref[...] = slab


def _mod_row(i):
    return jnp.where(i < N_CTX_TILES, 0, 1 + (i - N_CTX_TILES) // LAT_TILES_PER_SEQ)


def _post_attention(mix_ctx, mix_lat, x_all, mod_l, lw):
    ins = [mix_ctx, mix_lat, x_all, mod_l, lw["n2g"], lw["w_out"], lw["rw_hi"], lw["rw_lo"], lw["rb"]]
    in_specs = [
        pl.BlockSpec((TM, D), lambda i: (jnp.minimum(i, N_CTX_TILES - 1), 0)),
        pl.BlockSpec((TM, D), lambda i: (jnp.maximum(i - N_CTX_TILES, 0), 0)),
        pl.BlockSpec((TM, D), lambda i: (i, 0)),
        pl.BlockSpec((None, 1, 6 * D), lambda i: (_mod_row(i), 0, 0)),
    ] + [_full(a.shape) for a in ins[4:]]
    return pl.pallas_call(
        _post_kernel,
        grid=(N_TILES,),
        in_specs=in_specs,
        out_specs=[pl.BlockSpec((TM, D), lambda i: (i, 0)), pl.BlockSpec((TM, D), lambda i: (i, 0)),
                   pl.BlockSpec((TM, LANE), lambda i: (i, 0))],
        out_shape=[jax.ShapeDtypeStruct((T_ALL, D), F32), jax.ShapeDtypeStruct((T_ALL, D), BF16),
                   jax.ShapeDtypeStruct((T_ALL, LANE), F32)],
        compiler_params=pltpu.CompilerParams(
            dimension_semantics=("arbitrary",), vmem_limit_bytes=VMEM_LIMIT),
        name="post_attn",
    )(*ins)


def _expert_kernel(te_ref, nu_ref, x_ref, wgu_ref, bgu_ref, wdn_ref, bdn_ref, y_ref, wgu16, wdn16):
    i = pl.program_id(0)
    prev = te_ref[jnp.maximum(i - 1, 0)]
    new_expert = jnp.logical_or(i == 0, te_ref[i] != prev)

    @pl.when(new_expert)
    def _():
        wgu16[...] = wgu_ref[...].astype(BF16)
        wdn16[...] = wdn_ref[...].astype(BF16)

    @pl.when(i < nu_ref[0])
    def _():
        gu = _dot(x_ref[...], wgu16[...]) + bgu_ref[...]
        g = jnp.minimum(gu[:, :D_FF], SWIGLU_LIMIT)
        up = jnp.clip(gu[:, D_FF:], -SWIGLU_LIMIT, SWIGLU_LIMIT)
        hid = (up + 1.0) * (g * jax.nn.sigmoid(SWIGLU_ALPHA * g))
        y_ref[...] = (_dot(hid.astype(BF16), wdn16[...]) + bdn_ref[...]).astype(BF16)


def _experts(tile_expert, n_used, x_sorted, w_gu, b_gu, w_dn, b_dn):
    def row_map(i, te, nu):
        return (jnp.minimum(i, nu[0] - 1), 0)

    def w_map(i, te, nu):
        return (te[i], 0, 0)

    grid_spec = pltpu.PrefetchScalarGridSpec(
        num_scalar_prefetch=2,
        grid=(MOE_TILES,),
        in_specs=[
            pl.BlockSpec((MOE_TM, D), row_map),
            pl.BlockSpec((None, D, 2 * D_FF), w_map),
            pl.BlockSpec((None, 1, 2 * D_FF), w_map),
            pl.BlockSpec((None, D_FF, D), w_map),
            pl.BlockSpec((None, 1, D), w_map),
        ],
        out_specs=pl.BlockSpec((MOE_TM, D), row_map),
        scratch_shapes=[pltpu.VMEM((D, 2 * D_FF), BF16), pltpu.VMEM((D_FF, D), BF16)],
    )
    return pl.pallas_call(
        _expert_kernel,
        grid_spec=grid_spec,
        out_shape=jax.ShapeDtypeStruct((MOE_TILES * MOE_TM, D), BF16),
        compiler_params=pltpu.CompilerParams(
            dimension_semantics=("arbitrary",), vmem_limit_bytes=VMEM_LIMIT),
        name="experts",
    )(tile_expert, n_used, x_sorted, w_gu, b_gu.reshape(N_EXPERTS, 1, 2 * D_FF), w_dn,
      b_dn.reshape(N_EXPERTS, 1, D))


def _combine_kernel(x_ref, y_ref, route_ref, mod_ref, o_ref):
    g2 = mod_ref[...][:, 5 * D:6 * D]
    r = route_ref[...]
    acc = r[:, 0:1] * y_ref[:, 0:D].astype(F32)
    for k in range(1, TOP_K):
        acc = acc + r[:, k:k + 1] * y_ref[:, k * D:(k + 1) * D].astype(F32)
    o_ref[...] = x_ref[...] + g2 * acc


def _combine(x_all, y_tok, route, mod_l):
    return pl.pallas_call(
        _combine_kernel,
        grid=(N_TILES,),
        in_specs=[
            pl.BlockSpec((TM, D), lambda i: (i, 0)),
            pl.BlockSpec((TM, TOP_K * D), lambda i: (i, 0)),
            pl.BlockSpec((TM, LANE), lambda i: (i, 0)),
            pl.BlockSpec((None, 1, 6 * D), lambda i: (_mod_row(i), 0, 0)),
        ],
        out_specs=pl.BlockSpec((TM, D), lambda i: (i, 0)),
        out_shape=jax.ShapeDtypeStruct((T_ALL, D), F32),
        compiler_params=pltpu.CompilerParams(
            dimension_semantics=("arbitrary",), vmem_limit_bytes=VMEM_LIMIT),
        name="combine",
    )(x_all, y_tok, route, mod_l)


def _constants():
    lane64 = np.arange(4 * HD)
    seg64 = (lane64[:, None] // HD == lane64[None, :] // HD).astype(np.float32)
    lane96 = np.arange(4 * BP)
    real = lane96 % BP < B_QK
    seg96 = ((lane96[:, None] // BP == lane96[None, :] // BP) & real[:, None] & real[None, :]).astype(np.float32)
    place = np.zeros((LANE, 4 * BP), np.float32)
    for hh in range(4):
        place[np.arange(B_ROPE), hh * BP + B_NOPE + np.arange(B_ROPE)] = 1.0

    def angles(rot_dim):
        pos = np.arange(LAT_LEN)
        rows = (pos // GRID_W).astype(np.float32)
        cols = (pos % GRID_W).astype(np.float32)
        axis_dim = rot_dim // 2
        inv = jnp.asarray(ROPE_THETA, F32) ** (-jnp.arange(0, axis_dim, 2, dtype=F32) / axis_dim)
        ang = jnp.concatenate([rows[:, None] * inv, cols[:, None] * inv], axis=-1)
        return jnp.cos(ang), jnp.sin(ang)

    def head_tables(rot_dim):
        cos, sin = angles(rot_dim)
        q = rot_dim // 4
        cr, cc, sr, sc = cos[:, :q], cos[:, q:], sin[:, :q], sin[:, q:]
        return (jnp.concatenate([cr, cr, cc, cc], axis=-1), jnp.concatenate([-sr, sr, -sc, sc], axis=-1))

    c64, s64 = head_tables(HD)
    cos64 = jnp.tile(c64, (1, 4))
    sin64 = jnp.tile(s64, (1, 4))
    c32, s32 = head_tables(B_ROPE)
    ones = jnp.ones((LAT_LEN, B_NOPE), F32)
    zeros = jnp.zeros((LAT_LEN, B_NOPE), F32)
    padz = jnp.zeros((LAT_LEN, BP - B_QK), F32)
    cos96 = jnp.tile(jnp.concatenate([ones, c32, padz], axis=-1), (1, 4))
    sin96 = jnp.tile(jnp.concatenate([zeros, s32, padz], axis=-1), (1, 4))
    first64 = ((lane64 % 32) < 16).astype(np.float32)[None, :]
    first96 = (((lane96 % BP) % 16) < 8).astype(np.float32)[None, :]
    return dict(seg64=jnp.asarray(seg64, BF16), seg96=jnp.asarray(seg96, BF16), place=jnp.asarray(place, BF16),
                cos64=cos64, sin64=sin64, cos96=cos96, sin96=sin96,
                first64=jnp.asarray(first64), first96=jnp.asarray(first96))


def _pad_heads(w, per_head, width):
    r = w.shape[0]
    w = w.reshape(r, 4, per_head)
    return jnp.pad(w, ((0, 0), (0, 0), (0, width - per_head))).reshape(r, 4 * width)


def _layer_weights(l, norm1_g, norm2_g, w_in, a_q_g, a_k_g, b_cq_g, b_ckv_g, w_uq, w_ukv, b_q_g, b_k_g,
                   c_q_g, c_k_g, w_out, router_w, router_b):
    wi = w_in[l]
    o = np.cumsum((0, 256, 128, 128, 384, 256, 32, 256, 128, 128))
    seg = lambda k: wi[:, o[k]:o[k + 1]]
    w_in_r = jnp.concatenate([seg(0), seg(1), seg(2), seg(3), seg(4), seg(6), seg(7), seg(8), seg(5),
                              jnp.zeros((D, LANE - B_ROPE), F32)], axis=-1).astype(BF16)
    ukv = w_ukv[l].reshape(B_KV_LORA, 4, B_NOPE + B_V)
    w_uk = jnp.pad(ukv[:, :, :B_NOPE], ((0, 0), (0, 0), (0, BP - B_NOPE))).reshape(B_KV_LORA, 4 * BP)
    w_uv = ukv[:, :, B_NOPE:].reshape(B_KV_LORA, 4 * B_V)
    rw = router_w[l]
    rw_hi = rw.astype(BF16)
    rw_lo = (rw - rw_hi.astype(F32)).astype(BF16)
    row = lambda v: v[None, :]
    return dict(
        n1g=row(norm1_g[l]), n2g=row(norm2_g[l]), w_in=w_in_r,
        aqg=row(jnp.tile(a_q_g[l], 4)), akg=row(jnp.tile(a_k_g[l], 2)),
        cqg=row(jnp.tile(c_q_g[l], 4)), ckg=row(jnp.tile(c_k_g[l], 2)),
        bcqg=row(b_cq_g[l]), bckvg=row(b_ckv_g[l]),
        bqg=_pad_heads(row(jnp.tile(b_q_g[l], 4)), B_QK, BP), bkg=_pad_heads(row(jnp.tile(b_k_g[l], 4)), B_QK, BP),
        w_uq=_pad_heads(w_uq[l], B_QK, BP).astype(BF16), w_uk=w_uk.astype(BF16), w_uv=w_uv.astype(BF16),
        w_out=w_out[l].astype(BF16), rw_hi=rw_hi, rw_lo=rw_lo, rb=row(router_b[l]))


def _moe(x_all, h2, route, mod_l, w_gu, b_gu, w_dn, b_dn):
    e = route[:, TOP_K:2 * TOP_K].astype(jnp.int32)
    sel = jnp.sum(jax.nn.one_hot(e, N_EXPERTS, dtype=jnp.int32), axis=1)
    counts = jnp.sum(sel, axis=0)
    rank = jnp.take_along_axis(jnp.cumsum(sel, axis=0) - sel, e, axis=1)
    padded = (counts + MOE_TM - 1) // MOE_TM * MOE_TM
    pend = jnp.cumsum(padded)
    pstart = pend - padded
    dest = (pstart[e] + rank).reshape(-1)
    n_used = (pend[-1] // MOE_TM).astype(jnp.int32)
    tiles = jnp.minimum(jnp.arange(MOE_TILES, dtype=jnp.int32), n_used - 1)
    tile_expert = jnp.minimum(jnp.searchsorted(pend, tiles * MOE_TM, side="right"), N_EXPERTS - 1).astype(jnp.int32)
    tok = jnp.repeat(jnp.arange(T_ALL, dtype=jnp.int32), TOP_K)
    tok_of_row = jnp.zeros((MOE_TILES * MOE_TM,), jnp.int32).at[dest].set(tok)
    x_sorted = jnp.take(h2, tok_of_row, axis=0)
    y = _experts(tile_expert, n_used.reshape(1), x_sorted, w_gu, b_gu, w_dn, b_dn)
    y_tok = jnp.take(y, dest, axis=0).reshape(T_ALL, TOP_K * D)
    return _combine(x_all, y_tok, route, mod_l)


def kernel(x_prompt, x_sample, cache_a_k, cache_a_v, cache_b_ckv, cache_b_krope, cache_c_k, cache_c_v, c, c_ctx,
           norm1_g, norm2_g, w_mod, b_mod, w_in, a_q_g, a_k_g, a_sink, b_cq_g, b_ckv_g, w_uq, w_ukv, b_q_g, b_k_g,
           c_q_g, c_k_g, w_out, router_w, router_b, w_gu, b_gu, w_dn, b_dn):
    consts = _constants()
    cond = jnp.concatenate([c_ctx[None, :], c, jnp.zeros((3, D), F32)], axis=0)
    mod = _modulation(cond, w_mod, b_mod).reshape(DEPTH, 8, 1, 6 * D)
    x_all = jnp.concatenate([x_prompt.reshape(T_CTX, D), x_sample.reshape(T_LAT, D)], axis=0)

    states = []
    for l in range(DEPTH):
        lw = _layer_weights(l, norm1_g, norm2_g, w_in, a_q_g, a_k_g, b_cq_g, b_ckv_g, w_uq, w_ukv, b_q_g, b_k_g,
                            c_q_g, c_k_g, w_out, router_w, router_b)
        mod_l = mod[l]
        names = ["qa", "ka", "va", "qb", "kb", "vb", "qc", "kc", "vc"]
        outs = _projection(False, x_all, mod_l, lw, consts)
        p_ctx = dict(zip(names, outs[:9]))
        states.append(outs[9:])
        p_lat = dict(zip(names, _projection(True, x_all, mod_l, lw, consts)))

        flat = lambda a, w: a[:, l].reshape(N_LAT_SEQ * PAST, w)
        ckb, cvb = _cache_kv(flat(cache_b_ckv, B_KV_LORA),
                             jnp.pad(flat(cache_b_krope, B_ROPE), ((0, 0), (0, LANE - B_ROPE))), lw, consts)
        cache = dict(ka=flat(cache_a_k, 2 * HD).astype(BF16), va=flat(cache_a_v, 2 * HD).astype(BF16),
                     kb=ckb, vb=cvb,
                     kc=flat(cache_c_k, 2 * HD).astype(BF16), vc=flat(cache_c_v, 2 * HD).astype(BF16))
        mix_ctx = _attention_ctx(a_sink[l], p_ctx)
        mix_lat = _attention_lat(a_sink[l], p_lat, cache)
        x_all, h2, route = _post_attention(mix_ctx, mix_lat, x_all, mod_l, lw)
        x_all = _moe(x_all, h2, route, mod_l, w_gu[l], b_gu[l], w_dn[l], b_dn[l])

    def state(k, shape):
        return jnp.stack([states[l][k].reshape(shape) for l in range(DEPTH)], axis=1)

    kv_shape = (N_CTX_SEQ, CTX_LEN, 2, HD)
    return (x_all[:T_CTX].reshape(N_CTX_SEQ, CTX_LEN, D), x_all[T_CTX:].reshape(N_LAT_SEQ, LAT_LEN, D),
            state(0, kv_shape), state(1, kv_shape),
            state(2, (N_CTX_SEQ, CTX_LEN, B_KV_LORA)), state(3, (N_CTX_SEQ, CTX_LEN, B_ROPE)),
            state(4, kv_shape), state(5, kv_shape))
```

```python
import functools

import jax
import jax.numpy as jnp
import numpy as np
from jax import lax
from jax.experimental import pallas as pl
from jax.experimental.pallas import tpu as pltpu
from jax.experimental.pallas import tpu_sc as plsc

F32 = jnp.float32
BF16 = jnp.bfloat16

D = 1024
DEPTH = 2
N_CTX_SEQ, CTX_LEN = 16, 256
N_LAT_SEQ, LAT_LEN = 4, 2048
PAST = 512
T_CTX = N_CTX_SEQ * CTX_LEN
T_LAT = N_LAT_SEQ * LAT_LEN
T_ALL = T_CTX + T_LAT
GRID_W = 64
HD = 64
WINDOW = 128
B_NOPE, B_ROPE, B_V = 64, 32, 128
B_QK = B_NOPE + B_ROPE
B_Q_LORA, B_KV_LORA = 384, 256
N_EXPERTS, TOP_K = 32, 4
D_FF = 1024
SWIGLU_LIMIT = 7.0
SWIGLU_ALPHA = 1.702
ROPE_THETA = 10000.0
EPS = 1e-6
NEG_INF = -1e30

TM = 256
LANE = 128
BP = 128
N_CTX_TILES = T_CTX // TM
N_LAT_TILES = T_LAT // TM
N_TILES = T_ALL // TM
LAT_TILES_PER_SEQ = LAT_LEN // TM
MOE_TM = 256
MOE_TILES = T_ALL * TOP_K // MOE_TM + N_EXPERTS
MOE_ROWS = MOE_TILES * MOE_TM
DW = D // 2
VMEM_LIMIT = 56 * 1024 * 1024

C_QA, C_KA, C_VA, C_CQ, C_CKV, C_QC, C_KC, C_VC, C_KR, C_END = (
    0, 256, 384, 512, 896, 1152, 1408, 1536, 1664, 1792)


def _dot(a, b):
    return jnp.dot(a, b, preferred_element_type=F32)


def _dot_t(a, b):
    return lax.dot_general(a, b, (((1,), (1,)), ((), ())), preferred_element_type=F32)


def _rms(x, g):
    return x * lax.rsqrt(jnp.mean(x * x, axis=-1, keepdims=True) + EPS) * g


def _seg_norm(x, seg, g, n):
    sq = x * x
    hi = sq.astype(BF16)
    lo = (sq - hi.astype(F32)).astype(BF16)
    ss = _dot(hi, seg) + _dot(lo, seg)
    return x * lax.rsqrt(ss * (1.0 / n) + EPS) * g


def _rope(x, cos, sin, first, sh):
    w = x.shape[1]
    fwd = pltpu.roll(x, w - sh, 1)
    bwd = pltpu.roll(x, sh, 1)
    return x * cos + jnp.where(first > 0.5, fwd, bwd) * sin


MOD_BN = 1536


def _mod_kernel(c_ref, w_ref, b_ref, o_ref):
    c = c_ref[...]
    s = (c * jax.nn.sigmoid(c)).astype(BF16)
    o_ref[...] = _dot(s, w_ref[...].astype(BF16)) + b_ref[...]


def _modulation(cond, w_mod, b_mod):
    return pl.pallas_call(
        _mod_kernel,
        grid=(DEPTH, 6 * D // MOD_BN),
        in_specs=[
            pl.BlockSpec((8, D), lambda l, j: (0, 0)),
            pl.BlockSpec((None, D, MOD_BN), lambda l, j: (l, 0, j)),
            pl.BlockSpec((None, 1, MOD_BN), lambda l, j: (l, 0, j)),
        ],
        out_specs=pl.BlockSpec((None, 8, MOD_BN), lambda l, j: (l, 0, j)),
        out_shape=jax.ShapeDtypeStruct((DEPTH, 8, 6 * D), F32),
        compiler_params=pltpu.CompilerParams(
            dimension_semantics=("arbitrary", "arbitrary"), vmem_limit_bytes=VMEM_LIMIT),
        name="modulation",
    )(cond, w_mod, b_mod.reshape(DEPTH, 1, 6 * D))


def _proj_kernel(is_lat, *refs):
    (x_ref, mod_ref, n1g_ref, win_ref, seg64_ref, seg96_ref, aqg_ref, akg_ref, cqg_ref, ckg_ref,
     bcqg_ref, bckvg_ref, bqg_ref, bkg_ref, wuq_ref, wuk_ref, wuv_ref, plc_ref) = refs[:18]
    refs = refs[18:]
    if is_lat:
        (cos64_ref, sin64_ref, cos96_ref, sin96_ref, f64_ref, f96_ref) = refs[:6]
        refs = refs[6:]
    (qa_ref, ka_ref, va_ref, qb_ref, kb_ref, vb_ref, qc_ref, kc_ref, vc_ref) = refs[:9]
    refs = refs[9:]
    if not is_lat:
        (kas_ref, vas_ref, ckvs_ref, krs_ref, kcs_ref, vcs_ref) = refs

    x = x_ref[...]
    m = mod_ref[...]
    sh1, sc1 = m[:, 0:D], m[:, D:2 * D]
    h = (_rms(x, n1g_ref[...]) * (1.0 + sc1) + sh1).astype(BF16)

    def proj(a, b):
        return _dot(h, win_ref[:, a:b])

    def rope64(t):
        wd = t.shape[1]
        return _rope(t, cos64_ref[:, :wd], sin64_ref[:, :wd], f64_ref[:, :wd], 16) if is_lat else t

    def rope96(t):
        return _rope(t, cos96_ref[...], sin96_ref[...], f96_ref[...], 8) if is_lat else t

    seg64 = seg64_ref[...]
    seg64h = seg64_ref[0:2 * HD, 0:2 * HD]
    seg96 = seg96_ref[...]

    t = _seg_norm(proj(C_QA, C_KA), seg64, aqg_ref[...], HD)
    qa_ref[...] = (rope64(t) * HD ** -0.5).astype(BF16)
    t = _seg_norm(proj(C_KA, C_VA), seg64h, akg_ref[...], HD)
    if not is_lat:
        kas_ref[...] = t
    ka_ref[...] = rope64(t).astype(BF16)
    t = proj(C_VA, C_CQ)
    if not is_lat:
        vas_ref[...] = t
    va_ref[...] = t.astype(BF16)

    cq = _rms(proj(C_CQ, C_CKV), bcqg_ref[...]).astype(BF16)
    t = _seg_norm(_dot(cq, wuq_ref[...]), seg96, bqg_ref[...], B_QK)
    qb_ref[...] = (rope96(t) * B_QK ** -0.5).astype(BF16)
    ckv = _rms(proj(C_CKV, C_QC), bckvg_ref[...])
    if not is_lat:
        ckvs_ref[...] = ckv
    ckv16 = ckv.astype(BF16)
    vb_ref[...] = _dot(ckv16, wuv_ref[...]).astype(BF16)
    kr = proj(C_KR, C_END)
    if not is_lat:
        krs_ref[...] = kr[:, 0:B_ROPE]
    kpre = _dot(ckv16, wuk_ref[...]) + _dot(kr.astype(BF16), plc_ref[...])
    t = _seg_norm(kpre, seg96, bkg_ref[...], B_QK)
    kb_ref[...] = rope96(t).astype(BF16)

    t = _seg_norm(proj(C_QC, C_KC), seg64, cqg_ref[...], HD)
    qc_ref[...] = (rope64(t) * HD ** -0.5).astype(BF16)
    t = _seg_norm(proj(C_KC, C_VC), seg64h, ckg_ref[...], HD)
    if not is_lat:
        kcs_ref[...] = t
    kc_ref[...] = rope64(t).astype(BF16)
    t = proj(C_VC, C_KR)
    if not is_lat:
        vcs_ref[...] = t
    vc_ref[...] = t.astype(BF16)


def _full(shape):
    nd = len(shape)
    return pl.BlockSpec(shape, lambda i: (0,) * nd)


def _projection(is_lat, x_all, mod_l, lw, consts):
    n_tiles = N_LAT_TILES if is_lat else N_CTX_TILES
    t_rows = n_tiles * TM
    off = N_CTX_TILES if is_lat else 0
    if is_lat:
        mod_map = lambda i: (1 + i // LAT_TILES_PER_SEQ, 0, 0)
    else:
        mod_map = lambda i: (0, 0, 0)
    ins = [x_all, mod_l, lw["n1g"], lw["w_in"], consts["seg64"], consts["seg96"], lw["aqg"], lw["akg"],
           lw["cqg"], lw["ckg"], lw["bcqg"], lw["bckvg"], lw["bqg"], lw["bkg"], lw["w_uq"], lw["w_uk"],
           lw["w_uv"], consts["place"]]
    in_specs = [pl.BlockSpec((TM, D), lambda i: (i + off, 0)),
                pl.BlockSpec((None, 1, 6 * D), mod_map)]
    in_specs += [_full(a.shape) for a in ins[2:]]
    if is_lat:
        tabs = [consts["cos64"], consts["sin64"], consts["cos96"], consts["sin96"]]
        ins += tabs + [consts["first64"], consts["first96"]]
        in_specs += [pl.BlockSpec((TM, a.shape[1]), lambda i: (i % LAT_TILES_PER_SEQ, 0)) for a in tabs]
        in_specs += [_full(consts["first64"].shape), _full(consts["first96"].shape)]
    widths = [4 * HD, 2 * HD, 2 * HD, 4 * BP, 4 * BP, 4 * B_V, 4 * HD, 2 * HD, 2 * HD]
    out_shape = [jax.ShapeDtypeStruct((t_rows, w), BF16) for w in widths]
    if not is_lat:
        out_shape += [jax.ShapeDtypeStruct((t_rows, w), F32)
                      for w in (2 * HD, 2 * HD, B_KV_LORA, B_ROPE, 2 * HD, 2 * HD)]
    out_specs = [pl.BlockSpec((TM, s.shape[1]), lambda i: (i, 0)) for s in out_shape]
    return pl.pallas_call(
        functools.partial(_proj_kernel, is_lat),
        grid=(n_tiles,),
        in_specs=in_specs,
        out_specs=out_specs,
        out_shape=out_shape,
        compiler_params=pltpu.CompilerParams(
            dimension_semantics=("arbitrary",), vmem_limit_bytes=VMEM_LIMIT),
        name="proj_lat" if is_lat else "proj_ctx",
    )(*ins)


def _cache_kv_kernel(ckv_ref, kr_ref, seg96_ref, bkg_ref, wuk_ref, wuv_ref, plc_ref, kb_ref, vb_ref):
    ckv16 = ckv_ref[...].astype(BF16)
    vb_ref[...] = _dot(ckv16, wuv_ref[...]).astype(BF16)
    kpre = _dot(ckv16, wuk_ref[...]) + _dot(kr_ref[...].astype(BF16), plc_ref[...])
    kb_ref[...] = _seg_norm(kpre, seg96_ref[...], bkg_ref[...], B_QK).astype(BF16)


def _cache_kv(c_ckv, c_kr, lw, consts):
    rows = c_ckv.shape[0]
    ins = [c_ckv, c_kr, consts["seg96"], lw["bkg"], lw["w_uk"], lw["w_uv"], consts["place"]]
    in_specs = [pl.BlockSpec((TM, B_KV_LORA), lambda i: (i, 0)), pl.BlockSpec((TM, LANE), lambda i: (i, 0))]
    in_specs += [_full(a.shape) for a in ins[2:]]
    return pl.pallas_call(
        _cache_kv_kernel,
        grid=(rows // TM,),
        in_specs=in_specs,
        out_specs=[pl.BlockSpec((TM, 4 * BP), lambda i: (i, 0)), pl.BlockSpec((TM, 4 * B_V), lambda i: (i, 0))],
        out_shape=[jax.ShapeDtypeStruct((rows, 4 * BP), BF16), jax.ShapeDtypeStruct((rows, 4 * B_V), BF16)],
        compiler_params=pltpu.CompilerParams(
            dimension_semantics=("arbitrary",), vmem_limit_bytes=VMEM_LIMIT),
        name="cache_kv",
    )(*ins)


def _attend(q, segs, sink=None):
    scores = []
    m = None
    for k, _, mask in segs:
        s = _dot_t(q, k)
        if mask is not None:
            s = jnp.where(mask, s, NEG_INF)
        scores.append(s)
        sm = jnp.max(s, axis=-1, keepdims=True)
        m = sm if m is None else jnp.maximum(m, sm)
    if sink is not None:
        m = jnp.maximum(m, sink)
    acc = None
    den = None
    for s, (_, v, _) in zip(scores, segs):
        e = jnp.exp(s - m)
        d = jnp.sum(e, axis=-1, keepdims=True)
        o = _dot(e.astype(BF16), v)
        acc = o if acc is None else acc + o
        den = d if den is None else den + d
    if sink is not None:
        den = den + jnp.exp(sink - m)
    return acc / den


def _attn_ctx_kernel(sink_ref, qa_ref, ka_ref, va_ref, qb_ref, kb_ref, vb_ref, qc_ref, kc_ref, vc_ref, o_ref):
    for h in range(4):
        kv = h // 2
        o = _attend(qa_ref[:, h * HD:(h + 1) * HD],
                    [(ka_ref[:, kv * HD:(kv + 1) * HD], va_ref[:, kv * HD:(kv + 1) * HD], None)],
                    sink=sink_ref[h])
        o_ref[:, h * HD:(h + 1) * HD] = o.astype(BF16)
    for h in range(4):
        o = _attend(qb_ref[:, h * BP:(h + 1) * BP],
                    [(kb_ref[:, h * BP:(h + 1) * BP], vb_ref[:, h * B_V:(h + 1) * B_V], None)])
        o_ref[:, 4 * HD + h * B_V:4 * HD + (h + 1) * B_V] = o.astype(BF16)
    for h in range(4):
        kv = h // 2
        o = _attend(qc_ref[:, h * HD:(h + 1) * HD],
                    [(kc_ref[:, kv * HD:(kv + 1) * HD], vc_ref[:, kv * HD:(kv + 1) * HD], None)])
        base = 4 * HD + 4 * B_V
        o_ref[:, base + h * HD:base + (h + 1) * HD] = o.astype(BF16)


def _attention_ctx(sink, p):
    names = ["qa", "ka", "va", "qb", "kb", "vb", "qc", "kc", "vc"]
    ins = [p[n] for n in names]
    in_specs = [pl.BlockSpec(memory_space=pltpu.SMEM)]
    in_specs += [pl.BlockSpec((CTX_LEN, a.shape[1]), lambda i: (i, 0)) for a in ins]
    return pl.pallas_call(
        _attn_ctx_kernel,
        grid=(N_CTX_SEQ,),
        in_specs=in_specs,
        out_specs=pl.BlockSpec((CTX_LEN, D), lambda i: (i, 0)),
        out_shape=jax.ShapeDtypeStruct((T_CTX, D), BF16),
        compiler_params=pltpu.CompilerParams(
            dimension_semantics=("arbitrary",), vmem_limit_bytes=VMEM_LIMIT),
        name="attn_ctx",
    )(sink, *ins)


WIN_SPAN = TM + 2 * WINDOW


def _attn_lat_kernel(sink_ref, qa_ref, qb_ref, qc_ref, ka_ref, va_ref, kb_ref, vb_ref, kc_ref, vc_ref,
                     cka_ref, cva_ref, ckb_ref, cvb_ref, ckc_ref, cvc_ref, o_ref):
    qi = pl.program_id(1)
    ws = pl.multiple_of(jnp.clip(qi * TM - WINDOW, 0, LAT_LEN - WIN_SPAN), WINDOW)
    qpos = qi * TM + lax.broadcasted_iota(jnp.int32, (TM, WIN_SPAN), 0)
    kpos = ws + lax.broadcasted_iota(jnp.int32, (TM, WIN_SPAN), 1)
    band = jnp.abs(qpos - kpos) <= WINDOW
    for h in range(4):
        kv = h // 2
        cs = slice(kv * HD, (kv + 1) * HD)
        o = _attend(qa_ref[:, h * HD:(h + 1) * HD],
                    [(ka_ref[pl.ds(ws, WIN_SPAN), cs], va_ref[pl.ds(ws, WIN_SPAN), cs], band),
                     (cka_ref[:, cs], cva_ref[:, cs], None)],
                    sink=sink_ref[h])
        o_ref[:, h * HD:(h + 1) * HD] = o.astype(BF16)
    for h in range(4):
        ks = slice(h * BP, (h + 1) * BP)
        vs = slice(h * B_V, (h + 1) * B_V)
        o = _attend(qb_ref[:, ks], [(ckb_ref[:, ks], cvb_ref[:, vs], None), (kb_ref[:, ks], vb_ref[:, vs], None)])
        o_ref[:, 4 * HD + h * B_V:4 * HD + (h + 1) * B_V] = o.astype(BF16)
    for h in range(4):
        kv = h // 2
        cs = slice(kv * HD, (kv + 1) * HD)
        o = _attend(qc_ref[:, h * HD:(h + 1) * HD],
                    [(ckc_ref[:, cs], cvc_ref[:, cs], None), (kc_ref[:, cs], vc_ref[:, cs], None)])
        base = 4 * HD + 4 * B_V
        o_ref[:, base + h * HD:base + (h + 1) * HD] = o.astype(BF16)


def _attention_lat(sink, p, cache):
    q_ins = [p["qa"], p["qb"], p["qc"]]
    kv_ins = [p[n] for n in ("ka", "va", "kb", "vb", "kc", "vc")]
    c_ins = [cache[n] for n in ("ka", "va", "kb", "vb", "kc", "vc")]
    in_specs = [pl.BlockSpec(memory_space=pltpu.SMEM)]
    in_specs += [pl.BlockSpec((TM, a.shape[1]), lambda b, i: (b * LAT_TILES_PER_SEQ + i, 0)) for a in q_ins]
    in_specs += [pl.BlockSpec((LAT_LEN, a.shape[1]), lambda b, i: (b, 0)) for a in kv_ins]
    in_specs += [pl.BlockSpec((PAST, a.shape[1]), lambda b, i: (b, 0)) for a in c_ins]
    return pl.pallas_call(
        _attn_lat_kernel,
        grid=(N_LAT_SEQ, LAT_TILES_PER_SEQ),
        in_specs=in_specs,
        out_specs=pl.BlockSpec((TM, D), lambda b, i: (b * LAT_TILES_PER_SEQ + i, 0)),
        out_shape=jax.ShapeDtypeStruct((T_LAT, D), BF16),
        compiler_params=pltpu.CompilerParams(
            dimension_semantics=("arbitrary", "arbitrary"), vmem_limit_bytes=VMEM_LIMIT),
        name="attn_lat",
    )(sink, *q_ins, *kv_ins, *c_ins)


def _pack_rows(x):
    half = x.shape[1] // 2
    r = x.astype(BF16).astype(F32)
    hi = lax.bitcast_convert_type(r[:, :half], jnp.int32)
    lo = lax.bitcast_convert_type(r[:, half:], jnp.int32)
    return jnp.bitwise_or(hi, lax.shift_right_logical(lo, 16))


def _unpack_rows(p):
    a = lax.bitcast_convert_type(jnp.bitwise_and(p, -65536), F32)
    b = lax.bitcast_convert_type(lax.shift_left(p, 16), F32)
    return jnp.concatenate([a, b], axis=1).astype(BF16)


def _post_kernel(mc_ref, ml_ref, x_ref, mod_ref, n2g_ref, wout_ref, rwh_ref, rwl_ref, rb_ref, tri_ref,
                 xo_ref, h2_ref, route_ref, cnt_ref, run_ref):
    i = pl.program_id(0)

    @pl.when(i == 0)
    def _():
        run_ref[...] = jnp.zeros_like(run_ref)
        cnt_ref[...] = jnp.zeros_like(cnt_ref)

    mix = jnp.where(i < N_CTX_TILES, mc_ref[...], ml_ref[...])
    m = mod_ref[...]
    g1, sh2, sc2 = m[:, 2 * D:3 * D], m[:, 3 * D:4 * D], m[:, 4 * D:5 * D]
    x = x_ref[...] + g1 * _dot(mix, wout_ref[...])
    xo_ref[...] = x
    h2 = _rms(x, n2g_ref[...]) * (1.0 + sc2) + sh2
    hi = h2.astype(BF16)
    h2_ref[...] = _pack_rows(h2)
    lo = (h2 - hi.astype(F32)).astype(BF16)
    logits = _dot(hi, rwh_ref[...]) + _dot(lo, rwh_ref[...]) + _dot(hi, rwl_ref[...]) + rb_ref[...]
    lane = lax.broadcasted_iota(jnp.int32, logits.shape, 1).astype(F32)
    lane_o = lax.broadcasted_iota(jnp.int32, (TM, LANE), 1)
    work = logits
    vals, idxs, hots = [], [], []
    for _ in range(TOP_K):
        v = jnp.max(work, axis=-1, keepdims=True)
        idx = jnp.min(jnp.where(work == v, lane, float(N_EXPERTS)), axis=-1, keepdims=True)
        hot = lane == idx
        vals.append(v)
        idxs.append(idx)
        hots.append(hot)
        work = jnp.where(hot, -jnp.inf, work)
    es = [jnp.exp(v - vals[0]) for v in vals]
    den = es[0] + es[1] + es[2] + es[3]
    sel = jnp.where(hots[0] | hots[1] | hots[2] | hots[3], 1.0, 0.0)
    run = run_ref[0:1, 0:N_EXPERTS]
    before = _dot(tri_ref[...], sel.astype(BF16)) + run
    run_new = run + jnp.sum(sel, axis=0, keepdims=True)
    run_ref[0:1, 0:N_EXPERTS] = run_new
    cnt_ref[0:1, 0:N_EXPERTS] = run_new
    slab = jnp.zeros((TM, LANE), F32)
    for k in range(TOP_K):
        rank = jnp.sum(jnp.where(hots[k], before, 0.0), axis=-1, keepdims=True)
        slab = jnp.where(lane_o == k, es[k] / den, slab)
        slab = jnp.where(lane_o == TOP_K + k, idxs[k], slab)
        slab = jnp.where(lane_o == 2 * TOP_K + k, rank, slab)
    route_ref[...] = slab


def _mod_row(i):
    return jnp.where(i < N_CTX_TILES, 0, 1 + (i - N_CTX_TILES) // LAT_TILES_PER_SEQ)


def _post_attention(mix_ctx, mix_lat, x_all, mod_l, lw, consts):
    ins = [mix_ctx, mix_lat, x_all, mod_l, lw["n2g"], lw["w_out"], lw["rw_hi"], lw["rw_lo"], lw["rb"],
           consts["tri"]]
    in_specs = [
        pl.BlockSpec((TM, D), lambda i: (jnp.minimum(i, N_CTX_TILES - 1), 0)),
        pl.BlockSpec((TM, D), lambda i: (jnp.maximum(i - N_CTX_TILES, 0), 0)),
        pl.BlockSpec((TM, D), lambda i: (i, 0)),
        pl.BlockSpec((None, 1, 6 * D), lambda i: (_mod_row(i), 0, 0)),
    ] + [_full(a.shape) for a in ins[4:]]
    return pl.pallas_call(
        _post_kernel,
        grid=(N_TILES,),
        in_specs=in_specs,
        out_specs=[pl.BlockSpec((TM, D), lambda i: (i, 0)), pl.BlockSpec((TM, DW), lambda i: (i, 0)),
                   pl.BlockSpec((TM, LANE), lambda i: (i, 0)), pl.BlockSpec((8, LANE), lambda i: (0, 0))],
        out_shape=[jax.ShapeDtypeStruct((T_ALL, D), F32), jax.ShapeDtypeStruct((T_ALL, DW), jnp.int32),
                   jax.ShapeDtypeStruct((T_ALL, LANE), F32), jax.ShapeDtypeStruct((8, LANE), F32)],
        scratch_shapes=[pltpu.VMEM((8, LANE), F32)],
        compiler_params=pltpu.CompilerParams(
            dimension_semantics=("arbitrary",), vmem_limit_bytes=VMEM_LIMIT),
        name="post_attn",
    )(*ins)


def _expert_kernel(te_ref, nu_ref, x_ref, wgu_ref, bgu_ref, wdn_ref, bdn_ref, y_ref, wgu16, wdn16):
    i = pl.program_id(0)
    prev = te_ref[jnp.maximum(i - 1, 0)]
    new_expert = jnp.logical_or(i == 0, te_ref[i] != prev)

    @pl.when(new_expert)
    def _():
        wgu16[...] = wgu_ref[...].astype(BF16)
        wdn16[...] = wdn_ref[...].astype(BF16)

    @pl.when(i < nu_ref[0])
    def _():
        gu = _dot(_unpack_rows(x_ref[...]), wgu16[...]) + bgu_ref[...]
        g = jnp.minimum(gu[:, :D_FF], SWIGLU_LIMIT)
        up = jnp.clip(gu[:, D_FF:], -SWIGLU_LIMIT, SWIGLU_LIMIT)
        hid = (up + 1.0) * (g * jax.nn.sigmoid(SWIGLU_ALPHA * g))
        y_ref[...] = _pack_rows(_dot(hid.astype(BF16), wdn16[...]) + bdn_ref[...])


def _experts(tile_expert, n_used, x_sorted, w_gu, b_gu, w_dn, b_dn):
    def row_map(i, te, nu):
        return (jnp.minimum(i, nu[0] - 1), 0)

    def w_map(i, te, nu):
        return (te[i], 0, 0)

    grid_spec = pltpu.PrefetchScalarGridSpec(
        num_scalar_prefetch=2,
        grid=(MOE_TILES,),
        in_specs=[
            pl.BlockSpec((MOE_TM, DW), row_map),
            pl.BlockSpec((None, D, 2 * D_FF), w_map),
            pl.BlockSpec((None, 1, 2 * D_FF), w_map),
            pl.BlockSpec((None, D_FF, D), w_map),
            pl.BlockSpec((None, 1, D), w_map),
        ],
        out_specs=pl.BlockSpec((MOE_TM, DW), row_map),
        scratch_shapes=[pltpu.VMEM((D, 2 * D_FF), BF16), pltpu.VMEM((D_FF, D), BF16)],
    )
    return pl.pallas_call(
        _expert_kernel,
        grid_spec=grid_spec,
        out_shape=jax.ShapeDtypeStruct((MOE_ROWS, DW), jnp.int32),
        compiler_params=pltpu.CompilerParams(
            dimension_semantics=("arbitrary",), vmem_limit_bytes=VMEM_LIMIT),
        name="experts",
    )(tile_expert, n_used, x_sorted, w_gu, b_gu.reshape(N_EXPERTS, 1, 2 * D_FF), w_dn,
      b_dn.reshape(N_EXPERTS, 1, D))


def _combine_kernel(x_ref, y_ref, route_ref, mod_ref, o_ref):
    g2 = mod_ref[...][:, 5 * D:6 * D]
    r = route_ref[...]
    acc = r[:, 0:1] * _unpack_rows(y_ref[0]).astype(F32)
    for k in range(1, TOP_K):
        acc = acc + r[:, k:k + 1] * _unpack_rows(y_ref[k]).astype(F32)
    o_ref[...] = x_ref[...] + g2 * acc


def _combine(x_all, y_tok, route, mod_l):
    return pl.pallas_call(
        _combine_kernel,
        grid=(N_TILES,),
        in_specs=[
            pl.BlockSpec((TM, D), lambda i: (i, 0)),
            pl.BlockSpec((TOP_K, TM, DW), lambda i: (0, i, 0)),
            pl.BlockSpec((TM, LANE), lambda i: (i, 0)),
            pl.BlockSpec((None, 1, 6 * D), lambda i: (_mod_row(i), 0, 0)),
        ],
        out_specs=pl.BlockSpec((TM, D), lambda i: (i, 0)),
        out_shape=jax.ShapeDtypeStruct((T_ALL, D), F32),
        compiler_params=pltpu.CompilerParams(
            dimension_semantics=("arbitrary",), vmem_limit_bytes=VMEM_LIMIT),
        name="combine",
    )(x_all, y_tok, route, mod_l)


SC_CORES, SC_SUBCORES = 2, 16
SC_WORKERS = SC_CORES * SC_SUBCORES
SC_CHUNK = 128
SC_CHUNKS_PER_WORKER = T_ALL // SC_CHUNK // SC_WORKERS


def _sc_mesh():
    return plsc.VectorSubcoreMesh(core_axis_name="c", subcore_axis_name="s")


def _sc_scratch():
    return [pltpu.VMEM((TOP_K, SC_CHUNK), jnp.int32), pltpu.VMEM((SC_CHUNK, DW), jnp.int32),
            pltpu.SemaphoreType.DMA]


def _dispatch_rows(h2p, dest):
    @functools.partial(pl.kernel, mesh=_sc_mesh(), out_type=jax.ShapeDtypeStruct((MOE_ROWS, DW), jnp.int32),
                       scratch_types=_sc_scratch(), name="dispatch_rows")
    def run(h_hbm, d_hbm, o_hbm, idx_v, rows_v, sem):
        wid = lax.axis_index("s") * SC_CORES + lax.axis_index("c")
        for j in range(SC_CHUNKS_PER_WORKER):
            c = wid * SC_CHUNKS_PER_WORKER + j
            pltpu.sync_copy(d_hbm.at[c], idx_v)
            pltpu.sync_copy(h_hbm.at[pl.ds(c * SC_CHUNK, SC_CHUNK)], rows_v)
            copies = [pltpu.async_copy(rows_v, o_hbm.at[idx_v.at[k]], sem) for k in range(TOP_K)]
            for cp in copies:
                cp.wait()

    return run(h2p, dest)


def _gather_rows(y, dest):
    @functools.partial(pl.kernel, mesh=_sc_mesh(), out_type=jax.ShapeDtypeStruct((TOP_K, T_ALL, DW), jnp.int32),
                       scratch_types=_sc_scratch(), name="gather_rows")
    def run(y_hbm, d_hbm, o_hbm, idx_v, rows_v, sem):
        wid = lax.axis_index("s") * SC_CORES + lax.axis_index("c")
        for j in range(SC_CHUNKS_PER_WORKER):
            c = wid * SC_CHUNKS_PER_WORKER + j
            pltpu.sync_copy(d_hbm.at[c], idx_v)
            for k in range(TOP_K):
                pltpu.async_copy(y_hbm.at[idx_v.at[k]], rows_v, sem).wait()
                pltpu.sync_copy(rows_v, o_hbm.at[k, pl.ds(c * SC_CHUNK, SC_CHUNK)])

    return run(y, dest)


def _constants():
    lane64 = np.arange(4 * HD)
    seg64 = (lane64[:, None] // HD == lane64[None, :] // HD).astype(np.float32)
    lane96 = np.arange(4 * BP)
    real = lane96 % BP < B_QK
    seg96 = ((lane96[:, None] // BP == lane96[None, :] // BP) & real[:, None] & real[None, :]).astype(np.float32)
    place = np.zeros((LANE, 4 * BP), np.float32)
    for hh in range(4):
        place[np.arange(B_ROPE), hh * BP + B_NOPE + np.arange(B_ROPE)] = 1.0

    def angles(rot_dim):
        pos = np.arange(LAT_LEN)
        rows = (pos // GRID_W).astype(np.float32)
        cols = (pos % GRID_W).astype(np.float32)
        axis_dim = rot_dim // 2
        inv = jnp.asarray(ROPE_THETA, F32) ** (-jnp.arange(0, axis_dim, 2, dtype=F32) / axis_dim)
        ang = jnp.concatenate([rows[:, None] * inv, cols[:, None] * inv], axis=-1)
        return jnp.cos(ang), jnp.sin(ang)

    def head_tables(rot_dim):
        cos, sin = angles(rot_dim)
        q = rot_dim // 4
        cr, cc, sr, sc = cos[:, :q], cos[:, q:], sin[:, :q], sin[:, q:]
        return (jnp.concatenate([cr, cr, cc, cc], axis=-1), jnp.concatenate([-sr, sr, -sc, sc], axis=-1))

    c64, s64 = head_tables(HD)
    cos64 = jnp.tile(c64, (1, 4))
    sin64 = jnp.tile(s64, (1, 4))
    c32, s32 = head_tables(B_ROPE)
    ones = jnp.ones((LAT_LEN, B_NOPE), F32)
    zeros = jnp.zeros((LAT_LEN, B_NOPE), F32)
    padz = jnp.zeros((LAT_LEN, BP - B_QK), F32)
    cos96 = jnp.tile(jnp.concatenate([ones, c32, padz], axis=-1), (1, 4))
    sin96 = jnp.tile(jnp.concatenate([zeros, s32, padz], axis=-1), (1, 4))
    first64 = ((lane64 % 32) < 16).astype(np.float32)[None, :]
    first96 = (((lane96 % BP) % 16) < 8).astype(np.float32)[None, :]
    tri = (np.arange(TM)[:, None] > np.arange(TM)[None, :]).astype(np.float32)
    return dict(seg64=jnp.asarray(seg64, BF16), seg96=jnp.asarray(seg96, BF16), place=jnp.asarray(place, BF16),
                tri=jnp.asarray(tri, BF16),
                cos64=cos64, sin64=sin64, cos96=cos96, sin96=sin96,
                first64=jnp.asarray(first64), first96=jnp.asarray(first96))


def _pad_heads(w, per_head, width):
    r = w.shape[0]
    w = w.reshape(r, 4, per_head)
    return jnp.pad(w, ((0, 0), (0, 0), (0, width - per_head))).reshape(r, 4 * width)


def _layer_weights(l, norm1_g, norm2_g, w_in, a_q_g, a_k_g, b_cq_g, b_ckv_g, w_uq, w_ukv, b_q_g, b_k_g,
                   c_q_g, c_k_g, w_out, router_w, router_b):
    wi = w_in[l]
    o = np.cumsum((0, 256, 128, 128, 384, 256, 32, 256, 128, 128))
    seg = lambda k: wi[:, o[k]:o[k + 1]]
    w_in_r = jnp.concatenate([seg(0), seg(1), seg(2), seg(3), seg(4), seg(6), seg(7), seg(8), seg(5),
                              jnp.zeros((D, LANE - B_ROPE), F32)], axis=-1).astype(BF16)
    ukv = w_ukv[l].reshape(B_KV_LORA, 4, B_NOPE + B_V)
    w_uk = jnp.pad(ukv[:, :, :B_NOPE], ((0, 0), (0, 0), (0, BP - B_NOPE))).reshape(B_KV_LORA, 4 * BP)
    w_uv = ukv[:, :, B_NOPE:].reshape(B_KV_LORA, 4 * B_V)
    rw = router_w[l]
    rw_hi = rw.astype(BF16)
    rw_lo = (rw - rw_hi.astype(F32)).astype(BF16)
    row = lambda v: v[None, :]
    return dict(
        n1g=row(norm1_g[l]), n2g=row(norm2_g[l]), w_in=w_in_r,
        aqg=row(jnp.tile(a_q_g[l], 4)), akg=row(jnp.tile(a_k_g[l], 2)),
        cqg=row(jnp.tile(c_q_g[l], 4)), ckg=row(jnp.tile(c_k_g[l], 2)),
        bcqg=row(b_cq_g[l]), bckvg=row(b_ckv_g[l]),
        bqg=_pad_heads(row(jnp.tile(b_q_g[l], 4)), B_QK, BP), bkg=_pad_heads(row(jnp.tile(b_k_g[l], 4)), B_QK, BP),
        w_uq=_pad_heads(w_uq[l], B_QK, BP).astype(BF16), w_uk=w_uk.astype(BF16), w_uv=w_uv.astype(BF16),
        w_out=w_out[l].astype(BF16), rw_hi=rw_hi, rw_lo=rw_lo, rb=row(router_b[l]))


def _moe(x_all, h2p, route, counts, mod_l, w_gu, b_gu, w_dn, b_dn):
    counts = counts[0, :N_EXPERTS].astype(jnp.int32)
    padded = (counts + MOE_TM - 1) // MOE_TM * MOE_TM
    pend = jnp.cumsum(padded)
    pstart = pend - padded
    n_used = (pend[-1] // MOE_TM).astype(jnp.int32)
    tiles = jnp.minimum(jnp.arange(MOE_TILES, dtype=jnp.int32), n_used - 1)
    tile_expert = jnp.sum((pend[None, :] <= tiles[:, None] * MOE_TM).astype(jnp.int32), axis=1)
    tile_expert = jnp.minimum(tile_expert, N_EXPERTS - 1)
    e = route[:, TOP_K:2 * TOP_K].astype(jnp.int32)
    rank = route[:, 2 * TOP_K:3 * TOP_K].astype(jnp.int32)
    start = jnp.sum(jnp.where(e[:, :, None] == jnp.arange(N_EXPERTS)[None, None, :], pstart[None, None, :], 0), axis=-1)
    dest = (start + rank).reshape(T_ALL // SC_CHUNK, SC_CHUNK, TOP_K).transpose(0, 2, 1)
    x_sorted = _dispatch_rows(h2p, dest)
    y = _experts(tile_expert, n_used.reshape(1), x_sorted, w_gu, b_gu, w_dn, b_dn)
    y_tok = _gather_rows(y, dest)
    return _combine(x_all, y_tok, route, mod_l)


def kernel(x_prompt, x_sample, cache_a_k, cache_a_v, cache_b_ckv, cache_b_krope, cache_c_k, cache_c_v, c, c_ctx,
           norm1_g, norm2_g, w_mod, b_mod, w_in, a_q_g, a_k_g, a_sink, b_cq_g, b_ckv_g, w_uq, w_ukv, b_q_g, b_k_g,
           c_q_g, c_k_g, w_out, router_w, router_b, w_gu, b_gu, w_dn, b_dn):
    consts = _constants()
    cond = jnp.concatenate([c_ctx[None, :], c, jnp.zeros((3, D), F32)], axis=0)
    mod = _modulation(cond, w_mod, b_mod).reshape(DEPTH, 8, 1, 6 * D)
    x_all = jnp.concatenate([x_prompt.reshape(T_CTX, D), x_sample.reshape(T_LAT, D)], axis=0)

    states = []
    for l in range(DEPTH):
        lw = _layer_weights(l, norm1_g, norm2_g, w_in, a_q_g, a_k_g, b_cq_g, b_ckv_g, w_uq, w_ukv, b_q_g, b_k_g,
                            c_q_g, c_k_g, w_out, router_w, router_b)
        mod_l = mod[l]
        names = ["qa", "ka", "va", "qb", "kb", "vb", "qc", "kc", "vc"]
        outs = _projection(False, x_all, mod_l, lw, consts)
        p_ctx = dict(zip(names, outs[:9]))
        states.append(outs[9:])
        p_lat = dict(zip(names, _projection(True, x_all, mod_l, lw, consts)))

        flat = lambda a, w: a[:, l].reshape(N_LAT_SEQ * PAST, w)
        ckb, cvb = _cache_kv(flat(cache_b_ckv, B_KV_LORA),
                             jnp.pad(flat(cache_b_krope, B_ROPE), ((0, 0), (0, LANE - B_ROPE))), lw, consts)
        cache = dict(ka=flat(cache_a_k, 2 * HD).astype(BF16), va=flat(cache_a_v, 2 * HD).astype(BF16),
                     kb=ckb, vb=cvb,
                     kc=flat(cache_c_k, 2 * HD).astype(BF16), vc=flat(cache_c_v, 2 * HD).astype(BF16))
        mix_ctx = _attention_ctx(a_sink[l], p_ctx)
        mix_lat = _attention_lat(a_sink[l], p_lat, cache)
        x_all, h2p, route, counts = _post_attention(mix_ctx, mix_lat, x_all, mod_l, lw, consts)
        x_all = _moe(x_all, h2p, route, counts, mod_l, w_gu[l], b_gu[l], w_dn[l], b_dn[l])

    def state(k, shape):
        return jnp.stack([states[l][k].reshape(shape) for l in range(DEPTH)], axis=1)

    kv_shape = (N_CTX_SEQ, CTX_LEN, 2, HD)
    return (x_all[:T_CTX].reshape(N_CTX_SEQ, CTX_LEN, D), x_all[T_CTX:].reshape(N_LAT_SEQ, LAT_LEN, D),
            state(0, kv_shape), state(1, kv_shape),
            state(2, (N_CTX_SEQ, CTX_LEN, B_KV_LORA)), state(3, (N_CTX_SEQ, CTX_LEN, B_ROPE)),
            state(4, kv_shape), state(5, kv_shape))
```

```python
import functools

import jax
import jax.numpy as jnp
import numpy as np
from jax import lax
from jax.experimental import pallas as pl
from jax.experimental.pallas import tpu as pltpu
from jax.experimental.pallas import tpu_sc as plsc

F32 = jnp.float32
BF16 = jnp.bfloat16

D = 1024
DEPTH = 2
N_CTX_SEQ, CTX_LEN = 16, 256
N_LAT_SEQ, LAT_LEN = 4, 2048
PAST = 512
T_CTX = N_CTX_SEQ * CTX_LEN
T_LAT = N_LAT_SEQ * LAT_LEN
T_ALL = T_CTX + T_LAT
GRID_W = 64
HD = 64
WINDOW = 128
B_NOPE, B_ROPE, B_V = 64, 32, 128
B_QK = B_NOPE + B_ROPE
B_Q_LORA, B_KV_LORA = 384, 256
N_EXPERTS, TOP_K = 32, 4
D_FF = 1024
SWIGLU_LIMIT = 7.0
SWIGLU_ALPHA = 1.702
ROPE_THETA = 10000.0
EPS = 1e-6
NEG_INF = -1e30

TM = 256
LANE = 128
BP = 128
N_CTX_TILES = T_CTX // TM
N_LAT_TILES = T_LAT // TM
N_TILES = T_ALL // TM
LAT_TILES_PER_SEQ = LAT_LEN // TM
MOE_TM = 256
FF_CHUNK = 256
MOE_TILES = T_ALL * TOP_K // MOE_TM + N_EXPERTS
MOE_ROWS = MOE_TILES * MOE_TM
DW = D // 2
VMEM_LIMIT = 56 * 1024 * 1024

C_QA, C_KA, C_VA, C_CQ, C_CKV, C_QC, C_KC, C_VC, C_KR, C_END = (
    0, 256, 384, 512, 896, 1152, 1408, 1536, 1664, 1792)


def _dot(a, b):
    return jnp.dot(a, b, preferred_element_type=F32)


def _dot_t(a, b):
    return lax.dot_general(a, b, (((1,), (1,)), ((), ())), preferred_element_type=F32)


def _rms(x, g):
    return x * lax.rsqrt(jnp.mean(x * x, axis=-1, keepdims=True) + EPS) * g


def _seg_norm(x, seg, g, n):
    ss = _dot((x * x).astype(BF16), seg)
    return x * lax.rsqrt(ss * (1.0 / n) + EPS) * g


def _rope(x, cos, sin, first, sh):
    w = x.shape[1]
    fwd = pltpu.roll(x, w - sh, 1)
    bwd = pltpu.roll(x, sh, 1)
    return x * cos + jnp.where(first > 0.5, fwd, bwd) * sin


MOD_BN = 1536


def _mod_kernel(c_ref, w_ref, b_ref, o_ref):
    c = c_ref[...]
    s = (c * jax.nn.sigmoid(c)).astype(BF16)
    o_ref[...] = _dot(s, w_ref[...].astype(BF16)) + b_ref[...]


def _modulation(cond, w_mod, b_mod):
    return pl.pallas_call(
        _mod_kernel,
        grid=(DEPTH, 6 * D // MOD_BN),
        in_specs=[
            pl.BlockSpec((8, D), lambda l, j: (0, 0)),
            pl.BlockSpec((None, D, MOD_BN), lambda l, j: (l, 0, j)),
            pl.BlockSpec((None, 1, MOD_BN), lambda l, j: (l, 0, j)),
        ],
        out_specs=pl.BlockSpec((None, 8, MOD_BN), lambda l, j: (l, 0, j)),
        out_shape=jax.ShapeDtypeStruct((DEPTH, 8, 6 * D), F32),
        compiler_params=pltpu.CompilerParams(
            dimension_semantics=("arbitrary", "arbitrary"), vmem_limit_bytes=VMEM_LIMIT),
        name="modulation",
    )(cond, w_mod, b_mod.reshape(DEPTH, 1, 6 * D))


def _proj_kernel(is_lat, n_aliased, *refs):
    (x_ref, mod_ref, n1g_ref, win_ref, seg64_ref, seg96_ref, aqg_ref, akg_ref, cqg_ref, ckg_ref,
     bcqg_ref, bckvg_ref, bqg_ref, bkg_ref, wuq_ref, wuk_ref, wuv_ref, plc_ref) = refs[:18]
    refs = refs[18 + n_aliased:]
    if is_lat:
        (cos64_ref, sin64_ref, cos96_ref, sin96_ref, f64_ref, f96_ref) = refs[:6]
        refs = refs[6:]
    (qa_ref, ka_ref, va_ref, qb_ref, kb_ref, vb_ref, qc_ref, kc_ref, vc_ref) = refs[:9]
    refs = refs[9:]
    if not is_lat:
        (kas_ref, vas_ref, ckvs_ref, krs_ref, kcs_ref, vcs_ref) = refs

    x = x_ref[...]
    m = mod_ref[...]
    sh1, sc1 = m[:, 0:D], m[:, D:2 * D]
    h = (_rms(x, n1g_ref[...]) * (1.0 + sc1) + sh1).astype(BF16)

    def proj(a, b):
        return _dot(h, win_ref[:, a:b])

    def rope64(t):
        wd = t.shape[1]
        return _rope(t, cos64_ref[:, :wd], sin64_ref[:, :wd], f64_ref[:, :wd], 16) if is_lat else t

    def rope96(t):
        return _rope(t, cos96_ref[...], sin96_ref[...], f96_ref[...], 8) if is_lat else t

    seg64 = seg64_ref[...]
    seg64h = seg64_ref[0:2 * HD, 0:2 * HD]
    seg96 = seg96_ref[...]

    t = _seg_norm(proj(C_QA, C_KA), seg64, aqg_ref[...], HD)
    qa_ref[...] = (rope64(t) * HD ** -0.5).astype(BF16)
    t = _seg_norm(proj(C_KA, C_VA), seg64h, akg_ref[...], HD)
    if not is_lat:
        kas_ref[...] = t
    ka_ref[...] = rope64(t).astype(BF16)
    t = proj(C_VA, C_CQ)
    if not is_lat:
        vas_ref[...] = t
    va_ref[...] = t.astype(BF16)

    cq = _rms(proj(C_CQ, C_CKV), bcqg_ref[...]).astype(BF16)
    t = _seg_norm(_dot(cq, wuq_ref[...]), seg96, bqg_ref[...], B_QK)
    qb_ref[...] = (rope96(t) * B_QK ** -0.5).astype(BF16)
    ckv = _rms(proj(C_CKV, C_QC), bckvg_ref[...])
    if not is_lat:
        ckvs_ref[...] = ckv
    ckv16 = ckv.astype(BF16)
    vb_ref[...] = _dot(ckv16, wuv_ref[...]).astype(BF16)
    kr = proj(C_KR, C_END)
    if not is_lat:
        krs_ref[...] = kr[:, 0:B_ROPE]
    kpre = _dot(ckv16, wuk_ref[...]) + _dot(kr.astype(BF16), plc_ref[...])
    t = _seg_norm(kpre, seg96, bkg_ref[...], B_QK)
    kb_ref[...] = rope96(t).astype(BF16)

    t = _seg_norm(proj(C_QC, C_KC), seg64, cqg_ref[...], HD)
    qc_ref[...] = (rope64(t) * HD ** -0.5).astype(BF16)
    t = _seg_norm(proj(C_KC, C_VC), seg64h, ckg_ref[...], HD)
    if not is_lat:
        kcs_ref[...] = t
    kc_ref[...] = rope64(t).astype(BF16)
    t = proj(C_VC, C_KR)
    if not is_lat:
        vcs_ref[...] = t
    vc_ref[...] = t.astype(BF16)


def _full(shape):
    nd = len(shape)
    return pl.BlockSpec(shape, lambda i: (0,) * nd)


STATE_WIDTHS = (2 * HD, 2 * HD, B_KV_LORA, B_ROPE, 2 * HD, 2 * HD)


def _projection(is_lat, l, x_src, x_off, mod_l, lw, consts, prev_states=()):
    n_tiles = N_LAT_TILES if is_lat else N_CTX_TILES
    t_rows = n_tiles * TM
    if is_lat:
        mod_map = lambda i: (1 + i // LAT_TILES_PER_SEQ, 0, 0)
    else:
        mod_map = lambda i: (0, 0, 0)
    ins = [x_src, mod_l, lw["n1g"], lw["w_in"], consts["seg64"], consts["seg96"], lw["aqg"], lw["akg"],
           lw["cqg"], lw["ckg"], lw["bcqg"], lw["bckvg"], lw["bqg"], lw["bkg"], lw["w_uq"], lw["w_uk"],
           lw["w_uv"], consts["place"]]
    in_specs = [pl.BlockSpec((TM, D), lambda i: (i + x_off, 0)),
                pl.BlockSpec((None, 1, 6 * D), mod_map)]
    in_specs += [_full(a.shape) for a in ins[2:]]
    n_plain = len(ins)
    ins += list(prev_states)
    in_specs += [pl.BlockSpec(memory_space=pl.ANY) for _ in prev_states]
    if is_lat:
        tabs = [consts["cos64"], consts["sin64"], consts["cos96"], consts["sin96"]]
        ins += tabs + [consts["first64"], consts["first96"]]
        in_specs += [pl.BlockSpec((TM, a.shape[1]), lambda i: (i % LAT_TILES_PER_SEQ, 0)) for a in tabs]
        in_specs += [_full(consts["first64"].shape), _full(consts["first96"].shape)]
    widths = [4 * HD, 2 * HD, 2 * HD, 4 * BP, 4 * BP, 4 * B_V, 4 * HD, 2 * HD, 2 * HD]
    out_shape = [jax.ShapeDtypeStruct((t_rows, w), BF16) for w in widths]
    out_specs = [pl.BlockSpec((TM, w), lambda i: (i, 0)) for w in widths]
    if not is_lat:
        out_shape += [jax.ShapeDtypeStruct((DEPTH * t_rows, w), F32) for w in STATE_WIDTHS]
        out_specs += [pl.BlockSpec((TM, w), lambda i: (i * DEPTH + l, 0)) for w in STATE_WIDTHS]
    aliases = {n_plain + k: len(widths) + k for k in range(len(prev_states))}
    return pl.pallas_call(
        functools.partial(_proj_kernel, is_lat, len(prev_states)),
        grid=(n_tiles,),
        in_specs=in_specs,
        out_specs=out_specs,
        out_shape=out_shape,
        input_output_aliases=aliases,
        compiler_params=pltpu.CompilerParams(
            dimension_semantics=("arbitrary",), vmem_limit_bytes=VMEM_LIMIT),
        name="proj_lat" if is_lat else "proj_ctx",
    )(*ins)


def _cache_kv_kernel(ckv_ref, kr_ref, seg96_ref, bkg_ref, wuk_ref, wuv_ref, plc_ref, kb_ref, vb_ref):
    ckv16 = ckv_ref[...].astype(BF16)
    vb_ref[...] = _dot(ckv16, wuv_ref[...]).astype(BF16)
    kpre = _dot(ckv16, wuk_ref[...]) + _dot(kr_ref[...].astype(BF16), plc_ref[...])
    kb_ref[...] = _seg_norm(kpre, seg96_ref[...], bkg_ref[...], B_QK).astype(BF16)


def _cache_kv(c_ckv, c_kr, lw, consts):
    rows = c_ckv.shape[0]
    ins = [c_ckv, c_kr, consts["seg96"], lw["bkg"], lw["w_uk"], lw["w_uv"], consts["place"]]
    in_specs = [pl.BlockSpec((TM, B_KV_LORA), lambda i: (i, 0)), pl.BlockSpec((TM, LANE), lambda i: (i, 0))]
    in_specs += [_full(a.shape) for a in ins[2:]]
    return pl.pallas_call(
        _cache_kv_kernel,
        grid=(rows // TM,),
        in_specs=in_specs,
        out_specs=[pl.BlockSpec((TM, 4 * BP), lambda i: (i, 0)), pl.BlockSpec((TM, 4 * B_V), lambda i: (i, 0))],
        out_shape=[jax.ShapeDtypeStruct((rows, 4 * BP), BF16), jax.ShapeDtypeStruct((rows, 4 * B_V), BF16)],
        compiler_params=pltpu.CompilerParams(
            dimension_semantics=("arbitrary",), vmem_limit_bytes=VMEM_LIMIT),
        name="cache_kv",
    )(*ins)


def _attend(q, segs, sink=None):
    scores = []
    m = None
    for k, _, mask in segs:
        s = _dot_t(q, k)
        if mask is not None:
            s = jnp.where(mask, s, NEG_INF)
        scores.append(s)
        sm = jnp.max(s, axis=-1, keepdims=True)
        m = sm if m is None else jnp.maximum(m, sm)
    if sink is not None:
        m = jnp.maximum(m, sink)
    acc = None
    den = None
    for s, (_, v, _) in zip(scores, segs):
        e = jnp.exp(s - m)
        d = jnp.sum(e, axis=-1, keepdims=True)
        o = _dot(e.astype(BF16), v)
        acc = o if acc is None else acc + o
        den = d if den is None else den + d
    if sink is not None:
        den = den + jnp.exp(sink - m)
    return acc / den


def _attn_ctx_kernel(sink_ref, qa_ref, ka_ref, va_ref, qb_ref, kb_ref, vb_ref, qc_ref, kc_ref, vc_ref, o_ref):
    for h in range(4):
        kv = h // 2
        o = _attend(qa_ref[:, h * HD:(h + 1) * HD],
                    [(ka_ref[:, kv * HD:(kv + 1) * HD], va_ref[:, kv * HD:(kv + 1) * HD], None)],
                    sink=sink_ref[h])
        o_ref[:, h * HD:(h + 1) * HD] = o.astype(BF16)
    for h in range(4):
        o = _attend(qb_ref[:, h * BP:(h + 1) * BP],
                    [(kb_ref[:, h * BP:(h + 1) * BP], vb_ref[:, h * B_V:(h + 1) * B_V], None)])
        o_ref[:, 4 * HD + h * B_V:4 * HD + (h + 1) * B_V] = o.astype(BF16)
    for h in range(4):
        kv = h // 2
        o = _attend(qc_ref[:, h * HD:(h + 1) * HD],
                    [(kc_ref[:, kv * HD:(kv + 1) * HD], vc_ref[:, kv * HD:(kv + 1) * HD], None)])
        base = 4 * HD + 4 * B_V
        o_ref[:, base + h * HD:base + (h + 1) * HD] = o.astype(BF16)


def _attention_ctx(sink, p):
    names = ["qa", "ka", "va", "qb", "kb", "vb", "qc", "kc", "vc"]
    ins = [p[n] for n in names]
    in_specs = [pl.BlockSpec(memory_space=pltpu.SMEM)]
    in_specs += [pl.BlockSpec((CTX_LEN, a.shape[1]), lambda i: (i, 0)) for a in ins]
    return pl.pallas_call(
        _attn_ctx_kernel,
        grid=(N_CTX_SEQ,),
        in_specs=in_specs,
        out_specs=pl.BlockSpec((CTX_LEN, D), lambda i: (i, 0)),
        out_shape=jax.ShapeDtypeStruct((T_CTX, D), BF16),
        compiler_params=pltpu.CompilerParams(
            dimension_semantics=("arbitrary",), vmem_limit_bytes=VMEM_LIMIT),
        name="attn_ctx",
    )(sink, *ins)


WIN_SPAN = TM + 2 * WINDOW


def _attn_lat_kernel(sink_ref, qa_ref, qb_ref, qc_ref, ka_ref, va_ref, kb_ref, vb_ref, kc_ref, vc_ref,
                     cka_ref, cva_ref, ckb_ref, cvb_ref, ckc_ref, cvc_ref, o_ref):
    qi = pl.program_id(1)
    ws = pl.multiple_of(jnp.clip(qi * TM - WINDOW, 0, LAT_LEN - WIN_SPAN), WINDOW)
    qpos = qi * TM + lax.broadcasted_iota(jnp.int32, (TM, WIN_SPAN), 0)
    kpos = ws + lax.broadcasted_iota(jnp.int32, (TM, WIN_SPAN), 1)
    band = jnp.abs(qpos - kpos) <= WINDOW
    for h in range(4):
        kv = h // 2
        cs = slice(kv * HD, (kv + 1) * HD)
        o = _attend(qa_ref[:, h * HD:(h + 1) * HD],
                    [(ka_ref[pl.ds(ws, WIN_SPAN), cs], va_ref[pl.ds(ws, WIN_SPAN), cs], band),
                     (cka_ref[:, cs], cva_ref[:, cs], None)],
                    sink=sink_ref[h])
        o_ref[:, h * HD:(h + 1) * HD] = o.astype(BF16)
    for h in range(4):
        ks = slice(h * BP, (h + 1) * BP)
        vs = slice(h * B_V, (h + 1) * B_V)
        o = _attend(qb_ref[:, ks], [(ckb_ref[:, ks], cvb_ref[:, vs], None), (kb_ref[:, ks], vb_ref[:, vs], None)])
        o_ref[:, 4 * HD + h * B_V:4 * HD + (h + 1) * B_V] = o.astype(BF16)
    for h in range(4):
        kv = h // 2
        cs = slice(kv * HD, (kv + 1) * HD)
        o = _attend(qc_ref[:, h * HD:(h + 1) * HD],
                    [(ckc_ref[:, cs], cvc_ref[:, cs], None), (kc_ref[:, cs], vc_ref[:, cs], None)])
        base = 4 * HD + 4 * B_V
        o_ref[:, base + h * HD:base + (h + 1) * HD] = o.astype(BF16)


def _attention_lat(sink, p, cache):
    q_ins = [p["qa"], p["qb"], p["qc"]]
    kv_ins = [p[n] for n in ("ka", "va", "kb", "vb", "kc", "vc")]
    c_ins = [cache[n] for n in ("ka", "va", "kb", "vb", "kc", "vc")]
    in_specs = [pl.BlockSpec(memory_space=pltpu.SMEM)]
    in_specs += [pl.BlockSpec((TM, a.shape[1]), lambda b, i: (b * LAT_TILES_PER_SEQ + i, 0)) for a in q_ins]
    in_specs += [pl.BlockSpec((LAT_LEN, a.shape[1]), lambda b, i: (b, 0)) for a in kv_ins]
    in_specs += [pl.BlockSpec((PAST, a.shape[1]), lambda b, i: (b, 0)) for a in c_ins]
    return pl.pallas_call(
        _attn_lat_kernel,
        grid=(N_LAT_SEQ, LAT_TILES_PER_SEQ),
        in_specs=in_specs,
        out_specs=pl.BlockSpec((TM, D), lambda b, i: (b * LAT_TILES_PER_SEQ + i, 0)),
        out_shape=jax.ShapeDtypeStruct((T_LAT, D), BF16),
        compiler_params=pltpu.CompilerParams(
            dimension_semantics=("arbitrary", "arbitrary"), vmem_limit_bytes=VMEM_LIMIT),
        name="attn_lat",
    )(sink, *q_ins, *kv_ins, *c_ins)


def _pack_rows(x):
    half = x.shape[1] // 2
    r = x.astype(BF16).astype(F32)
    hi = lax.bitcast_convert_type(r[:, :half], jnp.int32)
    lo = lax.bitcast_convert_type(r[:, half:], jnp.int32)
    return jnp.bitwise_or(hi, lax.shift_right_logical(lo, 16))


def _unpack_rows(p):
    a = lax.bitcast_convert_type(jnp.bitwise_and(p, -65536), F32)
    b = lax.bitcast_convert_type(lax.shift_left(p, 16), F32)
    return jnp.concatenate([a, b], axis=1).astype(BF16)


def _post_kernel(mc_ref, ml_ref, xc_ref, xl_ref, mod_ref, n2g_ref, wout_ref, rwh_ref, rwl_ref, rb_ref, tri_ref,
                 xo_ref, h2_ref, route_ref, cnt_ref, run_ref):
    i = pl.program_id(0)

    @pl.when(i == 0)
    def _():
        run_ref[...] = jnp.zeros_like(run_ref)
        cnt_ref[...] = jnp.zeros_like(cnt_ref)

    is_ctx = i < N_CTX_TILES
    mix = jnp.where(is_ctx, mc_ref[...], ml_ref[...])
    m = mod_ref[...]
    g1, sh2, sc2 = m[:, 2 * D:3 * D], m[:, 3 * D:4 * D], m[:, 4 * D:5 * D]
    x = jnp.where(is_ctx, xc_ref[...], xl_ref[...]) + g1 * _dot(mix, wout_ref[...])
    xo_ref[...] = x
    h2 = _rms(x, n2g_ref[...]) * (1.0 + sc2) + sh2
    hi = h2.astype(BF16)
    h2_ref[...] = _pack_rows(h2)
    lo = (h2 - hi.astype(F32)).astype(BF16)
    logits = _dot(hi, rwh_ref[...]) + _dot(lo, rwh_ref[...]) + _dot(hi, rwl_ref[...]) + rb_ref[...]
    lane = lax.broadcasted_iota(jnp.int32, logits.shape, 1).astype(F32)
    lane_o = lax.broadcasted_iota(jnp.int32, (TM, LANE), 1)
    work = logits
    vals, idxs, hots = [], [], []
    for _ in range(TOP_K):
        v = jnp.max(work, axis=-1, keepdims=True)
        idx = jnp.min(jnp.where(work == v, lane, float(N_EXPERTS)), axis=-1, keepdims=True)
        hot = lane == idx
        vals.append(v)
        idxs.append(idx)
        hots.append(hot)
        work = jnp.where(hot, -jnp.inf, work)
    es = [jnp.exp(v - vals[0]) for v in vals]
    den = es[0] + es[1] + es[2] + es[3]
    sel = jnp.where(hots[0] | hots[1] | hots[2] | hots[3], 1.0, 0.0)
    run = run_ref[0:1, 0:N_EXPERTS]
    before = _dot(tri_ref[...], sel.astype(BF16)) + run
    run_new = run + jnp.sum(sel, axis=0, keepdims=True)
    run_ref[0:1, 0:N_EXPERTS] = run_new
    cnt_ref[0:1, 0:N_EXPERTS] = run_new
    slab = jnp.zeros((TM, LANE), F32)
    for k in range(TOP_K):
        rank = jnp.sum(jnp.where(hots[k], before, 0.0), axis=-1, keepdims=True)
        slab = jnp.where(lane_o == k, es[k] / den, slab)
        slab = jnp.where(lane_o == TOP_K + k, idxs[k], slab)
        slab = jnp.where(lane_o == 2 * TOP_K + k, rank, slab)
    route_ref[...] = slab


def _mod_row(i):
    return jnp.where(i < N_CTX_TILES, 0, 1 + (i - N_CTX_TILES) // LAT_TILES_PER_SEQ)


def _ctx_tile(i):
    return jnp.minimum(i, N_CTX_TILES - 1)


def _lat_tile(i):
    return jnp.maximum(i - N_CTX_TILES, 0)


def _post_attention(mix_ctx, mix_lat, x_ctx, x_lat, x_lat_off, mod_l, lw, consts):
    ins = [mix_ctx, mix_lat, x_ctx, x_lat, mod_l, lw["n2g"], lw["w_out"], lw["rw_hi"], lw["rw_lo"], lw["rb"],
           consts["tri"]]
    in_specs = [
        pl.BlockSpec((TM, D), lambda i: (_ctx_tile(i), 0)),
        pl.BlockSpec((TM, D), lambda i: (_lat_tile(i), 0)),
        pl.BlockSpec((TM, D), lambda i: (_ctx_tile(i), 0)),
        pl.BlockSpec((TM, D), lambda i: (_lat_tile(i) + x_lat_off, 0)),
        pl.BlockSpec((None, 1, 6 * D), lambda i: (_mod_row(i), 0, 0)),
    ] + [_full(a.shape) for a in ins[5:]]
    return pl.pallas_call(
        _post_kernel,
        grid=(N_TILES,),
        in_specs=in_specs,
        out_specs=[pl.BlockSpec((TM, D), lambda i: (i, 0)), pl.BlockSpec((TM, DW), lambda i: (i, 0)),
                   pl.BlockSpec((TM, LANE), lambda i: (i, 0)), pl.BlockSpec((8, LANE), lambda i: (0, 0))],
        out_shape=[jax.ShapeDtypeStruct((T_ALL, D), F32), jax.ShapeDtypeStruct((T_ALL, DW), jnp.int32),
                   jax.ShapeDtypeStruct((T_ALL, LANE), F32), jax.ShapeDtypeStruct((8, LANE), F32)],
        scratch_shapes=[pltpu.VMEM((8, LANE), F32)],
        compiler_params=pltpu.CompilerParams(
            dimension_semantics=("arbitrary",), vmem_limit_bytes=VMEM_LIMIT),
        name="post_attn",
    )(*ins)


def _expert_kernel(te_ref, nu_ref, x_ref, wgu_ref, bgu_ref, wdn_ref, bdn_ref, y_ref, wgu16, wdn16):
    i = pl.program_id(0)
    prev = te_ref[jnp.maximum(i - 1, 0)]
    new_expert = jnp.logical_or(i == 0, te_ref[i] != prev)

    @pl.when(new_expert)
    def _():
        wgu16[...] = wgu_ref[...].astype(BF16)
        wdn16[...] = wdn_ref[...].astype(BF16)

    @pl.when(i < nu_ref[0])
    def _():
        x = _unpack_rows(x_ref[...])

        def gate_up(a):
            b = a + FF_CHUNK
            g = _dot(x, wgu16[:, a:b]) + bgu_ref[:, a:b]
            up = _dot(x, wgu16[:, D_FF + a:D_FF + b]) + bgu_ref[:, D_FF + a:D_FF + b]
            return g, up

        def activate(g, up):
            g = jnp.minimum(g, SWIGLU_LIMIT)
            up = jnp.clip(up, -SWIGLU_LIMIT, SWIGLU_LIMIT)
            return ((up + 1.0) * (g * jax.nn.sigmoid(SWIGLU_ALPHA * g))).astype(BF16)

        starts = list(range(0, D_FF, FF_CHUNK))
        acc = None
        pre = gate_up(starts[0])
        for n, a in enumerate(starts):
            hid = activate(*pre)
            if n + 1 < len(starts):
                pre = gate_up(starts[n + 1])
            o = _dot(hid, wdn16[a:a + FF_CHUNK, :])
            acc = o if acc is None else acc + o
        y_ref[...] = _pack_rows(acc + bdn_ref[...])


def _experts(l, tile_expert, n_used, x_sorted, w_gu, b_gu, w_dn, b_dn):
    def row_map(i, te, nu):
        return (jnp.minimum(i, nu[0] - 1), 0)

    def w_map(i, te, nu):
        return (l, te[i], 0, 0)

    grid_spec = pltpu.PrefetchScalarGridSpec(
        num_scalar_prefetch=2,
        grid=(MOE_TILES,),
        in_specs=[
            pl.BlockSpec((MOE_TM, DW), row_map),
            pl.BlockSpec((None, None, D, 2 * D_FF), w_map),
            pl.BlockSpec((None, None, 1, 2 * D_FF), w_map),
            pl.BlockSpec((None, None, D_FF, D), w_map),
            pl.BlockSpec((None, None, 1, D), w_map),
        ],
        out_specs=pl.BlockSpec((MOE_TM, DW), row_map),
        scratch_shapes=[pltpu.VMEM((D, 2 * D_FF), BF16), pltpu.VMEM((D_FF, D), BF16)],
    )
    return pl.pallas_call(
        _expert_kernel,
        grid_spec=grid_spec,
        out_shape=jax.ShapeDtypeStruct((MOE_ROWS, DW), jnp.int32),
        compiler_params=pltpu.CompilerParams(
            dimension_semantics=("arbitrary",), vmem_limit_bytes=VMEM_LIMIT),
        name="experts",
    )(tile_expert, n_used, x_sorted, w_gu, b_gu.reshape(DEPTH, N_EXPERTS, 1, 2 * D_FF), w_dn,
      b_dn.reshape(DEPTH, N_EXPERTS, 1, D))


def _combine_kernel(split, x_ref, y_ref, route_ref, mod_ref, *o_refs):
    g2 = mod_ref[...][:, 5 * D:6 * D]
    r = route_ref[...]
    acc = r[:, 0:1] * _unpack_rows(y_ref[0]).astype(F32)
    for k in range(1, TOP_K):
        acc = acc + r[:, k:k + 1] * _unpack_rows(y_ref[k]).astype(F32)
    res = x_ref[...] + g2 * acc
    if not split:
        o_refs[0][...] = res
        return
    is_ctx = pl.program_id(0) < N_CTX_TILES

    @pl.when(is_ctx)
    def _():
        o_refs[0][...] = res

    @pl.when(jnp.logical_not(is_ctx))
    def _():
        o_refs[1][...] = res


def _combine(split, x_all, y_tok, route, mod_l):
    if split:
        out_specs = [pl.BlockSpec((TM, D), lambda i: (_ctx_tile(i), 0)),
                     pl.BlockSpec((TM, D), lambda i: (_lat_tile(i), 0))]
        out_shape = [jax.ShapeDtypeStruct((T_CTX, D), F32), jax.ShapeDtypeStruct((T_LAT, D), F32)]
    else:
        out_specs = pl.BlockSpec((TM, D), lambda i: (i, 0))
        out_shape = jax.ShapeDtypeStruct((T_ALL, D), F32)
    return pl.pallas_call(
        functools.partial(_combine_kernel, split),
        grid=(N_TILES,),
        in_specs=[
            pl.BlockSpec((TM, D), lambda i: (i, 0)),
            pl.BlockSpec((TOP_K, TM, DW), lambda i: (0, i, 0)),
            pl.BlockSpec((TM, LANE), lambda i: (i, 0)),
            pl.BlockSpec((None, 1, 6 * D), lambda i: (_mod_row(i), 0, 0)),
        ],
        out_specs=out_specs,
        out_shape=out_shape,
        compiler_params=pltpu.CompilerParams(
            dimension_semantics=("arbitrary",), vmem_limit_bytes=VMEM_LIMIT),
        name="combine_split" if split else "combine",
    )(x_all, y_tok, route, mod_l)


SC_CORES, SC_SUBCORES = 2, 16
SC_WORKERS = SC_CORES * SC_SUBCORES
SC_CHUNK = 128
SC_CHUNKS_PER_WORKER = T_ALL // SC_CHUNK // SC_WORKERS


def _sc_mesh():
    return plsc.VectorSubcoreMesh(core_axis_name="c", subcore_axis_name="s")


def _sc_scratch():
    return [pltpu.VMEM((TOP_K, SC_CHUNK), jnp.int32), pltpu.VMEM((SC_CHUNK, DW), jnp.int32),
            pltpu.SemaphoreType.DMA]


def _dispatch_rows(h2p, dest):
    @functools.partial(pl.kernel, mesh=_sc_mesh(), out_type=jax.ShapeDtypeStruct((MOE_ROWS, DW), jnp.int32),
                       scratch_types=_sc_scratch(), name="dispatch_rows")
    def run(h_hbm, d_hbm, o_hbm, idx_v, rows_v, sem):
        wid = lax.axis_index("s") * SC_CORES + lax.axis_index("c")
        for j in range(SC_CHUNKS_PER_WORKER):
            c = wid * SC_CHUNKS_PER_WORKER + j
            pltpu.sync_copy(d_hbm.at[c], idx_v)
            pltpu.sync_copy(h_hbm.at[pl.ds(c * SC_CHUNK, SC_CHUNK)], rows_v)
            copies = [pltpu.async_copy(rows_v, o_hbm.at[idx_v.at[k]], sem) for k in range(TOP_K)]
            for cp in copies:
                cp.wait()

    return run(h2p, dest)


def _gather_rows(y, dest):
    @functools.partial(pl.kernel, mesh=_sc_mesh(), out_type=jax.ShapeDtypeStruct((TOP_K, T_ALL, DW), jnp.int32),
                       scratch_types=_sc_scratch(), name="gather_rows")
    def run(y_hbm, d_hbm, o_hbm, idx_v, rows_v, sem):
        wid = lax.axis_index("s") * SC_CORES + lax.axis_index("c")
        for j in range(SC_CHUNKS_PER_WORKER):
            c = wid * SC_CHUNKS_PER_WORKER + j
            pltpu.sync_copy(d_hbm.at[c], idx_v)
            for k in range(TOP_K):
                pltpu.async_copy(y_hbm.at[idx_v.at[k]], rows_v, sem).wait()
                pltpu.sync_copy(rows_v, o_hbm.at[k, pl.ds(c * SC_CHUNK, SC_CHUNK)])

    return run(y, dest)


def _constants():
    lane64 = np.arange(4 * HD)
    seg64 = (lane64[:, None] // HD == lane64[None, :] // HD).astype(np.float32)
    lane96 = np.arange(4 * BP)
    real = lane96 % BP < B_QK
    seg96 = ((lane96[:, None] // BP == lane96[None, :] // BP) & real[:, None] & real[None, :]).astype(np.float32)
    place = np.zeros((LANE, 4 * BP), np.float32)
    for hh in range(4):
        place[np.arange(B_ROPE), hh * BP + B_NOPE + np.arange(B_ROPE)] = 1.0

    def angles(rot_dim):
        pos = np.arange(LAT_LEN)
        rows = (pos // GRID_W).astype(np.float32)
        cols = (pos % GRID_W).astype(np.float32)
        axis_dim = rot_dim // 2
        inv = jnp.asarray(ROPE_THETA, F32) ** (-jnp.arange(0, axis_dim, 2, dtype=F32) / axis_dim)
        ang = jnp.concatenate([rows[:, None] * inv, cols[:, None] * inv], axis=-1)
        return jnp.cos(ang), jnp.sin(ang)

    def head_tables(rot_dim):
        cos, sin = angles(rot_dim)
        q = rot_dim // 4
        cr, cc, sr, sc = cos[:, :q], cos[:, q:], sin[:, :q], sin[:, q:]
        return (jnp.concatenate([cr, cr, cc, cc], axis=-1), jnp.concatenate([-sr, sr, -sc, sc], axis=-1))

    c64, s64 = head_tables(HD)
    cos64 = jnp.tile(c64, (1, 4))
    sin64 = jnp.tile(s64, (1, 4))
    c32, s32 = head_tables(B_ROPE)
    ones = jnp.ones((LAT_LEN, B_NOPE), F32)
    zeros = jnp.zeros((LAT_LEN, B_NOPE), F32)
    padz = jnp.zeros((LAT_LEN, BP - B_QK), F32)
    cos96 = jnp.tile(jnp.concatenate([ones, c32, padz], axis=-1), (1, 4))
    sin96 = jnp.tile(jnp.concatenate([zeros, s32, padz], axis=-1), (1, 4))
    first64 = ((lane64 % 32) < 16).astype(np.float32)[None, :]
    first96 = (((lane96 % BP) % 16) < 8).astype(np.float32)[None, :]
    tri = (np.arange(TM)[:, None] > np.arange(TM)[None, :]).astype(np.float32)
    return dict(seg64=jnp.asarray(seg64, BF16), seg96=jnp.asarray(seg96, BF16), place=jnp.asarray(place, BF16),
                tri=jnp.asarray(tri, BF16),
                cos64=cos64, sin64=sin64, cos96=cos96, sin96=sin96,
                first64=jnp.asarray(first64), first96=jnp.asarray(first96))


def _pad_heads(w, per_head, width):
    r = w.shape[0]
    w = w.reshape(r, 4, per_head)
    return jnp.pad(w, ((0, 0), (0, 0), (0, width - per_head))).reshape(r, 4 * width)


def _layer_weights(l, norm1_g, norm2_g, w_in, a_q_g, a_k_g, b_cq_g, b_ckv_g, w_uq, w_ukv, b_q_g, b_k_g,
                   c_q_g, c_k_g, w_out, router_w, router_b):
    wi = w_in[l]
    o = np.cumsum((0, 256, 128, 128, 384, 256, 32, 256, 128, 128))
    seg = lambda k: wi[:, o[k]:o[k + 1]]
    w_in_r = jnp.concatenate([seg(0), seg(1), seg(2), seg(3), seg(4), seg(6), seg(7), seg(8), seg(5),
                              jnp.zeros((D, LANE - B_ROPE), F32)], axis=-1).astype(BF16)
    ukv = w_ukv[l].reshape(B_KV_LORA, 4, B_NOPE + B_V)
    w_uk = jnp.pad(ukv[:, :, :B_NOPE], ((0, 0), (0, 0), (0, BP - B_NOPE))).reshape(B_KV_LORA, 4 * BP)
    w_uv = ukv[:, :, B_NOPE:].reshape(B_KV_LORA, 4 * B_V)
    rw = router_w[l]
    rw_hi = rw.astype(BF16)
    rw_lo = (rw - rw_hi.astype(F32)).astype(BF16)
    row = lambda v: v[None, :]
    return dict(
        n1g=row(norm1_g[l]), n2g=row(norm2_g[l]), w_in=w_in_r,
        aqg=row(jnp.tile(a_q_g[l], 4)), akg=row(jnp.tile(a_k_g[l], 2)),
        cqg=row(jnp.tile(c_q_g[l], 4)), ckg=row(jnp.tile(c_k_g[l], 2)),
        bcqg=row(b_cq_g[l]), bckvg=row(b_ckv_g[l]),
        bqg=_pad_heads(row(jnp.tile(b_q_g[l], 4)), B_QK, BP), bkg=_pad_heads(row(jnp.tile(b_k_g[l], 4)), B_QK, BP),
        w_uq=_pad_heads(w_uq[l], B_QK, BP).astype(BF16), w_uk=w_uk.astype(BF16), w_uv=w_uv.astype(BF16),
        w_out=w_out[l].astype(BF16), rw_hi=rw_hi, rw_lo=rw_lo, rb=row(router_b[l]))


def _moe(l, x_all, h2p, route, counts, mod_l, w_gu, b_gu, w_dn, b_dn):
    counts = counts[0, :N_EXPERTS].astype(jnp.int32)
    padded = (counts + MOE_TM - 1) // MOE_TM * MOE_TM
    pend = jnp.cumsum(padded)
    pstart = pend - padded
    n_used = (pend[-1] // MOE_TM).astype(jnp.int32)
    tiles = jnp.minimum(jnp.arange(MOE_TILES, dtype=jnp.int32), n_used - 1)
    tile_expert = jnp.sum((pend[None, :] <= tiles[:, None] * MOE_TM).astype(jnp.int32), axis=1)
    tile_expert = jnp.minimum(tile_expert, N_EXPERTS - 1)
    e = route[:, TOP_K:2 * TOP_K].astype(jnp.int32)
    rank = route[:, 2 * TOP_K:3 * TOP_K].astype(jnp.int32)
    start = jnp.sum(jnp.where(e[:, :, None] == jnp.arange(N_EXPERTS)[None, None, :], pstart[None, None, :], 0), axis=-1)
    dest = (start + rank).reshape(T_ALL // SC_CHUNK, SC_CHUNK, TOP_K).transpose(0, 2, 1)
    x_sorted = _dispatch_rows(h2p, dest)
    y = _experts(l, tile_expert, n_used.reshape(1), x_sorted, w_gu, b_gu, w_dn, b_dn)
    y_tok = _gather_rows(y, dest)
    return _combine(l == DEPTH - 1, x_all, y_tok, route, mod_l)


def kernel(x_prompt, x_sample, cache_a_k, cache_a_v, cache_b_ckv, cache_b_krope, cache_c_k, cache_c_v, c, c_ctx,
           norm1_g, norm2_g, w_mod, b_mod, w_in, a_q_g, a_k_g, a_sink, b_cq_g, b_ckv_g, w_uq, w_ukv, b_q_g, b_k_g,
           c_q_g, c_k_g, w_out, router_w, router_b, w_gu, b_gu, w_dn, b_dn):
    consts = _constants()
    cond = jnp.concatenate([c_ctx[None, :], c, jnp.zeros((3, D), F32)], axis=0)
    mod = _modulation(cond, w_mod, b_mod).reshape(DEPTH, 8, 1, 6 * D)
    x_ctx, x_lat, x_lat_off = x_prompt.reshape(T_CTX, D), x_sample.reshape(T_LAT, D), 0

    states = ()
    for l in range(DEPTH):
        lw = _layer_weights(l, norm1_g, norm2_g, w_in, a_q_g, a_k_g, b_cq_g, b_ckv_g, w_uq, w_ukv, b_q_g, b_k_g,
                            c_q_g, c_k_g, w_out, router_w, router_b)
        mod_l = mod[l]
        names = ["qa", "ka", "va", "qb", "kb", "vb", "qc", "kc", "vc"]
        outs = _projection(False, l, x_ctx, 0, mod_l, lw, consts, states)
        p_ctx = dict(zip(names, outs[:9]))
        states = tuple(outs[9:])
        p_lat = dict(zip(names, _projection(True, l, x_lat, x_lat_off, mod_l, lw, consts)))

        flat = lambda a, w: a[:, l].reshape(N_LAT_SEQ * PAST, w)
        ckb, cvb = _cache_kv(flat(cache_b_ckv, B_KV_LORA),
                             jnp.pad(flat(cache_b_krope, B_ROPE), ((0, 0), (0, LANE - B_ROPE))), lw, consts)
        cache = dict(ka=flat(cache_a_k, 2 * HD).astype(BF16), va=flat(cache_a_v, 2 * HD).astype(BF16),
                     kb=ckb, vb=cvb,
                     kc=flat(cache_c_k, 2 * HD).astype(BF16), vc=flat(cache_c_v, 2 * HD).astype(BF16))
        mix_ctx = _attention_ctx(a_sink[l], p_ctx)
        mix_lat = _attention_lat(a_sink[l], p_lat, cache)
        x_mid, h2p, route, counts = _post_attention(mix_ctx, mix_lat, x_ctx, x_lat, x_lat_off, mod_l, lw, consts)
        out = _moe(l, x_mid, h2p, route, counts, mod_l, w_gu, b_gu, w_dn, b_dn)
        if l < DEPTH - 1:
            x_ctx, x_lat, x_lat_off = out, out, N_CTX_TILES
    y_ctx, y_lat = out

    kv_shape = (N_CTX_SEQ, DEPTH, CTX_LEN, 2, HD)
    return (y_ctx.reshape(N_CTX_SEQ, CTX_LEN, D), y_lat.reshape(N_LAT_SEQ, LAT_LEN, D),
            states[0].reshape(kv_shape), states[1].reshape(kv_shape),
            states[2].reshape(N_CTX_SEQ, DEPTH, CTX_LEN, B_KV_LORA),
            states[3].reshape(N_CTX_SEQ, DEPTH, CTX_LEN, B_ROPE),
            states[4].reshape(kv_shape), states[5].reshape(kv_shape))
```

```python
import functools

import jax
import jax.numpy as jnp
import numpy as np
from jax import lax
from jax.experimental import pallas as pl
from jax.experimental.pallas import tpu as pltpu
from jax.experimental.pallas import tpu_sc as plsc

F32 = jnp.float32
BF16 = jnp.bfloat16

D = 1024
DEPTH = 2
N_CTX_SEQ, CTX_LEN = 16, 256
N_LAT_SEQ, LAT_LEN = 4, 2048
PAST = 512
T_CTX = N_CTX_SEQ * CTX_LEN
T_LAT = N_LAT_SEQ * LAT_LEN
T_ALL = T_CTX + T_LAT
GRID_W = 64
HD = 64
WINDOW = 128
B_NOPE, B_ROPE, B_V = 64, 32, 128
B_QK = B_NOPE + B_ROPE
B_Q_LORA, B_KV_LORA = 384, 256
N_EXPERTS, TOP_K = 32, 4
D_FF = 1024
SWIGLU_LIMIT = 7.0
SWIGLU_ALPHA = 1.702
ROPE_THETA = 10000.0
EPS = 1e-6
NEG_INF = -1e30

TM = 256
LANE = 128
BP = 128
N_CTX_TILES = T_CTX // TM
N_LAT_TILES = T_LAT // TM
N_TILES = T_ALL // TM
LAT_TILES_PER_SEQ = LAT_LEN // TM
MOE_TM = 256
FF_CHUNK = 256
MOE_TILES = T_ALL * TOP_K // MOE_TM + N_EXPERTS
MOE_ROWS = MOE_TILES * MOE_TM
DW = D // 2
VMEM_LIMIT = 56 * 1024 * 1024

C_QA, C_KA, C_VA, C_CQ, C_CKV, C_QC, C_KC, C_VC, C_KR, C_END = (
    0, 256, 384, 512, 896, 1152, 1408, 1536, 1664, 1792)


def _dot(a, b):
    return jnp.dot(a, b, preferred_element_type=F32)


def _dot_t(a, b):
    return lax.dot_general(a, b, (((1,), (1,)), ((), ())), preferred_element_type=F32)


def _rms(x, g):
    return x * lax.rsqrt(jnp.mean(x * x, axis=-1, keepdims=True) + EPS) * g


def _seg_norm(x, seg, g, n):
    ss = _dot((x * x).astype(BF16), seg)
    return x * lax.rsqrt(ss * (1.0 / n) + EPS) * g


def _rope(x, cos, sin, first, sh):
    w = x.shape[1]
    fwd = pltpu.roll(x, w - sh, 1)
    bwd = pltpu.roll(x, sh, 1)
    return x * cos + jnp.where(first > 0.5, fwd, bwd) * sin


MOD_BN = 1536


def _mod_kernel(c_ref, w_ref, b_ref, o_ref):
    c = c_ref[...]
    s = (c * jax.nn.sigmoid(c)).astype(BF16)
    o_ref[...] = _dot(s, w_ref[...].astype(BF16)) + b_ref[...]


def _modulation(cond, w_mod, b_mod):
    return pl.pallas_call(
        _mod_kernel,
        grid=(DEPTH, 6 * D // MOD_BN),
        in_specs=[
            pl.BlockSpec((8, D), lambda l, j: (0, 0)),
            pl.BlockSpec((None, D, MOD_BN), lambda l, j: (l, 0, j)),
            pl.BlockSpec((None, 1, MOD_BN), lambda l, j: (l, 0, j)),
        ],
        out_specs=pl.BlockSpec((None, 8, MOD_BN), lambda l, j: (l, 0, j)),
        out_shape=jax.ShapeDtypeStruct((DEPTH, 8, 6 * D), F32),
        compiler_params=pltpu.CompilerParams(
            dimension_semantics=("arbitrary", "arbitrary"), vmem_limit_bytes=VMEM_LIMIT),
        name="modulation",
    )(cond, w_mod, b_mod.reshape(DEPTH, 1, 6 * D))


def _proj_kernel(is_lat, n_aliased, *refs):
    (x_ref, mod_ref, n1g_ref, win_ref, seg64_ref, seg96_ref, aqg_ref, akg_ref, cqg_ref, ckg_ref,
     bcqg_ref, bckvg_ref, bqg_ref, bkg_ref, wuq_ref, wuk_ref, wuv_ref, plc_ref) = refs[:18]
    refs = refs[18 + n_aliased:]
    if is_lat:
        (cos64_ref, sin64_ref, cos96_ref, sin96_ref, f64_ref, f96_ref) = refs[:6]
        refs = refs[6:]
    (qa_ref, ka_ref, va_ref, qb_ref, kb_ref, vb_ref, qc_ref, kc_ref, vc_ref) = refs[:9]
    refs = refs[9:]
    if not is_lat:
        (kas_ref, vas_ref, ckvs_ref, krs_ref, kcs_ref, vcs_ref) = refs

    x = x_ref[...]
    m = mod_ref[...]
    sh1, sc1 = m[:, 0:D], m[:, D:2 * D]
    h = (_rms(x, n1g_ref[...]) * (1.0 + sc1) + sh1).astype(BF16)

    def proj(a, b):
        return _dot(h, win_ref[:, a:b])

    def rope64(t):
        wd = t.shape[1]
        return _rope(t, cos64_ref[:, :wd], sin64_ref[:, :wd], f64_ref[:, :wd], 16) if is_lat else t

    def rope96(t):
        return _rope(t, cos96_ref[...], sin96_ref[...], f96_ref[...], 8) if is_lat else t

    seg64 = seg64_ref[...]
    seg64h = seg64_ref[0:2 * HD, 0:2 * HD]
    seg96 = seg96_ref[...]

    def seg_sum(t, seg):
        return _dot((t * t).astype(BF16), seg)

    def seg_finish(t, ss, g, n):
        return t * lax.rsqrt(ss * (1.0 / n) + EPS) * g

    p_cq, p_ckv, p_kr = proj(C_CQ, C_CKV), proj(C_CKV, C_QC), proj(C_KR, C_END)
    p_qa, p_ka, p_qc, p_kc = proj(C_QA, C_KA), proj(C_KA, C_VA), proj(C_QC, C_KC), proj(C_KC, C_VC)
    p_va, p_vc = proj(C_VA, C_CQ), proj(C_VC, C_KR)

    cq = _rms(p_cq, bcqg_ref[...]).astype(BF16)
    ckv = _rms(p_ckv, bckvg_ref[...])
    ckv16 = ckv.astype(BF16)
    u_q = _dot(cq, wuq_ref[...])
    u_k = _dot(ckv16, wuk_ref[...]) + _dot(p_kr.astype(BF16), plc_ref[...])
    u_v = _dot(ckv16, wuv_ref[...])
    ss_qa, ss_ka = seg_sum(p_qa, seg64), seg_sum(p_ka, seg64h)
    ss_qc, ss_kc = seg_sum(p_qc, seg64), seg_sum(p_kc, seg64h)
    ss_qb, ss_kb = seg_sum(u_q, seg96), seg_sum(u_k, seg96)

    if not is_lat:
        vas_ref[...] = p_va
        vcs_ref[...] = p_vc
        ckvs_ref[...] = ckv
        krs_ref[...] = p_kr[:, 0:B_ROPE]
    va_ref[...] = p_va.astype(BF16)
    vc_ref[...] = p_vc.astype(BF16)
    vb_ref[...] = u_v.astype(BF16)

    t = seg_finish(p_qa, ss_qa, aqg_ref[...], HD)
    qa_ref[...] = (rope64(t) * HD ** -0.5).astype(BF16)
    t = seg_finish(p_ka, ss_ka, akg_ref[...], HD)
    if not is_lat:
        kas_ref[...] = t
    ka_ref[...] = rope64(t).astype(BF16)
    t = seg_finish(p_qc, ss_qc, cqg_ref[...], HD)
    qc_ref[...] = (rope64(t) * HD ** -0.5).astype(BF16)
    t = seg_finish(p_kc, ss_kc, ckg_ref[...], HD)
    if not is_lat:
        kcs_ref[...] = t
    kc_ref[...] = rope64(t).astype(BF16)
    t = seg_finish(u_q, ss_qb, bqg_ref[...], B_QK)
    qb_ref[...] = (rope96(t) * B_QK ** -0.5).astype(BF16)
    t = seg_finish(u_k, ss_kb, bkg_ref[...], B_QK)
    kb_ref[...] = rope96(t).astype(BF16)


def _full(shape):
    nd = len(shape)
    return pl.BlockSpec(shape, lambda i: (0,) * nd)


STATE_WIDTHS = (2 * HD, 2 * HD, B_KV_LORA, B_ROPE, 2 * HD, 2 * HD)


def _projection(is_lat, l, x_src, x_off, mod_l, lw, consts, prev_states=()):
    n_tiles = N_LAT_TILES if is_lat else N_CTX_TILES
    t_rows = n_tiles * TM
    if is_lat:
        mod_map = lambda i: (1 + i // LAT_TILES_PER_SEQ, 0, 0)
    else:
        mod_map = lambda i: (0, 0, 0)
    ins = [x_src, mod_l, lw["n1g"], lw["w_in"], consts["seg64"], consts["seg96"], lw["aqg"], lw["akg"],
           lw["cqg"], lw["ckg"], lw["bcqg"], lw["bckvg"], lw["bqg"], lw["bkg"], lw["w_uq"], lw["w_uk"],
           lw["w_uv"], consts["place"]]
    in_specs = [pl.BlockSpec((TM, D), lambda i: (i + x_off, 0)),
                pl.BlockSpec((None, 1, 6 * D), mod_map)]
    in_specs += [_full(a.shape) for a in ins[2:]]
    n_plain = len(ins)
    ins += list(prev_states)
    in_specs += [pl.BlockSpec(memory_space=pl.ANY) for _ in prev_states]
    if is_lat:
        tabs = [consts["cos64"], consts["sin64"], consts["cos96"], consts["sin96"]]
        ins += tabs + [consts["first64"], consts["first96"]]
        in_specs += [pl.BlockSpec((TM, a.shape[1]), lambda i: (i % LAT_TILES_PER_SEQ, 0)) for a in tabs]
        in_specs += [_full(consts["first64"].shape), _full(consts["first96"].shape)]
    widths = [4 * HD, 2 * HD, 2 * HD, 4 * BP, 4 * BP, 4 * B_V, 4 * HD, 2 * HD, 2 * HD]
    out_shape = [jax.ShapeDtypeStruct((t_rows, w), BF16) for w in widths]
    out_specs = [pl.BlockSpec((TM, w), lambda i: (i, 0)) for w in widths]
    if not is_lat:
        out_shape += [jax.ShapeDtypeStruct((DEPTH * t_rows, w), F32) for w in STATE_WIDTHS]
        out_specs += [pl.BlockSpec((TM, w), lambda i: (i * DEPTH + l, 0)) for w in STATE_WIDTHS]
    aliases = {n_plain + k: len(widths) + k for k in range(len(prev_states))}
    return pl.pallas_call(
        functools.partial(_proj_kernel, is_lat, len(prev_states)),
        grid=(n_tiles,),
        in_specs=in_specs,
        out_specs=out_specs,
        out_shape=out_shape,
        input_output_aliases=aliases,
        compiler_params=pltpu.CompilerParams(
            dimension_semantics=("arbitrary",), vmem_limit_bytes=VMEM_LIMIT),
        name="proj_lat" if is_lat else "proj_ctx",
    )(*ins)


def _cache_kv_kernel(ckv_ref, kr_ref, seg96_ref, bkg_ref, wuk_ref, wuv_ref, plc_ref, kb_ref, vb_ref):
    ckv16 = ckv_ref[...].astype(BF16)
    vb_ref[...] = _dot(ckv16, wuv_ref[...]).astype(BF16)
    kpre = _dot(ckv16, wuk_ref[...]) + _dot(kr_ref[...].astype(BF16), plc_ref[...])
    kb_ref[...] = _seg_norm(kpre, seg96_ref[...], bkg_ref[...], B_QK).astype(BF16)


def _cache_kv(c_ckv, c_kr, lw, consts):
    rows = c_ckv.shape[0]
    ins = [c_ckv, c_kr, consts["seg96"], lw["bkg"], lw["w_uk"], lw["w_uv"], consts["place"]]
    in_specs = [pl.BlockSpec((TM, B_KV_LORA), lambda i: (i, 0)), pl.BlockSpec((TM, LANE), lambda i: (i, 0))]
    in_specs += [_full(a.shape) for a in ins[2:]]
    return pl.pallas_call(
        _cache_kv_kernel,
        grid=(rows // TM,),
        in_specs=in_specs,
        out_specs=[pl.BlockSpec((TM, 4 * BP), lambda i: (i, 0)), pl.BlockSpec((TM, 4 * B_V), lambda i: (i, 0))],
        out_shape=[jax.ShapeDtypeStruct((rows, 4 * BP), BF16), jax.ShapeDtypeStruct((rows, 4 * B_V), BF16)],
        compiler_params=pltpu.CompilerParams(
            dimension_semantics=("arbitrary",), vmem_limit_bytes=VMEM_LIMIT),
        name="cache_kv",
    )(*ins)


def _attend(q, segs, sink=None):
    scores = []
    m = None
    for k, _, mask in segs:
        s = _dot_t(q, k)
        if mask is not None:
            s = jnp.where(mask, s, NEG_INF)
        scores.append(s)
        sm = jnp.max(s, axis=-1, keepdims=True)
        m = sm if m is None else jnp.maximum(m, sm)
    if sink is not None:
        m = jnp.maximum(m, sink)
    acc = None
    den = None
    for s, (_, v, _) in zip(scores, segs):
        e = jnp.exp(s - m)
        d = jnp.sum(e, axis=-1, keepdims=True)
        o = _dot(e.astype(BF16), v)
        acc = o if acc is None else acc + o
        den = d if den is None else den + d
    if sink is not None:
        den = den + jnp.exp(sink - m)
    return acc / den


def _attn_ctx_kernel(sink_ref, qa_ref, ka_ref, va_ref, qb_ref, kb_ref, vb_ref, qc_ref, kc_ref, vc_ref, o_ref):
    for h in range(4):
        kv = h // 2
        o = _attend(qa_ref[:, h * HD:(h + 1) * HD],
                    [(ka_ref[:, kv * HD:(kv + 1) * HD], va_ref[:, kv * HD:(kv + 1) * HD], None)],
                    sink=sink_ref[h])
        o_ref[:, h * HD:(h + 1) * HD] = o.astype(BF16)
    for h in range(4):
        o = _attend(qb_ref[:, h * BP:(h + 1) * BP],
                    [(kb_ref[:, h * BP:(h + 1) * BP], vb_ref[:, h * B_V:(h + 1) * B_V], None)])
        o_ref[:, 4 * HD + h * B_V:4 * HD + (h + 1) * B_V] = o.astype(BF16)
    for h in range(4):
        kv = h // 2
        o = _attend(qc_ref[:, h * HD:(h + 1) * HD],
                    [(kc_ref[:, kv * HD:(kv + 1) * HD], vc_ref[:, kv * HD:(kv + 1) * HD], None)])
        base = 4 * HD + 4 * B_V
        o_ref[:, base + h * HD:base + (h + 1) * HD] = o.astype(BF16)


def _attention_ctx(sink, p):
    names = ["qa", "ka", "va", "qb", "kb", "vb", "qc", "kc", "vc"]
    ins = [p[n] for n in names]
    in_specs = [pl.BlockSpec(memory_space=pltpu.SMEM)]
    in_specs += [pl.BlockSpec((CTX_LEN, a.shape[1]), lambda i: (i, 0)) for a in ins]
    return pl.pallas_call(
        _attn_ctx_kernel,
        grid=(N_CTX_SEQ,),
        in_specs=in_specs,
        out_specs=pl.BlockSpec((CTX_LEN, D), lambda i: (i, 0)),
        out_shape=jax.ShapeDtypeStruct((T_CTX, D), BF16),
        compiler_params=pltpu.CompilerParams(
            dimension_semantics=("arbitrary",), vmem_limit_bytes=VMEM_LIMIT),
        name="attn_ctx",
    )(sink, *ins)


WIN_SPAN = TM + 2 * WINDOW


def _attn_lat_kernel(sink_ref, qa_ref, qb_ref, qc_ref, ka_ref, va_ref, kb_ref, vb_ref, kc_ref, vc_ref,
                     cka_ref, cva_ref, ckb_ref, cvb_ref, ckc_ref, cvc_ref, o_ref):
    qi = pl.program_id(1)
    ws = pl.multiple_of(jnp.clip(qi * TM - WINDOW, 0, LAT_LEN - WIN_SPAN), WINDOW)
    qpos = qi * TM + lax.broadcasted_iota(jnp.int32, (TM, WIN_SPAN), 0)
    kpos = ws + lax.broadcasted_iota(jnp.int32, (TM, WIN_SPAN), 1)
    band = jnp.abs(qpos - kpos) <= WINDOW
    for h in range(4):
        kv = h // 2
        cs = slice(kv * HD, (kv + 1) * HD)
        o = _attend(qa_ref[:, h * HD:(h + 1) * HD],
                    [(ka_ref[pl.ds(ws, WIN_SPAN), cs], va_ref[pl.ds(ws, WIN_SPAN), cs], band),
                     (cka_ref[:, cs], cva_ref[:, cs], None)],
                    sink=sink_ref[h])
        o_ref[:, h * HD:(h + 1) * HD] = o.astype(BF16)
    for h in range(4):
        ks = slice(h * BP, (h + 1) * BP)
        vs = slice(h * B_V, (h + 1) * B_V)
        o = _attend(qb_ref[:, ks], [(ckb_ref[:, ks], cvb_ref[:, vs], None), (kb_ref[:, ks], vb_ref[:, vs], None)])
        o_ref[:, 4 * HD + h * B_V:4 * HD + (h + 1) * B_V] = o.astype(BF16)
    for h in range(4):
        kv = h // 2
        cs = slice(kv * HD, (kv + 1) * HD)
        o = _attend(qc_ref[:, h * HD:(h + 1) * HD],
                    [(ckc_ref[:, cs], cvc_ref[:, cs], None), (kc_ref[:, cs], vc_ref[:, cs], None)])
        base = 4 * HD + 4 * B_V
        o_ref[:, base + h * HD:base + (h + 1) * HD] = o.astype(BF16)


def _attention_lat(sink, p, cache):
    q_ins = [p["qa"], p["qb"], p["qc"]]
    kv_ins = [p[n] for n in ("ka", "va", "kb", "vb", "kc", "vc")]
    c_ins = [cache[n] for n in ("ka", "va", "kb", "vb", "kc", "vc")]
    in_specs = [pl.BlockSpec(memory_space=pltpu.SMEM)]
    in_specs += [pl.BlockSpec((TM, a.shape[1]), lambda b, i: (b * LAT_TILES_PER_SEQ + i, 0)) for a in q_ins]
    in_specs += [pl.BlockSpec((LAT_LEN, a.shape[1]), lambda b, i: (b, 0)) for a in kv_ins]
    in_specs += [pl.BlockSpec((PAST, a.shape[1]), lambda b, i: (b, 0)) for a in c_ins]
    return pl.pallas_call(
        _attn_lat_kernel,
        grid=(N_LAT_SEQ, LAT_TILES_PER_SEQ),
        in_specs=in_specs,
        out_specs=pl.BlockSpec((TM, D), lambda b, i: (b * LAT_TILES_PER_SEQ + i, 0)),
        out_shape=jax.ShapeDtypeStruct((T_LAT, D), BF16),
        compiler_params=pltpu.CompilerParams(
            dimension_semantics=("arbitrary", "arbitrary"), vmem_limit_bytes=VMEM_LIMIT),
        name="attn_lat",
    )(sink, *q_ins, *kv_ins, *c_ins)


def _pack_rows(x):
    half = x.shape[1] // 2
    r = x.astype(BF16).astype(F32)
    hi = lax.bitcast_convert_type(r[:, :half], jnp.int32)
    lo = lax.bitcast_convert_type(r[:, half:], jnp.int32)
    return jnp.bitwise_or(hi, lax.shift_right_logical(lo, 16))


def _unpack_rows(p):
    a = lax.bitcast_convert_type(jnp.bitwise_and(p, -65536), F32)
    b = lax.bitcast_convert_type(lax.shift_left(p, 16), F32)
    return jnp.concatenate([a, b], axis=1).astype(BF16)


ROUTE_ROWS = 16


def _post_kernel(mc_ref, ml_ref, xc_ref, xl_ref, mod_ref, n2g_ref, wout_ref, rwt_ref, rb_ref, tri_ref,
                 xo_ref, h2_ref, gate_ref, route_ref, cnt_ref, run_ref):
    i = pl.program_id(0)

    @pl.when(i == 0)
    def _():
        run_ref[...] = jnp.zeros_like(run_ref)
        cnt_ref[...] = jnp.zeros_like(cnt_ref)

    is_ctx = i < N_CTX_TILES
    mix = jnp.where(is_ctx, mc_ref[...], ml_ref[...])
    m = mod_ref[...]
    g1, sh2, sc2 = m[:, 2 * D:3 * D], m[:, 3 * D:4 * D], m[:, 4 * D:5 * D]
    x = jnp.where(is_ctx, xc_ref[...], xl_ref[...]) + g1 * _dot(mix, wout_ref[...])
    xo_ref[...] = x
    h2 = _rms(x, n2g_ref[...]) * (1.0 + sc2) + sh2
    hi = h2.astype(BF16)
    h2_ref[...] = _pack_rows(h2)
    lo = (h2 - hi.astype(F32)).astype(BF16)
    a = _dot_t(rwt_ref[...], hi)
    b = _dot_t(rwt_ref[0:N_EXPERTS, :], lo)
    logits = a[0:N_EXPERTS] + a[N_EXPERTS:] + b + rb_ref[...]
    eidx = lax.broadcasted_iota(jnp.int32, logits.shape, 0).astype(F32)
    work = logits
    vals, idxs, hots = [], [], []
    for _ in range(TOP_K):
        v = jnp.max(work, axis=0, keepdims=True)
        idx = jnp.min(jnp.where(work == v, eidx, float(N_EXPERTS)), axis=0, keepdims=True)
        hot = eidx == idx
        vals.append(v)
        idxs.append(idx)
        hots.append(hot)
        work = jnp.where(hot, -jnp.inf, work)
    es = [jnp.exp(v - vals[0]) for v in vals]
    den = es[0] + es[1] + es[2] + es[3]
    gates = [e / den for e in es]
    sel = jnp.where(hots[0] | hots[1] | hots[2] | hots[3], 1.0, 0.0)
    run = run_ref[:, 0:1]
    before = _dot(sel.astype(BF16), tri_ref[...]) + run
    run_new = jnp.broadcast_to(run + jnp.sum(sel, axis=1, keepdims=True), run_ref.shape)
    run_ref[...] = run_new
    cnt_ref[...] = run_new
    ranks = [jnp.sum(jnp.where(hots[k], before, 0.0), axis=0, keepdims=True) for k in range(TOP_K)]
    pad = [jnp.zeros((ROUTE_ROWS - 3 * TOP_K, TM), F32)]
    route_ref[...] = jnp.concatenate(gates + idxs + ranks + pad, axis=0)
    gate_ref[...] = jnp.concatenate(gates + [jnp.zeros((LANE - TOP_K, TM), F32)], axis=0).T


def _mod_row(i):
    return jnp.where(i < N_CTX_TILES, 0, 1 + (i - N_CTX_TILES) // LAT_TILES_PER_SEQ)


def _ctx_tile(i):
    return jnp.minimum(i, N_CTX_TILES - 1)


def _lat_tile(i):
    return jnp.maximum(i - N_CTX_TILES, 0)


def _post_attention(mix_ctx, mix_lat, x_ctx, x_lat, x_lat_off, mod_l, lw, consts):
    ins = [mix_ctx, mix_lat, x_ctx, x_lat, mod_l, lw["n2g"], lw["w_out"], lw["rwt"], lw["rb"], consts["tri"]]
    in_specs = [
        pl.BlockSpec((TM, D), lambda i: (_ctx_tile(i), 0)),
        pl.BlockSpec((TM, D), lambda i: (_lat_tile(i), 0)),
        pl.BlockSpec((TM, D), lambda i: (_ctx_tile(i), 0)),
        pl.BlockSpec((TM, D), lambda i: (_lat_tile(i) + x_lat_off, 0)),
        pl.BlockSpec((None, 1, 6 * D), lambda i: (_mod_row(i), 0, 0)),
    ] + [_full(a.shape) for a in ins[5:]]
    return pl.pallas_call(
        _post_kernel,
        grid=(N_TILES,),
        in_specs=in_specs,
        out_specs=[pl.BlockSpec((TM, D), lambda i: (i, 0)), pl.BlockSpec((TM, DW), lambda i: (i, 0)),
                   pl.BlockSpec((TM, LANE), lambda i: (i, 0)), pl.BlockSpec((ROUTE_ROWS, TM), lambda i: (0, i)),
                   pl.BlockSpec((N_EXPERTS, LANE), lambda i: (0, 0))],
        out_shape=[jax.ShapeDtypeStruct((T_ALL, D), F32), jax.ShapeDtypeStruct((T_ALL, DW), jnp.int32),
                   jax.ShapeDtypeStruct((T_ALL, LANE), F32), jax.ShapeDtypeStruct((ROUTE_ROWS, T_ALL), F32),
                   jax.ShapeDtypeStruct((N_EXPERTS, LANE), F32)],
        scratch_shapes=[pltpu.VMEM((N_EXPERTS, LANE), F32)],
        compiler_params=pltpu.CompilerParams(
            dimension_semantics=("arbitrary",), vmem_limit_bytes=VMEM_LIMIT),
        name="post_attn",
    )(*ins)


def _expert_kernel(l, te_ref, nu_ref, nxt_ref, x_ref, wgu_hbm, bgu_ref, wdn_hbm, bdn_ref, y_ref,
                   wgu32, wdn32, wgu16, wdn16, sem):
    i = pl.program_id(0)
    prev = te_ref[jnp.maximum(i - 1, 0)]
    new_expert = jnp.logical_or(i == 0, te_ref[i] != prev)

    def weight_copies(e):
        return (pltpu.make_async_copy(wgu_hbm.at[l, e], wgu32, sem.at[0]),
                pltpu.make_async_copy(wdn_hbm.at[l, e], wdn32, sem.at[1]))

    @pl.when(i == 0)
    def _():
        for cp in weight_copies(te_ref[0]):
            cp.start()

    @pl.when(new_expert)
    def _():
        cps = weight_copies(te_ref[i])
        cps[0].wait()
        wgu16[...] = wgu32[...].astype(BF16)
        cps[1].wait()
        wdn16[...] = wdn32[...].astype(BF16)

        @pl.when(nxt_ref[i] >= 0)
        def _():
            for cp in weight_copies(nxt_ref[i]):
                cp.start()

    @pl.when(i < nu_ref[0])
    def _():
        x = _unpack_rows(x_ref[...])

        def gate_up(a):
            b = a + FF_CHUNK
            g = _dot(x, wgu16[:, a:b]) + bgu_ref[:, a:b]
            up = _dot(x, wgu16[:, D_FF + a:D_FF + b]) + bgu_ref[:, D_FF + a:D_FF + b]
            return g, up

        def activate(g, up):
            g = jnp.minimum(g, SWIGLU_LIMIT)
            up = jnp.clip(up, -SWIGLU_LIMIT, SWIGLU_LIMIT)
            return ((up + 1.0) * (g * jax.nn.sigmoid(SWIGLU_ALPHA * g))).astype(BF16)

        starts = list(range(0, D_FF, FF_CHUNK))
        acc = None
        pre = gate_up(starts[0])
        for n, a in enumerate(starts):
            hid = activate(*pre)
            if n + 1 < len(starts):
                pre = gate_up(starts[n + 1])
            o = _dot(hid, wdn16[a:a + FF_CHUNK, :])
            acc = o if acc is None else acc + o
        y_ref[...] = _pack_rows(acc + bdn_ref[...])


def _experts(l, tile_expert, n_used, next_expert, x_sorted, w_gu, b_gu, w_dn, b_dn):
    def row_map(i, te, nu, nxt):
        return (jnp.minimum(i, nu[0] - 1), 0)

    def b_map(i, te, nu, nxt):
        return (l, te[i], 0, 0)

    grid_spec = pltpu.PrefetchScalarGridSpec(
        num_scalar_prefetch=3,
        grid=(MOE_TILES,),
        in_specs=[
            pl.BlockSpec((MOE_TM, DW), row_map),
            pl.BlockSpec(memory_space=pl.ANY),
            pl.BlockSpec((None, None, 1, 2 * D_FF), b_map),
            pl.BlockSpec(memory_space=pl.ANY),
            pl.BlockSpec((None, None, 1, D), b_map),
        ],
        out_specs=pl.BlockSpec((MOE_TM, DW), row_map),
        scratch_shapes=[pltpu.VMEM((D, 2 * D_FF), F32), pltpu.VMEM((D_FF, D), F32),
                        pltpu.VMEM((D, 2 * D_FF), BF16), pltpu.VMEM((D_FF, D), BF16),
                        pltpu.SemaphoreType.DMA((2,))],
    )
    return pl.pallas_call(
        functools.partial(_expert_kernel, l),
        grid_spec=grid_spec,
        out_shape=jax.ShapeDtypeStruct((MOE_ROWS, DW), jnp.int32),
        compiler_params=pltpu.CompilerParams(
            dimension_semantics=("arbitrary",), vmem_limit_bytes=VMEM_LIMIT),
        name="experts",
    )(tile_expert, n_used, next_expert, x_sorted, w_gu, b_gu.reshape(DEPTH, N_EXPERTS, 1, 2 * D_FF), w_dn,
      b_dn.reshape(DEPTH, N_EXPERTS, 1, D))


def _combine_kernel(split, x_ref, y_ref, route_ref, mod_ref, *o_refs):
    g2 = mod_ref[...][:, 5 * D:6 * D]
    r = route_ref[...]
    acc = r[:, 0:1] * _unpack_rows(y_ref[0]).astype(F32)
    for k in range(1, TOP_K):
        acc = acc + r[:, k:k + 1] * _unpack_rows(y_ref[k]).astype(F32)
    res = x_ref[...] + g2 * acc
    if not split:
        o_refs[0][...] = res
        return
    is_ctx = pl.program_id(0) < N_CTX_TILES

    @pl.when(is_ctx)
    def _():
        o_refs[0][...] = res

    @pl.when(jnp.logical_not(is_ctx))
    def _():
        o_refs[1][...] = res


def _combine(split, x_all, y_tok, route, mod_l):
    if split:
        out_specs = [pl.BlockSpec((TM, D), lambda i: (_ctx_tile(i), 0)),
                     pl.BlockSpec((TM, D), lambda i: (_lat_tile(i), 0))]
        out_shape = [jax.ShapeDtypeStruct((T_CTX, D), F32), jax.ShapeDtypeStruct((T_LAT, D), F32)]
    else:
        out_specs = pl.BlockSpec((TM, D), lambda i: (i, 0))
        out_shape = jax.ShapeDtypeStruct((T_ALL, D), F32)
    return pl.pallas_call(
        functools.partial(_combine_kernel, split),
        grid=(N_TILES,),
        in_specs=[
            pl.BlockSpec((TM, D), lambda i: (i, 0)),
            pl.BlockSpec((TOP_K, TM, DW), lambda i: (0, i, 0)),
            pl.BlockSpec((TM, LANE), lambda i: (i, 0)),
            pl.BlockSpec((None, 1, 6 * D), lambda i: (_mod_row(i), 0, 0)),
        ],
        out_specs=out_specs,
        out_shape=out_shape,
        compiler_params=pltpu.CompilerParams(
            dimension_semantics=("arbitrary",), vmem_limit_bytes=VMEM_LIMIT),
        name="combine_split" if split else "combine",
    )(x_all, y_tok, route, mod_l)


SC_CORES, SC_SUBCORES = 2, 16
SC_WORKERS = SC_CORES * SC_SUBCORES
SC_CHUNK = 128
SC_CHUNKS_PER_WORKER = T_ALL // SC_CHUNK // SC_WORKERS


def _sc_mesh():
    return plsc.VectorSubcoreMesh(core_axis_name="c", subcore_axis_name="s")


def _sc_scratch():
    return [pltpu.VMEM((TOP_K, SC_CHUNK), jnp.int32), pltpu.VMEM((SC_CHUNK, DW), jnp.int32),
            pltpu.SemaphoreType.DMA]


def _dispatch_rows(h2p, dest):
    @functools.partial(pl.kernel, mesh=_sc_mesh(), out_type=jax.ShapeDtypeStruct((MOE_ROWS, DW), jnp.int32),
                       scratch_types=_sc_scratch(), name="dispatch_rows")
    def run(h_hbm, d_hbm, o_hbm, idx_v, rows_v, sem):
        wid = lax.axis_index("s") * SC_CORES + lax.axis_index("c")
        for j in range(SC_CHUNKS_PER_WORKER):
            c = wid * SC_CHUNKS_PER_WORKER + j
            pltpu.sync_copy(d_hbm.at[c], idx_v)
            pltpu.sync_copy(h_hbm.at[pl.ds(c * SC_CHUNK, SC_CHUNK)], rows_v)
            copies = [pltpu.async_copy(rows_v, o_hbm.at[idx_v.at[k]], sem) for k in range(TOP_K)]
            for cp in copies:
                cp.wait()

    return run(h2p, dest)


def _gather_rows(y, dest):
    @functools.partial(pl.kernel, mesh=_sc_mesh(), out_type=jax.ShapeDtypeStruct((TOP_K, T_ALL, DW), jnp.int32),
                       scratch_types=_sc_scratch(), name="gather_rows")
    def run(y_hbm, d_hbm, o_hbm, idx_v, rows_v, sem):
        wid = lax.axis_index("s") * SC_CORES + lax.axis_index("c")
        for j in range(SC_CHUNKS_PER_WORKER):
            c = wid * SC_CHUNKS_PER_WORKER + j
            pltpu.sync_copy(d_hbm.at[c], idx_v)
            for k in range(TOP_K):
                pltpu.async_copy(y_hbm.at[idx_v.at[k]], rows_v, sem).wait()
                pltpu.sync_copy(rows_v, o_hbm.at[k, pl.ds(c * SC_CHUNK, SC_CHUNK)])

    return run(y, dest)


def _constants():
    lane64 = np.arange(4 * HD)
    seg64 = (lane64[:, None] // HD == lane64[None, :] // HD).astype(np.float32)
    lane96 = np.arange(4 * BP)
    real = lane96 % BP < B_QK
    seg96 = ((lane96[:, None] // BP == lane96[None, :] // BP) & real[:, None] & real[None, :]).astype(np.float32)
    place = np.zeros((LANE, 4 * BP), np.float32)
    for hh in range(4):
        place[np.arange(B_ROPE), hh * BP + B_NOPE + np.arange(B_ROPE)] = 1.0

    def angles(rot_dim):
        pos = np.arange(LAT_LEN)
        rows = (pos // GRID_W).astype(np.float32)
        cols = (pos % GRID_W).astype(np.float32)
        axis_dim = rot_dim // 2
        inv = jnp.asarray(ROPE_THETA, F32) ** (-jnp.arange(0, axis_dim, 2, dtype=F32) / axis_dim)
        ang = jnp.concatenate([rows[:, None] * inv, cols[:, None] * inv], axis=-1)
        return jnp.cos(ang), jnp.sin(ang)

    def head_tables(rot_dim):
        cos, sin = angles(rot_dim)
        q = rot_dim // 4
        cr, cc, sr, sc = cos[:, :q], cos[:, q:], sin[:, :q], sin[:, q:]
        return (jnp.concatenate([cr, cr, cc, cc], axis=-1), jnp.concatenate([-sr, sr, -sc, sc], axis=-1))

    c64, s64 = head_tables(HD)
    cos64 = jnp.tile(c64, (1, 4))
    sin64 = jnp.tile(s64, (1, 4))
    c32, s32 = head_tables(B_ROPE)
    ones = jnp.ones((LAT_LEN, B_NOPE), F32)
    zeros = jnp.zeros((LAT_LEN, B_NOPE), F32)
    padz = jnp.zeros((LAT_LEN, BP - B_QK), F32)
    cos96 = jnp.tile(jnp.concatenate([ones, c32, padz], axis=-1), (1, 4))
    sin96 = jnp.tile(jnp.concatenate([zeros, s32, padz], axis=-1), (1, 4))
    first64 = ((lane64 % 32) < 16).astype(np.float32)[None, :]
    first96 = (((lane96 % BP) % 16) < 8).astype(np.float32)[None, :]
    tri = (np.arange(TM)[:, None] < np.arange(TM)[None, :]).astype(np.float32)
    return dict(seg64=jnp.asarray(seg64, BF16), seg96=jnp.asarray(seg96, BF16), place=jnp.asarray(place, BF16),
                tri=jnp.asarray(tri, BF16),
                cos64=cos64, sin64=sin64, cos96=cos96, sin96=sin96,
                first64=jnp.asarray(first64), first96=jnp.asarray(first96))


def _pad_heads(w, per_head, width):
    r = w.shape[0]
    w = w.reshape(r, 4, per_head)
    return jnp.pad(w, ((0, 0), (0, 0), (0, width - per_head))).reshape(r, 4 * width)


def _layer_weights(l, norm1_g, norm2_g, w_in, a_q_g, a_k_g, b_cq_g, b_ckv_g, w_uq, w_ukv, b_q_g, b_k_g,
                   c_q_g, c_k_g, w_out, router_w, router_b):
    wi = w_in[l]
    o = np.cumsum((0, 256, 128, 128, 384, 256, 32, 256, 128, 128))
    seg = lambda k: wi[:, o[k]:o[k + 1]]
    w_in_r = jnp.concatenate([seg(0), seg(1), seg(2), seg(3), seg(4), seg(6), seg(7), seg(8), seg(5),
                              jnp.zeros((D, LANE - B_ROPE), F32)], axis=-1).astype(BF16)
    ukv = w_ukv[l].reshape(B_KV_LORA, 4, B_NOPE + B_V)
    w_uk = jnp.pad(ukv[:, :, :B_NOPE], ((0, 0), (0, 0), (0, BP - B_NOPE))).reshape(B_KV_LORA, 4 * BP)
    w_uv = ukv[:, :, B_NOPE:].reshape(B_KV_LORA, 4 * B_V)
    rw = router_w[l]
    rw_hi = rw.astype(BF16)
    rw_lo = (rw - rw_hi.astype(F32)).astype(BF16)
    row = lambda v: v[None, :]
    return dict(
        n1g=row(norm1_g[l]), n2g=row(norm2_g[l]), w_in=w_in_r,
        aqg=row(jnp.tile(a_q_g[l], 4)), akg=row(jnp.tile(a_k_g[l], 2)),
        cqg=row(jnp.tile(c_q_g[l], 4)), ckg=row(jnp.tile(c_k_g[l], 2)),
        bcqg=row(b_cq_g[l]), bckvg=row(b_ckv_g[l]),
        bqg=_pad_heads(row(jnp.tile(b_q_g[l], 4)), B_QK, BP), bkg=_pad_heads(row(jnp.tile(b_k_g[l], 4)), B_QK, BP),
        w_uq=_pad_heads(w_uq[l], B_QK, BP).astype(BF16), w_uk=w_uk.astype(BF16), w_uv=w_uv.astype(BF16),
        w_out=w_out[l].astype(BF16), rwt=jnp.concatenate([rw_hi.T, rw_lo.T], axis=0), rb=router_b[l][:, None])


def _moe(l, x_all, h2p, gate_slab, route, counts, mod_l, w_gu, b_gu, w_dn, b_dn):
    counts = counts[:, 0].astype(jnp.int32)
    padded = (counts + MOE_TM - 1) // MOE_TM * MOE_TM
    pend = jnp.cumsum(padded)
    pstart = pend - padded
    n_used = (pend[-1] // MOE_TM).astype(jnp.int32)
    tiles = jnp.minimum(jnp.arange(MOE_TILES, dtype=jnp.int32), n_used - 1)
    tile_expert = jnp.sum((pend[None, :] <= tiles[:, None] * MOE_TM).astype(jnp.int32), axis=1)
    tile_expert = jnp.minimum(tile_expert, N_EXPERTS - 1)
    group_end_tile = jnp.sum(jnp.where(tile_expert[:, None] == jnp.arange(N_EXPERTS)[None, :], pend[None, :], 0),
                             axis=1) // MOE_TM
    after = jnp.sum((pend[None, :] <= group_end_tile[:, None] * MOE_TM).astype(jnp.int32), axis=1)
    next_expert = jnp.where(group_end_tile < n_used, jnp.minimum(after, N_EXPERTS - 1), -1).astype(jnp.int32)
    e = route[TOP_K:2 * TOP_K].astype(jnp.int32)
    rank = route[2 * TOP_K:3 * TOP_K].astype(jnp.int32)
    start = jnp.sum(jnp.where(e[:, :, None] == jnp.arange(N_EXPERTS)[None, None, :], pstart[None, None, :], 0), axis=-1)
    dest = (start + rank).reshape(TOP_K, T_ALL // SC_CHUNK, SC_CHUNK).transpose(1, 0, 2)
    x_sorted = _dispatch_rows(h2p, dest)
    y = _experts(l, tile_expert, n_used.reshape(1), next_expert, x_sorted, w_gu, b_gu, w_dn, b_dn)
    y_tok = _gather_rows(y, dest)
    return _combine(l == DEPTH - 1, x_all, y_tok, gate_slab, mod_l)


def kernel(x_prompt, x_sample, cache_a_k, cache_a_v, cache_b_ckv, cache_b_krope, cache_c_k, cache_c_v, c, c_ctx,
           norm1_g, norm2_g, w_mod, b_mod, w_in, a_q_g, a_k_g, a_sink, b_cq_g, b_ckv_g, w_uq, w_ukv, b_q_g, b_k_g,
           c_q_g, c_k_g, w_out, router_w, router_b, w_gu, b_gu, w_dn, b_dn):
    consts = _constants()
    cond = jnp.concatenate([c_ctx[None, :], c, jnp.zeros((3, D), F32)], axis=0)
    mod = _modulation(cond, w_mod, b_mod).reshape(DEPTH, 8, 1, 6 * D)
    x_ctx, x_lat, x_lat_off = x_prompt.reshape(T_CTX, D), x_sample.reshape(T_LAT, D), 0

    states = ()
    for l in range(DEPTH):
        lw = _layer_weights(l, norm1_g, norm2_g, w_in, a_q_g, a_k_g, b_cq_g, b_ckv_g, w_uq, w_ukv, b_q_g, b_k_g,
                            c_q_g, c_k_g, w_out, router_w, router_b)
        mod_l = mod[l]
        names = ["qa", "ka", "va", "qb", "kb", "vb", "qc", "kc", "vc"]
        outs = _projection(False, l, x_ctx, 0, mod_l, lw, consts, states)
        p_ctx = dict(zip(names, outs[:9]))
        states = tuple(outs[9:])
        p_lat = dict(zip(names, _projection(True, l, x_lat, x_lat_off, mod_l, lw, consts)))

        flat = lambda a, w: a[:, l].reshape(N_LAT_SEQ * PAST, w)
        ckb, cvb = _cache_kv(flat(cache_b_ckv, B_KV_LORA),
                             jnp.pad(flat(cache_b_krope, B_ROPE), ((0, 0), (0, LANE - B_ROPE))), lw, consts)
        cache = dict(ka=flat(cache_a_k, 2 * HD).astype(BF16), va=flat(cache_a_v, 2 * HD).astype(BF16),
                     kb=ckb, vb=cvb,
                     kc=flat(cache_c_k, 2 * HD).astype(BF16), vc=flat(cache_c_v, 2 * HD).astype(BF16))
        mix_ctx = _attention_ctx(a_sink[l], p_ctx)
        mix_lat = _attention_lat(a_sink[l], p_lat, cache)
        x_mid, h2p, gate_slab, route, counts = _post_attention(mix_ctx, mix_lat, x_ctx, x_lat, x_lat_off, mod_l, lw,
                                                               consts)
        out = _moe(l, x_mid, h2p, gate_slab, route, counts, mod_l, w_gu, b_gu, w_dn, b_dn)
        if l < DEPTH - 1:
            x_ctx, x_lat, x_lat_off = out, out, N_CTX_TILES
    y_ctx, y_lat = out

    kv_shape = (N_CTX_SEQ, DEPTH, CTX_LEN, 2, HD)
    return (y_ctx.reshape(N_CTX_SEQ, CTX_LEN, D), y_lat.reshape(N_LAT_SEQ, LAT_LEN, D),
            states[0].reshape(kv_shape), states[1].reshape(kv_shape),
            states[2].reshape(N_CTX_SEQ, DEPTH, CTX_LEN, B_KV_LORA),
            states[3].reshape(N_CTX_SEQ, DEPTH, CTX_LEN, B_ROPE),
            states[4].reshape(kv_shape), states[5].reshape(kv_shape))
```

```python
import functools

import jax
import jax.numpy as jnp
import numpy as np
from jax import lax
from jax.experimental import pallas as pl
from jax.experimental.pallas import tpu as pltpu
from jax.experimental.pallas import tpu_sc as plsc

F32 = jnp.float32
BF16 = jnp.bfloat16

D = 1024
DEPTH = 2
N_CTX_SEQ, CTX_LEN = 16, 256
N_LAT_SEQ, LAT_LEN = 4, 2048
PAST = 512
T_CTX = N_CTX_SEQ * CTX_LEN
T_LAT = N_LAT_SEQ * LAT_LEN
T_ALL = T_CTX + T_LAT
GRID_W = 64
HD = 64
WINDOW = 128
B_NOPE, B_ROPE, B_V = 64, 32, 128
B_QK = B_NOPE + B_ROPE
B_Q_LORA, B_KV_LORA = 384, 256
N_EXPERTS, TOP_K = 32, 4
D_FF = 1024
SWIGLU_LIMIT = 7.0
SWIGLU_ALPHA = 1.702
ROPE_THETA = 10000.0
EPS = 1e-6
NEG_INF = -1e30

TM = 256
LANE = 128
BP = 128
N_CTX_TILES = T_CTX // TM
N_LAT_TILES = T_LAT // TM
N_TILES = T_ALL // TM
LAT_TILES_PER_SEQ = LAT_LEN // TM
MOE_TM = 256
FF_CHUNK = 256
MOE_TILES = T_ALL * TOP_K // MOE_TM + N_EXPERTS
MOE_ROWS = MOE_TILES * MOE_TM
DW = D // 2
VMEM_LIMIT = 56 * 1024 * 1024

C_QA, C_KA, C_VA, C_CQ, C_CKV, C_QC, C_KC, C_VC, C_KR, C_END = (
    0, 256, 384, 512, 896, 1152, 1408, 1536, 1664, 1792)


def _dot(a, b):
    return jnp.dot(a, b, preferred_element_type=F32)


def _dot_t(a, b):
    return lax.dot_general(a, b, (((1,), (1,)), ((), ())), preferred_element_type=F32)


def _rms(x, g):
    return x * lax.rsqrt(jnp.mean(x * x, axis=-1, keepdims=True) + EPS) * g


def _seg_norm(x, seg, g, n):
    ss = _dot((x * x).astype(BF16), seg)
    return x * lax.rsqrt(ss * (1.0 / n) + EPS) * g


def _rope(x, cos, sin, first, sh):
    w = x.shape[1]
    fwd = pltpu.roll(x, w - sh, 1)
    bwd = pltpu.roll(x, sh, 1)
    return x * cos + jnp.where(first > 0.5, fwd, bwd) * sin


MOD_BN = 1536


def _mod_kernel(c_ref, w_ref, b_ref, o_ref):
    c = c_ref[...]
    s = (c * jax.nn.sigmoid(c)).astype(BF16)
    o_ref[...] = _dot(s, w_ref[...].astype(BF16)) + b_ref[...]


def _modulation(cond, w_mod, b_mod):
    return pl.pallas_call(
        _mod_kernel,
        grid=(DEPTH, 6 * D // MOD_BN),
        in_specs=[
            pl.BlockSpec((8, D), lambda l, j: (0, 0)),
            pl.BlockSpec((None, D, MOD_BN), lambda l, j: (l, 0, j)),
            pl.BlockSpec((None, 1, MOD_BN), lambda l, j: (l, 0, j)),
        ],
        out_specs=pl.BlockSpec((None, 8, MOD_BN), lambda l, j: (l, 0, j)),
        out_shape=jax.ShapeDtypeStruct((DEPTH, 8, 6 * D), F32),
        compiler_params=pltpu.CompilerParams(
            dimension_semantics=("arbitrary", "arbitrary"), vmem_limit_bytes=VMEM_LIMIT),
        name="modulation",
    )(cond, w_mod, b_mod.reshape(DEPTH, 1, 6 * D))


def _proj_kernel(is_lat, n_aliased, *refs):
    (x_ref, mod_ref, n1g_ref, win_ref, seg64_ref, seg96_ref, aqg_ref, akg_ref, cqg_ref, ckg_ref,
     bcqg_ref, bckvg_ref, bqg_ref, bkg_ref, wuq_ref, wuk_ref, wuv_ref, plc_ref) = refs[:18]
    refs = refs[18 + n_aliased:]
    if is_lat:
        (cos64_ref, sin64_ref, cos96_ref, sin96_ref, f64_ref, f96_ref) = refs[:6]
        refs = refs[6:]
    (qa_ref, ka_ref, va_ref, qb_ref, kb_ref, vb_ref, qc_ref, kc_ref, vc_ref) = refs[:9]
    refs = refs[9:]
    if not is_lat:
        (kas_ref, vas_ref, ckvs_ref, krs_ref, kcs_ref, vcs_ref) = refs

    x = x_ref[...]
    m = mod_ref[...]
    sh1, sc1 = m[:, 0:D], m[:, D:2 * D]
    h = (_rms(x, n1g_ref[...]) * (1.0 + sc1) + sh1).astype(BF16)

    def proj(a, b):
        return _dot(h, win_ref[:, a:b])

    def rope64(t):
        wd = t.shape[1]
        return _rope(t, cos64_ref[:, :wd], sin64_ref[:, :wd], f64_ref[:, :wd], 16) if is_lat else t

    def rope96(t):
        return _rope(t, cos96_ref[...], sin96_ref[...], f96_ref[...], 8) if is_lat else t

    seg64 = seg64_ref[...]
    seg64h = seg64_ref[0:2 * HD, 0:2 * HD]
    seg96 = seg96_ref[...]

    def seg_sum(t, seg):
        return _dot((t * t).astype(BF16), seg)

    def seg_finish(t, ss, g, n):
        return t * lax.rsqrt(ss * (1.0 / n) + EPS) * g

    p_cq, p_ckv, p_kr = proj(C_CQ, C_CKV), proj(C_CKV, C_QC), proj(C_KR, C_END)
    p_qa, p_ka, p_qc, p_kc = proj(C_QA, C_KA), proj(C_KA, C_VA), proj(C_QC, C_KC), proj(C_KC, C_VC)
    p_va, p_vc = proj(C_VA, C_CQ), proj(C_VC, C_KR)

    cq = _rms(p_cq, bcqg_ref[...]).astype(BF16)
    ckv = _rms(p_ckv, bckvg_ref[...])
    ckv16 = ckv.astype(BF16)
    u_q = _dot(cq, wuq_ref[...])
    u_k = _dot(ckv16, wuk_ref[...]) + _dot(p_kr.astype(BF16), plc_ref[...])
    u_v = _dot(ckv16, wuv_ref[...])
    ss_qa, ss_ka = seg_sum(p_qa, seg64), seg_sum(p_ka, seg64h)
    ss_qc, ss_kc = seg_sum(p_qc, seg64), seg_sum(p_kc, seg64h)
    ss_qb, ss_kb = seg_sum(u_q, seg96), seg_sum(u_k, seg96)

    def store_kv_state(ref, t):
        for kv in range(2):
            ref[:, kv, :] = t[:, kv * HD:(kv + 1) * HD]

    if not is_lat:
        store_kv_state(vas_ref, p_va)
        store_kv_state(vcs_ref, p_vc)
        ckvs_ref[...] = ckv
        krs_ref[...] = p_kr[:, 0:B_ROPE]
    va_ref[...] = p_va.astype(BF16)
    vc_ref[...] = p_vc.astype(BF16)
    vb_ref[...] = u_v.astype(BF16)

    t = seg_finish(p_qa, ss_qa, aqg_ref[...], HD)
    qa_ref[...] = (rope64(t) * HD ** -0.5).astype(BF16)
    t = seg_finish(p_ka, ss_ka, akg_ref[...], HD)
    if not is_lat:
        store_kv_state(kas_ref, t)
    ka_ref[...] = rope64(t).astype(BF16)
    t = seg_finish(p_qc, ss_qc, cqg_ref[...], HD)
    qc_ref[...] = (rope64(t) * HD ** -0.5).astype(BF16)
    t = seg_finish(p_kc, ss_kc, ckg_ref[...], HD)
    if not is_lat:
        store_kv_state(kcs_ref, t)
    kc_ref[...] = rope64(t).astype(BF16)
    t = seg_finish(u_q, ss_qb, bqg_ref[...], B_QK)
    qb_ref[...] = (rope96(t) * B_QK ** -0.5).astype(BF16)
    t = seg_finish(u_k, ss_kb, bkg_ref[...], B_QK)
    kb_ref[...] = rope96(t).astype(BF16)


def _full(shape):
    nd = len(shape)
    return pl.BlockSpec(shape, lambda i: (0,) * nd)


STATE_TAILS = ((2, HD), (2, HD), (B_KV_LORA,), (B_ROPE,), (2, HD), (2, HD))


def _projection(is_lat, l, x_src, x_off, mod_l, lw, consts, prev_states=()):
    n_tiles = N_LAT_TILES if is_lat else N_CTX_TILES
    t_rows = n_tiles * TM
    if is_lat:
        mod_map = lambda i: (1 + i // LAT_TILES_PER_SEQ, 0, 0)
    else:
        mod_map = lambda i: (0, 0, 0)
    ins = [x_src, mod_l, lw["n1g"], lw["w_in"], consts["seg64"], consts["seg96"], lw["aqg"], lw["akg"],
           lw["cqg"], lw["ckg"], lw["bcqg"], lw["bckvg"], lw["bqg"], lw["bkg"], lw["w_uq"], lw["w_uk"],
           lw["w_uv"], consts["place"]]
    in_specs = [pl.BlockSpec((TM, D), lambda i: (i + x_off, 0)),
                pl.BlockSpec((None, 1, 6 * D), mod_map)]
    in_specs += [_full(a.shape) for a in ins[2:]]
    n_plain = len(ins)
    ins += list(prev_states)
    in_specs += [pl.BlockSpec(memory_space=pl.ANY) for _ in prev_states]
    if is_lat:
        tabs = [consts["cos64"], consts["sin64"], consts["cos96"], consts["sin96"]]
        ins += tabs + [consts["first64"], consts["first96"]]
        in_specs += [pl.BlockSpec((TM, a.shape[1]), lambda i: (i % LAT_TILES_PER_SEQ, 0)) for a in tabs]
        in_specs += [_full(consts["first64"].shape), _full(consts["first96"].shape)]
    widths = [4 * HD, 2 * HD, 2 * HD, 4 * BP, 4 * BP, 4 * B_V, 4 * HD, 2 * HD, 2 * HD]
    out_shape = [jax.ShapeDtypeStruct((t_rows, w), BF16) for w in widths]
    out_specs = [pl.BlockSpec((TM, w), lambda i: (i, 0)) for w in widths]
    if not is_lat:
        for tail in STATE_TAILS:
            zeros = (0,) * len(tail)
            out_shape.append(jax.ShapeDtypeStruct((N_CTX_SEQ, DEPTH, CTX_LEN) + tail, F32))
            out_specs.append(pl.BlockSpec((None, None, CTX_LEN) + tail, lambda i, z=zeros: (i, l, 0) + z))
    aliases = {n_plain + k: len(widths) + k for k in range(len(prev_states))}
    return pl.pallas_call(
        functools.partial(_proj_kernel, is_lat, len(prev_states)),
        grid=(n_tiles,),
        in_specs=in_specs,
        out_specs=out_specs,
        out_shape=out_shape,
        input_output_aliases=aliases,
        compiler_params=pltpu.CompilerParams(
            dimension_semantics=("arbitrary",), vmem_limit_bytes=VMEM_LIMIT),
        name="proj_lat" if is_lat else "proj_ctx",
    )(*ins)


def _cache_kv_kernel(ckv_ref, kr_ref, seg96_ref, bkg_ref, wuk_ref, wuv_ref, plc_ref, kb_ref, vb_ref):
    ckv16 = ckv_ref[...].astype(BF16)
    vb_ref[...] = _dot(ckv16, wuv_ref[...]).astype(BF16)
    kpre = _dot(ckv16, wuk_ref[...]) + _dot(kr_ref[...].astype(BF16), plc_ref[...])
    kb_ref[...] = _seg_norm(kpre, seg96_ref[...], bkg_ref[...], B_QK).astype(BF16)


def _cache_kv(c_ckv, c_kr, lw, consts):
    rows = c_ckv.shape[0]
    ins = [c_ckv, c_kr, consts["seg96"], lw["bkg"], lw["w_uk"], lw["w_uv"], consts["place"]]
    in_specs = [pl.BlockSpec((TM, B_KV_LORA), lambda i: (i, 0)), pl.BlockSpec((TM, LANE), lambda i: (i, 0))]
    in_specs += [_full(a.shape) for a in ins[2:]]
    return pl.pallas_call(
        _cache_kv_kernel,
        grid=(rows // TM,),
        in_specs=in_specs,
        out_specs=[pl.BlockSpec((TM, 4 * BP), lambda i: (i, 0)), pl.BlockSpec((TM, 4 * B_V), lambda i: (i, 0))],
        out_shape=[jax.ShapeDtypeStruct((rows, 4 * BP), BF16), jax.ShapeDtypeStruct((rows, 4 * B_V), BF16)],
        compiler_params=pltpu.CompilerParams(
            dimension_semantics=("arbitrary",), vmem_limit_bytes=VMEM_LIMIT),
        name="cache_kv",
    )(*ins)


def _scores(q, segs):
    scores = []
    for k, _, mask in segs:
        s = _dot_t(q, k)
        if mask is not None:
            s = jnp.where(mask, s, NEG_INF)
        scores.append(s)
    return scores


def _softmax_pv(scores, segs, sink):
    m = None
    for s in scores:
        sm = jnp.max(s, axis=-1, keepdims=True)
        m = sm if m is None else jnp.maximum(m, sm)
    if sink is not None:
        m = jnp.maximum(m, sink)
    acc = None
    den = None
    for s, (_, v, _) in zip(scores, segs):
        e = jnp.exp(s - m)
        d = jnp.sum(e, axis=-1, keepdims=True)
        o = _dot(e.astype(BF16), v)
        acc = o if acc is None else acc + o
        den = d if den is None else den + d
    if sink is not None:
        den = den + jnp.exp(sink - m)
    return acc / den


def _run_heads(jobs, o_ref):
    nxt = _scores(jobs[0][0](), jobs[0][1]())
    for n, (_, load_segs, sink, col) in enumerate(jobs):
        scores = nxt
        if n + 1 < len(jobs):
            nxt = _scores(jobs[n + 1][0](), jobs[n + 1][1]())
        o = _softmax_pv(scores, load_segs(), sink)
        o_ref[:, col:col + o.shape[1]] = o.astype(BF16)


OUT_B = 4 * HD
OUT_C = 4 * HD + 4 * B_V


def _head_job(q_ref, q_cols, seg_refs, kv_cols, v_cols, sink, col, rows=None, masks=None):
    def load_segs():
        segs = []
        for n, (k_ref, v_ref) in enumerate(seg_refs):
            r = rows if (rows is not None and n == 0) else slice(None)
            segs.append((k_ref[r, kv_cols], v_ref[r, v_cols], None if masks is None else masks[n]))
        return segs

    return (lambda: q_ref[:, q_cols]), load_segs, sink, col


def _attn_ctx_kernel(sink_ref, qa_ref, ka_ref, va_ref, qb_ref, kb_ref, vb_ref, qc_ref, kc_ref, vc_ref, o_ref):
    jobs = []
    for h in range(4):
        cs = slice(h // 2 * HD, (h // 2 + 1) * HD)
        jobs.append(_head_job(qa_ref, slice(h * HD, (h + 1) * HD), [(ka_ref, va_ref)], cs, cs, sink_ref[h], h * HD))
    for h in range(4):
        ks, vs = slice(h * BP, (h + 1) * BP), slice(h * B_V, (h + 1) * B_V)
        jobs.append(_head_job(qb_ref, ks, [(kb_ref, vb_ref)], ks, vs, None, OUT_B + h * B_V))
    for h in range(4):
        cs = slice(h // 2 * HD, (h // 2 + 1) * HD)
        jobs.append(_head_job(qc_ref, slice(h * HD, (h + 1) * HD), [(kc_ref, vc_ref)], cs, cs, None, OUT_C + h * HD))
    _run_heads(jobs, o_ref)


def _attention_ctx(sink, p):
    names = ["qa", "ka", "va", "qb", "kb", "vb", "qc", "kc", "vc"]
    ins = [p[n] for n in names]
    in_specs = [pl.BlockSpec(memory_space=pltpu.SMEM)]
    in_specs += [pl.BlockSpec((CTX_LEN, a.shape[1]), lambda i: (i, 0)) for a in ins]
    return pl.pallas_call(
        _attn_ctx_kernel,
        grid=(N_CTX_SEQ,),
        in_specs=in_specs,
        out_specs=pl.BlockSpec((CTX_LEN, D), lambda i: (i, 0)),
        out_shape=jax.ShapeDtypeStruct((T_CTX, D), BF16),
        compiler_params=pltpu.CompilerParams(
            dimension_semantics=("arbitrary",), vmem_limit_bytes=VMEM_LIMIT),
        name="attn_ctx",
    )(sink, *ins)


WIN_SPAN = TM + 2 * WINDOW


def _attn_lat_kernel(sink_ref, qa_ref, qb_ref, qc_ref, ka_ref, va_ref, kb_ref, vb_ref, kc_ref, vc_ref,
                     cka_ref, cva_ref, ckb_ref, cvb_ref, ckc_ref, cvc_ref, o_ref):
    qi = pl.program_id(1)
    ws = pl.multiple_of(jnp.clip(qi * TM - WINDOW, 0, LAT_LEN - WIN_SPAN), WINDOW)
    qpos = qi * TM + lax.broadcasted_iota(jnp.int32, (TM, WIN_SPAN), 0)
    kpos = ws + lax.broadcasted_iota(jnp.int32, (TM, WIN_SPAN), 1)
    band = jnp.abs(qpos - kpos) <= WINDOW
    jobs = []
    for h in range(4):
        cs = slice(h // 2 * HD, (h // 2 + 1) * HD)
        jobs.append(_head_job(qa_ref, slice(h * HD, (h + 1) * HD), [(ka_ref, va_ref), (cka_ref, cva_ref)], cs, cs,
                              sink_ref[h], h * HD, rows=pl.ds(ws, WIN_SPAN), masks=(band, None)))
    for h in range(4):
        ks, vs = slice(h * BP, (h + 1) * BP), slice(h * B_V, (h + 1) * B_V)
        jobs.append(_head_job(qb_ref, ks, [(kb_ref, vb_ref), (ckb_ref, cvb_ref)], ks, vs, None, OUT_B + h * B_V))
    for h in range(4):
        cs = slice(h // 2 * HD, (h // 2 + 1) * HD)
        jobs.append(_head_job(qc_ref, slice(h * HD, (h + 1) * HD), [(kc_ref, vc_ref), (ckc_ref, cvc_ref)], cs, cs,
                              None, OUT_C + h * HD))
    _run_heads(jobs, o_ref)


def _attention_lat(sink, p, cache):
    q_ins = [p["qa"], p["qb"], p["qc"]]
    kv_ins = [p[n] for n in ("ka", "va", "kb", "vb", "kc", "vc")]
    c_ins = [cache[n] for n in ("ka", "va", "kb", "vb", "kc", "vc")]
    in_specs = [pl.BlockSpec(memory_space=pltpu.SMEM)]
    in_specs += [pl.BlockSpec((TM, a.shape[1]), lambda b, i: (b * LAT_TILES_PER_SEQ + i, 0)) for a in q_ins]
    in_specs += [pl.BlockSpec((LAT_LEN, a.shape[1]), lambda b, i: (b, 0)) for a in kv_ins]
    in_specs += [pl.BlockSpec((PAST, a.shape[1]), lambda b, i: (b, 0)) for a in c_ins]
    return pl.pallas_call(
        _attn_lat_kernel,
        grid=(N_LAT_SEQ, LAT_TILES_PER_SEQ),
        in_specs=in_specs,
        out_specs=pl.BlockSpec((TM, D), lambda b, i: (b * LAT_TILES_PER_SEQ + i, 0)),
        out_shape=jax.ShapeDtypeStruct((T_LAT, D), BF16),
        compiler_params=pltpu.CompilerParams(
            dimension_semantics=("arbitrary", "arbitrary"), vmem_limit_bytes=VMEM_LIMIT),
        name="attn_lat",
    )(sink, *q_ins, *kv_ins, *c_ins)


def _pack_rows(x):
    half = x.shape[1] // 2
    r = x.astype(BF16).astype(F32)
    hi = lax.bitcast_convert_type(r[:, :half], jnp.int32)
    lo = lax.bitcast_convert_type(r[:, half:], jnp.int32)
    return jnp.bitwise_or(hi, lax.shift_right_logical(lo, 16))


def _unpack_rows(p):
    a = lax.bitcast_convert_type(jnp.bitwise_and(p, -65536), F32)
    b = lax.bitcast_convert_type(lax.shift_left(p, 16), F32)
    return jnp.concatenate([a, b], axis=1).astype(BF16)


ROUTE_ROWS = 16


def _post_kernel(mc_ref, ml_ref, xc_ref, xl_ref, mod_ref, n2g_ref, wout_ref, rwt_ref, rb_ref, tri_ref,
                 xo_ref, h2_ref, gate_ref, route_ref, cnt_ref, run_ref):
    i = pl.program_id(0)

    @pl.when(i == 0)
    def _():
        run_ref[...] = jnp.zeros_like(run_ref)
        cnt_ref[...] = jnp.zeros_like(cnt_ref)

    is_ctx = i < N_CTX_TILES
    mix = jnp.where(is_ctx, mc_ref[...], ml_ref[...])
    m = mod_ref[...]
    g1, sh2, sc2 = m[:, 2 * D:3 * D], m[:, 3 * D:4 * D], m[:, 4 * D:5 * D]
    x = jnp.where(is_ctx, xc_ref[...], xl_ref[...]) + g1 * _dot(mix, wout_ref[...])
    xo_ref[...] = x
    h2 = _rms(x, n2g_ref[...]) * (1.0 + sc2) + sh2
    hi = h2.astype(BF16)
    h2_ref[...] = _pack_rows(h2)
    lo = (h2 - hi.astype(F32)).astype(BF16)
    a = _dot_t(rwt_ref[...], hi)
    b = _dot_t(rwt_ref[0:N_EXPERTS, :], lo)
    logits = a[0:N_EXPERTS] + a[N_EXPERTS:] + b + rb_ref[...]
    eidx = lax.broadcasted_iota(jnp.int32, logits.shape, 0).astype(F32)
    work = logits
    vals, idxs, hots = [], [], []
    for _ in range(TOP_K):
        v = jnp.max(work, axis=0, keepdims=True)
        idx = jnp.min(jnp.where(work == v, eidx, float(N_EXPERTS)), axis=0, keepdims=True)
        hot = eidx == idx
        vals.append(v)
        idxs.append(idx)
        hots.append(hot)
        work = jnp.where(hot, -jnp.inf, work)
    es = [jnp.exp(v - vals[0]) for v in vals]
    den = es[0] + es[1] + es[2] + es[3]
    gates = [e / den for e in es]
    sel = jnp.where(hots[0] | hots[1] | hots[2] | hots[3], 1.0, 0.0)
    run = run_ref[:, 0:1]
    before = _dot(sel.astype(BF16), tri_ref[...]) + run
    run_new = jnp.broadcast_to(run + jnp.sum(sel, axis=1, keepdims=True), run_ref.shape)
    run_ref[...] = run_new
    cnt_ref[...] = run_new
    ranks = [jnp.sum(jnp.where(hots[k], before, 0.0), axis=0, keepdims=True) for k in range(TOP_K)]
    pad = [jnp.zeros((ROUTE_ROWS - 3 * TOP_K, TM), F32)]
    route_ref[...] = jnp.concatenate(gates + idxs + ranks + pad, axis=0)
    gate_ref[...] = jnp.concatenate(gates + [jnp.zeros((LANE - TOP_K, TM), F32)], axis=0).T


def _mod_row(i):
    return jnp.where(i < N_CTX_TILES, 0, 1 + (i - N_CTX_TILES) // LAT_TILES_PER_SEQ)


def _ctx_tile(i):
    return jnp.minimum(i, N_CTX_TILES - 1)


def _lat_tile(i):
    return jnp.maximum(i - N_CTX_TILES, 0)


def _post_attention(mix_ctx, mix_lat, x_ctx, x_lat, x_lat_off, mod_l, lw, consts):
    ins = [mix_ctx, mix_lat, x_ctx, x_lat, mod_l, lw["n2g"], lw["w_out"], lw["rwt"], lw["rb"], consts["tri"]]
    in_specs = [
        pl.BlockSpec((TM, D), lambda i: (_ctx_tile(i), 0)),
        pl.BlockSpec((TM, D), lambda i: (_lat_tile(i), 0)),
        pl.BlockSpec((TM, D), lambda i: (_ctx_tile(i), 0)),
        pl.BlockSpec((TM, D), lambda i: (_lat_tile(i) + x_lat_off, 0)),
        pl.BlockSpec((None, 1, 6 * D), lambda i: (_mod_row(i), 0, 0)),
    ] + [_full(a.shape) for a in ins[5:]]
    return pl.pallas_call(
        _post_kernel,
        grid=(N_TILES,),
        in_specs=in_specs,
        out_specs=[pl.BlockSpec((TM, D), lambda i: (i, 0)), pl.BlockSpec((TM, DW), lambda i: (i, 0)),
                   pl.BlockSpec((TM, LANE), lambda i: (i, 0)), pl.BlockSpec((ROUTE_ROWS, TM), lambda i: (0, i)),
                   pl.BlockSpec((N_EXPERTS, LANE), lambda i: (0, 0))],
        out_shape=[jax.ShapeDtypeStruct((T_ALL, D), F32), jax.ShapeDtypeStruct((T_ALL, DW), jnp.int32),
                   jax.ShapeDtypeStruct((T_ALL, LANE), F32), jax.ShapeDtypeStruct((ROUTE_ROWS, T_ALL), F32),
                   jax.ShapeDtypeStruct((N_EXPERTS, LANE), F32)],
        scratch_shapes=[pltpu.VMEM((N_EXPERTS, LANE), F32)],
        compiler_params=pltpu.CompilerParams(
            dimension_semantics=("arbitrary",), vmem_limit_bytes=VMEM_LIMIT),
        name="post_attn",
    )(*ins)


def _expert_kernel(l, te_ref, nu_ref, nxt_ref, x_ref, wgu_hbm, bgu_ref, wdn_hbm, bdn_ref, y_ref,
                   wgu32, wdn32, wgu16, wdn16, sem):
    i = pl.program_id(0)
    prev = te_ref[jnp.maximum(i - 1, 0)]
    new_expert = jnp.logical_or(i == 0, te_ref[i] != prev)

    def weight_copies(e):
        return (pltpu.make_async_copy(wgu_hbm.at[l, e], wgu32, sem.at[0]),
                pltpu.make_async_copy(wdn_hbm.at[l, e], wdn32, sem.at[1]))

    @pl.when(i == 0)
    def _():
        for cp in weight_copies(te_ref[0]):
            cp.start()

    @pl.when(new_expert)
    def _():
        cps = weight_copies(te_ref[i])
        cps[0].wait()
        wgu16[...] = wgu32[...].astype(BF16)
        cps[1].wait()
        wdn16[...] = wdn32[...].astype(BF16)

        @pl.when(nxt_ref[i] >= 0)
        def _():
            for cp in weight_copies(nxt_ref[i]):
                cp.start()

    @pl.when(i < nu_ref[0])
    def _():
        x = _unpack_rows(x_ref[...])

        def gate_up(a):
            b = a + FF_CHUNK
            g = _dot(x, wgu16[:, a:b]) + bgu_ref[:, a:b]
            up = _dot(x, wgu16[:, D_FF + a:D_FF + b]) + bgu_ref[:, D_FF + a:D_FF + b]
            return g, up

        def activate(g, up):
            g = jnp.minimum(g, SWIGLU_LIMIT)
            up = jnp.clip(up, -SWIGLU_LIMIT, SWIGLU_LIMIT)
            return ((up + 1.0) * (g * jax.nn.sigmoid(SWIGLU_ALPHA * g))).astype(BF16)

        starts = list(range(0, D_FF, FF_CHUNK))
        acc = None
        pre = gate_up(starts[0])
        for n, a in enumerate(starts):
            hid = activate(*pre)
            if n + 1 < len(starts):
                pre = gate_up(starts[n + 1])
            o = _dot(hid, wdn16[a:a + FF_CHUNK, :])
            acc = o if acc is None else acc + o
        y_ref[...] = _pack_rows(acc + bdn_ref[...])


def _experts(l, tile_expert, n_used, next_expert, x_sorted, w_gu, b_gu, w_dn, b_dn):
    def row_map(i, te, nu, nxt):
        return (jnp.minimum(i, nu[0] - 1), 0)

    def b_map(i, te, nu, nxt):
        return (l, te[i], 0, 0)

    grid_spec = pltpu.PrefetchScalarGridSpec(
        num_scalar_prefetch=3,
        grid=(MOE_TILES,),
        in_specs=[
            pl.BlockSpec((MOE_TM, DW), row_map),
            pl.BlockSpec(memory_space=pl.ANY),
            pl.BlockSpec((None, None, 1, 2 * D_FF), b_map),
            pl.BlockSpec(memory_space=pl.ANY),
            pl.BlockSpec((None, None, 1, D), b_map),
        ],
        out_specs=pl.BlockSpec((MOE_TM, DW), row_map),
        scratch_shapes=[pltpu.VMEM((D, 2 * D_FF), F32), pltpu.VMEM((D_FF, D), F32),
                        pltpu.VMEM((D, 2 * D_FF), BF16), pltpu.VMEM((D_FF, D), BF16),
                        pltpu.SemaphoreType.DMA((2,))],
    )
    return pl.pallas_call(
        functools.partial(_expert_kernel, l),
        grid_spec=grid_spec,
        out_shape=jax.ShapeDtypeStruct((MOE_ROWS, DW), jnp.int32),
        compiler_params=pltpu.CompilerParams(
            dimension_semantics=("arbitrary",), vmem_limit_bytes=VMEM_LIMIT),
        name="experts",
    )(tile_expert, n_used, next_expert, x_sorted, w_gu, b_gu.reshape(DEPTH, N_EXPERTS, 1, 2 * D_FF), w_dn,
      b_dn.reshape(DEPTH, N_EXPERTS, 1, D))


def _combine_kernel(split, x_ref, y_ref, route_ref, mod_ref, *o_refs):
    g2 = mod_ref[...][:, 5 * D:6 * D]
    r = route_ref[...]
    acc = r[:, 0:1] * _unpack_rows(y_ref[0]).astype(F32)
    for k in range(1, TOP_K):
        acc = acc + r[:, k:k + 1] * _unpack_rows(y_ref[k]).astype(F32)
    res = x_ref[...] + g2 * acc
    if not split:
        o_refs[0][...] = res
        return
    is_ctx = pl.program_id(0) < N_CTX_TILES

    @pl.when(is_ctx)
    def _():
        o_refs[0][...] = res

    @pl.when(jnp.logical_not(is_ctx))
    def _():
        o_refs[1][...] = res


def _combine(split, x_all, y_tok, route, mod_l):
    if split:
        out_specs = [pl.BlockSpec((TM, D), lambda i: (_ctx_tile(i), 0)),
                     pl.BlockSpec((TM, D), lambda i: (_lat_tile(i), 0))]
        out_shape = [jax.ShapeDtypeStruct((T_CTX, D), F32), jax.ShapeDtypeStruct((T_LAT, D), F32)]
    else:
        out_specs = pl.BlockSpec((TM, D), lambda i: (i, 0))
        out_shape = jax.ShapeDtypeStruct((T_ALL, D), F32)
    return pl.pallas_call(
        functools.partial(_combine_kernel, split),
        grid=(N_TILES,),
        in_specs=[
            pl.BlockSpec((TM, D), lambda i: (i, 0)),
            pl.BlockSpec((TOP_K, TM, DW), lambda i: (0, i, 0)),
            pl.BlockSpec((TM, LANE), lambda i: (i, 0)),
            pl.BlockSpec((None, 1, 6 * D), lambda i: (_mod_row(i), 0, 0)),
        ],
        out_specs=out_specs,
        out_shape=out_shape,
        compiler_params=pltpu.CompilerParams(
            dimension_semantics=("arbitrary",), vmem_limit_bytes=VMEM_LIMIT),
        name="combine_split" if split else "combine",
    )(x_all, y_tok, route, mod_l)


SC_CORES, SC_SUBCORES = 2, 16
SC_WORKERS = SC_CORES * SC_SUBCORES
SC_CHUNK = 128
SC_CHUNKS_PER_WORKER = T_ALL // SC_CHUNK // SC_WORKERS


def _sc_mesh():
    return plsc.VectorSubcoreMesh(core_axis_name="c", subcore_axis_name="s")


def _sc_scratch():
    return [pltpu.VMEM((TOP_K, SC_CHUNK), jnp.int32), pltpu.VMEM((SC_CHUNK, DW), jnp.int32),
            pltpu.SemaphoreType.DMA]


def _dispatch_rows(h2p, dest):
    @functools.partial(pl.kernel, mesh=_sc_mesh(), out_type=jax.ShapeDtypeStruct((MOE_ROWS, DW), jnp.int32),
                       scratch_types=_sc_scratch(), name="dispatch_rows")
    def run(h_hbm, d_hbm, o_hbm, idx_v, rows_v, sem):
        wid = lax.axis_index("s") * SC_CORES + lax.axis_index("c")
        for j in range(SC_CHUNKS_PER_WORKER):
            c = wid * SC_CHUNKS_PER_WORKER + j
            pltpu.sync_copy(d_hbm.at[c], idx_v)
            pltpu.sync_copy(h_hbm.at[pl.ds(c * SC_CHUNK, SC_CHUNK)], rows_v)
            copies = [pltpu.async_copy(rows_v, o_hbm.at[idx_v.at[k]], sem) for k in range(TOP_K)]
            for cp in copies:
                cp.wait()

    return run(h2p, dest)


def _gather_rows(y, dest):
    @functools.partial(pl.kernel, mesh=_sc_mesh(), out_type=jax.ShapeDtypeStruct((TOP_K, T_ALL, DW), jnp.int32),
                       scratch_types=_sc_scratch(), name="gather_rows")
    def run(y_hbm, d_hbm, o_hbm, idx_v, rows_v, sem):
        wid = lax.axis_index("s") * SC_CORES + lax.axis_index("c")
        for j in range(SC_CHUNKS_PER_WORKER):
            c = wid * SC_CHUNKS_PER_WORKER + j
            pltpu.sync_copy(d_hbm.at[c], idx_v)
            for k in range(TOP_K):
                pltpu.async_copy(y_hbm.at[idx_v.at[k]], rows_v, sem).wait()
                pltpu.sync_copy(rows_v, o_hbm.at[k, pl.ds(c * SC_CHUNK, SC_CHUNK)])

    return run(y, dest)


def _constants():
    lane64 = np.arange(4 * HD)
    seg64 = (lane64[:, None] // HD == lane64[None, :] // HD).astype(np.float32)
    lane96 = np.arange(4 * BP)
    real = lane96 % BP < B_QK
    seg96 = ((lane96[:, None] // BP == lane96[None, :] // BP) & real[:, None] & real[None, :]).astype(np.float32)
    place = np.zeros((LANE, 4 * BP), np.float32)
    for hh in range(4):
        place[np.arange(B_ROPE), hh * BP + B_NOPE + np.arange(B_ROPE)] = 1.0

    def angles(rot_dim):
        pos = np.arange(LAT_LEN)
        rows = (pos // GRID_W).astype(np.float32)
        cols = (pos % GRID_W).astype(np.float32)
        axis_dim = rot_dim // 2
        inv = jnp.asarray(ROPE_THETA, F32) ** (-jnp.arange(0, axis_dim, 2, dtype=F32) / axis_dim)
        ang = jnp.concatenate([rows[:, None] * inv, cols[:, None] * inv], axis=-1)
        return jnp.cos(ang), jnp.sin(ang)

    def head_tables(rot_dim):
        cos, sin = angles(rot_dim)
        q = rot_dim // 4
        cr, cc, sr, sc = cos[:, :q], cos[:, q:], sin[:, :q], sin[:, q:]
        return (jnp.concatenate([cr, cr, cc, cc], axis=-1), jnp.concatenate([-sr, sr, -sc, sc], axis=-1))

    c64, s64 = head_tables(HD)
    cos64 = jnp.tile(c64, (1, 4))
    sin64 = jnp.tile(s64, (1, 4))
    c32, s32 = head_tables(B_ROPE)
    ones = jnp.ones((LAT_LEN, B_NOPE), F32)
    zeros = jnp.zeros((LAT_LEN, B_NOPE), F32)
    padz = jnp.zeros((LAT_LEN, BP - B_QK), F32)
    cos96 = jnp.tile(jnp.concatenate([ones, c32, padz], axis=-1), (1, 4))
    sin96 = jnp.tile(jnp.concatenate([zeros, s32, padz], axis=-1), (1, 4))
    first64 = ((lane64 % 32) < 16).astype(np.float32)[None, :]
    first96 = (((lane96 % BP) % 16) < 8).astype(np.float32)[None, :]
    tri = (np.arange(TM)[:, None] < np.arange(TM)[None, :]).astype(np.float32)
    return dict(seg64=jnp.asarray(seg64, BF16), seg96=jnp.asarray(seg96, BF16), place=jnp.asarray(place, BF16),
                tri=jnp.asarray(tri, BF16),
                cos64=cos64, sin64=sin64, cos96=cos96, sin96=sin96,
                first64=jnp.asarray(first64), first96=jnp.asarray(first96))


def _pad_heads(w, per_head, width):
    r = w.shape[0]
    w = w.reshape(r, 4, per_head)
    return jnp.pad(w, ((0, 0), (0, 0), (0, width - per_head))).reshape(r, 4 * width)


def _layer_weights(l, norm1_g, norm2_g, w_in, a_q_g, a_k_g, b_cq_g, b_ckv_g, w_uq, w_ukv, b_q_g, b_k_g,
                   c_q_g, c_k_g, w_out, router_w, router_b):
    wi = w_in[l]
    o = np.cumsum((0, 256, 128, 128, 384, 256, 32, 256, 128, 128))
    seg = lambda k: wi[:, o[k]:o[k + 1]]
    w_in_r = jnp.concatenate([seg(0), seg(1), seg(2), seg(3), seg(4), seg(6), seg(7), seg(8), seg(5),
                              jnp.zeros((D, LANE - B_ROPE), F32)], axis=-1).astype(BF16)
    ukv = w_ukv[l].reshape(B_KV_LORA, 4, B_NOPE + B_V)
    w_uk = jnp.pad(ukv[:, :, :B_NOPE], ((0, 0), (0, 0), (0, BP - B_NOPE))).reshape(B_KV_LORA, 4 * BP)
    w_uv = ukv[:, :, B_NOPE:].reshape(B_KV_LORA, 4 * B_V)
    rw = router_w[l]
    rw_hi = rw.astype(BF16)
    rw_lo = (rw - rw_hi.astype(F32)).astype(BF16)
    row = lambda v: v[None, :]
    return dict(
        n1g=row(norm1_g[l]), n2g=row(norm2_g[l]), w_in=w_in_r,
        aqg=row(jnp.tile(a_q_g[l], 4)), akg=row(jnp.tile(a_k_g[l], 2)),
        cqg=row(jnp.tile(c_q_g[l], 4)), ckg=row(jnp.tile(c_k_g[l], 2)),
        bcqg=row(b_cq_g[l]), bckvg=row(b_ckv_g[l]),
        bqg=_pad_heads(row(jnp.tile(b_q_g[l], 4)), B_QK, BP), bkg=_pad_heads(row(jnp.tile(b_k_g[l], 4)), B_QK, BP),
        w_uq=_pad_heads(w_uq[l], B_QK, BP).astype(BF16), w_uk=w_uk.astype(BF16), w_uv=w_uv.astype(BF16),
        w_out=w_out[l].astype(BF16), rwt=jnp.concatenate([rw_hi.T, rw_lo.T], axis=0), rb=router_b[l][:, None])


def _moe(l, x_all, h2p, gate_slab, route, counts, mod_l, w_gu, b_gu, w_dn, b_dn):
    counts = counts[:, 0].astype(jnp.int32)
    padded = (counts + MOE_TM - 1) // MOE_TM * MOE_TM
    pend = jnp.cumsum(padded)
    pstart = pend - padded
    n_used = (pend[-1] // MOE_TM).astype(jnp.int32)
    tiles = jnp.minimum(jnp.arange(MOE_TILES, dtype=jnp.int32), n_used - 1)
    tile_expert = jnp.sum((pend[None, :] <= tiles[:, None] * MOE_TM).astype(jnp.int32), axis=1)
    tile_expert = jnp.minimum(tile_expert, N_EXPERTS - 1)
    group_end_tile = jnp.sum(jnp.where(tile_expert[:, None] == jnp.arange(N_EXPERTS)[None, :], pend[None, :], 0),
                             axis=1) // MOE_TM
    after = jnp.sum((pend[None, :] <= group_end_tile[:, None] * MOE_TM).astype(jnp.int32), axis=1)
    next_expert = jnp.where(group_end_tile < n_used, jnp.minimum(after, N_EXPERTS - 1), -1).astype(jnp.int32)
    e = route[TOP_K:2 * TOP_K].astype(jnp.int32)
    rank = route[2 * TOP_K:3 * TOP_K].astype(jnp.int32)
    start = jnp.sum(jnp.where(e[:, :, None] == jnp.arange(N_EXPERTS)[None, None, :], pstart[None, None, :], 0), axis=-1)
    dest = (start + rank).reshape(TOP_K, T_ALL // SC_CHUNK, SC_CHUNK).transpose(1, 0, 2)
    x_sorted = _dispatch_rows(h2p, dest)
    y = _experts(l, tile_expert, n_used.reshape(1), next_expert, x_sorted, w_gu, b_gu, w_dn, b_dn)
    y_tok = _gather_rows(y, dest)
    return _combine(l == DEPTH - 1, x_all, y_tok, gate_slab, mod_l)


def kernel(x_prompt, x_sample, cache_a_k, cache_a_v, cache_b_ckv, cache_b_krope, cache_c_k, cache_c_v, c, c_ctx,
           norm1_g, norm2_g, w_mod, b_mod, w_in, a_q_g, a_k_g, a_sink, b_cq_g, b_ckv_g, w_uq, w_ukv, b_q_g, b_k_g,
           c_q_g, c_k_g, w_out, router_w, router_b, w_gu, b_gu, w_dn, b_dn):
    consts = _constants()
    cond = jnp.concatenate([c_ctx[None, :], c, jnp.zeros((3, D), F32)], axis=0)
    mod = _modulation(cond, w_mod, b_mod).reshape(DEPTH, 8, 1, 6 * D)
    x_ctx, x_lat, x_lat_off = x_prompt.reshape(T_CTX, D), x_sample.reshape(T_LAT, D), 0

    states = ()
    for l in range(DEPTH):
        lw = _layer_weights(l, norm1_g, norm2_g, w_in, a_q_g, a_k_g, b_cq_g, b_ckv_g, w_uq, w_ukv, b_q_g, b_k_g,
                            c_q_g, c_k_g, w_out, router_w, router_b)
        mod_l = mod[l]
        names = ["qa", "ka", "va", "qb", "kb", "vb", "qc", "kc", "vc"]
        if l == 0:
            states = tuple(jnp.zeros((N_CTX_SEQ, DEPTH, CTX_LEN) + tail, F32) for tail in STATE_TAILS)
        outs = _projection(False, l, x_ctx, 0, mod_l, lw, consts, states)
        p_ctx = dict(zip(names, outs[:9]))
        states = tuple(outs[9:])
        p_lat = dict(zip(names, _projection(True, l, x_lat, x_lat_off, mod_l, lw, consts)))

        flat = lambda a, w: a[:, l].reshape(N_LAT_SEQ * PAST, w)
        ckb, cvb = _cache_kv(flat(cache_b_ckv, B_KV_LORA),
                             jnp.pad(flat(cache_b_krope, B_ROPE), ((0, 0), (0, LANE - B_ROPE))), lw, consts)
        cache = dict(ka=flat(cache_a_k, 2 * HD).astype(BF16), va=flat(cache_a_v, 2 * HD).astype(BF16),
                     kb=ckb, vb=cvb,
                     kc=flat(cache_c_k, 2 * HD).astype(BF16), vc=flat(cache_c_v, 2 * HD).astype(BF16))
        mix_ctx = _attention_ctx(a_sink[l], p_ctx)
        mix_lat = _attention_lat(a_sink[l], p_lat, cache)
        x_mid, h2p, gate_slab, route, counts = _post_attention(mix_ctx, mix_lat, x_ctx, x_lat, x_lat_off, mod_l, lw,
                                                               consts)
        out = _moe(l, x_mid, h2p, gate_slab, route, counts, mod_l, w_gu, b_gu, w_dn, b_dn)
        if l < DEPTH - 1:
            x_ctx, x_lat, x_lat_off = out, out, N_CTX_TILES
    y_ctx, y_lat = out

    return (y_ctx.reshape(N_CTX_SEQ, CTX_LEN, D), y_lat.reshape(N_LAT_SEQ, LAT_LEN, D)) + states
```

```python
import functools

import jax
import jax.numpy as jnp
import numpy as np
from jax import lax
from jax.experimental import pallas as pl
from jax.experimental.pallas import tpu as pltpu
from jax.experimental.pallas import tpu_sc as plsc

F32 = jnp.float32
BF16 = jnp.bfloat16

D = 1024
DEPTH = 2
N_CTX_SEQ, CTX_LEN = 16, 256
N_LAT_SEQ, LAT_LEN = 4, 2048
PAST = 512
T_CTX = N_CTX_SEQ * CTX_LEN
T_LAT = N_LAT_SEQ * LAT_LEN
T_ALL = T_CTX + T_LAT
GRID_W = 64
HD = 64
WINDOW = 128
B_NOPE, B_ROPE, B_V = 64, 32, 128
B_QK = B_NOPE + B_ROPE
B_Q_LORA, B_KV_LORA = 384, 256
N_EXPERTS, TOP_K = 32, 4
D_FF = 1024
SWIGLU_LIMIT = 7.0
SWIGLU_ALPHA = 1.702
ROPE_THETA = 10000.0
EPS = 1e-6
NEG_INF = -1e30

TM = 256
LANE = 128
BP = 128
N_CTX_TILES = T_CTX // TM
N_LAT_TILES = T_LAT // TM
N_TILES = T_ALL // TM
LAT_TILES_PER_SEQ = LAT_LEN // TM
MOE_TM = 256
FF_CHUNK = 256
MOE_TILES = T_ALL * TOP_K // MOE_TM + N_EXPERTS
MOE_ROWS = MOE_TILES * MOE_TM
DW = D // 2
VMEM_LIMIT = 56 * 1024 * 1024

C_QA, C_KA, C_VA, C_CQ, C_CKV, C_QC, C_KC, C_VC, C_KR, C_END = (
    0, 256, 384, 512, 896, 1152, 1408, 1536, 1664, 1792)


def _dot(a, b):
    return jnp.dot(a, b, preferred_element_type=F32)


def _dot_t(a, b):
    return lax.dot_general(a, b, (((1,), (1,)), ((), ())), preferred_element_type=F32)


def _rms(x, g):
    return x * lax.rsqrt(jnp.mean(x * x, axis=-1, keepdims=True) + EPS) * g


def _seg_norm(x, seg, g, n):
    ss = _dot((x * x).astype(BF16), seg)
    return x * lax.rsqrt(ss * (1.0 / n) + EPS) * g


def _rope(x, cos, sin, first, sh):
    w = x.shape[1]
    fwd = pltpu.roll(x, w - sh, 1)
    bwd = pltpu.roll(x, sh, 1)
    return x * cos + jnp.where(first > 0.5, fwd, bwd) * sin


MOD_BN = 1536


def _mod_kernel(c_ref, w_ref, b_ref, o_ref):
    c = c_ref[...]
    s = (c * jax.nn.sigmoid(c)).astype(BF16)
    o_ref[...] = _dot(s, w_ref[...].astype(BF16)) + b_ref[...]


def _modulation(cond, w_mod, b_mod):
    return pl.pallas_call(
        _mod_kernel,
        grid=(DEPTH, 6 * D // MOD_BN),
        in_specs=[
            pl.BlockSpec((8, D), lambda l, j: (0, 0)),
            pl.BlockSpec((None, D, MOD_BN), lambda l, j: (l, 0, j)),
            pl.BlockSpec((None, 1, MOD_BN), lambda l, j: (l, 0, j)),
        ],
        out_specs=pl.BlockSpec((None, 8, MOD_BN), lambda l, j: (l, 0, j)),
        out_shape=jax.ShapeDtypeStruct((DEPTH, 8, 6 * D), F32),
        compiler_params=pltpu.CompilerParams(
            dimension_semantics=("arbitrary", "arbitrary"), vmem_limit_bytes=VMEM_LIMIT),
        name="modulation",
    )(cond, w_mod, b_mod.reshape(DEPTH, 1, 6 * D))


def _proj_kernel(is_lat, n_aliased, *refs):
    (x_ref, mod_ref, n1g_ref, win_ref, seg64_ref, seg96_ref, aqg_ref, akg_ref, cqg_ref, ckg_ref,
     bcqg_ref, bckvg_ref, bqg_ref, bkg_ref, wuq_ref, wuk_ref, wuv_ref, plc_ref) = refs[:18]
    refs = refs[18 + n_aliased:]
    if is_lat:
        (cos64_ref, sin64_ref, cos96_ref, sin96_ref, f64_ref, f96_ref) = refs[:6]
        refs = refs[6:]
    (qa_ref, ka_ref, va_ref, qb_ref, kb_ref, vb_ref, qc_ref, kc_ref, vc_ref) = refs[:9]
    refs = refs[9:]
    if not is_lat:
        if n_aliased == 0:
            for r in refs:
                r[1:] = jnp.zeros((DEPTH - 1,) + r.shape[1:], F32)
            refs = [r.at[0] for r in refs]
        (kas_ref, vas_ref, ckvs_ref, krs_ref, kcs_ref, vcs_ref) = refs

    x = x_ref[...]
    m = mod_ref[...]
    sh1, sc1 = m[:, 0:D], m[:, D:2 * D]
    h = (_rms(x, n1g_ref[...]) * (1.0 + sc1) + sh1).astype(BF16)

    def proj(a, b):
        return _dot(h, win_ref[:, a:b])

    def rope64(t):
        wd = t.shape[1]
        return _rope(t, cos64_ref[:, :wd], sin64_ref[:, :wd], f64_ref[:, :wd], 16) if is_lat else t

    def rope96(t):
        return _rope(t, cos96_ref[...], sin96_ref[...], f96_ref[...], 8) if is_lat else t

    seg64 = seg64_ref[...]
    seg64h = seg64_ref[0:2 * HD, 0:2 * HD]
    seg96 = seg96_ref[...]

    def seg_sum(t, seg):
        return _dot((t * t).astype(BF16), seg)

    def seg_finish(t, ss, g, n):
        return t * lax.rsqrt(ss * (1.0 / n) + EPS) * g

    p_cq, p_ckv, p_kr = proj(C_CQ, C_CKV), proj(C_CKV, C_QC), proj(C_KR, C_END)
    p_qa, p_ka, p_qc, p_kc = proj(C_QA, C_KA), proj(C_KA, C_VA), proj(C_QC, C_KC), proj(C_KC, C_VC)
    p_va, p_vc = proj(C_VA, C_CQ), proj(C_VC, C_KR)

    cq = _rms(p_cq, bcqg_ref[...]).astype(BF16)
    ckv = _rms(p_ckv, bckvg_ref[...])
    ckv16 = ckv.astype(BF16)
    u_q = _dot(cq, wuq_ref[...])
    u_k = _dot(ckv16, wuk_ref[...]) + _dot(p_kr.astype(BF16), plc_ref[...])
    u_v = _dot(ckv16, wuv_ref[...])
    ss_qa, ss_ka = seg_sum(p_qa, seg64), seg_sum(p_ka, seg64h)
    ss_qc, ss_kc = seg_sum(p_qc, seg64), seg_sum(p_kc, seg64h)
    ss_qb, ss_kb = seg_sum(u_q, seg96), seg_sum(u_k, seg96)

    def store_kv_state(ref, t):
        for kv in range(2):
            ref[:, kv, :] = t[:, kv * HD:(kv + 1) * HD]

    if not is_lat:
        store_kv_state(vas_ref, p_va)
        store_kv_state(vcs_ref, p_vc)
        ckvs_ref[...] = ckv
        krs_ref[...] = p_kr[:, 0:B_ROPE]
    va_ref[...] = p_va.astype(BF16)
    vc_ref[...] = p_vc.astype(BF16)
    vb_ref[...] = u_v.astype(BF16)

    t = seg_finish(p_qa, ss_qa, aqg_ref[...], HD)
    qa_ref[...] = (rope64(t) * HD ** -0.5).astype(BF16)
    t = seg_finish(p_ka, ss_ka, akg_ref[...], HD)
    if not is_lat:
        store_kv_state(kas_ref, t)
    ka_ref[...] = rope64(t).astype(BF16)
    t = seg_finish(p_qc, ss_qc, cqg_ref[...], HD)
    qc_ref[...] = (rope64(t) * HD ** -0.5).astype(BF16)
    t = seg_finish(p_kc, ss_kc, ckg_ref[...], HD)
    if not is_lat:
        store_kv_state(kcs_ref, t)
    kc_ref[...] = rope64(t).astype(BF16)
    t = seg_finish(u_q, ss_qb, bqg_ref[...], B_QK)
    qb_ref[...] = (rope96(t) * B_QK ** -0.5).astype(BF16)
    t = seg_finish(u_k, ss_kb, bkg_ref[...], B_QK)
    kb_ref[...] = rope96(t).astype(BF16)


def _full(shape):
    nd = len(shape)
    return pl.BlockSpec(shape, lambda i: (0,) * nd)


STATE_TAILS = ((2, HD), (2, HD), (B_KV_LORA,), (B_ROPE,), (2, HD), (2, HD))


def _layer_spec(a, l):
    nd = a.ndim - 1
    return pl.BlockSpec((None,) + a.shape[1:], lambda *_: (l,) + (0,) * nd)


def _projection(is_lat, l, x_src, x_off, mod, lw, consts, prev_states=()):
    assert is_lat or bool(prev_states) == (l > 0)
    n_tiles = N_LAT_TILES if is_lat else N_CTX_TILES
    t_rows = n_tiles * TM
    if is_lat:
        mod_map = lambda i: (l, 1 + i // LAT_TILES_PER_SEQ, 0, 0)
    else:
        mod_map = lambda i: (l, 0, 0, 0)
    ins = [x_src, mod, lw["n1g"], lw["w_in"], consts["seg64"], consts["seg96"], lw["aqg"], lw["akg"],
           lw["cqg"], lw["ckg"], lw["bcqg"], lw["bckvg"], lw["bqg"], lw["bkg"], lw["w_uq"], lw["w_uk"],
           lw["w_uv"], consts["place"]]
    shared = (consts["seg64"], consts["seg96"], consts["place"])
    in_specs = [pl.BlockSpec((TM, D), lambda i: (i + x_off, 0)),
                pl.BlockSpec((None, None, 1, 6 * D), mod_map)]
    in_specs += [_full(a.shape) if any(a is s for s in shared) else _layer_spec(a, l) for a in ins[2:]]
    n_plain = len(ins)
    ins += list(prev_states)
    in_specs += [pl.BlockSpec(memory_space=pl.ANY) for _ in prev_states]
    if is_lat:
        tabs = [consts["cos64"], consts["sin64"], consts["cos96"], consts["sin96"]]
        ins += tabs + [consts["first64"], consts["first96"]]
        in_specs += [pl.BlockSpec((TM, a.shape[1]), lambda i: (i % LAT_TILES_PER_SEQ, 0)) for a in tabs]
        in_specs += [_full(consts["first64"].shape), _full(consts["first96"].shape)]
    widths = [4 * HD, 2 * HD, 2 * HD, 4 * BP, 4 * BP, 4 * B_V, 4 * HD, 2 * HD, 2 * HD]
    out_shape = [jax.ShapeDtypeStruct((t_rows, w), BF16) for w in widths]
    out_specs = [pl.BlockSpec((TM, w), lambda i: (i, 0)) for w in widths]
    if not is_lat:
        for tail in STATE_TAILS:
            zeros = (0,) * len(tail)
            out_shape.append(jax.ShapeDtypeStruct((N_CTX_SEQ, DEPTH, CTX_LEN) + tail, F32))
            if prev_states:
                out_specs.append(pl.BlockSpec((None, None, CTX_LEN) + tail, lambda i, z=zeros: (i, l, 0) + z))
            else:
                out_specs.append(pl.BlockSpec((None, DEPTH, CTX_LEN) + tail, lambda i, z=zeros: (i, 0, 0) + z))
    aliases = {n_plain + k: len(widths) + k for k in range(len(prev_states))}
    return pl.pallas_call(
        functools.partial(_proj_kernel, is_lat, len(prev_states)),
        grid=(n_tiles,),
        in_specs=in_specs,
        out_specs=out_specs,
        out_shape=out_shape,
        input_output_aliases=aliases,
        compiler_params=pltpu.CompilerParams(
            dimension_semantics=("arbitrary",), vmem_limit_bytes=VMEM_LIMIT),
        name="proj_lat" if is_lat else "proj_ctx",
    )(*ins)


def _cache_kv_kernel(ckv_ref, kr_ref, seg96_ref, bkg_ref, wuk_ref, wuv_ref, plc_ref, kb_ref, vb_ref):
    ckv16 = ckv_ref[...].astype(BF16)
    vb_ref[...] = _dot(ckv16, wuv_ref[...]).astype(BF16)
    kpre = _dot(ckv16, wuk_ref[...]) + _dot(kr_ref[...].astype(BF16), plc_ref[0:B_ROPE, :])
    kb_ref[...] = _seg_norm(kpre, seg96_ref[...], bkg_ref[...], B_QK).astype(BF16)


def _cache_kv(cache_b_ckv, cache_b_krope, lw, consts):
    def c_map(l, r):
        return (r // (PAST // TM), l, r % (PAST // TM), 0)

    def w_spec(a):
        return pl.BlockSpec((None,) + a.shape[1:], lambda l, r: (l,) + (0,) * (a.ndim - 1))

    def s_spec(a):
        return pl.BlockSpec(a.shape, lambda l, r: (0,) * a.ndim)

    rows = N_LAT_SEQ * PAST
    return pl.pallas_call(
        _cache_kv_kernel,
        grid=(DEPTH, rows // TM),
        in_specs=[pl.BlockSpec((None, None, TM, B_KV_LORA), c_map), pl.BlockSpec((None, None, TM, B_ROPE), c_map),
                  s_spec(consts["seg96"]), w_spec(lw["bkg"]), w_spec(lw["w_uk"]), w_spec(lw["w_uv"]),
                  s_spec(consts["place"])],
        out_specs=[pl.BlockSpec((None, TM, 4 * BP), lambda l, r: (l, r, 0)),
                   pl.BlockSpec((None, TM, 4 * B_V), lambda l, r: (l, r, 0))],
        out_shape=[jax.ShapeDtypeStruct((DEPTH, rows, 4 * BP), BF16),
                   jax.ShapeDtypeStruct((DEPTH, rows, 4 * B_V), BF16)],
        compiler_params=pltpu.CompilerParams(
            dimension_semantics=("arbitrary", "arbitrary"), vmem_limit_bytes=VMEM_LIMIT),
        name="cache_kv",
    )(cache_b_ckv, cache_b_krope, consts["seg96"], lw["bkg"], lw["w_uk"], lw["w_uv"], consts["place"])


def _scores(q, segs):
    scores = []
    for k, _, mask in segs:
        s = _dot_t(q, k)
        if mask is not None:
            s = jnp.where(mask, s, NEG_INF)
        scores.append(s)
    return scores


def _softmax_pv(scores, segs, sink):
    m = None
    for s in scores:
        sm = jnp.max(s, axis=-1, keepdims=True)
        m = sm if m is None else jnp.maximum(m, sm)
    if sink is not None:
        m = jnp.maximum(m, sink)
    acc = None
    den = None
    for s, (_, v, _) in zip(scores, segs):
        e = jnp.exp(s - m)
        d = jnp.sum(e, axis=-1, keepdims=True)
        o = _dot(e.astype(BF16), v)
        acc = o if acc is None else acc + o
        den = d if den is None else den + d
    if sink is not None:
        den = den + jnp.exp(sink - m)
    return acc / den


def _run_heads(jobs, o_ref):
    nxt = _scores(jobs[0][0](), jobs[0][1]())
    for n, (_, load_segs, sink, col) in enumerate(jobs):
        scores = nxt
        if n + 1 < len(jobs):
            nxt = _scores(jobs[n + 1][0](), jobs[n + 1][1]())
        o = _softmax_pv(scores, load_segs(), sink)
        o_ref[:, col:col + o.shape[1]] = o.astype(BF16)


OUT_B = 4 * HD
OUT_C = 4 * HD + 4 * B_V


def _head_job(q_ref, q_cols, seg_refs, kv_cols, v_cols, sink, col, rows=None, masks=None):
    def load_segs():
        segs = []
        for n, (k_ref, v_ref) in enumerate(seg_refs):
            r = rows if (rows is not None and n == 0) else slice(None)
            segs.append((k_ref[r, kv_cols], v_ref[r, v_cols], None if masks is None else masks[n]))
        return segs

    return (lambda: q_ref[:, q_cols]), load_segs, sink, col


def _attn_ctx_kernel(l, sink_ref, qa_ref, ka_ref, va_ref, qb_ref, kb_ref, vb_ref, qc_ref, kc_ref, vc_ref, o_ref):
    jobs = []
    for h in range(4):
        cs = slice(h // 2 * HD, (h // 2 + 1) * HD)
        jobs.append(_head_job(qa_ref, slice(h * HD, (h + 1) * HD), [(ka_ref, va_ref)], cs, cs, sink_ref[l, h],
                              h * HD))
    for h in range(4):
        ks, vs = slice(h * BP, (h + 1) * BP), slice(h * B_V, (h + 1) * B_V)
        jobs.append(_head_job(qb_ref, ks, [(kb_ref, vb_ref)], ks, vs, None, OUT_B + h * B_V))
    for h in range(4):
        cs = slice(h // 2 * HD, (h // 2 + 1) * HD)
        jobs.append(_head_job(qc_ref, slice(h * HD, (h + 1) * HD), [(kc_ref, vc_ref)], cs, cs, None, OUT_C + h * HD))
    _run_heads(jobs, o_ref)


def _attention_ctx(l, sink, p):
    names = ["qa", "ka", "va", "qb", "kb", "vb", "qc", "kc", "vc"]
    ins = [p[n] for n in names]
    in_specs = [pl.BlockSpec(memory_space=pltpu.SMEM)]
    in_specs += [pl.BlockSpec((CTX_LEN, a.shape[1]), lambda i: (i, 0)) for a in ins]
    return pl.pallas_call(
        functools.partial(_attn_ctx_kernel, l),
        grid=(N_CTX_SEQ,),
        in_specs=in_specs,
        out_specs=pl.BlockSpec((CTX_LEN, D), lambda i: (i, 0)),
        out_shape=jax.ShapeDtypeStruct((T_CTX, D), BF16),
        compiler_params=pltpu.CompilerParams(
            dimension_semantics=("arbitrary",), vmem_limit_bytes=VMEM_LIMIT),
        name="attn_ctx",
    )(sink, *ins)


WIN_SPAN = TM + 2 * WINDOW


def _attn_lat_kernel(l, sink_ref, qa_ref, qb_ref, qc_ref, ka_ref, va_ref, kb_ref, vb_ref, kc_ref, vc_ref,
                     cka_ref, cva_ref, ckb_ref, cvb_ref, ckc_ref, cvc_ref, o_ref):
    qi = pl.program_id(1)
    ws = pl.multiple_of(jnp.clip(qi * TM - WINDOW, 0, LAT_LEN - WIN_SPAN), WINDOW)
    qpos = qi * TM + lax.broadcasted_iota(jnp.int32, (TM, WIN_SPAN), 0)
    kpos = ws + lax.broadcasted_iota(jnp.int32, (TM, WIN_SPAN), 1)
    band = jnp.abs(qpos - kpos) <= WINDOW
    jobs = []
    for h in range(4):
        cs = slice(h // 2 * HD, (h // 2 + 1) * HD)
        jobs.append(_head_job(qa_ref, slice(h * HD, (h + 1) * HD), [(ka_ref, va_ref), (cka_ref, cva_ref)], cs, cs,
                              sink_ref[l, h], h * HD, rows=pl.ds(ws, WIN_SPAN), masks=(band, None)))
    for h in range(4):
        ks, vs = slice(h * BP, (h + 1) * BP), slice(h * B_V, (h + 1) * B_V)
        jobs.append(_head_job(qb_ref, ks, [(kb_ref, vb_ref), (ckb_ref, cvb_ref)], ks, vs, None, OUT_B + h * B_V))
    for h in range(4):
        cs = slice(h // 2 * HD, (h // 2 + 1) * HD)
        jobs.append(_head_job(qc_ref, slice(h * HD, (h + 1) * HD), [(kc_ref, vc_ref), (ckc_ref, cvc_ref)], cs, cs,
                              None, OUT_C + h * HD))
    _run_heads(jobs, o_ref)


def _attention_lat(l, sink, p, cache):
    q_ins = [p["qa"], p["qb"], p["qc"]]
    kv_ins = [p[n] for n in ("ka", "va", "kb", "vb", "kc", "vc")]
    c_names = ("ka", "va", "kb", "vb", "kc", "vc")
    c_ins = [cache[n] for n in c_names]
    in_specs = [pl.BlockSpec(memory_space=pltpu.SMEM)]
    in_specs += [pl.BlockSpec((TM, a.shape[1]), lambda b, i: (b * LAT_TILES_PER_SEQ + i, 0)) for a in q_ins]
    in_specs += [pl.BlockSpec((LAT_LEN, a.shape[1]), lambda b, i: (b, 0)) for a in kv_ins]
    for n, a in zip(c_names, c_ins):
        if n in ("kb", "vb"):
            in_specs.append(pl.BlockSpec((None, PAST, a.shape[-1]), lambda b, i: (l, b, 0)))
        else:
            in_specs.append(pl.BlockSpec((None, None, PAST, a.shape[-1]), lambda b, i: (b, l, 0, 0)))
    return pl.pallas_call(
        functools.partial(_attn_lat_kernel, l),
        grid=(N_LAT_SEQ, LAT_TILES_PER_SEQ),
        in_specs=in_specs,
        out_specs=pl.BlockSpec((TM, D), lambda b, i: (b * LAT_TILES_PER_SEQ + i, 0)),
        out_shape=jax.ShapeDtypeStruct((T_LAT, D), BF16),
        compiler_params=pltpu.CompilerParams(
            dimension_semantics=("arbitrary", "arbitrary"), vmem_limit_bytes=VMEM_LIMIT),
        name="attn_lat",
    )(sink, *q_ins, *kv_ins, *c_ins)


def _pack_rows(x):
    half = x.shape[1] // 2
    r = x.astype(BF16).astype(F32)
    hi = lax.bitcast_convert_type(r[:, :half], jnp.int32)
    lo = lax.bitcast_convert_type(r[:, half:], jnp.int32)
    return jnp.bitwise_or(hi, lax.shift_right_logical(lo, 16))


def _unpack_rows(p):
    a = lax.bitcast_convert_type(jnp.bitwise_and(p, -65536), F32)
    b = lax.bitcast_convert_type(lax.shift_left(p, 16), F32)
    return jnp.concatenate([a, b], axis=1).astype(BF16)


ROUTE_ROWS = 16


def _post_kernel(mc_ref, ml_ref, xc_ref, xl_ref, mod_ref, n2g_ref, wout_ref, rwt_ref, rb_ref, tri_ref,
                 xo_ref, h2_ref, gate_ref, route_ref, cnt_ref, run_ref):
    i = pl.program_id(0)

    @pl.when(i == 0)
    def _():
        run_ref[...] = jnp.zeros_like(run_ref)
        cnt_ref[...] = jnp.zeros_like(cnt_ref)

    is_ctx = i < N_CTX_TILES
    mix = jnp.where(is_ctx, mc_ref[...], ml_ref[...])
    m = mod_ref[...]
    g1, sh2, sc2 = m[:, 2 * D:3 * D], m[:, 3 * D:4 * D], m[:, 4 * D:5 * D]
    x = jnp.where(is_ctx, xc_ref[...], xl_ref[...]) + g1 * _dot(mix, wout_ref[...])
    xo_ref[...] = x
    h2 = _rms(x, n2g_ref[...]) * (1.0 + sc2) + sh2
    hi = h2.astype(BF16)
    h2_ref[...] = _pack_rows(h2)
    lo = (h2 - hi.astype(F32)).astype(BF16)
    a = _dot_t(rwt_ref[...], hi)
    b = _dot_t(rwt_ref[0:N_EXPERTS, :], lo)
    logits = a[0:N_EXPERTS] + a[N_EXPERTS:] + b + rb_ref[...]
    eidx = lax.broadcasted_iota(jnp.int32, logits.shape, 0).astype(F32)
    work = logits
    vals, idxs, hots = [], [], []
    for _ in range(TOP_K):
        v = jnp.max(work, axis=0, keepdims=True)
        idx = jnp.min(jnp.where(work == v, eidx, float(N_EXPERTS)), axis=0, keepdims=True)
        hot = eidx == idx
        vals.append(v)
        idxs.append(idx)
        hots.append(hot)
        work = jnp.where(hot, -jnp.inf, work)
    es = [jnp.exp(v - vals[0]) for v in vals]
    den = es[0] + es[1] + es[2] + es[3]
    gates = [e / den for e in es]
    sel = jnp.where(hots[0] | hots[1] | hots[2] | hots[3], 1.0, 0.0)
    run = run_ref[:, 0:1]
    before = _dot(sel.astype(BF16), tri_ref[...]) + run
    run_new = jnp.broadcast_to(run + jnp.sum(sel, axis=1, keepdims=True), run_ref.shape)
    run_ref[...] = run_new
    cnt_ref[...] = run_new
    ranks = [jnp.sum(jnp.where(hots[k], before, 0.0), axis=0, keepdims=True) for k in range(TOP_K)]
    pad = [jnp.zeros((ROUTE_ROWS - 3 * TOP_K, TM), F32)]
    route_ref[...] = jnp.concatenate(gates + idxs + ranks + pad, axis=0)
    gate_ref[...] = jnp.concatenate(gates + [jnp.zeros((LANE - TOP_K, TM), F32)], axis=0).T


def _mod_row(i):
    return jnp.where(i < N_CTX_TILES, 0, 1 + (i - N_CTX_TILES) // LAT_TILES_PER_SEQ)


def _ctx_tile(i):
    return jnp.minimum(i, N_CTX_TILES - 1)


def _lat_tile(i):
    return jnp.maximum(i - N_CTX_TILES, 0)


def _post_attention(l, mix_ctx, mix_lat, x_ctx, x_lat, x_lat_off, mod, lw, consts):
    ins = [mix_ctx, mix_lat, x_ctx, x_lat, mod, lw["n2g"], lw["w_out"], lw["rwt"], lw["rb"], consts["tri"]]
    in_specs = [
        pl.BlockSpec((TM, D), lambda i: (_ctx_tile(i), 0)),
        pl.BlockSpec((TM, D), lambda i: (_lat_tile(i), 0)),
        pl.BlockSpec((TM, D), lambda i: (_ctx_tile(i), 0)),
        pl.BlockSpec((TM, D), lambda i: (_lat_tile(i) + x_lat_off, 0)),
        pl.BlockSpec((None, None, 1, 6 * D), lambda i: (l, _mod_row(i), 0, 0)),
    ] + [_layer_spec(a, l) for a in ins[5:9]] + [_full(consts["tri"].shape)]
    return pl.pallas_call(
        _post_kernel,
        grid=(N_TILES,),
        in_specs=in_specs,
        out_specs=[pl.BlockSpec((TM, D), lambda i: (i, 0)), pl.BlockSpec((TM, DW), lambda i: (i, 0)),
                   pl.BlockSpec((TM, LANE), lambda i: (i, 0)), pl.BlockSpec((ROUTE_ROWS, TM), lambda i: (0, i)),
                   pl.BlockSpec((N_EXPERTS, LANE), lambda i: (0, 0))],
        out_shape=[jax.ShapeDtypeStruct((T_ALL, D), F32), jax.ShapeDtypeStruct((T_ALL, DW), jnp.int32),
                   jax.ShapeDtypeStruct((T_ALL, LANE), F32), jax.ShapeDtypeStruct((ROUTE_ROWS, T_ALL), F32),
                   jax.ShapeDtypeStruct((N_EXPERTS, LANE), F32)],
        scratch_shapes=[pltpu.VMEM((N_EXPERTS, LANE), F32)],
        compiler_params=pltpu.CompilerParams(
            dimension_semantics=("arbitrary",), vmem_limit_bytes=VMEM_LIMIT),
        name="post_attn",
    )(*ins)


def _expert_kernel(l, te_ref, nu_ref, nxt_ref, x_ref, wgu_hbm, bgu_ref, wdn_hbm, bdn_ref, y_ref,
                   wgu32, wdn32, wgu16, wdn16, sem):
    i = pl.program_id(0)
    prev = te_ref[jnp.maximum(i - 1, 0)]
    new_expert = jnp.logical_or(i == 0, te_ref[i] != prev)

    def weight_copies(e):
        return (pltpu.make_async_copy(wgu_hbm.at[l, e], wgu32, sem.at[0]),
                pltpu.make_async_copy(wdn_hbm.at[l, e], wdn32, sem.at[1]))

    @pl.when(i == 0)
    def _():
        for cp in weight_copies(te_ref[0]):
            cp.start()

    @pl.when(new_expert)
    def _():
        cps = weight_copies(te_ref[i])
        cps[0].wait()
        wgu16[...] = wgu32[...].astype(BF16)
        cps[1].wait()
        wdn16[...] = wdn32[...].astype(BF16)

        @pl.when(nxt_ref[i] >= 0)
        def _():
            for cp in weight_copies(nxt_ref[i]):
                cp.start()

    @pl.when(i < nu_ref[0])
    def _():
        x = _unpack_rows(x_ref[...])

        def gate_up(a):
            b = a + FF_CHUNK
            g = _dot(x, wgu16[:, a:b]) + bgu_ref[:, a:b]
            up = _dot(x, wgu16[:, D_FF + a:D_FF + b]) + bgu_ref[:, D_FF + a:D_FF + b]
            return g, up

        def activate(g, up):
            g = jnp.minimum(g, SWIGLU_LIMIT)
            up = jnp.clip(up, -SWIGLU_LIMIT, SWIGLU_LIMIT)
            return ((up + 1.0) * (g * jax.nn.sigmoid(SWIGLU_ALPHA * g))).astype(BF16)

        starts = list(range(0, D_FF, FF_CHUNK))
        acc = None
        pre = gate_up(starts[0])
        for n, a in enumerate(starts):
            hid = activate(*pre)
            if n + 1 < len(starts):
                pre = gate_up(starts[n + 1])
            o = _dot(hid, wdn16[a:a + FF_CHUNK, :])
            acc = o if acc is None else acc + o
        y_ref[...] = _pack_rows(acc + bdn_ref[...])


def _experts(l, tile_expert, n_used, next_expert, x_sorted, w_gu, b_gu, w_dn, b_dn):
    def row_map(i, te, nu, nxt):
        return (jnp.minimum(i, nu[0] - 1), 0)

    def b_map(i, te, nu, nxt):
        return (l, te[i], 0, 0)

    grid_spec = pltpu.PrefetchScalarGridSpec(
        num_scalar_prefetch=3,
        grid=(MOE_TILES,),
        in_specs=[
            pl.BlockSpec((MOE_TM, DW), row_map),
            pl.BlockSpec(memory_space=pl.ANY),
            pl.BlockSpec((None, None, 1, 2 * D_FF), b_map),
            pl.BlockSpec(memory_space=pl.ANY),
            pl.BlockSpec((None, None, 1, D), b_map),
        ],
        out_specs=pl.BlockSpec((MOE_TM, DW), row_map),
        scratch_shapes=[pltpu.VMEM((D, 2 * D_FF), F32), pltpu.VMEM((D_FF, D), F32),
                        pltpu.VMEM((D, 2 * D_FF), BF16), pltpu.VMEM((D_FF, D), BF16),
                        pltpu.SemaphoreType.DMA((2,))],
    )
    return pl.pallas_call(
        functools.partial(_expert_kernel, l),
        grid_spec=grid_spec,
        out_shape=jax.ShapeDtypeStruct((MOE_ROWS, DW), jnp.int32),
        compiler_params=pltpu.CompilerParams(
            dimension_semantics=("arbitrary",), vmem_limit_bytes=VMEM_LIMIT),
        name="experts",
    )(tile_expert, n_used, next_expert, x_sorted, w_gu, b_gu.reshape(DEPTH, N_EXPERTS, 1, 2 * D_FF), w_dn,
      b_dn.reshape(DEPTH, N_EXPERTS, 1, D))


def _combine_kernel(split, x_ref, y_ref, route_ref, mod_ref, *o_refs):
    g2 = mod_ref[...][:, 5 * D:6 * D]
    r = route_ref[...]
    acc = r[:, 0:1] * _unpack_rows(y_ref[0]).astype(F32)
    for k in range(1, TOP_K):
        acc = acc + r[:, k:k + 1] * _unpack_rows(y_ref[k]).astype(F32)
    res = x_ref[...] + g2 * acc
    if not split:
        o_refs[0][...] = res
        return
    is_ctx = pl.program_id(0) < N_CTX_TILES

    @pl.when(is_ctx)
    def _():
        o_refs[0][...] = res

    @pl.when(jnp.logical_not(is_ctx))
    def _():
        o_refs[1][...] = res


def _combine(l, x_all, y_tok, route, mod):
    split = l == DEPTH - 1
    if split:
        out_specs = [pl.BlockSpec((TM, D), lambda i: (_ctx_tile(i), 0)),
                     pl.BlockSpec((TM, D), lambda i: (_lat_tile(i), 0))]
        out_shape = [jax.ShapeDtypeStruct((T_CTX, D), F32), jax.ShapeDtypeStruct((T_LAT, D), F32)]
    else:
        out_specs = pl.BlockSpec((TM, D), lambda i: (i, 0))
        out_shape = jax.ShapeDtypeStruct((T_ALL, D), F32)
    return pl.pallas_call(
        functools.partial(_combine_kernel, split),
        grid=(N_TILES,),
        in_specs=[
            pl.BlockSpec((TM, D), lambda i: (i, 0)),
            pl.BlockSpec((TOP_K, TM, DW), lambda i: (0, i, 0)),
            pl.BlockSpec((TM, LANE), lambda i: (i, 0)),
            pl.BlockSpec((None, None, 1, 6 * D), lambda i: (l, _mod_row(i), 0, 0)),
        ],
        out_specs=out_specs,
        out_shape=out_shape,
        compiler_params=pltpu.CompilerParams(
            dimension_semantics=("arbitrary",), vmem_limit_bytes=VMEM_LIMIT),
        name="combine_split" if split else "combine",
    )(x_all, y_tok, route, mod)


SC_CORES, SC_SUBCORES = 2, 16
SC_WORKERS = SC_CORES * SC_SUBCORES
SC_CHUNK = 128
SC_CHUNKS_PER_WORKER = T_ALL // SC_CHUNK // SC_WORKERS


def _sc_mesh():
    return plsc.VectorSubcoreMesh(core_axis_name="c", subcore_axis_name="s")


def _sc_scratch():
    return [pltpu.VMEM((TOP_K, SC_CHUNK), jnp.int32), pltpu.VMEM((SC_CHUNK, DW), jnp.int32),
            pltpu.SemaphoreType.DMA]


def _dispatch_rows(h2p, dest):
    @functools.partial(pl.kernel, mesh=_sc_mesh(), out_type=jax.ShapeDtypeStruct((MOE_ROWS, DW), jnp.int32),
                       scratch_types=_sc_scratch(), name="dispatch_rows")
    def run(h_hbm, d_hbm, o_hbm, idx_v, rows_v, sem):
        wid = lax.axis_index("s") * SC_CORES + lax.axis_index("c")
        for j in range(SC_CHUNKS_PER_WORKER):
            c = wid * SC_CHUNKS_PER_WORKER + j
            pltpu.sync_copy(d_hbm.at[c], idx_v)
            pltpu.sync_copy(h_hbm.at[pl.ds(c * SC_CHUNK, SC_CHUNK)], rows_v)
            copies = [pltpu.async_copy(rows_v, o_hbm.at[idx_v.at[k]], sem) for k in range(TOP_K)]
            for cp in copies:
                cp.wait()

    return run(h2p, dest)


def _gather_rows(y, dest):
    @functools.partial(pl.kernel, mesh=_sc_mesh(), out_type=jax.ShapeDtypeStruct((TOP_K, T_ALL, DW), jnp.int32),
                       scratch_types=_sc_scratch(), name="gather_rows")
    def run(y_hbm, d_hbm, o_hbm, idx_v, rows_v, sem):
        wid = lax.axis_index("s") * SC_CORES + lax.axis_index("c")
        for j in range(SC_CHUNKS_PER_WORKER):
            c = wid * SC_CHUNKS_PER_WORKER + j
            pltpu.sync_copy(d_hbm.at[c], idx_v)
            for k in range(TOP_K):
                pltpu.async_copy(y_hbm.at[idx_v.at[k]], rows_v, sem).wait()
                pltpu.sync_copy(rows_v, o_hbm.at[k, pl.ds(c * SC_CHUNK, SC_CHUNK)])

    return run(y, dest)


def _constants():
    lane64 = np.arange(4 * HD)
    seg64 = (lane64[:, None] // HD == lane64[None, :] // HD).astype(np.float32)
    lane96 = np.arange(4 * BP)
    real = lane96 % BP < B_QK
    seg96 = ((lane96[:, None] // BP == lane96[None, :] // BP) & real[:, None] & real[None, :]).astype(np.float32)
    place = np.zeros((LANE, 4 * BP), np.float32)
    for hh in range(4):
        place[np.arange(B_ROPE), hh * BP + B_NOPE + np.arange(B_ROPE)] = 1.0

    def angles(rot_dim):
        pos = np.arange(LAT_LEN)
        rows = (pos // GRID_W).astype(np.float32)
        cols = (pos % GRID_W).astype(np.float32)
        axis_dim = rot_dim // 2
        inv = jnp.asarray(ROPE_THETA, F32) ** (-jnp.arange(0, axis_dim, 2, dtype=F32) / axis_dim)
        ang = jnp.concatenate([rows[:, None] * inv, cols[:, None] * inv], axis=-1)
        return jnp.cos(ang), jnp.sin(ang)

    def head_tables(rot_dim):
        cos, sin = angles(rot_dim)
        q = rot_dim // 4
        cr, cc, sr, sc = cos[:, :q], cos[:, q:], sin[:, :q], sin[:, q:]
        return (jnp.concatenate([cr, cr, cc, cc], axis=-1), jnp.concatenate([-sr, sr, -sc, sc], axis=-1))

    c64, s64 = head_tables(HD)
    cos64 = jnp.tile(c64, (1, 4))
    sin64 = jnp.tile(s64, (1, 4))
    c32, s32 = head_tables(B_ROPE)
    ones = jnp.ones((LAT_LEN, B_NOPE), F32)
    zeros = jnp.zeros((LAT_LEN, B_NOPE), F32)
    padz = jnp.zeros((LAT_LEN, BP - B_QK), F32)
    cos96 = jnp.tile(jnp.concatenate([ones, c32, padz], axis=-1), (1, 4))
    sin96 = jnp.tile(jnp.concatenate([zeros, s32, padz], axis=-1), (1, 4))
    first64 = ((lane64 % 32) < 16).astype(np.float32)[None, :]
    first96 = (((lane96 % BP) % 16) < 8).astype(np.float32)[None, :]
    tri = (np.arange(TM)[:, None] < np.arange(TM)[None, :]).astype(np.float32)
    return dict(seg64=jnp.asarray(seg64, BF16), seg96=jnp.asarray(seg96, BF16), place=jnp.asarray(place, BF16),
                tri=jnp.asarray(tri, BF16),
                cos64=cos64, sin64=sin64, cos96=cos96, sin96=sin96,
                first64=jnp.asarray(first64), first96=jnp.asarray(first96))


def _pad_heads(w, per_head, width):
    lead = w.shape[:-1]
    w = w.reshape(lead + (4, per_head))
    return jnp.pad(w, ((0, 0),) * (len(lead) + 1) + ((0, width - per_head),)).reshape(lead + (4 * width,))


def _weights(norm1_g, norm2_g, w_in, a_q_g, a_k_g, b_cq_g, b_ckv_g, w_uq, w_ukv, b_q_g, b_k_g,
             c_q_g, c_k_g, w_out, router_w, router_b):
    o = np.cumsum((0, 256, 128, 128, 384, 256, 32, 256, 128, 128))
    seg = lambda k: w_in[:, :, o[k]:o[k + 1]]
    w_in_r = jnp.concatenate([seg(0), seg(1), seg(2), seg(3), seg(4), seg(6), seg(7), seg(8), seg(5),
                              jnp.zeros((DEPTH, D, LANE - B_ROPE), F32)], axis=-1).astype(BF16)
    ukv = w_ukv.reshape(DEPTH, B_KV_LORA, 4, B_NOPE + B_V)
    w_uk = _pad_heads(ukv[..., :B_NOPE].reshape(DEPTH, B_KV_LORA, 4 * B_NOPE), B_NOPE, BP)
    w_uv = ukv[..., B_NOPE:].reshape(DEPTH, B_KV_LORA, 4 * B_V)
    rw_hi = router_w.astype(BF16)
    rw_lo = (router_w - rw_hi.astype(F32)).astype(BF16)
    row = lambda v: v[:, None, :]
    tile = lambda v, n: row(jnp.tile(v, (1, n)))
    return dict(
        n1g=row(norm1_g), n2g=row(norm2_g), w_in=w_in_r,
        aqg=tile(a_q_g, 4), akg=tile(a_k_g, 2), cqg=tile(c_q_g, 4), ckg=tile(c_k_g, 2),
        bcqg=row(b_cq_g), bckvg=row(b_ckv_g),
        bqg=_pad_heads(tile(b_q_g, 4), B_QK, BP), bkg=_pad_heads(tile(b_k_g, 4), B_QK, BP),
        w_uq=_pad_heads(w_uq, B_QK, BP).astype(BF16), w_uk=w_uk.astype(BF16), w_uv=w_uv.astype(BF16),
        w_out=w_out.astype(BF16),
        rwt=jnp.concatenate([jnp.swapaxes(rw_hi, 1, 2), jnp.swapaxes(rw_lo, 1, 2)], axis=1),
        rb=router_b[:, :, None])


def _moe(l, x_all, h2p, gate_slab, route, counts, mod, w_gu, b_gu, w_dn, b_dn):
    counts = counts[:, 0].astype(jnp.int32)
    padded = (counts + MOE_TM - 1) // MOE_TM * MOE_TM
    pend = jnp.cumsum(padded)
    pstart = pend - padded
    n_used = (pend[-1] // MOE_TM).astype(jnp.int32)
    tiles = jnp.minimum(jnp.arange(MOE_TILES, dtype=jnp.int32), n_used - 1)
    tile_expert = jnp.sum((pend[None, :] <= tiles[:, None] * MOE_TM).astype(jnp.int32), axis=1)
    tile_expert = jnp.minimum(tile_expert, N_EXPERTS - 1)
    group_end_tile = jnp.sum(jnp.where(tile_expert[:, None] == jnp.arange(N_EXPERTS)[None, :], pend[None, :], 0),
                             axis=1) // MOE_TM
    after = jnp.sum((pend[None, :] <= group_end_tile[:, None] * MOE_TM).astype(jnp.int32), axis=1)
    next_expert = jnp.where(group_end_tile < n_used, jnp.minimum(after, N_EXPERTS - 1), -1).astype(jnp.int32)
    e = route[TOP_K:2 * TOP_K].astype(jnp.int32)
    rank = route[2 * TOP_K:3 * TOP_K].astype(jnp.int32)
    start = jnp.sum(jnp.where(e[:, :, None] == jnp.arange(N_EXPERTS)[None, None, :], pstart[None, None, :], 0), axis=-1)
    dest = (start + rank).reshape(TOP_K, T_ALL // SC_CHUNK, SC_CHUNK).transpose(1, 0, 2)
    x_sorted = _dispatch_rows(h2p, dest)
    y = _experts(l, tile_expert, n_used.reshape(1), next_expert, x_sorted, w_gu, b_gu, w_dn, b_dn)
    y_tok = _gather_rows(y, dest)
    return _combine(l, x_all, y_tok, gate_slab, mod)


def kernel(x_prompt, x_sample, cache_a_k, cache_a_v, cache_b_ckv, cache_b_krope, cache_c_k, cache_c_v, c, c_ctx,
           norm1_g, norm2_g, w_mod, b_mod, w_in, a_q_g, a_k_g, a_sink, b_cq_g, b_ckv_g, w_uq, w_ukv, b_q_g, b_k_g,
           c_q_g, c_k_g, w_out, router_w, router_b, w_gu, b_gu, w_dn, b_dn):
    consts = _constants()
    cond = jnp.concatenate([c_ctx[None, :], c, jnp.zeros((3, D), F32)], axis=0)
    mod = _modulation(cond, w_mod, b_mod).reshape(DEPTH, 8, 1, 6 * D)
    x_ctx, x_lat, x_lat_off = x_prompt.reshape(T_CTX, D), x_sample.reshape(T_LAT, D), 0

    lw = _weights(norm1_g, norm2_g, w_in, a_q_g, a_k_g, b_cq_g, b_ckv_g, w_uq, w_ukv, b_q_g, b_k_g,
                  c_q_g, c_k_g, w_out, router_w, router_b)
    ckb, cvb = _cache_kv(cache_b_ckv, cache_b_krope, lw, consts)
    merge_heads = lambda a: a.reshape(N_LAT_SEQ, DEPTH, PAST, 2 * HD).astype(BF16)
    cache = dict(ka=merge_heads(cache_a_k), va=merge_heads(cache_a_v), kb=ckb, vb=cvb,
                 kc=merge_heads(cache_c_k), vc=merge_heads(cache_c_v))

    states = ()
    names = ["qa", "ka", "va", "qb", "kb", "vb", "qc", "kc", "vc"]
    for l in range(DEPTH):
        outs = _projection(False, l, x_ctx, 0, mod, lw, consts, states)
        p_ctx = dict(zip(names, outs[:9]))
        states = tuple(outs[9:])
        p_lat = dict(zip(names, _projection(True, l, x_lat, x_lat_off, mod, lw, consts)))
        mix_ctx = _attention_ctx(l, a_sink, p_ctx)
        mix_lat = _attention_lat(l, a_sink, p_lat, cache)
        x_mid, h2p, gate_slab, route, counts = _post_attention(l, mix_ctx, mix_lat, x_ctx, x_lat, x_lat_off, mod, lw,
                                                               consts)
        out = _moe(l, x_mid, h2p, gate_slab, route, counts, mod, w_gu, b_gu, w_dn, b_dn)
        if l < DEPTH - 1:
            x_ctx, x_lat, x_lat_off = out, out, N_CTX_TILES
    y_ctx, y_lat = out

    return (y_ctx.reshape(N_CTX_SEQ, CTX_LEN, D), y_lat.reshape(N_LAT_SEQ, LAT_LEN, D)) + states
```

```python
import functools

import jax
import jax.numpy as jnp
import numpy as np
from jax import lax
from jax.experimental import pallas as pl
from jax.experimental.pallas import tpu as pltpu
from jax.experimental.pallas import tpu_sc as plsc

F32 = jnp.float32
BF16 = jnp.bfloat16

D = 1024
DEPTH = 2
N_CTX_SEQ, CTX_LEN = 16, 256
N_LAT_SEQ, LAT_LEN = 4, 2048
PAST = 512
T_CTX = N_CTX_SEQ * CTX_LEN
T_LAT = N_LAT_SEQ * LAT_LEN
T_ALL = T_CTX + T_LAT
GRID_W = 64
HD = 64
WINDOW = 128
B_NOPE, B_ROPE, B_V = 64, 32, 128
B_QK = B_NOPE + B_ROPE
B_Q_LORA, B_KV_LORA = 384, 256
N_EXPERTS, TOP_K = 32, 4
D_FF = 1024
SWIGLU_LIMIT = 7.0
SWIGLU_ALPHA = 1.702
ROPE_THETA = 10000.0
EPS = 1e-6
NEG_INF = -1e30

TM = 256
LANE = 128
BP = 128
N_CTX_TILES = T_CTX // TM
N_LAT_TILES = T_LAT // TM
N_TILES = T_ALL // TM
LAT_TILES_PER_SEQ = LAT_LEN // TM
MOE_TM = 256
FF_CHUNK = 256
MOE_TILES = T_ALL * TOP_K // MOE_TM + N_EXPERTS
MOE_ROWS = MOE_TILES * MOE_TM
DW = D // 2
VMEM_LIMIT = 56 * 1024 * 1024

C_QA, C_KA, C_VA, C_CQ, C_CKV, C_QC, C_KC, C_VC, C_KR, C_END = (
    0, 256, 384, 512, 896, 1152, 1408, 1536, 1664, 1792)


def _dot(a, b):
    return jnp.dot(a, b, preferred_element_type=F32)


def _dot_t(a, b):
    return lax.dot_general(a, b, (((1,), (1,)), ((), ())), preferred_element_type=F32)


def _rms(x, g):
    return x * lax.rsqrt(jnp.mean(x * x, axis=-1, keepdims=True) + EPS) * g


def _seg_norm(x, seg, g, n):
    ss = _dot((x * x).astype(BF16), seg)
    return x * lax.rsqrt(ss * (1.0 / n) + EPS) * g


def _rope(x, cos, sin, first, sh):
    w = x.shape[1]
    fwd = pltpu.roll(x, w - sh, 1)
    bwd = pltpu.roll(x, sh, 1)
    return x * cos + jnp.where(first > 0.5, fwd, bwd) * sin


MOD_BN = 1536


def _mod_kernel(c_ref, w_ref, b_ref, o_ref):
    c = c_ref[...]
    s = (c * jax.nn.sigmoid(c)).astype(BF16)
    o_ref[...] = _dot(s, w_ref[...].astype(BF16)) + b_ref[...]


def _modulation(cond, w_mod, b_mod):
    return pl.pallas_call(
        _mod_kernel,
        grid=(DEPTH, 6 * D // MOD_BN),
        in_specs=[
            pl.BlockSpec((8, D), lambda l, j: (0, 0)),
            pl.BlockSpec((None, D, MOD_BN), lambda l, j: (l, 0, j)),
            pl.BlockSpec((None, 1, MOD_BN), lambda l, j: (l, 0, j)),
        ],
        out_specs=pl.BlockSpec((None, 8, MOD_BN), lambda l, j: (l, 0, j)),
        out_shape=jax.ShapeDtypeStruct((DEPTH, 8, 6 * D), F32),
        compiler_params=pltpu.CompilerParams(
            dimension_semantics=("arbitrary", "arbitrary"), vmem_limit_bytes=VMEM_LIMIT),
        name="modulation",
    )(cond, w_mod, b_mod.reshape(DEPTH, 1, 6 * D))


def _proj_kernel(is_lat, n_aliased, *refs):
    (x_ref, mod_ref, n1g_ref, win_ref, seg64_ref, seg96_ref, aqg_ref, akg_ref, cqg_ref, ckg_ref,
     bcqg_ref, bckvg_ref, bqg_ref, bkg_ref, wuq_ref, wuk_ref, wuv_ref, plc_ref) = refs[:18]
    refs = refs[18 + n_aliased:]
    if is_lat:
        (cos64_ref, sin64_ref, cos96_ref, sin96_ref, f64_ref, f96_ref) = refs[:6]
        refs = refs[6:]
    (qa_ref, ka_ref, va_ref, qb_ref, kb_ref, vb_ref, qc_ref, kc_ref, vc_ref) = refs[:9]
    refs = refs[9:]
    if not is_lat:
        if n_aliased == 0:
            for r in refs:
                r[1:] = jnp.zeros((DEPTH - 1,) + r.shape[1:], F32)
            refs = [r.at[0] for r in refs]
        (kas_ref, vas_ref, ckvs_ref, krs_ref, kcs_ref, vcs_ref) = refs

    x = x_ref[...]
    m = mod_ref[...]
    sh1, sc1 = m[:, 0:D], m[:, D:2 * D]
    h = (_rms(x, n1g_ref[...]) * (1.0 + sc1) + sh1).astype(BF16)

    def proj(a, b):
        return _dot(h, win_ref[:, a:b])

    def rope64(t):
        wd = t.shape[1]
        return _rope(t, cos64_ref[:, :wd], sin64_ref[:, :wd], f64_ref[:, :wd], 16) if is_lat else t

    def rope96(t):
        return _rope(t, cos96_ref[...], sin96_ref[...], f96_ref[...], 8) if is_lat else t

    seg64 = seg64_ref[...]
    seg64h = seg64_ref[0:2 * HD, 0:2 * HD]
    seg96 = seg96_ref[...]

    def seg_sum(t, seg):
        return _dot((t * t).astype(BF16), seg)

    def seg_finish(t, ss, g, n):
        return t * lax.rsqrt(ss * (1.0 / n) + EPS) * g

    p_cq, p_ckv, p_kr = proj(C_CQ, C_CKV), proj(C_CKV, C_QC), proj(C_KR, C_END)
    p_qa, p_ka, p_qc, p_kc = proj(C_QA, C_KA), proj(C_KA, C_VA), proj(C_QC, C_KC), proj(C_KC, C_VC)
    p_va, p_vc = proj(C_VA, C_CQ), proj(C_VC, C_KR)

    cq = _rms(p_cq, bcqg_ref[...]).astype(BF16)
    ckv = _rms(p_ckv, bckvg_ref[...])
    ckv16 = ckv.astype(BF16)
    u_q = _dot(cq, wuq_ref[...])
    u_k = _dot(ckv16, wuk_ref[...]) + _dot(p_kr.astype(BF16), plc_ref[...])
    u_v = _dot(ckv16, wuv_ref[...])
    ss_qa, ss_ka = seg_sum(p_qa, seg64), seg_sum(p_ka, seg64h)
    ss_qc, ss_kc = seg_sum(p_qc, seg64), seg_sum(p_kc, seg64h)
    ss_qb, ss_kb = seg_sum(u_q, seg96), seg_sum(u_k, seg96)

    def store_kv_state(ref, t):
        for kv in range(2):
            ref[:, kv, :] = t[:, kv * HD:(kv + 1) * HD]

    if not is_lat:
        store_kv_state(vas_ref, p_va)
        store_kv_state(vcs_ref, p_vc)
        ckvs_ref[...] = ckv
        krs_ref[...] = p_kr[:, 0:B_ROPE]
    va_ref[...] = p_va.astype(BF16)
    vc_ref[...] = p_vc.astype(BF16)
    vb_ref[...] = u_v.astype(BF16)

    t = seg_finish(p_qa, ss_qa, aqg_ref[...], HD)
    qa_ref[...] = (rope64(t) * HD ** -0.5).astype(BF16)
    t = seg_finish(p_ka, ss_ka, akg_ref[...], HD)
    if not is_lat:
        store_kv_state(kas_ref, t)
    ka_ref[...] = rope64(t).astype(BF16)
    t = seg_finish(p_qc, ss_qc, cqg_ref[...], HD)
    qc_ref[...] = (rope64(t) * HD ** -0.5).astype(BF16)
    t = seg_finish(p_kc, ss_kc, ckg_ref[...], HD)
    if not is_lat:
        store_kv_state(kcs_ref, t)
    kc_ref[...] = rope64(t).astype(BF16)
    t = seg_finish(u_q, ss_qb, bqg_ref[...], B_QK)
    qb_ref[...] = (rope96(t) * B_QK ** -0.5).astype(BF16)
    t = seg_finish(u_k, ss_kb, bkg_ref[...], B_QK)
    kb_ref[...] = rope96(t).astype(BF16)


def _full(shape):
    nd = len(shape)
    return pl.BlockSpec(shape, lambda i: (0,) * nd)


STATE_TAILS = ((2, HD), (2, HD), (B_KV_LORA,), (B_ROPE,), (2, HD), (2, HD))


def _layer_spec(a, l):
    nd = a.ndim - 1
    return pl.BlockSpec((None,) + a.shape[1:], lambda *_: (l,) + (0,) * nd)


VEC_ROWS = dict(n1g=(0, D), n2g=(1, D), aqg=(2, 4 * HD), akg=(3, 2 * HD), cqg=(4, 4 * HD), ckg=(5, 2 * HD),
                bcqg=(6, B_Q_LORA), bckvg=(7, B_KV_LORA), bqg=(8, 4 * BP), bkg=(9, 4 * BP))


def _vec_spec(l, name):
    row, width = VEC_ROWS[name]
    return pl.BlockSpec((None, None, 1, width), lambda *_: (l, row, 0, 0))


def _projection(is_lat, l, x_src, x_off, mod, lw, consts, prev_states=()):
    assert is_lat or bool(prev_states) == (l > 0)
    n_tiles = N_LAT_TILES if is_lat else N_CTX_TILES
    t_rows = n_tiles * TM
    if is_lat:
        mod_map = lambda i: (l, 1 + i // LAT_TILES_PER_SEQ, 0, 0)
    else:
        mod_map = lambda i: (l, 0, 0, 0)
    vec = lw["vec"]
    gain = lambda name: (vec, _vec_spec(l, name))
    whole = lambda a: (a, _full(a.shape))
    layer = lambda a: (a, _layer_spec(a, l))
    pairs = [(x_src, pl.BlockSpec((TM, D), lambda i: (i + x_off, 0))),
             (mod, pl.BlockSpec((None, None, 1, 6 * D), mod_map)),
             gain("n1g"), layer(lw["w_in"]), whole(consts["seg64"]), whole(consts["seg96"]),
             gain("aqg"), gain("akg"), gain("cqg"), gain("ckg"), gain("bcqg"), gain("bckvg"), gain("bqg"), gain("bkg"),
             layer(lw["w_uq"]), layer(lw["w_uk"]), layer(lw["w_uv"]), whole(consts["place"])]
    ins = [a for a, _ in pairs]
    in_specs = [s for _, s in pairs]
    n_plain = len(ins)
    ins += list(prev_states)
    in_specs += [pl.BlockSpec(memory_space=pl.ANY) for _ in prev_states]
    if is_lat:
        tabs = [consts["cos64"], consts["sin64"], consts["cos96"], consts["sin96"]]
        ins += tabs + [consts["first64"], consts["first96"]]
        in_specs += [pl.BlockSpec((TM, a.shape[1]), lambda i: (i % LAT_TILES_PER_SEQ, 0)) for a in tabs]
        in_specs += [_full(consts["first64"].shape), _full(consts["first96"].shape)]
    widths = [4 * HD, 2 * HD, 2 * HD, 4 * BP, 4 * BP, 4 * B_V, 4 * HD, 2 * HD, 2 * HD]
    out_shape = [jax.ShapeDtypeStruct((t_rows, w), BF16) for w in widths]
    out_specs = [pl.BlockSpec((TM, w), lambda i: (i, 0)) for w in widths]
    if not is_lat:
        for tail in STATE_TAILS:
            zeros = (0,) * len(tail)
            out_shape.append(jax.ShapeDtypeStruct((N_CTX_SEQ, DEPTH, CTX_LEN) + tail, F32))
            if prev_states:
                out_specs.append(pl.BlockSpec((None, None, CTX_LEN) + tail, lambda i, z=zeros: (i, l, 0) + z))
            else:
                out_specs.append(pl.BlockSpec((None, DEPTH, CTX_LEN) + tail, lambda i, z=zeros: (i, 0, 0) + z))
    aliases = {n_plain + k: len(widths) + k for k in range(len(prev_states))}
    return pl.pallas_call(
        functools.partial(_proj_kernel, is_lat, len(prev_states)),
        grid=(n_tiles,),
        in_specs=in_specs,
        out_specs=out_specs,
        out_shape=out_shape,
        input_output_aliases=aliases,
        compiler_params=pltpu.CompilerParams(
            dimension_semantics=("arbitrary",), vmem_limit_bytes=VMEM_LIMIT),
        name="proj_lat" if is_lat else "proj_ctx",
    )(*ins)


def _cache_kv_kernel(ckv_ref, kr_ref, seg96_ref, bkg_ref, wuk_ref, wuv_ref, plc_ref, kb_ref, vb_ref):
    ckv16 = ckv_ref[...].astype(BF16)
    vb_ref[...] = _dot(ckv16, wuv_ref[...]).astype(BF16)
    kpre = _dot(ckv16, wuk_ref[...]) + _dot(kr_ref[...].astype(BF16), plc_ref[0:B_ROPE, :])
    kb_ref[...] = _seg_norm(kpre, seg96_ref[...], bkg_ref[...], B_QK).astype(BF16)


def _cache_kv(cache_b_ckv, cache_b_krope, lw, consts):
    def c_map(l, r):
        return (r // (PAST // TM), l, r % (PAST // TM), 0)

    def w_spec(a):
        return pl.BlockSpec((None,) + a.shape[1:], lambda l, r: (l,) + (0,) * (a.ndim - 1))

    def s_spec(a):
        return pl.BlockSpec(a.shape, lambda l, r: (0,) * a.ndim)

    rows = N_LAT_SEQ * PAST
    return pl.pallas_call(
        _cache_kv_kernel,
        grid=(DEPTH, rows // TM),
        in_specs=[pl.BlockSpec((None, None, TM, B_KV_LORA), c_map), pl.BlockSpec((None, None, TM, B_ROPE), c_map),
                  s_spec(consts["seg96"]),
                  pl.BlockSpec((None, None, 1, 4 * BP), lambda l, r: (l, VEC_ROWS["bkg"][0], 0, 0)),
                  w_spec(lw["w_uk"]), w_spec(lw["w_uv"]),
                  s_spec(consts["place"])],
        out_specs=[pl.BlockSpec((None, TM, 4 * BP), lambda l, r: (l, r, 0)),
                   pl.BlockSpec((None, TM, 4 * B_V), lambda l, r: (l, r, 0))],
        out_shape=[jax.ShapeDtypeStruct((DEPTH, rows, 4 * BP), BF16),
                   jax.ShapeDtypeStruct((DEPTH, rows, 4 * B_V), BF16)],
        compiler_params=pltpu.CompilerParams(
            dimension_semantics=("arbitrary", "arbitrary"), vmem_limit_bytes=VMEM_LIMIT),
        name="cache_kv",
    )(cache_b_ckv, cache_b_krope, consts["seg96"], lw["vec"], lw["w_uk"], lw["w_uv"], consts["place"])


def _scores(q, segs):
    scores = []
    for k, _, mask in segs:
        s = _dot_t(q, k)
        if mask is not None:
            s = jnp.where(mask, s, NEG_INF)
        scores.append(s)
    return scores


def _softmax_pv(scores, segs, sink):
    m = None
    for s in scores:
        sm = jnp.max(s, axis=-1, keepdims=True)
        m = sm if m is None else jnp.maximum(m, sm)
    if sink is not None:
        m = jnp.maximum(m, sink)
    acc = None
    den = None
    for s, (_, v, _) in zip(scores, segs):
        e = jnp.exp(s - m)
        d = jnp.sum(e, axis=-1, keepdims=True)
        o = _dot(e.astype(BF16), v)
        acc = o if acc is None else acc + o
        den = d if den is None else den + d
    if sink is not None:
        den = den + jnp.exp(sink - m)
    return acc / den


def _run_heads(jobs, o_ref):
    nxt = _scores(jobs[0][0](), jobs[0][1]())
    for n, (_, load_segs, sink, col) in enumerate(jobs):
        scores = nxt
        if n + 1 < len(jobs):
            nxt = _scores(jobs[n + 1][0](), jobs[n + 1][1]())
        o = _softmax_pv(scores, load_segs(), sink)
        o_ref[:, col:col + o.shape[1]] = o.astype(BF16)


OUT_B = 4 * HD
OUT_C = 4 * HD + 4 * B_V


def _head_job(q_ref, q_cols, seg_refs, kv_cols, v_cols, sink, col, rows=None, masks=None):
    def load_segs():
        segs = []
        for n, (k_ref, v_ref) in enumerate(seg_refs):
            r = rows if (rows is not None and n == 0) else slice(None)
            segs.append((k_ref[r, kv_cols], v_ref[r, v_cols], None if masks is None else masks[n]))
        return segs

    return (lambda: q_ref[:, q_cols]), load_segs, sink, col


def _attn_ctx_kernel(l, sink_ref, qa_ref, ka_ref, va_ref, qb_ref, kb_ref, vb_ref, qc_ref, kc_ref, vc_ref, o_ref):
    jobs = []
    for h in range(4):
        cs = slice(h // 2 * HD, (h // 2 + 1) * HD)
        jobs.append(_head_job(qa_ref, slice(h * HD, (h + 1) * HD), [(ka_ref, va_ref)], cs, cs, sink_ref[l, h],
                              h * HD))
    for h in range(4):
        ks, vs = slice(h * BP, (h + 1) * BP), slice(h * B_V, (h + 1) * B_V)
        jobs.append(_head_job(qb_ref, ks, [(kb_ref, vb_ref)], ks, vs, None, OUT_B + h * B_V))
    for h in range(4):
        cs = slice(h // 2 * HD, (h // 2 + 1) * HD)
        jobs.append(_head_job(qc_ref, slice(h * HD, (h + 1) * HD), [(kc_ref, vc_ref)], cs, cs, None, OUT_C + h * HD))
    _run_heads(jobs, o_ref)


def _attention_ctx(l, sink, p):
    names = ["qa", "ka", "va", "qb", "kb", "vb", "qc", "kc", "vc"]
    ins = [p[n] for n in names]
    in_specs = [pl.BlockSpec(memory_space=pltpu.SMEM)]
    in_specs += [pl.BlockSpec((CTX_LEN, a.shape[1]), lambda i: (i, 0)) for a in ins]
    return pl.pallas_call(
        functools.partial(_attn_ctx_kernel, l),
        grid=(N_CTX_SEQ,),
        in_specs=in_specs,
        out_specs=pl.BlockSpec((CTX_LEN, D), lambda i: (i, 0)),
        out_shape=jax.ShapeDtypeStruct((T_CTX, D), BF16),
        compiler_params=pltpu.CompilerParams(
            dimension_semantics=("arbitrary",), vmem_limit_bytes=VMEM_LIMIT),
        name="attn_ctx",
    )(sink, *ins)


WIN_SPAN = TM + 2 * WINDOW


def _attn_lat_kernel(l, sink_ref, qa_ref, qb_ref, qc_ref, ka_ref, va_ref, kb_ref, vb_ref, kc_ref, vc_ref,
                     cka_ref, cva_ref, ckb_ref, cvb_ref, ckc_ref, cvc_ref, o_ref):
    qi = pl.program_id(1)
    ws = pl.multiple_of(jnp.clip(qi * TM - WINDOW, 0, LAT_LEN - WIN_SPAN), WINDOW)
    qpos = qi * TM + lax.broadcasted_iota(jnp.int32, (TM, WIN_SPAN), 0)
    kpos = ws + lax.broadcasted_iota(jnp.int32, (TM, WIN_SPAN), 1)
    band = jnp.abs(qpos - kpos) <= WINDOW
    jobs = []
    for h in range(4):
        cs = slice(h // 2 * HD, (h // 2 + 1) * HD)
        jobs.append(_head_job(qa_ref, slice(h * HD, (h + 1) * HD), [(ka_ref, va_ref), (cka_ref, cva_ref)], cs, cs,
                              sink_ref[l, h], h * HD, rows=pl.ds(ws, WIN_SPAN), masks=(band, None)))
    for h in range(4):
        ks, vs = slice(h * BP, (h + 1) * BP), slice(h * B_V, (h + 1) * B_V)
        jobs.append(_head_job(qb_ref, ks, [(kb_ref, vb_ref), (ckb_ref, cvb_ref)], ks, vs, None, OUT_B + h * B_V))
    for h in range(4):
        cs = slice(h // 2 * HD, (h // 2 + 1) * HD)
        jobs.append(_head_job(qc_ref, slice(h * HD, (h + 1) * HD), [(kc_ref, vc_ref), (ckc_ref, cvc_ref)], cs, cs,
                              None, OUT_C + h * HD))
    _run_heads(jobs, o_ref)


def _attention_lat(l, sink, p, cache):
    q_ins = [p["qa"], p["qb"], p["qc"]]
    kv_ins = [p[n] for n in ("ka", "va", "kb", "vb", "kc", "vc")]
    c_names = ("ka", "va", "kb", "vb", "kc", "vc")
    c_ins = [cache[n] for n in c_names]
    in_specs = [pl.BlockSpec(memory_space=pltpu.SMEM)]
    in_specs += [pl.BlockSpec((TM, a.shape[1]), lambda b, i: (b * LAT_TILES_PER_SEQ + i, 0)) for a in q_ins]
    in_specs += [pl.BlockSpec((LAT_LEN, a.shape[1]), lambda b, i: (b, 0)) for a in kv_ins]
    for n, a in zip(c_names, c_ins):
        if n in ("kb", "vb"):
            in_specs.append(pl.BlockSpec((None, PAST, a.shape[-1]), lambda b, i: (l, b, 0)))
        else:
            in_specs.append(pl.BlockSpec((None, None, PAST, a.shape[-1]), lambda b, i: (b, l, 0, 0)))
    return pl.pallas_call(
        functools.partial(_attn_lat_kernel, l),
        grid=(N_LAT_SEQ, LAT_TILES_PER_SEQ),
        in_specs=in_specs,
        out_specs=pl.BlockSpec((TM, D), lambda b, i: (b * LAT_TILES_PER_SEQ + i, 0)),
        out_shape=jax.ShapeDtypeStruct((T_LAT, D), BF16),
        compiler_params=pltpu.CompilerParams(
            dimension_semantics=("arbitrary", "arbitrary"), vmem_limit_bytes=VMEM_LIMIT),
        name="attn_lat",
    )(sink, *q_ins, *kv_ins, *c_ins)


def _pack_rows(x):
    half = x.shape[1] // 2
    r = x.astype(BF16).astype(F32)
    hi = lax.bitcast_convert_type(r[:, :half], jnp.int32)
    lo = lax.bitcast_convert_type(r[:, half:], jnp.int32)
    return jnp.bitwise_or(hi, lax.shift_right_logical(lo, 16))


def _unpack_rows(p):
    a = lax.bitcast_convert_type(jnp.bitwise_and(p, -65536), F32)
    b = lax.bitcast_convert_type(lax.shift_left(p, 16), F32)
    return jnp.concatenate([a, b], axis=1).astype(BF16)


ROUTE_ROWS = 16


def _post_kernel(mc_ref, ml_ref, xc_ref, xl_ref, mod_ref, n2g_ref, wout_ref, rwt_ref, rb_ref, tri_ref,
                 xo_ref, h2_ref, gate_ref, route_ref, cnt_ref, run_ref):
    i = pl.program_id(0)

    @pl.when(i == 0)
    def _():
        run_ref[...] = jnp.zeros_like(run_ref)
        cnt_ref[...] = jnp.zeros_like(cnt_ref)

    is_ctx = i < N_CTX_TILES
    mix = jnp.where(is_ctx, mc_ref[...], ml_ref[...])
    m = mod_ref[...]
    g1, sh2, sc2 = m[:, 2 * D:3 * D], m[:, 3 * D:4 * D], m[:, 4 * D:5 * D]
    x = jnp.where(is_ctx, xc_ref[...], xl_ref[...]) + g1 * _dot(mix, wout_ref[...])
    xo_ref[...] = x
    h2 = _rms(x, n2g_ref[...]) * (1.0 + sc2) + sh2
    hi = h2.astype(BF16)
    h2_ref[...] = _pack_rows(h2)
    lo = (h2 - hi.astype(F32)).astype(BF16)
    a = _dot_t(rwt_ref[...], hi)
    b = _dot_t(rwt_ref[0:N_EXPERTS, :], lo)
    logits = a[0:N_EXPERTS] + a[N_EXPERTS:] + b + rb_ref[...]
    eidx = lax.broadcasted_iota(jnp.int32, logits.shape, 0).astype(F32)
    work = logits
    vals, idxs, hots = [], [], []
    for _ in range(TOP_K):
        v = jnp.max(work, axis=0, keepdims=True)
        idx = jnp.min(jnp.where(work == v, eidx, float(N_EXPERTS)), axis=0, keepdims=True)
        hot = eidx == idx
        vals.append(v)
        idxs.append(idx)
        hots.append(hot)
        work = jnp.where(hot, -jnp.inf, work)
    es = [jnp.exp(v - vals[0]) for v in vals]
    den = es[0] + es[1] + es[2] + es[3]
    gates = [e / den for e in es]
    sel = jnp.where(hots[0] | hots[1] | hots[2] | hots[3], 1.0, 0.0)
    run = run_ref[:, 0:1]
    before = _dot(sel.astype(BF16), tri_ref[...]) + run
    run_new = jnp.broadcast_to(run + jnp.sum(sel, axis=1, keepdims=True), run_ref.shape)
    run_ref[...] = run_new
    cnt_ref[...] = run_new
    ranks = [jnp.sum(jnp.where(hots[k], before, 0.0), axis=0, keepdims=True) for k in range(TOP_K)]
    pad = [jnp.zeros((ROUTE_ROWS - 3 * TOP_K, TM), F32)]
    route_ref[...] = jnp.concatenate(gates + idxs + ranks + pad, axis=0)
    gate_ref[...] = jnp.concatenate(gates + [jnp.zeros((LANE - TOP_K, TM), F32)], axis=0).T


def _mod_row(i):
    return jnp.where(i < N_CTX_TILES, 0, 1 + (i - N_CTX_TILES) // LAT_TILES_PER_SEQ)


def _ctx_tile(i):
    return jnp.minimum(i, N_CTX_TILES - 1)


def _lat_tile(i):
    return jnp.maximum(i - N_CTX_TILES, 0)


def _post_attention(l, mix_ctx, mix_lat, x_ctx, x_lat, x_lat_off, mod, lw, consts):
    ins = [mix_ctx, mix_lat, x_ctx, x_lat, mod, lw["vec"], lw["w_out"], lw["rwt"], lw["rb"], consts["tri"]]
    in_specs = [
        pl.BlockSpec((TM, D), lambda i: (_ctx_tile(i), 0)),
        pl.BlockSpec((TM, D), lambda i: (_lat_tile(i), 0)),
        pl.BlockSpec((TM, D), lambda i: (_ctx_tile(i), 0)),
        pl.BlockSpec((TM, D), lambda i: (_lat_tile(i) + x_lat_off, 0)),
        pl.BlockSpec((None, None, 1, 6 * D), lambda i: (l, _mod_row(i), 0, 0)),
        _vec_spec(l, "n2g"),
    ] + [_layer_spec(a, l) for a in ins[6:9]] + [_full(consts["tri"].shape)]
    return pl.pallas_call(
        _post_kernel,
        grid=(N_TILES,),
        in_specs=in_specs,
        out_specs=[pl.BlockSpec((TM, D), lambda i: (i, 0)), pl.BlockSpec((TM, DW), lambda i: (i, 0)),
                   pl.BlockSpec((TM, LANE), lambda i: (i, 0)), pl.BlockSpec((ROUTE_ROWS, TM), lambda i: (0, i)),
                   pl.BlockSpec((N_EXPERTS, LANE), lambda i: (0, 0))],
        out_shape=[jax.ShapeDtypeStruct((T_ALL, D), F32), jax.ShapeDtypeStruct((T_ALL, DW), jnp.int32),
                   jax.ShapeDtypeStruct((T_ALL, LANE), F32), jax.ShapeDtypeStruct((ROUTE_ROWS, T_ALL), F32),
                   jax.ShapeDtypeStruct((N_EXPERTS, LANE), F32)],
        scratch_shapes=[pltpu.VMEM((N_EXPERTS, LANE), F32)],
        compiler_params=pltpu.CompilerParams(
            dimension_semantics=("arbitrary",), vmem_limit_bytes=VMEM_LIMIT),
        name="post_attn",
    )(*ins)


def _expert_kernel(l, te_ref, nu_ref, nxt_ref, x_ref, wgu_hbm, bgu_ref, wdn_hbm, bdn_ref, y_ref,
                   wgu32, wdn32, wgu16, wdn16, sem):
    i = pl.program_id(0)
    prev = te_ref[jnp.maximum(i - 1, 0)]
    new_expert = jnp.logical_or(i == 0, te_ref[i] != prev)

    def weight_copies(e):
        return (pltpu.make_async_copy(wgu_hbm.at[l, e], wgu32, sem.at[0]),
                pltpu.make_async_copy(wdn_hbm.at[l, e], wdn32, sem.at[1]))

    @pl.when(i == 0)
    def _():
        for cp in weight_copies(te_ref[0]):
            cp.start()

    @pl.when(new_expert)
    def _():
        cps = weight_copies(te_ref[i])
        cps[0].wait()
        wgu16[...] = wgu32[...].astype(BF16)
        cps[1].wait()
        wdn16[...] = wdn32[...].astype(BF16)

        @pl.when(nxt_ref[i] >= 0)
        def _():
            for cp in weight_copies(nxt_ref[i]):
                cp.start()

    @pl.when(i < nu_ref[0])
    def _():
        x = _unpack_rows(x_ref[...])

        def gate_up(a):
            b = a + FF_CHUNK
            g = _dot(x, wgu16[:, a:b]) + bgu_ref[:, a:b]
            up = _dot(x, wgu16[:, D_FF + a:D_FF + b]) + bgu_ref[:, D_FF + a:D_FF + b]
            return g, up

        def activate(g, up):
            g = jnp.minimum(g, SWIGLU_LIMIT)
            up = jnp.clip(up, -SWIGLU_LIMIT, SWIGLU_LIMIT)
            return ((up + 1.0) * (g * jax.nn.sigmoid(SWIGLU_ALPHA * g))).astype(BF16)

        starts = list(range(0, D_FF, FF_CHUNK))
        acc = None
        pre = gate_up(starts[0])
        for n, a in enumerate(starts):
            hid = activate(*pre)
            if n + 1 < len(starts):
                pre = gate_up(starts[n + 1])
            o = _dot(hid, wdn16[a:a + FF_CHUNK, :])
            acc = o if acc is None else acc + o
        y_ref[...] = _pack_rows(acc + bdn_ref[...])


def _experts(l, tile_expert, n_used, next_expert, x_sorted, w_gu, b_gu, w_dn, b_dn):
    def row_map(i, te, nu, nxt):
        return (jnp.minimum(i, nu[0] - 1), 0)

    def b_map(i, te, nu, nxt):
        return (l, te[i], 0, 0)

    grid_spec = pltpu.PrefetchScalarGridSpec(
        num_scalar_prefetch=3,
        grid=(MOE_TILES,),
        in_specs=[
            pl.BlockSpec((MOE_TM, DW), row_map),
            pl.BlockSpec(memory_space=pl.ANY),
            pl.BlockSpec((None, None, 1, 2 * D_FF), b_map),
            pl.BlockSpec(memory_space=pl.ANY),
            pl.BlockSpec((None, None, 1, D), b_map),
        ],
        out_specs=pl.BlockSpec((MOE_TM, DW), row_map),
        scratch_shapes=[pltpu.VMEM((D, 2 * D_FF), F32), pltpu.VMEM((D_FF, D), F32),
                        pltpu.VMEM((D, 2 * D_FF), BF16), pltpu.VMEM((D_FF, D), BF16),
                        pltpu.SemaphoreType.DMA((2,))],
    )
    return pl.pallas_call(
        functools.partial(_expert_kernel, l),
        grid_spec=grid_spec,
        out_shape=jax.ShapeDtypeStruct((MOE_ROWS, DW), jnp.int32),
        compiler_params=pltpu.CompilerParams(
            dimension_semantics=("arbitrary",), vmem_limit_bytes=VMEM_LIMIT),
        name="experts",
    )(tile_expert, n_used, next_expert, x_sorted, w_gu, b_gu.reshape(DEPTH, N_EXPERTS, 1, 2 * D_FF), w_dn,
      b_dn.reshape(DEPTH, N_EXPERTS, 1, D))


def _combine_kernel(x_ref, y_ref, route_ref, mod_ref, o_ref):
    g2 = mod_ref[...][:, 5 * D:6 * D]
    r = route_ref[...]
    acc = r[:, 0:1] * _unpack_rows(y_ref[0]).astype(F32)
    for k in range(1, TOP_K):
        acc = acc + r[:, k:k + 1] * _unpack_rows(y_ref[k]).astype(F32)
    o_ref[...] = x_ref[...] + g2 * acc


def _combine(l, first_tile, n_tiles, x_all, y_tok, gates, mod):
    return pl.pallas_call(
        _combine_kernel,
        grid=(n_tiles,),
        in_specs=[
            pl.BlockSpec((TM, D), lambda i: (i + first_tile, 0)),
            pl.BlockSpec((TOP_K, TM, DW), lambda i: (0, i, 0)),
            pl.BlockSpec((TM, LANE), lambda i: (i + first_tile, 0)),
            pl.BlockSpec((None, None, 1, 6 * D), lambda i: (l, _mod_row(i + first_tile), 0, 0)),
        ],
        out_specs=pl.BlockSpec((TM, D), lambda i: (i, 0)),
        out_shape=jax.ShapeDtypeStruct((n_tiles * TM, D), F32),
        compiler_params=pltpu.CompilerParams(
            dimension_semantics=("arbitrary",), vmem_limit_bytes=VMEM_LIMIT),
        name="combine",
    )(x_all, y_tok, gates, mod)


SC_CORES, SC_SUBCORES = 2, 16
SC_WORKERS = SC_CORES * SC_SUBCORES
SC_CHUNK = 128
SC_CHUNKS_PER_WORKER = T_ALL // SC_CHUNK // SC_WORKERS


def _sc_mesh():
    return plsc.VectorSubcoreMesh(core_axis_name="c", subcore_axis_name="s")


def _sc_scratch():
    return [pltpu.VMEM((TOP_K, SC_CHUNK), jnp.int32), pltpu.VMEM((SC_CHUNK, DW), jnp.int32),
            pltpu.SemaphoreType.DMA]


def _dispatch_rows(h2p, dest):
    @functools.partial(pl.kernel, mesh=_sc_mesh(), out_type=jax.ShapeDtypeStruct((MOE_ROWS, DW), jnp.int32),
                       scratch_types=_sc_scratch(), name="dispatch_rows")
    def run(h_hbm, d_hbm, o_hbm, idx_v, rows_v, sem):
        wid = lax.axis_index("s") * SC_CORES + lax.axis_index("c")
        for j in range(SC_CHUNKS_PER_WORKER):
            c = wid * SC_CHUNKS_PER_WORKER + j
            pltpu.sync_copy(d_hbm.at[c], idx_v)
            pltpu.sync_copy(h_hbm.at[pl.ds(c * SC_CHUNK, SC_CHUNK)], rows_v)
            copies = [pltpu.async_copy(rows_v, o_hbm.at[idx_v.at[k]], sem) for k in range(TOP_K)]
            for cp in copies:
                cp.wait()

    return run(h2p, dest)


def _gather_rows(y, dest, first_chunk, n_chunks):
    per_worker = n_chunks // SC_WORKERS
    assert per_worker * SC_WORKERS == n_chunks

    @functools.partial(pl.kernel, mesh=_sc_mesh(),
                       out_type=jax.ShapeDtypeStruct((TOP_K, n_chunks * SC_CHUNK, DW), jnp.int32),
                       scratch_types=_sc_scratch(), name="gather_rows")
    def run(y_hbm, d_hbm, o_hbm, idx_v, rows_v, sem):
        wid = lax.axis_index("s") * SC_CORES + lax.axis_index("c")
        for j in range(per_worker):
            c = wid * per_worker + j
            pltpu.sync_copy(d_hbm.at[first_chunk + c], idx_v)
            for k in range(TOP_K):
                pltpu.async_copy(y_hbm.at[idx_v.at[k]], rows_v, sem).wait()
                pltpu.sync_copy(rows_v, o_hbm.at[k, pl.ds(c * SC_CHUNK, SC_CHUNK)])

    return run(y, dest)


def _constants():
    lane64 = np.arange(4 * HD)
    seg64 = (lane64[:, None] // HD == lane64[None, :] // HD).astype(np.float32)
    lane96 = np.arange(4 * BP)
    real = lane96 % BP < B_QK
    seg96 = ((lane96[:, None] // BP == lane96[None, :] // BP) & real[:, None] & real[None, :]).astype(np.float32)
    place = np.zeros((LANE, 4 * BP), np.float32)
    for hh in range(4):
        place[np.arange(B_ROPE), hh * BP + B_NOPE + np.arange(B_ROPE)] = 1.0

    def angles(rot_dim):
        pos = np.arange(LAT_LEN)
        rows = (pos // GRID_W).astype(np.float32)
        cols = (pos % GRID_W).astype(np.float32)
        axis_dim = rot_dim // 2
        inv = np.power(np.float32(ROPE_THETA), -(np.arange(0, axis_dim, 2, dtype=np.float32) / np.float32(axis_dim)))
        ang = np.concatenate([rows[:, None] * inv, cols[:, None] * inv], axis=-1).astype(np.float32)
        return np.cos(ang), np.sin(ang)

    def head_tables(rot_dim):
        cos, sin = angles(rot_dim)
        q = rot_dim // 4
        cr, cc, sr, sc = cos[:, :q], cos[:, q:], sin[:, :q], sin[:, q:]
        return (np.concatenate([cr, cr, cc, cc], axis=-1), np.concatenate([-sr, sr, -sc, sc], axis=-1))

    c64, s64 = head_tables(HD)
    cos64 = np.tile(c64, (1, 4))
    sin64 = np.tile(s64, (1, 4))
    c32, s32 = head_tables(B_ROPE)
    ones = np.ones((LAT_LEN, B_NOPE), np.float32)
    zeros = np.zeros((LAT_LEN, B_NOPE), np.float32)
    padz = np.zeros((LAT_LEN, BP - B_QK), np.float32)
    cos96 = np.tile(np.concatenate([ones, c32, padz], axis=-1), (1, 4))
    sin96 = np.tile(np.concatenate([zeros, s32, padz], axis=-1), (1, 4))
    first64 = ((lane64 % 32) < 16).astype(np.float32)[None, :]
    first96 = (((lane96 % BP) % 16) < 8).astype(np.float32)[None, :]
    tri = (np.arange(TM)[:, None] < np.arange(TM)[None, :]).astype(np.float32)
    f32 = lambda a: jnp.asarray(a, F32)
    return dict(seg64=jnp.asarray(seg64, BF16), seg96=jnp.asarray(seg96, BF16), place=jnp.asarray(place, BF16),
                tri=jnp.asarray(tri, BF16),
                cos64=f32(cos64), sin64=f32(sin64), cos96=f32(cos96), sin96=f32(sin96),
                first64=f32(first64), first96=f32(first96))


def _pad_heads(w, per_head, width):
    lead = w.shape[:-1]
    w = w.reshape(lead + (4, per_head))
    return jnp.pad(w, ((0, 0),) * (len(lead) + 1) + ((0, width - per_head),)).reshape(lead + (4 * width,))


def _weights(norm1_g, norm2_g, w_in, a_q_g, a_k_g, b_cq_g, b_ckv_g, w_uq, w_ukv, b_q_g, b_k_g,
             c_q_g, c_k_g, w_out, router_w, router_b):
    o = np.cumsum((0, 256, 128, 128, 384, 256, 32, 256, 128, 128))
    seg = lambda k: w_in[:, :, o[k]:o[k + 1]]
    w_in_r = jnp.concatenate([seg(0), seg(1), seg(2), seg(3), seg(4), seg(6), seg(7), seg(8), seg(5),
                              jnp.zeros((DEPTH, D, LANE - B_ROPE), F32)], axis=-1).astype(BF16)
    ukv = w_ukv.reshape(DEPTH, B_KV_LORA, 4, B_NOPE + B_V)
    w_uk = _pad_heads(ukv[..., :B_NOPE].reshape(DEPTH, B_KV_LORA, 4 * B_NOPE), B_NOPE, BP)
    w_uv = ukv[..., B_NOPE:].reshape(DEPTH, B_KV_LORA, 4 * B_V)
    rw_hi = router_w.astype(BF16)
    rw_lo = (router_w - rw_hi.astype(F32)).astype(BF16)
    tile = lambda v, n: jnp.tile(v, (1, n))
    rows = dict(n1g=norm1_g, n2g=norm2_g, aqg=tile(a_q_g, 4), akg=tile(a_k_g, 2), cqg=tile(c_q_g, 4),
                ckg=tile(c_k_g, 2), bcqg=b_cq_g, bckvg=b_ckv_g,
                bqg=_pad_heads(tile(b_q_g, 4), B_QK, BP), bkg=_pad_heads(tile(b_k_g, 4), B_QK, BP))
    order = sorted(VEC_ROWS, key=lambda n: VEC_ROWS[n][0])
    assert all(rows[n].shape == (DEPTH, VEC_ROWS[n][1]) for n in order)
    vec = jnp.stack([jnp.pad(rows[n], ((0, 0), (0, D - rows[n].shape[1]))) for n in order], axis=1)
    return dict(
        vec=vec[:, :, None, :], w_in=w_in_r,
        w_uq=_pad_heads(w_uq, B_QK, BP).astype(BF16), w_uk=w_uk.astype(BF16), w_uv=w_uv.astype(BF16),
        w_out=w_out.astype(BF16),
        rwt=jnp.concatenate([jnp.swapaxes(rw_hi, 1, 2), jnp.swapaxes(rw_lo, 1, 2)], axis=1),
        rb=router_b[:, :, None])


def _moe(l, x_all, h2p, gate_slab, route, counts, mod, w_gu, b_gu, w_dn, b_dn):
    counts = counts[:, 0].astype(jnp.int32)
    padded = (counts + MOE_TM - 1) // MOE_TM * MOE_TM
    pend = jnp.cumsum(padded)
    pstart = pend - padded
    n_used = (pend[-1] // MOE_TM).astype(jnp.int32)
    tiles = jnp.minimum(jnp.arange(MOE_TILES, dtype=jnp.int32), n_used - 1)
    tile_expert = jnp.sum((pend[None, :] <= tiles[:, None] * MOE_TM).astype(jnp.int32), axis=1)
    tile_expert = jnp.minimum(tile_expert, N_EXPERTS - 1)
    group_end_tile = jnp.sum(jnp.where(tile_expert[:, None] == jnp.arange(N_EXPERTS)[None, :], pend[None, :], 0),
                             axis=1) // MOE_TM
    after = jnp.sum((pend[None, :] <= group_end_tile[:, None] * MOE_TM).astype(jnp.int32), axis=1)
    next_expert = jnp.where(group_end_tile < n_used, jnp.minimum(after, N_EXPERTS - 1), -1).astype(jnp.int32)
    e = route[TOP_K:2 * TOP_K].astype(jnp.int32)
    rank = route[2 * TOP_K:3 * TOP_K].astype(jnp.int32)
    start = jnp.sum(jnp.where(e[:, :, None] == jnp.arange(N_EXPERTS)[None, None, :], pstart[None, None, :], 0), axis=-1)
    dest = (start + rank).reshape(TOP_K, T_ALL // SC_CHUNK, SC_CHUNK).transpose(1, 0, 2)
    x_sorted = _dispatch_rows(h2p, dest)
    y = _experts(l, tile_expert, n_used.reshape(1), next_expert, x_sorted, w_gu, b_gu, w_dn, b_dn)
    ctx_chunks, lat_chunks = T_CTX // SC_CHUNK, T_LAT // SC_CHUNK
    y_ctx = _gather_rows(y, dest, 0, ctx_chunks)
    y_lat = _gather_rows(y, dest, ctx_chunks, lat_chunks)
    return (_combine(l, 0, N_CTX_TILES, x_all, y_ctx, gate_slab, mod),
            _combine(l, N_CTX_TILES, N_LAT_TILES, x_all, y_lat, gate_slab, mod))


def kernel(x_prompt, x_sample, cache_a_k, cache_a_v, cache_b_ckv, cache_b_krope, cache_c_k, cache_c_v, c, c_ctx,
           norm1_g, norm2_g, w_mod, b_mod, w_in, a_q_g, a_k_g, a_sink, b_cq_g, b_ckv_g, w_uq, w_ukv, b_q_g, b_k_g,
           c_q_g, c_k_g, w_out, router_w, router_b, w_gu, b_gu, w_dn, b_dn):
    consts = _constants()
    cond = jnp.concatenate([c_ctx[None, :], c, jnp.zeros((3, D), F32)], axis=0)
    mod = _modulation(cond, w_mod, b_mod).reshape(DEPTH, 8, 1, 6 * D)
    x_ctx, x_lat, x_lat_off = x_prompt.reshape(T_CTX, D), x_sample.reshape(T_LAT, D), 0

    lw = _weights(norm1_g, norm2_g, w_in, a_q_g, a_k_g, b_cq_g, b_ckv_g, w_uq, w_ukv, b_q_g, b_k_g,
                  c_q_g, c_k_g, w_out, router_w, router_b)
    ckb, cvb = _cache_kv(cache_b_ckv, cache_b_krope, lw, consts)
    merge_heads = lambda a: a.reshape(N_LAT_SEQ, DEPTH, PAST, 2 * HD).astype(BF16)
    cache = dict(ka=merge_heads(cache_a_k), va=merge_heads(cache_a_v), kb=ckb, vb=cvb,
                 kc=merge_heads(cache_c_k), vc=merge_heads(cache_c_v))

    states = ()
    names = ["qa", "ka", "va", "qb", "kb", "vb", "qc", "kc", "vc"]
    for l in range(DEPTH):
        outs = _projection(False, l, x_ctx, 0, mod, lw, consts, states)
        p_ctx = dict(zip(names, outs[:9]))
        states = tuple(outs[9:])
        p_lat = dict(zip(names, _projection(True, l, x_lat, x_lat_off, mod, lw, consts)))
        mix_ctx = _attention_ctx(l, a_sink, p_ctx)
        mix_lat = _attention_lat(l, a_sink, p_lat, cache)
        x_mid, h2p, gate_slab, route, counts = _post_attention(l, mix_ctx, mix_lat, x_ctx, x_lat, x_lat_off, mod, lw,
                                                               consts)
        x_ctx, x_lat = _moe(l, x_mid, h2p, gate_slab, route, counts, mod, w_gu, b_gu, w_dn, b_dn)
    y_ctx, y_lat = x_ctx, x_lat

    return (y_ctx.reshape(N_CTX_SEQ, CTX_LEN, D), y_lat.reshape(N_LAT_SEQ, LAT_LEN, D)) + states
```

```python
import functools

import jax
import jax.numpy as jnp
import numpy as np
from jax import lax
from jax.experimental import pallas as pl
from jax.experimental.pallas import tpu as pltpu
from jax.experimental.pallas import tpu_sc as plsc

F32 = jnp.float32
BF16 = jnp.bfloat16

D = 1024
DEPTH = 2
N_CTX_SEQ, CTX_LEN = 16, 256
N_LAT_SEQ, LAT_LEN = 4, 2048
PAST = 512
T_CTX = N_CTX_SEQ * CTX_LEN
T_LAT = N_LAT_SEQ * LAT_LEN
T_ALL = T_CTX + T_LAT
GRID_W = 64
HD = 64
WINDOW = 128
B_NOPE, B_ROPE, B_V = 64, 32, 128
B_QK = B_NOPE + B_ROPE
B_Q_LORA, B_KV_LORA = 384, 256
N_EXPERTS, TOP_K = 32, 4
D_FF = 1024
SWIGLU_LIMIT = 7.0
SWIGLU_ALPHA = 1.702
ROPE_THETA = 10000.0
EPS = 1e-6
NEG_INF = -1e30

TM = 256
LANE = 128
BP = 128
N_CTX_TILES = T_CTX // TM
N_LAT_TILES = T_LAT // TM
N_TILES = T_ALL // TM
LAT_TILES_PER_SEQ = LAT_LEN // TM
MOE_TM = 256
FF_CHUNK = 256
MOE_TILES = T_ALL * TOP_K // MOE_TM + N_EXPERTS
MOE_ROWS = MOE_TILES * MOE_TM
DW = D // 2
VMEM_LIMIT = 56 * 1024 * 1024

C_QA, C_KA, C_VA, C_CQ, C_CKV, C_QC, C_KC, C_VC, C_KR, C_END = (
    0, 256, 384, 512, 896, 1152, 1408, 1536, 1664, 1792)


def _dot(a, b):
    return jnp.dot(a, b, preferred_element_type=F32)


def _dot_t(a, b):
    return lax.dot_general(a, b, (((1,), (1,)), ((), ())), preferred_element_type=F32)


def _rms(x, g):
    return x * lax.rsqrt(jnp.mean(x * x, axis=-1, keepdims=True) + EPS) * g


def _seg_norm(x, seg, g, n):
    ss = _dot((x * x).astype(BF16), seg)
    return x * lax.rsqrt(ss * (1.0 / n) + EPS) * g


def _rope(x, cos, sin, first, sh):
    w = x.shape[1]
    fwd = pltpu.roll(x, w - sh, 1)
    bwd = pltpu.roll(x, sh, 1)
    return x * cos + jnp.where(first > 0.5, fwd, bwd) * sin


MOD_BN = 1536


def _mod_kernel(c_ref, w_ref, b_ref, o_ref):
    c = c_ref[...]
    s = (c * jax.nn.sigmoid(c)).astype(BF16)
    o_ref[...] = _dot(s, w_ref[...].astype(BF16)) + b_ref[...]


def _modulation(cond, w_mod, b_mod):
    return pl.pallas_call(
        _mod_kernel,
        grid=(DEPTH, 6 * D // MOD_BN),
        in_specs=[
            pl.BlockSpec((8, D), lambda l, j: (0, 0)),
            pl.BlockSpec((None, D, MOD_BN), lambda l, j: (l, 0, j)),
            pl.BlockSpec((None, 1, MOD_BN), lambda l, j: (l, 0, j)),
        ],
        out_specs=pl.BlockSpec((None, 8, MOD_BN), lambda l, j: (l, 0, j)),
        out_shape=jax.ShapeDtypeStruct((DEPTH, 8, 6 * D), F32),
        compiler_params=pltpu.CompilerParams(
            dimension_semantics=("arbitrary", "arbitrary"), vmem_limit_bytes=VMEM_LIMIT),
        name="modulation",
    )(cond, w_mod, b_mod.reshape(DEPTH, 1, 6 * D))


def _proj_kernel(is_lat, n_aliased, *refs):
    (x_ref, mod_ref, n1g_ref, win_ref, seg64_ref, seg96_ref, aqg_ref, akg_ref, cqg_ref, ckg_ref,
     bcqg_ref, bckvg_ref, bqg_ref, bkg_ref, wuq_ref, wuk_ref, wuv_ref, plc_ref) = refs[:18]
    refs = refs[18 + n_aliased:]
    if is_lat:
        (cos64_ref, sin64_ref, cos96_ref, sin96_ref, f64_ref, f96_ref) = refs[:6]
        refs = refs[6:]
    (qa_ref, ka_ref, va_ref, qb_ref, kb_ref, vb_ref, qc_ref, kc_ref, vc_ref) = refs[:9]
    refs = refs[9:]
    if not is_lat:
        if n_aliased == 0:
            for r in refs:
                r[1:] = jnp.zeros((DEPTH - 1,) + r.shape[1:], F32)
            refs = [r.at[0] for r in refs]
        (kas_ref, vas_ref, ckvs_ref, krs_ref, kcs_ref, vcs_ref) = refs

    x = x_ref[...]
    m = mod_ref[...]
    sh1, sc1 = m[:, 0:D], m[:, D:2 * D]
    h = (_rms(x, n1g_ref[...]) * (1.0 + sc1) + sh1).astype(BF16)

    def proj(a, b):
        return _dot(h, win_ref[:, a:b])

    def rope64(t):
        wd = t.shape[1]
        return _rope(t, cos64_ref[:, :wd], sin64_ref[:, :wd], f64_ref[:, :wd], 16) if is_lat else t

    def rope96(t):
        return _rope(t, cos96_ref[...], sin96_ref[...], f96_ref[...], 8) if is_lat else t

    seg64 = seg64_ref[...]
    seg64h = seg64_ref[0:2 * HD, 0:2 * HD]
    seg96 = seg96_ref[...]

    def seg_sum(t, seg):
        return _dot((t * t).astype(BF16), seg)

    def seg_finish(t, ss, g, n):
        return t * lax.rsqrt(ss * (1.0 / n) + EPS) * g

    p_cq, p_ckv, p_kr = proj(C_CQ, C_CKV), proj(C_CKV, C_QC), proj(C_KR, C_END)
    p_qa, p_ka, p_qc, p_kc = proj(C_QA, C_KA), proj(C_KA, C_VA), proj(C_QC, C_KC), proj(C_KC, C_VC)
    p_va, p_vc = proj(C_VA, C_CQ), proj(C_VC, C_KR)

    cq = _rms(p_cq, bcqg_ref[...]).astype(BF16)
    ckv = _rms(p_ckv, bckvg_ref[...])
    ckv16 = ckv.astype(BF16)
    u_q = _dot(cq, wuq_ref[...])
    u_k = _dot(ckv16, wuk_ref[...]) + _dot(p_kr.astype(BF16), plc_ref[...])
    u_v = _dot(ckv16, wuv_ref[...])
    ss_qa, ss_ka = seg_sum(p_qa, seg64), seg_sum(p_ka, seg64h)
    ss_qc, ss_kc = seg_sum(p_qc, seg64), seg_sum(p_kc, seg64h)
    ss_qb, ss_kb = seg_sum(u_q, seg96), seg_sum(u_k, seg96)

    def store_kv_state(ref, t):
        for kv in range(2):
            ref[:, kv, :] = t[:, kv * HD:(kv + 1) * HD]

    if not is_lat:
        store_kv_state(vas_ref, p_va)
        store_kv_state(vcs_ref, p_vc)
        ckvs_ref[...] = ckv
        krs_ref[...] = p_kr[:, 0:B_ROPE]
    va_ref[...] = p_va.astype(BF16)
    vc_ref[...] = p_vc.astype(BF16)
    vb_ref[...] = u_v.astype(BF16)

    t = seg_finish(p_qa, ss_qa, aqg_ref[...], HD)
    qa_ref[...] = (rope64(t) * (HD ** -0.5 * LOG2E)).astype(BF16)
    t = seg_finish(p_ka, ss_ka, akg_ref[...], HD)
    if not is_lat:
        store_kv_state(kas_ref, t)
    ka_ref[...] = rope64(t).astype(BF16)
    t = seg_finish(p_qc, ss_qc, cqg_ref[...], HD)
    qc_ref[...] = (rope64(t) * (HD ** -0.5 * LOG2E)).astype(BF16)
    t = seg_finish(p_kc, ss_kc, ckg_ref[...], HD)
    if not is_lat:
        store_kv_state(kcs_ref, t)
    kc_ref[...] = rope64(t).astype(BF16)
    t = seg_finish(u_q, ss_qb, bqg_ref[...], B_QK)
    qb_ref[...] = (rope96(t) * (B_QK ** -0.5 * LOG2E)).astype(BF16)
    t = seg_finish(u_k, ss_kb, bkg_ref[...], B_QK)
    kb_ref[...] = rope96(t).astype(BF16)


def _full(shape):
    nd = len(shape)
    return pl.BlockSpec(shape, lambda i: (0,) * nd)


STATE_TAILS = ((2, HD), (2, HD), (B_KV_LORA,), (B_ROPE,), (2, HD), (2, HD))


def _layer_spec(a, l):
    nd = a.ndim - 1
    return pl.BlockSpec((None,) + a.shape[1:], lambda *_: (l,) + (0,) * nd)


VEC_ROWS = dict(n1g=(0, D), n2g=(1, D), aqg=(2, 4 * HD), akg=(3, 2 * HD), cqg=(4, 4 * HD), ckg=(5, 2 * HD),
                bcqg=(6, B_Q_LORA), bckvg=(7, B_KV_LORA), bqg=(8, 4 * BP), bkg=(9, 4 * BP))


def _vec_spec(l, name):
    row, width = VEC_ROWS[name]
    return pl.BlockSpec((None, None, 1, width), lambda *_: (l, row, 0, 0))


def _projection(is_lat, l, x_src, x_off, mod, lw, consts, prev_states=()):
    assert is_lat or bool(prev_states) == (l > 0)
    n_tiles = N_LAT_TILES if is_lat else N_CTX_TILES
    t_rows = n_tiles * TM
    if is_lat:
        mod_map = lambda i: (l, 1 + i // LAT_TILES_PER_SEQ, 0, 0)
    else:
        mod_map = lambda i: (l, 0, 0, 0)
    vec = lw["vec"]
    gain = lambda name: (vec, _vec_spec(l, name))
    whole = lambda a: (a, _full(a.shape))
    layer = lambda a: (a, _layer_spec(a, l))
    pairs = [(x_src, pl.BlockSpec((TM, D), lambda i: (i + x_off, 0))),
             (mod, pl.BlockSpec((None, None, 1, 6 * D), mod_map)),
             gain("n1g"), layer(lw["w_in"]), whole(consts["seg64"]), whole(consts["seg96"]),
             gain("aqg"), gain("akg"), gain("cqg"), gain("ckg"), gain("bcqg"), gain("bckvg"), gain("bqg"), gain("bkg"),
             layer(lw["w_uq"]), layer(lw["w_uk"]), layer(lw["w_uv"]), whole(consts["place"])]
    ins = [a for a, _ in pairs]
    in_specs = [s for _, s in pairs]
    n_plain = len(ins)
    ins += list(prev_states)
    in_specs += [pl.BlockSpec(memory_space=pl.ANY) for _ in prev_states]
    if is_lat:
        tabs = [consts["cos64"], consts["sin64"], consts["cos96"], consts["sin96"]]
        ins += tabs + [consts["first64"], consts["first96"]]
        in_specs += [pl.BlockSpec((TM, a.shape[1]), lambda i: (i % LAT_TILES_PER_SEQ, 0)) for a in tabs]
        in_specs += [_full(consts["first64"].shape), _full(consts["first96"].shape)]
    widths = [4 * HD, 2 * HD, 2 * HD, 4 * BP, 4 * BP, 4 * B_V, 4 * HD, 2 * HD, 2 * HD]
    out_shape = [jax.ShapeDtypeStruct((t_rows, w), BF16) for w in widths]
    out_specs = [pl.BlockSpec((TM, w), lambda i: (i, 0)) for w in widths]
    if not is_lat:
        for tail in STATE_TAILS:
            zeros = (0,) * len(tail)
            out_shape.append(jax.ShapeDtypeStruct((N_CTX_SEQ, DEPTH, CTX_LEN) + tail, F32))
            if prev_states:
                out_specs.append(pl.BlockSpec((None, None, CTX_LEN) + tail, lambda i, z=zeros: (i, l, 0) + z))
            else:
                out_specs.append(pl.BlockSpec((None, DEPTH, CTX_LEN) + tail, lambda i, z=zeros: (i, 0, 0) + z))
    aliases = {n_plain + k: len(widths) + k for k in range(len(prev_states))}
    return pl.pallas_call(
        functools.partial(_proj_kernel, is_lat, len(prev_states)),
        grid=(n_tiles,),
        in_specs=in_specs,
        out_specs=out_specs,
        out_shape=out_shape,
        input_output_aliases=aliases,
        compiler_params=pltpu.CompilerParams(
            dimension_semantics=("arbitrary",), vmem_limit_bytes=VMEM_LIMIT),
        name="proj_lat" if is_lat else "proj_ctx",
    )(*ins)


def _cache_kv_kernel(ckv_ref, kr_ref, seg96_ref, bkg_ref, wuk_ref, wuv_ref, plc_ref, kb_ref, vb_ref):
    ckv16 = ckv_ref[...].astype(BF16)
    vb_ref[...] = _dot(ckv16, wuv_ref[...]).astype(BF16)
    kpre = _dot(ckv16, wuk_ref[...]) + _dot(kr_ref[...].astype(BF16), plc_ref[0:B_ROPE, :])
    kb_ref[...] = _seg_norm(kpre, seg96_ref[...], bkg_ref[...], B_QK).astype(BF16)


def _cache_kv(cache_b_ckv, cache_b_krope, lw, consts):
    def c_map(l, r):
        return (r // (PAST // TM), l, r % (PAST // TM), 0)

    def w_spec(a):
        return pl.BlockSpec((None,) + a.shape[1:], lambda l, r: (l,) + (0,) * (a.ndim - 1))

    def s_spec(a):
        return pl.BlockSpec(a.shape, lambda l, r: (0,) * a.ndim)

    rows = N_LAT_SEQ * PAST
    return pl.pallas_call(
        _cache_kv_kernel,
        grid=(DEPTH, rows // TM),
        in_specs=[pl.BlockSpec((None, None, TM, B_KV_LORA), c_map), pl.BlockSpec((None, None, TM, B_ROPE), c_map),
                  s_spec(consts["seg96"]),
                  pl.BlockSpec((None, None, 1, 4 * BP), lambda l, r: (l, VEC_ROWS["bkg"][0], 0, 0)),
                  w_spec(lw["w_uk"]), w_spec(lw["w_uv"]),
                  s_spec(consts["place"])],
        out_specs=[pl.BlockSpec((None, TM, 4 * BP), lambda l, r: (l, r, 0)),
                   pl.BlockSpec((None, TM, 4 * B_V), lambda l, r: (l, r, 0))],
        out_shape=[jax.ShapeDtypeStruct((DEPTH, rows, 4 * BP), BF16),
                   jax.ShapeDtypeStruct((DEPTH, rows, 4 * B_V), BF16)],
        compiler_params=pltpu.CompilerParams(
            dimension_semantics=("arbitrary", "arbitrary"), vmem_limit_bytes=VMEM_LIMIT),
        name="cache_kv",
    )(cache_b_ckv, cache_b_krope, consts["seg96"], lw["vec"], lw["w_uk"], lw["w_uv"], consts["place"])


def _scores(q, segs):
    scores = []
    for k, _, mask in segs:
        s = _dot_t(q, k)
        if mask is not None:
            s = jnp.where(mask, s, NEG_INF)
        scores.append(s)
    return scores


def _softmax_pv(scores, segs, sink, num_cols, den_col):
    m = None
    for s in scores:
        sm = jnp.max(s, axis=-1, keepdims=True)
        m = sm if m is None else jnp.maximum(m, sm)
    if sink is not None:
        m = jnp.maximum(m, sink)
    acc = None
    den = None
    for s, (_, v, _) in zip(scores, segs):
        e = jnp.exp2(s - m)
        if den_col is None:
            d = jnp.sum(e, axis=-1, keepdims=True)
            den = d if den is None else den + d
        o = _dot(e.astype(BF16), v)
        acc = o if acc is None else acc + o
    if den_col is not None:
        den = acc[:, den_col:den_col + 1]
        acc = acc[:, num_cols]
    if sink is not None:
        den = den + jnp.exp2(sink - m)
    return acc / den


def _run_heads(jobs, o_ref):
    nxt = _scores(jobs[0][0](), jobs[0][1]())
    for n, (_, load_segs, sink, col, num_cols, den_col) in enumerate(jobs):
        scores = nxt
        if n + 1 < len(jobs):
            nxt = _scores(jobs[n + 1][0](), jobs[n + 1][1]())
        o = _softmax_pv(scores, load_segs(values=True), sink, num_cols, den_col)
        o_ref[:, col:col + o.shape[1]] = o.astype(BF16)


OUT_B = 4 * HD
OUT_C = 4 * HD + 4 * B_V
LOG2E = 1.4426950408889634


def _head_job(q_ref, q_cols, seg_refs, kv_cols, v_cols, sink, col, rows=None, masks=None, mxu_sums=True):
    width = v_cols.stop - v_cols.start
    if not mxu_sums:
        load_cols, num_cols, den_col = v_cols, None, None
    elif width == HD:
        pair = v_cols.start // LANE * LANE
        load_cols = slice(pair, pair + LANE)
        lo = v_cols.start - pair
        num_cols, den_col = slice(lo, lo + HD), (lo + HD) % LANE
    else:
        load_cols, num_cols, den_col = v_cols, slice(0, width), width
    if sink is not None:
        sink = sink * LOG2E

    def with_ones(v):
        if not mxu_sums:
            return v
        if width != HD:
            return jnp.concatenate([v, jnp.ones_like(v)], axis=1)
        lane = lax.broadcasted_iota(jnp.int32, (1, LANE), 1)
        keep = jnp.where((lane >= num_cols.start) & (lane < num_cols.stop), 1.0, 0.0).astype(BF16)
        return v * keep + (1.0 - keep)

    def load_segs(values=False):
        segs = []
        for n, (k_ref, v_ref) in enumerate(seg_refs):
            r = rows if (rows is not None and n == 0) else slice(None)
            if values:
                segs.append((None, with_ones(v_ref[r, load_cols]), None))
            else:
                segs.append((k_ref[r, kv_cols], None, None if masks is None else masks[n]))
        return segs

    return (lambda: q_ref[:, q_cols]), load_segs, sink, col, num_cols, den_col


def _attn_ctx_kernel(l, sink_ref, qa_ref, ka_ref, va_ref, qb_ref, kb_ref, vb_ref, qc_ref, kc_ref, vc_ref, o_ref):
    jobs = []
    for h in range(4):
        cs = slice(h // 2 * HD, (h // 2 + 1) * HD)
        jobs.append(_head_job(qa_ref, slice(h * HD, (h + 1) * HD), [(ka_ref, va_ref)], cs, cs, sink_ref[l, h],
                              h * HD, mxu_sums=False))
    for h in range(4):
        ks, vs = slice(h * BP, (h + 1) * BP), slice(h * B_V, (h + 1) * B_V)
        jobs.append(_head_job(qb_ref, ks, [(kb_ref, vb_ref)], ks, vs, None, OUT_B + h * B_V, mxu_sums=False))
    for h in range(4):
        cs = slice(h // 2 * HD, (h // 2 + 1) * HD)
        jobs.append(_head_job(qc_ref, slice(h * HD, (h + 1) * HD), [(kc_ref, vc_ref)], cs, cs, None, OUT_C + h * HD,
                              mxu_sums=False))
    _run_heads(jobs, o_ref)


def _attention_ctx(l, sink, p):
    names = ["qa", "ka", "va", "qb", "kb", "vb", "qc", "kc", "vc"]
    ins = [p[n] for n in names]
    in_specs = [pl.BlockSpec(memory_space=pltpu.SMEM)]
    in_specs += [pl.BlockSpec((CTX_LEN, a.shape[1]), lambda i: (i, 0)) for a in ins]
    return pl.pallas_call(
        functools.partial(_attn_ctx_kernel, l),
        grid=(N_CTX_SEQ,),
        in_specs=in_specs,
        out_specs=pl.BlockSpec((CTX_LEN, D), lambda i: (i, 0)),
        out_shape=jax.ShapeDtypeStruct((T_CTX, D), BF16),
        compiler_params=pltpu.CompilerParams(
            dimension_semantics=("arbitrary",), vmem_limit_bytes=VMEM_LIMIT),
        name="attn_ctx",
    )(sink, *ins)


WIN_SPAN = TM + 2 * WINDOW


def _attn_lat_kernel(l, sink_ref, qa_ref, qb_ref, qc_ref, ka_ref, va_ref, kb_ref, vb_ref, kc_ref, vc_ref,
                     cka_ref, cva_ref, ckb_ref, cvb_ref, ckc_ref, cvc_ref, o_ref):
    qi = pl.program_id(1)
    ws = pl.multiple_of(jnp.clip(qi * TM - WINDOW, 0, LAT_LEN - WIN_SPAN), WINDOW)
    qpos = qi * TM + lax.broadcasted_iota(jnp.int32, (TM, WIN_SPAN), 0)
    kpos = ws + lax.broadcasted_iota(jnp.int32, (TM, WIN_SPAN), 1)
    band = jnp.abs(qpos - kpos) <= WINDOW
    jobs = []
    for h in range(4):
        cs = slice(h // 2 * HD, (h // 2 + 1) * HD)
        jobs.append(_head_job(qa_ref, slice(h * HD, (h + 1) * HD), [(ka_ref, va_ref), (cka_ref, cva_ref)], cs, cs,
                              sink_ref[l, h], h * HD, rows=pl.ds(ws, WIN_SPAN), masks=(band, None)))
    for h in range(4):
        ks, vs = slice(h * BP, (h + 1) * BP), slice(h * B_V, (h + 1) * B_V)
        jobs.append(_head_job(qb_ref, ks, [(kb_ref, vb_ref), (ckb_ref, cvb_ref)], ks, vs, None, OUT_B + h * B_V))
    for h in range(4):
        cs = slice(h // 2 * HD, (h // 2 + 1) * HD)
        jobs.append(_head_job(qc_ref, slice(h * HD, (h + 1) * HD), [(kc_ref, vc_ref), (ckc_ref, cvc_ref)], cs, cs,
                              None, OUT_C + h * HD))
    _run_heads(jobs, o_ref)


def _attention_lat(l, sink, p, cache):
    q_ins = [p["qa"], p["qb"], p["qc"]]
    kv_ins = [p[n] for n in ("ka", "va", "kb", "vb", "kc", "vc")]
    c_names = ("ka", "va", "kb", "vb", "kc", "vc")
    c_ins = [cache[n] for n in c_names]
    in_specs = [pl.BlockSpec(memory_space=pltpu.SMEM)]
    in_specs += [pl.BlockSpec((TM, a.shape[1]), lambda b, i: (b * LAT_TILES_PER_SEQ + i, 0)) for a in q_ins]
    in_specs += [pl.BlockSpec((LAT_LEN, a.shape[1]), lambda b, i: (b, 0)) for a in kv_ins]
    for n, a in zip(c_names, c_ins):
        if n in ("kb", "vb"):
            in_specs.append(pl.BlockSpec((None, PAST, a.shape[-1]), lambda b, i: (l, b, 0)))
        else:
            in_specs.append(pl.BlockSpec((None, None, PAST, a.shape[-1]), lambda b, i: (b, l, 0, 0)))
    return pl.pallas_call(
        functools.partial(_attn_lat_kernel, l),
        grid=(N_LAT_SEQ, LAT_TILES_PER_SEQ),
        in_specs=in_specs,
        out_specs=pl.BlockSpec((TM, D), lambda b, i: (b * LAT_TILES_PER_SEQ + i, 0)),
        out_shape=jax.ShapeDtypeStruct((T_LAT, D), BF16),
        compiler_params=pltpu.CompilerParams(
            dimension_semantics=("arbitrary", "arbitrary"), vmem_limit_bytes=VMEM_LIMIT),
        name="attn_lat",
    )(sink, *q_ins, *kv_ins, *c_ins)


def _pack_rows(x):
    half = x.shape[1] // 2
    r = x.astype(BF16).astype(F32)
    hi = lax.bitcast_convert_type(r[:, :half], jnp.int32)
    lo = lax.bitcast_convert_type(r[:, half:], jnp.int32)
    return jnp.bitwise_or(hi, lax.shift_right_logical(lo, 16))


def _unpack_rows(p):
    a = lax.bitcast_convert_type(jnp.bitwise_and(p, -65536), F32)
    b = lax.bitcast_convert_type(lax.shift_left(p, 16), F32)
    return jnp.concatenate([a, b], axis=1).astype(BF16)


ROUTE_ROWS = 16


def _post_kernel(mc_ref, ml_ref, xc_ref, xl_ref, mod_ref, n2g_ref, wout_ref, rwt_ref, rb_ref, tri_ref,
                 xo_ref, h2_ref, gate_ref, route_ref, cnt_ref, run_ref):
    i = pl.program_id(0)

    @pl.when(i == 0)
    def _():
        run_ref[...] = jnp.zeros_like(run_ref)
        cnt_ref[...] = jnp.zeros_like(cnt_ref)

    is_ctx = i < N_CTX_TILES
    mix = jnp.where(is_ctx, mc_ref[...], ml_ref[...])
    m = mod_ref[...]
    g1, sh2, sc2 = m[:, 2 * D:3 * D], m[:, 3 * D:4 * D], m[:, 4 * D:5 * D]
    x = jnp.where(is_ctx, xc_ref[...], xl_ref[...]) + g1 * _dot(mix, wout_ref[...])
    xo_ref[...] = x
    h2 = _rms(x, n2g_ref[...]) * (1.0 + sc2) + sh2
    hi = h2.astype(BF16)
    h2_ref[...] = _pack_rows(h2)
    lo = (h2 - hi.astype(F32)).astype(BF16)
    a = _dot_t(rwt_ref[...], hi)
    b = _dot_t(rwt_ref[0:N_EXPERTS, :], lo)
    logits = a[0:N_EXPERTS] + a[N_EXPERTS:] + b + rb_ref[...]
    eidx = lax.broadcasted_iota(jnp.int32, logits.shape, 0).astype(F32)
    work = logits
    vals, idxs, hots = [], [], []
    for _ in range(TOP_K):
        v = jnp.max(work, axis=0, keepdims=True)
        idx = jnp.min(jnp.where(work == v, eidx, float(N_EXPERTS)), axis=0, keepdims=True)
        hot = eidx == idx
        vals.append(v)
        idxs.append(idx)
        hots.append(hot)
        work = jnp.where(hot, -jnp.inf, work)
    es = [jnp.exp(v - vals[0]) for v in vals]
    den = es[0] + es[1] + es[2] + es[3]
    gates = [e / den for e in es]
    sel = jnp.where(hots[0] | hots[1] | hots[2] | hots[3], 1.0, 0.0)
    run = run_ref[:, 0:1]
    before = _dot(sel.astype(BF16), tri_ref[...]) + run
    run_new = jnp.broadcast_to(run + jnp.sum(sel, axis=1, keepdims=True), run_ref.shape)
    run_ref[...] = run_new
    cnt_ref[...] = run_new
    ranks = [jnp.sum(jnp.where(hots[k], before, 0.0), axis=0, keepdims=True) for k in range(TOP_K)]
    pad = [jnp.zeros((ROUTE_ROWS - 3 * TOP_K, TM), F32)]
    route_ref[...] = jnp.concatenate(gates + idxs + ranks + pad, axis=0)
    gate_ref[...] = jnp.concatenate(gates + [jnp.zeros((LANE - TOP_K, TM), F32)], axis=0).T


def _mod_row(i):
    return jnp.where(i < N_CTX_TILES, 0, 1 + (i - N_CTX_TILES) // LAT_TILES_PER_SEQ)


def _ctx_tile(i):
    return jnp.minimum(i, N_CTX_TILES - 1)


def _lat_tile(i):
    return jnp.maximum(i - N_CTX_TILES, 0)


def _post_attention(l, mix_ctx, mix_lat, x_ctx, x_lat, x_lat_off, mod, lw, consts):
    ins = [mix_ctx, mix_lat, x_ctx, x_lat, mod, lw["vec"], lw["w_out"], lw["rwt"], lw["rb"], consts["tri"]]
    in_specs = [
        pl.BlockSpec((TM, D), lambda i: (_ctx_tile(i), 0)),
        pl.BlockSpec((TM, D), lambda i: (_lat_tile(i), 0)),
        pl.BlockSpec((TM, D), lambda i: (_ctx_tile(i), 0)),
        pl.BlockSpec((TM, D), lambda i: (_lat_tile(i) + x_lat_off, 0)),
        pl.BlockSpec((None, None, 1, 6 * D), lambda i: (l, _mod_row(i), 0, 0)),
        _vec_spec(l, "n2g"),
    ] + [_layer_spec(a, l) for a in ins[6:9]] + [_full(consts["tri"].shape)]
    return pl.pallas_call(
        _post_kernel,
        grid=(N_TILES,),
        in_specs=in_specs,
        out_specs=[pl.BlockSpec((TM, D), lambda i: (i, 0)), pl.BlockSpec((TM, DW), lambda i: (i, 0)),
                   pl.BlockSpec((TM, LANE), lambda i: (i, 0)), pl.BlockSpec((ROUTE_ROWS, TM), lambda i: (0, i)),
                   pl.BlockSpec((N_EXPERTS, LANE), lambda i: (0, 0))],
        out_shape=[jax.ShapeDtypeStruct((T_ALL, D), F32), jax.ShapeDtypeStruct((T_ALL, DW), jnp.int32),
                   jax.ShapeDtypeStruct((T_ALL, LANE), F32), jax.ShapeDtypeStruct((ROUTE_ROWS, T_ALL), F32),
                   jax.ShapeDtypeStruct((N_EXPERTS, LANE), F32)],
        scratch_shapes=[pltpu.VMEM((N_EXPERTS, LANE), F32)],
        compiler_params=pltpu.CompilerParams(
            dimension_semantics=("arbitrary",), vmem_limit_bytes=VMEM_LIMIT),
        name="post_attn",
    )(*ins)


def _expert_kernel(l, te_ref, nu_ref, nxt_ref, x_ref, wgu_hbm, bgu_ref, wdn_hbm, bdn_ref, y_ref,
                   wgu32, wdn32, wgu16, wdn16, sem):
    i = pl.program_id(0)
    prev = te_ref[jnp.maximum(i - 1, 0)]
    new_expert = jnp.logical_or(i == 0, te_ref[i] != prev)

    def weight_copies(e):
        return (pltpu.make_async_copy(wgu_hbm.at[l, e], wgu32, sem.at[0]),
                pltpu.make_async_copy(wdn_hbm.at[l, e], wdn32, sem.at[1]))

    @pl.when(i == 0)
    def _():
        for cp in weight_copies(te_ref[0]):
            cp.start()

    @pl.when(new_expert)
    def _():
        cps = weight_copies(te_ref[i])
        cps[0].wait()
        wgu16[...] = wgu32[...].astype(BF16)
        cps[1].wait()
        wdn16[...] = wdn32[...].astype(BF16)

        @pl.when(nxt_ref[i] >= 0)
        def _():
            for cp in weight_copies(nxt_ref[i]):
                cp.start()

    @pl.when(i < nu_ref[0])
    def _():
        x = _unpack_rows(x_ref[...])

        def gate_up(a):
            b = a + FF_CHUNK
            g = _dot(x, wgu16[:, a:b]) + bgu_ref[:, a:b]
            up = _dot(x, wgu16[:, D_FF + a:D_FF + b]) + bgu_ref[:, D_FF + a:D_FF + b]
            return g, up

        def activate(g, up):
            g = jnp.minimum(g, SWIGLU_LIMIT)
            up = jnp.clip(up, -SWIGLU_LIMIT, SWIGLU_LIMIT)
            return ((up + 1.0) * (g * jax.nn.sigmoid(SWIGLU_ALPHA * g))).astype(BF16)

        starts = list(range(0, D_FF, FF_CHUNK))
        acc = None
        pre = gate_up(starts[0])
        for n, a in enumerate(starts):
            hid = activate(*pre)
            if n + 1 < len(starts):
                pre = gate_up(starts[n + 1])
            o = _dot(hid, wdn16[a:a + FF_CHUNK, :])
            acc = o if acc is None else acc + o
        y_ref[...] = _pack_rows(acc + bdn_ref[...])


def _experts(l, tile_expert, n_used, next_expert, x_sorted, w_gu, b_gu, w_dn, b_dn):
    def row_map(i, te, nu, nxt):
        return (jnp.minimum(i, nu[0] - 1), 0)

    def b_map(i, te, nu, nxt):
        return (l, te[i], 0, 0)

    grid_spec = pltpu.PrefetchScalarGridSpec(
        num_scalar_prefetch=3,
        grid=(MOE_TILES,),
        in_specs=[
            pl.BlockSpec((MOE_TM, DW), row_map),
            pl.BlockSpec(memory_space=pl.ANY),
            pl.BlockSpec((None, None, 1, 2 * D_FF), b_map),
            pl.BlockSpec(memory_space=pl.ANY),
            pl.BlockSpec((None, None, 1, D), b_map),
        ],
        out_specs=pl.BlockSpec((MOE_TM, DW), row_map),
        scratch_shapes=[pltpu.VMEM((D, 2 * D_FF), F32), pltpu.VMEM((D_FF, D), F32),
                        pltpu.VMEM((D, 2 * D_FF), BF16), pltpu.VMEM((D_FF, D), BF16),
                        pltpu.SemaphoreType.DMA((2,))],
    )
    return pl.pallas_call(
        functools.partial(_expert_kernel, l),
        grid_spec=grid_spec,
        out_shape=jax.ShapeDtypeStruct((MOE_ROWS, DW), jnp.int32),
        compiler_params=pltpu.CompilerParams(
            dimension_semantics=("arbitrary",), vmem_limit_bytes=VMEM_LIMIT),
        name="experts",
    )(tile_expert, n_used, next_expert, x_sorted, w_gu, b_gu.reshape(DEPTH, N_EXPERTS, 1, 2 * D_FF), w_dn,
      b_dn.reshape(DEPTH, N_EXPERTS, 1, D))


def _combine_kernel(x_ref, y_ref, route_ref, mod_ref, o_ref):
    g2 = mod_ref[...][:, 5 * D:6 * D]
    r = route_ref[...]
    acc = r[:, 0:1] * _unpack_rows(y_ref[0]).astype(F32)
    for k in range(1, TOP_K):
        acc = acc + r[:, k:k + 1] * _unpack_rows(y_ref[k]).astype(F32)
    o_ref[...] = x_ref[...] + g2 * acc


def _combine(l, first_tile, n_tiles, x_all, y_tok, gates, mod):
    return pl.pallas_call(
        _combine_kernel,
        grid=(n_tiles,),
        in_specs=[
            pl.BlockSpec((TM, D), lambda i: (i + first_tile, 0)),
            pl.BlockSpec((TOP_K, TM, DW), lambda i: (0, i, 0)),
            pl.BlockSpec((TM, LANE), lambda i: (i + first_tile, 0)),
            pl.BlockSpec((None, None, 1, 6 * D), lambda i: (l, _mod_row(i + first_tile), 0, 0)),
        ],
        out_specs=pl.BlockSpec((TM, D), lambda i: (i, 0)),
        out_shape=jax.ShapeDtypeStruct((n_tiles * TM, D), F32),
        compiler_params=pltpu.CompilerParams(
            dimension_semantics=("arbitrary",), vmem_limit_bytes=VMEM_LIMIT),
        name="combine",
    )(x_all, y_tok, gates, mod)


SC_CORES, SC_SUBCORES = 2, 16
SC_WORKERS = SC_CORES * SC_SUBCORES
SC_CHUNK = 128
SC_CHUNKS_PER_WORKER = T_ALL // SC_CHUNK // SC_WORKERS


def _sc_mesh():
    return plsc.VectorSubcoreMesh(core_axis_name="c", subcore_axis_name="s")


def _sc_scratch():
    return [pltpu.VMEM((TOP_K, SC_CHUNK), jnp.int32), pltpu.VMEM((SC_CHUNK, DW), jnp.int32),
            pltpu.SemaphoreType.DMA]


def _dispatch_rows(h2p, dest):
    @functools.partial(pl.kernel, mesh=_sc_mesh(), out_type=jax.ShapeDtypeStruct((MOE_ROWS, DW), jnp.int32),
                       scratch_types=_sc_scratch(), name="dispatch_rows")
    def run(h_hbm, d_hbm, o_hbm, idx_v, rows_v, sem):
        wid = lax.axis_index("s") * SC_CORES + lax.axis_index("c")
        for j in range(SC_CHUNKS_PER_WORKER):
            c = wid * SC_CHUNKS_PER_WORKER + j
            pltpu.sync_copy(d_hbm.at[c], idx_v)
            pltpu.sync_copy(h_hbm.at[pl.ds(c * SC_CHUNK, SC_CHUNK)], rows_v)
            copies = [pltpu.async_copy(rows_v, o_hbm.at[idx_v.at[k]], sem) for k in range(TOP_K)]
            for cp in copies:
                cp.wait()

    return run(h2p, dest)


def _gather_rows(y, dest, first_chunk, n_chunks):
    per_worker = n_chunks // SC_WORKERS
    assert per_worker * SC_WORKERS == n_chunks

    @functools.partial(pl.kernel, mesh=_sc_mesh(),
                       out_type=jax.ShapeDtypeStruct((TOP_K, n_chunks * SC_CHUNK, DW), jnp.int32),
                       scratch_types=_sc_scratch(), name="gather_rows")
    def run(y_hbm, d_hbm, o_hbm, idx_v, rows_v, sem):
        wid = lax.axis_index("s") * SC_CORES + lax.axis_index("c")
        for j in range(per_worker):
            c = wid * per_worker + j
            pltpu.sync_copy(d_hbm.at[first_chunk + c], idx_v)
            for k in range(TOP_K):
                pltpu.async_copy(y_hbm.at[idx_v.at[k]], rows_v, sem).wait()
                pltpu.sync_copy(rows_v, o_hbm.at[k, pl.ds(c * SC_CHUNK, SC_CHUNK)])

    return run(y, dest)


def _constants():
    lane64 = np.arange(4 * HD)
    seg64 = (lane64[:, None] // HD == lane64[None, :] // HD).astype(np.float32)
    lane96 = np.arange(4 * BP)
    real = lane96 % BP < B_QK
    seg96 = ((lane96[:, None] // BP == lane96[None, :] // BP) & real[:, None] & real[None, :]).astype(np.float32)
    place = np.zeros((LANE, 4 * BP), np.float32)
    for hh in range(4):
        place[np.arange(B_ROPE), hh * BP + B_NOPE + np.arange(B_ROPE)] = 1.0

    def angles(rot_dim):
        pos = np.arange(LAT_LEN)
        rows = (pos // GRID_W).astype(np.float32)
        cols = (pos % GRID_W).astype(np.float32)
        axis_dim = rot_dim // 2
        inv = np.power(np.float32(ROPE_THETA), -(np.arange(0, axis_dim, 2, dtype=np.float32) / np.float32(axis_dim)))
        ang = np.concatenate([rows[:, None] * inv, cols[:, None] * inv], axis=-1).astype(np.float32)
        return np.cos(ang), np.sin(ang)

    def head_tables(rot_dim):
        cos, sin = angles(rot_dim)
        q = rot_dim // 4
        cr, cc, sr, sc = cos[:, :q], cos[:, q:], sin[:, :q], sin[:, q:]
        return (np.concatenate([cr, cr, cc, cc], axis=-1), np.concatenate([-sr, sr, -sc, sc], axis=-1))

    c64, s64 = head_tables(HD)
    cos64 = np.tile(c64, (1, 4))
    sin64 = np.tile(s64, (1, 4))
    c32, s32 = head_tables(B_ROPE)
    ones = np.ones((LAT_LEN, B_NOPE), np.float32)
    zeros = np.zeros((LAT_LEN, B_NOPE), np.float32)
    padz = np.zeros((LAT_LEN, BP - B_QK), np.float32)
    cos96 = np.tile(np.concatenate([ones, c32, padz], axis=-1), (1, 4))
    sin96 = np.tile(np.concatenate([zeros, s32, padz], axis=-1), (1, 4))
    first64 = ((lane64 % 32) < 16).astype(np.float32)[None, :]
    first96 = (((lane96 % BP) % 16) < 8).astype(np.float32)[None, :]
    tri = (np.arange(TM)[:, None] < np.arange(TM)[None, :]).astype(np.float32)
    f32 = lambda a: jnp.asarray(a, F32)
    return dict(seg64=jnp.asarray(seg64, BF16), seg96=jnp.asarray(seg96, BF16), place=jnp.asarray(place, BF16),
                tri=jnp.asarray(tri, BF16),
                cos64=f32(cos64), sin64=f32(sin64), cos96=f32(cos96), sin96=f32(sin96),
                first64=f32(first64), first96=f32(first96))


def _pad_heads(w, per_head, width):
    lead = w.shape[:-1]
    w = w.reshape(lead + (4, per_head))
    return jnp.pad(w, ((0, 0),) * (len(lead) + 1) + ((0, width - per_head),)).reshape(lead + (4 * width,))


def _weights(norm1_g, norm2_g, w_in, a_q_g, a_k_g, b_cq_g, b_ckv_g, w_uq, w_ukv, b_q_g, b_k_g,
             c_q_g, c_k_g, w_out, router_w, router_b):
    o = np.cumsum((0, 256, 128, 128, 384, 256, 32, 256, 128, 128))
    seg = lambda k: w_in[:, :, o[k]:o[k + 1]]
    w_in_r = jnp.concatenate([seg(0), seg(1), seg(2), seg(3), seg(4), seg(6), seg(7), seg(8), seg(5),
                              jnp.zeros((DEPTH, D, LANE - B_ROPE), F32)], axis=-1).astype(BF16)
    ukv = w_ukv.reshape(DEPTH, B_KV_LORA, 4, B_NOPE + B_V)
    w_uk = _pad_heads(ukv[..., :B_NOPE].reshape(DEPTH, B_KV_LORA, 4 * B_NOPE), B_NOPE, BP)
    w_uv = ukv[..., B_NOPE:].reshape(DEPTH, B_KV_LORA, 4 * B_V)
    rw_hi = router_w.astype(BF16)
    rw_lo = (router_w - rw_hi.astype(F32)).astype(BF16)
    tile = lambda v, n: jnp.tile(v, (1, n))
    rows = dict(n1g=norm1_g, n2g=norm2_g, aqg=tile(a_q_g, 4), akg=tile(a_k_g, 2), cqg=tile(c_q_g, 4),
                ckg=tile(c_k_g, 2), bcqg=b_cq_g, bckvg=b_ckv_g,
                bqg=_pad_heads(tile(b_q_g, 4), B_QK, BP), bkg=_pad_heads(tile(b_k_g, 4), B_QK, BP))
    order = sorted(VEC_ROWS, key=lambda n: VEC_ROWS[n][0])
    assert all(rows[n].shape == (DEPTH, VEC_ROWS[n][1]) for n in order)
    vec = jnp.stack([jnp.pad(rows[n], ((0, 0), (0, D - rows[n].shape[1]))) for n in order], axis=1)
    return dict(
        vec=vec[:, :, None, :], w_in=w_in_r,
        w_uq=_pad_heads(w_uq, B_QK, BP).astype(BF16), w_uk=w_uk.astype(BF16), w_uv=w_uv.astype(BF16),
        w_out=w_out.astype(BF16),
        rwt=jnp.concatenate([jnp.swapaxes(rw_hi, 1, 2), jnp.swapaxes(rw_lo, 1, 2)], axis=1),
        rb=router_b[:, :, None])


def _moe(l, x_all, h2p, gate_slab, route, counts, mod, w_gu, b_gu, w_dn, b_dn):
    counts = counts[:, 0].astype(jnp.int32)
    padded = (counts + MOE_TM - 1) // MOE_TM * MOE_TM
    pend = jnp.cumsum(padded)
    pstart = pend - padded
    n_used = (pend[-1] // MOE_TM).astype(jnp.int32)
    tiles = jnp.minimum(jnp.arange(MOE_TILES, dtype=jnp.int32), n_used - 1)
    tile_expert = jnp.sum((pend[None, :] <= tiles[:, None] * MOE_TM).astype(jnp.int32), axis=1)
    tile_expert = jnp.minimum(tile_expert, N_EXPERTS - 1)
    group_end_tile = jnp.sum(jnp.where(tile_expert[:, None] == jnp.arange(N_EXPERTS)[None, :], pend[None, :], 0),
                             axis=1) // MOE_TM
    after = jnp.sum((pend[None, :] <= group_end_tile[:, None] * MOE_TM).astype(jnp.int32), axis=1)
    next_expert = jnp.where(group_end_tile < n_used, jnp.minimum(after, N_EXPERTS - 1), -1).astype(jnp.int32)
    e = route[TOP_K:2 * TOP_K].astype(jnp.int32)
    rank = route[2 * TOP_K:3 * TOP_K].astype(jnp.int32)
    start = jnp.sum(jnp.where(e[:, :, None] == jnp.arange(N_EXPERTS)[None, None, :], pstart[None, None, :], 0), axis=-1)
    dest = (start + rank).reshape(TOP_K, T_ALL // SC_CHUNK, SC_CHUNK).transpose(1, 0, 2)
    x_sorted = _dispatch_rows(h2p, dest)
    y = _experts(l, tile_expert, n_used.reshape(1), next_expert, x_sorted, w_gu, b_gu, w_dn, b_dn)
    ctx_chunks, lat_chunks = T_CTX // SC_CHUNK, T_LAT // SC_CHUNK
    y_ctx = _gather_rows(y, dest, 0, ctx_chunks)
    y_lat = _gather_rows(y, dest, ctx_chunks, lat_chunks)
    return (_combine(l, 0, N_CTX_TILES, x_all, y_ctx, gate_slab, mod),
            _combine(l, N_CTX_TILES, N_LAT_TILES, x_all, y_lat, gate_slab, mod))


def kernel(x_prompt, x_sample, cache_a_k, cache_a_v, cache_b_ckv, cache_b_krope, cache_c_k, cache_c_v, c, c_ctx,
           norm1_g, norm2_g, w_mod, b_mod, w_in, a_q_g, a_k_g, a_sink, b_cq_g, b_ckv_g, w_uq, w_ukv, b_q_g, b_k_g,
           c_q_g, c_k_g, w_out, router_w, router_b, w_gu, b_gu, w_dn, b_dn):
    consts = _constants()
    cond = jnp.concatenate([c_ctx[None, :], c, jnp.zeros((3, D), F32)], axis=0)
    mod = _modulation(cond, w_mod, b_mod).reshape(DEPTH, 8, 1, 6 * D)
    x_ctx, x_lat, x_lat_off = x_prompt.reshape(T_CTX, D), x_sample.reshape(T_LAT, D), 0

    lw = _weights(norm1_g, norm2_g, w_in, a_q_g, a_k_g, b_cq_g, b_ckv_g, w_uq, w_ukv, b_q_g, b_k_g,
                  c_q_g, c_k_g, w_out, router_w, router_b)
    ckb, cvb = _cache_kv(cache_b_ckv, cache_b_krope, lw, consts)
    merge_heads = lambda a: a.reshape(N_LAT_SEQ, DEPTH, PAST, 2 * HD).astype(BF16)
    cache = dict(ka=merge_heads(cache_a_k), va=merge_heads(cache_a_v), kb=ckb, vb=cvb,
                 kc=merge_heads(cache_c_k), vc=merge_heads(cache_c_v))

    states = ()
    names = ["qa", "ka", "va", "qb", "kb", "vb", "qc", "kc", "vc"]
    for l in range(DEPTH):
        outs = _projection(False, l, x_ctx, 0, mod, lw, consts, states)
        p_ctx = dict(zip(names, outs[:9]))
        states = tuple(outs[9:])
        p_lat = dict(zip(names, _projection(True, l, x_lat, x_lat_off, mod, lw, consts)))
        mix_ctx = _attention_ctx(l, a_sink, p_ctx)
        mix_lat = _attention_lat(l, a_sink, p_lat, cache)
        x_mid, h2p, gate_slab, route, counts = _post_attention(l, mix_ctx, mix_lat, x_ctx, x_lat, x_lat_off, mod, lw,
                                                               consts)
        x_ctx, x_lat = _moe(l, x_mid, h2p, gate_slab, route, counts, mod, w_gu, b_gu, w_dn, b_dn)
    y_ctx, y_lat = x_ctx, x_lat

    return (y_ctx.reshape(N_CTX_SEQ, CTX_LEN, D), y_lat.reshape(N_LAT_SEQ, LAT_LEN, D)) + states
```

```python
import functools

import jax
import jax.numpy as jnp
import numpy as np
from jax import lax
from jax.experimental import pallas as pl
from jax.experimental.pallas import tpu as pltpu
from jax.experimental.pallas import tpu_sc as plsc

F32 = jnp.float32
BF16 = jnp.bfloat16

D = 1024
DEPTH = 2
N_CTX_SEQ, CTX_LEN = 16, 256
N_LAT_SEQ, LAT_LEN = 4, 2048
PAST = 512
T_CTX = N_CTX_SEQ * CTX_LEN
T_LAT = N_LAT_SEQ * LAT_LEN
T_ALL = T_CTX + T_LAT
GRID_W = 64
HD = 64
WINDOW = 128
B_NOPE, B_ROPE, B_V = 64, 32, 128
B_QK = B_NOPE + B_ROPE
B_Q_LORA, B_KV_LORA = 384, 256
N_EXPERTS, TOP_K = 32, 4
D_FF = 1024
SWIGLU_LIMIT = 7.0
SWIGLU_ALPHA = 1.702
ROPE_THETA = 10000.0
EPS = 1e-6
NEG_INF = -1e30

TM = 256
TP = 512
TQ_LAT = 512
TQ_CTX = 256
SEQ_PER_STEP = TQ_CTX // CTX_LEN
LANE = 128
BP = 128
N_CTX_TILES = T_CTX // TM
N_LAT_TILES = T_LAT // TM
N_TILES = T_ALL // TM
LAT_TILES_PER_SEQ = LAT_LEN // TM
MOE_TM = 256
FF_CHUNK = 256
MOE_TILES = T_ALL * TOP_K // MOE_TM + N_EXPERTS
MOE_ROWS = MOE_TILES * MOE_TM
DW = D // 2
VMEM_LIMIT = 56 * 1024 * 1024

C_QA, C_KA, C_VA, C_CQ, C_CKV, C_QC, C_KC, C_VC, C_KR, C_END = (
    0, 256, 384, 512, 896, 1152, 1408, 1536, 1664, 1792)


def _dot(a, b):
    return jnp.dot(a, b, preferred_element_type=F32)


def _dot_t(a, b):
    return lax.dot_general(a, b, (((1,), (1,)), ((), ())), preferred_element_type=F32)


def _rms(x, g):
    return x * lax.rsqrt(jnp.mean(x * x, axis=-1, keepdims=True) + EPS) * g


def _seg_norm(x, seg, g, n):
    ss = _dot((x * x).astype(BF16), seg)
    return x * lax.rsqrt(ss * (1.0 / n) + EPS) * g


def _rope(x, cos, sin, first, sh):
    w = x.shape[1]
    fwd = pltpu.roll(x, w - sh, 1)
    bwd = pltpu.roll(x, sh, 1)
    return x * cos + jnp.where(first > 0.5, fwd, bwd) * sin


MOD_BN = 1536


def _mod_kernel(c_ref, w_ref, b_ref, o_ref):
    c = c_ref[...]
    s = (c * jax.nn.sigmoid(c)).astype(BF16)
    o_ref[...] = _dot(s, w_ref[...].astype(BF16)) + b_ref[...]


def _modulation(cond, w_mod, b_mod):
    return pl.pallas_call(
        _mod_kernel,
        grid=(DEPTH, 6 * D // MOD_BN),
        in_specs=[
            pl.BlockSpec((8, D), lambda l, j: (0, 0)),
            pl.BlockSpec((None, D, MOD_BN), lambda l, j: (l, 0, j)),
            pl.BlockSpec((None, 1, MOD_BN), lambda l, j: (l, 0, j)),
        ],
        out_specs=pl.BlockSpec((None, 8, MOD_BN), lambda l, j: (l, 0, j)),
        out_shape=jax.ShapeDtypeStruct((DEPTH, 8, 6 * D), F32),
        compiler_params=pltpu.CompilerParams(
            dimension_semantics=("arbitrary", "arbitrary"), vmem_limit_bytes=VMEM_LIMIT),
        name="modulation",
    )(cond, w_mod, b_mod.reshape(DEPTH, 1, 6 * D))


def _proj_kernel(is_lat, n_aliased, *refs):
    (x_ref, mod_ref, n1g_ref, win_ref, seg64_ref, seg96_ref, aqg_ref, akg_ref, cqg_ref, ckg_ref,
     bcqg_ref, bckvg_ref, bqg_ref, bkg_ref, wuq_ref, wuk_ref, wuv_ref, plc_ref) = refs[:18]
    refs = refs[18 + n_aliased:]
    if is_lat:
        (cos64_ref, sin64_ref, cos96_ref, sin96_ref, f64_ref, f96_ref) = refs[:6]
        refs = refs[6:]
    (qa_ref, ka_ref, va_ref, qb_ref, kb_ref, vb_ref, qc_ref, kc_ref, vc_ref) = refs[:9]
    refs = refs[9:]
    if not is_lat:
        if n_aliased == 0:
            for r in refs:
                r[:, 1:] = jnp.zeros((SEQ_PER_STEP, DEPTH - 1) + r.shape[2:], F32)
            refs = [r.at[:, 0] for r in refs]
        (kas_ref, vas_ref, ckvs_ref, krs_ref, kcs_ref, vcs_ref) = refs

    x = x_ref[...]
    m = mod_ref[...]
    sh1, sc1 = m[:, 0:D], m[:, D:2 * D]
    h = (_rms(x, n1g_ref[...]) * (1.0 + sc1) + sh1).astype(BF16)

    def proj(a, b):
        return _dot(h, win_ref[:, a:b])

    def rope64(t):
        wd = t.shape[1]
        return _rope(t, cos64_ref[:, :wd], sin64_ref[:, :wd], f64_ref[:, :wd], 16) if is_lat else t

    def rope96(t):
        return _rope(t, cos96_ref[...], sin96_ref[...], f96_ref[...], 8) if is_lat else t

    seg64 = seg64_ref[...]
    seg64h = seg64_ref[0:2 * HD, 0:2 * HD]
    seg96 = seg96_ref[...]

    def seg_sum(t, seg):
        return _dot((t * t).astype(BF16), seg)

    def seg_finish(t, ss, g, n):
        return t * lax.rsqrt(ss * (1.0 / n) + EPS) * g

    p_cq, p_ckv, p_kr = proj(C_CQ, C_CKV), proj(C_CKV, C_QC), proj(C_KR, C_END)
    p_qa, p_ka, p_qc, p_kc = proj(C_QA, C_KA), proj(C_KA, C_VA), proj(C_QC, C_KC), proj(C_KC, C_VC)
    p_va, p_vc = proj(C_VA, C_CQ), proj(C_VC, C_KR)

    cq = _rms(p_cq, bcqg_ref[...]).astype(BF16)
    ckv = _rms(p_ckv, bckvg_ref[...])
    ckv16 = ckv.astype(BF16)
    u_q = _dot(cq, wuq_ref[...])
    u_k = _dot(ckv16, wuk_ref[...]) + _dot(p_kr.astype(BF16), plc_ref[...])
    u_v = _dot(ckv16, wuv_ref[...])
    ss_qa, ss_ka = seg_sum(p_qa, seg64), seg_sum(p_ka, seg64h)
    ss_qc, ss_kc = seg_sum(p_qc, seg64), seg_sum(p_kc, seg64h)
    ss_qb, ss_kb = seg_sum(u_q, seg96), seg_sum(u_k, seg96)

    def store_state(ref, t):
        for s in range(SEQ_PER_STEP):
            ref[s] = t[s * CTX_LEN:(s + 1) * CTX_LEN]

    def store_kv_state(ref, t):
        for s in range(SEQ_PER_STEP):
            for kv in range(2):
                ref[s, :, kv, :] = t[s * CTX_LEN:(s + 1) * CTX_LEN, kv * HD:(kv + 1) * HD]

    if not is_lat:
        store_kv_state(vas_ref, p_va)
        store_kv_state(vcs_ref, p_vc)
        store_state(ckvs_ref, ckv)
        store_state(krs_ref, p_kr[:, 0:B_ROPE])
    va_ref[...] = p_va.astype(BF16)
    vc_ref[...] = p_vc.astype(BF16)
    vb_ref[...] = u_v.astype(BF16)

    t = seg_finish(p_qa, ss_qa, aqg_ref[...], HD)
    qa_ref[...] = (rope64(t) * (HD ** -0.5 * LOG2E)).astype(BF16)
    t = seg_finish(p_ka, ss_ka, akg_ref[...], HD)
    if not is_lat:
        store_kv_state(kas_ref, t)
    ka_ref[...] = rope64(t).astype(BF16)
    t = seg_finish(p_qc, ss_qc, cqg_ref[...], HD)
    qc_ref[...] = (rope64(t) * (HD ** -0.5 * LOG2E)).astype(BF16)
    t = seg_finish(p_kc, ss_kc, ckg_ref[...], HD)
    if not is_lat:
        store_kv_state(kcs_ref, t)
    kc_ref[...] = rope64(t).astype(BF16)
    t = seg_finish(u_q, ss_qb, bqg_ref[...], B_QK)
    qb_ref[...] = (rope96(t) * (B_QK ** -0.5 * LOG2E)).astype(BF16)
    t = seg_finish(u_k, ss_kb, bkg_ref[...], B_QK)
    kb_ref[...] = rope96(t).astype(BF16)


def _full(shape):
    nd = len(shape)
    return pl.BlockSpec(shape, lambda i: (0,) * nd)


STATE_TAILS = ((2, HD), (2, HD), (B_KV_LORA,), (B_ROPE,), (2, HD), (2, HD))


def _layer_spec(a, l):
    nd = a.ndim - 1
    return pl.BlockSpec((None,) + a.shape[1:], lambda *_: (l,) + (0,) * nd)


VEC_ROWS = dict(n1g=(0, D), n2g=(1, D), aqg=(2, 4 * HD), akg=(3, 2 * HD), cqg=(4, 4 * HD), ckg=(5, 2 * HD),
                bcqg=(6, B_Q_LORA), bckvg=(7, B_KV_LORA), bqg=(8, 4 * BP), bkg=(9, 4 * BP))


def _vec_spec(l, name):
    row, width = VEC_ROWS[name]
    return pl.BlockSpec((None, None, 1, width), lambda *_: (l, row, 0, 0))


def _projection(is_lat, l, x_src, x_off, mod, lw, consts, prev_states=()):
    assert is_lat or bool(prev_states) == (l > 0)
    t_rows = T_LAT if is_lat else T_CTX
    TQ = TQ_LAT if is_lat else TQ_CTX
    n_tiles = t_rows // TQ
    if is_lat:
        mod_map = lambda i: (l, 1 + i // (LAT_LEN // TQ), 0, 0)
    else:
        mod_map = lambda i: (l, 0, 0, 0)
    vec = lw["vec"]
    gain = lambda name: (vec, _vec_spec(l, name))
    whole = lambda a: (a, _full(a.shape))
    layer = lambda a: (a, _layer_spec(a, l))
    pairs = [(x_src, pl.BlockSpec((TQ, D), lambda i: (i + x_off, 0))),
             (mod, pl.BlockSpec((None, None, 1, 6 * D), mod_map)),
             gain("n1g"), layer(lw["w_in"]), whole(consts["seg64"]), whole(consts["seg96"]),
             gain("aqg"), gain("akg"), gain("cqg"), gain("ckg"), gain("bcqg"), gain("bckvg"), gain("bqg"), gain("bkg"),
             layer(lw["w_uq"]), layer(lw["w_uk"]), layer(lw["w_uv"]), whole(consts["place"])]
    ins = [a for a, _ in pairs]
    in_specs = [s for _, s in pairs]
    n_plain = len(ins)
    ins += list(prev_states)
    in_specs += [pl.BlockSpec(memory_space=pl.ANY) for _ in prev_states]
    if is_lat:
        tabs = [consts["cos64"], consts["sin64"], consts["cos96"], consts["sin96"]]
        ins += tabs + [consts["first64"], consts["first96"]]
        in_specs += [pl.BlockSpec((TQ, a.shape[1]), lambda i: (i % (LAT_LEN // TQ), 0)) for a in tabs]
        in_specs += [_full(consts["first64"].shape), _full(consts["first96"].shape)]
    widths = [4 * HD, 2 * HD, 2 * HD, 4 * BP, 4 * BP, 4 * B_V, 4 * HD, 2 * HD, 2 * HD]
    out_shape = [jax.ShapeDtypeStruct((t_rows, w), BF16) for w in widths]
    out_specs = [pl.BlockSpec((TQ, w), lambda i: (i, 0)) for w in widths]
    if not is_lat:
        for tail in STATE_TAILS:
            zeros = (0,) * len(tail)
            out_shape.append(jax.ShapeDtypeStruct((N_CTX_SEQ, DEPTH, CTX_LEN) + tail, F32))
            if prev_states:
                out_specs.append(pl.BlockSpec((SEQ_PER_STEP, None, CTX_LEN) + tail, lambda i, z=zeros: (i, l, 0) + z))
            else:
                out_specs.append(pl.BlockSpec((SEQ_PER_STEP, DEPTH, CTX_LEN) + tail, lambda i, z=zeros: (i, 0, 0) + z))
    aliases = {n_plain + k: len(widths) + k for k in range(len(prev_states))}
    return pl.pallas_call(
        functools.partial(_proj_kernel, is_lat, len(prev_states)),
        grid=(n_tiles,),
        in_specs=in_specs,
        out_specs=out_specs,
        out_shape=out_shape,
        input_output_aliases=aliases,
        compiler_params=pltpu.CompilerParams(
            dimension_semantics=("arbitrary",), vmem_limit_bytes=VMEM_LIMIT),
        name="proj_lat" if is_lat else "proj_ctx",
    )(*ins)


def _cache_kv_kernel(ckv_ref, kr_ref, seg96_ref, bkg_ref, wuk_ref, wuv_ref, plc_ref, kb_ref, vb_ref):
    ckv16 = ckv_ref[...].astype(BF16)
    vb_ref[...] = _dot(ckv16, wuv_ref[...]).astype(BF16)
    kpre = _dot(ckv16, wuk_ref[...]) + _dot(kr_ref[...].astype(BF16), plc_ref[0:B_ROPE, :])
    kb_ref[...] = _seg_norm(kpre, seg96_ref[...], bkg_ref[...], B_QK).astype(BF16)


def _cache_kv(cache_b_ckv, cache_b_krope, lw, consts):
    def c_map(l, r):
        return (r // (PAST // TM), l, r % (PAST // TM), 0)

    def w_spec(a):
        return pl.BlockSpec((None,) + a.shape[1:], lambda l, r: (l,) + (0,) * (a.ndim - 1))

    def s_spec(a):
        return pl.BlockSpec(a.shape, lambda l, r: (0,) * a.ndim)

    rows = N_LAT_SEQ * PAST
    return pl.pallas_call(
        _cache_kv_kernel,
        grid=(DEPTH, rows // TM),
        in_specs=[pl.BlockSpec((None, None, TM, B_KV_LORA), c_map), pl.BlockSpec((None, None, TM, B_ROPE), c_map),
                  s_spec(consts["seg96"]),
                  pl.BlockSpec((None, None, 1, 4 * BP), lambda l, r: (l, VEC_ROWS["bkg"][0], 0, 0)),
                  w_spec(lw["w_uk"]), w_spec(lw["w_uv"]),
                  s_spec(consts["place"])],
        out_specs=[pl.BlockSpec((None, TM, 4 * BP), lambda l, r: (l, r, 0)),
                   pl.BlockSpec((None, TM, 4 * B_V), lambda l, r: (l, r, 0))],
        out_shape=[jax.ShapeDtypeStruct((DEPTH, rows, 4 * BP), BF16),
                   jax.ShapeDtypeStruct((DEPTH, rows, 4 * B_V), BF16)],
        compiler_params=pltpu.CompilerParams(
            dimension_semantics=("arbitrary", "arbitrary"), vmem_limit_bytes=VMEM_LIMIT),
        name="cache_kv",
    )(cache_b_ckv, cache_b_krope, consts["seg96"], lw["vec"], lw["w_uk"], lw["w_uv"], consts["place"])


def _scores(q, segs):
    scores = []
    for k, _, mask in segs:
        s = _dot_t(q, k)
        if mask is not None:
            s = jnp.where(mask, s, NEG_INF)
        scores.append(s)
    return scores


def _softmax_pv(scores, segs, sink, num_cols, den_col):
    m = None
    for s in scores:
        sm = jnp.max(s, axis=-1, keepdims=True)
        m = sm if m is None else jnp.maximum(m, sm)
    if sink is not None:
        m = jnp.maximum(m, sink)
    acc = None
    den = None
    for s, (_, v, _) in zip(scores, segs):
        e = jnp.exp2(s - m)
        if den_col is None:
            d = jnp.sum(e, axis=-1, keepdims=True)
            den = d if den is None else den + d
        o = _dot(e.astype(BF16), v)
        acc = o if acc is None else acc + o
    if den_col is not None:
        den = acc[:, den_col:den_col + 1]
        acc = acc[:, num_cols]
    if sink is not None:
        den = den + jnp.exp2(sink - m)
    return acc / den


def _run_heads(jobs, o_ref):
    nxt = _scores(jobs[0][0](), jobs[0][1]())
    for n, (_, load_segs, sink, col, num_cols, den_col) in enumerate(jobs):
        scores = nxt
        if n + 1 < len(jobs):
            nxt = _scores(jobs[n + 1][0](), jobs[n + 1][1]())
        o = _softmax_pv(scores, load_segs(values=True), sink, num_cols, den_col)
        o_ref[:, col:col + o.shape[1]] = o.astype(BF16)


OUT_B = 4 * HD
OUT_C = 4 * HD + 4 * B_V
LOG2E = 1.4426950408889634


def _head_job(q_ref, q_cols, seg_refs, kv_cols, v_cols, sink, col, rows=None, masks=None, mxu_sums=True):
    width = v_cols.stop - v_cols.start
    if not mxu_sums:
        load_cols, num_cols, den_col = v_cols, None, None
    elif width == HD:
        pair = v_cols.start // LANE * LANE
        load_cols = slice(pair, pair + LANE)
        lo = v_cols.start - pair
        num_cols, den_col = slice(lo, lo + HD), (lo + HD) % LANE
    else:
        load_cols, num_cols, den_col = v_cols, slice(0, width), width
    if sink is not None:
        sink = sink * LOG2E

    def with_ones(v):
        if not mxu_sums:
            return v
        if width != HD:
            return jnp.concatenate([v, jnp.ones_like(v)], axis=1)
        lane = lax.broadcasted_iota(jnp.int32, (1, LANE), 1)
        keep = jnp.where((lane >= num_cols.start) & (lane < num_cols.stop), 1.0, 0.0).astype(BF16)
        return v * keep + (1.0 - keep)

    def load_segs(values=False):
        segs = []
        for n, (k_ref, v_ref) in enumerate(seg_refs):
            r = rows if (rows is not None and n == 0) else slice(None)
            if values:
                segs.append((None, with_ones(v_ref[r, load_cols]), None))
            else:
                segs.append((k_ref[r, kv_cols], None, None if masks is None else masks[n]))
        return segs

    return (lambda: q_ref[:, q_cols]), load_segs, sink, col, num_cols, den_col


def _attn_ctx_kernel(l, sink_ref, qa_ref, ka_ref, va_ref, qb_ref, kb_ref, vb_ref, qc_ref, kc_ref, vc_ref, o_ref):
    jobs = []
    for h in range(4):
        cs = slice(h // 2 * HD, (h // 2 + 1) * HD)
        jobs.append(_head_job(qa_ref, slice(h * HD, (h + 1) * HD), [(ka_ref, va_ref)], cs, cs, sink_ref[l, h],
                              h * HD, mxu_sums=False))
    for h in range(4):
        ks, vs = slice(h * BP, (h + 1) * BP), slice(h * B_V, (h + 1) * B_V)
        jobs.append(_head_job(qb_ref, ks, [(kb_ref, vb_ref)], ks, vs, None, OUT_B + h * B_V, mxu_sums=False))
    for h in range(4):
        cs = slice(h // 2 * HD, (h // 2 + 1) * HD)
        jobs.append(_head_job(qc_ref, slice(h * HD, (h + 1) * HD), [(kc_ref, vc_ref)], cs, cs, None, OUT_C + h * HD,
                              mxu_sums=False))
    _run_heads(jobs, o_ref)


def _attention_ctx(l, sink, p):
    names = ["qa", "ka", "va", "qb", "kb", "vb", "qc", "kc", "vc"]
    ins = [p[n] for n in names]
    in_specs = [pl.BlockSpec(memory_space=pltpu.SMEM)]
    in_specs += [pl.BlockSpec((CTX_LEN, a.shape[1]), lambda i: (i, 0)) for a in ins]
    return pl.pallas_call(
        functools.partial(_attn_ctx_kernel, l),
        grid=(N_CTX_SEQ,),
        in_specs=in_specs,
        out_specs=pl.BlockSpec((CTX_LEN, D), lambda i: (i, 0)),
        out_shape=jax.ShapeDtypeStruct((T_CTX, D), BF16),
        compiler_params=pltpu.CompilerParams(
            dimension_semantics=("arbitrary",), vmem_limit_bytes=VMEM_LIMIT),
        name="attn_ctx",
    )(sink, *ins)


WIN_SPAN = TM + 2 * WINDOW


def _attn_lat_kernel(l, sink_ref, qa_ref, qb_ref, qc_ref, ka_ref, va_ref, kb_ref, vb_ref, kc_ref, vc_ref,
                     cka_ref, cva_ref, ckb_ref, cvb_ref, ckc_ref, cvc_ref, o_ref):
    qi = pl.program_id(1)
    ws = pl.multiple_of(jnp.clip(qi * TM - WINDOW, 0, LAT_LEN - WIN_SPAN), WINDOW)
    qpos = qi * TM + lax.broadcasted_iota(jnp.int32, (TM, WIN_SPAN), 0)
    kpos = ws + lax.broadcasted_iota(jnp.int32, (TM, WIN_SPAN), 1)
    band = jnp.abs(qpos - kpos) <= WINDOW
    jobs = []
    for h in range(4):
        cs = slice(h // 2 * HD, (h // 2 + 1) * HD)
        jobs.append(_head_job(qa_ref, slice(h * HD, (h + 1) * HD), [(ka_ref, va_ref), (cka_ref, cva_ref)], cs, cs,
                              sink_ref[l, h], h * HD, rows=pl.ds(ws, WIN_SPAN), masks=(band, None)))
    for h in range(4):
        ks, vs = slice(h * BP, (h + 1) * BP), slice(h * B_V, (h + 1) * B_V)
        jobs.append(_head_job(qb_ref, ks, [(kb_ref, vb_ref), (ckb_ref, cvb_ref)], ks, vs, None, OUT_B + h * B_V))
    for h in range(4):
        cs = slice(h // 2 * HD, (h // 2 + 1) * HD)
        jobs.append(_head_job(qc_ref, slice(h * HD, (h + 1) * HD), [(kc_ref, vc_ref), (ckc_ref, cvc_ref)], cs, cs,
                              None, OUT_C + h * HD))
    _run_heads(jobs, o_ref)


def _attention_lat(l, sink, p, cache):
    q_ins = [p["qa"], p["qb"], p["qc"]]
    kv_ins = [p[n] for n in ("ka", "va", "kb", "vb", "kc", "vc")]
    c_names = ("ka", "va", "kb", "vb", "kc", "vc")
    c_ins = [cache[n] for n in c_names]
    in_specs = [pl.BlockSpec(memory_space=pltpu.SMEM)]
    in_specs += [pl.BlockSpec((TM, a.shape[1]), lambda b, i: (b * LAT_TILES_PER_SEQ + i, 0)) for a in q_ins]
    in_specs += [pl.BlockSpec((LAT_LEN, a.shape[1]), lambda b, i: (b, 0)) for a in kv_ins]
    for n, a in zip(c_names, c_ins):
        if n in ("kb", "vb"):
            in_specs.append(pl.BlockSpec((None, PAST, a.shape[-1]), lambda b, i: (l, b, 0)))
        else:
            in_specs.append(pl.BlockSpec((None, None, PAST, a.shape[-1]), lambda b, i: (b, l, 0, 0)))
    return pl.pallas_call(
        functools.partial(_attn_lat_kernel, l),
        grid=(N_LAT_SEQ, LAT_TILES_PER_SEQ),
        in_specs=in_specs,
        out_specs=pl.BlockSpec((TM, D), lambda b, i: (b * LAT_TILES_PER_SEQ + i, 0)),
        out_shape=jax.ShapeDtypeStruct((T_LAT, D), BF16),
        compiler_params=pltpu.CompilerParams(
            dimension_semantics=("arbitrary", "arbitrary"), vmem_limit_bytes=VMEM_LIMIT),
        name="attn_lat",
    )(sink, *q_ins, *kv_ins, *c_ins)


def _pack_rows(x):
    half = x.shape[1] // 2
    r = x.astype(BF16).astype(F32)
    hi = lax.bitcast_convert_type(r[:, :half], jnp.int32)
    lo = lax.bitcast_convert_type(r[:, half:], jnp.int32)
    return jnp.bitwise_or(hi, lax.shift_right_logical(lo, 16))


def _unpack_rows(p):
    a = lax.bitcast_convert_type(jnp.bitwise_and(p, -65536), F32)
    b = lax.bitcast_convert_type(lax.shift_left(p, 16), F32)
    return jnp.concatenate([a, b], axis=1).astype(BF16)


ROUTE_ROWS = 16


def _post_kernel(mc_ref, ml_ref, xc_ref, xl_ref, mod_ref, n2g_ref, wout_ref, rwt_ref, rb_ref, tri_ref,
                 xo_ref, h2_ref, gate_ref, route_ref, cnt_ref, run_ref):
    i = pl.program_id(0)

    @pl.when(i == 0)
    def _():
        run_ref[...] = jnp.zeros_like(run_ref)
        cnt_ref[...] = jnp.zeros_like(cnt_ref)

    is_ctx = i < T_CTX // TP
    mix = jnp.where(is_ctx, mc_ref[...], ml_ref[...])
    m = mod_ref[...]
    g1, sh2, sc2 = m[:, 2 * D:3 * D], m[:, 3 * D:4 * D], m[:, 4 * D:5 * D]
    x = jnp.where(is_ctx, xc_ref[...], xl_ref[...]) + g1 * _dot(mix, wout_ref[...])
    xo_ref[...] = x
    h2 = _rms(x, n2g_ref[...]) * (1.0 + sc2) + sh2
    hi = h2.astype(BF16)
    h2_ref[...] = _pack_rows(h2)
    lo = (h2 - hi.astype(F32)).astype(BF16)
    a = _dot_t(rwt_ref[...], hi)
    b = _dot_t(rwt_ref[0:N_EXPERTS, :], lo)
    logits = a[0:N_EXPERTS] + a[N_EXPERTS:] + b + rb_ref[...]
    eidx = lax.broadcasted_iota(jnp.int32, logits.shape, 0).astype(F32)
    work = logits
    vals, idxs, hots = [], [], []
    for _ in range(TOP_K):
        v = jnp.max(work, axis=0, keepdims=True)
        idx = jnp.min(jnp.where(work == v, eidx, float(N_EXPERTS)), axis=0, keepdims=True)
        hot = eidx == idx
        vals.append(v)
        idxs.append(idx)
        hots.append(hot)
        work = jnp.where(hot, -jnp.inf, work)
    es = [jnp.exp(v - vals[0]) for v in vals]
    den = es[0] + es[1] + es[2] + es[3]
    gates = [e / den for e in es]
    sel = jnp.where(hots[0] | hots[1] | hots[2] | hots[3], 1.0, 0.0)
    run = run_ref[:, 0:1]
    before = _dot(sel.astype(BF16), tri_ref[...]) + run
    run_new = jnp.broadcast_to(run + jnp.sum(sel, axis=1, keepdims=True), run_ref.shape)
    run_ref[...] = run_new
    cnt_ref[...] = run_new
    ranks = [jnp.sum(jnp.where(hots[k], before, 0.0), axis=0, keepdims=True) for k in range(TOP_K)]
    pad = [jnp.zeros((ROUTE_ROWS - 3 * TOP_K, TP), F32)]
    route_ref[...] = jnp.concatenate(gates + idxs + ranks + pad, axis=0)
    gate_ref[...] = jnp.concatenate(gates + [jnp.zeros((LANE - TOP_K, TP), F32)], axis=0).T


def _mod_row(i, tm=TM):
    n_ctx = T_CTX // tm
    return jnp.where(i < n_ctx, 0, 1 + (i - n_ctx) // (LAT_LEN // tm))


def _ctx_tile(i):
    return jnp.minimum(i, T_CTX // TP - 1)


def _lat_tile(i):
    return jnp.maximum(i - T_CTX // TP, 0)


def _post_attention(l, mix_ctx, mix_lat, x_ctx, x_lat, x_lat_off, mod, lw, consts):
    ins = [mix_ctx, mix_lat, x_ctx, x_lat, mod, lw["vec"], lw["w_out"], lw["rwt"], lw["rb"], consts["tri"]]
    in_specs = [
        pl.BlockSpec((TP, D), lambda i: (_ctx_tile(i), 0)),
        pl.BlockSpec((TP, D), lambda i: (_lat_tile(i), 0)),
        pl.BlockSpec((TP, D), lambda i: (_ctx_tile(i), 0)),
        pl.BlockSpec((TP, D), lambda i: (_lat_tile(i) + x_lat_off, 0)),
        pl.BlockSpec((None, None, 1, 6 * D), lambda i: (l, _mod_row(i, TP), 0, 0)),
        _vec_spec(l, "n2g"),
    ] + [_layer_spec(a, l) for a in ins[6:9]] + [_full(consts["tri"].shape)]
    return pl.pallas_call(
        _post_kernel,
        grid=(T_ALL // TP,),
        in_specs=in_specs,
        out_specs=[pl.BlockSpec((TP, D), lambda i: (i, 0)), pl.BlockSpec((TP, DW), lambda i: (i, 0)),
                   pl.BlockSpec((TP, LANE), lambda i: (i, 0)), pl.BlockSpec((ROUTE_ROWS, TP), lambda i: (0, i)),
                   pl.BlockSpec((N_EXPERTS, LANE), lambda i: (0, 0))],
        out_shape=[jax.ShapeDtypeStruct((T_ALL, D), F32), jax.ShapeDtypeStruct((T_ALL, DW), jnp.int32),
                   jax.ShapeDtypeStruct((T_ALL, LANE), F32), jax.ShapeDtypeStruct((ROUTE_ROWS, T_ALL), F32),
                   jax.ShapeDtypeStruct((N_EXPERTS, LANE), F32)],
        scratch_shapes=[pltpu.VMEM((N_EXPERTS, LANE), F32)],
        compiler_params=pltpu.CompilerParams(
            dimension_semantics=("arbitrary",), vmem_limit_bytes=VMEM_LIMIT),
        name="post_attn",
    )(*ins)


def _expert_kernel(l, te_ref, nu_ref, nxt_ref, x_ref, wgu_hbm, bgu_ref, wdn_hbm, bdn_ref, y_ref,
                   wgu32, wdn32, wgu16, wdn16, sem):
    i = pl.program_id(0)
    prev = te_ref[jnp.maximum(i - 1, 0)]
    new_expert = jnp.logical_or(i == 0, te_ref[i] != prev)

    def weight_copies(e):
        return (pltpu.make_async_copy(wgu_hbm.at[l, e], wgu32, sem.at[0]),
                pltpu.make_async_copy(wdn_hbm.at[l, e], wdn32, sem.at[1]))

    @pl.when(i == 0)
    def _():
        for cp in weight_copies(te_ref[0]):
            cp.start()

    @pl.when(new_expert)
    def _():
        cps = weight_copies(te_ref[i])
        cps[0].wait()
        wgu16[...] = wgu32[...].astype(BF16)
        cps[1].wait()
        wdn16[...] = wdn32[...].astype(BF16)

        @pl.when(nxt_ref[i] >= 0)
        def _():
            for cp in weight_copies(nxt_ref[i]):
                cp.start()

    @pl.when(i < nu_ref[0])
    def _():
        x = _unpack_rows(x_ref[...])

        def gate_up(a):
            b = a + FF_CHUNK
            g = _dot(x, wgu16[:, a:b]) + bgu_ref[:, a:b]
            up = _dot(x, wgu16[:, D_FF + a:D_FF + b]) + bgu_ref[:, D_FF + a:D_FF + b]
            return g, up

        def activate(g, up):
            g = jnp.minimum(g, SWIGLU_LIMIT)
            up = jnp.clip(up, -SWIGLU_LIMIT, SWIGLU_LIMIT)
            return ((up + 1.0) * (g * jax.nn.sigmoid(SWIGLU_ALPHA * g))).astype(BF16)

        starts = list(range(0, D_FF, FF_CHUNK))
        acc = None
        pre = gate_up(starts[0])
        for n, a in enumerate(starts):
            hid = activate(*pre)
            if n + 1 < len(starts):
                pre = gate_up(starts[n + 1])
            o = _dot(hid, wdn16[a:a + FF_CHUNK, :])
            acc = o if acc is None else acc + o
        y_ref[...] = _pack_rows(acc + bdn_ref[...])


def _experts(l, tile_expert, n_used, next_expert, x_sorted, w_gu, b_gu, w_dn, b_dn):
    def row_map(i, te, nu, nxt):
        return (jnp.minimum(i, nu[0] - 1), 0)

    def b_map(i, te, nu, nxt):
        return (l, te[i], 0, 0)

    grid_spec = pltpu.PrefetchScalarGridSpec(
        num_scalar_prefetch=3,
        grid=(MOE_TILES,),
        in_specs=[
            pl.BlockSpec((MOE_TM, DW), row_map),
            pl.BlockSpec(memory_space=pl.ANY),
            pl.BlockSpec((None, None, 1, 2 * D_FF), b_map),
            pl.BlockSpec(memory_space=pl.ANY),
            pl.BlockSpec((None, None, 1, D), b_map),
        ],
        out_specs=pl.BlockSpec((MOE_TM, DW), row_map),
        scratch_shapes=[pltpu.VMEM((D, 2 * D_FF), F32), pltpu.VMEM((D_FF, D), F32),
                        pltpu.VMEM((D, 2 * D_FF), BF16), pltpu.VMEM((D_FF, D), BF16),
                        pltpu.SemaphoreType.DMA((2,))],
    )
    return pl.pallas_call(
        functools.partial(_expert_kernel, l),
        grid_spec=grid_spec,
        out_shape=jax.ShapeDtypeStruct((MOE_ROWS, DW), jnp.int32),
        compiler_params=pltpu.CompilerParams(
            dimension_semantics=("arbitrary",), vmem_limit_bytes=VMEM_LIMIT),
        name="experts",
    )(tile_expert, n_used, next_expert, x_sorted, w_gu, b_gu.reshape(DEPTH, N_EXPERTS, 1, 2 * D_FF), w_dn,
      b_dn.reshape(DEPTH, N_EXPERTS, 1, D))


def _combine_kernel(x_ref, y_ref, route_ref, mod_ref, o_ref):
    g2 = mod_ref[...][:, 5 * D:6 * D]
    r = route_ref[...]
    acc = r[:, 0:1] * _unpack_rows(y_ref[0]).astype(F32)
    for k in range(1, TOP_K):
        acc = acc + r[:, k:k + 1] * _unpack_rows(y_ref[k]).astype(F32)
    o_ref[...] = x_ref[...] + g2 * acc


def _combine(l, first_tile, n_tiles, x_all, y_tok, gates, mod):
    return pl.pallas_call(
        _combine_kernel,
        grid=(n_tiles,),
        in_specs=[
            pl.BlockSpec((TM, D), lambda i: (i + first_tile, 0)),
            pl.BlockSpec((TOP_K, TM, DW), lambda i: (0, i, 0)),
            pl.BlockSpec((TM, LANE), lambda i: (i + first_tile, 0)),
            pl.BlockSpec((None, None, 1, 6 * D), lambda i: (l, _mod_row(i + first_tile), 0, 0)),
        ],
        out_specs=pl.BlockSpec((TM, D), lambda i: (i, 0)),
        out_shape=jax.ShapeDtypeStruct((n_tiles * TM, D), F32),
        compiler_params=pltpu.CompilerParams(
            dimension_semantics=("arbitrary",), vmem_limit_bytes=VMEM_LIMIT),
        name="combine",
    )(x_all, y_tok, gates, mod)


SC_CORES, SC_SUBCORES = 2, 16
SC_WORKERS = SC_CORES * SC_SUBCORES
SC_CHUNK = 128
SC_CHUNKS_PER_WORKER = T_ALL // SC_CHUNK // SC_WORKERS


def _sc_mesh():
    return plsc.VectorSubcoreMesh(core_axis_name="c", subcore_axis_name="s")


def _sc_scratch():
    return [pltpu.VMEM((TOP_K, SC_CHUNK), jnp.int32), pltpu.VMEM((SC_CHUNK, DW), jnp.int32),
            pltpu.SemaphoreType.DMA]


def _dispatch_rows(h2p, dest):
    @functools.partial(pl.kernel, mesh=_sc_mesh(), out_type=jax.ShapeDtypeStruct((MOE_ROWS, DW), jnp.int32),
                       scratch_types=_sc_scratch(), name="dispatch_rows")
    def run(h_hbm, d_hbm, o_hbm, idx_v, rows_v, sem):
        wid = lax.axis_index("s") * SC_CORES + lax.axis_index("c")
        for j in range(SC_CHUNKS_PER_WORKER):
            c = wid * SC_CHUNKS_PER_WORKER + j
            pltpu.sync_copy(d_hbm.at[c], idx_v)
            pltpu.sync_copy(h_hbm.at[pl.ds(c * SC_CHUNK, SC_CHUNK)], rows_v)
            copies = [pltpu.async_copy(rows_v, o_hbm.at[idx_v.at[k]], sem) for k in range(TOP_K)]
            for cp in copies:
                cp.wait()

    return run(h2p, dest)


def _gather_rows(y, dest, first_chunk, n_chunks):
    per_worker = n_chunks // SC_WORKERS
    assert per_worker * SC_WORKERS == n_chunks

    @functools.partial(pl.kernel, mesh=_sc_mesh(),
                       out_type=jax.ShapeDtypeStruct((TOP_K, n_chunks * SC_CHUNK, DW), jnp.int32),
                       scratch_types=_sc_scratch(), name="gather_rows")
    def run(y_hbm, d_hbm, o_hbm, idx_v, rows_v, sem):
        wid = lax.axis_index("s") * SC_CORES + lax.axis_index("c")
        for j in range(per_worker):
            c = wid * per_worker + j
            pltpu.sync_copy(d_hbm.at[first_chunk + c], idx_v)
            for k in range(TOP_K):
                pltpu.async_copy(y_hbm.at[idx_v.at[k]], rows_v, sem).wait()
                pltpu.sync_copy(rows_v, o_hbm.at[k, pl.ds(c * SC_CHUNK, SC_CHUNK)])

    return run(y, dest)


def _constants():
    lane64 = np.arange(4 * HD)
    seg64 = (lane64[:, None] // HD == lane64[None, :] // HD).astype(np.float32)
    lane96 = np.arange(4 * BP)
    real = lane96 % BP < B_QK
    seg96 = ((lane96[:, None] // BP == lane96[None, :] // BP) & real[:, None] & real[None, :]).astype(np.float32)
    place = np.zeros((LANE, 4 * BP), np.float32)
    for hh in range(4):
        place[np.arange(B_ROPE), hh * BP + B_NOPE + np.arange(B_ROPE)] = 1.0

    def angles(rot_dim):
        pos = np.arange(LAT_LEN)
        rows = (pos // GRID_W).astype(np.float32)
        cols = (pos % GRID_W).astype(np.float32)
        axis_dim = rot_dim // 2
        inv = np.power(np.float32(ROPE_THETA), -(np.arange(0, axis_dim, 2, dtype=np.float32) / np.float32(axis_dim)))
        ang = np.concatenate([rows[:, None] * inv, cols[:, None] * inv], axis=-1).astype(np.float32)
        return np.cos(ang), np.sin(ang)

    def head_tables(rot_dim):
        cos, sin = angles(rot_dim)
        q = rot_dim // 4
        cr, cc, sr, sc = cos[:, :q], cos[:, q:], sin[:, :q], sin[:, q:]
        return (np.concatenate([cr, cr, cc, cc], axis=-1), np.concatenate([-sr, sr, -sc, sc], axis=-1))

    c64, s64 = head_tables(HD)
    cos64 = np.tile(c64, (1, 4))
    sin64 = np.tile(s64, (1, 4))
    c32, s32 = head_tables(B_ROPE)
    ones = np.ones((LAT_LEN, B_NOPE), np.float32)
    zeros = np.zeros((LAT_LEN, B_NOPE), np.float32)
    padz = np.zeros((LAT_LEN, BP - B_QK), np.float32)
    cos96 = np.tile(np.concatenate([ones, c32, padz], axis=-1), (1, 4))
    sin96 = np.tile(np.concatenate([zeros, s32, padz], axis=-1), (1, 4))
    first64 = ((lane64 % 32) < 16).astype(np.float32)[None, :]
    first96 = (((lane96 % BP) % 16) < 8).astype(np.float32)[None, :]
    tri = (np.arange(TP)[:, None] < np.arange(TP)[None, :]).astype(np.float32)
    f32 = lambda a: jnp.asarray(a, F32)
    return dict(seg64=jnp.asarray(seg64, BF16), seg96=jnp.asarray(seg96, BF16), place=jnp.asarray(place, BF16),
                tri=jnp.asarray(tri, BF16),
                cos64=f32(cos64), sin64=f32(sin64), cos96=f32(cos96), sin96=f32(sin96),
                first64=f32(first64), first96=f32(first96))


def _pad_heads(w, per_head, width):
    lead = w.shape[:-1]
    w = w.reshape(lead + (4, per_head))
    return jnp.pad(w, ((0, 0),) * (len(lead) + 1) + ((0, width - per_head),)).reshape(lead + (4 * width,))


def _weights(norm1_g, norm2_g, w_in, a_q_g, a_k_g, b_cq_g, b_ckv_g, w_uq, w_ukv, b_q_g, b_k_g,
             c_q_g, c_k_g, w_out, router_w, router_b):
    o = np.cumsum((0, 256, 128, 128, 384, 256, 32, 256, 128, 128))
    seg = lambda k: w_in[:, :, o[k]:o[k + 1]]
    w_in_r = jnp.concatenate([seg(0), seg(1), seg(2), seg(3), seg(4), seg(6), seg(7), seg(8), seg(5),
                              jnp.zeros((DEPTH, D, LANE - B_ROPE), F32)], axis=-1).astype(BF16)
    ukv = w_ukv.reshape(DEPTH, B_KV_LORA, 4, B_NOPE + B_V)
    w_uk = _pad_heads(ukv[..., :B_NOPE].reshape(DEPTH, B_KV_LORA, 4 * B_NOPE), B_NOPE, BP)
    w_uv = ukv[..., B_NOPE:].reshape(DEPTH, B_KV_LORA, 4 * B_V)
    rw_hi = router_w.astype(BF16)
    rw_lo = (router_w - rw_hi.astype(F32)).astype(BF16)
    tile = lambda v, n: jnp.tile(v, (1, n))
    rows = dict(n1g=norm1_g, n2g=norm2_g, aqg=tile(a_q_g, 4), akg=tile(a_k_g, 2), cqg=tile(c_q_g, 4),
                ckg=tile(c_k_g, 2), bcqg=b_cq_g, bckvg=b_ckv_g,
                bqg=_pad_heads(tile(b_q_g, 4), B_QK, BP), bkg=_pad_heads(tile(b_k_g, 4), B_QK, BP))
    order = sorted(VEC_ROWS, key=lambda n: VEC_ROWS[n][0])
    assert all(rows[n].shape == (DEPTH, VEC_ROWS[n][1]) for n in order)
    vec = jnp.stack([jnp.pad(rows[n], ((0, 0), (0, D - rows[n].shape[1]))) for n in order], axis=1)
    return dict(
        vec=vec[:, :, None, :], w_in=w_in_r,
        w_uq=_pad_heads(w_uq, B_QK, BP).astype(BF16), w_uk=w_uk.astype(BF16), w_uv=w_uv.astype(BF16),
        w_out=w_out.astype(BF16),
        rwt=jnp.concatenate([jnp.swapaxes(rw_hi, 1, 2), jnp.swapaxes(rw_lo, 1, 2)], axis=1),
        rb=router_b[:, :, None])


def _moe(l, x_all, h2p, gate_slab, route, counts, mod, w_gu, b_gu, w_dn, b_dn):
    counts = counts[:, 0].astype(jnp.int32)
    padded = (counts + MOE_TM - 1) // MOE_TM * MOE_TM
    pend = jnp.cumsum(padded)
    pstart = pend - padded
    n_used = (pend[-1] // MOE_TM).astype(jnp.int32)
    tiles = jnp.minimum(jnp.arange(MOE_TILES, dtype=jnp.int32), n_used - 1)
    tile_expert = jnp.sum((pend[None, :] <= tiles[:, None] * MOE_TM).astype(jnp.int32), axis=1)
    tile_expert = jnp.minimum(tile_expert, N_EXPERTS - 1)
    group_end_tile = jnp.sum(jnp.where(tile_expert[:, None] == jnp.arange(N_EXPERTS)[None, :], pend[None, :], 0),
                             axis=1) // MOE_TM
    after = jnp.sum((pend[None, :] <= group_end_tile[:, None] * MOE_TM).astype(jnp.int32), axis=1)
    next_expert = jnp.where(group_end_tile < n_used, jnp.minimum(after, N_EXPERTS - 1), -1).astype(jnp.int32)
    e = route[TOP_K:2 * TOP_K].astype(jnp.int32)
    rank = route[2 * TOP_K:3 * TOP_K].astype(jnp.int32)
    start = jnp.sum(jnp.where(e[:, :, None] == jnp.arange(N_EXPERTS)[None, None, :], pstart[None, None, :], 0), axis=-1)
    dest = (start + rank).reshape(TOP_K, T_ALL // SC_CHUNK, SC_CHUNK).transpose(1, 0, 2)
    x_sorted = _dispatch_rows(h2p, dest)
    y = _experts(l, tile_expert, n_used.reshape(1), next_expert, x_sorted, w_gu, b_gu, w_dn, b_dn)
    ctx_chunks, lat_chunks = T_CTX // SC_CHUNK, T_LAT // SC_CHUNK
    y_ctx = _gather_rows(y, dest, 0, ctx_chunks)
    y_lat = _gather_rows(y, dest, ctx_chunks, lat_chunks)
    return (_combine(l, 0, N_CTX_TILES, x_all, y_ctx, gate_slab, mod),
            _combine(l, N_CTX_TILES, N_LAT_TILES, x_all, y_lat, gate_slab, mod))


def kernel(x_prompt, x_sample, cache_a_k, cache_a_v, cache_b_ckv, cache_b_krope, cache_c_k, cache_c_v, c, c_ctx,
           norm1_g, norm2_g, w_mod, b_mod, w_in, a_q_g, a_k_g, a_sink, b_cq_g, b_ckv_g, w_uq, w_ukv, b_q_g, b_k_g,
           c_q_g, c_k_g, w_out, router_w, router_b, w_gu, b_gu, w_dn, b_dn):
    consts = _constants()
    cond = jnp.concatenate([c_ctx[None, :], c, jnp.zeros((3, D), F32)], axis=0)
    mod = _modulation(cond, w_mod, b_mod).reshape(DEPTH, 8, 1, 6 * D)
    x_ctx, x_lat, x_lat_off = x_prompt.reshape(T_CTX, D), x_sample.reshape(T_LAT, D), 0

    lw = _weights(norm1_g, norm2_g, w_in, a_q_g, a_k_g, b_cq_g, b_ckv_g, w_uq, w_ukv, b_q_g, b_k_g,
                  c_q_g, c_k_g, w_out, router_w, router_b)
    ckb, cvb = _cache_kv(cache_b_ckv, cache_b_krope, lw, consts)
    merge_heads = lambda a: a.reshape(N_LAT_SEQ, DEPTH, PAST, 2 * HD).astype(BF16)
    cache = dict(ka=merge_heads(cache_a_k), va=merge_heads(cache_a_v), kb=ckb, vb=cvb,
                 kc=merge_heads(cache_c_k), vc=merge_heads(cache_c_v))

    states = ()
    names = ["qa", "ka", "va", "qb", "kb", "vb", "qc", "kc", "vc"]
    for l in range(DEPTH):
        outs = _projection(False, l, x_ctx, 0, mod, lw, consts, states)
        p_ctx = dict(zip(names, outs[:9]))
        states = tuple(outs[9:])
        p_lat = dict(zip(names, _projection(True, l, x_lat, x_lat_off, mod, lw, consts)))
        mix_ctx = _attention_ctx(l, a_sink, p_ctx)
        mix_lat = _attention_lat(l, a_sink, p_lat, cache)
        x_mid, h2p, gate_slab, route, counts = _post_attention(l, mix_ctx, mix_lat, x_ctx, x_lat, x_lat_off, mod, lw,
                                                               consts)
        x_ctx, x_lat = _moe(l, x_mid, h2p, gate_slab, route, counts, mod, w_gu, b_gu, w_dn, b_dn)
    y_ctx, y_lat = x_ctx, x_lat

    return (y_ctx.reshape(N_CTX_SEQ, CTX_LEN, D), y_lat.reshape(N_LAT_SEQ, LAT_LEN, D)) + states
```

```python
import functools

import jax
import jax.numpy as jnp
import numpy as np
from jax import lax
from jax.experimental import pallas as pl
from jax.experimental.pallas import tpu as pltpu
from jax.experimental.pallas import tpu_sc as plsc

F32 = jnp.float32
BF16 = jnp.bfloat16

D = 1024
DEPTH = 2
N_CTX_SEQ, CTX_LEN = 16, 256
N_LAT_SEQ, LAT_LEN = 4, 2048
PAST = 512
T_CTX = N_CTX_SEQ * CTX_LEN
T_LAT = N_LAT_SEQ * LAT_LEN
T_ALL = T_CTX + T_LAT
GRID_W = 64
HD = 64
WINDOW = 128
B_NOPE, B_ROPE, B_V = 64, 32, 128
B_QK = B_NOPE + B_ROPE
B_Q_LORA, B_KV_LORA = 384, 256
N_EXPERTS, TOP_K = 32, 4
D_FF = 1024
SWIGLU_LIMIT = 7.0
SWIGLU_ALPHA = 1.702
ROPE_THETA = 10000.0
EPS = 1e-6
NEG_INF = -1e30

TM = 256
TP = 512
TQ_LAT = 512
TQ_CTX = 256
SEQ_PER_STEP = TQ_CTX // CTX_LEN
LANE = 128
BP = 128
N_CTX_TILES = T_CTX // TM
N_LAT_TILES = T_LAT // TM
N_TILES = T_ALL // TM
LAT_TILES_PER_SEQ = LAT_LEN // TM
MOE_TM = 256
MOE_STEP = 2 * MOE_TM
FF_CHUNK = 256
MOE_STEPS = T_ALL * TOP_K // MOE_STEP + N_EXPERTS
MOE_ROWS = MOE_STEPS * MOE_STEP
DW = D // 2
VMEM_LIMIT = 56 * 1024 * 1024

C_QA, C_KA, C_VA, C_CQ, C_CKV, C_QC, C_KC, C_VC, C_KR, C_END = (
    0, 256, 384, 512, 896, 1152, 1408, 1536, 1664, 1792)


def _dot(a, b):
    return jnp.dot(a, b, preferred_element_type=F32)


def _dot_t(a, b):
    return lax.dot_general(a, b, (((1,), (1,)), ((), ())), preferred_element_type=F32)


def _rms(x, g):
    return x * lax.rsqrt(jnp.mean(x * x, axis=-1, keepdims=True) + EPS) * g


def _seg_norm(x, seg, g, n):
    ss = _dot((x * x).astype(BF16), seg)
    return x * lax.rsqrt(ss * (1.0 / n) + EPS) * g


def _rope(x, cos, sin, first, sh):
    w = x.shape[1]
    fwd = pltpu.roll(x, w - sh, 1)
    bwd = pltpu.roll(x, sh, 1)
    return x * cos + jnp.where(first > 0.5, fwd, bwd) * sin


MOD_BN = 1536


def _mod_kernel(c_ref, w_ref, b_ref, o_ref):
    c = c_ref[...]
    s = (c * jax.nn.sigmoid(c)).astype(BF16)
    o_ref[...] = _dot(s, w_ref[...].astype(BF16)) + b_ref[...]


def _modulation(cond, w_mod, b_mod):
    return pl.pallas_call(
        _mod_kernel,
        grid=(DEPTH, 6 * D // MOD_BN),
        in_specs=[
            pl.BlockSpec((8, D), lambda l, j: (0, 0)),
            pl.BlockSpec((None, D, MOD_BN), lambda l, j: (l, 0, j)),
            pl.BlockSpec((None, 1, MOD_BN), lambda l, j: (l, 0, j)),
        ],
        out_specs=pl.BlockSpec((None, 8, MOD_BN), lambda l, j: (l, 0, j)),
        out_shape=jax.ShapeDtypeStruct((DEPTH, 8, 6 * D), F32),
        compiler_params=pltpu.CompilerParams(
            dimension_semantics=("arbitrary", "arbitrary"), vmem_limit_bytes=VMEM_LIMIT),
        name="modulation",
    )(cond, w_mod, b_mod.reshape(DEPTH, 1, 6 * D))


def _proj_kernel(is_lat, n_aliased, *refs):
    (x_ref, mod_ref, n1g_ref, win_ref, seg64_ref, seg96_ref, aqg_ref, akg_ref, cqg_ref, ckg_ref,
     bcqg_ref, bckvg_ref, bqg_ref, bkg_ref, wuq_ref, wuk_ref, wuv_ref, plc_ref) = refs[:18]
    refs = refs[18 + n_aliased:]
    if is_lat:
        (cos64_ref, sin64_ref, cos96_ref, sin96_ref, f64_ref, f96_ref) = refs[:6]
        refs = refs[6:]
    (qa_ref, ka_ref, va_ref, qb_ref, kb_ref, vb_ref, qc_ref, kc_ref, vc_ref) = refs[:9]
    refs = refs[9:]
    if not is_lat:
        if n_aliased == 0:
            for r in refs:
                r[:, 1:] = jnp.zeros((SEQ_PER_STEP, DEPTH - 1) + r.shape[2:], F32)
            refs = [r.at[:, 0] for r in refs]
        (kas_ref, vas_ref, ckvs_ref, krs_ref, kcs_ref, vcs_ref) = refs

    x = x_ref[...]
    m = mod_ref[...]
    sh1, sc1 = m[:, 0:D], m[:, D:2 * D]
    h = (_rms(x, n1g_ref[...]) * (1.0 + sc1) + sh1).astype(BF16)

    def proj(a, b):
        return _dot(h, win_ref[:, a:b])

    def rope64(t):
        wd = t.shape[1]
        return _rope(t, cos64_ref[:, :wd], sin64_ref[:, :wd], f64_ref[:, :wd], 16) if is_lat else t

    def rope96(t):
        return _rope(t, cos96_ref[...], sin96_ref[...], f96_ref[...], 8) if is_lat else t

    seg64 = seg64_ref[...]
    seg64h = seg64_ref[0:2 * HD, 0:2 * HD]
    seg96 = seg96_ref[...]

    def seg_sum(t, seg):
        return _dot((t * t).astype(BF16), seg)

    def seg_finish(t, ss, g, n):
        return t * lax.rsqrt(ss * (1.0 / n) + EPS) * g

    p_cq, p_ckv, p_kr = proj(C_CQ, C_CKV), proj(C_CKV, C_QC), proj(C_KR, C_END)
    p_qa, p_ka, p_qc, p_kc = proj(C_QA, C_KA), proj(C_KA, C_VA), proj(C_QC, C_KC), proj(C_KC, C_VC)
    p_va, p_vc = proj(C_VA, C_CQ), proj(C_VC, C_KR)

    cq = _rms(p_cq, bcqg_ref[...]).astype(BF16)
    ckv = _rms(p_ckv, bckvg_ref[...])
    ckv16 = ckv.astype(BF16)
    u_q = _dot(cq, wuq_ref[...])
    u_k = _dot(ckv16, wuk_ref[...]) + _dot(p_kr.astype(BF16), plc_ref[...])
    u_v = _dot(ckv16, wuv_ref[...])
    ss_qa, ss_ka = seg_sum(p_qa, seg64), seg_sum(p_ka, seg64h)
    ss_qc, ss_kc = seg_sum(p_qc, seg64), seg_sum(p_kc, seg64h)
    ss_qb, ss_kb = seg_sum(u_q, seg96), seg_sum(u_k, seg96)

    def store_state(ref, t):
        for s in range(SEQ_PER_STEP):
            ref[s] = t[s * CTX_LEN:(s + 1) * CTX_LEN]

    def store_kv_state(ref, t):
        for s in range(SEQ_PER_STEP):
            for kv in range(2):
                ref[s, :, kv, :] = t[s * CTX_LEN:(s + 1) * CTX_LEN, kv * HD:(kv + 1) * HD]

    if not is_lat:
        store_kv_state(vas_ref, p_va)
        store_kv_state(vcs_ref, p_vc)
        store_state(ckvs_ref, ckv)
        store_state(krs_ref, p_kr[:, 0:B_ROPE])
    va_ref[...] = p_va.astype(BF16)
    vc_ref[...] = p_vc.astype(BF16)
    vb_ref[...] = u_v.astype(BF16)

    t = seg_finish(p_qa, ss_qa, aqg_ref[...], HD)
    qa_ref[...] = (rope64(t) * (HD ** -0.5 * LOG2E)).astype(BF16)
    t = seg_finish(p_ka, ss_ka, akg_ref[...], HD)
    if not is_lat:
        store_kv_state(kas_ref, t)
    ka_ref[...] = rope64(t).astype(BF16)
    t = seg_finish(p_qc, ss_qc, cqg_ref[...], HD)
    qc_ref[...] = (rope64(t) * (HD ** -0.5 * LOG2E)).astype(BF16)
    t = seg_finish(p_kc, ss_kc, ckg_ref[...], HD)
    if not is_lat:
        store_kv_state(kcs_ref, t)
    kc_ref[...] = rope64(t).astype(BF16)
    t = seg_finish(u_q, ss_qb, bqg_ref[...], B_QK)
    qb_ref[...] = (rope96(t) * (B_QK ** -0.5 * LOG2E)).astype(BF16)
    t = seg_finish(u_k, ss_kb, bkg_ref[...], B_QK)
    kb_ref[...] = rope96(t).astype(BF16)


def _full(shape):
    nd = len(shape)
    return pl.BlockSpec(shape, lambda i: (0,) * nd)


STATE_TAILS = ((2, HD), (2, HD), (B_KV_LORA,), (B_ROPE,), (2, HD), (2, HD))


def _layer_spec(a, l):
    nd = a.ndim - 1
    return pl.BlockSpec((None,) + a.shape[1:], lambda *_: (l,) + (0,) * nd)


VEC_ROWS = dict(n1g=(0, D), n2g=(1, D), aqg=(2, 4 * HD), akg=(3, 2 * HD), cqg=(4, 4 * HD), ckg=(5, 2 * HD),
                bcqg=(6, B_Q_LORA), bckvg=(7, B_KV_LORA), bqg=(8, 4 * BP), bkg=(9, 4 * BP))


def _vec_spec(l, name):
    row, width = VEC_ROWS[name]
    return pl.BlockSpec((None, None, 1, width), lambda *_: (l, row, 0, 0))


def _projection(is_lat, l, x_src, x_off, mod, lw, consts, prev_states=()):
    assert is_lat or bool(prev_states) == (l > 0)
    t_rows = T_LAT if is_lat else T_CTX
    TQ = TQ_LAT if is_lat else TQ_CTX
    n_tiles = t_rows // TQ
    if is_lat:
        mod_map = lambda i: (l, 1 + i // (LAT_LEN // TQ), 0, 0)
    else:
        mod_map = lambda i: (l, 0, 0, 0)
    vec = lw["vec"]
    gain = lambda name: (vec, _vec_spec(l, name))
    whole = lambda a: (a, _full(a.shape))
    layer = lambda a: (a, _layer_spec(a, l))
    pairs = [(x_src, pl.BlockSpec((TQ, D), lambda i: (i + x_off, 0))),
             (mod, pl.BlockSpec((None, None, 1, 6 * D), mod_map)),
             gain("n1g"), layer(lw["w_in"]), whole(consts["seg64"]), whole(consts["seg96"]),
             gain("aqg"), gain("akg"), gain("cqg"), gain("ckg"), gain("bcqg"), gain("bckvg"), gain("bqg"), gain("bkg"),
             layer(lw["w_uq"]), layer(lw["w_uk"]), layer(lw["w_uv"]), whole(consts["place"])]
    ins = [a for a, _ in pairs]
    in_specs = [s for _, s in pairs]
    n_plain = len(ins)
    ins += list(prev_states)
    in_specs += [pl.BlockSpec(memory_space=pl.ANY) for _ in prev_states]
    if is_lat:
        tabs = [consts["cos64"], consts["sin64"], consts["cos96"], consts["sin96"]]
        ins += tabs + [consts["first64"], consts["first96"]]
        in_specs += [pl.BlockSpec((TQ, a.shape[1]), lambda i: (i % (LAT_LEN // TQ), 0)) for a in tabs]
        in_specs += [_full(consts["first64"].shape), _full(consts["first96"].shape)]
    widths = [4 * HD, 2 * HD, 2 * HD, 4 * BP, 4 * BP, 4 * B_V, 4 * HD, 2 * HD, 2 * HD]
    out_shape = [jax.ShapeDtypeStruct((t_rows, w), BF16) for w in widths]
    out_specs = [pl.BlockSpec((TQ, w), lambda i: (i, 0)) for w in widths]
    if not is_lat:
        for tail in STATE_TAILS:
            zeros = (0,) * len(tail)
            out_shape.append(jax.ShapeDtypeStruct((N_CTX_SEQ, DEPTH, CTX_LEN) + tail, F32))
            if prev_states:
                out_specs.append(pl.BlockSpec((SEQ_PER_STEP, None, CTX_LEN) + tail, lambda i, z=zeros: (i, l, 0) + z))
            else:
                out_specs.append(pl.BlockSpec((SEQ_PER_STEP, DEPTH, CTX_LEN) + tail, lambda i, z=zeros: (i, 0, 0) + z))
    aliases = {n_plain + k: len(widths) + k for k in range(len(prev_states))}
    return pl.pallas_call(
        functools.partial(_proj_kernel, is_lat, len(prev_states)),
        grid=(n_tiles,),
        in_specs=in_specs,
        out_specs=out_specs,
        out_shape=out_shape,
        input_output_aliases=aliases,
        compiler_params=pltpu.CompilerParams(
            dimension_semantics=("arbitrary",), vmem_limit_bytes=VMEM_LIMIT),
        name="proj_lat" if is_lat else "proj_ctx",
    )(*ins)


def _cache_kv_kernel(ckv_ref, kr_ref, seg96_ref, bkg_ref, wuk_ref, wuv_ref, plc_ref, kb_ref, vb_ref):
    ckv16 = ckv_ref[...].astype(BF16)
    vb_ref[...] = _dot(ckv16, wuv_ref[...]).astype(BF16)
    kpre = _dot(ckv16, wuk_ref[...]) + _dot(kr_ref[...].astype(BF16), plc_ref[0:B_ROPE, :])
    kb_ref[...] = _seg_norm(kpre, seg96_ref[...], bkg_ref[...], B_QK).astype(BF16)


def _cache_kv(cache_b_ckv, cache_b_krope, lw, consts):
    def c_map(l, r):
        return (r // (PAST // TM), l, r % (PAST // TM), 0)

    def w_spec(a):
        return pl.BlockSpec((None,) + a.shape[1:], lambda l, r: (l,) + (0,) * (a.ndim - 1))

    def s_spec(a):
        return pl.BlockSpec(a.shape, lambda l, r: (0,) * a.ndim)

    rows = N_LAT_SEQ * PAST
    return pl.pallas_call(
        _cache_kv_kernel,
        grid=(DEPTH, rows // TM),
        in_specs=[pl.BlockSpec((None, None, TM, B_KV_LORA), c_map), pl.BlockSpec((None, None, TM, B_ROPE), c_map),
                  s_spec(consts["seg96"]),
                  pl.BlockSpec((None, None, 1, 4 * BP), lambda l, r: (l, VEC_ROWS["bkg"][0], 0, 0)),
                  w_spec(lw["w_uk"]), w_spec(lw["w_uv"]),
                  s_spec(consts["place"])],
        out_specs=[pl.BlockSpec((None, TM, 4 * BP), lambda l, r: (l, r, 0)),
                   pl.BlockSpec((None, TM, 4 * B_V), lambda l, r: (l, r, 0))],
        out_shape=[jax.ShapeDtypeStruct((DEPTH, rows, 4 * BP), BF16),
                   jax.ShapeDtypeStruct((DEPTH, rows, 4 * B_V), BF16)],
        compiler_params=pltpu.CompilerParams(
            dimension_semantics=("arbitrary", "arbitrary"), vmem_limit_bytes=VMEM_LIMIT),
        name="cache_kv",
    )(cache_b_ckv, cache_b_krope, consts["seg96"], lw["vec"], lw["w_uk"], lw["w_uv"], consts["place"])


def _scores(q, segs):
    scores = []
    for k, _, mask in segs:
        s = _dot_t(q, k)
        if mask is not None:
            s = jnp.where(mask, s, NEG_INF)
        scores.append(s)
    return scores


def _softmax_pv(scores, segs, sink, num_cols, den_col):
    m = None
    for s in scores:
        sm = jnp.max(s, axis=-1, keepdims=True)
        m = sm if m is None else jnp.maximum(m, sm)
    if sink is not None:
        m = jnp.maximum(m, sink)
    acc = None
    den = None
    for s, (_, v, _) in zip(scores, segs):
        e = jnp.exp2(s - m)
        if den_col is None:
            d = jnp.sum(e, axis=-1, keepdims=True)
            den = d if den is None else den + d
        o = _dot(e.astype(BF16), v)
        acc = o if acc is None else acc + o
    if den_col is not None:
        den = acc[:, den_col:den_col + 1]
        acc = acc[:, num_cols]
    if sink is not None:
        den = den + jnp.exp2(sink - m)
    return acc / den


def _run_heads(jobs, o_ref):
    nxt = _scores(jobs[0][0](), jobs[0][1]())
    for n, (_, load_segs, sink, col, num_cols, den_col) in enumerate(jobs):
        scores = nxt
        if n + 1 < len(jobs):
            nxt = _scores(jobs[n + 1][0](), jobs[n + 1][1]())
        o = _softmax_pv(scores, load_segs(values=True), sink, num_cols, den_col)
        o_ref[:, col:col + o.shape[1]] = o.astype(BF16)


OUT_B = 4 * HD
OUT_C = 4 * HD + 4 * B_V
LOG2E = 1.4426950408889634


def _head_job(q_ref, q_cols, seg_refs, kv_cols, v_cols, sink, col, rows=None, masks=None, mxu_sums=True):
    width = v_cols.stop - v_cols.start
    if not mxu_sums:
        load_cols, num_cols, den_col = v_cols, None, None
    elif width == HD:
        pair = v_cols.start // LANE * LANE
        load_cols = slice(pair, pair + LANE)
        lo = v_cols.start - pair
        num_cols, den_col = slice(lo, lo + HD), (lo + HD) % LANE
    else:
        load_cols, num_cols, den_col = v_cols, slice(0, width), width
    if sink is not None:
        sink = sink * LOG2E

    def with_ones(v):
        if not mxu_sums:
            return v
        if width != HD:
            return jnp.concatenate([v, jnp.ones_like(v)], axis=1)
        lane = lax.broadcasted_iota(jnp.int32, (1, LANE), 1)
        keep = jnp.where((lane >= num_cols.start) & (lane < num_cols.stop), 1.0, 0.0).astype(BF16)
        return v * keep + (1.0 - keep)

    def load_segs(values=False):
        segs = []
        for n, (k_ref, v_ref) in enumerate(seg_refs):
            r = rows if (rows is not None and n == 0) else slice(None)
            if values:
                segs.append((None, with_ones(v_ref[r, load_cols]), None))
            else:
                segs.append((k_ref[r, kv_cols], None, None if masks is None else masks[n]))
        return segs

    return (lambda: q_ref[:, q_cols]), load_segs, sink, col, num_cols, den_col


def _attn_ctx_kernel(l, sink_ref, qa_ref, ka_ref, va_ref, qb_ref, kb_ref, vb_ref, qc_ref, kc_ref, vc_ref, o_ref):
    jobs = []
    for h in range(4):
        cs = slice(h // 2 * HD, (h // 2 + 1) * HD)
        jobs.append(_head_job(qa_ref, slice(h * HD, (h + 1) * HD), [(ka_ref, va_ref)], cs, cs, sink_ref[l, h],
                              h * HD, mxu_sums=False))
    for h in range(4):
        ks, vs = slice(h * BP, (h + 1) * BP), slice(h * B_V, (h + 1) * B_V)
        jobs.append(_head_job(qb_ref, ks, [(kb_ref, vb_ref)], ks, vs, None, OUT_B + h * B_V, mxu_sums=False))
    for h in range(4):
        cs = slice(h // 2 * HD, (h // 2 + 1) * HD)
        jobs.append(_head_job(qc_ref, slice(h * HD, (h + 1) * HD), [(kc_ref, vc_ref)], cs, cs, None, OUT_C + h * HD,
                              mxu_sums=False))
    _run_heads(jobs, o_ref)


def _attention_ctx(l, sink, p):
    names = ["qa", "ka", "va", "qb", "kb", "vb", "qc", "kc", "vc"]
    ins = [p[n] for n in names]
    in_specs = [pl.BlockSpec(memory_space=pltpu.SMEM)]
    in_specs += [pl.BlockSpec((CTX_LEN, a.shape[1]), lambda i: (i, 0)) for a in ins]
    return pl.pallas_call(
        functools.partial(_attn_ctx_kernel, l),
        grid=(N_CTX_SEQ,),
        in_specs=in_specs,
        out_specs=pl.BlockSpec((CTX_LEN, D), lambda i: (i, 0)),
        out_shape=jax.ShapeDtypeStruct((T_CTX, D), BF16),
        compiler_params=pltpu.CompilerParams(
            dimension_semantics=("arbitrary",), vmem_limit_bytes=VMEM_LIMIT),
        name="attn_ctx",
    )(sink, *ins)


WIN_SPAN = TM + 2 * WINDOW


def _attn_lat_kernel(l, sink_ref, qa_ref, qb_ref, qc_ref, ka_ref, va_ref, kb_ref, vb_ref, kc_ref, vc_ref,
                     cka_ref, cva_ref, ckb_ref, cvb_ref, ckc_ref, cvc_ref, o_ref):
    qi = pl.program_id(1)
    ws = pl.multiple_of(jnp.clip(qi * TM - WINDOW, 0, LAT_LEN - WIN_SPAN), WINDOW)
    qpos = qi * TM + lax.broadcasted_iota(jnp.int32, (TM, WIN_SPAN), 0)
    kpos = ws + lax.broadcasted_iota(jnp.int32, (TM, WIN_SPAN), 1)
    band = jnp.abs(qpos - kpos) <= WINDOW
    jobs = []
    for h in range(4):
        cs = slice(h // 2 * HD, (h // 2 + 1) * HD)
        jobs.append(_head_job(qa_ref, slice(h * HD, (h + 1) * HD), [(ka_ref, va_ref), (cka_ref, cva_ref)], cs, cs,
                              sink_ref[l, h], h * HD, rows=pl.ds(ws, WIN_SPAN), masks=(band, None)))
    for h in range(4):
        ks, vs = slice(h * BP, (h + 1) * BP), slice(h * B_V, (h + 1) * B_V)
        jobs.append(_head_job(qb_ref, ks, [(kb_ref, vb_ref), (ckb_ref, cvb_ref)], ks, vs, None, OUT_B + h * B_V))
    for h in range(4):
        cs = slice(h // 2 * HD, (h // 2 + 1) * HD)
        jobs.append(_head_job(qc_ref, slice(h * HD, (h + 1) * HD), [(kc_ref, vc_ref), (ckc_ref, cvc_ref)], cs, cs,
                              None, OUT_C + h * HD))
    _run_heads(jobs, o_ref)


def _attention_lat(l, sink, p, cache):
    q_ins = [p["qa"], p["qb"], p["qc"]]
    kv_ins = [p[n] for n in ("ka", "va", "kb", "vb", "kc", "vc")]
    c_names = ("ka", "va", "kb", "vb", "kc", "vc")
    c_ins = [cache[n] for n in c_names]
    in_specs = [pl.BlockSpec(memory_space=pltpu.SMEM)]
    in_specs += [pl.BlockSpec((TM, a.shape[1]), lambda b, i: (b * LAT_TILES_PER_SEQ + i, 0)) for a in q_ins]
    in_specs += [pl.BlockSpec((LAT_LEN, a.shape[1]), lambda b, i: (b, 0)) for a in kv_ins]
    for n, a in zip(c_names, c_ins):
        if n in ("kb", "vb"):
            in_specs.append(pl.BlockSpec((None, PAST, a.shape[-1]), lambda b, i: (l, b, 0)))
        else:
            in_specs.append(pl.BlockSpec((None, None, PAST, a.shape[-1]), lambda b, i: (b, l, 0, 0)))
    return pl.pallas_call(
        functools.partial(_attn_lat_kernel, l),
        grid=(N_LAT_SEQ, LAT_TILES_PER_SEQ),
        in_specs=in_specs,
        out_specs=pl.BlockSpec((TM, D), lambda b, i: (b * LAT_TILES_PER_SEQ + i, 0)),
        out_shape=jax.ShapeDtypeStruct((T_LAT, D), BF16),
        compiler_params=pltpu.CompilerParams(
            dimension_semantics=("arbitrary", "arbitrary"), vmem_limit_bytes=VMEM_LIMIT),
        name="attn_lat",
    )(sink, *q_ins, *kv_ins, *c_ins)


def _pack_rows(x):
    half = x.shape[1] // 2
    r = x.astype(BF16).astype(F32)
    hi = lax.bitcast_convert_type(r[:, :half], jnp.int32)
    lo = lax.bitcast_convert_type(r[:, half:], jnp.int32)
    return jnp.bitwise_or(hi, lax.shift_right_logical(lo, 16))


def _unpack_rows(p):
    a = lax.bitcast_convert_type(jnp.bitwise_and(p, -65536), F32)
    b = lax.bitcast_convert_type(lax.shift_left(p, 16), F32)
    return jnp.concatenate([a, b], axis=1).astype(BF16)


ROUTE_ROWS = 16


def _post_kernel(mc_ref, ml_ref, xc_ref, xl_ref, mod_ref, n2g_ref, wout_ref, rwt_ref, rb_ref, tri_ref,
                 xo_ref, h2_ref, gate_ref, route_ref, cnt_ref, run_ref):
    i = pl.program_id(0)

    @pl.when(i == 0)
    def _():
        run_ref[...] = jnp.zeros_like(run_ref)
        cnt_ref[...] = jnp.zeros_like(cnt_ref)

    is_ctx = i < T_CTX // TP
    mix = jnp.where(is_ctx, mc_ref[...], ml_ref[...])
    m = mod_ref[...]
    g1, sh2, sc2 = m[:, 2 * D:3 * D], m[:, 3 * D:4 * D], m[:, 4 * D:5 * D]
    x = jnp.where(is_ctx, xc_ref[...], xl_ref[...]) + g1 * _dot(mix, wout_ref[...])
    xo_ref[...] = x
    h2 = _rms(x, n2g_ref[...]) * (1.0 + sc2) + sh2
    hi = h2.astype(BF16)
    h2_ref[...] = _pack_rows(h2)
    lo = (h2 - hi.astype(F32)).astype(BF16)
    a = _dot_t(rwt_ref[...], hi)
    b = _dot_t(rwt_ref[0:N_EXPERTS, :], lo)
    logits = a[0:N_EXPERTS] + a[N_EXPERTS:] + b + rb_ref[...]
    eidx = lax.broadcasted_iota(jnp.int32, logits.shape, 0).astype(F32)
    work = logits
    vals, idxs, hots = [], [], []
    for _ in range(TOP_K):
        v = jnp.max(work, axis=0, keepdims=True)
        idx = jnp.min(jnp.where(work == v, eidx, float(N_EXPERTS)), axis=0, keepdims=True)
        hot = eidx == idx
        vals.append(v)
        idxs.append(idx)
        hots.append(hot)
        work = jnp.where(hot, -jnp.inf, work)
    es = [jnp.exp(v - vals[0]) for v in vals]
    den = es[0] + es[1] + es[2] + es[3]
    gates = [e / den for e in es]
    sel = jnp.where(hots[0] | hots[1] | hots[2] | hots[3], 1.0, 0.0)
    run = run_ref[:, 0:1]
    before = _dot(sel.astype(BF16), tri_ref[...]) + run
    run_new = jnp.broadcast_to(run + jnp.sum(sel, axis=1, keepdims=True), run_ref.shape)
    run_ref[...] = run_new
    cnt_ref[...] = run_new
    ranks = [jnp.sum(jnp.where(hots[k], before, 0.0), axis=0, keepdims=True) for k in range(TOP_K)]
    pad = [jnp.zeros((ROUTE_ROWS - 3 * TOP_K, TP), F32)]
    route_ref[...] = jnp.concatenate(gates + idxs + ranks + pad, axis=0)
    gate_ref[...] = jnp.concatenate(gates + [jnp.zeros((LANE - TOP_K, TP), F32)], axis=0).T


def _mod_row(i, tm=TM):
    n_ctx = T_CTX // tm
    return jnp.where(i < n_ctx, 0, 1 + (i - n_ctx) // (LAT_LEN // tm))


def _ctx_tile(i):
    return jnp.minimum(i, T_CTX // TP - 1)


def _lat_tile(i):
    return jnp.maximum(i - T_CTX // TP, 0)


def _post_attention(l, mix_ctx, mix_lat, x_ctx, x_lat, x_lat_off, mod, lw, consts):
    ins = [mix_ctx, mix_lat, x_ctx, x_lat, mod, lw["vec"], lw["w_out"], lw["rwt"], lw["rb"], consts["tri"]]
    in_specs = [
        pl.BlockSpec((TP, D), lambda i: (_ctx_tile(i), 0)),
        pl.BlockSpec((TP, D), lambda i: (_lat_tile(i), 0)),
        pl.BlockSpec((TP, D), lambda i: (_ctx_tile(i), 0)),
        pl.BlockSpec((TP, D), lambda i: (_lat_tile(i) + x_lat_off, 0)),
        pl.BlockSpec((None, None, 1, 6 * D), lambda i: (l, _mod_row(i, TP), 0, 0)),
        _vec_spec(l, "n2g"),
    ] + [_layer_spec(a, l) for a in ins[6:9]] + [_full(consts["tri"].shape)]
    return pl.pallas_call(
        _post_kernel,
        grid=(T_ALL // TP,),
        in_specs=in_specs,
        out_specs=[pl.BlockSpec((TP, D), lambda i: (i, 0)), pl.BlockSpec((TP, DW), lambda i: (i, 0)),
                   pl.BlockSpec((TP, LANE), lambda i: (i, 0)), pl.BlockSpec((ROUTE_ROWS, TP), lambda i: (0, i)),
                   pl.BlockSpec((N_EXPERTS, LANE), lambda i: (0, 0))],
        out_shape=[jax.ShapeDtypeStruct((T_ALL, D), F32), jax.ShapeDtypeStruct((T_ALL, DW), jnp.int32),
                   jax.ShapeDtypeStruct((T_ALL, LANE), F32), jax.ShapeDtypeStruct((ROUTE_ROWS, T_ALL), F32),
                   jax.ShapeDtypeStruct((N_EXPERTS, LANE), F32)],
        scratch_shapes=[pltpu.VMEM((N_EXPERTS, LANE), F32)],
        compiler_params=pltpu.CompilerParams(
            dimension_semantics=("arbitrary",), vmem_limit_bytes=VMEM_LIMIT),
        name="post_attn",
    )(*ins)


def _expert_kernel(l, te_ref, nu_ref, nxt_ref, nv_ref, x_ref, wgu_hbm, bgu_ref, wdn_hbm, bdn_ref, y_ref,
                   wgu32, wdn32, wgu16, wdn16, sem):
    i = pl.program_id(0)
    prev = te_ref[jnp.maximum(i - 1, 0)]
    new_expert = jnp.logical_or(i == 0, te_ref[i] != prev)

    def weight_copies(e):
        return (pltpu.make_async_copy(wgu_hbm.at[l, e], wgu32, sem.at[0]),
                pltpu.make_async_copy(wdn_hbm.at[l, e], wdn32, sem.at[1]))

    @pl.when(i == 0)
    def _():
        for cp in weight_copies(te_ref[0]):
            cp.start()

    @pl.when(new_expert)
    def _():
        cps = weight_copies(te_ref[i])
        cps[0].wait()
        wgu16[...] = wgu32[...].astype(BF16)
        cps[1].wait()
        wdn16[...] = wdn32[...].astype(BF16)

        @pl.when(nxt_ref[i] >= 0)
        def _():
            for cp in weight_copies(nxt_ref[i]):
                cp.start()

    def compute(rows):
        x = _unpack_rows(x_ref[0:rows, :])

        def gate_up(a):
            b = a + FF_CHUNK
            g = _dot(x, wgu16[:, a:b]) + bgu_ref[:, a:b]
            up = _dot(x, wgu16[:, D_FF + a:D_FF + b]) + bgu_ref[:, D_FF + a:D_FF + b]
            return g, up

        def activate(g, up):
            g = jnp.minimum(g, SWIGLU_LIMIT)
            up = jnp.clip(up, -SWIGLU_LIMIT, SWIGLU_LIMIT)
            return ((up + 1.0) * (g * jax.nn.sigmoid(SWIGLU_ALPHA * g))).astype(BF16)

        starts = list(range(0, D_FF, FF_CHUNK))
        acc = None
        pre = gate_up(starts[0])
        for n, a in enumerate(starts):
            hid = activate(*pre)
            if n + 1 < len(starts):
                pre = gate_up(starts[n + 1])
            o = _dot(hid, wdn16[a:a + FF_CHUNK, :])
            acc = o if acc is None else acc + o
        y_ref[0:rows, :] = _pack_rows(acc + bdn_ref[...])

    used = i < nu_ref[0]
    for n_valid in (1, 2):
        @pl.when(jnp.logical_and(used, nv_ref[i] == n_valid))
        def _():
            compute(n_valid * MOE_TM)


def _experts(l, tile_expert, n_used, next_expert, n_valid, x_sorted, w_gu, b_gu, w_dn, b_dn):
    def row_map(i, te, nu, nxt, nv):
        return (jnp.minimum(i, nu[0] - 1), 0)

    def b_map(i, te, nu, nxt, nv):
        return (l, te[i], 0, 0)

    grid_spec = pltpu.PrefetchScalarGridSpec(
        num_scalar_prefetch=4,
        grid=(MOE_STEPS,),
        in_specs=[
            pl.BlockSpec((MOE_STEP, DW), row_map),
            pl.BlockSpec(memory_space=pl.ANY),
            pl.BlockSpec((None, None, 1, 2 * D_FF), b_map),
            pl.BlockSpec(memory_space=pl.ANY),
            pl.BlockSpec((None, None, 1, D), b_map),
        ],
        out_specs=pl.BlockSpec((MOE_STEP, DW), row_map),
        scratch_shapes=[pltpu.VMEM((D, 2 * D_FF), F32), pltpu.VMEM((D_FF, D), F32),
                        pltpu.VMEM((D, 2 * D_FF), BF16), pltpu.VMEM((D_FF, D), BF16),
                        pltpu.SemaphoreType.DMA((2,))],
    )
    return pl.pallas_call(
        functools.partial(_expert_kernel, l),
        grid_spec=grid_spec,
        out_shape=jax.ShapeDtypeStruct((MOE_ROWS, DW), jnp.int32),
        compiler_params=pltpu.CompilerParams(
            dimension_semantics=("arbitrary",), vmem_limit_bytes=VMEM_LIMIT),
        name="experts",
    )(tile_expert, n_used, next_expert, n_valid, x_sorted, w_gu, b_gu.reshape(DEPTH, N_EXPERTS, 1, 2 * D_FF), w_dn,
      b_dn.reshape(DEPTH, N_EXPERTS, 1, D))


def _combine_kernel(x_ref, y_ref, route_ref, mod_ref, o_ref):
    g2 = mod_ref[...][:, 5 * D:6 * D]
    r = route_ref[...]
    acc = r[:, 0:1] * _unpack_rows(y_ref[0]).astype(F32)
    for k in range(1, TOP_K):
        acc = acc + r[:, k:k + 1] * _unpack_rows(y_ref[k]).astype(F32)
    o_ref[...] = x_ref[...] + g2 * acc


def _combine(l, first_tile, n_tiles, x_all, y_tok, gates, mod):
    return pl.pallas_call(
        _combine_kernel,
        grid=(n_tiles,),
        in_specs=[
            pl.BlockSpec((TM, D), lambda i: (i + first_tile, 0)),
            pl.BlockSpec((TOP_K, TM, DW), lambda i: (0, i, 0)),
            pl.BlockSpec((TM, LANE), lambda i: (i + first_tile, 0)),
            pl.BlockSpec((None, None, 1, 6 * D), lambda i: (l, _mod_row(i + first_tile), 0, 0)),
        ],
        out_specs=pl.BlockSpec((TM, D), lambda i: (i, 0)),
        out_shape=jax.ShapeDtypeStruct((n_tiles * TM, D), F32),
        compiler_params=pltpu.CompilerParams(
            dimension_semantics=("arbitrary",), vmem_limit_bytes=VMEM_LIMIT),
        name="combine",
    )(x_all, y_tok, gates, mod)


SC_CORES, SC_SUBCORES = 2, 16
SC_WORKERS = SC_CORES * SC_SUBCORES
SC_CHUNK = 128
SC_CHUNKS_PER_WORKER = T_ALL // SC_CHUNK // SC_WORKERS


def _sc_mesh():
    return plsc.VectorSubcoreMesh(core_axis_name="c", subcore_axis_name="s")


def _sc_scratch():
    return [pltpu.VMEM((TOP_K, SC_CHUNK), jnp.int32), pltpu.VMEM((SC_CHUNK, DW), jnp.int32),
            pltpu.SemaphoreType.DMA]


def _dispatch_rows(h2p, dest):
    @functools.partial(pl.kernel, mesh=_sc_mesh(), out_type=jax.ShapeDtypeStruct((MOE_ROWS, DW), jnp.int32),
                       scratch_types=_sc_scratch(), name="dispatch_rows")
    def run(h_hbm, d_hbm, o_hbm, idx_v, rows_v, sem):
        wid = lax.axis_index("s") * SC_CORES + lax.axis_index("c")
        for j in range(SC_CHUNKS_PER_WORKER):
            c = wid * SC_CHUNKS_PER_WORKER + j
            pltpu.sync_copy(d_hbm.at[c], idx_v)
            pltpu.sync_copy(h_hbm.at[pl.ds(c * SC_CHUNK, SC_CHUNK)], rows_v)
            copies = [pltpu.async_copy(rows_v, o_hbm.at[idx_v.at[k]], sem) for k in range(TOP_K)]
            for cp in copies:
                cp.wait()

    return run(h2p, dest)


def _gather_rows(y, dest, first_chunk, n_chunks, *after):
    per_worker = n_chunks // SC_WORKERS
    assert per_worker * SC_WORKERS == n_chunks

    @functools.partial(pl.kernel, mesh=_sc_mesh(),
                       out_type=jax.ShapeDtypeStruct((TOP_K, n_chunks * SC_CHUNK, DW), jnp.int32),
                       scratch_types=_sc_scratch(), name="gather_rows")
    def run(y_hbm, d_hbm, *rest):
        o_hbm, idx_v, rows_v, sem = rest[len(after):]
        wid = lax.axis_index("s") * SC_CORES + lax.axis_index("c")
        for j in range(per_worker):
            c = wid * per_worker + j
            pltpu.sync_copy(d_hbm.at[first_chunk + c], idx_v)
            for k in range(TOP_K):
                pltpu.async_copy(y_hbm.at[idx_v.at[k]], rows_v, sem).wait()
                pltpu.sync_copy(rows_v, o_hbm.at[k, pl.ds(c * SC_CHUNK, SC_CHUNK)])

    return run(y, dest, *after)


def _constants():
    lane64 = np.arange(4 * HD)
    seg64 = (lane64[:, None] // HD == lane64[None, :] // HD).astype(np.float32)
    lane96 = np.arange(4 * BP)
    real = lane96 % BP < B_QK
    seg96 = ((lane96[:, None] // BP == lane96[None, :] // BP) & real[:, None] & real[None, :]).astype(np.float32)
    place = np.zeros((LANE, 4 * BP), np.float32)
    for hh in range(4):
        place[np.arange(B_ROPE), hh * BP + B_NOPE + np.arange(B_ROPE)] = 1.0

    def angles(rot_dim):
        pos = np.arange(LAT_LEN)
        rows = (pos // GRID_W).astype(np.float32)
        cols = (pos % GRID_W).astype(np.float32)
        axis_dim = rot_dim // 2
        inv = np.power(np.float32(ROPE_THETA), -(np.arange(0, axis_dim, 2, dtype=np.float32) / np.float32(axis_dim)))
        ang = np.concatenate([rows[:, None] * inv, cols[:, None] * inv], axis=-1).astype(np.float32)
        return np.cos(ang), np.sin(ang)

    def head_tables(rot_dim):
        cos, sin = angles(rot_dim)
        q = rot_dim // 4
        cr, cc, sr, sc = cos[:, :q], cos[:, q:], sin[:, :q], sin[:, q:]
        return (np.concatenate([cr, cr, cc, cc], axis=-1), np.concatenate([-sr, sr, -sc, sc], axis=-1))

    c64, s64 = head_tables(HD)
    cos64 = np.tile(c64, (1, 4))
    sin64 = np.tile(s64, (1, 4))
    c32, s32 = head_tables(B_ROPE)
    ones = np.ones((LAT_LEN, B_NOPE), np.float32)
    zeros = np.zeros((LAT_LEN, B_NOPE), np.float32)
    padz = np.zeros((LAT_LEN, BP - B_QK), np.float32)
    cos96 = np.tile(np.concatenate([ones, c32, padz], axis=-1), (1, 4))
    sin96 = np.tile(np.concatenate([zeros, s32, padz], axis=-1), (1, 4))
    first64 = ((lane64 % 32) < 16).astype(np.float32)[None, :]
    first96 = (((lane96 % BP) % 16) < 8).astype(np.float32)[None, :]
    tri = (np.arange(TP)[:, None] < np.arange(TP)[None, :]).astype(np.float32)
    f32 = lambda a: jnp.asarray(a, F32)
    return dict(seg64=jnp.asarray(seg64, BF16), seg96=jnp.asarray(seg96, BF16), place=jnp.asarray(place, BF16),
                tri=jnp.asarray(tri, BF16),
                cos64=f32(cos64), sin64=f32(sin64), cos96=f32(cos96), sin96=f32(sin96),
                first64=f32(first64), first96=f32(first96))


def _pad_heads(w, per_head, width):
    lead = w.shape[:-1]
    w = w.reshape(lead + (4, per_head))
    return jnp.pad(w, ((0, 0),) * (len(lead) + 1) + ((0, width - per_head),)).reshape(lead + (4 * width,))


def _weights(norm1_g, norm2_g, w_in, a_q_g, a_k_g, b_cq_g, b_ckv_g, w_uq, w_ukv, b_q_g, b_k_g,
             c_q_g, c_k_g, w_out, router_w, router_b):
    o = np.cumsum((0, 256, 128, 128, 384, 256, 32, 256, 128, 128))
    seg = lambda k: w_in[:, :, o[k]:o[k + 1]]
    w_in_r = jnp.concatenate([seg(0), seg(1), seg(2), seg(3), seg(4), seg(6), seg(7), seg(8), seg(5),
                              jnp.zeros((DEPTH, D, LANE - B_ROPE), F32)], axis=-1).astype(BF16)
    ukv = w_ukv.reshape(DEPTH, B_KV_LORA, 4, B_NOPE + B_V)
    w_uk = _pad_heads(ukv[..., :B_NOPE].reshape(DEPTH, B_KV_LORA, 4 * B_NOPE), B_NOPE, BP)
    w_uv = ukv[..., B_NOPE:].reshape(DEPTH, B_KV_LORA, 4 * B_V)
    rw_hi = router_w.astype(BF16)
    rw_lo = (router_w - rw_hi.astype(F32)).astype(BF16)
    tile = lambda v, n: jnp.tile(v, (1, n))
    rows = dict(n1g=norm1_g, n2g=norm2_g, aqg=tile(a_q_g, 4), akg=tile(a_k_g, 2), cqg=tile(c_q_g, 4),
                ckg=tile(c_k_g, 2), bcqg=b_cq_g, bckvg=b_ckv_g,
                bqg=_pad_heads(tile(b_q_g, 4), B_QK, BP), bkg=_pad_heads(tile(b_k_g, 4), B_QK, BP))
    order = sorted(VEC_ROWS, key=lambda n: VEC_ROWS[n][0])
    assert all(rows[n].shape == (DEPTH, VEC_ROWS[n][1]) for n in order)
    vec = jnp.stack([jnp.pad(rows[n], ((0, 0), (0, D - rows[n].shape[1]))) for n in order], axis=1)
    return dict(
        vec=vec[:, :, None, :], w_in=w_in_r,
        w_uq=_pad_heads(w_uq, B_QK, BP).astype(BF16), w_uk=w_uk.astype(BF16), w_uv=w_uv.astype(BF16),
        w_out=w_out.astype(BF16),
        rwt=jnp.concatenate([jnp.swapaxes(rw_hi, 1, 2), jnp.swapaxes(rw_lo, 1, 2)], axis=1),
        rb=router_b[:, :, None])


def _moe(l, x_all, h2p, gate_slab, route, counts, mod, w_gu, b_gu, w_dn, b_dn):
    counts = counts[:, 0].astype(jnp.int32)
    padded = (counts + MOE_STEP - 1) // MOE_STEP * MOE_STEP
    pend = jnp.cumsum(padded)
    pstart = pend - padded
    n_used = (pend[-1] // MOE_STEP).astype(jnp.int32)
    steps = jnp.minimum(jnp.arange(MOE_STEPS, dtype=jnp.int32), n_used - 1)
    tile_expert = jnp.sum((pend[None, :] <= steps[:, None] * MOE_STEP).astype(jnp.int32), axis=1)
    tile_expert = jnp.minimum(tile_expert, N_EXPERTS - 1)
    of_expert = lambda table: jnp.sum(
        jnp.where(tile_expert[:, None] == jnp.arange(N_EXPERTS)[None, :], table[None, :], 0), axis=1)
    group_end_step = of_expert(pend) // MOE_STEP
    after = jnp.sum((pend[None, :] <= group_end_step[:, None] * MOE_STEP).astype(jnp.int32), axis=1)
    next_expert = jnp.where(group_end_step < n_used, jnp.minimum(after, N_EXPERTS - 1), -1).astype(jnp.int32)
    rows_left = of_expert(pstart + counts) - steps * MOE_STEP
    n_valid = jnp.clip((rows_left + MOE_TM - 1) // MOE_TM, 1, MOE_STEP // MOE_TM).astype(jnp.int32)
    e = route[TOP_K:2 * TOP_K].astype(jnp.int32)
    rank = route[2 * TOP_K:3 * TOP_K].astype(jnp.int32)
    start = jnp.sum(jnp.where(e[:, :, None] == jnp.arange(N_EXPERTS)[None, None, :], pstart[None, None, :], 0), axis=-1)
    dest = (start + rank).reshape(TOP_K, T_ALL // SC_CHUNK, SC_CHUNK).transpose(1, 0, 2)
    x_sorted = _dispatch_rows(h2p, dest)
    y = _experts(l, tile_expert, n_used.reshape(1), next_expert, n_valid, x_sorted, w_gu, b_gu, w_dn, b_dn)
    ctx_chunks, lat_chunks = T_CTX // SC_CHUNK, T_LAT // SC_CHUNK
    y_ctx = _gather_rows(y, dest, 0, ctx_chunks)
    out_ctx = _combine(l, 0, N_CTX_TILES, x_all, y_ctx, gate_slab, mod)
    y_lat = _gather_rows(y, dest, ctx_chunks, lat_chunks, out_ctx)
    return out_ctx, _combine(l, N_CTX_TILES, N_LAT_TILES, x_all, y_lat, gate_slab, mod)


def kernel(x_prompt, x_sample, cache_a_k, cache_a_v, cache_b_ckv, cache_b_krope, cache_c_k, cache_c_v, c, c_ctx,
           norm1_g, norm2_g, w_mod, b_mod, w_in, a_q_g, a_k_g, a_sink, b_cq_g, b_ckv_g, w_uq, w_ukv, b_q_g, b_k_g,
           c_q_g, c_k_g, w_out, router_w, router_b, w_gu, b_gu, w_dn, b_dn):
    consts = _constants()
    cond = jnp.concatenate([c_ctx[None, :], c, jnp.zeros((3, D), F32)], axis=0)
    mod = _modulation(cond, w_mod, b_mod).reshape(DEPTH, 8, 1, 6 * D)
    x_ctx, x_lat, x_lat_off = x_prompt.reshape(T_CTX, D), x_sample.reshape(T_LAT, D), 0

    lw = _weights(norm1_g, norm2_g, w_in, a_q_g, a_k_g, b_cq_g, b_ckv_g, w_uq, w_ukv, b_q_g, b_k_g,
                  c_q_g, c_k_g, w_out, router_w, router_b)
    ckb, cvb = _cache_kv(cache_b_ckv, cache_b_krope, lw, consts)
    merge_heads = lambda a: a.reshape(N_LAT_SEQ, DEPTH, PAST, 2 * HD).astype(BF16)
    cache = dict(ka=merge_heads(cache_a_k), va=merge_heads(cache_a_v), kb=ckb, vb=cvb,
                 kc=merge_heads(cache_c_k), vc=merge_heads(cache_c_v))

    states = ()
    names = ["qa", "ka", "va", "qb", "kb", "vb", "qc", "kc", "vc"]
    for l in range(DEPTH):
        outs = _projection(False, l, x_ctx, 0, mod, lw, consts, states)
        p_ctx = dict(zip(names, outs[:9]))
        states = tuple(outs[9:])
        p_lat = dict(zip(names, _projection(True, l, x_lat, x_lat_off, mod, lw, consts)))
        mix_ctx = _attention_ctx(l, a_sink, p_ctx)
        mix_lat = _attention_lat(l, a_sink, p_lat, cache)
        x_mid, h2p, gate_slab, route, counts = _post_attention(l, mix_ctx, mix_lat, x_ctx, x_lat, x_lat_off, mod, lw,
                                                               consts)
        x_ctx, x_lat = _moe(l, x_mid, h2p, gate_slab, route, counts, mod, w_gu, b_gu, w_dn, b_dn)
    y_ctx, y_lat = x_ctx, x_lat

    return (y_ctx.reshape(N_CTX_SEQ, CTX_LEN, D), y_lat.reshape(N_LAT_SEQ, LAT_LEN, D)) + states
```

```python
import functools

import jax
import jax.numpy as jnp
import numpy as np
from jax import lax
from jax.experimental import pallas as pl
from jax.experimental.pallas import tpu as pltpu
from jax.experimental.pallas import tpu_sc as plsc

F32 = jnp.float32
BF16 = jnp.bfloat16

D = 1024
DEPTH = 2
N_CTX_SEQ, CTX_LEN = 16, 256
N_LAT_SEQ, LAT_LEN = 4, 2048
PAST = 512
T_CTX = N_CTX_SEQ * CTX_LEN
T_LAT = N_LAT_SEQ * LAT_LEN
T_ALL = T_CTX + T_LAT
GRID_W = 64
HD = 64
WINDOW = 128
B_NOPE, B_ROPE, B_V = 64, 32, 128
B_QK = B_NOPE + B_ROPE
B_Q_LORA, B_KV_LORA = 384, 256
N_EXPERTS, TOP_K = 32, 4
D_FF = 1024
SWIGLU_LIMIT = 7.0
SWIGLU_ALPHA = 1.702
ROPE_THETA = 10000.0
EPS = 1e-6
NEG_INF = -1e30

TM = 256
TP = 512
TQ_LAT = 512
TQ_CTX = 256
SEQ_PER_STEP = TQ_CTX // CTX_LEN
LANE = 128
BP = 128
N_CTX_TILES = T_CTX // TM
N_LAT_TILES = T_LAT // TM
N_TILES = T_ALL // TM
LAT_TILES_PER_SEQ = LAT_LEN // TM
MOE_TM = 256
MOE_STEP = 2 * MOE_TM
FF_CHUNK = 256
MOE_STEPS = T_ALL * TOP_K // MOE_STEP + N_EXPERTS
MOE_ROWS = MOE_STEPS * MOE_STEP
DW = D // 2
VMEM_LIMIT = 56 * 1024 * 1024

C_QA, C_KA, C_VA, C_CQ, C_CKV, C_QC, C_KC, C_VC, C_KR, C_END = (
    0, 256, 384, 512, 896, 1152, 1408, 1536, 1664, 1792)


def _dot(a, b):
    return jnp.dot(a, b, preferred_element_type=F32)


def _dot_t(a, b):
    return lax.dot_general(a, b, (((1,), (1,)), ((), ())), preferred_element_type=F32)


def _rms(x, g):
    return x * lax.rsqrt(jnp.mean(x * x, axis=-1, keepdims=True) + EPS) * g


def _seg_norm(x, seg, g, n):
    ss = _dot((x * x).astype(BF16), seg)
    return x * lax.rsqrt(ss * (1.0 / n) + EPS) * g


def _rope(x, cos, sin, first, sh):
    w = x.shape[1]
    fwd = pltpu.roll(x, w - sh, 1)
    bwd = pltpu.roll(x, sh, 1)
    return x * cos + jnp.where(first > 0.5, fwd, bwd) * sin


MOD_BN = 1536


def _mod_kernel(c_ref, w_ref, b_ref, o_ref):
    c = c_ref[...]
    s = (c * jax.nn.sigmoid(c)).astype(BF16)
    o_ref[...] = _dot(s, w_ref[...].astype(BF16)) + b_ref[...]


def _modulation(cond, w_mod, b_mod):
    return pl.pallas_call(
        _mod_kernel,
        grid=(DEPTH, 6 * D // MOD_BN),
        in_specs=[
            pl.BlockSpec((8, D), lambda l, j: (0, 0)),
            pl.BlockSpec((None, D, MOD_BN), lambda l, j: (l, 0, j)),
            pl.BlockSpec((None, 1, MOD_BN), lambda l, j: (l, 0, j)),
        ],
        out_specs=pl.BlockSpec((None, 8, MOD_BN), lambda l, j: (l, 0, j)),
        out_shape=jax.ShapeDtypeStruct((DEPTH, 8, 6 * D), F32),
        compiler_params=pltpu.CompilerParams(
            dimension_semantics=("arbitrary", "arbitrary"), vmem_limit_bytes=VMEM_LIMIT),
        name="modulation",
    )(cond, w_mod, b_mod.reshape(DEPTH, 1, 6 * D))


def _proj_kernel(is_lat, n_aliased, *refs):
    (x_ref, mod_ref, n1g_ref, win_ref, seg64_ref, seg96_ref, aqg_ref, akg_ref, cqg_ref, ckg_ref,
     bcqg_ref, bckvg_ref, bqg_ref, bkg_ref, wuq_ref, wuk_ref, wuv_ref, plc_ref) = refs[:18]
    refs = refs[18 + n_aliased:]
    if is_lat:
        (cos64_ref, sin64_ref, cos96_ref, sin96_ref, f64_ref, f96_ref) = refs[:6]
        refs = refs[6:]
    (qa_ref, ka_ref, va_ref, qb_ref, kb_ref, vb_ref, qc_ref, kc_ref, vc_ref) = refs[:9]
    refs = refs[9:]
    if not is_lat:
        if n_aliased == 0:
            for r in refs:
                r[:, 1:] = jnp.zeros((SEQ_PER_STEP, DEPTH - 1) + r.shape[2:], F32)
            refs = [r.at[:, 0] for r in refs]
        (kas_ref, vas_ref, ckvs_ref, krs_ref, kcs_ref, vcs_ref) = refs

    x = x_ref[...]
    m = mod_ref[...]
    sh1, sc1 = m[:, 0:D], m[:, D:2 * D]
    h = (_rms(x, n1g_ref[...]) * (1.0 + sc1) + sh1).astype(BF16)

    def proj(a, b):
        return _dot(h, win_ref[:, a:b])

    def rope64(t):
        wd = t.shape[1]
        return _rope(t, cos64_ref[:, :wd], sin64_ref[:, :wd], f64_ref[:, :wd], 16) if is_lat else t

    def rope96(t):
        return _rope(t, cos96_ref[...], sin96_ref[...], f96_ref[...], 8) if is_lat else t

    seg64 = seg64_ref[...]
    seg64h = seg64_ref[0:2 * HD, 0:2 * HD]
    seg96 = seg96_ref[...]

    def seg_sum(t, seg):
        return _dot((t * t).astype(BF16), seg)

    def seg_finish(t, ss, g, n):
        return t * lax.rsqrt(ss * (1.0 / n) + EPS) * g

    p_cq, p_ckv, p_kr = proj(C_CQ, C_CKV), proj(C_CKV, C_QC), proj(C_KR, C_END)
    p_qa, p_ka, p_qc, p_kc = proj(C_QA, C_KA), proj(C_KA, C_VA), proj(C_QC, C_KC), proj(C_KC, C_VC)
    p_va, p_vc = proj(C_VA, C_CQ), proj(C_VC, C_KR)

    cq = _rms(p_cq, bcqg_ref[...]).astype(BF16)
    ckv = _rms(p_ckv, bckvg_ref[...])
    ckv16 = ckv.astype(BF16)
    u_q = _dot(cq, wuq_ref[...])
    u_k = _dot(ckv16, wuk_ref[...]) + _dot(p_kr.astype(BF16), plc_ref[...])
    u_v = _dot(ckv16, wuv_ref[...])
    ss_qa, ss_ka = seg_sum(p_qa, seg64), seg_sum(p_ka, seg64h)
    ss_qc, ss_kc = seg_sum(p_qc, seg64), seg_sum(p_kc, seg64h)
    ss_qb, ss_kb = seg_sum(u_q, seg96), seg_sum(u_k, seg96)

    def store_state(ref, t):
        for s in range(SEQ_PER_STEP):
            ref[s] = t[s * CTX_LEN:(s + 1) * CTX_LEN]

    def store_kv_state(ref, t):
        for s in range(SEQ_PER_STEP):
            for kv in range(2):
                ref[s, :, kv, :] = t[s * CTX_LEN:(s + 1) * CTX_LEN, kv * HD:(kv + 1) * HD]

    if not is_lat:
        store_kv_state(vas_ref, p_va)
        store_kv_state(vcs_ref, p_vc)
        store_state(ckvs_ref, ckv)
        store_state(krs_ref, p_kr[:, 0:B_ROPE])
    va_ref[...] = p_va.astype(BF16)
    vc_ref[...] = p_vc.astype(BF16)
    vb_ref[...] = u_v.astype(BF16)

    t = seg_finish(p_qa, ss_qa, aqg_ref[...], HD)
    qa_ref[...] = (rope64(t) * (HD ** -0.5 * LOG2E)).astype(BF16)
    t = seg_finish(p_ka, ss_ka, akg_ref[...], HD)
    if not is_lat:
        store_kv_state(kas_ref, t)
    ka_ref[...] = rope64(t).astype(BF16)
    t = seg_finish(p_qc, ss_qc, cqg_ref[...], HD)
    qc_ref[...] = (rope64(t) * (HD ** -0.5 * LOG2E)).astype(BF16)
    t = seg_finish(p_kc, ss_kc, ckg_ref[...], HD)
    if not is_lat:
        store_kv_state(kcs_ref, t)
    kc_ref[...] = rope64(t).astype(BF16)
    t = seg_finish(u_q, ss_qb, bqg_ref[...], B_QK)
    qb_ref[...] = (rope96(t) * (B_QK ** -0.5 * LOG2E)).astype(BF16)
    t = seg_finish(u_k, ss_kb, bkg_ref[...], B_QK)
    kb_ref[...] = rope96(t).astype(BF16)


def _full(shape):
    nd = len(shape)
    return pl.BlockSpec(shape, lambda i: (0,) * nd)


STATE_TAILS = ((2, HD), (2, HD), (B_KV_LORA,), (B_ROPE,), (2, HD), (2, HD))


def _layer_spec(a, l):
    nd = a.ndim - 1
    return pl.BlockSpec((None,) + a.shape[1:], lambda *_: (l,) + (0,) * nd)


VEC_ROWS = dict(n1g=(0, D), n2g=(1, D), aqg=(2, 4 * HD), akg=(3, 2 * HD), cqg=(4, 4 * HD), ckg=(5, 2 * HD),
                bcqg=(6, B_Q_LORA), bckvg=(7, B_KV_LORA), bqg=(8, 4 * BP), bkg=(9, 4 * BP))


def _vec_spec(l, name):
    row, width = VEC_ROWS[name]
    return pl.BlockSpec((None, None, 1, width), lambda *_: (l, row, 0, 0))


def _projection(is_lat, l, x_src, x_off, mod, lw, consts, prev_states=()):
    assert is_lat or bool(prev_states) == (l > 0)
    t_rows = T_LAT if is_lat else T_CTX
    TQ = TQ_LAT if is_lat else TQ_CTX
    n_tiles = t_rows // TQ
    if is_lat:
        mod_map = lambda i: (l, 1 + i // (LAT_LEN // TQ), 0, 0)
    else:
        mod_map = lambda i: (l, 0, 0, 0)
    vec = lw["vec"]
    gain = lambda name: (vec, _vec_spec(l, name))
    whole = lambda a: (a, _full(a.shape))
    layer = lambda a: (a, _layer_spec(a, l))
    pairs = [(x_src, pl.BlockSpec((TQ, D), lambda i: (i + x_off, 0))),
             (mod, pl.BlockSpec((None, None, 1, 6 * D), mod_map)),
             gain("n1g"), layer(lw["w_in"]), whole(consts["seg64"]), whole(consts["seg96"]),
             gain("aqg"), gain("akg"), gain("cqg"), gain("ckg"), gain("bcqg"), gain("bckvg"), gain("bqg"), gain("bkg"),
             layer(lw["w_uq"]), layer(lw["w_uk"]), layer(lw["w_uv"]), whole(consts["place"])]
    ins = [a for a, _ in pairs]
    in_specs = [s for _, s in pairs]
    n_plain = len(ins)
    ins += list(prev_states)
    in_specs += [pl.BlockSpec(memory_space=pl.ANY) for _ in prev_states]
    if is_lat:
        tabs = [consts["cos64"], consts["sin64"], consts["cos96"], consts["sin96"]]
        ins += tabs + [consts["first64"], consts["first96"]]
        in_specs += [pl.BlockSpec((TQ, a.shape[1]), lambda i: (i % (LAT_LEN // TQ), 0)) for a in tabs]
        in_specs += [_full(consts["first64"].shape), _full(consts["first96"].shape)]
    widths = [4 * HD, 2 * HD, 2 * HD, 4 * BP, 4 * BP, 4 * B_V, 4 * HD, 2 * HD, 2 * HD]
    out_shape = [jax.ShapeDtypeStruct((t_rows, w), BF16) for w in widths]
    out_specs = [pl.BlockSpec((TQ, w), lambda i: (i, 0)) for w in widths]
    if not is_lat:
        for tail in STATE_TAILS:
            zeros = (0,) * len(tail)
            out_shape.append(jax.ShapeDtypeStruct((N_CTX_SEQ, DEPTH, CTX_LEN) + tail, F32))
            if prev_states:
                out_specs.append(pl.BlockSpec((SEQ_PER_STEP, None, CTX_LEN) + tail, lambda i, z=zeros: (i, l, 0) + z))
            else:
                out_specs.append(pl.BlockSpec((SEQ_PER_STEP, DEPTH, CTX_LEN) + tail, lambda i, z=zeros: (i, 0, 0) + z))
    aliases = {n_plain + k: len(widths) + k for k in range(len(prev_states))}
    return pl.pallas_call(
        functools.partial(_proj_kernel, is_lat, len(prev_states)),
        grid=(n_tiles,),
        in_specs=in_specs,
        out_specs=out_specs,
        out_shape=out_shape,
        input_output_aliases=aliases,
        compiler_params=pltpu.CompilerParams(
            dimension_semantics=("arbitrary",), vmem_limit_bytes=VMEM_LIMIT),
        name="proj_lat" if is_lat else "proj_ctx",
    )(*ins)


def _cache_kv_kernel(ckv_ref, kr_ref, seg96_ref, bkg_ref, wuk_ref, wuv_ref, plc_ref, kb_ref, vb_ref):
    ckv16 = ckv_ref[...].astype(BF16)
    vb_ref[...] = _dot(ckv16, wuv_ref[...]).astype(BF16)
    kpre = _dot(ckv16, wuk_ref[...]) + _dot(kr_ref[...].astype(BF16), plc_ref[0:B_ROPE, :])
    kb_ref[...] = _seg_norm(kpre, seg96_ref[...], bkg_ref[...], B_QK).astype(BF16)


def _cache_kv(cache_b_ckv, cache_b_krope, lw, consts):
    def c_map(l, r):
        return (r // (PAST // TM), l, r % (PAST // TM), 0)

    def w_spec(a):
        return pl.BlockSpec((None,) + a.shape[1:], lambda l, r: (l,) + (0,) * (a.ndim - 1))

    def s_spec(a):
        return pl.BlockSpec(a.shape, lambda l, r: (0,) * a.ndim)

    rows = N_LAT_SEQ * PAST
    return pl.pallas_call(
        _cache_kv_kernel,
        grid=(DEPTH, rows // TM),
        in_specs=[pl.BlockSpec((None, None, TM, B_KV_LORA), c_map), pl.BlockSpec((None, None, TM, B_ROPE), c_map),
                  s_spec(consts["seg96"]),
                  pl.BlockSpec((None, None, 1, 4 * BP), lambda l, r: (l, VEC_ROWS["bkg"][0], 0, 0)),
                  w_spec(lw["w_uk"]), w_spec(lw["w_uv"]),
                  s_spec(consts["place"])],
        out_specs=[pl.BlockSpec((None, TM, 4 * BP), lambda l, r: (l, r, 0)),
                   pl.BlockSpec((None, TM, 4 * B_V), lambda l, r: (l, r, 0))],
        out_shape=[jax.ShapeDtypeStruct((DEPTH, rows, 4 * BP), BF16),
                   jax.ShapeDtypeStruct((DEPTH, rows, 4 * B_V), BF16)],
        compiler_params=pltpu.CompilerParams(
            dimension_semantics=("arbitrary", "arbitrary"), vmem_limit_bytes=VMEM_LIMIT),
        name="cache_kv",
    )(cache_b_ckv, cache_b_krope, consts["seg96"], lw["vec"], lw["w_uk"], lw["w_uv"], consts["place"])


def _scores(q, segs):
    scores = []
    for k, _, mask in segs:
        s = _dot_t(q, k)
        if mask is not None:
            s = jnp.where(mask, s, NEG_INF)
        scores.append(s)
    return scores


def _softmax_pv(scores, segs, sink, num_cols, den_col):
    m = None
    for s in scores:
        sm = jnp.max(s, axis=-1, keepdims=True)
        m = sm if m is None else jnp.maximum(m, sm)
    if sink is not None:
        m = jnp.maximum(m, sink)
    acc = None
    den = None
    for s, (_, v, _) in zip(scores, segs):
        e = jnp.exp2(s - m)
        if den_col is None:
            d = jnp.sum(e, axis=-1, keepdims=True)
            den = d if den is None else den + d
        o = _dot(e.astype(BF16), v)
        acc = o if acc is None else acc + o
    if den_col is not None:
        den = acc[:, den_col:den_col + 1]
        acc = acc[:, num_cols]
    if sink is not None:
        den = den + jnp.exp2(sink - m)
    return acc / den


HEAD_LOOKAHEAD = 1


def _run_heads(jobs, o_ref):
    ready = [_scores(job[0](), job[1]()) for job in jobs[:HEAD_LOOKAHEAD]]
    for n, (_, load_segs, sink, col, num_cols, den_col) in enumerate(jobs):
        scores = ready.pop(0)
        if n + HEAD_LOOKAHEAD < len(jobs):
            ahead = jobs[n + HEAD_LOOKAHEAD]
            ready.append(_scores(ahead[0](), ahead[1]()))
        o = _softmax_pv(scores, load_segs(values=True), sink, num_cols, den_col)
        o_ref[:, col:col + o.shape[1]] = o.astype(BF16)


OUT_B = 4 * HD
OUT_C = 4 * HD + 4 * B_V
LOG2E = 1.4426950408889634


def _head_job(q_ref, q_cols, seg_refs, kv_cols, v_cols, sink, col, rows=None, masks=None, mxu_sums=True):
    width = v_cols.stop - v_cols.start
    if not mxu_sums:
        load_cols, num_cols, den_col = v_cols, None, None
    elif width == HD:
        pair = v_cols.start // LANE * LANE
        load_cols = slice(pair, pair + LANE)
        lo = v_cols.start - pair
        num_cols, den_col = slice(lo, lo + HD), (lo + HD) % LANE
    else:
        load_cols, num_cols, den_col = v_cols, slice(0, width), width
    if sink is not None:
        sink = sink * LOG2E

    def with_ones(v):
        if not mxu_sums:
            return v
        if width != HD:
            return jnp.concatenate([v, jnp.ones_like(v)], axis=1)
        lane = lax.broadcasted_iota(jnp.int32, (1, LANE), 1)
        keep = jnp.where((lane >= num_cols.start) & (lane < num_cols.stop), 1.0, 0.0).astype(BF16)
        return v * keep + (1.0 - keep)

    def load_segs(values=False):
        segs = []
        for n, (k_ref, v_ref) in enumerate(seg_refs):
            r = rows if (rows is not None and n == 0) else slice(None)
            if values:
                segs.append((None, with_ones(v_ref[r, load_cols]), None))
            else:
                segs.append((k_ref[r, kv_cols], None, None if masks is None else masks[n]))
        return segs

    return (lambda: q_ref[:, q_cols]), load_segs, sink, col, num_cols, den_col


def _attn_ctx_kernel(l, sink_ref, qa_ref, ka_ref, va_ref, qb_ref, kb_ref, vb_ref, qc_ref, kc_ref, vc_ref, o_ref):
    jobs = []
    for h in range(4):
        cs = slice(h // 2 * HD, (h // 2 + 1) * HD)
        jobs.append(_head_job(qa_ref, slice(h * HD, (h + 1) * HD), [(ka_ref, va_ref)], cs, cs, sink_ref[l, h],
                              h * HD, mxu_sums=False))
    for h in range(4):
        ks, vs = slice(h * BP, (h + 1) * BP), slice(h * B_V, (h + 1) * B_V)
        jobs.append(_head_job(qb_ref, ks, [(kb_ref, vb_ref)], ks, vs, None, OUT_B + h * B_V, mxu_sums=False))
    for h in range(4):
        cs = slice(h // 2 * HD, (h // 2 + 1) * HD)
        jobs.append(_head_job(qc_ref, slice(h * HD, (h + 1) * HD), [(kc_ref, vc_ref)], cs, cs, None, OUT_C + h * HD,
                              mxu_sums=False))
    _run_heads(jobs, o_ref)


def _attention_ctx(l, sink, p):
    names = ["qa", "ka", "va", "qb", "kb", "vb", "qc", "kc", "vc"]
    ins = [p[n] for n in names]
    in_specs = [pl.BlockSpec(memory_space=pltpu.SMEM)]
    in_specs += [pl.BlockSpec((CTX_LEN, a.shape[1]), lambda i: (i, 0)) for a in ins]
    return pl.pallas_call(
        functools.partial(_attn_ctx_kernel, l),
        grid=(N_CTX_SEQ,),
        in_specs=in_specs,
        out_specs=pl.BlockSpec((CTX_LEN, D), lambda i: (i, 0)),
        out_shape=jax.ShapeDtypeStruct((T_CTX, D), BF16),
        compiler_params=pltpu.CompilerParams(
            dimension_semantics=("arbitrary",), vmem_limit_bytes=VMEM_LIMIT),
        name="attn_ctx",
    )(sink, *ins)


WIN_SPAN = TM + 2 * WINDOW


def _attn_lat_kernel(l, sink_ref, qa_ref, qb_ref, qc_ref, ka_ref, va_ref, kb_ref, vb_ref, kc_ref, vc_ref,
                     cka_ref, cva_ref, ckb_ref, cvb_ref, ckc_ref, cvc_ref, o_ref):
    qi = pl.program_id(1)
    ws = pl.multiple_of(jnp.clip(qi * TM - WINDOW, 0, LAT_LEN - WIN_SPAN), WINDOW)
    qpos = qi * TM + lax.broadcasted_iota(jnp.int32, (TM, WIN_SPAN), 0)
    kpos = ws + lax.broadcasted_iota(jnp.int32, (TM, WIN_SPAN), 1)
    band = jnp.abs(qpos - kpos) <= WINDOW
    jobs = []
    for h in range(4):
        cs = slice(h // 2 * HD, (h // 2 + 1) * HD)
        jobs.append(_head_job(qa_ref, slice(h * HD, (h + 1) * HD), [(ka_ref, va_ref), (cka_ref, cva_ref)], cs, cs,
                              sink_ref[l, h], h * HD, rows=pl.ds(ws, WIN_SPAN), masks=(band, None)))
    for h in range(4):
        ks, vs = slice(h * BP, (h + 1) * BP), slice(h * B_V, (h + 1) * B_V)
        jobs.append(_head_job(qb_ref, ks, [(kb_ref, vb_ref), (ckb_ref, cvb_ref)], ks, vs, None, OUT_B + h * B_V))
    for h in range(4):
        cs = slice(h // 2 * HD, (h // 2 + 1) * HD)
        jobs.append(_head_job(qc_ref, slice(h * HD, (h + 1) * HD), [(kc_ref, vc_ref), (ckc_ref, cvc_ref)], cs, cs,
                              None, OUT_C + h * HD))
    _run_heads(jobs, o_ref)


def _attention_lat(l, sink, p, cache):
    q_ins = [p["qa"], p["qb"], p["qc"]]
    kv_ins = [p[n] for n in ("ka", "va", "kb", "vb", "kc", "vc")]
    c_names = ("ka", "va", "kb", "vb", "kc", "vc")
    c_ins = [cache[n] for n in c_names]
    in_specs = [pl.BlockSpec(memory_space=pltpu.SMEM)]
    in_specs += [pl.BlockSpec((TM, a.shape[1]), lambda b, i: (b * LAT_TILES_PER_SEQ + i, 0)) for a in q_ins]
    in_specs += [pl.BlockSpec((LAT_LEN, a.shape[1]), lambda b, i: (b, 0)) for a in kv_ins]
    for n, a in zip(c_names, c_ins):
        if n in ("kb", "vb"):
            in_specs.append(pl.BlockSpec((None, PAST, a.shape[-1]), lambda b, i: (l, b, 0)))
        else:
            in_specs.append(pl.BlockSpec((None, None, PAST, a.shape[-1]), lambda b, i: (b, l, 0, 0)))
    return pl.pallas_call(
        functools.partial(_attn_lat_kernel, l),
        grid=(N_LAT_SEQ, LAT_TILES_PER_SEQ),
        in_specs=in_specs,
        out_specs=pl.BlockSpec((TM, D), lambda b, i: (b * LAT_TILES_PER_SEQ + i, 0)),
        out_shape=jax.ShapeDtypeStruct((T_LAT, D), BF16),
        compiler_params=pltpu.CompilerParams(
            dimension_semantics=("arbitrary", "arbitrary"), vmem_limit_bytes=VMEM_LIMIT),
        name="attn_lat",
    )(sink, *q_ins, *kv_ins, *c_ins)


def _pack_rows(x):
    half = x.shape[1] // 2
    r = x.astype(BF16).astype(F32)
    hi = lax.bitcast_convert_type(r[:, :half], jnp.int32)
    lo = lax.bitcast_convert_type(r[:, half:], jnp.int32)
    return jnp.bitwise_or(hi, lax.shift_right_logical(lo, 16))


def _unpack_rows(p):
    a = lax.bitcast_convert_type(jnp.bitwise_and(p, -65536), F32)
    b = lax.bitcast_convert_type(lax.shift_left(p, 16), F32)
    return jnp.concatenate([a, b], axis=1).astype(BF16)


ROUTE_ROWS = 16


def _post_kernel(mc_ref, ml_ref, xc_ref, xl_ref, mod_ref, n2g_ref, wout_ref, rwt_ref, rb_ref, tri_ref,
                 xo_ref, h2_ref, gate_ref, route_ref, cnt_ref, run_ref):
    i = pl.program_id(0)

    @pl.when(i == 0)
    def _():
        run_ref[...] = jnp.zeros_like(run_ref)
        cnt_ref[...] = jnp.zeros_like(cnt_ref)

    is_ctx = i < T_CTX // TP
    m = mod_ref[...]
    g1, sh2, sc2 = m[:, 2 * D:3 * D], m[:, 3 * D:4 * D], m[:, 4 * D:5 * D]

    mix = jnp.where(is_ctx, mc_ref[...], ml_ref[...])
    x = jnp.where(is_ctx, xc_ref[...], xl_ref[...]) + g1 * _dot(mix, wout_ref[...])
    xo_ref[...] = x
    h2 = _rms(x, n2g_ref[...]) * (1.0 + sc2) + sh2
    hi = h2.astype(BF16)
    h2_ref[...] = _pack_rows(h2)
    lo = (h2 - hi.astype(F32)).astype(BF16)
    a = _dot_t(rwt_ref[...], hi)
    b = _dot_t(rwt_ref[0:N_EXPERTS, :], lo)
    logits = a[0:N_EXPERTS] + a[N_EXPERTS:] + b + rb_ref[...]
    eidx = lax.broadcasted_iota(jnp.int32, logits.shape, 0).astype(F32)
    work = logits
    vals, idxs, hots = [], [], []
    for _ in range(TOP_K):
        v = jnp.max(work, axis=0, keepdims=True)
        idx = jnp.min(jnp.where(work == v, eidx, float(N_EXPERTS)), axis=0, keepdims=True)
        hot = eidx == idx
        vals.append(v)
        idxs.append(idx)
        hots.append(hot)
        work = jnp.where(hot, -jnp.inf, work)
    es = [jnp.exp(v - vals[0]) for v in vals]
    den = es[0] + es[1] + es[2] + es[3]
    gates = [e / den for e in es]
    sel = jnp.where(hots[0] | hots[1] | hots[2] | hots[3], 1.0, 0.0)
    run = run_ref[:, 0:1]
    before = _dot(sel.astype(BF16), tri_ref[...]) + run
    run_new = jnp.broadcast_to(run + jnp.sum(sel, axis=1, keepdims=True), run_ref.shape)
    run_ref[...] = run_new
    cnt_ref[...] = run_new
    ranks = [jnp.sum(jnp.where(hots[k], before, 0.0), axis=0, keepdims=True) for k in range(TOP_K)]
    pad = [jnp.zeros((ROUTE_ROWS - 3 * TOP_K, TP), F32)]
    route_ref[...] = jnp.concatenate(gates + idxs + ranks + pad, axis=0)
    gate_ref[...] = jnp.concatenate(gates + [jnp.zeros((LANE - TOP_K, TP), F32)], axis=0).T


def _mod_row(i, tm=TM):
    n_ctx = T_CTX // tm
    return jnp.where(i < n_ctx, 0, 1 + (i - n_ctx) // (LAT_LEN // tm))


def _ctx_tile(i):
    return jnp.minimum(i, T_CTX // TP - 1)


def _lat_tile(i):
    return jnp.maximum(i - T_CTX // TP, 0)


def _post_attention(l, mix_ctx, mix_lat, x_ctx, x_lat, x_lat_off, mod, lw, consts):
    ins = [mix_ctx, mix_lat, x_ctx, x_lat, mod, lw["vec"], lw["w_out"], lw["rwt"], lw["rb"], consts["tri"]]
    in_specs = [
        pl.BlockSpec((TP, D), lambda i: (_ctx_tile(i), 0)),
        pl.BlockSpec((TP, D), lambda i: (_lat_tile(i), 0)),
        pl.BlockSpec((TP, D), lambda i: (_ctx_tile(i), 0)),
        pl.BlockSpec((TP, D), lambda i: (_lat_tile(i) + x_lat_off, 0)),
        pl.BlockSpec((None, None, 1, 6 * D), lambda i: (l, _mod_row(i, TP), 0, 0)),
        _vec_spec(l, "n2g"),
    ] + [_layer_spec(a, l) for a in ins[6:9]] + [_full(consts["tri"].shape)]
    return pl.pallas_call(
        _post_kernel,
        grid=(T_ALL // TP,),
        in_specs=in_specs,
        out_specs=[pl.BlockSpec((TP, D), lambda i: (i, 0)), pl.BlockSpec((TP, DW), lambda i: (i, 0)),
                   pl.BlockSpec((TP, LANE), lambda i: (i, 0)), pl.BlockSpec((ROUTE_ROWS, TP), lambda i: (0, i)),
                   pl.BlockSpec((N_EXPERTS, LANE), lambda i: (0, 0))],
        out_shape=[jax.ShapeDtypeStruct((T_ALL, D), F32), jax.ShapeDtypeStruct((T_ALL, DW), jnp.int32),
                   jax.ShapeDtypeStruct((T_ALL, LANE), F32), jax.ShapeDtypeStruct((ROUTE_ROWS, T_ALL), F32),
                   jax.ShapeDtypeStruct((N_EXPERTS, LANE), F32)],
        scratch_shapes=[pltpu.VMEM((N_EXPERTS, LANE), F32)],
        compiler_params=pltpu.CompilerParams(
            dimension_semantics=("arbitrary",), vmem_limit_bytes=VMEM_LIMIT),
        name="post_attn",
    )(*ins)


def _expert_kernel(l, te_ref, nu_ref, nxt_ref, nv_ref, x_ref, wgu_hbm, bgu_ref, wdn_hbm, bdn_ref, y_ref,
                   wgu32, wdn32, wgu16, wdn16, sem):
    i = pl.program_id(0)
    prev = te_ref[jnp.maximum(i - 1, 0)]
    new_expert = jnp.logical_or(i == 0, te_ref[i] != prev)

    def weight_copies(e):
        return (pltpu.make_async_copy(wgu_hbm.at[l, e], wgu32, sem.at[0]),
                pltpu.make_async_copy(wdn_hbm.at[l, e], wdn32, sem.at[1]))

    @pl.when(i == 0)
    def _():
        for cp in weight_copies(te_ref[0]):
            cp.start()

    @pl.when(new_expert)
    def _():
        for cp in weight_copies(te_ref[i]):
            cp.wait()

    def compute(rows, convert):
        x = _unpack_rows(x_ref[0:rows, :])

        def gate_up(a):
            if convert:
                for c in (slice(a, a + FF_CHUNK), slice(D_FF + a, D_FF + a + FF_CHUNK)):
                    wgu16[:, c] = wgu32[:, c].astype(BF16)
                wdn16[a:a + FF_CHUNK, :] = wdn32[a:a + FF_CHUNK, :].astype(BF16)
            b = a + FF_CHUNK
            g = _dot(x, wgu16[:, a:b]) + bgu_ref[:, a:b]
            up = _dot(x, wgu16[:, D_FF + a:D_FF + b]) + bgu_ref[:, D_FF + a:D_FF + b]
            return g, up

        def activate(g, up):
            g = jnp.minimum(g, SWIGLU_LIMIT)
            up = jnp.clip(up, -SWIGLU_LIMIT, SWIGLU_LIMIT)
            return ((up + 1.0) * (g * jax.nn.sigmoid(SWIGLU_ALPHA * g))).astype(BF16)

        starts = list(range(0, D_FF, FF_CHUNK))
        acc = None
        pre = gate_up(starts[0])
        for n, a in enumerate(starts):
            hid = activate(*pre)
            if n + 1 < len(starts):
                pre = gate_up(starts[n + 1])
            o = _dot(hid, wdn16[a:a + FF_CHUNK, :])
            acc = o if acc is None else acc + o
        y_ref[0:rows, :] = _pack_rows(acc + bdn_ref[...])

    used = i < nu_ref[0]
    for n_valid in (1, 2):
        for convert in (False, True):
            @pl.when(used & (nv_ref[i] == n_valid) & (new_expert == convert))
            def _():
                compute(n_valid * MOE_TM, convert)

    @pl.when(new_expert & (nxt_ref[i] >= 0))
    def _():
        for cp in weight_copies(nxt_ref[i]):
            cp.start()


def _experts(l, tile_expert, n_used, next_expert, n_valid, x_sorted, w_gu, b_gu, w_dn, b_dn):
    def row_map(i, te, nu, nxt, nv):
        return (jnp.minimum(i, nu[0] - 1), 0)

    def b_map(i, te, nu, nxt, nv):
        return (l, te[i], 0, 0)

    grid_spec = pltpu.PrefetchScalarGridSpec(
        num_scalar_prefetch=4,
        grid=(MOE_STEPS,),
        in_specs=[
            pl.BlockSpec((MOE_STEP, DW), row_map),
            pl.BlockSpec(memory_space=pl.ANY),
            pl.BlockSpec((None, None, 1, 2 * D_FF), b_map),
            pl.BlockSpec(memory_space=pl.ANY),
            pl.BlockSpec((None, None, 1, D), b_map),
        ],
        out_specs=pl.BlockSpec((MOE_STEP, DW), row_map),
        scratch_shapes=[pltpu.VMEM((D, 2 * D_FF), F32), pltpu.VMEM((D_FF, D), F32),
                        pltpu.VMEM((D, 2 * D_FF), BF16), pltpu.VMEM((D_FF, D), BF16),
                        pltpu.SemaphoreType.DMA((2,))],
    )
    return pl.pallas_call(
        functools.partial(_expert_kernel, l),
        grid_spec=grid_spec,
        out_shape=jax.ShapeDtypeStruct((MOE_ROWS, DW), jnp.int32),
        compiler_params=pltpu.CompilerParams(
            dimension_semantics=("arbitrary",), vmem_limit_bytes=VMEM_LIMIT),
        name="experts",
    )(tile_expert, n_used, next_expert, n_valid, x_sorted, w_gu, b_gu.reshape(DEPTH, N_EXPERTS, 1, 2 * D_FF), w_dn,
      b_dn.reshape(DEPTH, N_EXPERTS, 1, D))


def _combine_kernel(x_ref, y_ref, route_ref, mod_ref, o_ref):
    g2 = mod_ref[...][:, 5 * D:6 * D]
    r = route_ref[...]
    acc = r[:, 0:1] * _unpack_rows(y_ref[0]).astype(F32)
    for k in range(1, TOP_K):
        acc = acc + r[:, k:k + 1] * _unpack_rows(y_ref[k]).astype(F32)
    o_ref[...] = x_ref[...] + g2 * acc


def _combine(l, first_tile, n_tiles, x_all, y_tok, gates, mod):
    return pl.pallas_call(
        _combine_kernel,
        grid=(n_tiles,),
        in_specs=[
            pl.BlockSpec((TM, D), lambda i: (i + first_tile, 0)),
            pl.BlockSpec((TOP_K, TM, DW), lambda i: (0, i, 0)),
            pl.BlockSpec((TM, LANE), lambda i: (i + first_tile, 0)),
            pl.BlockSpec((None, None, 1, 6 * D), lambda i: (l, _mod_row(i + first_tile), 0, 0)),
        ],
        out_specs=pl.BlockSpec((TM, D), lambda i: (i, 0)),
        out_shape=jax.ShapeDtypeStruct((n_tiles * TM, D), F32),
        compiler_params=pltpu.CompilerParams(
            dimension_semantics=("arbitrary",), vmem_limit_bytes=VMEM_LIMIT),
        name="combine",
    )(x_all, y_tok, gates, mod)


SC_CORES, SC_SUBCORES = 2, 16
SC_WORKERS = SC_CORES * SC_SUBCORES
SC_CHUNK = 128
SC_CHUNKS_PER_WORKER = T_ALL // SC_CHUNK // SC_WORKERS


def _sc_mesh():
    return plsc.VectorSubcoreMesh(core_axis_name="c", subcore_axis_name="s")


def _sc_scratch():
    return [pltpu.VMEM((TOP_K, SC_CHUNK), jnp.int32), pltpu.VMEM((SC_CHUNK, DW), jnp.int32),
            pltpu.SemaphoreType.DMA]


def _dispatch_rows(h2p, dest):
    @functools.partial(pl.kernel, mesh=_sc_mesh(), out_type=jax.ShapeDtypeStruct((MOE_ROWS, DW), jnp.int32),
                       scratch_types=_sc_scratch(), name="dispatch_rows")
    def run(h_hbm, d_hbm, o_hbm, idx_v, rows_v, sem):
        wid = lax.axis_index("s") * SC_CORES + lax.axis_index("c")
        for j in range(SC_CHUNKS_PER_WORKER):
            c = wid * SC_CHUNKS_PER_WORKER + j
            pltpu.sync_copy(d_hbm.at[c], idx_v)
            pltpu.sync_copy(h_hbm.at[pl.ds(c * SC_CHUNK, SC_CHUNK)], rows_v)
            copies = [pltpu.async_copy(rows_v, o_hbm.at[idx_v.at[k]], sem) for k in range(TOP_K)]
            for cp in copies:
                cp.wait()

    return run(h2p, dest)


def _gather_rows(y, dest, first_chunk, n_chunks, *after):
    per_worker = n_chunks // SC_WORKERS
    assert per_worker * SC_WORKERS == n_chunks

    @functools.partial(pl.kernel, mesh=_sc_mesh(),
                       out_type=jax.ShapeDtypeStruct((TOP_K, n_chunks * SC_CHUNK, DW), jnp.int32),
                       scratch_types=_sc_scratch(), name="gather_rows")
    def run(y_hbm, d_hbm, *rest):
        o_hbm, idx_v, rows_v, sem = rest[len(after):]
        wid = lax.axis_index("s") * SC_CORES + lax.axis_index("c")
        for j in range(per_worker):
            c = wid * per_worker + j
            pltpu.sync_copy(d_hbm.at[first_chunk + c], idx_v)
            for k in range(TOP_K):
                pltpu.async_copy(y_hbm.at[idx_v.at[k]], rows_v, sem).wait()
                pltpu.sync_copy(rows_v, o_hbm.at[k, pl.ds(c * SC_CHUNK, SC_CHUNK)])

    return run(y, dest, *after)


def _constants():
    lane64 = np.arange(4 * HD)
    seg64 = (lane64[:, None] // HD == lane64[None, :] // HD).astype(np.float32)
    lane96 = np.arange(4 * BP)
    real = lane96 % BP < B_QK
    seg96 = ((lane96[:, None] // BP == lane96[None, :] // BP) & real[:, None] & real[None, :]).astype(np.float32)
    place = np.zeros((LANE, 4 * BP), np.float32)
    for hh in range(4):
        place[np.arange(B_ROPE), hh * BP + B_NOPE + np.arange(B_ROPE)] = 1.0

    def angles(rot_dim):
        pos = np.arange(LAT_LEN)
        rows = (pos // GRID_W).astype(np.float32)
        cols = (pos % GRID_W).astype(np.float32)
        axis_dim = rot_dim // 2
        inv = np.power(np.float32(ROPE_THETA), -(np.arange(0, axis_dim, 2, dtype=np.float32) / np.float32(axis_dim)))
        ang = np.concatenate([rows[:, None] * inv, cols[:, None] * inv], axis=-1).astype(np.float32)
        return np.cos(ang), np.sin(ang)

    def head_tables(rot_dim):
        cos, sin = angles(rot_dim)
        q = rot_dim // 4
        cr, cc, sr, sc = cos[:, :q], cos[:, q:], sin[:, :q], sin[:, q:]
        return (np.concatenate([cr, cr, cc, cc], axis=-1), np.concatenate([-sr, sr, -sc, sc], axis=-1))

    c64, s64 = head_tables(HD)
    cos64 = np.tile(c64, (1, 4))
    sin64 = np.tile(s64, (1, 4))
    c32, s32 = head_tables(B_ROPE)
    ones = np.ones((LAT_LEN, B_NOPE), np.float32)
    zeros = np.zeros((LAT_LEN, B_NOPE), np.float32)
    padz = np.zeros((LAT_LEN, BP - B_QK), np.float32)
    cos96 = np.tile(np.concatenate([ones, c32, padz], axis=-1), (1, 4))
    sin96 = np.tile(np.concatenate([zeros, s32, padz], axis=-1), (1, 4))
    first64 = ((lane64 % 32) < 16).astype(np.float32)[None, :]
    first96 = (((lane96 % BP) % 16) < 8).astype(np.float32)[None, :]
    tri = (np.arange(TP)[:, None] < np.arange(TP)[None, :]).astype(np.float32)
    f32 = lambda a: jnp.asarray(a, F32)
    return dict(seg64=jnp.asarray(seg64, BF16), seg96=jnp.asarray(seg96, BF16), place=jnp.asarray(place, BF16),
                tri=jnp.asarray(tri, BF16),
                cos64=f32(cos64), sin64=f32(sin64), cos96=f32(cos96), sin96=f32(sin96),
                first64=f32(first64), first96=f32(first96))


def _pad_heads(w, per_head, width):
    lead = w.shape[:-1]
    w = w.reshape(lead + (4, per_head))
    return jnp.pad(w, ((0, 0),) * (len(lead) + 1) + ((0, width - per_head),)).reshape(lead + (4 * width,))


def _weights(norm1_g, norm2_g, w_in, a_q_g, a_k_g, b_cq_g, b_ckv_g, w_uq, w_ukv, b_q_g, b_k_g,
             c_q_g, c_k_g, w_out, router_w, router_b):
    o = np.cumsum((0, 256, 128, 128, 384, 256, 32, 256, 128, 128))
    seg = lambda k: w_in[:, :, o[k]:o[k + 1]]
    w_in_r = jnp.concatenate([seg(0), seg(1), seg(2), seg(3), seg(4), seg(6), seg(7), seg(8), seg(5),
                              jnp.zeros((DEPTH, D, LANE - B_ROPE), F32)], axis=-1).astype(BF16)
    ukv = w_ukv.reshape(DEPTH, B_KV_LORA, 4, B_NOPE + B_V)
    w_uk = _pad_heads(ukv[..., :B_NOPE].reshape(DEPTH, B_KV_LORA, 4 * B_NOPE), B_NOPE, BP)
    w_uv = ukv[..., B_NOPE:].reshape(DEPTH, B_KV_LORA, 4 * B_V)
    rw_hi = router_w.astype(BF16)
    rw_lo = (router_w - rw_hi.astype(F32)).astype(BF16)
    tile = lambda v, n: jnp.tile(v, (1, n))
    rows = dict(n1g=norm1_g, n2g=norm2_g, aqg=tile(a_q_g, 4), akg=tile(a_k_g, 2), cqg=tile(c_q_g, 4),
                ckg=tile(c_k_g, 2), bcqg=b_cq_g, bckvg=b_ckv_g,
                bqg=_pad_heads(tile(b_q_g, 4), B_QK, BP), bkg=_pad_heads(tile(b_k_g, 4), B_QK, BP))
    order = sorted(VEC_ROWS, key=lambda n: VEC_ROWS[n][0])
    assert all(rows[n].shape == (DEPTH, VEC_ROWS[n][1]) for n in order)
    vec = jnp.stack([jnp.pad(rows[n], ((0, 0), (0, D - rows[n].shape[1]))) for n in order], axis=1)
    return dict(
        vec=vec[:, :, None, :], w_in=w_in_r,
        w_uq=_pad_heads(w_uq, B_QK, BP).astype(BF16), w_uk=w_uk.astype(BF16), w_uv=w_uv.astype(BF16),
        w_out=w_out.astype(BF16),
        rwt=jnp.concatenate([jnp.swapaxes(rw_hi, 1, 2), jnp.swapaxes(rw_lo, 1, 2)], axis=1),
        rb=router_b[:, :, None])


def _moe(l, x_all, h2p, gate_slab, route, counts, mod, w_gu, b_gu, w_dn, b_dn):
    counts = counts[:, 0].astype(jnp.int32)
    padded = (counts + MOE_STEP - 1) // MOE_STEP * MOE_STEP
    pend = jnp.cumsum(padded)
    pstart = pend - padded
    n_used = (pend[-1] // MOE_STEP).astype(jnp.int32)
    steps = jnp.minimum(jnp.arange(MOE_STEPS, dtype=jnp.int32), n_used - 1)
    tile_expert = jnp.sum((pend[None, :] <= steps[:, None] * MOE_STEP).astype(jnp.int32), axis=1)
    tile_expert = jnp.minimum(tile_expert, N_EXPERTS - 1)
    of_expert = lambda table: jnp.sum(
        jnp.where(tile_expert[:, None] == jnp.arange(N_EXPERTS)[None, :], table[None, :], 0), axis=1)
    group_end_step = of_expert(pend) // MOE_STEP
    after = jnp.sum((pend[None, :] <= group_end_step[:, None] * MOE_STEP).astype(jnp.int32), axis=1)
    next_expert = jnp.where(group_end_step < n_used, jnp.minimum(after, N_EXPERTS - 1), -1).astype(jnp.int32)
    rows_left = of_expert(pstart + counts) - steps * MOE_STEP
    n_valid = jnp.clip((rows_left + MOE_TM - 1) // MOE_TM, 1, MOE_STEP // MOE_TM).astype(jnp.int32)
    e = route[TOP_K:2 * TOP_K].astype(jnp.int32)
    rank = route[2 * TOP_K:3 * TOP_K].astype(jnp.int32)
    start = jnp.sum(jnp.where(e[:, :, None] == jnp.arange(N_EXPERTS)[None, None, :], pstart[None, None, :], 0), axis=-1)
    dest = (start + rank).reshape(TOP_K, T_ALL // SC_CHUNK, SC_CHUNK).transpose(1, 0, 2)
    x_sorted = _dispatch_rows(h2p, dest)
    y = _experts(l, tile_expert, n_used.reshape(1), next_expert, n_valid, x_sorted, w_gu, b_gu, w_dn, b_dn)
    ctx_chunks, lat_chunks = T_CTX // SC_CHUNK, T_LAT // SC_CHUNK
    y_ctx = _gather_rows(y, dest, 0, ctx_chunks)
    out_ctx = _combine(l, 0, N_CTX_TILES, x_all, y_ctx, gate_slab, mod)
    y_lat = _gather_rows(y, dest, ctx_chunks, lat_chunks, out_ctx)
    return out_ctx, _combine(l, N_CTX_TILES, N_LAT_TILES, x_all, y_lat, gate_slab, mod)


def kernel(x_prompt, x_sample, cache_a_k, cache_a_v, cache_b_ckv, cache_b_krope, cache_c_k, cache_c_v, c, c_ctx,
           norm1_g, norm2_g, w_mod, b_mod, w_in, a_q_g, a_k_g, a_sink, b_cq_g, b_ckv_g, w_uq, w_ukv, b_q_g, b_k_g,
           c_q_g, c_k_g, w_out, router_w, router_b, w_gu, b_gu, w_dn, b_dn):
    consts = _constants()
    cond = jnp.concatenate([c_ctx[None, :], c, jnp.zeros((3, D), F32)], axis=0)
    mod = _modulation(cond, w_mod, b_mod).reshape(DEPTH, 8, 1, 6 * D)
    x_ctx, x_lat, x_lat_off = x_prompt.reshape(T_CTX, D), x_sample.reshape(T_LAT, D), 0

    lw = _weights(norm1_g, norm2_g, w_in, a_q_g, a_k_g, b_cq_g, b_ckv_g, w_uq, w_ukv, b_q_g, b_k_g,
                  c_q_g, c_k_g, w_out, router_w, router_b)
    ckb, cvb = _cache_kv(cache_b_ckv, cache_b_krope, lw, consts)
    merge_heads = lambda a: a.reshape(N_LAT_SEQ, DEPTH, PAST, 2 * HD).astype(BF16)
    cache = dict(ka=merge_heads(cache_a_k), va=merge_heads(cache_a_v), kb=ckb, vb=cvb,
                 kc=merge_heads(cache_c_k), vc=merge_heads(cache_c_v))

    states = ()
    names = ["qa", "ka", "va", "qb", "kb", "vb", "qc", "kc", "vc"]
    for l in range(DEPTH):
        outs = _projection(False, l, x_ctx, 0, mod, lw, consts, states)
        p_ctx = dict(zip(names, outs[:9]))
        states = tuple(outs[9:])
        p_lat = dict(zip(names, _projection(True, l, x_lat, x_lat_off, mod, lw, consts)))
        mix_ctx = _attention_ctx(l, a_sink, p_ctx)
        mix_lat = _attention_lat(l, a_sink, p_lat, cache)
        x_mid, h2p, gate_slab, route, counts = _post_attention(l, mix_ctx, mix_lat, x_ctx, x_lat, x_lat_off, mod, lw,
                                                               consts)
        x_ctx, x_lat = _moe(l, x_mid, h2p, gate_slab, route, counts, mod, w_gu, b_gu, w_dn, b_dn)
    y_ctx, y_lat = x_ctx, x_lat

    return (y_ctx.reshape(N_CTX_SEQ, CTX_LEN, D), y_lat.reshape(N_LAT_SEQ, LAT_LEN, D)) + states
```

```python
import functools

import jax
import jax.numpy as jnp
import numpy as np
from jax import lax
from jax.experimental import pallas as pl
from jax.experimental.pallas import tpu as pltpu
from jax.experimental.pallas import tpu_sc as plsc

F32 = jnp.float32
BF16 = jnp.bfloat16

D = 1024
DEPTH = 2
N_CTX_SEQ, CTX_LEN = 16, 256
N_LAT_SEQ, LAT_LEN = 4, 2048
PAST = 512
T_CTX = N_CTX_SEQ * CTX_LEN
T_LAT = N_LAT_SEQ * LAT_LEN
T_ALL = T_CTX + T_LAT
GRID_W = 64
HD = 64
WINDOW = 128
B_NOPE, B_ROPE, B_V = 64, 32, 128
B_QK = B_NOPE + B_ROPE
B_Q_LORA, B_KV_LORA = 384, 256
N_EXPERTS, TOP_K = 32, 4
D_FF = 1024
SWIGLU_LIMIT = 7.0
SWIGLU_ALPHA = 1.702
ROPE_THETA = 10000.0
EPS = 1e-6
NEG_INF = -1e30

TM = 256
TP = 512
TQ_LAT = 512
TQ_CTX = 256
SEQ_PER_STEP = TQ_CTX // CTX_LEN
LANE = 128
BP = 128
N_CTX_TILES = T_CTX // TM
N_LAT_TILES = T_LAT // TM
N_TILES = T_ALL // TM
LAT_TILES_PER_SEQ = LAT_LEN // TM
MOE_TM = 256
MOE_STEP = 2 * MOE_TM
FF_CHUNK = 256
MOE_STEPS = T_ALL * TOP_K // MOE_STEP + N_EXPERTS
MOE_ROWS = MOE_STEPS * MOE_STEP
DW = D // 2
VMEM_LIMIT = 56 * 1024 * 1024

C_QA, C_KA, C_VA, C_CQ, C_CKV, C_QC, C_KC, C_VC, C_KR, C_END = (
    0, 256, 384, 512, 896, 1152, 1408, 1536, 1664, 1792)


def _dot(a, b):
    return jnp.dot(a, b, preferred_element_type=F32)


def _dot_t(a, b):
    return lax.dot_general(a, b, (((1,), (1,)), ((), ())), preferred_element_type=F32)


def _rms(x, g):
    return x * lax.rsqrt(jnp.mean(x * x, axis=-1, keepdims=True) + EPS) * g


def _seg_norm(x, seg, g, n):
    ss = _dot((x * x).astype(BF16), seg)
    return x * lax.rsqrt(ss * (1.0 / n) + EPS) * g


def _rope(x, cos, sin, first, sh):
    w = x.shape[1]
    fwd = pltpu.roll(x, w - sh, 1)
    bwd = pltpu.roll(x, sh, 1)
    return x * cos + jnp.where(first > 0.5, fwd, bwd) * sin


MOD_BN = 1536


def _mod_kernel(c_ref, w_ref, b_ref, o_ref):
    c = c_ref[...]
    s = (c * jax.nn.sigmoid(c)).astype(BF16)
    o_ref[...] = _dot(s, w_ref[...].astype(BF16)) + b_ref[...]


def _modulation(cond, w_mod, b_mod):
    return pl.pallas_call(
        _mod_kernel,
        grid=(DEPTH, 6 * D // MOD_BN),
        in_specs=[
            pl.BlockSpec((8, D), lambda l, j: (0, 0)),
            pl.BlockSpec((None, D, MOD_BN), lambda l, j: (l, 0, j)),
            pl.BlockSpec((None, 1, MOD_BN), lambda l, j: (l, 0, j)),
        ],
        out_specs=pl.BlockSpec((None, 8, MOD_BN), lambda l, j: (l, 0, j)),
        out_shape=jax.ShapeDtypeStruct((DEPTH, 8, 6 * D), F32),
        compiler_params=pltpu.CompilerParams(
            dimension_semantics=("arbitrary", "arbitrary"), vmem_limit_bytes=VMEM_LIMIT),
        name="modulation",
    )(cond, w_mod, b_mod.reshape(DEPTH, 1, 6 * D))


def _proj_kernel(is_lat, n_aliased, *refs):
    (x_ref, mod_ref, n1g_ref, win_ref, seg64_ref, seg96_ref, aqg_ref, akg_ref, cqg_ref, ckg_ref,
     bcqg_ref, bckvg_ref, bqg_ref, bkg_ref, wuq_ref, wuk_ref, wuv_ref, plc_ref) = refs[:18]
    refs = refs[18 + n_aliased:]
    if is_lat:
        (cos64_ref, sin64_ref, cos96_ref, sin96_ref, f64_ref, f96_ref) = refs[:6]
        refs = refs[6:]
    (qa_ref, ka_ref, va_ref, qb_ref, kb_ref, vb_ref, qc_ref, kc_ref, vc_ref) = refs[:9]
    refs = refs[9:]
    if not is_lat:
        if n_aliased == 0:
            for r in refs:
                r[:, 1:] = jnp.zeros((SEQ_PER_STEP, DEPTH - 1) + r.shape[2:], F32)
            refs = [r.at[:, 0] for r in refs]
        (kas_ref, vas_ref, ckvs_ref, krs_ref, kcs_ref, vcs_ref) = refs

    x = x_ref[...]
    m = mod_ref[...]
    sh1, sc1 = m[:, 0:D], m[:, D:2 * D]
    h = (_rms(x, n1g_ref[...]) * (1.0 + sc1) + sh1).astype(BF16)

    def proj(a, b):
        return _dot(h, win_ref[:, a:b])

    def rope64(t):
        wd = t.shape[1]
        return _rope(t, cos64_ref[:, :wd], sin64_ref[:, :wd], f64_ref[:, :wd], 16) if is_lat else t

    def rope96(t):
        return _rope(t, cos96_ref[...], sin96_ref[...], f96_ref[...], 8) if is_lat else t

    seg64 = seg64_ref[...]
    seg64h = seg64_ref[0:2 * HD, 0:2 * HD]
    seg96 = seg96_ref[...]

    def seg_sum(t, seg):
        return _dot((t * t).astype(BF16), seg)

    def seg_finish(t, ss, g, n):
        return t * lax.rsqrt(ss * (1.0 / n) + EPS) * g

    p_cq, p_ckv, p_kr = proj(C_CQ, C_CKV), proj(C_CKV, C_QC), proj(C_KR, C_END)
    p_qa, p_ka, p_qc, p_kc = proj(C_QA, C_KA), proj(C_KA, C_VA), proj(C_QC, C_KC), proj(C_KC, C_VC)
    p_va, p_vc = proj(C_VA, C_CQ), proj(C_VC, C_KR)

    cq = _rms(p_cq, bcqg_ref[...]).astype(BF16)
    ckv = _rms(p_ckv, bckvg_ref[...])
    ckv16 = ckv.astype(BF16)
    u_q = _dot(cq, wuq_ref[...])
    u_k = _dot(ckv16, wuk_ref[...]) + _dot(p_kr.astype(BF16), plc_ref[...])
    u_v = _dot(ckv16, wuv_ref[...])
    ss_qa, ss_ka = seg_sum(p_qa, seg64), seg_sum(p_ka, seg64h)
    ss_qc, ss_kc = seg_sum(p_qc, seg64), seg_sum(p_kc, seg64h)
    ss_qb, ss_kb = seg_sum(u_q, seg96), seg_sum(u_k, seg96)

    def store_state(ref, t):
        for s in range(SEQ_PER_STEP):
            ref[s] = t[s * CTX_LEN:(s + 1) * CTX_LEN]

    def store_kv_state(ref, t):
        for s in range(SEQ_PER_STEP):
            for kv in range(2):
                ref[s, :, kv, :] = t[s * CTX_LEN:(s + 1) * CTX_LEN, kv * HD:(kv + 1) * HD]

    if not is_lat:
        store_kv_state(vas_ref, p_va)
        store_kv_state(vcs_ref, p_vc)
        store_state(ckvs_ref, ckv)
        store_state(krs_ref, p_kr[:, 0:B_ROPE])
    va_ref[...] = p_va.astype(BF16)
    vc_ref[...] = p_vc.astype(BF16)
    vb_ref[...] = u_v.astype(BF16)

    t = seg_finish(p_qa, ss_qa, aqg_ref[...], HD)
    qa_ref[...] = (rope64(t) * (HD ** -0.5 * LOG2E)).astype(BF16)
    t = seg_finish(p_ka, ss_ka, akg_ref[...], HD)
    if not is_lat:
        store_kv_state(kas_ref, t)
    ka_ref[...] = rope64(t).astype(BF16)
    t = seg_finish(p_qc, ss_qc, cqg_ref[...], HD)
    qc_ref[...] = (rope64(t) * (HD ** -0.5 * LOG2E)).astype(BF16)
    t = seg_finish(p_kc, ss_kc, ckg_ref[...], HD)
    if not is_lat:
        store_kv_state(kcs_ref, t)
    kc_ref[...] = rope64(t).astype(BF16)
    t = seg_finish(u_q, ss_qb, bqg_ref[...], B_QK)
    qb_ref[...] = (rope96(t) * (B_QK ** -0.5 * LOG2E)).astype(BF16)
    t = seg_finish(u_k, ss_kb, bkg_ref[...], B_QK)
    kb_ref[...] = rope96(t).astype(BF16)


def _full(shape):
    nd = len(shape)
    return pl.BlockSpec(shape, lambda i: (0,) * nd)


STATE_TAILS = ((2, HD), (2, HD), (B_KV_LORA,), (B_ROPE,), (2, HD), (2, HD))


def _layer_spec(a, l):
    nd = a.ndim - 1
    return pl.BlockSpec((None,) + a.shape[1:], lambda *_: (l,) + (0,) * nd)


VEC_ROWS = dict(n1g=(0, D), n2g=(1, D), aqg=(2, 4 * HD), akg=(3, 2 * HD), cqg=(4, 4 * HD), ckg=(5, 2 * HD),
                bcqg=(6, B_Q_LORA), bckvg=(7, B_KV_LORA), bqg=(8, 4 * BP), bkg=(9, 4 * BP))


def _vec_spec(l, name):
    row, width = VEC_ROWS[name]
    return pl.BlockSpec((None, None, 1, width), lambda *_: (l, row, 0, 0))


def _projection(is_lat, l, x_src, x_off, mod, lw, consts, prev_states=()):
    assert is_lat or bool(prev_states) == (l > 0)
    t_rows = T_LAT if is_lat else T_CTX
    TQ = TQ_LAT if is_lat else TQ_CTX
    n_tiles = t_rows // TQ
    if is_lat:
        mod_map = lambda i: (l, 1 + i // (LAT_LEN // TQ), 0, 0)
    else:
        mod_map = lambda i: (l, 0, 0, 0)
    vec = lw["vec"]
    gain = lambda name: (vec, _vec_spec(l, name))
    whole = lambda a: (a, _full(a.shape))
    layer = lambda a: (a, _layer_spec(a, l))
    pairs = [(x_src, pl.BlockSpec((TQ, D), lambda i: (i + x_off, 0))),
             (mod, pl.BlockSpec((None, None, 1, 6 * D), mod_map)),
             gain("n1g"), layer(lw["w_in"]), whole(consts["seg64"]), whole(consts["seg96"]),
             gain("aqg"), gain("akg"), gain("cqg"), gain("ckg"), gain("bcqg"), gain("bckvg"), gain("bqg"), gain("bkg"),
             layer(lw["w_uq"]), layer(lw["w_uk"]), layer(lw["w_uv"]), whole(consts["place"])]
    ins = [a for a, _ in pairs]
    in_specs = [s for _, s in pairs]
    n_plain = len(ins)
    ins += list(prev_states)
    in_specs += [pl.BlockSpec(memory_space=pl.ANY) for _ in prev_states]
    if is_lat:
        tabs = [consts["cos64"], consts["sin64"], consts["cos96"], consts["sin96"]]
        ins += tabs + [consts["first64"], consts["first96"]]
        in_specs += [pl.BlockSpec((TQ, a.shape[1]), lambda i: (i % (LAT_LEN // TQ), 0)) for a in tabs]
        in_specs += [_full(consts["first64"].shape), _full(consts["first96"].shape)]
    widths = [4 * HD, 2 * HD, 2 * HD, 4 * BP, 4 * BP, 4 * B_V, 4 * HD, 2 * HD, 2 * HD]
    out_shape = [jax.ShapeDtypeStruct((t_rows, w), BF16) for w in widths]
    out_specs = [pl.BlockSpec((TQ, w), lambda i: (i, 0)) for w in widths]
    if not is_lat:
        for tail in STATE_TAILS:
            zeros = (0,) * len(tail)
            out_shape.append(jax.ShapeDtypeStruct((N_CTX_SEQ, DEPTH, CTX_LEN) + tail, F32))
            if prev_states:
                out_specs.append(pl.BlockSpec((SEQ_PER_STEP, None, CTX_LEN) + tail, lambda i, z=zeros: (i, l, 0) + z))
            else:
                out_specs.append(pl.BlockSpec((SEQ_PER_STEP, DEPTH, CTX_LEN) + tail, lambda i, z=zeros: (i, 0, 0) + z))
    aliases = {n_plain + k: len(widths) + k for k in range(len(prev_states))}
    return pl.pallas_call(
        functools.partial(_proj_kernel, is_lat, len(prev_states)),
        grid=(n_tiles,),
        in_specs=in_specs,
        out_specs=out_specs,
        out_shape=out_shape,
        input_output_aliases=aliases,
        compiler_params=pltpu.CompilerParams(
            dimension_semantics=("arbitrary",), vmem_limit_bytes=VMEM_LIMIT),
        name="proj_lat" if is_lat else "proj_ctx",
    )(*ins)


def _cache_kv_kernel(ckv_ref, kr_ref, seg96_ref, bkg_ref, wuk_ref, wuv_ref, plc_ref, kb_ref, vb_ref):
    ckv16 = ckv_ref[...].astype(BF16)
    vb_ref[...] = _dot(ckv16, wuv_ref[...]).astype(BF16)
    kpre = _dot(ckv16, wuk_ref[...]) + _dot(kr_ref[...].astype(BF16), plc_ref[0:B_ROPE, :])
    kb_ref[...] = _seg_norm(kpre, seg96_ref[...], bkg_ref[...], B_QK).astype(BF16)


def _cache_kv(cache_b_ckv, cache_b_krope, lw, consts):
    def c_map(l, r):
        return (r // (PAST // TM), l, r % (PAST // TM), 0)

    def w_spec(a):
        return pl.BlockSpec((None,) + a.shape[1:], lambda l, r: (l,) + (0,) * (a.ndim - 1))

    def s_spec(a):
        return pl.BlockSpec(a.shape, lambda l, r: (0,) * a.ndim)

    rows = N_LAT_SEQ * PAST
    return pl.pallas_call(
        _cache_kv_kernel,
        grid=(DEPTH, rows // TM),
        in_specs=[pl.BlockSpec((None, None, TM, B_KV_LORA), c_map), pl.BlockSpec((None, None, TM, B_ROPE), c_map),
                  s_spec(consts["seg96"]),
                  pl.BlockSpec((None, None, 1, 4 * BP), lambda l, r: (l, VEC_ROWS["bkg"][0], 0, 0)),
                  w_spec(lw["w_uk"]), w_spec(lw["w_uv"]),
                  s_spec(consts["place"])],
        out_specs=[pl.BlockSpec((None, TM, 4 * BP), lambda l, r: (l, r, 0)),
                   pl.BlockSpec((None, TM, 4 * B_V), lambda l, r: (l, r, 0))],
        out_shape=[jax.ShapeDtypeStruct((DEPTH, rows, 4 * BP), BF16),
                   jax.ShapeDtypeStruct((DEPTH, rows, 4 * B_V), BF16)],
        compiler_params=pltpu.CompilerParams(
            dimension_semantics=("arbitrary", "arbitrary"), vmem_limit_bytes=VMEM_LIMIT),
        name="cache_kv",
    )(cache_b_ckv, cache_b_krope, consts["seg96"], lw["vec"], lw["w_uk"], lw["w_uv"], consts["place"])


def _scores(q, segs):
    scores = []
    for k, _, mask in segs:
        s = _dot_t(q, k)
        if mask is not None:
            s = jnp.where(mask, s, NEG_INF)
        scores.append(s)
    return scores


def _softmax_pv(scores, segs, sink, num_cols, den_col):
    m = None
    for s in scores:
        sm = jnp.max(s, axis=-1, keepdims=True)
        m = sm if m is None else jnp.maximum(m, sm)
    if sink is not None:
        m = jnp.maximum(m, sink)
    acc = None
    den = None
    for s, (_, v, _) in zip(scores, segs):
        e = jnp.exp2(s - m)
        if den_col is None:
            d = jnp.sum(e, axis=-1, keepdims=True)
            den = d if den is None else den + d
        o = _dot(e.astype(BF16), v)
        acc = o if acc is None else acc + o
    if den_col is not None:
        den = acc[:, den_col:den_col + 1]
        acc = acc[:, num_cols]
    if sink is not None:
        den = den + jnp.exp2(sink - m)
    return acc / den


HEAD_LOOKAHEAD = 1


def _run_heads(jobs, o_ref):
    ready = [_scores(job[0](), job[1]()) for job in jobs[:HEAD_LOOKAHEAD]]
    for n, (_, load_segs, sink, col, num_cols, den_col) in enumerate(jobs):
        scores = ready.pop(0)
        if n + HEAD_LOOKAHEAD < len(jobs):
            ahead = jobs[n + HEAD_LOOKAHEAD]
            ready.append(_scores(ahead[0](), ahead[1]()))
        o = _softmax_pv(scores, load_segs(values=True), sink, num_cols, den_col)
        o_ref[:, col:col + o.shape[1]] = o.astype(BF16)


OUT_B = 4 * HD
OUT_C = 4 * HD + 4 * B_V
LOG2E = 1.4426950408889634


def _head_job(q_ref, q_cols, seg_refs, kv_cols, v_cols, sink, col, rows=None, masks=None, mxu_sums=True):
    width = v_cols.stop - v_cols.start
    if not mxu_sums:
        load_cols, num_cols, den_col = v_cols, None, None
    elif width == HD:
        pair = v_cols.start // LANE * LANE
        load_cols = slice(pair, pair + LANE)
        lo = v_cols.start - pair
        num_cols, den_col = slice(lo, lo + HD), (lo + HD) % LANE
    else:
        load_cols, num_cols, den_col = v_cols, slice(0, width), width
    if sink is not None:
        sink = sink * LOG2E

    def with_ones(v):
        if not mxu_sums:
            return v
        if width != HD:
            return jnp.concatenate([v, jnp.ones_like(v)], axis=1)
        lane = lax.broadcasted_iota(jnp.int32, (1, LANE), 1)
        keep = jnp.where((lane >= num_cols.start) & (lane < num_cols.stop), 1.0, 0.0).astype(BF16)
        return v * keep + (1.0 - keep)

    def load_segs(values=False):
        segs = []
        for n, (k_ref, v_ref) in enumerate(seg_refs):
            r = rows if (rows is not None and n == 0) else slice(None)
            if values:
                segs.append((None, with_ones(v_ref[r, load_cols]), None))
            else:
                segs.append((k_ref[r, kv_cols], None, None if masks is None else masks[n]))
        return segs

    return (lambda: q_ref[:, q_cols]), load_segs, sink, col, num_cols, den_col


def _attn_ctx_kernel(l, sink_ref, qa_ref, ka_ref, va_ref, qb_ref, kb_ref, vb_ref, qc_ref, kc_ref, vc_ref, o_ref):
    jobs = []
    for h in range(4):
        cs = slice(h // 2 * HD, (h // 2 + 1) * HD)
        jobs.append(_head_job(qa_ref, slice(h * HD, (h + 1) * HD), [(ka_ref, va_ref)], cs, cs, sink_ref[l, h],
                              h * HD, mxu_sums=False))
    for h in range(4):
        ks, vs = slice(h * BP, (h + 1) * BP), slice(h * B_V, (h + 1) * B_V)
        jobs.append(_head_job(qb_ref, ks, [(kb_ref, vb_ref)], ks, vs, None, OUT_B + h * B_V, mxu_sums=False))
    for h in range(4):
        cs = slice(h // 2 * HD, (h // 2 + 1) * HD)
        jobs.append(_head_job(qc_ref, slice(h * HD, (h + 1) * HD), [(kc_ref, vc_ref)], cs, cs, None, OUT_C + h * HD,
                              mxu_sums=False))
    _run_heads(jobs, o_ref)


def _attention_ctx(l, sink, p):
    names = ["qa", "ka", "va", "qb", "kb", "vb", "qc", "kc", "vc"]
    ins = [p[n] for n in names]
    in_specs = [pl.BlockSpec(memory_space=pltpu.SMEM)]
    in_specs += [pl.BlockSpec((CTX_LEN, a.shape[1]), lambda i: (i, 0)) for a in ins]
    return pl.pallas_call(
        functools.partial(_attn_ctx_kernel, l),
        grid=(N_CTX_SEQ,),
        in_specs=in_specs,
        out_specs=pl.BlockSpec((CTX_LEN, D), lambda i: (i, 0)),
        out_shape=jax.ShapeDtypeStruct((T_CTX, D), BF16),
        compiler_params=pltpu.CompilerParams(
            dimension_semantics=("arbitrary",), vmem_limit_bytes=VMEM_LIMIT),
        name="attn_ctx",
    )(sink, *ins)


WIN_SPAN = TM + 2 * WINDOW


def _attn_lat_kernel(l, sink_ref, qa_ref, qb_ref, qc_ref, ka_ref, va_ref, kb_ref, vb_ref, kc_ref, vc_ref,
                     cka_ref, cva_ref, ckb_ref, cvb_ref, ckc_ref, cvc_ref, o_ref):
    qi = pl.program_id(1)
    ws = pl.multiple_of(jnp.clip(qi * TM - WINDOW, 0, LAT_LEN - WIN_SPAN), WINDOW)
    qpos = qi * TM + lax.broadcasted_iota(jnp.int32, (TM, WIN_SPAN), 0)
    kpos = ws + lax.broadcasted_iota(jnp.int32, (TM, WIN_SPAN), 1)
    band = jnp.abs(qpos - kpos) <= WINDOW
    jobs = []
    for h in range(4):
        cs = slice(h // 2 * HD, (h // 2 + 1) * HD)
        jobs.append(_head_job(qa_ref, slice(h * HD, (h + 1) * HD), [(ka_ref, va_ref), (cka_ref, cva_ref)], cs, cs,
                              sink_ref[l, h], h * HD, rows=pl.ds(ws, WIN_SPAN), masks=(band, None)))
    for h in range(4):
        ks, vs = slice(h * BP, (h + 1) * BP), slice(h * B_V, (h + 1) * B_V)
        jobs.append(_head_job(qb_ref, ks, [(kb_ref, vb_ref), (ckb_ref, cvb_ref)], ks, vs, None, OUT_B + h * B_V))
    for h in range(4):
        cs = slice(h // 2 * HD, (h // 2 + 1) * HD)
        jobs.append(_head_job(qc_ref, slice(h * HD, (h + 1) * HD), [(kc_ref, vc_ref), (ckc_ref, cvc_ref)], cs, cs,
                              None, OUT_C + h * HD))
    _run_heads(jobs, o_ref)


def _attention_lat(l, sink, p, cache):
    q_ins = [p["qa"], p["qb"], p["qc"]]
    kv_ins = [p[n] for n in ("ka", "va", "kb", "vb", "kc", "vc")]
    c_names = ("ka", "va", "kb", "vb", "kc", "vc")
    c_ins = [cache[n] for n in c_names]
    in_specs = [pl.BlockSpec(memory_space=pltpu.SMEM)]
    in_specs += [pl.BlockSpec((TM, a.shape[1]), lambda b, i: (b * LAT_TILES_PER_SEQ + i, 0)) for a in q_ins]
    in_specs += [pl.BlockSpec((LAT_LEN, a.shape[1]), lambda b, i: (b, 0)) for a in kv_ins]
    for n, a in zip(c_names, c_ins):
        if n in ("kb", "vb"):
            in_specs.append(pl.BlockSpec((None, PAST, a.shape[-1]), lambda b, i: (l, b, 0)))
        else:
            in_specs.append(pl.BlockSpec((None, None, PAST, a.shape[-1]), lambda b, i: (b, l, 0, 0)))
    return pl.pallas_call(
        functools.partial(_attn_lat_kernel, l),
        grid=(N_LAT_SEQ, LAT_TILES_PER_SEQ),
        in_specs=in_specs,
        out_specs=pl.BlockSpec((TM, D), lambda b, i: (b * LAT_TILES_PER_SEQ + i, 0)),
        out_shape=jax.ShapeDtypeStruct((T_LAT, D), BF16),
        compiler_params=pltpu.CompilerParams(
            dimension_semantics=("arbitrary", "arbitrary"), vmem_limit_bytes=VMEM_LIMIT),
        name="attn_lat",
    )(sink, *q_ins, *kv_ins, *c_ins)


def _pack_rows(x):
    half = x.shape[1] // 2
    r = x.astype(BF16).astype(F32)
    hi = lax.bitcast_convert_type(r[:, :half], jnp.int32)
    lo = lax.bitcast_convert_type(r[:, half:], jnp.int32)
    return jnp.bitwise_or(hi, lax.shift_right_logical(lo, 16))


def _unpack_rows(p):
    a = lax.bitcast_convert_type(jnp.bitwise_and(p, -65536), F32)
    b = lax.bitcast_convert_type(lax.shift_left(p, 16), F32)
    return jnp.concatenate([a, b], axis=1).astype(BF16)


ROUTE_ROWS = 16


def _post_kernel(mc_ref, ml_ref, xc_ref, xl_ref, mod_ref, n2g_ref, wout_ref, rwt_ref, rb_ref, tri_ref,
                 xo_ref, h2_ref, gate_ref, route_ref, cnt_ref, run_ref):
    i = pl.program_id(0)

    @pl.when(i == 0)
    def _():
        run_ref[...] = jnp.zeros_like(run_ref)
        cnt_ref[...] = jnp.zeros_like(cnt_ref)

    is_ctx = i < T_CTX // TP
    m = mod_ref[...]
    g1, sh2, sc2 = m[:, 2 * D:3 * D], m[:, 3 * D:4 * D], m[:, 4 * D:5 * D]

    mix = jnp.where(is_ctx, mc_ref[...], ml_ref[...])
    x = jnp.where(is_ctx, xc_ref[...], xl_ref[...]) + g1 * _dot(mix, wout_ref[...])
    xo_ref[...] = x
    h2 = _rms(x, n2g_ref[...]) * (1.0 + sc2) + sh2
    hi = h2.astype(BF16)
    h2_ref[...] = _pack_rows(h2)
    lo = (h2 - hi.astype(F32)).astype(BF16)
    a = _dot_t(rwt_ref[...], hi)
    b = _dot_t(rwt_ref[0:N_EXPERTS, :], lo)
    logits = a[0:N_EXPERTS] + a[N_EXPERTS:] + b + rb_ref[...]
    eidx = lax.broadcasted_iota(jnp.int32, logits.shape, 0).astype(F32)
    work = logits
    vals, idxs, hots = [], [], []
    for _ in range(TOP_K):
        v = jnp.max(work, axis=0, keepdims=True)
        idx = jnp.min(jnp.where(work == v, eidx, float(N_EXPERTS)), axis=0, keepdims=True)
        hot = eidx == idx
        vals.append(v)
        idxs.append(idx)
        hots.append(hot)
        work = jnp.where(hot, -jnp.inf, work)
    es = [jnp.exp(v - vals[0]) for v in vals]
    den = es[0] + es[1] + es[2] + es[3]
    gates = [e / den for e in es]
    sel = jnp.where(hots[0] | hots[1] | hots[2] | hots[3], 1.0, 0.0)
    run = run_ref[:, 0:1]
    before = _dot(sel.astype(BF16), tri_ref[...]) + run
    run_new = jnp.broadcast_to(run + jnp.sum(sel, axis=1, keepdims=True), run_ref.shape)
    run_ref[...] = run_new
    cnt_ref[...] = run_new
    ranks = [jnp.sum(jnp.where(hots[k], before, 0.0), axis=0, keepdims=True) for k in range(TOP_K)]
    pad = [jnp.zeros((ROUTE_ROWS - 3 * TOP_K, TP), F32)]
    route_ref[...] = jnp.concatenate(gates + idxs + ranks + pad, axis=0)
    gate_ref[...] = jnp.concatenate(gates + [jnp.zeros((LANE - TOP_K, TP), F32)], axis=0).T


def _mod_row(i, tm=TM):
    n_ctx = T_CTX // tm
    return jnp.where(i < n_ctx, 0, 1 + (i - n_ctx) // (LAT_LEN // tm))


def _ctx_tile(i):
    return jnp.minimum(i, T_CTX // TP - 1)


def _lat_tile(i):
    return jnp.maximum(i - T_CTX // TP, 0)


def _post_attention(l, mix_ctx, mix_lat, x_ctx, x_lat, x_lat_off, mod, lw, consts):
    ins = [mix_ctx, mix_lat, x_ctx, x_lat, mod, lw["vec"], lw["w_out"], lw["rwt"], lw["rb"], consts["tri"]]
    in_specs = [
        pl.BlockSpec((TP, D), lambda i: (_ctx_tile(i), 0)),
        pl.BlockSpec((TP, D), lambda i: (_lat_tile(i), 0)),
        pl.BlockSpec((TP, D), lambda i: (_ctx_tile(i), 0)),
        pl.BlockSpec((TP, D), lambda i: (_lat_tile(i) + x_lat_off, 0)),
        pl.BlockSpec((None, None, 1, 6 * D), lambda i: (l, _mod_row(i, TP), 0, 0)),
        _vec_spec(l, "n2g"),
    ] + [_layer_spec(a, l) for a in ins[6:9]] + [_full(consts["tri"].shape)]
    return pl.pallas_call(
        _post_kernel,
        grid=(T_ALL // TP,),
        in_specs=in_specs,
        out_specs=[pl.BlockSpec((TP, D), lambda i: (i, 0)), pl.BlockSpec((TP, DW), lambda i: (i, 0)),
                   pl.BlockSpec((TP, LANE), lambda i: (i, 0)), pl.BlockSpec((ROUTE_ROWS, TP), lambda i: (0, i)),
                   pl.BlockSpec((N_EXPERTS, LANE), lambda i: (0, 0))],
        out_shape=[jax.ShapeDtypeStruct((T_ALL, D), F32), jax.ShapeDtypeStruct((T_ALL, DW), jnp.int32),
                   jax.ShapeDtypeStruct((T_ALL, LANE), F32), jax.ShapeDtypeStruct((ROUTE_ROWS, T_ALL), F32),
                   jax.ShapeDtypeStruct((N_EXPERTS, LANE), F32)],
        scratch_shapes=[pltpu.VMEM((N_EXPERTS, LANE), F32)],
        compiler_params=pltpu.CompilerParams(
            dimension_semantics=("arbitrary",), vmem_limit_bytes=VMEM_LIMIT),
        name="post_attn",
    )(*ins)


def _expert_kernel(l, te_ref, nu_ref, nxt_ref, nv_ref, slot_ref, x_ref, wgu_hbm, bgu_ref, wdn_hbm, bdn_ref, y_ref,
                   wgu32, wdn32, wgu16, wdn16, sem):
    i = pl.program_id(0)
    prev = te_ref[jnp.maximum(i - 1, 0)]
    new_expert = jnp.logical_or(i == 0, te_ref[i] != prev)
    slot = slot_ref[i]

    def weight_copies(e, s):
        return (pltpu.make_async_copy(wgu_hbm.at[l, e], wgu32.at[s], sem.at[s, 0]),
                pltpu.make_async_copy(wdn_hbm.at[l, e], wdn32.at[s], sem.at[s, 1]))

    @pl.when(i == 0)
    def _():
        for cp in weight_copies(te_ref[0], slot):
            cp.start()

    @pl.when(new_expert)
    def _():
        for cp in weight_copies(te_ref[i], slot):
            cp.wait()

        @pl.when(nxt_ref[i] >= 0)
        def _():
            for cp in weight_copies(nxt_ref[i], 1 - slot):
                cp.start()

    def compute(rows, convert):
        x = _unpack_rows(x_ref[0:rows, :])

        def gate_up(a):
            if convert:
                for c in (slice(a, a + FF_CHUNK), slice(D_FF + a, D_FF + a + FF_CHUNK)):
                    wgu16[:, c] = wgu32[slot, :, c].astype(BF16)
                wdn16[a:a + FF_CHUNK, :] = wdn32[slot, a:a + FF_CHUNK, :].astype(BF16)
            b = a + FF_CHUNK
            g = _dot(x, wgu16[:, a:b]) + bgu_ref[:, a:b]
            up = _dot(x, wgu16[:, D_FF + a:D_FF + b]) + bgu_ref[:, D_FF + a:D_FF + b]
            return g, up

        def activate(g, up):
            g = jnp.minimum(g, SWIGLU_LIMIT)
            up = jnp.clip(up, -SWIGLU_LIMIT, SWIGLU_LIMIT)
            return ((up + 1.0) * (g * jax.nn.sigmoid(SWIGLU_ALPHA * g))).astype(BF16)

        starts = list(range(0, D_FF, FF_CHUNK))
        acc = None
        pre = gate_up(starts[0])
        for n, a in enumerate(starts):
            hid = activate(*pre)
            if n + 1 < len(starts):
                pre = gate_up(starts[n + 1])
            o = _dot(hid, wdn16[a:a + FF_CHUNK, :])
            acc = o if acc is None else acc + o
        y_ref[0:rows, :] = _pack_rows(acc + bdn_ref[...])

    used = i < nu_ref[0]
    for n_valid in (1, 2):
        for convert in (False, True):
            @pl.when(used & (nv_ref[i] == n_valid) & (new_expert == convert))
            def _():
                compute(n_valid * MOE_TM, convert)


def _experts(l, tile_expert, n_used, next_expert, n_valid, slot, x_sorted, w_gu, b_gu, w_dn, b_dn):
    def row_map(i, te, nu, nxt, nv, sl):
        return (jnp.minimum(i, nu[0] - 1), 0)

    def b_map(i, te, nu, nxt, nv, sl):
        return (l, te[i], 0, 0)

    grid_spec = pltpu.PrefetchScalarGridSpec(
        num_scalar_prefetch=5,
        grid=(MOE_STEPS,),
        in_specs=[
            pl.BlockSpec((MOE_STEP, DW), row_map),
            pl.BlockSpec(memory_space=pl.ANY),
            pl.BlockSpec((None, None, 1, 2 * D_FF), b_map),
            pl.BlockSpec(memory_space=pl.ANY),
            pl.BlockSpec((None, None, 1, D), b_map),
        ],
        out_specs=pl.BlockSpec((MOE_STEP, DW), row_map),
        scratch_shapes=[pltpu.VMEM((2, D, 2 * D_FF), F32), pltpu.VMEM((2, D_FF, D), F32),
                        pltpu.VMEM((D, 2 * D_FF), BF16), pltpu.VMEM((D_FF, D), BF16),
                        pltpu.SemaphoreType.DMA((2, 2))],
    )
    return pl.pallas_call(
        functools.partial(_expert_kernel, l),
        grid_spec=grid_spec,
        out_shape=jax.ShapeDtypeStruct((MOE_ROWS, DW), jnp.int32),
        compiler_params=pltpu.CompilerParams(
            dimension_semantics=("arbitrary",), vmem_limit_bytes=VMEM_LIMIT),
        name="experts",
    )(tile_expert, n_used, next_expert, n_valid, slot, x_sorted, w_gu, b_gu.reshape(DEPTH, N_EXPERTS, 1, 2 * D_FF),
      w_dn, b_dn.reshape(DEPTH, N_EXPERTS, 1, D))


def _combine_kernel(x_ref, y_ref, route_ref, mod_ref, o_ref):
    g2 = mod_ref[...][:, 5 * D:6 * D]
    r = route_ref[...]
    acc = r[:, 0:1] * _unpack_rows(y_ref[0]).astype(F32)
    for k in range(1, TOP_K):
        acc = acc + r[:, k:k + 1] * _unpack_rows(y_ref[k]).astype(F32)
    o_ref[...] = x_ref[...] + g2 * acc


def _combine(l, first_tile, n_tiles, x_all, y_tok, gates, mod):
    return pl.pallas_call(
        _combine_kernel,
        grid=(n_tiles,),
        in_specs=[
            pl.BlockSpec((TM, D), lambda i: (i + first_tile, 0)),
            pl.BlockSpec((TOP_K, TM, DW), lambda i: (0, i, 0)),
            pl.BlockSpec((TM, LANE), lambda i: (i + first_tile, 0)),
            pl.BlockSpec((None, None, 1, 6 * D), lambda i: (l, _mod_row(i + first_tile), 0, 0)),
        ],
        out_specs=pl.BlockSpec((TM, D), lambda i: (i, 0)),
        out_shape=jax.ShapeDtypeStruct((n_tiles * TM, D), F32),
        compiler_params=pltpu.CompilerParams(
            dimension_semantics=("arbitrary",), vmem_limit_bytes=VMEM_LIMIT),
        name="combine",
    )(x_all, y_tok, gates, mod)


SC_CORES, SC_SUBCORES = 2, 16
SC_WORKERS = SC_CORES * SC_SUBCORES
SC_CHUNK = 128
SC_CHUNKS_PER_WORKER = T_ALL // SC_CHUNK // SC_WORKERS


def _sc_mesh():
    return plsc.VectorSubcoreMesh(core_axis_name="c", subcore_axis_name="s")


def _sc_scratch():
    return [pltpu.VMEM((TOP_K, SC_CHUNK), jnp.int32), pltpu.VMEM((SC_CHUNK, DW), jnp.int32),
            pltpu.SemaphoreType.DMA]


def _dispatch_rows(h2p, dest):
    @functools.partial(pl.kernel, mesh=_sc_mesh(), out_type=jax.ShapeDtypeStruct((MOE_ROWS, DW), jnp.int32),
                       scratch_types=_sc_scratch(), name="dispatch_rows")
    def run(h_hbm, d_hbm, o_hbm, idx_v, rows_v, sem):
        wid = lax.axis_index("s") * SC_CORES + lax.axis_index("c")
        for j in range(SC_CHUNKS_PER_WORKER):
            c = wid * SC_CHUNKS_PER_WORKER + j
            pltpu.sync_copy(d_hbm.at[c], idx_v)
            pltpu.sync_copy(h_hbm.at[pl.ds(c * SC_CHUNK, SC_CHUNK)], rows_v)
            copies = [pltpu.async_copy(rows_v, o_hbm.at[idx_v.at[k]], sem) for k in range(TOP_K)]
            for cp in copies:
                cp.wait()

    return run(h2p, dest)


def _gather_rows(y, dest, first_chunk, n_chunks, *after):
    per_worker = n_chunks // SC_WORKERS
    assert per_worker * SC_WORKERS == n_chunks

    @functools.partial(pl.kernel, mesh=_sc_mesh(),
                       out_type=jax.ShapeDtypeStruct((TOP_K, n_chunks * SC_CHUNK, DW), jnp.int32),
                       scratch_types=_sc_scratch(), name="gather_rows")
    def run(y_hbm, d_hbm, *rest):
        o_hbm, idx_v, rows_v, sem = rest[len(after):]
        wid = lax.axis_index("s") * SC_CORES + lax.axis_index("c")
        for j in range(per_worker):
            c = wid * per_worker + j
            pltpu.sync_copy(d_hbm.at[first_chunk + c], idx_v)
            for k in range(TOP_K):
                pltpu.async_copy(y_hbm.at[idx_v.at[k]], rows_v, sem).wait()
                pltpu.sync_copy(rows_v, o_hbm.at[k, pl.ds(c * SC_CHUNK, SC_CHUNK)])

    return run(y, dest, *after)


def _constants():
    lane64 = np.arange(4 * HD)
    seg64 = (lane64[:, None] // HD == lane64[None, :] // HD).astype(np.float32)
    lane96 = np.arange(4 * BP)
    real = lane96 % BP < B_QK
    seg96 = ((lane96[:, None] // BP == lane96[None, :] // BP) & real[:, None] & real[None, :]).astype(np.float32)
    place = np.zeros((LANE, 4 * BP), np.float32)
    for hh in range(4):
        place[np.arange(B_ROPE), hh * BP + B_NOPE + np.arange(B_ROPE)] = 1.0

    def angles(rot_dim):
        pos = np.arange(LAT_LEN)
        rows = (pos // GRID_W).astype(np.float32)
        cols = (pos % GRID_W).astype(np.float32)
        axis_dim = rot_dim // 2
        inv = np.power(np.float32(ROPE_THETA), -(np.arange(0, axis_dim, 2, dtype=np.float32) / np.float32(axis_dim)))
        ang = np.concatenate([rows[:, None] * inv, cols[:, None] * inv], axis=-1).astype(np.float32)
        return np.cos(ang), np.sin(ang)

    def head_tables(rot_dim):
        cos, sin = angles(rot_dim)
        q = rot_dim // 4
        cr, cc, sr, sc = cos[:, :q], cos[:, q:], sin[:, :q], sin[:, q:]
        return (np.concatenate([cr, cr, cc, cc], axis=-1), np.concatenate([-sr, sr, -sc, sc], axis=-1))

    c64, s64 = head_tables(HD)
    cos64 = np.tile(c64, (1, 4))
    sin64 = np.tile(s64, (1, 4))
    c32, s32 = head_tables(B_ROPE)
    ones = np.ones((LAT_LEN, B_NOPE), np.float32)
    zeros = np.zeros((LAT_LEN, B_NOPE), np.float32)
    padz = np.zeros((LAT_LEN, BP - B_QK), np.float32)
    cos96 = np.tile(np.concatenate([ones, c32, padz], axis=-1), (1, 4))
    sin96 = np.tile(np.concatenate([zeros, s32, padz], axis=-1), (1, 4))
    first64 = ((lane64 % 32) < 16).astype(np.float32)[None, :]
    first96 = (((lane96 % BP) % 16) < 8).astype(np.float32)[None, :]
    tri = (np.arange(TP)[:, None] < np.arange(TP)[None, :]).astype(np.float32)
    f32 = lambda a: jnp.asarray(a, F32)
    return dict(seg64=jnp.asarray(seg64, BF16), seg96=jnp.asarray(seg96, BF16), place=jnp.asarray(place, BF16),
                tri=jnp.asarray(tri, BF16),
                cos64=f32(cos64), sin64=f32(sin64), cos96=f32(cos96), sin96=f32(sin96),
                first64=f32(first64), first96=f32(first96))


def _pad_heads(w, per_head, width):
    lead = w.shape[:-1]
    w = w.reshape(lead + (4, per_head))
    return jnp.pad(w, ((0, 0),) * (len(lead) + 1) + ((0, width - per_head),)).reshape(lead + (4 * width,))


def _weights(norm1_g, norm2_g, w_in, a_q_g, a_k_g, b_cq_g, b_ckv_g, w_uq, w_ukv, b_q_g, b_k_g,
             c_q_g, c_k_g, w_out, router_w, router_b):
    o = np.cumsum((0, 256, 128, 128, 384, 256, 32, 256, 128, 128))
    seg = lambda k: w_in[:, :, o[k]:o[k + 1]]
    w_in_r = jnp.concatenate([seg(0), seg(1), seg(2), seg(3), seg(4), seg(6), seg(7), seg(8), seg(5),
                              jnp.zeros((DEPTH, D, LANE - B_ROPE), F32)], axis=-1).astype(BF16)
    ukv = w_ukv.reshape(DEPTH, B_KV_LORA, 4, B_NOPE + B_V)
    w_uk = _pad_heads(ukv[..., :B_NOPE].reshape(DEPTH, B_KV_LORA, 4 * B_NOPE), B_NOPE, BP)
    w_uv = ukv[..., B_NOPE:].reshape(DEPTH, B_KV_LORA, 4 * B_V)
    rw_hi = router_w.astype(BF16)
    rw_lo = (router_w - rw_hi.astype(F32)).astype(BF16)
    tile = lambda v, n: jnp.tile(v, (1, n))
    rows = dict(n1g=norm1_g, n2g=norm2_g, aqg=tile(a_q_g, 4), akg=tile(a_k_g, 2), cqg=tile(c_q_g, 4),
                ckg=tile(c_k_g, 2), bcqg=b_cq_g, bckvg=b_ckv_g,
                bqg=_pad_heads(tile(b_q_g, 4), B_QK, BP), bkg=_pad_heads(tile(b_k_g, 4), B_QK, BP))
    order = sorted(VEC_ROWS, key=lambda n: VEC_ROWS[n][0])
    assert all(rows[n].shape == (DEPTH, VEC_ROWS[n][1]) for n in order)
    vec = jnp.stack([jnp.pad(rows[n], ((0, 0), (0, D - rows[n].shape[1]))) for n in order], axis=1)
    return dict(
        vec=vec[:, :, None, :], w_in=w_in_r,
        w_uq=_pad_heads(w_uq, B_QK, BP).astype(BF16), w_uk=w_uk.astype(BF16), w_uv=w_uv.astype(BF16),
        w_out=w_out.astype(BF16),
        rwt=jnp.concatenate([jnp.swapaxes(rw_hi, 1, 2), jnp.swapaxes(rw_lo, 1, 2)], axis=1),
        rb=router_b[:, :, None])


def _moe(l, x_all, h2p, gate_slab, route, counts, mod, w_gu, b_gu, w_dn, b_dn):
    counts = counts[:, 0].astype(jnp.int32)
    padded = (counts + MOE_STEP - 1) // MOE_STEP * MOE_STEP
    pend = jnp.cumsum(padded)
    pstart = pend - padded
    n_used = (pend[-1] // MOE_STEP).astype(jnp.int32)
    steps = jnp.minimum(jnp.arange(MOE_STEPS, dtype=jnp.int32), n_used - 1)
    tile_expert = jnp.sum((pend[None, :] <= steps[:, None] * MOE_STEP).astype(jnp.int32), axis=1)
    tile_expert = jnp.minimum(tile_expert, N_EXPERTS - 1)
    of_expert = lambda table: jnp.sum(
        jnp.where(tile_expert[:, None] == jnp.arange(N_EXPERTS)[None, :], table[None, :], 0), axis=1)
    group_end_step = of_expert(pend) // MOE_STEP
    after = jnp.sum((pend[None, :] <= group_end_step[:, None] * MOE_STEP).astype(jnp.int32), axis=1)
    next_expert = jnp.where(group_end_step < n_used, jnp.minimum(after, N_EXPERTS - 1), -1).astype(jnp.int32)
    rows_left = of_expert(pstart + counts) - steps * MOE_STEP
    n_valid = jnp.clip((rows_left + MOE_TM - 1) // MOE_TM, 1, MOE_STEP // MOE_TM).astype(jnp.int32)
    e = route[TOP_K:2 * TOP_K].astype(jnp.int32)
    rank = route[2 * TOP_K:3 * TOP_K].astype(jnp.int32)
    start = jnp.sum(jnp.where(e[:, :, None] == jnp.arange(N_EXPERTS)[None, None, :], pstart[None, None, :], 0), axis=-1)
    dest = (start + rank).reshape(TOP_K, T_ALL // SC_CHUNK, SC_CHUNK).transpose(1, 0, 2)
    x_sorted = _dispatch_rows(h2p, dest)
    changed = jnp.concatenate([jnp.zeros((1,), jnp.int32), (tile_expert[1:] != tile_expert[:-1]).astype(jnp.int32)])
    slot = jnp.cumsum(changed) % 2
    y = _experts(l, tile_expert, n_used.reshape(1), next_expert, n_valid, slot.astype(jnp.int32), x_sorted,
                 w_gu, b_gu, w_dn, b_dn)
    ctx_chunks, lat_chunks = T_CTX // SC_CHUNK, T_LAT // SC_CHUNK
    y_ctx = _gather_rows(y, dest, 0, ctx_chunks)
    out_ctx = _combine(l, 0, N_CTX_TILES, x_all, y_ctx, gate_slab, mod)
    y_lat = _gather_rows(y, dest, ctx_chunks, lat_chunks, out_ctx)
    return out_ctx, _combine(l, N_CTX_TILES, N_LAT_TILES, x_all, y_lat, gate_slab, mod)


def kernel(x_prompt, x_sample, cache_a_k, cache_a_v, cache_b_ckv, cache_b_krope, cache_c_k, cache_c_v, c, c_ctx,
           norm1_g, norm2_g, w_mod, b_mod, w_in, a_q_g, a_k_g, a_sink, b_cq_g, b_ckv_g, w_uq, w_ukv, b_q_g, b_k_g,
           c_q_g, c_k_g, w_out, router_w, router_b, w_gu, b_gu, w_dn, b_dn):
    consts = _constants()
    cond = jnp.concatenate([c_ctx[None, :], c, jnp.zeros((3, D), F32)], axis=0)
    mod = _modulation(cond, w_mod, b_mod).reshape(DEPTH, 8, 1, 6 * D)
    x_ctx, x_lat, x_lat_off = x_prompt.reshape(T_CTX, D), x_sample.reshape(T_LAT, D), 0

    lw = _weights(norm1_g, norm2_g, w_in, a_q_g, a_k_g, b_cq_g, b_ckv_g, w_uq, w_ukv, b_q_g, b_k_g,
                  c_q_g, c_k_g, w_out, router_w, router_b)
    ckb, cvb = _cache_kv(cache_b_ckv, cache_b_krope, lw, consts)
    merge_heads = lambda a: a.reshape(N_LAT_SEQ, DEPTH, PAST, 2 * HD).astype(BF16)
    cache = dict(ka=merge_heads(cache_a_k), va=merge_heads(cache_a_v), kb=ckb, vb=cvb,
                 kc=merge_heads(cache_c_k), vc=merge_heads(cache_c_v))

    states = ()
    names = ["qa", "ka", "va", "qb", "kb", "vb", "qc", "kc", "vc"]
    for l in range(DEPTH):
        outs = _projection(False, l, x_ctx, 0, mod, lw, consts, states)
        p_ctx = dict(zip(names, outs[:9]))
        states = tuple(outs[9:])
        p_lat = dict(zip(names, _projection(True, l, x_lat, x_lat_off, mod, lw, consts)))
        mix_ctx = _attention_ctx(l, a_sink, p_ctx)
        mix_lat = _attention_lat(l, a_sink, p_lat, cache)
        x_mid, h2p, gate_slab, route, counts = _post_attention(l, mix_ctx, mix_lat, x_ctx, x_lat, x_lat_off, mod, lw,
                                                               consts)
        x_ctx, x_lat = _moe(l, x_mid, h2p, gate_slab, route, counts, mod, w_gu, b_gu, w_dn, b_dn)
    y_ctx, y_lat = x_ctx, x_lat

    return (y_ctx.reshape(N_CTX_SEQ, CTX_LEN, D), y_lat.reshape(N_LAT_SEQ, LAT_LEN, D)) + states
```

```python
import functools

import jax
import jax.numpy as jnp
import numpy as np
from jax import lax
from jax.experimental import pallas as pl
from jax.experimental.pallas import tpu as pltpu
from jax.experimental.pallas import tpu_sc as plsc

F32 = jnp.float32
BF16 = jnp.bfloat16

D = 1024
DEPTH = 2
N_CTX_SEQ, CTX_LEN = 16, 256
N_LAT_SEQ, LAT_LEN = 4, 2048
PAST = 512
T_CTX = N_CTX_SEQ * CTX_LEN
T_LAT = N_LAT_SEQ * LAT_LEN
T_ALL = T_CTX + T_LAT
GRID_W = 64
HD = 64
WINDOW = 128
B_NOPE, B_ROPE, B_V = 64, 32, 128
B_QK = B_NOPE + B_ROPE
B_Q_LORA, B_KV_LORA = 384, 256
N_EXPERTS, TOP_K = 32, 4
D_FF = 1024
SWIGLU_LIMIT = 7.0
SWIGLU_ALPHA = 1.702
ROPE_THETA = 10000.0
EPS = 1e-6
NEG_INF = -1e30

TM = 256
TP = 512
TQ_LAT = 512
TQ_CTX = 256
SEQ_PER_STEP = TQ_CTX // CTX_LEN
LANE = 128
BP = 128
N_CTX_TILES = T_CTX // TM
N_LAT_TILES = T_LAT // TM
N_TILES = T_ALL // TM
LAT_TILES_PER_SEQ = LAT_LEN // TM
MOE_TM = 256
MOE_STEP = 4 * MOE_TM
FF_CHUNK = 256
MOE_STEPS = T_ALL * TOP_K // MOE_STEP + N_EXPERTS
MOE_ROWS = MOE_STEPS * MOE_STEP
DW = D // 2
VMEM_LIMIT = 56 * 1024 * 1024

C_QA, C_KA, C_VA, C_CQ, C_CKV, C_QC, C_KC, C_VC, C_KR, C_END = (
    0, 256, 384, 512, 896, 1152, 1408, 1536, 1664, 1792)


def _dot(a, b):
    return jnp.dot(a, b, preferred_element_type=F32)


def _dot_t(a, b):
    return lax.dot_general(a, b, (((1,), (1,)), ((), ())), preferred_element_type=F32)


def _rms(x, g):
    return x * lax.rsqrt(jnp.mean(x * x, axis=-1, keepdims=True) + EPS) * g


def _seg_norm(x, seg, g, n):
    ss = _dot((x * x).astype(BF16), seg)
    return x * lax.rsqrt(ss * (1.0 / n) + EPS) * g


def _rope(x, cos, sin, first, sh):
    w = x.shape[1]
    fwd = pltpu.roll(x, w - sh, 1)
    bwd = pltpu.roll(x, sh, 1)
    return x * cos + jnp.where(first > 0.5, fwd, bwd) * sin


MOD_BN = 1536


def _mod_kernel(c_ref, w_ref, b_ref, o_ref):
    c = c_ref[...]
    s = (c * jax.nn.sigmoid(c)).astype(BF16)
    o_ref[...] = _dot(s, w_ref[...].astype(BF16)) + b_ref[...]


def _modulation(cond, w_mod, b_mod):
    return pl.pallas_call(
        _mod_kernel,
        grid=(DEPTH, 6 * D // MOD_BN),
        in_specs=[
            pl.BlockSpec((8, D), lambda l, j: (0, 0)),
            pl.BlockSpec((None, D, MOD_BN), lambda l, j: (l, 0, j)),
            pl.BlockSpec((None, 1, MOD_BN), lambda l, j: (l, 0, j)),
        ],
        out_specs=pl.BlockSpec((None, 8, MOD_BN), lambda l, j: (l, 0, j)),
        out_shape=jax.ShapeDtypeStruct((DEPTH, 8, 6 * D), F32),
        compiler_params=pltpu.CompilerParams(
            dimension_semantics=("arbitrary", "arbitrary"), vmem_limit_bytes=VMEM_LIMIT),
        name="modulation",
    )(cond, w_mod, b_mod.reshape(DEPTH, 1, 6 * D))


def _proj_kernel(is_lat, n_aliased, *refs):
    (x_ref, mod_ref, n1g_ref, win_ref, seg64_ref, seg96_ref, aqg_ref, akg_ref, cqg_ref, ckg_ref,
     bcqg_ref, bckvg_ref, bqg_ref, bkg_ref, wuq_ref, wuk_ref, wuv_ref, plc_ref) = refs[:18]
    refs = refs[18 + n_aliased:]
    if is_lat:
        (cos64_ref, sin64_ref, cos96_ref, sin96_ref, f64_ref, f96_ref) = refs[:6]
        refs = refs[6:]
    (qa_ref, ka_ref, va_ref, qb_ref, kb_ref, vb_ref, qc_ref, kc_ref, vc_ref) = refs[:9]
    refs = refs[9:]
    if not is_lat:
        if n_aliased == 0:
            for r in refs:
                r[:, 1:] = jnp.zeros((SEQ_PER_STEP, DEPTH - 1) + r.shape[2:], F32)
            refs = [r.at[:, 0] for r in refs]
        (kas_ref, vas_ref, ckvs_ref, krs_ref, kcs_ref, vcs_ref) = refs

    x = x_ref[...]
    m = mod_ref[...]
    sh1, sc1 = m[:, 0:D], m[:, D:2 * D]
    h = (_rms(x, n1g_ref[...]) * (1.0 + sc1) + sh1).astype(BF16)

    def proj(a, b):
        return _dot(h, win_ref[:, a:b])

    def rope64(t):
        wd = t.shape[1]
        return _rope(t, cos64_ref[:, :wd], sin64_ref[:, :wd], f64_ref[:, :wd], 16) if is_lat else t

    def rope96(t):
        return _rope(t, cos96_ref[...], sin96_ref[...], f96_ref[...], 8) if is_lat else t

    seg64 = seg64_ref[...]
    seg64h = seg64_ref[0:2 * HD, 0:2 * HD]
    seg96 = seg96_ref[...]

    def seg_sum(t, seg):
        return _dot((t * t).astype(BF16), seg)

    def seg_finish(t, ss, g, n):
        return t * lax.rsqrt(ss * (1.0 / n) + EPS) * g

    p_cq, p_ckv, p_kr = proj(C_CQ, C_CKV), proj(C_CKV, C_QC), proj(C_KR, C_END)
    p_qa, p_ka, p_qc, p_kc = proj(C_QA, C_KA), proj(C_KA, C_VA), proj(C_QC, C_KC), proj(C_KC, C_VC)
    p_va, p_vc = proj(C_VA, C_CQ), proj(C_VC, C_KR)

    cq = _rms(p_cq, bcqg_ref[...]).astype(BF16)
    ckv = _rms(p_ckv, bckvg_ref[...])
    ckv16 = ckv.astype(BF16)
    u_q = _dot(cq, wuq_ref[...])
    u_k = _dot(ckv16, wuk_ref[...]) + _dot(p_kr.astype(BF16), plc_ref[...])
    u_v = _dot(ckv16, wuv_ref[...])
    ss_qa, ss_ka = seg_sum(p_qa, seg64), seg_sum(p_ka, seg64h)
    ss_qc, ss_kc = seg_sum(p_qc, seg64), seg_sum(p_kc, seg64h)
    ss_qb, ss_kb = seg_sum(u_q, seg96), seg_sum(u_k, seg96)

    def store_state(ref, t):
        for s in range(SEQ_PER_STEP):
            ref[s] = t[s * CTX_LEN:(s + 1) * CTX_LEN]

    def store_kv_state(ref, t):
        for s in range(SEQ_PER_STEP):
            for kv in range(2):
                ref[s, :, kv, :] = t[s * CTX_LEN:(s + 1) * CTX_LEN, kv * HD:(kv + 1) * HD]

    if not is_lat:
        store_kv_state(vas_ref, p_va)
        store_kv_state(vcs_ref, p_vc)
        store_state(ckvs_ref, ckv)
        store_state(krs_ref, p_kr[:, 0:B_ROPE])
    va_ref[...] = p_va.astype(BF16)
    vc_ref[...] = p_vc.astype(BF16)
    vb_ref[...] = u_v.astype(BF16)

    t = seg_finish(p_qa, ss_qa, aqg_ref[...], HD)
    qa_ref[...] = (rope64(t) * (HD ** -0.5 * LOG2E)).astype(BF16)
    t = seg_finish(p_ka, ss_ka, akg_ref[...], HD)
    if not is_lat:
        store_kv_state(kas_ref, t)
    ka_ref[...] = rope64(t).astype(BF16)
    t = seg_finish(p_qc, ss_qc, cqg_ref[...], HD)
    qc_ref[...] = (rope64(t) * (HD ** -0.5 * LOG2E)).astype(BF16)
    t = seg_finish(p_kc, ss_kc, ckg_ref[...], HD)
    if not is_lat:
        store_kv_state(kcs_ref, t)
    kc_ref[...] = rope64(t).astype(BF16)
    t = seg_finish(u_q, ss_qb, bqg_ref[...], B_QK)
    qb_ref[...] = (rope96(t) * (B_QK ** -0.5 * LOG2E)).astype(BF16)
    t = seg_finish(u_k, ss_kb, bkg_ref[...], B_QK)
    kb_ref[...] = rope96(t).astype(BF16)


def _full(shape):
    nd = len(shape)
    return pl.BlockSpec(shape, lambda i: (0,) * nd)


STATE_TAILS = ((2, HD), (2, HD), (B_KV_LORA,), (B_ROPE,), (2, HD), (2, HD))


def _layer_spec(a, l):
    nd = a.ndim - 1
    return pl.BlockSpec((None,) + a.shape[1:], lambda *_: (l,) + (0,) * nd)


VEC_ROWS = dict(n1g=(0, D), n2g=(1, D), aqg=(2, 4 * HD), akg=(3, 2 * HD), cqg=(4, 4 * HD), ckg=(5, 2 * HD),
                bcqg=(6, B_Q_LORA), bckvg=(7, B_KV_LORA), bqg=(8, 4 * BP), bkg=(9, 4 * BP))


def _vec_spec(l, name):
    row, width = VEC_ROWS[name]
    return pl.BlockSpec((None, None, 1, width), lambda *_: (l, row, 0, 0))


def _projection(is_lat, l, x_src, x_off, mod, lw, consts, prev_states=()):
    assert is_lat or bool(prev_states) == (l > 0)
    t_rows = T_LAT if is_lat else T_CTX
    TQ = TQ_LAT if is_lat else TQ_CTX
    n_tiles = t_rows // TQ
    if is_lat:
        mod_map = lambda i: (l, 1 + i // (LAT_LEN // TQ), 0, 0)
    else:
        mod_map = lambda i: (l, 0, 0, 0)
    vec = lw["vec"]
    gain = lambda name: (vec, _vec_spec(l, name))
    whole = lambda a: (a, _full(a.shape))
    layer = lambda a: (a, _layer_spec(a, l))
    pairs = [(x_src, pl.BlockSpec((TQ, D), lambda i: (i + x_off, 0))),
             (mod, pl.BlockSpec((None, None, 1, 6 * D), mod_map)),
             gain("n1g"), layer(lw["w_in"]), whole(consts["seg64"]), whole(consts["seg96"]),
             gain("aqg"), gain("akg"), gain("cqg"), gain("ckg"), gain("bcqg"), gain("bckvg"), gain("bqg"), gain("bkg"),
             layer(lw["w_uq"]), layer(lw["w_uk"]), layer(lw["w_uv"]), whole(consts["place"])]
    ins = [a for a, _ in pairs]
    in_specs = [s for _, s in pairs]
    n_plain = len(ins)
    ins += list(prev_states)
    in_specs += [pl.BlockSpec(memory_space=pl.ANY) for _ in prev_states]
    if is_lat:
        tabs = [consts["cos64"], consts["sin64"], consts["cos96"], consts["sin96"]]
        ins += tabs + [consts["first64"], consts["first96"]]
        in_specs += [pl.BlockSpec((TQ, a.shape[1]), lambda i: (i % (LAT_LEN // TQ), 0)) for a in tabs]
        in_specs += [_full(consts["first64"].shape), _full(consts["first96"].shape)]
    widths = [4 * HD, 2 * HD, 2 * HD, 4 * BP, 4 * BP, 4 * B_V, 4 * HD, 2 * HD, 2 * HD]
    out_shape = [jax.ShapeDtypeStruct((t_rows, w), BF16) for w in widths]
    out_specs = [pl.BlockSpec((TQ, w), lambda i: (i, 0)) for w in widths]
    if not is_lat:
        for tail in STATE_TAILS:
            zeros = (0,) * len(tail)
            out_shape.append(jax.ShapeDtypeStruct((N_CTX_SEQ, DEPTH, CTX_LEN) + tail, F32))
            if prev_states:
                out_specs.append(pl.BlockSpec((SEQ_PER_STEP, None, CTX_LEN) + tail, lambda i, z=zeros: (i, l, 0) + z))
            else:
                out_specs.append(pl.BlockSpec((SEQ_PER_STEP, DEPTH, CTX_LEN) + tail, lambda i, z=zeros: (i, 0, 0) + z))
    aliases = {n_plain + k: len(widths) + k for k in range(len(prev_states))}
    return pl.pallas_call(
        functools.partial(_proj_kernel, is_lat, len(prev_states)),
        grid=(n_tiles,),
        in_specs=in_specs,
        out_specs=out_specs,
        out_shape=out_shape,
        input_output_aliases=aliases,
        compiler_params=pltpu.CompilerParams(
            dimension_semantics=("arbitrary",), vmem_limit_bytes=VMEM_LIMIT),
        name="proj_lat" if is_lat else "proj_ctx",
    )(*ins)


def _cache_kv_kernel(ckv_ref, kr_ref, seg96_ref, bkg_ref, wuk_ref, wuv_ref, plc_ref, kb_ref, vb_ref):
    ckv16 = ckv_ref[...].astype(BF16)
    vb_ref[...] = _dot(ckv16, wuv_ref[...]).astype(BF16)
    kpre = _dot(ckv16, wuk_ref[...]) + _dot(kr_ref[...].astype(BF16), plc_ref[0:B_ROPE, :])
    kb_ref[...] = _seg_norm(kpre, seg96_ref[...], bkg_ref[...], B_QK).astype(BF16)


def _cache_kv(cache_b_ckv, cache_b_krope, lw, consts):
    def c_map(l, r):
        return (r // (PAST // TM), l, r % (PAST // TM), 0)

    def w_spec(a):
        return pl.BlockSpec((None,) + a.shape[1:], lambda l, r: (l,) + (0,) * (a.ndim - 1))

    def s_spec(a):
        return pl.BlockSpec(a.shape, lambda l, r: (0,) * a.ndim)

    rows = N_LAT_SEQ * PAST
    return pl.pallas_call(
        _cache_kv_kernel,
        grid=(DEPTH, rows // TM),
        in_specs=[pl.BlockSpec((None, None, TM, B_KV_LORA), c_map), pl.BlockSpec((None, None, TM, B_ROPE), c_map),
                  s_spec(consts["seg96"]),
                  pl.BlockSpec((None, None, 1, 4 * BP), lambda l, r: (l, VEC_ROWS["bkg"][0], 0, 0)),
                  w_spec(lw["w_uk"]), w_spec(lw["w_uv"]),
                  s_spec(consts["place"])],
        out_specs=[pl.BlockSpec((None, TM, 4 * BP), lambda l, r: (l, r, 0)),
                   pl.BlockSpec((None, TM, 4 * B_V), lambda l, r: (l, r, 0))],
        out_shape=[jax.ShapeDtypeStruct((DEPTH, rows, 4 * BP), BF16),
                   jax.ShapeDtypeStruct((DEPTH, rows, 4 * B_V), BF16)],
        compiler_params=pltpu.CompilerParams(
            dimension_semantics=("arbitrary", "arbitrary"), vmem_limit_bytes=VMEM_LIMIT),
        name="cache_kv",
    )(cache_b_ckv, cache_b_krope, consts["seg96"], lw["vec"], lw["w_uk"], lw["w_uv"], consts["place"])


def _scores(q, segs):
    scores = []
    for k, _, mask in segs:
        s = _dot_t(q, k)
        if mask is not None:
            s = jnp.where(mask, s, NEG_INF)
        scores.append(s)
    return scores


def _softmax_pv(scores, segs, sink, num_cols, den_col):
    m = None
    for s in scores:
        sm = jnp.max(s, axis=-1, keepdims=True)
        m = sm if m is None else jnp.maximum(m, sm)
    if sink is not None:
        m = jnp.maximum(m, sink)
    acc = None
    den = None
    for s, (_, v, _) in zip(scores, segs):
        e = jnp.exp2(s - m)
        if den_col is None:
            d = jnp.sum(e, axis=-1, keepdims=True)
            den = d if den is None else den + d
        o = _dot(e.astype(BF16), v)
        acc = o if acc is None else acc + o
    if den_col is not None:
        den = acc[:, den_col:den_col + 1]
        acc = acc[:, num_cols]
    if sink is not None:
        den = den + jnp.exp2(sink - m)
    return acc / den


HEAD_LOOKAHEAD = 1


def _run_heads(jobs, o_ref):
    ready = [_scores(job[0](), job[1]()) for job in jobs[:HEAD_LOOKAHEAD]]
    for n, (_, load_segs, sink, col, num_cols, den_col) in enumerate(jobs):
        scores = ready.pop(0)
        if n + HEAD_LOOKAHEAD < len(jobs):
            ahead = jobs[n + HEAD_LOOKAHEAD]
            ready.append(_scores(ahead[0](), ahead[1]()))
        o = _softmax_pv(scores, load_segs(values=True), sink, num_cols, den_col)
        o_ref[:, col:col + o.shape[1]] = o.astype(BF16)


OUT_B = 4 * HD
OUT_C = 4 * HD + 4 * B_V
LOG2E = 1.4426950408889634


def _head_job(q_ref, q_cols, seg_refs, kv_cols, v_cols, sink, col, rows=None, masks=None, mxu_sums=True):
    width = v_cols.stop - v_cols.start
    if not mxu_sums:
        load_cols, num_cols, den_col = v_cols, None, None
    elif width == HD:
        pair = v_cols.start // LANE * LANE
        load_cols = slice(pair, pair + LANE)
        lo = v_cols.start - pair
        num_cols, den_col = slice(lo, lo + HD), (lo + HD) % LANE
    else:
        load_cols, num_cols, den_col = v_cols, slice(0, width), width
    if sink is not None:
        sink = sink * LOG2E

    def with_ones(v):
        if not mxu_sums:
            return v
        if width != HD:
            return jnp.concatenate([v, jnp.ones_like(v)], axis=1)
        lane = lax.broadcasted_iota(jnp.int32, (1, LANE), 1)
        keep = jnp.where((lane >= num_cols.start) & (lane < num_cols.stop), 1.0, 0.0).astype(BF16)
        return v * keep + (1.0 - keep)

    def load_segs(values=False):
        segs = []
        for n, (k_ref, v_ref) in enumerate(seg_refs):
            r = rows if (rows is not None and n == 0) else slice(None)
            if values:
                segs.append((None, with_ones(v_ref[r, load_cols]), None))
            else:
                segs.append((k_ref[r, kv_cols], None, None if masks is None else masks[n]))
        return segs

    return (lambda: q_ref[:, q_cols]), load_segs, sink, col, num_cols, den_col


def _attn_ctx_kernel(l, sink_ref, qa_ref, ka_ref, va_ref, qb_ref, kb_ref, vb_ref, qc_ref, kc_ref, vc_ref, o_ref):
    jobs = []
    for h in range(4):
        cs = slice(h // 2 * HD, (h // 2 + 1) * HD)
        jobs.append(_head_job(qa_ref, slice(h * HD, (h + 1) * HD), [(ka_ref, va_ref)], cs, cs, sink_ref[l, h],
                              h * HD, mxu_sums=False))
    for h in range(4):
        ks, vs = slice(h * BP, (h + 1) * BP), slice(h * B_V, (h + 1) * B_V)
        jobs.append(_head_job(qb_ref, ks, [(kb_ref, vb_ref)], ks, vs, None, OUT_B + h * B_V, mxu_sums=False))
    for h in range(4):
        cs = slice(h // 2 * HD, (h // 2 + 1) * HD)
        jobs.append(_head_job(qc_ref, slice(h * HD, (h + 1) * HD), [(kc_ref, vc_ref)], cs, cs, None, OUT_C + h * HD,
                              mxu_sums=False))
    _run_heads(jobs, o_ref)


def _attention_ctx(l, sink, p):
    names = ["qa", "ka", "va", "qb", "kb", "vb", "qc", "kc", "vc"]
    ins = [p[n] for n in names]
    in_specs = [pl.BlockSpec(memory_space=pltpu.SMEM)]
    in_specs += [pl.BlockSpec((CTX_LEN, a.shape[1]), lambda i: (i, 0)) for a in ins]
    return pl.pallas_call(
        functools.partial(_attn_ctx_kernel, l),
        grid=(N_CTX_SEQ,),
        in_specs=in_specs,
        out_specs=pl.BlockSpec((CTX_LEN, D), lambda i: (i, 0)),
        out_shape=jax.ShapeDtypeStruct((T_CTX, D), BF16),
        compiler_params=pltpu.CompilerParams(
            dimension_semantics=("arbitrary",), vmem_limit_bytes=VMEM_LIMIT),
        name="attn_ctx",
    )(sink, *ins)


WIN_SPAN = TM + 2 * WINDOW


def _attn_lat_kernel(l, sink_ref, qa_ref, qb_ref, qc_ref, ka_ref, va_ref, kb_ref, vb_ref, kc_ref, vc_ref,
                     cka_ref, cva_ref, ckb_ref, cvb_ref, ckc_ref, cvc_ref, o_ref):
    qi = pl.program_id(1)
    ws = pl.multiple_of(jnp.clip(qi * TM - WINDOW, 0, LAT_LEN - WIN_SPAN), WINDOW)
    qpos = qi * TM + lax.broadcasted_iota(jnp.int32, (TM, WIN_SPAN), 0)
    kpos = ws + lax.broadcasted_iota(jnp.int32, (TM, WIN_SPAN), 1)
    band = jnp.abs(qpos - kpos) <= WINDOW
    jobs = []
    for h in range(4):
        cs = slice(h // 2 * HD, (h // 2 + 1) * HD)
        jobs.append(_head_job(qa_ref, slice(h * HD, (h + 1) * HD), [(ka_ref, va_ref), (cka_ref, cva_ref)], cs, cs,
                              sink_ref[l, h], h * HD, rows=pl.ds(ws, WIN_SPAN), masks=(band, None)))
    for h in range(4):
        ks, vs = slice(h * BP, (h + 1) * BP), slice(h * B_V, (h + 1) * B_V)
        jobs.append(_head_job(qb_ref, ks, [(kb_ref, vb_ref), (ckb_ref, cvb_ref)], ks, vs, None, OUT_B + h * B_V))
    for h in range(4):
        cs = slice(h // 2 * HD, (h // 2 + 1) * HD)
        jobs.append(_head_job(qc_ref, slice(h * HD, (h + 1) * HD), [(kc_ref, vc_ref), (ckc_ref, cvc_ref)], cs, cs,
                              None, OUT_C + h * HD))
    _run_heads(jobs, o_ref)


def _attention_lat(l, sink, p, cache):
    q_ins = [p["qa"], p["qb"], p["qc"]]
    kv_ins = [p[n] for n in ("ka", "va", "kb", "vb", "kc", "vc")]
    c_names = ("ka", "va", "kb", "vb", "kc", "vc")
    c_ins = [cache[n] for n in c_names]
    in_specs = [pl.BlockSpec(memory_space=pltpu.SMEM)]
    in_specs += [pl.BlockSpec((TM, a.shape[1]), lambda b, i: (b * LAT_TILES_PER_SEQ + i, 0)) for a in q_ins]
    in_specs += [pl.BlockSpec((LAT_LEN, a.shape[1]), lambda b, i: (b, 0)) for a in kv_ins]
    for n, a in zip(c_names, c_ins):
        if n in ("kb", "vb"):
            in_specs.append(pl.BlockSpec((None, PAST, a.shape[-1]), lambda b, i: (l, b, 0)))
        else:
            in_specs.append(pl.BlockSpec((None, None, PAST, a.shape[-1]), lambda b, i: (b, l, 0, 0)))
    return pl.pallas_call(
        functools.partial(_attn_lat_kernel, l),
        grid=(N_LAT_SEQ, LAT_TILES_PER_SEQ),
        in_specs=in_specs,
        out_specs=pl.BlockSpec((TM, D), lambda b, i: (b * LAT_TILES_PER_SEQ + i, 0)),
        out_shape=jax.ShapeDtypeStruct((T_LAT, D), BF16),
        compiler_params=pltpu.CompilerParams(
            dimension_semantics=("arbitrary", "arbitrary"), vmem_limit_bytes=VMEM_LIMIT),
        name="attn_lat",
    )(sink, *q_ins, *kv_ins, *c_ins)


def _pack_rows(x):
    half = x.shape[1] // 2
    r = x.astype(BF16).astype(F32)
    hi = lax.bitcast_convert_type(r[:, :half], jnp.int32)
    lo = lax.bitcast_convert_type(r[:, half:], jnp.int32)
    return jnp.bitwise_or(hi, lax.shift_right_logical(lo, 16))


def _unpack_rows(p):
    a = lax.bitcast_convert_type(jnp.bitwise_and(p, -65536), F32)
    b = lax.bitcast_convert_type(lax.shift_left(p, 16), F32)
    return jnp.concatenate([a, b], axis=1).astype(BF16)


ROUTE_ROWS = 16


def _post_kernel(mc_ref, ml_ref, xc_ref, xl_ref, mod_ref, n2g_ref, wout_ref, rwt_ref, rb_ref, tri_ref,
                 xo_ref, h2_ref, gate_ref, route_ref, cnt_ref, run_ref):
    i = pl.program_id(0)

    @pl.when(i == 0)
    def _():
        run_ref[...] = jnp.zeros_like(run_ref)
        cnt_ref[...] = jnp.zeros_like(cnt_ref)

    is_ctx = i < T_CTX // TP
    m = mod_ref[...]
    g1, sh2, sc2 = m[:, 2 * D:3 * D], m[:, 3 * D:4 * D], m[:, 4 * D:5 * D]

    mix = jnp.where(is_ctx, mc_ref[...], ml_ref[...])
    x = jnp.where(is_ctx, xc_ref[...], xl_ref[...]) + g1 * _dot(mix, wout_ref[...])
    xo_ref[...] = x
    h2 = _rms(x, n2g_ref[...]) * (1.0 + sc2) + sh2
    hi = h2.astype(BF16)
    h2_ref[...] = _pack_rows(h2)
    lo = (h2 - hi.astype(F32)).astype(BF16)
    a = _dot_t(rwt_ref[...], hi)
    b = _dot_t(rwt_ref[0:N_EXPERTS, :], lo)
    logits = a[0:N_EXPERTS] + a[N_EXPERTS:] + b + rb_ref[...]
    eidx = lax.broadcasted_iota(jnp.int32, logits.shape, 0).astype(F32)
    work = logits
    vals, idxs, hots = [], [], []
    for _ in range(TOP_K):
        v = jnp.max(work, axis=0, keepdims=True)
        idx = jnp.min(jnp.where(work == v, eidx, float(N_EXPERTS)), axis=0, keepdims=True)
        hot = eidx == idx
        vals.append(v)
        idxs.append(idx)
        hots.append(hot)
        work = jnp.where(hot, -jnp.inf, work)
    es = [jnp.exp(v - vals[0]) for v in vals]
    den = es[0] + es[1] + es[2] + es[3]
    gates = [e / den for e in es]
    sel = jnp.where(hots[0] | hots[1] | hots[2] | hots[3], 1.0, 0.0)
    run = run_ref[:, 0:1]
    before = _dot(sel.astype(BF16), tri_ref[...]) + run
    run_new = jnp.broadcast_to(run + jnp.sum(sel, axis=1, keepdims=True), run_ref.shape)
    run_ref[...] = run_new
    cnt_ref[...] = run_new
    ranks = [jnp.sum(jnp.where(hots[k], before, 0.0), axis=0, keepdims=True) for k in range(TOP_K)]
    pad = [jnp.zeros((ROUTE_ROWS - 3 * TOP_K, TP), F32)]
    route_ref[...] = jnp.concatenate(gates + idxs + ranks + pad, axis=0)
    gate_ref[...] = jnp.concatenate(gates + [jnp.zeros((LANE - TOP_K, TP), F32)], axis=0).T


def _mod_row(i, tm=TM):
    n_ctx = T_CTX // tm
    return jnp.where(i < n_ctx, 0, 1 + (i - n_ctx) // (LAT_LEN // tm))


def _ctx_tile(i):
    return jnp.minimum(i, T_CTX // TP - 1)


def _lat_tile(i):
    return jnp.maximum(i - T_CTX // TP, 0)


def _post_attention(l, mix_ctx, mix_lat, x_ctx, x_lat, x_lat_off, mod, lw, consts):
    ins = [mix_ctx, mix_lat, x_ctx, x_lat, mod, lw["vec"], lw["w_out"], lw["rwt"], lw["rb"], consts["tri"]]
    in_specs = [
        pl.BlockSpec((TP, D), lambda i: (_ctx_tile(i), 0)),
        pl.BlockSpec((TP, D), lambda i: (_lat_tile(i), 0)),
        pl.BlockSpec((TP, D), lambda i: (_ctx_tile(i), 0)),
        pl.BlockSpec((TP, D), lambda i: (_lat_tile(i) + x_lat_off, 0)),
        pl.BlockSpec((None, None, 1, 6 * D), lambda i: (l, _mod_row(i, TP), 0, 0)),
        _vec_spec(l, "n2g"),
    ] + [_layer_spec(a, l) for a in ins[6:9]] + [_full(consts["tri"].shape)]
    return pl.pallas_call(
        _post_kernel,
        grid=(T_ALL // TP,),
        in_specs=in_specs,
        out_specs=[pl.BlockSpec((TP, D), lambda i: (i, 0)), pl.BlockSpec((TP, DW), lambda i: (i, 0)),
                   pl.BlockSpec((TP, LANE), lambda i: (i, 0)), pl.BlockSpec((ROUTE_ROWS, TP), lambda i: (0, i)),
                   pl.BlockSpec((N_EXPERTS, LANE), lambda i: (0, 0))],
        out_shape=[jax.ShapeDtypeStruct((T_ALL, D), F32), jax.ShapeDtypeStruct((T_ALL, DW), jnp.int32),
                   jax.ShapeDtypeStruct((T_ALL, LANE), F32), jax.ShapeDtypeStruct((ROUTE_ROWS, T_ALL), F32),
                   jax.ShapeDtypeStruct((N_EXPERTS, LANE), F32)],
        scratch_shapes=[pltpu.VMEM((N_EXPERTS, LANE), F32)],
        compiler_params=pltpu.CompilerParams(
            dimension_semantics=("arbitrary",), vmem_limit_bytes=VMEM_LIMIT),
        name="post_attn",
    )(*ins)


def _expert_kernel(l, te_ref, nu_ref, nxt_ref, nv_ref, slot_ref, x_ref, wgu_hbm, bgu_ref, wdn_hbm, bdn_ref, y_ref,
                   wgu32, wdn32, wgu16, wdn16, sem):
    i = pl.program_id(0)
    prev = te_ref[jnp.maximum(i - 1, 0)]
    new_expert = jnp.logical_or(i == 0, te_ref[i] != prev)
    slot = slot_ref[i]

    def weight_copies(e, s):
        return (pltpu.make_async_copy(wgu_hbm.at[l, e], wgu32.at[s], sem.at[s, 0]),
                pltpu.make_async_copy(wdn_hbm.at[l, e], wdn32.at[s], sem.at[s, 1]))

    @pl.when(i == 0)
    def _():
        for cp in weight_copies(te_ref[0], slot):
            cp.start()

    @pl.when(new_expert)
    def _():
        for cp in weight_copies(te_ref[i], slot):
            cp.wait()

        @pl.when(nxt_ref[i] >= 0)
        def _():
            for cp in weight_copies(nxt_ref[i], 1 - slot):
                cp.start()

    def compute(rows, convert):
        x = _unpack_rows(x_ref[0:rows, :])

        def gate_up(a):
            if convert:
                for c in (slice(a, a + FF_CHUNK), slice(D_FF + a, D_FF + a + FF_CHUNK)):
                    wgu16[:, c] = wgu32[slot, :, c].astype(BF16)
                wdn16[a:a + FF_CHUNK, :] = wdn32[slot, a:a + FF_CHUNK, :].astype(BF16)
            b = a + FF_CHUNK
            g = _dot(x, wgu16[:, a:b]) + bgu_ref[:, a:b]
            up = _dot(x, wgu16[:, D_FF + a:D_FF + b]) + bgu_ref[:, D_FF + a:D_FF + b]
            return g, up

        def activate(g, up):
            g = jnp.minimum(g, SWIGLU_LIMIT)
            up = jnp.clip(up, -SWIGLU_LIMIT, SWIGLU_LIMIT)
            return ((up + 1.0) * (g * jax.nn.sigmoid(SWIGLU_ALPHA * g))).astype(BF16)

        starts = list(range(0, D_FF, FF_CHUNK))
        acc = None
        pre = gate_up(starts[0])
        for n, a in enumerate(starts):
            hid = activate(*pre)
            if n + 1 < len(starts):
                pre = gate_up(starts[n + 1])
            o = _dot(hid, wdn16[a:a + FF_CHUNK, :])
            acc = o if acc is None else acc + o
        y_ref[0:rows, :] = _pack_rows(acc + bdn_ref[...])

    used = i < nu_ref[0]
    for n_valid in range(1, MOE_STEP // MOE_TM + 1):
        for convert in (False, True):
            @pl.when(used & (nv_ref[i] == n_valid) & (new_expert == convert))
            def _():
                compute(n_valid * MOE_TM, convert)


def _experts(l, tile_expert, n_used, next_expert, n_valid, slot, x_sorted, w_gu, b_gu, w_dn, b_dn):
    def row_map(i, te, nu, nxt, nv, sl):
        return (jnp.minimum(i, nu[0] - 1), 0)

    def b_map(i, te, nu, nxt, nv, sl):
        return (l, te[i], 0, 0)

    grid_spec = pltpu.PrefetchScalarGridSpec(
        num_scalar_prefetch=5,
        grid=(MOE_STEPS,),
        in_specs=[
            pl.BlockSpec((MOE_STEP, DW), row_map),
            pl.BlockSpec(memory_space=pl.ANY),
            pl.BlockSpec((None, None, 1, 2 * D_FF), b_map),
            pl.BlockSpec(memory_space=pl.ANY),
            pl.BlockSpec((None, None, 1, D), b_map),
        ],
        out_specs=pl.BlockSpec((MOE_STEP, DW), row_map),
        scratch_shapes=[pltpu.VMEM((2, D, 2 * D_FF), F32), pltpu.VMEM((2, D_FF, D), F32),
                        pltpu.VMEM((D, 2 * D_FF), BF16), pltpu.VMEM((D_FF, D), BF16),
                        pltpu.SemaphoreType.DMA((2, 2))],
    )
    return pl.pallas_call(
        functools.partial(_expert_kernel, l),
        grid_spec=grid_spec,
        out_shape=jax.ShapeDtypeStruct((MOE_ROWS, DW), jnp.int32),
        compiler_params=pltpu.CompilerParams(
            dimension_semantics=("arbitrary",), vmem_limit_bytes=VMEM_LIMIT),
        name="experts",
    )(tile_expert, n_used, next_expert, n_valid, slot, x_sorted, w_gu, b_gu.reshape(DEPTH, N_EXPERTS, 1, 2 * D_FF),
      w_dn, b_dn.reshape(DEPTH, N_EXPERTS, 1, D))


def _combine_kernel(x_ref, y_ref, route_ref, mod_ref, o_ref):
    g2 = mod_ref[...][:, 5 * D:6 * D]
    r = route_ref[...]
    acc = r[:, 0:1] * _unpack_rows(y_ref[0]).astype(F32)
    for k in range(1, TOP_K):
        acc = acc + r[:, k:k + 1] * _unpack_rows(y_ref[k]).astype(F32)
    o_ref[...] = x_ref[...] + g2 * acc


def _combine(l, first_tile, n_tiles, x_all, y_tok, gates, mod):
    return pl.pallas_call(
        _combine_kernel,
        grid=(n_tiles,),
        in_specs=[
            pl.BlockSpec((TM, D), lambda i: (i + first_tile, 0)),
            pl.BlockSpec((TOP_K, TM, DW), lambda i: (0, i, 0)),
            pl.BlockSpec((TM, LANE), lambda i: (i + first_tile, 0)),
            pl.BlockSpec((None, None, 1, 6 * D), lambda i: (l, _mod_row(i + first_tile), 0, 0)),
        ],
        out_specs=pl.BlockSpec((TM, D), lambda i: (i, 0)),
        out_shape=jax.ShapeDtypeStruct((n_tiles * TM, D), F32),
        compiler_params=pltpu.CompilerParams(
            dimension_semantics=("arbitrary",), vmem_limit_bytes=VMEM_LIMIT),
        name="combine",
    )(x_all, y_tok, gates, mod)


SC_CORES, SC_SUBCORES = 2, 16
SC_WORKERS = SC_CORES * SC_SUBCORES
SC_CHUNK = 128
SC_CHUNKS_PER_WORKER = T_ALL // SC_CHUNK // SC_WORKERS


def _sc_mesh():
    return plsc.VectorSubcoreMesh(core_axis_name="c", subcore_axis_name="s")


def _sc_scratch():
    return [pltpu.VMEM((TOP_K, SC_CHUNK), jnp.int32), pltpu.VMEM((SC_CHUNK, DW), jnp.int32),
            pltpu.SemaphoreType.DMA]


def _dispatch_rows(h2p, dest):
    @functools.partial(pl.kernel, mesh=_sc_mesh(), out_type=jax.ShapeDtypeStruct((MOE_ROWS, DW), jnp.int32),
                       scratch_types=_sc_scratch(), name="dispatch_rows")
    def run(h_hbm, d_hbm, o_hbm, idx_v, rows_v, sem):
        wid = lax.axis_index("s") * SC_CORES + lax.axis_index("c")
        for j in range(SC_CHUNKS_PER_WORKER):
            c = wid * SC_CHUNKS_PER_WORKER + j
            pltpu.sync_copy(d_hbm.at[c], idx_v)
            pltpu.sync_copy(h_hbm.at[pl.ds(c * SC_CHUNK, SC_CHUNK)], rows_v)
            copies = [pltpu.async_copy(rows_v, o_hbm.at[idx_v.at[k]], sem) for k in range(TOP_K)]
            for cp in copies:
                cp.wait()

    return run(h2p, dest)


def _gather_rows(y, dest, first_chunk, n_chunks, *after):
    per_worker = n_chunks // SC_WORKERS
    assert per_worker * SC_WORKERS == n_chunks

    @functools.partial(pl.kernel, mesh=_sc_mesh(),
                       out_type=jax.ShapeDtypeStruct((TOP_K, n_chunks * SC_CHUNK, DW), jnp.int32),
                       scratch_types=_sc_scratch(), name="gather_rows")
    def run(y_hbm, d_hbm, *rest):
        o_hbm, idx_v, rows_v, sem = rest[len(after):]
        wid = lax.axis_index("s") * SC_CORES + lax.axis_index("c")
        for j in range(per_worker):
            c = wid * per_worker + j
            pltpu.sync_copy(d_hbm.at[first_chunk + c], idx_v)
            for k in range(TOP_K):
                pltpu.async_copy(y_hbm.at[idx_v.at[k]], rows_v, sem).wait()
                pltpu.sync_copy(rows_v, o_hbm.at[k, pl.ds(c * SC_CHUNK, SC_CHUNK)])

    return run(y, dest, *after)


def _constants():
    lane64 = np.arange(4 * HD)
    seg64 = (lane64[:, None] // HD == lane64[None, :] // HD).astype(np.float32)
    lane96 = np.arange(4 * BP)
    real = lane96 % BP < B_QK
    seg96 = ((lane96[:, None] // BP == lane96[None, :] // BP) & real[:, None] & real[None, :]).astype(np.float32)
    place = np.zeros((LANE, 4 * BP), np.float32)
    for hh in range(4):
        place[np.arange(B_ROPE), hh * BP + B_NOPE + np.arange(B_ROPE)] = 1.0

    def angles(rot_dim):
        pos = np.arange(LAT_LEN)
        rows = (pos // GRID_W).astype(np.float32)
        cols = (pos % GRID_W).astype(np.float32)
        axis_dim = rot_dim // 2
        inv = np.power(np.float32(ROPE_THETA), -(np.arange(0, axis_dim, 2, dtype=np.float32) / np.float32(axis_dim)))
        ang = np.concatenate([rows[:, None] * inv, cols[:, None] * inv], axis=-1).astype(np.float32)
        return np.cos(ang), np.sin(ang)

    def head_tables(rot_dim):
        cos, sin = angles(rot_dim)
        q = rot_dim // 4
        cr, cc, sr, sc = cos[:, :q], cos[:, q:], sin[:, :q], sin[:, q:]
        return (np.concatenate([cr, cr, cc, cc], axis=-1), np.concatenate([-sr, sr, -sc, sc], axis=-1))

    c64, s64 = head_tables(HD)
    cos64 = np.tile(c64, (1, 4))
    sin64 = np.tile(s64, (1, 4))
    c32, s32 = head_tables(B_ROPE)
    ones = np.ones((LAT_LEN, B_NOPE), np.float32)
    zeros = np.zeros((LAT_LEN, B_NOPE), np.float32)
    padz = np.zeros((LAT_LEN, BP - B_QK), np.float32)
    cos96 = np.tile(np.concatenate([ones, c32, padz], axis=-1), (1, 4))
    sin96 = np.tile(np.concatenate([zeros, s32, padz], axis=-1), (1, 4))
    first64 = ((lane64 % 32) < 16).astype(np.float32)[None, :]
    first96 = (((lane96 % BP) % 16) < 8).astype(np.float32)[None, :]
    tri = (np.arange(TP)[:, None] < np.arange(TP)[None, :]).astype(np.float32)
    f32 = lambda a: jnp.asarray(a, F32)
    return dict(seg64=jnp.asarray(seg64, BF16), seg96=jnp.asarray(seg96, BF16), place=jnp.asarray(place, BF16),
                tri=jnp.asarray(tri, BF16),
                cos64=f32(cos64), sin64=f32(sin64), cos96=f32(cos96), sin96=f32(sin96),
                first64=f32(first64), first96=f32(first96))


def _pad_heads(w, per_head, width):
    lead = w.shape[:-1]
    w = w.reshape(lead + (4, per_head))
    return jnp.pad(w, ((0, 0),) * (len(lead) + 1) + ((0, width - per_head),)).reshape(lead + (4 * width,))


def _weights(norm1_g, norm2_g, w_in, a_q_g, a_k_g, b_cq_g, b_ckv_g, w_uq, w_ukv, b_q_g, b_k_g,
             c_q_g, c_k_g, w_out, router_w, router_b):
    o = np.cumsum((0, 256, 128, 128, 384, 256, 32, 256, 128, 128))
    seg = lambda k: w_in[:, :, o[k]:o[k + 1]]
    w_in_r = jnp.concatenate([seg(0), seg(1), seg(2), seg(3), seg(4), seg(6), seg(7), seg(8), seg(5),
                              jnp.zeros((DEPTH, D, LANE - B_ROPE), F32)], axis=-1).astype(BF16)
    ukv = w_ukv.reshape(DEPTH, B_KV_LORA, 4, B_NOPE + B_V)
    w_uk = _pad_heads(ukv[..., :B_NOPE].reshape(DEPTH, B_KV_LORA, 4 * B_NOPE), B_NOPE, BP)
    w_uv = ukv[..., B_NOPE:].reshape(DEPTH, B_KV_LORA, 4 * B_V)
    rw_hi = router_w.astype(BF16)
    rw_lo = (router_w - rw_hi.astype(F32)).astype(BF16)
    tile = lambda v, n: jnp.tile(v, (1, n))
    rows = dict(n1g=norm1_g, n2g=norm2_g, aqg=tile(a_q_g, 4), akg=tile(a_k_g, 2), cqg=tile(c_q_g, 4),
                ckg=tile(c_k_g, 2), bcqg=b_cq_g, bckvg=b_ckv_g,
                bqg=_pad_heads(tile(b_q_g, 4), B_QK, BP), bkg=_pad_heads(tile(b_k_g, 4), B_QK, BP))
    order = sorted(VEC_ROWS, key=lambda n: VEC_ROWS[n][0])
    assert all(rows[n].shape == (DEPTH, VEC_ROWS[n][1]) for n in order)
    vec = jnp.stack([jnp.pad(rows[n], ((0, 0), (0, D - rows[n].shape[1]))) for n in order], axis=1)
    return dict(
        vec=vec[:, :, None, :], w_in=w_in_r,
        w_uq=_pad_heads(w_uq, B_QK, BP).astype(BF16), w_uk=w_uk.astype(BF16), w_uv=w_uv.astype(BF16),
        w_out=w_out.astype(BF16),
        rwt=jnp.concatenate([jnp.swapaxes(rw_hi, 1, 2), jnp.swapaxes(rw_lo, 1, 2)], axis=1),
        rb=router_b[:, :, None])


def _moe(l, x_all, h2p, gate_slab, route, counts, mod, w_gu, b_gu, w_dn, b_dn):
    counts = counts[:, 0].astype(jnp.int32)
    padded = (counts + MOE_STEP - 1) // MOE_STEP * MOE_STEP
    pend = jnp.cumsum(padded)
    pstart = pend - padded
    n_used = (pend[-1] // MOE_STEP).astype(jnp.int32)
    steps = jnp.minimum(jnp.arange(MOE_STEPS, dtype=jnp.int32), n_used - 1)
    tile_expert = jnp.sum((pend[None, :] <= steps[:, None] * MOE_STEP).astype(jnp.int32), axis=1)
    tile_expert = jnp.minimum(tile_expert, N_EXPERTS - 1)
    of_expert = lambda table: jnp.sum(
        jnp.where(tile_expert[:, None] == jnp.arange(N_EXPERTS)[None, :], table[None, :], 0), axis=1)
    group_end_step = of_expert(pend) // MOE_STEP
    after = jnp.sum((pend[None, :] <= group_end_step[:, None] * MOE_STEP).astype(jnp.int32), axis=1)
    next_expert = jnp.where(group_end_step < n_used, jnp.minimum(after, N_EXPERTS - 1), -1).astype(jnp.int32)
    rows_left = of_expert(pstart + counts) - steps * MOE_STEP
    n_valid = jnp.clip((rows_left + MOE_TM - 1) // MOE_TM, 1, MOE_STEP // MOE_TM).astype(jnp.int32)
    e = route[TOP_K:2 * TOP_K].astype(jnp.int32)
    rank = route[2 * TOP_K:3 * TOP_K].astype(jnp.int32)
    start = jnp.sum(jnp.where(e[:, :, None] == jnp.arange(N_EXPERTS)[None, None, :], pstart[None, None, :], 0), axis=-1)
    dest = (start + rank).reshape(TOP_K, T_ALL // SC_CHUNK, SC_CHUNK).transpose(1, 0, 2)
    x_sorted = _dispatch_rows(h2p, dest)
    changed = jnp.concatenate([jnp.zeros((1,), jnp.int32), (tile_expert[1:] != tile_expert[:-1]).astype(jnp.int32)])
    slot = jnp.cumsum(changed) % 2
    y = _experts(l, tile_expert, n_used.reshape(1), next_expert, n_valid, slot.astype(jnp.int32), x_sorted,
                 w_gu, b_gu, w_dn, b_dn)
    ctx_chunks, lat_chunks = T_CTX // SC_CHUNK, T_LAT // SC_CHUNK
    y_ctx = _gather_rows(y, dest, 0, ctx_chunks)
    out_ctx = _combine(l, 0, N_CTX_TILES, x_all, y_ctx, gate_slab, mod)
    y_lat = _gather_rows(y, dest, ctx_chunks, lat_chunks, out_ctx)
    return out_ctx, _combine(l, N_CTX_TILES, N_LAT_TILES, x_all, y_lat, gate_slab, mod)


def kernel(x_prompt, x_sample, cache_a_k, cache_a_v, cache_b_ckv, cache_b_krope, cache_c_k, cache_c_v, c, c_ctx,
           norm1_g, norm2_g, w_mod, b_mod, w_in, a_q_g, a_k_g, a_sink, b_cq_g, b_ckv_g, w_uq, w_ukv, b_q_g, b_k_g,
           c_q_g, c_k_g, w_out, router_w, router_b, w_gu, b_gu, w_dn, b_dn):
    consts = _constants()
    cond = jnp.concatenate([c_ctx[None, :], c, jnp.zeros((3, D), F32)], axis=0)
    mod = _modulation(cond, w_mod, b_mod).reshape(DEPTH, 8, 1, 6 * D)
    x_ctx, x_lat, x_lat_off = x_prompt.reshape(T_CTX, D), x_sample.reshape(T_LAT, D), 0

    lw = _weights(norm1_g, norm2_g, w_in, a_q_g, a_k_g, b_cq_g, b_ckv_g, w_uq, w_ukv, b_q_g, b_k_g,
                  c_q_g, c_k_g, w_out, router_w, router_b)
    ckb, cvb = _cache_kv(cache_b_ckv, cache_b_krope, lw, consts)
    merge_heads = lambda a: a.reshape(N_LAT_SEQ, DEPTH, PAST, 2 * HD).astype(BF16)
    cache = dict(ka=merge_heads(cache_a_k), va=merge_heads(cache_a_v), kb=ckb, vb=cvb,
                 kc=merge_heads(cache_c_k), vc=merge_heads(cache_c_v))

    states = ()
    names = ["qa", "ka", "va", "qb", "kb", "vb", "qc", "kc", "vc"]
    for l in range(DEPTH):
        outs = _projection(False, l, x_ctx, 0, mod, lw, consts, states)
        p_ctx = dict(zip(names, outs[:9]))
        states = tuple(outs[9:])
        p_lat = dict(zip(names, _projection(True, l, x_lat, x_lat_off, mod, lw, consts)))
        mix_ctx = _attention_ctx(l, a_sink, p_ctx)
        mix_lat = _attention_lat(l, a_sink, p_lat, cache)
        x_mid, h2p, gate_slab, route, counts = _post_attention(l, mix_ctx, mix_lat, x_ctx, x_lat, x_lat_off, mod, lw,
                                                               consts)
        x_ctx, x_lat = _moe(l, x_mid, h2p, gate_slab, route, counts, mod, w_gu, b_gu, w_dn, b_dn)
    y_ctx, y_lat = x_ctx, x_lat

    return (y_ctx.reshape(N_CTX_SEQ, CTX_LEN, D), y_lat.reshape(N_LAT_SEQ, LAT_LEN, D)) + states
```

```python
import functools

import jax
import jax.numpy as jnp
import numpy as np
from jax import lax
from jax.experimental import pallas as pl
from jax.experimental.pallas import tpu as pltpu
from jax.experimental.pallas import tpu_sc as plsc

F32 = jnp.float32
BF16 = jnp.bfloat16

D = 1024
DEPTH = 2
N_CTX_SEQ, CTX_LEN = 16, 256
N_LAT_SEQ, LAT_LEN = 4, 2048
PAST = 512
T_CTX = N_CTX_SEQ * CTX_LEN
T_LAT = N_LAT_SEQ * LAT_LEN
T_ALL = T_CTX + T_LAT
GRID_W = 64
HD = 64
WINDOW = 128
B_NOPE, B_ROPE, B_V = 64, 32, 128
B_QK = B_NOPE + B_ROPE
B_Q_LORA, B_KV_LORA = 384, 256
N_EXPERTS, TOP_K = 32, 4
D_FF = 1024
SWIGLU_LIMIT = 7.0
SWIGLU_ALPHA = 1.702
ROPE_THETA = 10000.0
EPS = 1e-6
NEG_INF = -1e30

TM = 256
TP = 1024
TQ_LAT = 1024
TQ_CTX = 256
SEQ_PER_STEP = TQ_CTX // CTX_LEN
LANE = 128
BP = 128
N_CTX_TILES = T_CTX // TM
N_LAT_TILES = T_LAT // TM
N_TILES = T_ALL // TM
LAT_TILES_PER_SEQ = LAT_LEN // TM
MOE_TM = 256
MOE_STEP = 4 * MOE_TM
FF_CHUNK = 256
MOE_STEPS = T_ALL * TOP_K // MOE_STEP + N_EXPERTS
MOE_ROWS = MOE_STEPS * MOE_STEP
DW = D // 2
VMEM_LIMIT = 56 * 1024 * 1024

C_QA, C_KA, C_VA, C_CQ, C_CKV, C_QC, C_KC, C_VC, C_KR, C_END = (
    0, 256, 384, 512, 896, 1152, 1408, 1536, 1664, 1792)


def _dot(a, b):
    return jnp.dot(a, b, preferred_element_type=F32)


def _dot_t(a, b):
    return lax.dot_general(a, b, (((1,), (1,)), ((), ())), preferred_element_type=F32)


def _rms(x, g):
    return x * lax.rsqrt(jnp.mean(x * x, axis=-1, keepdims=True) + EPS) * g


def _seg_norm(x, seg, g, n):
    ss = _dot((x * x).astype(BF16), seg)
    return x * lax.rsqrt(ss * (1.0 / n) + EPS) * g


def _rope(x, cos, sin, first, sh):
    w = x.shape[1]
    fwd = pltpu.roll(x, w - sh, 1)
    bwd = pltpu.roll(x, sh, 1)
    return x * cos + jnp.where(first > 0.5, fwd, bwd) * sin


MOD_BN = 1536


def _mod_kernel(c_ref, w_ref, b_ref, o_ref):
    c = c_ref[...]
    s = (c * jax.nn.sigmoid(c)).astype(BF16)
    o_ref[...] = _dot(s, w_ref[...].astype(BF16)) + b_ref[...]


def _modulation(cond, w_mod, b_mod):
    return pl.pallas_call(
        _mod_kernel,
        grid=(DEPTH, 6 * D // MOD_BN),
        in_specs=[
            pl.BlockSpec((8, D), lambda l, j: (0, 0)),
            pl.BlockSpec((None, D, MOD_BN), lambda l, j: (l, 0, j)),
            pl.BlockSpec((None, 1, MOD_BN), lambda l, j: (l, 0, j)),
        ],
        out_specs=pl.BlockSpec((None, 8, MOD_BN), lambda l, j: (l, 0, j)),
        out_shape=jax.ShapeDtypeStruct((DEPTH, 8, 6 * D), F32),
        compiler_params=pltpu.CompilerParams(
            dimension_semantics=("arbitrary", "arbitrary"), vmem_limit_bytes=VMEM_LIMIT),
        name="modulation",
    )(cond, w_mod, b_mod.reshape(DEPTH, 1, 6 * D))


def _proj_kernel(is_lat, n_aliased, *refs):
    (x_ref, mod_ref, n1g_ref, win_ref, seg64_ref, seg96_ref, aqg_ref, akg_ref, cqg_ref, ckg_ref,
     bcqg_ref, bckvg_ref, bqg_ref, bkg_ref, wuq_ref, wuk_ref, wuv_ref, plc_ref) = refs[:18]
    refs = refs[18 + n_aliased:]
    if is_lat:
        (cos64_ref, sin64_ref, cos96_ref, sin96_ref, f64_ref, f96_ref) = refs[:6]
        refs = refs[6:]
    (qa_ref, ka_ref, va_ref, qb_ref, kb_ref, vb_ref, qc_ref, kc_ref, vc_ref) = refs[:9]
    refs = refs[9:]
    if not is_lat:
        if n_aliased == 0:
            for r in refs:
                r[:, 1:] = jnp.zeros((SEQ_PER_STEP, DEPTH - 1) + r.shape[2:], F32)
            refs = [r.at[:, 0] for r in refs]
        (kas_ref, vas_ref, ckvs_ref, krs_ref, kcs_ref, vcs_ref) = refs

    x = x_ref[...]
    m = mod_ref[...]
    sh1, sc1 = m[:, 0:D], m[:, D:2 * D]
    h = (_rms(x, n1g_ref[...]) * (1.0 + sc1) + sh1).astype(BF16)

    def proj(a, b):
        return _dot(h, win_ref[:, a:b])

    def rope64(t):
        wd = t.shape[1]
        return _rope(t, cos64_ref[:, :wd], sin64_ref[:, :wd], f64_ref[:, :wd], 16) if is_lat else t

    def rope96(t):
        return _rope(t, cos96_ref[...], sin96_ref[...], f96_ref[...], 8) if is_lat else t

    seg64 = seg64_ref[...]
    seg64h = seg64_ref[0:2 * HD, 0:2 * HD]
    seg96 = seg96_ref[...]

    def seg_sum(t, seg):
        return _dot((t * t).astype(BF16), seg)

    def seg_finish(t, ss, g, n):
        return t * lax.rsqrt(ss * (1.0 / n) + EPS) * g

    p_cq, p_ckv, p_kr = proj(C_CQ, C_CKV), proj(C_CKV, C_QC), proj(C_KR, C_END)
    p_qa, p_ka, p_qc, p_kc = proj(C_QA, C_KA), proj(C_KA, C_VA), proj(C_QC, C_KC), proj(C_KC, C_VC)
    p_va, p_vc = proj(C_VA, C_CQ), proj(C_VC, C_KR)

    cq = _rms(p_cq, bcqg_ref[...]).astype(BF16)
    ckv = _rms(p_ckv, bckvg_ref[...])
    ckv16 = ckv.astype(BF16)
    u_q = _dot(cq, wuq_ref[...])
    u_k = _dot(ckv16, wuk_ref[...]) + _dot(p_kr.astype(BF16), plc_ref[...])
    u_v = _dot(ckv16, wuv_ref[...])
    ss_qa, ss_ka = seg_sum(p_qa, seg64), seg_sum(p_ka, seg64h)
    ss_qc, ss_kc = seg_sum(p_qc, seg64), seg_sum(p_kc, seg64h)
    ss_qb, ss_kb = seg_sum(u_q, seg96), seg_sum(u_k, seg96)

    def store_state(ref, t):
        for s in range(SEQ_PER_STEP):
            ref[s] = t[s * CTX_LEN:(s + 1) * CTX_LEN]

    def store_kv_state(ref, t):
        for s in range(SEQ_PER_STEP):
            for kv in range(2):
                ref[s, :, kv, :] = t[s * CTX_LEN:(s + 1) * CTX_LEN, kv * HD:(kv + 1) * HD]

    if not is_lat:
        store_kv_state(vas_ref, p_va)
        store_kv_state(vcs_ref, p_vc)
        store_state(ckvs_ref, ckv)
        store_state(krs_ref, p_kr[:, 0:B_ROPE])
    va_ref[...] = p_va.astype(BF16)
    vc_ref[...] = p_vc.astype(BF16)
    vb_ref[...] = u_v.astype(BF16)

    t = seg_finish(p_qa, ss_qa, aqg_ref[...], HD)
    qa_ref[...] = (rope64(t) * (HD ** -0.5 * LOG2E)).astype(BF16)
    t = seg_finish(p_ka, ss_ka, akg_ref[...], HD)
    if not is_lat:
        store_kv_state(kas_ref, t)
    ka_ref[...] = rope64(t).astype(BF16)
    t = seg_finish(p_qc, ss_qc, cqg_ref[...], HD)
    qc_ref[...] = (rope64(t) * (HD ** -0.5 * LOG2E)).astype(BF16)
    t = seg_finish(p_kc, ss_kc, ckg_ref[...], HD)
    if not is_lat:
        store_kv_state(kcs_ref, t)
    kc_ref[...] = rope64(t).astype(BF16)
    t = seg_finish(u_q, ss_qb, bqg_ref[...], B_QK)
    qb_ref[...] = (rope96(t) * (B_QK ** -0.5 * LOG2E)).astype(BF16)
    t = seg_finish(u_k, ss_kb, bkg_ref[...], B_QK)
    kb_ref[...] = rope96(t).astype(BF16)


def _full(shape):
    nd = len(shape)
    return pl.BlockSpec(shape, lambda i: (0,) * nd)


STATE_TAILS = ((2, HD), (2, HD), (B_KV_LORA,), (B_ROPE,), (2, HD), (2, HD))


def _layer_spec(a, l):
    nd = a.ndim - 1
    return pl.BlockSpec((None,) + a.shape[1:], lambda *_: (l,) + (0,) * nd)


VEC_ROWS = dict(n1g=(0, D), n2g=(1, D), aqg=(2, 4 * HD), akg=(3, 2 * HD), cqg=(4, 4 * HD), ckg=(5, 2 * HD),
                bcqg=(6, B_Q_LORA), bckvg=(7, B_KV_LORA), bqg=(8, 4 * BP), bkg=(9, 4 * BP))


def _vec_spec(l, name):
    row, width = VEC_ROWS[name]
    return pl.BlockSpec((None, None, 1, width), lambda *_: (l, row, 0, 0))


def _projection(is_lat, l, x_src, x_off, mod, lw, consts, prev_states=()):
    assert is_lat or bool(prev_states) == (l > 0)
    t_rows = T_LAT if is_lat else T_CTX
    TQ = TQ_LAT if is_lat else TQ_CTX
    n_tiles = t_rows // TQ
    if is_lat:
        mod_map = lambda i: (l, 1 + i // (LAT_LEN // TQ), 0, 0)
    else:
        mod_map = lambda i: (l, 0, 0, 0)
    vec = lw["vec"]
    gain = lambda name: (vec, _vec_spec(l, name))
    whole = lambda a: (a, _full(a.shape))
    layer = lambda a: (a, _layer_spec(a, l))
    pairs = [(x_src, pl.BlockSpec((TQ, D), lambda i: (i + x_off, 0))),
             (mod, pl.BlockSpec((None, None, 1, 6 * D), mod_map)),
             gain("n1g"), layer(lw["w_in"]), whole(consts["seg64"]), whole(consts["seg96"]),
             gain("aqg"), gain("akg"), gain("cqg"), gain("ckg"), gain("bcqg"), gain("bckvg"), gain("bqg"), gain("bkg"),
             layer(lw["w_uq"]), layer(lw["w_uk"]), layer(lw["w_uv"]), whole(consts["place"])]
    ins = [a for a, _ in pairs]
    in_specs = [s for _, s in pairs]
    n_plain = len(ins)
    ins += list(prev_states)
    in_specs += [pl.BlockSpec(memory_space=pl.ANY) for _ in prev_states]
    if is_lat:
        tabs = [consts["cos64"], consts["sin64"], consts["cos96"], consts["sin96"]]
        ins += tabs + [consts["first64"], consts["first96"]]
        in_specs += [pl.BlockSpec((TQ, a.shape[1]), lambda i: (i % (LAT_LEN // TQ), 0)) for a in tabs]
        in_specs += [_full(consts["first64"].shape), _full(consts["first96"].shape)]
    widths = [4 * HD, 2 * HD, 2 * HD, 4 * BP, 4 * BP, 4 * B_V, 4 * HD, 2 * HD, 2 * HD]
    out_shape = [jax.ShapeDtypeStruct((t_rows, w), BF16) for w in widths]
    out_specs = [pl.BlockSpec((TQ, w), lambda i: (i, 0)) for w in widths]
    if not is_lat:
        for tail in STATE_TAILS:
            zeros = (0,) * len(tail)
            out_shape.append(jax.ShapeDtypeStruct((N_CTX_SEQ, DEPTH, CTX_LEN) + tail, F32))
            if prev_states:
                out_specs.append(pl.BlockSpec((SEQ_PER_STEP, None, CTX_LEN) + tail, lambda i, z=zeros: (i, l, 0) + z))
            else:
                out_specs.append(pl.BlockSpec((SEQ_PER_STEP, DEPTH, CTX_LEN) + tail, lambda i, z=zeros: (i, 0, 0) + z))
    aliases = {n_plain + k: len(widths) + k for k in range(len(prev_states))}
    return pl.pallas_call(
        functools.partial(_proj_kernel, is_lat, len(prev_states)),
        grid=(n_tiles,),
        in_specs=in_specs,
        out_specs=out_specs,
        out_shape=out_shape,
        input_output_aliases=aliases,
        compiler_params=pltpu.CompilerParams(
            dimension_semantics=("arbitrary",), vmem_limit_bytes=VMEM_LIMIT),
        name="proj_lat" if is_lat else "proj_ctx",
    )(*ins)


def _cache_kv_kernel(ckv_ref, kr_ref, seg96_ref, bkg_ref, wuk_ref, wuv_ref, plc_ref, kb_ref, vb_ref):
    ckv16 = ckv_ref[...].astype(BF16)
    vb_ref[...] = _dot(ckv16, wuv_ref[...]).astype(BF16)
    kpre = _dot(ckv16, wuk_ref[...]) + _dot(kr_ref[...].astype(BF16), plc_ref[0:B_ROPE, :])
    kb_ref[...] = _seg_norm(kpre, seg96_ref[...], bkg_ref[...], B_QK).astype(BF16)


def _cache_kv(cache_b_ckv, cache_b_krope, lw, consts):
    def c_map(l, r):
        return (r // (PAST // TM), l, r % (PAST // TM), 0)

    def w_spec(a):
        return pl.BlockSpec((None,) + a.shape[1:], lambda l, r: (l,) + (0,) * (a.ndim - 1))

    def s_spec(a):
        return pl.BlockSpec(a.shape, lambda l, r: (0,) * a.ndim)

    rows = N_LAT_SEQ * PAST
    return pl.pallas_call(
        _cache_kv_kernel,
        grid=(DEPTH, rows // TM),
        in_specs=[pl.BlockSpec((None, None, TM, B_KV_LORA), c_map), pl.BlockSpec((None, None, TM, B_ROPE), c_map),
                  s_spec(consts["seg96"]),
                  pl.BlockSpec((None, None, 1, 4 * BP), lambda l, r: (l, VEC_ROWS["bkg"][0], 0, 0)),
                  w_spec(lw["w_uk"]), w_spec(lw["w_uv"]),
                  s_spec(consts["place"])],
        out_specs=[pl.BlockSpec((None, TM, 4 * BP), lambda l, r: (l, r, 0)),
                   pl.BlockSpec((None, TM, 4 * B_V), lambda l, r: (l, r, 0))],
        out_shape=[jax.ShapeDtypeStruct((DEPTH, rows, 4 * BP), BF16),
                   jax.ShapeDtypeStruct((DEPTH, rows, 4 * B_V), BF16)],
        compiler_params=pltpu.CompilerParams(
            dimension_semantics=("arbitrary", "arbitrary"), vmem_limit_bytes=VMEM_LIMIT),
        name="cache_kv",
    )(cache_b_ckv, cache_b_krope, consts["seg96"], lw["vec"], lw["w_uk"], lw["w_uv"], consts["place"])


def _scores(q, segs):
    scores = []
    for k, _, mask in segs:
        s = _dot_t(q, k)
        if mask is not None:
            s = jnp.where(mask, s, NEG_INF)
        scores.append(s)
    return scores


def _softmax_pv(scores, segs, sink, num_cols, den_col):
    m = None
    for s in scores:
        sm = jnp.max(s, axis=-1, keepdims=True)
        m = sm if m is None else jnp.maximum(m, sm)
    if sink is not None:
        m = jnp.maximum(m, sink)
    acc = None
    den = None
    for s, (_, v, _) in zip(scores, segs):
        e = jnp.exp2(s - m)
        if den_col is None:
            d = jnp.sum(e, axis=-1, keepdims=True)
            den = d if den is None else den + d
        o = _dot(e.astype(BF16), v)
        acc = o if acc is None else acc + o
    if den_col is not None:
        den = acc[:, den_col:den_col + 1]
        acc = acc[:, num_cols]
    if sink is not None:
        den = den + jnp.exp2(sink - m)
    return acc / den


HEAD_LOOKAHEAD = 1


def _run_heads(jobs, o_ref):
    ready = [_scores(job[0](), job[1]()) for job in jobs[:HEAD_LOOKAHEAD]]
    for n, (_, load_segs, sink, col, num_cols, den_col) in enumerate(jobs):
        scores = ready.pop(0)
        if n + HEAD_LOOKAHEAD < len(jobs):
            ahead = jobs[n + HEAD_LOOKAHEAD]
            ready.append(_scores(ahead[0](), ahead[1]()))
        o = _softmax_pv(scores, load_segs(values=True), sink, num_cols, den_col)
        o_ref[:, col:col + o.shape[1]] = o.astype(BF16)


OUT_B = 4 * HD
OUT_C = 4 * HD + 4 * B_V
LOG2E = 1.4426950408889634


def _head_job(q_ref, q_cols, seg_refs, kv_cols, v_cols, sink, col, rows=None, masks=None, mxu_sums=True):
    width = v_cols.stop - v_cols.start
    if not mxu_sums:
        load_cols, num_cols, den_col = v_cols, None, None
    elif width == HD:
        pair = v_cols.start // LANE * LANE
        load_cols = slice(pair, pair + LANE)
        lo = v_cols.start - pair
        num_cols, den_col = slice(lo, lo + HD), (lo + HD) % LANE
    else:
        load_cols, num_cols, den_col = v_cols, slice(0, width), width
    if sink is not None:
        sink = sink * LOG2E

    def with_ones(v):
        if not mxu_sums:
            return v
        if width != HD:
            return jnp.concatenate([v, jnp.ones_like(v)], axis=1)
        lane = lax.broadcasted_iota(jnp.int32, (1, LANE), 1)
        keep = jnp.where((lane >= num_cols.start) & (lane < num_cols.stop), 1.0, 0.0).astype(BF16)
        return v * keep + (1.0 - keep)

    def load_segs(values=False):
        segs = []
        for n, (k_ref, v_ref) in enumerate(seg_refs):
            r = rows if (rows is not None and n == 0) else slice(None)
            if values:
                segs.append((None, with_ones(v_ref[r, load_cols]), None))
            else:
                segs.append((k_ref[r, kv_cols], None, None if masks is None else masks[n]))
        return segs

    return (lambda: q_ref[:, q_cols]), load_segs, sink, col, num_cols, den_col


def _attn_ctx_kernel(l, sink_ref, qa_ref, ka_ref, va_ref, qb_ref, kb_ref, vb_ref, qc_ref, kc_ref, vc_ref, o_ref):
    jobs = []
    for h in range(4):
        cs = slice(h // 2 * HD, (h // 2 + 1) * HD)
        jobs.append(_head_job(qa_ref, slice(h * HD, (h + 1) * HD), [(ka_ref, va_ref)], cs, cs, sink_ref[l, h],
                              h * HD, mxu_sums=False))
    for h in range(4):
        ks, vs = slice(h * BP, (h + 1) * BP), slice(h * B_V, (h + 1) * B_V)
        jobs.append(_head_job(qb_ref, ks, [(kb_ref, vb_ref)], ks, vs, None, OUT_B + h * B_V, mxu_sums=False))
    for h in range(4):
        cs = slice(h // 2 * HD, (h // 2 + 1) * HD)
        jobs.append(_head_job(qc_ref, slice(h * HD, (h + 1) * HD), [(kc_ref, vc_ref)], cs, cs, None, OUT_C + h * HD,
                              mxu_sums=False))
    _run_heads(jobs, o_ref)


def _attention_ctx(l, sink, p):
    names = ["qa", "ka", "va", "qb", "kb", "vb", "qc", "kc", "vc"]
    ins = [p[n] for n in names]
    in_specs = [pl.BlockSpec(memory_space=pltpu.SMEM)]
    in_specs += [pl.BlockSpec((CTX_LEN, a.shape[1]), lambda i: (i, 0)) for a in ins]
    return pl.pallas_call(
        functools.partial(_attn_ctx_kernel, l),
        grid=(N_CTX_SEQ,),
        in_specs=in_specs,
        out_specs=pl.BlockSpec((CTX_LEN, D), lambda i: (i, 0)),
        out_shape=jax.ShapeDtypeStruct((T_CTX, D), BF16),
        compiler_params=pltpu.CompilerParams(
            dimension_semantics=("arbitrary",), vmem_limit_bytes=VMEM_LIMIT),
        name="attn_ctx",
    )(sink, *ins)


WIN_SPAN = TM + 2 * WINDOW


def _attn_lat_kernel(l, sink_ref, qa_ref, qb_ref, qc_ref, ka_ref, va_ref, kb_ref, vb_ref, kc_ref, vc_ref,
                     cka_ref, cva_ref, ckb_ref, cvb_ref, ckc_ref, cvc_ref, o_ref):
    qi = pl.program_id(1)
    ws = pl.multiple_of(jnp.clip(qi * TM - WINDOW, 0, LAT_LEN - WIN_SPAN), WINDOW)
    qpos = qi * TM + lax.broadcasted_iota(jnp.int32, (TM, WIN_SPAN), 0)
    kpos = ws + lax.broadcasted_iota(jnp.int32, (TM, WIN_SPAN), 1)
    band = jnp.abs(qpos - kpos) <= WINDOW
    jobs = []
    for h in range(4):
        cs = slice(h // 2 * HD, (h // 2 + 1) * HD)
        jobs.append(_head_job(qa_ref, slice(h * HD, (h + 1) * HD), [(ka_ref, va_ref), (cka_ref, cva_ref)], cs, cs,
                              sink_ref[l, h], h * HD, rows=pl.ds(ws, WIN_SPAN), masks=(band, None)))
    for h in range(4):
        ks, vs = slice(h * BP, (h + 1) * BP), slice(h * B_V, (h + 1) * B_V)
        jobs.append(_head_job(qb_ref, ks, [(kb_ref, vb_ref), (ckb_ref, cvb_ref)], ks, vs, None, OUT_B + h * B_V))
    for h in range(4):
        cs = slice(h // 2 * HD, (h // 2 + 1) * HD)
        jobs.append(_head_job(qc_ref, slice(h * HD, (h + 1) * HD), [(kc_ref, vc_ref), (ckc_ref, cvc_ref)], cs, cs,
                              None, OUT_C + h * HD))
    _run_heads(jobs, o_ref)


def _attention_lat(l, sink, p, cache):
    q_ins = [p["qa"], p["qb"], p["qc"]]
    kv_ins = [p[n] for n in ("ka", "va", "kb", "vb", "kc", "vc")]
    c_names = ("ka", "va", "kb", "vb", "kc", "vc")
    c_ins = [cache[n] for n in c_names]
    in_specs = [pl.BlockSpec(memory_space=pltpu.SMEM)]
    in_specs += [pl.BlockSpec((TM, a.shape[1]), lambda b, i: (b * LAT_TILES_PER_SEQ + i, 0)) for a in q_ins]
    in_specs += [pl.BlockSpec((LAT_LEN, a.shape[1]), lambda b, i: (b, 0)) for a in kv_ins]
    for n, a in zip(c_names, c_ins):
        if n in ("kb", "vb"):
            in_specs.append(pl.BlockSpec((None, PAST, a.shape[-1]), lambda b, i: (l, b, 0)))
        else:
            in_specs.append(pl.BlockSpec((None, None, PAST, a.shape[-1]), lambda b, i: (b, l, 0, 0)))
    return pl.pallas_call(
        functools.partial(_attn_lat_kernel, l),
        grid=(N_LAT_SEQ, LAT_TILES_PER_SEQ),
        in_specs=in_specs,
        out_specs=pl.BlockSpec((TM, D), lambda b, i: (b * LAT_TILES_PER_SEQ + i, 0)),
        out_shape=jax.ShapeDtypeStruct((T_LAT, D), BF16),
        compiler_params=pltpu.CompilerParams(
            dimension_semantics=("arbitrary", "arbitrary"), vmem_limit_bytes=VMEM_LIMIT),
        name="attn_lat",
    )(sink, *q_ins, *kv_ins, *c_ins)


def _pack_rows(x):
    half = x.shape[1] // 2
    r = x.astype(BF16).astype(F32)
    hi = lax.bitcast_convert_type(r[:, :half], jnp.int32)
    lo = lax.bitcast_convert_type(r[:, half:], jnp.int32)
    return jnp.bitwise_or(hi, lax.shift_right_logical(lo, 16))


def _unpack_rows(p):
    a = lax.bitcast_convert_type(jnp.bitwise_and(p, -65536), F32)
    b = lax.bitcast_convert_type(lax.shift_left(p, 16), F32)
    return jnp.concatenate([a, b], axis=1).astype(BF16)


ROUTE_ROWS = 16


def _post_kernel(mc_ref, ml_ref, xc_ref, xl_ref, mod_ref, n2g_ref, wout_ref, rwt_ref, rb_ref, tri_ref,
                 xo_ref, h2_ref, gate_ref, route_ref, cnt_ref, run_ref):
    i = pl.program_id(0)

    @pl.when(i == 0)
    def _():
        run_ref[...] = jnp.zeros_like(run_ref)
        cnt_ref[...] = jnp.zeros_like(cnt_ref)

    is_ctx = i < T_CTX // TP
    m = mod_ref[...]
    g1, sh2, sc2 = m[:, 2 * D:3 * D], m[:, 3 * D:4 * D], m[:, 4 * D:5 * D]

    mix = jnp.where(is_ctx, mc_ref[...], ml_ref[...])
    x = jnp.where(is_ctx, xc_ref[...], xl_ref[...]) + g1 * _dot(mix, wout_ref[...])
    xo_ref[...] = x
    h2 = _rms(x, n2g_ref[...]) * (1.0 + sc2) + sh2
    hi = h2.astype(BF16)
    h2_ref[...] = _pack_rows(h2)
    lo = (h2 - hi.astype(F32)).astype(BF16)
    a = _dot_t(rwt_ref[...], hi)
    b = _dot_t(rwt_ref[0:N_EXPERTS, :], lo)
    logits = a[0:N_EXPERTS] + a[N_EXPERTS:] + b + rb_ref[...]
    eidx = lax.broadcasted_iota(jnp.int32, logits.shape, 0).astype(F32)
    work = logits
    vals, idxs, hots = [], [], []
    for _ in range(TOP_K):
        v = jnp.max(work, axis=0, keepdims=True)
        idx = jnp.min(jnp.where(work == v, eidx, float(N_EXPERTS)), axis=0, keepdims=True)
        hot = eidx == idx
        vals.append(v)
        idxs.append(idx)
        hots.append(hot)
        work = jnp.where(hot, -jnp.inf, work)
    es = [jnp.exp(v - vals[0]) for v in vals]
    den = es[0] + es[1] + es[2] + es[3]
    gates = [e / den for e in es]
    sel = jnp.where(hots[0] | hots[1] | hots[2] | hots[3], 1.0, 0.0)
    run = run_ref[:, 0:1]
    before = _dot(sel.astype(BF16), tri_ref[...]) + run
    run_new = jnp.broadcast_to(run + jnp.sum(sel, axis=1, keepdims=True), run_ref.shape)
    run_ref[...] = run_new
    cnt_ref[...] = run_new
    ranks = [jnp.sum(jnp.where(hots[k], before, 0.0), axis=0, keepdims=True) for k in range(TOP_K)]
    pad = [jnp.zeros((ROUTE_ROWS - 3 * TOP_K, TP), F32)]
    route_ref[...] = jnp.concatenate(gates + idxs + ranks + pad, axis=0)
    gate_ref[...] = jnp.concatenate(gates + [jnp.zeros((LANE - TOP_K, TP), F32)], axis=0).T


def _mod_row(i, tm=TM):
    n_ctx = T_CTX // tm
    return jnp.where(i < n_ctx, 0, 1 + (i - n_ctx) // (LAT_LEN // tm))


def _ctx_tile(i):
    return jnp.minimum(i, T_CTX // TP - 1)


def _lat_tile(i):
    return jnp.maximum(i - T_CTX // TP, 0)


def _post_attention(l, mix_ctx, mix_lat, x_ctx, x_lat, x_lat_off, mod, lw, consts):
    ins = [mix_ctx, mix_lat, x_ctx, x_lat, mod, lw["vec"], lw["w_out"], lw["rwt"], lw["rb"], consts["tri"]]
    in_specs = [
        pl.BlockSpec((TP, D), lambda i: (_ctx_tile(i), 0)),
        pl.BlockSpec((TP, D), lambda i: (_lat_tile(i), 0)),
        pl.BlockSpec((TP, D), lambda i: (_ctx_tile(i), 0)),
        pl.BlockSpec((TP, D), lambda i: (_lat_tile(i) + x_lat_off, 0)),
        pl.BlockSpec((None, None, 1, 6 * D), lambda i: (l, _mod_row(i, TP), 0, 0)),
        _vec_spec(l, "n2g"),
    ] + [_layer_spec(a, l) for a in ins[6:9]] + [_full(consts["tri"].shape)]
    return pl.pallas_call(
        _post_kernel,
        grid=(T_ALL // TP,),
        in_specs=in_specs,
        out_specs=[pl.BlockSpec((TP, D), lambda i: (i, 0)), pl.BlockSpec((TP, DW), lambda i: (i, 0)),
                   pl.BlockSpec((TP, LANE), lambda i: (i, 0)), pl.BlockSpec((ROUTE_ROWS, TP), lambda i: (0, i)),
                   pl.BlockSpec((N_EXPERTS, LANE), lambda i: (0, 0))],
        out_shape=[jax.ShapeDtypeStruct((T_ALL, D), F32), jax.ShapeDtypeStruct((T_ALL, DW), jnp.int32),
                   jax.ShapeDtypeStruct((T_ALL, LANE), F32), jax.ShapeDtypeStruct((ROUTE_ROWS, T_ALL), F32),
                   jax.ShapeDtypeStruct((N_EXPERTS, LANE), F32)],
        scratch_shapes=[pltpu.VMEM((N_EXPERTS, LANE), F32)],
        compiler_params=pltpu.CompilerParams(
            dimension_semantics=("arbitrary",), vmem_limit_bytes=VMEM_LIMIT),
        name="post_attn",
    )(*ins)


def _expert_kernel(l, te_ref, nu_ref, nxt_ref, nv_ref, slot_ref, x_ref, wgu_hbm, bgu_ref, wdn_hbm, bdn_ref, y_ref,
                   wgu32, wdn32, wgu16, wdn16, sem):
    i = pl.program_id(0)
    prev = te_ref[jnp.maximum(i - 1, 0)]
    new_expert = jnp.logical_or(i == 0, te_ref[i] != prev)
    slot = slot_ref[i]

    def weight_copies(e, s):
        return (pltpu.make_async_copy(wgu_hbm.at[l, e], wgu32.at[s], sem.at[s, 0]),
                pltpu.make_async_copy(wdn_hbm.at[l, e], wdn32.at[s], sem.at[s, 1]))

    @pl.when(i == 0)
    def _():
        for cp in weight_copies(te_ref[0], slot):
            cp.start()

    @pl.when(new_expert)
    def _():
        for cp in weight_copies(te_ref[i], slot):
            cp.wait()

        @pl.when(nxt_ref[i] >= 0)
        def _():
            for cp in weight_copies(nxt_ref[i], 1 - slot):
                cp.start()

    def compute(rows, convert):
        x = _unpack_rows(x_ref[0:rows, :])

        def gate_up(a):
            if convert:
                for c in (slice(a, a + FF_CHUNK), slice(D_FF + a, D_FF + a + FF_CHUNK)):
                    wgu16[:, c] = wgu32[slot, :, c].astype(BF16)
                wdn16[a:a + FF_CHUNK, :] = wdn32[slot, a:a + FF_CHUNK, :].astype(BF16)
            b = a + FF_CHUNK
            g = _dot(x, wgu16[:, a:b]) + bgu_ref[:, a:b]
            up = _dot(x, wgu16[:, D_FF + a:D_FF + b]) + bgu_ref[:, D_FF + a:D_FF + b]
            return g, up

        def activate(g, up):
            g = jnp.minimum(g, SWIGLU_LIMIT)
            up = jnp.clip(up, -SWIGLU_LIMIT, SWIGLU_LIMIT)
            return ((up + 1.0) * (g * jax.nn.sigmoid(SWIGLU_ALPHA * g))).astype(BF16)

        starts = list(range(0, D_FF, FF_CHUNK))
        acc = None
        pre = gate_up(starts[0])
        for n, a in enumerate(starts):
            hid = activate(*pre)
            if n + 1 < len(starts):
                pre = gate_up(starts[n + 1])
            o = _dot(hid, wdn16[a:a + FF_CHUNK, :])
            acc = o if acc is None else acc + o
        y_ref[0:rows, :] = _pack_rows(acc + bdn_ref[...])

    used = i < nu_ref[0]
    for n_valid in range(1, MOE_STEP // MOE_TM + 1):
        for convert in (False, True):
            @pl.when(used & (nv_ref[i] == n_valid) & (new_expert == convert))
            def _():
                compute(n_valid * MOE_TM, convert)


def _experts(l, tile_expert, n_used, next_expert, n_valid, slot, x_sorted, w_gu, b_gu, w_dn, b_dn):
    def row_map(i, te, nu, nxt, nv, sl):
        return (jnp.minimum(i, nu[0] - 1), 0)

    def b_map(i, te, nu, nxt, nv, sl):
        return (l, te[i], 0, 0)

    grid_spec = pltpu.PrefetchScalarGridSpec(
        num_scalar_prefetch=5,
        grid=(MOE_STEPS,),
        in_specs=[
            pl.BlockSpec((MOE_STEP, DW), row_map),
            pl.BlockSpec(memory_space=pl.ANY),
            pl.BlockSpec((None, None, 1, 2 * D_FF), b_map),
            pl.BlockSpec(memory_space=pl.ANY),
            pl.BlockSpec((None, None, 1, D), b_map),
        ],
        out_specs=pl.BlockSpec((MOE_STEP, DW), row_map),
        scratch_shapes=[pltpu.VMEM((2, D, 2 * D_FF), F32), pltpu.VMEM((2, D_FF, D), F32),
                        pltpu.VMEM((D, 2 * D_FF), BF16), pltpu.VMEM((D_FF, D), BF16),
                        pltpu.SemaphoreType.DMA((2, 2))],
    )
    return pl.pallas_call(
        functools.partial(_expert_kernel, l),
        grid_spec=grid_spec,
        out_shape=jax.ShapeDtypeStruct((MOE_ROWS, DW), jnp.int32),
        compiler_params=pltpu.CompilerParams(
            dimension_semantics=("arbitrary",), vmem_limit_bytes=VMEM_LIMIT),
        name="experts",
    )(tile_expert, n_used, next_expert, n_valid, slot, x_sorted, w_gu, b_gu.reshape(DEPTH, N_EXPERTS, 1, 2 * D_FF),
      w_dn, b_dn.reshape(DEPTH, N_EXPERTS, 1, D))


def _combine_kernel(x_ref, y_ref, route_ref, mod_ref, o_ref):
    g2 = mod_ref[...][:, 5 * D:6 * D]
    r = route_ref[...]
    acc = r[:, 0:1] * _unpack_rows(y_ref[0]).astype(F32)
    for k in range(1, TOP_K):
        acc = acc + r[:, k:k + 1] * _unpack_rows(y_ref[k]).astype(F32)
    o_ref[...] = x_ref[...] + g2 * acc


def _combine(l, first_tile, n_tiles, x_all, y_tok, gates, mod):
    return pl.pallas_call(
        _combine_kernel,
        grid=(n_tiles,),
        in_specs=[
            pl.BlockSpec((TM, D), lambda i: (i + first_tile, 0)),
            pl.BlockSpec((TOP_K, TM, DW), lambda i: (0, i, 0)),
            pl.BlockSpec((TM, LANE), lambda i: (i + first_tile, 0)),
            pl.BlockSpec((None, None, 1, 6 * D), lambda i: (l, _mod_row(i + first_tile), 0, 0)),
        ],
        out_specs=pl.BlockSpec((TM, D), lambda i: (i, 0)),
        out_shape=jax.ShapeDtypeStruct((n_tiles * TM, D), F32),
        compiler_params=pltpu.CompilerParams(
            dimension_semantics=("arbitrary",), vmem_limit_bytes=VMEM_LIMIT),
        name="combine",
    )(x_all, y_tok, gates, mod)


SC_CORES, SC_SUBCORES = 2, 16
SC_WORKERS = SC_CORES * SC_SUBCORES
SC_CHUNK = 128
SC_CHUNKS_PER_WORKER = T_ALL // SC_CHUNK // SC_WORKERS


def _sc_mesh():
    return plsc.VectorSubcoreMesh(core_axis_name="c", subcore_axis_name="s")


def _sc_scratch():
    return [pltpu.VMEM((TOP_K, SC_CHUNK), jnp.int32), pltpu.VMEM((SC_CHUNK, DW), jnp.int32),
            pltpu.SemaphoreType.DMA]


def _dispatch_rows(h2p, dest):
    @functools.partial(pl.kernel, mesh=_sc_mesh(), out_type=jax.ShapeDtypeStruct((MOE_ROWS, DW), jnp.int32),
                       scratch_types=_sc_scratch(), name="dispatch_rows")
    def run(h_hbm, d_hbm, o_hbm, idx_v, rows_v, sem):
        wid = lax.axis_index("s") * SC_CORES + lax.axis_index("c")
        for j in range(SC_CHUNKS_PER_WORKER):
            c = wid * SC_CHUNKS_PER_WORKER + j
            pltpu.sync_copy(d_hbm.at[c], idx_v)
            pltpu.sync_copy(h_hbm.at[pl.ds(c * SC_CHUNK, SC_CHUNK)], rows_v)
            copies = [pltpu.async_copy(rows_v, o_hbm.at[idx_v.at[k]], sem) for k in range(TOP_K)]
            for cp in copies:
                cp.wait()

    return run(h2p, dest)


def _gather_rows(y, dest, first_chunk, n_chunks, *after):
    per_worker = n_chunks // SC_WORKERS
    assert per_worker * SC_WORKERS == n_chunks

    @functools.partial(pl.kernel, mesh=_sc_mesh(),
                       out_type=jax.ShapeDtypeStruct((TOP_K, n_chunks * SC_CHUNK, DW), jnp.int32),
                       scratch_types=_sc_scratch(), name="gather_rows")
    def run(y_hbm, d_hbm, *rest):
        o_hbm, idx_v, rows_v, sem = rest[len(after):]
        wid = lax.axis_index("s") * SC_CORES + lax.axis_index("c")
        for j in range(per_worker):
            c = wid * per_worker + j
            pltpu.sync_copy(d_hbm.at[first_chunk + c], idx_v)
            for k in range(TOP_K):
                pltpu.async_copy(y_hbm.at[idx_v.at[k]], rows_v, sem).wait()
                pltpu.sync_copy(rows_v, o_hbm.at[k, pl.ds(c * SC_CHUNK, SC_CHUNK)])

    return run(y, dest, *after)


def _constants():
    lane64 = np.arange(4 * HD)
    seg64 = (lane64[:, None] // HD == lane64[None, :] // HD).astype(np.float32)
    lane96 = np.arange(4 * BP)
    real = lane96 % BP < B_QK
    seg96 = ((lane96[:, None] // BP == lane96[None, :] // BP) & real[:, None] & real[None, :]).astype(np.float32)
    place = np.zeros((LANE, 4 * BP), np.float32)
    for hh in range(4):
        place[np.arange(B_ROPE), hh * BP + B_NOPE + np.arange(B_ROPE)] = 1.0

    def angles(rot_dim):
        pos = np.arange(LAT_LEN)
        rows = (pos // GRID_W).astype(np.float32)
        cols = (pos % GRID_W).astype(np.float32)
        axis_dim = rot_dim // 2
        inv = np.power(np.float32(ROPE_THETA), -(np.arange(0, axis_dim, 2, dtype=np.float32) / np.float32(axis_dim)))
        ang = np.concatenate([rows[:, None] * inv, cols[:, None] * inv], axis=-1).astype(np.float32)
        return np.cos(ang), np.sin(ang)

    def head_tables(rot_dim):
        cos, sin = angles(rot_dim)
        q = rot_dim // 4
        cr, cc, sr, sc = cos[:, :q], cos[:, q:], sin[:, :q], sin[:, q:]
        return (np.concatenate([cr, cr, cc, cc], axis=-1), np.concatenate([-sr, sr, -sc, sc], axis=-1))

    c64, s64 = head_tables(HD)
    cos64 = np.tile(c64, (1, 4))
    sin64 = np.tile(s64, (1, 4))
    c32, s32 = head_tables(B_ROPE)
    ones = np.ones((LAT_LEN, B_NOPE), np.float32)
    zeros = np.zeros((LAT_LEN, B_NOPE), np.float32)
    padz = np.zeros((LAT_LEN, BP - B_QK), np.float32)
    cos96 = np.tile(np.concatenate([ones, c32, padz], axis=-1), (1, 4))
    sin96 = np.tile(np.concatenate([zeros, s32, padz], axis=-1), (1, 4))
    first64 = ((lane64 % 32) < 16).astype(np.float32)[None, :]
    first96 = (((lane96 % BP) % 16) < 8).astype(np.float32)[None, :]
    tri = (np.arange(TP)[:, None] < np.arange(TP)[None, :]).astype(np.float32)
    f32 = lambda a: jnp.asarray(a, F32)
    return dict(seg64=jnp.asarray(seg64, BF16), seg96=jnp.asarray(seg96, BF16), place=jnp.asarray(place, BF16),
                tri=jnp.asarray(tri, BF16),
                cos64=f32(cos64), sin64=f32(sin64), cos96=f32(cos96), sin96=f32(sin96),
                first64=f32(first64), first96=f32(first96))


def _pad_heads(w, per_head, width):
    lead = w.shape[:-1]
    w = w.reshape(lead + (4, per_head))
    return jnp.pad(w, ((0, 0),) * (len(lead) + 1) + ((0, width - per_head),)).reshape(lead + (4 * width,))


def _weights(norm1_g, norm2_g, w_in, a_q_g, a_k_g, b_cq_g, b_ckv_g, w_uq, w_ukv, b_q_g, b_k_g,
             c_q_g, c_k_g, w_out, router_w, router_b):
    o = np.cumsum((0, 256, 128, 128, 384, 256, 32, 256, 128, 128))
    seg = lambda k: w_in[:, :, o[k]:o[k + 1]]
    w_in_r = jnp.concatenate([seg(0), seg(1), seg(2), seg(3), seg(4), seg(6), seg(7), seg(8), seg(5),
                              jnp.zeros((DEPTH, D, LANE - B_ROPE), F32)], axis=-1).astype(BF16)
    ukv = w_ukv.reshape(DEPTH, B_KV_LORA, 4, B_NOPE + B_V)
    w_uk = _pad_heads(ukv[..., :B_NOPE].reshape(DEPTH, B_KV_LORA, 4 * B_NOPE), B_NOPE, BP)
    w_uv = ukv[..., B_NOPE:].reshape(DEPTH, B_KV_LORA, 4 * B_V)
    rw_hi = router_w.astype(BF16)
    rw_lo = (router_w - rw_hi.astype(F32)).astype(BF16)
    tile = lambda v, n: jnp.tile(v, (1, n))
    rows = dict(n1g=norm1_g, n2g=norm2_g, aqg=tile(a_q_g, 4), akg=tile(a_k_g, 2), cqg=tile(c_q_g, 4),
                ckg=tile(c_k_g, 2), bcqg=b_cq_g, bckvg=b_ckv_g,
                bqg=_pad_heads(tile(b_q_g, 4), B_QK, BP), bkg=_pad_heads(tile(b_k_g, 4), B_QK, BP))
    order = sorted(VEC_ROWS, key=lambda n: VEC_ROWS[n][0])
    assert all(rows[n].shape == (DEPTH, VEC_ROWS[n][1]) for n in order)
    vec = jnp.stack([jnp.pad(rows[n], ((0, 0), (0, D - rows[n].shape[1]))) for n in order], axis=1)
    return dict(
        vec=vec[:, :, None, :], w_in=w_in_r,
        w_uq=_pad_heads(w_uq, B_QK, BP).astype(BF16), w_uk=w_uk.astype(BF16), w_uv=w_uv.astype(BF16),
        w_out=w_out.astype(BF16),
        rwt=jnp.concatenate([jnp.swapaxes(rw_hi, 1, 2), jnp.swapaxes(rw_lo, 1, 2)], axis=1),
        rb=router_b[:, :, None])


def _moe(l, x_all, h2p, gate_slab, route, counts, mod, w_gu, b_gu, w_dn, b_dn):
    counts = counts[:, 0].astype(jnp.int32)
    padded = (counts + MOE_STEP - 1) // MOE_STEP * MOE_STEP
    pend = jnp.cumsum(padded)
    pstart = pend - padded
    n_used = (pend[-1] // MOE_STEP).astype(jnp.int32)
    steps = jnp.minimum(jnp.arange(MOE_STEPS, dtype=jnp.int32), n_used - 1)
    tile_expert = jnp.sum((pend[None, :] <= steps[:, None] * MOE_STEP).astype(jnp.int32), axis=1)
    tile_expert = jnp.minimum(tile_expert, N_EXPERTS - 1)
    of_expert = lambda table: jnp.sum(
        jnp.where(tile_expert[:, None] == jnp.arange(N_EXPERTS)[None, :], table[None, :], 0), axis=1)
    group_end_step = of_expert(pend) // MOE_STEP
    after = jnp.sum((pend[None, :] <= group_end_step[:, None] * MOE_STEP).astype(jnp.int32), axis=1)
    next_expert = jnp.where(group_end_step < n_used, jnp.minimum(after, N_EXPERTS - 1), -1).astype(jnp.int32)
    rows_left = of_expert(pstart + counts) - steps * MOE_STEP
    n_valid = jnp.clip((rows_left + MOE_TM - 1) // MOE_TM, 1, MOE_STEP // MOE_TM).astype(jnp.int32)
    e = route[TOP_K:2 * TOP_K].astype(jnp.int32)
    rank = route[2 * TOP_K:3 * TOP_K].astype(jnp.int32)
    start = jnp.sum(jnp.where(e[:, :, None] == jnp.arange(N_EXPERTS)[None, None, :], pstart[None, None, :], 0), axis=-1)
    dest = (start + rank).reshape(TOP_K, T_ALL // SC_CHUNK, SC_CHUNK).transpose(1, 0, 2)
    x_sorted = _dispatch_rows(h2p, dest)
    changed = jnp.concatenate([jnp.zeros((1,), jnp.int32), (tile_expert[1:] != tile_expert[:-1]).astype(jnp.int32)])
    slot = jnp.cumsum(changed) % 2
    y = _experts(l, tile_expert, n_used.reshape(1), next_expert, n_valid, slot.astype(jnp.int32), x_sorted,
                 w_gu, b_gu, w_dn, b_dn)
    ctx_chunks, lat_chunks = T_CTX // SC_CHUNK, T_LAT // SC_CHUNK
    y_ctx = _gather_rows(y, dest, 0, ctx_chunks)
    out_ctx = _combine(l, 0, N_CTX_TILES, x_all, y_ctx, gate_slab, mod)
    y_lat = _gather_rows(y, dest, ctx_chunks, lat_chunks, out_ctx)
    return out_ctx, _combine(l, N_CTX_TILES, N_LAT_TILES, x_all, y_lat, gate_slab, mod)


def kernel(x_prompt, x_sample, cache_a_k, cache_a_v, cache_b_ckv, cache_b_krope, cache_c_k, cache_c_v, c, c_ctx,
           norm1_g, norm2_g, w_mod, b_mod, w_in, a_q_g, a_k_g, a_sink, b_cq_g, b_ckv_g, w_uq, w_ukv, b_q_g, b_k_g,
           c_q_g, c_k_g, w_out, router_w, router_b, w_gu, b_gu, w_dn, b_dn):
    consts = _constants()
    cond = jnp.concatenate([c_ctx[None, :], c, jnp.zeros((3, D), F32)], axis=0)
    mod = _modulation(cond, w_mod, b_mod).reshape(DEPTH, 8, 1, 6 * D)
    x_ctx, x_lat, x_lat_off = x_prompt.reshape(T_CTX, D), x_sample.reshape(T_LAT, D), 0

    lw = _weights(norm1_g, norm2_g, w_in, a_q_g, a_k_g, b_cq_g, b_ckv_g, w_uq, w_ukv, b_q_g, b_k_g,
                  c_q_g, c_k_g, w_out, router_w, router_b)
    ckb, cvb = _cache_kv(cache_b_ckv, cache_b_krope, lw, consts)
    merge_heads = lambda a: a.reshape(N_LAT_SEQ, DEPTH, PAST, 2 * HD).astype(BF16)
    cache = dict(ka=merge_heads(cache_a_k), va=merge_heads(cache_a_v), kb=ckb, vb=cvb,
                 kc=merge_heads(cache_c_k), vc=merge_heads(cache_c_v))

    states = ()
    names = ["qa", "ka", "va", "qb", "kb", "vb", "qc", "kc", "vc"]
    for l in range(DEPTH):
        outs = _projection(False, l, x_ctx, 0, mod, lw, consts, states)
        p_ctx = dict(zip(names, outs[:9]))
        states = tuple(outs[9:])
        p_lat = dict(zip(names, _projection(True, l, x_lat, x_lat_off, mod, lw, consts)))
        mix_ctx = _attention_ctx(l, a_sink, p_ctx)
        mix_lat = _attention_lat(l, a_sink, p_lat, cache)
        x_mid, h2p, gate_slab, route, counts = _post_attention(l, mix_ctx, mix_lat, x_ctx, x_lat, x_lat_off, mod, lw,
                                                               consts)
        x_ctx, x_lat = _moe(l, x_mid, h2p, gate_slab, route, counts, mod, w_gu, b_gu, w_dn, b_dn)
    y_ctx, y_lat = x_ctx, x_lat

    return (y_ctx.reshape(N_CTX_SEQ, CTX_LEN, D), y_lat.reshape(N_LAT_SEQ, LAT_LEN, D)) + states
```

```python
import functools

import jax
import jax.numpy as jnp
import numpy as np
from jax import lax
from jax.experimental import pallas as pl
from jax.experimental.pallas import tpu as pltpu
from jax.experimental.pallas import tpu_sc as plsc

F32 = jnp.float32
BF16 = jnp.bfloat16

D = 1024
DEPTH = 2
N_CTX_SEQ, CTX_LEN = 16, 256
N_LAT_SEQ, LAT_LEN = 4, 2048
PAST = 512
T_CTX = N_CTX_SEQ * CTX_LEN
T_LAT = N_LAT_SEQ * LAT_LEN
T_ALL = T_CTX + T_LAT
GRID_W = 64
HD = 64
WINDOW = 128
B_NOPE, B_ROPE, B_V = 64, 32, 128
B_QK = B_NOPE + B_ROPE
B_Q_LORA, B_KV_LORA = 384, 256
N_EXPERTS, TOP_K = 32, 4
D_FF = 1024
SWIGLU_LIMIT = 7.0
SWIGLU_ALPHA = 1.702
ROPE_THETA = 10000.0
EPS = 1e-6
NEG_INF = -1e30

TM = 256
TP = 1024
TQ_LAT = 1024
TQ_CTX = 256
SEQ_PER_STEP = TQ_CTX // CTX_LEN
LANE = 128
BP = 128
N_CTX_TILES = T_CTX // TM
N_LAT_TILES = T_LAT // TM
N_TILES = T_ALL // TM
LAT_TILES_PER_SEQ = LAT_LEN // TM
MOE_TM = 256
MOE_STEP = 4 * MOE_TM
FF_CHUNK = 256
MOE_STEPS = T_ALL * TOP_K // MOE_STEP + N_EXPERTS
MOE_ROWS = MOE_STEPS * MOE_STEP
DW = D // 2
VMEM_LIMIT = 56 * 1024 * 1024

C_QA, C_KA, C_VA, C_CQ, C_CKV, C_QC, C_KC, C_VC, C_KR, C_END = (
    0, 256, 384, 512, 896, 1152, 1408, 1536, 1664, 1792)


def _dot(a, b):
    return jnp.dot(a, b, preferred_element_type=F32)


def _dot_t(a, b):
    return lax.dot_general(a, b, (((1,), (1,)), ((), ())), preferred_element_type=F32)


def _rms(x, g):
    return x * lax.rsqrt(jnp.mean(x * x, axis=-1, keepdims=True) + EPS) * g


def _seg_norm(x, seg, g, n):
    ss = _dot((x * x).astype(BF16), seg)
    return x * lax.rsqrt(ss * (1.0 / n) + EPS) * g


def _rope(x, cos, sin, first, sh):
    w = x.shape[1]
    fwd = pltpu.roll(x, w - sh, 1)
    bwd = pltpu.roll(x, sh, 1)
    return x * cos + jnp.where(first > 0.5, fwd, bwd) * sin


MOD_BN = 1536


def _mod_kernel(c_ref, w_ref, b_ref, o_ref):
    c = c_ref[...]
    s = (c * jax.nn.sigmoid(c)).astype(BF16)
    o_ref[...] = _dot(s, w_ref[...].astype(BF16)) + b_ref[...]


def _modulation(cond, w_mod, b_mod):
    return pl.pallas_call(
        _mod_kernel,
        grid=(DEPTH, 6 * D // MOD_BN),
        in_specs=[
            pl.BlockSpec((8, D), lambda l, j: (0, 0)),
            pl.BlockSpec((None, D, MOD_BN), lambda l, j: (l, 0, j)),
            pl.BlockSpec((None, 1, MOD_BN), lambda l, j: (l, 0, j)),
        ],
        out_specs=pl.BlockSpec((None, 8, MOD_BN), lambda l, j: (l, 0, j)),
        out_shape=jax.ShapeDtypeStruct((DEPTH, 8, 6 * D), F32),
        compiler_params=pltpu.CompilerParams(
            dimension_semantics=("arbitrary", "arbitrary"), vmem_limit_bytes=VMEM_LIMIT),
        name="modulation",
    )(cond, w_mod, b_mod.reshape(DEPTH, 1, 6 * D))


def _proj_kernel(is_lat, n_aliased, *refs):
    (x_ref, mod_ref, n1g_ref, win_ref, seg64_ref, seg96_ref, aqg_ref, akg_ref, cqg_ref, ckg_ref,
     bcqg_ref, bckvg_ref, bqg_ref, bkg_ref, wuq_ref, wuk_ref, wuv_ref, plc_ref) = refs[:18]
    refs = refs[18 + n_aliased:]
    if is_lat:
        (cos64_ref, sin64_ref, cos96_ref, sin96_ref, f64_ref, f96_ref) = refs[:6]
        refs = refs[6:]
    (qa_ref, ka_ref, va_ref, qb_ref, kb_ref, vb_ref, qc_ref, kc_ref, vc_ref) = refs[:9]
    refs = refs[9:]
    if not is_lat:
        if n_aliased == 0:
            for r in refs:
                r[:, 1:] = jnp.zeros((SEQ_PER_STEP, DEPTH - 1) + r.shape[2:], F32)
            refs = [r.at[:, 0] for r in refs]
        (kas_ref, vas_ref, ckvs_ref, krs_ref, kcs_ref, vcs_ref) = refs

    x = x_ref[...]
    m = mod_ref[...]
    sh1, sc1 = m[:, 0:D], m[:, D:2 * D]
    h = (_rms(x, n1g_ref[...]) * (1.0 + sc1) + sh1).astype(BF16)

    def proj(a, b):
        return _dot(h, win_ref[:, a:b])

    def rope64(t):
        wd = t.shape[1]
        return _rope(t, cos64_ref[:, :wd], sin64_ref[:, :wd], f64_ref[:, :wd], 16) if is_lat else t

    def rope96(t):
        return _rope(t, cos96_ref[...], sin96_ref[...], f96_ref[...], 8) if is_lat else t

    seg64 = seg64_ref[...]
    seg64h = seg64_ref[0:2 * HD, 0:2 * HD]
    seg96 = seg96_ref[...]

    def seg_sum(t, seg):
        return _dot((t * t).astype(BF16), seg)

    def seg_finish(t, ss, g, n):
        return t * lax.rsqrt(ss * (1.0 / n) + EPS) * g

    p_cq, p_ckv, p_kr = proj(C_CQ, C_CKV), proj(C_CKV, C_QC), proj(C_KR, C_END)
    p_qa, p_ka, p_qc, p_kc = proj(C_QA, C_KA), proj(C_KA, C_VA), proj(C_QC, C_KC), proj(C_KC, C_VC)
    p_va, p_vc = proj(C_VA, C_CQ), proj(C_VC, C_KR)

    cq = _rms(p_cq, bcqg_ref[...]).astype(BF16)
    ckv = _rms(p_ckv, bckvg_ref[...])
    ckv16 = ckv.astype(BF16)
    u_q = _dot(cq, wuq_ref[...])
    u_k = _dot(ckv16, wuk_ref[...]) + _dot(p_kr.astype(BF16), plc_ref[...])
    u_v = _dot(ckv16, wuv_ref[...])
    ss_qa, ss_ka = seg_sum(p_qa, seg64), seg_sum(p_ka, seg64h)
    ss_qc, ss_kc = seg_sum(p_qc, seg64), seg_sum(p_kc, seg64h)
    ss_qb, ss_kb = seg_sum(u_q, seg96), seg_sum(u_k, seg96)

    def store_state(ref, t):
        for s in range(SEQ_PER_STEP):
            ref[s] = t[s * CTX_LEN:(s + 1) * CTX_LEN]

    def store_kv_state(ref, t):
        for s in range(SEQ_PER_STEP):
            ref[s] = t[s * CTX_LEN:(s + 1) * CTX_LEN].reshape(CTX_LEN, 2, HD)

    if not is_lat:
        store_kv_state(vas_ref, p_va)
        store_kv_state(vcs_ref, p_vc)
        store_state(ckvs_ref, ckv)
        store_state(krs_ref, p_kr[:, 0:B_ROPE])
    va_ref[...] = p_va.astype(BF16)
    vc_ref[...] = p_vc.astype(BF16)
    vb_ref[...] = u_v.astype(BF16)

    t = seg_finish(p_qa, ss_qa, aqg_ref[...], HD)
    qa_ref[...] = (rope64(t) * (HD ** -0.5 * LOG2E)).astype(BF16)
    t = seg_finish(p_ka, ss_ka, akg_ref[...], HD)
    if not is_lat:
        store_kv_state(kas_ref, t)
    ka_ref[...] = rope64(t).astype(BF16)
    t = seg_finish(p_qc, ss_qc, cqg_ref[...], HD)
    qc_ref[...] = (rope64(t) * (HD ** -0.5 * LOG2E)).astype(BF16)
    t = seg_finish(p_kc, ss_kc, ckg_ref[...], HD)
    if not is_lat:
        store_kv_state(kcs_ref, t)
    kc_ref[...] = rope64(t).astype(BF16)
    t = seg_finish(u_q, ss_qb, bqg_ref[...], B_QK)
    qb_ref[...] = (rope96(t) * (B_QK ** -0.5 * LOG2E)).astype(BF16)
    t = seg_finish(u_k, ss_kb, bkg_ref[...], B_QK)
    kb_ref[...] = rope96(t).astype(BF16)


def _full(shape):
    nd = len(shape)
    return pl.BlockSpec(shape, lambda i: (0,) * nd)


STATE_TAILS = ((2, HD), (2, HD), (B_KV_LORA,), (B_ROPE,), (2, HD), (2, HD))


def _layer_spec(a, l):
    nd = a.ndim - 1
    return pl.BlockSpec((None,) + a.shape[1:], lambda *_: (l,) + (0,) * nd)


VEC_ROWS = dict(n1g=(0, D), n2g=(1, D), aqg=(2, 4 * HD), akg=(3, 2 * HD), cqg=(4, 4 * HD), ckg=(5, 2 * HD),
                bcqg=(6, B_Q_LORA), bckvg=(7, B_KV_LORA), bqg=(8, 4 * BP), bkg=(9, 4 * BP))


def _vec_spec(l, name):
    row, width = VEC_ROWS[name]
    return pl.BlockSpec((None, None, 1, width), lambda *_: (l, row, 0, 0))


def _projection(is_lat, l, x_src, x_off, mod, lw, consts, prev_states=()):
    assert is_lat or bool(prev_states) == (l > 0)
    t_rows = T_LAT if is_lat else T_CTX
    TQ = TQ_LAT if is_lat else TQ_CTX
    n_tiles = t_rows // TQ
    if is_lat:
        mod_map = lambda i: (l, 1 + i // (LAT_LEN // TQ), 0, 0)
    else:
        mod_map = lambda i: (l, 0, 0, 0)
    vec = lw["vec"]
    gain = lambda name: (vec, _vec_spec(l, name))
    whole = lambda a: (a, _full(a.shape))
    layer = lambda a: (a, _layer_spec(a, l))
    pairs = [(x_src, pl.BlockSpec((TQ, D), lambda i: (i + x_off, 0))),
             (mod, pl.BlockSpec((None, None, 1, 6 * D), mod_map)),
             gain("n1g"), layer(lw["w_in"]), whole(consts["seg64"]), whole(consts["seg96"]),
             gain("aqg"), gain("akg"), gain("cqg"), gain("ckg"), gain("bcqg"), gain("bckvg"), gain("bqg"), gain("bkg"),
             layer(lw["w_uq"]), layer(lw["w_uk"]), layer(lw["w_uv"]), whole(consts["place"])]
    ins = [a for a, _ in pairs]
    in_specs = [s for _, s in pairs]
    n_plain = len(ins)
    ins += list(prev_states)
    in_specs += [pl.BlockSpec(memory_space=pl.ANY) for _ in prev_states]
    if is_lat:
        tabs = [consts["cos64"], consts["sin64"], consts["cos96"], consts["sin96"]]
        ins += tabs + [consts["first64"], consts["first96"]]
        in_specs += [pl.BlockSpec((TQ, a.shape[1]), lambda i: (i % (LAT_LEN // TQ), 0)) for a in tabs]
        in_specs += [_full(consts["first64"].shape), _full(consts["first96"].shape)]
    widths = [4 * HD, 2 * HD, 2 * HD, 4 * BP, 4 * BP, 4 * B_V, 4 * HD, 2 * HD, 2 * HD]
    out_shape = [jax.ShapeDtypeStruct((t_rows, w), BF16) for w in widths]
    out_specs = [pl.BlockSpec((TQ, w), lambda i: (i, 0)) for w in widths]
    if not is_lat:
        for tail in STATE_TAILS:
            zeros = (0,) * len(tail)
            out_shape.append(jax.ShapeDtypeStruct((N_CTX_SEQ, DEPTH, CTX_LEN) + tail, F32))
            if prev_states:
                out_specs.append(pl.BlockSpec((SEQ_PER_STEP, None, CTX_LEN) + tail, lambda i, z=zeros: (i, l, 0) + z))
            else:
                out_specs.append(pl.BlockSpec((SEQ_PER_STEP, DEPTH, CTX_LEN) + tail, lambda i, z=zeros: (i, 0, 0) + z))
    aliases = {n_plain + k: len(widths) + k for k in range(len(prev_states))}
    return pl.pallas_call(
        functools.partial(_proj_kernel, is_lat, len(prev_states)),
        grid=(n_tiles,),
        in_specs=in_specs,
        out_specs=out_specs,
        out_shape=out_shape,
        input_output_aliases=aliases,
        compiler_params=pltpu.CompilerParams(
            dimension_semantics=("arbitrary",), vmem_limit_bytes=VMEM_LIMIT),
        name="proj_lat" if is_lat else "proj_ctx",
    )(*ins)


def _cache_kv_kernel(ckv_ref, kr_ref, seg96_ref, bkg_ref, wuk_ref, wuv_ref, plc_ref, kb_ref, vb_ref):
    ckv16 = ckv_ref[...].astype(BF16)
    vb_ref[...] = _dot(ckv16, wuv_ref[...]).astype(BF16)
    kpre = _dot(ckv16, wuk_ref[...]) + _dot(kr_ref[...].astype(BF16), plc_ref[0:B_ROPE, :])
    kb_ref[...] = _seg_norm(kpre, seg96_ref[...], bkg_ref[...], B_QK).astype(BF16)


def _cache_kv(cache_b_ckv, cache_b_krope, lw, consts):
    def c_map(l, r):
        return (r // (PAST // TM), l, r % (PAST // TM), 0)

    def w_spec(a):
        return pl.BlockSpec((None,) + a.shape[1:], lambda l, r: (l,) + (0,) * (a.ndim - 1))

    def s_spec(a):
        return pl.BlockSpec(a.shape, lambda l, r: (0,) * a.ndim)

    rows = N_LAT_SEQ * PAST
    return pl.pallas_call(
        _cache_kv_kernel,
        grid=(DEPTH, rows // TM),
        in_specs=[pl.BlockSpec((None, None, TM, B_KV_LORA), c_map), pl.BlockSpec((None, None, TM, B_ROPE), c_map),
                  s_spec(consts["seg96"]),
                  pl.BlockSpec((None, None, 1, 4 * BP), lambda l, r: (l, VEC_ROWS["bkg"][0], 0, 0)),
                  w_spec(lw["w_uk"]), w_spec(lw["w_uv"]),
                  s_spec(consts["place"])],
        out_specs=[pl.BlockSpec((None, TM, 4 * BP), lambda l, r: (l, r, 0)),
                   pl.BlockSpec((None, TM, 4 * B_V), lambda l, r: (l, r, 0))],
        out_shape=[jax.ShapeDtypeStruct((DEPTH, rows, 4 * BP), BF16),
                   jax.ShapeDtypeStruct((DEPTH, rows, 4 * B_V), BF16)],
        compiler_params=pltpu.CompilerParams(
            dimension_semantics=("arbitrary", "arbitrary"), vmem_limit_bytes=VMEM_LIMIT),
        name="cache_kv",
    )(cache_b_ckv, cache_b_krope, consts["seg96"], lw["vec"], lw["w_uk"], lw["w_uv"], consts["place"])


def _scores(q, segs):
    scores = []
    for k, _, mask in segs:
        s = _dot_t(q, k)
        if mask is not None:
            s = jnp.where(mask, s, NEG_INF)
        scores.append(s)
    return scores


def _softmax_pv(scores, segs, sink, num_cols, den_col):
    m = None
    for s in scores:
        sm = jnp.max(s, axis=-1, keepdims=True)
        m = sm if m is None else jnp.maximum(m, sm)
    if sink is not None:
        m = jnp.maximum(m, sink)
    acc = None
    den = None
    for s, (_, v, _) in zip(scores, segs):
        e = jnp.exp2(s - m)
        if den_col is None:
            d = jnp.sum(e, axis=-1, keepdims=True)
            den = d if den is None else den + d
        o = _dot(e.astype(BF16), v)
        acc = o if acc is None else acc + o
    if den_col is not None:
        den = acc[:, den_col:den_col + 1]
        acc = acc[:, num_cols]
    if sink is not None:
        den = den + jnp.exp2(sink - m)
    return acc / den


HEAD_LOOKAHEAD = 1


def _run_heads(jobs, o_ref):
    ready = [_scores(job[0](), job[1]()) for job in jobs[:HEAD_LOOKAHEAD]]
    for n, (_, load_segs, sink, col, num_cols, den_col) in enumerate(jobs):
        scores = ready.pop(0)
        if n + HEAD_LOOKAHEAD < len(jobs):
            ahead = jobs[n + HEAD_LOOKAHEAD]
            ready.append(_scores(ahead[0](), ahead[1]()))
        o = _softmax_pv(scores, load_segs(values=True), sink, num_cols, den_col)
        o_ref[:, col:col + o.shape[1]] = o.astype(BF16)


OUT_B = 4 * HD
OUT_C = 4 * HD + 4 * B_V
LOG2E = 1.4426950408889634


def _head_job(q_ref, q_cols, seg_refs, kv_cols, v_cols, sink, col, rows=None, masks=None, mxu_sums=True):
    width = v_cols.stop - v_cols.start
    if not mxu_sums:
        load_cols, num_cols, den_col = v_cols, None, None
    elif width == HD:
        pair = v_cols.start // LANE * LANE
        load_cols = slice(pair, pair + LANE)
        lo = v_cols.start - pair
        num_cols, den_col = slice(lo, lo + HD), (lo + HD) % LANE
    else:
        load_cols, num_cols, den_col = v_cols, slice(0, width), width
    if sink is not None:
        sink = sink * LOG2E

    def with_ones(v):
        if not mxu_sums:
            return v
        if width != HD:
            return jnp.concatenate([v, jnp.ones_like(v)], axis=1)
        lane = lax.broadcasted_iota(jnp.int32, (1, LANE), 1)
        keep = jnp.where((lane >= num_cols.start) & (lane < num_cols.stop), 1.0, 0.0).astype(BF16)
        return v * keep + (1.0 - keep)

    def load_segs(values=False):
        segs = []
        for n, (k_ref, v_ref) in enumerate(seg_refs):
            r = rows if (rows is not None and n == 0) else slice(None)
            if values:
                segs.append((None, with_ones(v_ref[r, load_cols]), None))
            else:
                segs.append((k_ref[r, kv_cols], None, None if masks is None else masks[n]))
        return segs

    return (lambda: q_ref[:, q_cols]), load_segs, sink, col, num_cols, den_col


def _attn_ctx_kernel(l, sink_ref, qa_ref, ka_ref, va_ref, qb_ref, kb_ref, vb_ref, qc_ref, kc_ref, vc_ref, o_ref):
    jobs = []
    for h in range(4):
        cs = slice(h // 2 * HD, (h // 2 + 1) * HD)
        jobs.append(_head_job(qa_ref, slice(h * HD, (h + 1) * HD), [(ka_ref, va_ref)], cs, cs, sink_ref[l, h],
                              h * HD, mxu_sums=False))
    for h in range(4):
        ks, vs = slice(h * BP, (h + 1) * BP), slice(h * B_V, (h + 1) * B_V)
        jobs.append(_head_job(qb_ref, ks, [(kb_ref, vb_ref)], ks, vs, None, OUT_B + h * B_V, mxu_sums=False))
    for h in range(4):
        cs = slice(h // 2 * HD, (h // 2 + 1) * HD)
        jobs.append(_head_job(qc_ref, slice(h * HD, (h + 1) * HD), [(kc_ref, vc_ref)], cs, cs, None, OUT_C + h * HD,
                              mxu_sums=False))
    _run_heads(jobs, o_ref)


def _attention_ctx(l, sink, p):
    names = ["qa", "ka", "va", "qb", "kb", "vb", "qc", "kc", "vc"]
    ins = [p[n] for n in names]
    in_specs = [pl.BlockSpec(memory_space=pltpu.SMEM)]
    in_specs += [pl.BlockSpec((CTX_LEN, a.shape[1]), lambda i: (i, 0)) for a in ins]
    return pl.pallas_call(
        functools.partial(_attn_ctx_kernel, l),
        grid=(N_CTX_SEQ,),
        in_specs=in_specs,
        out_specs=pl.BlockSpec((CTX_LEN, D), lambda i: (i, 0)),
        out_shape=jax.ShapeDtypeStruct((T_CTX, D), BF16),
        compiler_params=pltpu.CompilerParams(
            dimension_semantics=("arbitrary",), vmem_limit_bytes=VMEM_LIMIT),
        name="attn_ctx",
    )(sink, *ins)


WIN_SPAN = TM + 2 * WINDOW


def _attn_lat_kernel(l, sink_ref, qa_ref, qb_ref, qc_ref, ka_ref, va_ref, kb_ref, vb_ref, kc_ref, vc_ref,
                     cka_ref, cva_ref, ckb_ref, cvb_ref, ckc_ref, cvc_ref, o_ref):
    qi = pl.program_id(1)
    ws = pl.multiple_of(jnp.clip(qi * TM - WINDOW, 0, LAT_LEN - WIN_SPAN), WINDOW)
    qpos = qi * TM + lax.broadcasted_iota(jnp.int32, (TM, WIN_SPAN), 0)
    kpos = ws + lax.broadcasted_iota(jnp.int32, (TM, WIN_SPAN), 1)
    band = jnp.abs(qpos - kpos) <= WINDOW
    jobs = []
    for h in range(4):
        cs = slice(h // 2 * HD, (h // 2 + 1) * HD)
        jobs.append(_head_job(qa_ref, slice(h * HD, (h + 1) * HD), [(ka_ref, va_ref), (cka_ref, cva_ref)], cs, cs,
                              sink_ref[l, h], h * HD, rows=pl.ds(ws, WIN_SPAN), masks=(band, None)))
    for h in range(4):
        ks, vs = slice(h * BP, (h + 1) * BP), slice(h * B_V, (h + 1) * B_V)
        jobs.append(_head_job(qb_ref, ks, [(kb_ref, vb_ref), (ckb_ref, cvb_ref)], ks, vs, None, OUT_B + h * B_V))
    for h in range(4):
        cs = slice(h // 2 * HD, (h // 2 + 1) * HD)
        jobs.append(_head_job(qc_ref, slice(h * HD, (h + 1) * HD), [(kc_ref, vc_ref), (ckc_ref, cvc_ref)], cs, cs,
                              None, OUT_C + h * HD))
    _run_heads(jobs, o_ref)


def _attention_lat(l, sink, p, cache):
    q_ins = [p["qa"], p["qb"], p["qc"]]
    kv_ins = [p[n] for n in ("ka", "va", "kb", "vb", "kc", "vc")]
    c_names = ("ka", "va", "kb", "vb", "kc", "vc")
    c_ins = [cache[n] for n in c_names]
    in_specs = [pl.BlockSpec(memory_space=pltpu.SMEM)]
    in_specs += [pl.BlockSpec((TM, a.shape[1]), lambda b, i: (b * LAT_TILES_PER_SEQ + i, 0)) for a in q_ins]
    in_specs += [pl.BlockSpec((LAT_LEN, a.shape[1]), lambda b, i: (b, 0)) for a in kv_ins]
    for n, a in zip(c_names, c_ins):
        if n in ("kb", "vb"):
            in_specs.append(pl.BlockSpec((None, PAST, a.shape[-1]), lambda b, i: (l, b, 0)))
        else:
            in_specs.append(pl.BlockSpec((None, None, PAST, a.shape[-1]), lambda b, i: (b, l, 0, 0)))
    return pl.pallas_call(
        functools.partial(_attn_lat_kernel, l),
        grid=(N_LAT_SEQ, LAT_TILES_PER_SEQ),
        in_specs=in_specs,
        out_specs=pl.BlockSpec((TM, D), lambda b, i: (b * LAT_TILES_PER_SEQ + i, 0)),
        out_shape=jax.ShapeDtypeStruct((T_LAT, D), BF16),
        compiler_params=pltpu.CompilerParams(
            dimension_semantics=("arbitrary", "arbitrary"), vmem_limit_bytes=VMEM_LIMIT),
        name="attn_lat",
    )(sink, *q_ins, *kv_ins, *c_ins)


def _pack_rows(x):
    half = x.shape[1] // 2
    r = x.astype(BF16).astype(F32)
    hi = lax.bitcast_convert_type(r[:, :half], jnp.int32)
    lo = lax.bitcast_convert_type(r[:, half:], jnp.int32)
    return jnp.bitwise_or(hi, lax.shift_right_logical(lo, 16))


def _unpack_rows(p):
    a = lax.bitcast_convert_type(jnp.bitwise_and(p, -65536), F32)
    b = lax.bitcast_convert_type(lax.shift_left(p, 16), F32)
    return jnp.concatenate([a, b], axis=1).astype(BF16)


ROUTE_ROWS = 16


def _post_kernel(mc_ref, ml_ref, xc_ref, xl_ref, mod_ref, n2g_ref, wout_ref, rwt_ref, rb_ref, tri_ref,
                 xo_ref, h2_ref, gate_ref, route_ref, cnt_ref, run_ref):
    i = pl.program_id(0)

    @pl.when(i == 0)
    def _():
        run_ref[...] = jnp.zeros_like(run_ref)
        cnt_ref[...] = jnp.zeros_like(cnt_ref)

    is_ctx = i < T_CTX // TP
    m = mod_ref[...]
    g1, sh2, sc2 = m[:, 2 * D:3 * D], m[:, 3 * D:4 * D], m[:, 4 * D:5 * D]

    mix = jnp.where(is_ctx, mc_ref[...], ml_ref[...])
    x = jnp.where(is_ctx, xc_ref[...], xl_ref[...]) + g1 * _dot(mix, wout_ref[...])
    xo_ref[...] = x
    h2 = _rms(x, n2g_ref[...]) * (1.0 + sc2) + sh2
    hi = h2.astype(BF16)
    h2_ref[...] = _pack_rows(h2)
    lo = (h2 - hi.astype(F32)).astype(BF16)
    a = _dot_t(rwt_ref[...], hi)
    b = _dot_t(rwt_ref[0:N_EXPERTS, :], lo)
    logits = a[0:N_EXPERTS] + a[N_EXPERTS:] + b + rb_ref[...]
    eidx = lax.broadcasted_iota(jnp.int32, logits.shape, 0).astype(F32)
    work = logits
    vals, idxs, hots = [], [], []
    for _ in range(TOP_K):
        v = jnp.max(work, axis=0, keepdims=True)
        idx = jnp.min(jnp.where(work == v, eidx, float(N_EXPERTS)), axis=0, keepdims=True)
        hot = eidx == idx
        vals.append(v)
        idxs.append(idx)
        hots.append(hot)
        work = jnp.where(hot, -jnp.inf, work)
    es = [jnp.exp(v - vals[0]) for v in vals]
    den = es[0] + es[1] + es[2] + es[3]
    gates = [e / den for e in es]
    sel = jnp.where(hots[0] | hots[1] | hots[2] | hots[3], 1.0, 0.0)
    run = run_ref[:, 0:1]
    before = _dot(sel.astype(BF16), tri_ref[...]) + run
    run_new = jnp.broadcast_to(run + jnp.sum(sel, axis=1, keepdims=True), run_ref.shape)
    run_ref[...] = run_new
    cnt_ref[...] = run_new
    ranks = [jnp.sum(jnp.where(hots[k], before, 0.0), axis=0, keepdims=True) for k in range(TOP_K)]
    pad = [jnp.zeros((ROUTE_ROWS - 3 * TOP_K, TP), F32)]
    route_ref[...] = jnp.concatenate(gates + idxs + ranks + pad, axis=0)
    gate_ref[...] = jnp.concatenate(gates + [jnp.zeros((LANE - TOP_K, TP), F32)], axis=0).T


def _mod_row(i, tm=TM):
    n_ctx = T_CTX // tm
    return jnp.where(i < n_ctx, 0, 1 + (i - n_ctx) // (LAT_LEN // tm))


def _ctx_tile(i):
    return jnp.minimum(i, T_CTX // TP - 1)


def _lat_tile(i):
    return jnp.maximum(i - T_CTX // TP, 0)


def _post_attention(l, mix_ctx, mix_lat, x_ctx, x_lat, x_lat_off, mod, lw, consts):
    ins = [mix_ctx, mix_lat, x_ctx, x_lat, mod, lw["vec"], lw["w_out"], lw["rwt"], lw["rb"], consts["tri"]]
    in_specs = [
        pl.BlockSpec((TP, D), lambda i: (_ctx_tile(i), 0)),
        pl.BlockSpec((TP, D), lambda i: (_lat_tile(i), 0)),
        pl.BlockSpec((TP, D), lambda i: (_ctx_tile(i), 0)),
        pl.BlockSpec((TP, D), lambda i: (_lat_tile(i) + x_lat_off, 0)),
        pl.BlockSpec((None, None, 1, 6 * D), lambda i: (l, _mod_row(i, TP), 0, 0)),
        _vec_spec(l, "n2g"),
    ] + [_layer_spec(a, l) for a in ins[6:9]] + [_full(consts["tri"].shape)]
    return pl.pallas_call(
        _post_kernel,
        grid=(T_ALL // TP,),
        in_specs=in_specs,
        out_specs=[pl.BlockSpec((TP, D), lambda i: (i, 0)), pl.BlockSpec((TP, DW), lambda i: (i, 0)),
                   pl.BlockSpec((TP, LANE), lambda i: (i, 0)), pl.BlockSpec((ROUTE_ROWS, TP), lambda i: (0, i)),
                   pl.BlockSpec((N_EXPERTS, LANE), lambda i: (0, 0))],
        out_shape=[jax.ShapeDtypeStruct((T_ALL, D), F32), jax.ShapeDtypeStruct((T_ALL, DW), jnp.int32),
                   jax.ShapeDtypeStruct((T_ALL, LANE), F32), jax.ShapeDtypeStruct((ROUTE_ROWS, T_ALL), F32),
                   jax.ShapeDtypeStruct((N_EXPERTS, LANE), F32)],
        scratch_shapes=[pltpu.VMEM((N_EXPERTS, LANE), F32)],
        compiler_params=pltpu.CompilerParams(
            dimension_semantics=("arbitrary",), vmem_limit_bytes=VMEM_LIMIT),
        name="post_attn",
    )(*ins)


def _expert_kernel(l, te_ref, nu_ref, nxt_ref, nv_ref, slot_ref, x_ref, wgu_hbm, bgu_ref, wdn_hbm, bdn_ref, y_ref,
                   wgu32, wdn32, wgu16, wdn16, sem):
    i = pl.program_id(0)
    prev = te_ref[jnp.maximum(i - 1, 0)]
    new_expert = jnp.logical_or(i == 0, te_ref[i] != prev)
    slot = slot_ref[i]

    def weight_copies(e, s):
        return (pltpu.make_async_copy(wgu_hbm.at[l, e], wgu32.at[s], sem.at[s, 0]),
                pltpu.make_async_copy(wdn_hbm.at[l, e], wdn32.at[s], sem.at[s, 1]))

    @pl.when(i == 0)
    def _():
        for cp in weight_copies(te_ref[0], slot):
            cp.start()

    @pl.when(new_expert)
    def _():
        for cp in weight_copies(te_ref[i], slot):
            cp.wait()

        @pl.when(nxt_ref[i] >= 0)
        def _():
            for cp in weight_copies(nxt_ref[i], 1 - slot):
                cp.start()

    def compute(rows, convert):
        x = _unpack_rows(x_ref[0:rows, :])

        def gate_up(a):
            if convert:
                for c in (slice(a, a + FF_CHUNK), slice(D_FF + a, D_FF + a + FF_CHUNK)):
                    wgu16[:, c] = wgu32[slot, :, c].astype(BF16)
                wdn16[a:a + FF_CHUNK, :] = wdn32[slot, a:a + FF_CHUNK, :].astype(BF16)
            b = a + FF_CHUNK
            g = _dot(x, wgu16[:, a:b]) + bgu_ref[:, a:b]
            up = _dot(x, wgu16[:, D_FF + a:D_FF + b]) + bgu_ref[:, D_FF + a:D_FF + b]
            return g, up

        def activate(g, up):
            g = jnp.minimum(g, SWIGLU_LIMIT)
            up = jnp.clip(up, -SWIGLU_LIMIT, SWIGLU_LIMIT)
            return ((up + 1.0) * (g * jax.nn.sigmoid(SWIGLU_ALPHA * g))).astype(BF16)

        starts = list(range(0, D_FF, FF_CHUNK))
        acc = None
        pre = gate_up(starts[0])
        for n, a in enumerate(starts):
            hid = activate(*pre)
            if n + 1 < len(starts):
                pre = gate_up(starts[n + 1])
            o = _dot(hid, wdn16[a:a + FF_CHUNK, :])
            acc = o if acc is None else acc + o
        y_ref[0:rows, :] = _pack_rows(acc + bdn_ref[...])

    used = i < nu_ref[0]
    for n_valid in range(1, MOE_STEP // MOE_TM + 1):
        for convert in (False, True):
            @pl.when(used & (nv_ref[i] == n_valid) & (new_expert == convert))
            def _():
                compute(n_valid * MOE_TM, convert)


def _experts(l, tile_expert, n_used, next_expert, n_valid, slot, x_sorted, w_gu, b_gu, w_dn, b_dn):
    def row_map(i, te, nu, nxt, nv, sl):
        return (jnp.minimum(i, nu[0] - 1), 0)

    def b_map(i, te, nu, nxt, nv, sl):
        return (l, te[i], 0, 0)

    grid_spec = pltpu.PrefetchScalarGridSpec(
        num_scalar_prefetch=5,
        grid=(MOE_STEPS,),
        in_specs=[
            pl.BlockSpec((MOE_STEP, DW), row_map),
            pl.BlockSpec(memory_space=pl.ANY),
            pl.BlockSpec((None, None, 1, 2 * D_FF), b_map),
            pl.BlockSpec(memory_space=pl.ANY),
            pl.BlockSpec((None, None, 1, D), b_map),
        ],
        out_specs=pl.BlockSpec((MOE_STEP, DW), row_map),
        scratch_shapes=[pltpu.VMEM((2, D, 2 * D_FF), F32), pltpu.VMEM((2, D_FF, D), F32),
                        pltpu.VMEM((D, 2 * D_FF), BF16), pltpu.VMEM((D_FF, D), BF16),
                        pltpu.SemaphoreType.DMA((2, 2))],
    )
    return pl.pallas_call(
        functools.partial(_expert_kernel, l),
        grid_spec=grid_spec,
        out_shape=jax.ShapeDtypeStruct((MOE_ROWS, DW), jnp.int32),
        compiler_params=pltpu.CompilerParams(
            dimension_semantics=("arbitrary",), vmem_limit_bytes=VMEM_LIMIT),
        name="experts",
    )(tile_expert, n_used, next_expert, n_valid, slot, x_sorted, w_gu, b_gu.reshape(DEPTH, N_EXPERTS, 1, 2 * D_FF),
      w_dn, b_dn.reshape(DEPTH, N_EXPERTS, 1, D))


def _combine_kernel(x_ref, y_ref, route_ref, mod_ref, o_ref):
    g2 = mod_ref[...][:, 5 * D:6 * D]
    r = route_ref[...]
    acc = r[:, 0:1] * _unpack_rows(y_ref[0]).astype(F32)
    for k in range(1, TOP_K):
        acc = acc + r[:, k:k + 1] * _unpack_rows(y_ref[k]).astype(F32)
    o_ref[...] = x_ref[...] + g2 * acc


def _combine(l, first_tile, n_tiles, x_all, y_tok, gates, mod):
    return pl.pallas_call(
        _combine_kernel,
        grid=(n_tiles,),
        in_specs=[
            pl.BlockSpec((TM, D), lambda i: (i + first_tile, 0)),
            pl.BlockSpec((TOP_K, TM, DW), lambda i: (0, i, 0)),
            pl.BlockSpec((TM, LANE), lambda i: (i + first_tile, 0)),
            pl.BlockSpec((None, None, 1, 6 * D), lambda i: (l, _mod_row(i + first_tile), 0, 0)),
        ],
        out_specs=pl.BlockSpec((TM, D), lambda i: (i, 0)),
        out_shape=jax.ShapeDtypeStruct((n_tiles * TM, D), F32),
        compiler_params=pltpu.CompilerParams(
            dimension_semantics=("arbitrary",), vmem_limit_bytes=VMEM_LIMIT),
        name="combine",
    )(x_all, y_tok, gates, mod)


SC_CORES, SC_SUBCORES = 2, 16
SC_WORKERS = SC_CORES * SC_SUBCORES
SC_CHUNK = 128
SC_CHUNKS_PER_WORKER = T_ALL // SC_CHUNK // SC_WORKERS


def _sc_mesh():
    return plsc.VectorSubcoreMesh(core_axis_name="c", subcore_axis_name="s")


def _sc_scratch():
    return [pltpu.VMEM((TOP_K, SC_CHUNK), jnp.int32), pltpu.VMEM((SC_CHUNK, DW), jnp.int32),
            pltpu.SemaphoreType.DMA]


def _dispatch_rows(h2p, dest):
    @functools.partial(pl.kernel, mesh=_sc_mesh(), out_type=jax.ShapeDtypeStruct((MOE_ROWS, DW), jnp.int32),
                       scratch_types=_sc_scratch(), name="dispatch_rows")
    def run(h_hbm, d_hbm, o_hbm, idx_v, rows_v, sem):
        wid = lax.axis_index("s") * SC_CORES + lax.axis_index("c")
        for j in range(SC_CHUNKS_PER_WORKER):
            c = wid * SC_CHUNKS_PER_WORKER + j
            pltpu.sync_copy(d_hbm.at[c], idx_v)
            pltpu.sync_copy(h_hbm.at[pl.ds(c * SC_CHUNK, SC_CHUNK)], rows_v)
            copies = [pltpu.async_copy(rows_v, o_hbm.at[idx_v.at[k]], sem) for k in range(TOP_K)]
            for cp in copies:
                cp.wait()

    return run(h2p, dest)


def _gather_rows(y, dest, first_chunk, n_chunks, *after):
    per_worker = n_chunks // SC_WORKERS
    assert per_worker * SC_WORKERS == n_chunks

    @functools.partial(pl.kernel, mesh=_sc_mesh(),
                       out_type=jax.ShapeDtypeStruct((TOP_K, n_chunks * SC_CHUNK, DW), jnp.int32),
                       scratch_types=_sc_scratch(), name="gather_rows")
    def run(y_hbm, d_hbm, *rest):
        o_hbm, idx_v, rows_v, sem = rest[len(after):]
        wid = lax.axis_index("s") * SC_CORES + lax.axis_index("c")
        for j in range(per_worker):
            c = wid * per_worker + j
            pltpu.sync_copy(d_hbm.at[first_chunk + c], idx_v)
            for k in range(TOP_K):
                pltpu.async_copy(y_hbm.at[idx_v.at[k]], rows_v, sem).wait()
                pltpu.sync_copy(rows_v, o_hbm.at[k, pl.ds(c * SC_CHUNK, SC_CHUNK)])

    return run(y, dest, *after)


def _constants():
    lane64 = np.arange(4 * HD)
    seg64 = (lane64[:, None] // HD == lane64[None, :] // HD).astype(np.float32)
    lane96 = np.arange(4 * BP)
    real = lane96 % BP < B_QK
    seg96 = ((lane96[:, None] // BP == lane96[None, :] // BP) & real[:, None] & real[None, :]).astype(np.float32)
    place = np.zeros((LANE, 4 * BP), np.float32)
    for hh in range(4):
        place[np.arange(B_ROPE), hh * BP + B_NOPE + np.arange(B_ROPE)] = 1.0

    def angles(rot_dim):
        pos = np.arange(LAT_LEN)
        rows = (pos // GRID_W).astype(np.float32)
        cols = (pos % GRID_W).astype(np.float32)
        axis_dim = rot_dim // 2
        inv = np.power(np.float32(ROPE_THETA), -(np.arange(0, axis_dim, 2, dtype=np.float32) / np.float32(axis_dim)))
        ang = np.concatenate([rows[:, None] * inv, cols[:, None] * inv], axis=-1).astype(np.float32)
        return np.cos(ang), np.sin(ang)

    def head_tables(rot_dim):
        cos, sin = angles(rot_dim)
        q = rot_dim // 4
        cr, cc, sr, sc = cos[:, :q], cos[:, q:], sin[:, :q], sin[:, q:]
        return (np.concatenate([cr, cr, cc, cc], axis=-1), np.concatenate([-sr, sr, -sc, sc], axis=-1))

    c64, s64 = head_tables(HD)
    cos64 = np.tile(c64, (1, 4))
    sin64 = np.tile(s64, (1, 4))
    c32, s32 = head_tables(B_ROPE)
    ones = np.ones((LAT_LEN, B_NOPE), np.float32)
    zeros = np.zeros((LAT_LEN, B_NOPE), np.float32)
    padz = np.zeros((LAT_LEN, BP - B_QK), np.float32)
    cos96 = np.tile(np.concatenate([ones, c32, padz], axis=-1), (1, 4))
    sin96 = np.tile(np.concatenate([zeros, s32, padz], axis=-1), (1, 4))
    first64 = ((lane64 % 32) < 16).astype(np.float32)[None, :]
    first96 = (((lane96 % BP) % 16) < 8).astype(np.float32)[None, :]
    tri = (np.arange(TP)[:, None] < np.arange(TP)[None, :]).astype(np.float32)
    f32 = lambda a: jnp.asarray(a, F32)
    return dict(seg64=jnp.asarray(seg64, BF16), seg96=jnp.asarray(seg96, BF16), place=jnp.asarray(place, BF16),
                tri=jnp.asarray(tri, BF16),
                cos64=f32(cos64), sin64=f32(sin64), cos96=f32(cos96), sin96=f32(sin96),
                first64=f32(first64), first96=f32(first96))


def _pad_heads(w, per_head, width):
    lead = w.shape[:-1]
    w = w.reshape(lead + (4, per_head))
    return jnp.pad(w, ((0, 0),) * (len(lead) + 1) + ((0, width - per_head),)).reshape(lead + (4 * width,))


def _weights(norm1_g, norm2_g, w_in, a_q_g, a_k_g, b_cq_g, b_ckv_g, w_uq, w_ukv, b_q_g, b_k_g,
             c_q_g, c_k_g, w_out, router_w, router_b):
    o = np.cumsum((0, 256, 128, 128, 384, 256, 32, 256, 128, 128))
    seg = lambda k: w_in[:, :, o[k]:o[k + 1]]
    w_in_r = jnp.concatenate([seg(0), seg(1), seg(2), seg(3), seg(4), seg(6), seg(7), seg(8), seg(5),
                              jnp.zeros((DEPTH, D, LANE - B_ROPE), F32)], axis=-1).astype(BF16)
    ukv = w_ukv.reshape(DEPTH, B_KV_LORA, 4, B_NOPE + B_V)
    w_uk = _pad_heads(ukv[..., :B_NOPE].reshape(DEPTH, B_KV_LORA, 4 * B_NOPE), B_NOPE, BP)
    w_uv = ukv[..., B_NOPE:].reshape(DEPTH, B_KV_LORA, 4 * B_V)
    rw_hi = router_w.astype(BF16)
    rw_lo = (router_w - rw_hi.astype(F32)).astype(BF16)
    tile = lambda v, n: jnp.tile(v, (1, n))
    rows = dict(n1g=norm1_g, n2g=norm2_g, aqg=tile(a_q_g, 4), akg=tile(a_k_g, 2), cqg=tile(c_q_g, 4),
                ckg=tile(c_k_g, 2), bcqg=b_cq_g, bckvg=b_ckv_g,
                bqg=_pad_heads(tile(b_q_g, 4), B_QK, BP), bkg=_pad_heads(tile(b_k_g, 4), B_QK, BP))
    order = sorted(VEC_ROWS, key=lambda n: VEC_ROWS[n][0])
    assert all(rows[n].shape == (DEPTH, VEC_ROWS[n][1]) for n in order)
    vec = jnp.stack([jnp.pad(rows[n], ((0, 0), (0, D - rows[n].shape[1]))) for n in order], axis=1)
    return dict(
        vec=vec[:, :, None, :], w_in=w_in_r,
        w_uq=_pad_heads(w_uq, B_QK, BP).astype(BF16), w_uk=w_uk.astype(BF16), w_uv=w_uv.astype(BF16),
        w_out=w_out.astype(BF16),
        rwt=jnp.concatenate([jnp.swapaxes(rw_hi, 1, 2), jnp.swapaxes(rw_lo, 1, 2)], axis=1),
        rb=router_b[:, :, None])


def _moe(l, x_all, h2p, gate_slab, route, counts, mod, w_gu, b_gu, w_dn, b_dn):
    counts = counts[:, 0].astype(jnp.int32)
    padded = (counts + MOE_STEP - 1) // MOE_STEP * MOE_STEP
    pend = jnp.cumsum(padded)
    pstart = pend - padded
    n_used = (pend[-1] // MOE_STEP).astype(jnp.int32)
    steps = jnp.minimum(jnp.arange(MOE_STEPS, dtype=jnp.int32), n_used - 1)
    tile_expert = jnp.sum((pend[None, :] <= steps[:, None] * MOE_STEP).astype(jnp.int32), axis=1)
    tile_expert = jnp.minimum(tile_expert, N_EXPERTS - 1)
    of_expert = lambda table: jnp.sum(
        jnp.where(tile_expert[:, None] == jnp.arange(N_EXPERTS)[None, :], table[None, :], 0), axis=1)
    group_end_step = of_expert(pend) // MOE_STEP
    after = jnp.sum((pend[None, :] <= group_end_step[:, None] * MOE_STEP).astype(jnp.int32), axis=1)
    next_expert = jnp.where(group_end_step < n_used, jnp.minimum(after, N_EXPERTS - 1), -1).astype(jnp.int32)
    rows_left = of_expert(pstart + counts) - steps * MOE_STEP
    n_valid = jnp.clip((rows_left + MOE_TM - 1) // MOE_TM, 1, MOE_STEP // MOE_TM).astype(jnp.int32)
    e = route[TOP_K:2 * TOP_K].astype(jnp.int32)
    rank = route[2 * TOP_K:3 * TOP_K].astype(jnp.int32)
    start = jnp.sum(jnp.where(e[:, :, None] == jnp.arange(N_EXPERTS)[None, None, :], pstart[None, None, :], 0), axis=-1)
    dest = (start + rank).reshape(TOP_K, T_ALL // SC_CHUNK, SC_CHUNK).transpose(1, 0, 2)
    x_sorted = _dispatch_rows(h2p, dest)
    changed = jnp.concatenate([jnp.zeros((1,), jnp.int32), (tile_expert[1:] != tile_expert[:-1]).astype(jnp.int32)])
    slot = jnp.cumsum(changed) % 2
    y = _experts(l, tile_expert, n_used.reshape(1), next_expert, n_valid, slot.astype(jnp.int32), x_sorted,
                 w_gu, b_gu, w_dn, b_dn)
    ctx_chunks, lat_chunks = T_CTX // SC_CHUNK, T_LAT // SC_CHUNK
    y_ctx = _gather_rows(y, dest, 0, ctx_chunks)
    out_ctx = _combine(l, 0, N_CTX_TILES, x_all, y_ctx, gate_slab, mod)
    y_lat = _gather_rows(y, dest, ctx_chunks, lat_chunks, out_ctx)
    return out_ctx, _combine(l, N_CTX_TILES, N_LAT_TILES, x_all, y_lat, gate_slab, mod)


def kernel(x_prompt, x_sample, cache_a_k, cache_a_v, cache_b_ckv, cache_b_krope, cache_c_k, cache_c_v, c, c_ctx,
           norm1_g, norm2_g, w_mod, b_mod, w_in, a_q_g, a_k_g, a_sink, b_cq_g, b_ckv_g, w_uq, w_ukv, b_q_g, b_k_g,
           c_q_g, c_k_g, w_out, router_w, router_b, w_gu, b_gu, w_dn, b_dn):
    consts = _constants()
    cond = jnp.concatenate([c_ctx[None, :], c, jnp.zeros((3, D), F32)], axis=0)
    mod = _modulation(cond, w_mod, b_mod).reshape(DEPTH, 8, 1, 6 * D)
    x_ctx, x_lat, x_lat_off = x_prompt.reshape(T_CTX, D), x_sample.reshape(T_LAT, D), 0

    lw = _weights(norm1_g, norm2_g, w_in, a_q_g, a_k_g, b_cq_g, b_ckv_g, w_uq, w_ukv, b_q_g, b_k_g,
                  c_q_g, c_k_g, w_out, router_w, router_b)
    ckb, cvb = _cache_kv(cache_b_ckv, cache_b_krope, lw, consts)
    merge_heads = lambda a: a.reshape(N_LAT_SEQ, DEPTH, PAST, 2 * HD).astype(BF16)
    cache = dict(ka=merge_heads(cache_a_k), va=merge_heads(cache_a_v), kb=ckb, vb=cvb,
                 kc=merge_heads(cache_c_k), vc=merge_heads(cache_c_v))

    states = ()
    names = ["qa", "ka", "va", "qb", "kb", "vb", "qc", "kc", "vc"]
    for l in range(DEPTH):
        outs = _projection(False, l, x_ctx, 0, mod, lw, consts, states)
        p_ctx = dict(zip(names, outs[:9]))
        states = tuple(outs[9:])
        p_lat = dict(zip(names, _projection(True, l, x_lat, x_lat_off, mod, lw, consts)))
        mix_ctx = _attention_ctx(l, a_sink, p_ctx)
        mix_lat = _attention_lat(l, a_sink, p_lat, cache)
        x_mid, h2p, gate_slab, route, counts = _post_attention(l, mix_ctx, mix_lat, x_ctx, x_lat, x_lat_off, mod, lw,
                                                               consts)
        x_ctx, x_lat = _moe(l, x_mid, h2p, gate_slab, route, counts, mod, w_gu, b_gu, w_dn, b_dn)
    y_ctx, y_lat = x_ctx, x_lat

    return (y_ctx.reshape(N_CTX_SEQ, CTX_LEN, D), y_lat.reshape(N_LAT_SEQ, LAT_LEN, D)) + states
```

```python
import functools

import jax
import jax.numpy as jnp
import numpy as np
from jax import lax
from jax.experimental import pallas as pl
from jax.experimental.pallas import tpu as pltpu
from jax.experimental.pallas import tpu_sc as plsc

F32 = jnp.float32
BF16 = jnp.bfloat16

D = 1024
DEPTH = 2
N_CTX_SEQ, CTX_LEN = 16, 256
N_LAT_SEQ, LAT_LEN = 4, 2048
PAST = 512
T_CTX = N_CTX_SEQ * CTX_LEN
T_LAT = N_LAT_SEQ * LAT_LEN
T_ALL = T_CTX + T_LAT
GRID_W = 64
HD = 64
WINDOW = 128
B_NOPE, B_ROPE, B_V = 64, 32, 128
B_QK = B_NOPE + B_ROPE
B_Q_LORA, B_KV_LORA = 384, 256
N_EXPERTS, TOP_K = 32, 4
D_FF = 1024
SWIGLU_LIMIT = 7.0
SWIGLU_ALPHA = 1.702
ROPE_THETA = 10000.0
EPS = 1e-6
NEG_INF = -1e30

TM = 256
TP = 1024
TQ_LAT = 1024
TQ_CTX = 256
TC = 512
SEQ_PER_STEP = TQ_CTX // CTX_LEN
LANE = 128
BP = 128
N_CTX_TILES = T_CTX // TM
N_LAT_TILES = T_LAT // TM
N_TILES = T_ALL // TM
LAT_TILES_PER_SEQ = LAT_LEN // TM
MOE_TM = 256
MOE_STEP = 4 * MOE_TM
FF_CHUNK = 256
MOE_STEPS = T_ALL * TOP_K // MOE_STEP + N_EXPERTS
MOE_ROWS = MOE_STEPS * MOE_STEP
DW = D // 2
VMEM_LIMIT = 56 * 1024 * 1024

C_QA, C_KA, C_VA, C_CQ, C_CKV, C_QC, C_KC, C_VC, C_KR, C_END = (
    0, 256, 384, 512, 896, 1152, 1408, 1536, 1664, 1792)


def _dot(a, b):
    return jnp.dot(a, b, preferred_element_type=F32)


def _dot_t(a, b):
    return lax.dot_general(a, b, (((1,), (1,)), ((), ())), preferred_element_type=F32)


def _rms(x, g):
    return x * lax.rsqrt(jnp.mean(x * x, axis=-1, keepdims=True) + EPS) * g


def _seg_norm(x, seg, g, n):
    ss = _dot((x * x).astype(BF16), seg)
    return x * lax.rsqrt(ss * (1.0 / n) + EPS) * g


def _rope(x, cos, sin, first, sh):
    w = x.shape[1]
    fwd = pltpu.roll(x, w - sh, 1)
    bwd = pltpu.roll(x, sh, 1)
    return x * cos + jnp.where(first > 0.5, fwd, bwd) * sin


MOD_BN = 1536


def _mod_kernel(c_ref, w_ref, b_ref, o_ref):
    c = c_ref[...]
    s = (c * jax.nn.sigmoid(c)).astype(BF16)
    o_ref[...] = _dot(s, w_ref[...].astype(BF16)) + b_ref[...]


def _modulation(cond, w_mod, b_mod):
    return pl.pallas_call(
        _mod_kernel,
        grid=(DEPTH, 6 * D // MOD_BN),
        in_specs=[
            pl.BlockSpec((8, D), lambda l, j: (0, 0)),
            pl.BlockSpec((None, D, MOD_BN), lambda l, j: (l, 0, j)),
            pl.BlockSpec((None, 1, MOD_BN), lambda l, j: (l, 0, j)),
        ],
        out_specs=pl.BlockSpec((None, 8, MOD_BN), lambda l, j: (l, 0, j)),
        out_shape=jax.ShapeDtypeStruct((DEPTH, 8, 6 * D), F32),
        compiler_params=pltpu.CompilerParams(
            dimension_semantics=("arbitrary", "arbitrary"), vmem_limit_bytes=VMEM_LIMIT),
        name="modulation",
    )(cond, w_mod, b_mod.reshape(DEPTH, 1, 6 * D))


def _proj_kernel(is_lat, n_aliased, *refs):
    (x_ref, mod_ref, n1g_ref, win_ref, seg64_ref, seg96_ref, aqg_ref, akg_ref, cqg_ref, ckg_ref,
     bcqg_ref, bckvg_ref, bqg_ref, bkg_ref, wuq_ref, wuk_ref, wuv_ref, plc_ref) = refs[:18]
    refs = refs[18 + n_aliased:]
    if is_lat:
        (cos64_ref, sin64_ref, cos96_ref, sin96_ref, f64_ref, f96_ref) = refs[:6]
        refs = refs[6:]
    (qa_ref, ka_ref, va_ref, qb_ref, kb_ref, vb_ref, qc_ref, kc_ref, vc_ref) = refs[:9]
    refs = refs[9:]
    if not is_lat:
        if n_aliased == 0:
            for r in refs:
                r[:, 1:] = jnp.zeros((SEQ_PER_STEP, DEPTH - 1) + r.shape[2:], F32)
            refs = [r.at[:, 0] for r in refs]
        (kas_ref, vas_ref, ckvs_ref, krs_ref, kcs_ref, vcs_ref) = refs

    x = x_ref[...]
    m = mod_ref[...]
    sh1, sc1 = m[:, 0:D], m[:, D:2 * D]
    h = (_rms(x, n1g_ref[...]) * (1.0 + sc1) + sh1).astype(BF16)

    def proj(a, b):
        return _dot(h, win_ref[:, a:b])

    def rope64(t):
        wd = t.shape[1]
        return _rope(t, cos64_ref[:, :wd], sin64_ref[:, :wd], f64_ref[:, :wd], 16) if is_lat else t

    def rope96(t):
        return _rope(t, cos96_ref[...], sin96_ref[...], f96_ref[...], 8) if is_lat else t

    seg64 = seg64_ref[...]
    seg64h = seg64_ref[0:2 * HD, 0:2 * HD]
    seg96 = seg96_ref[...]

    def seg_sum(t, seg):
        return _dot((t * t).astype(BF16), seg)

    def seg_finish(t, ss, g, n):
        return t * lax.rsqrt(ss * (1.0 / n) + EPS) * g

    p_cq, p_ckv, p_kr = proj(C_CQ, C_CKV), proj(C_CKV, C_QC), proj(C_KR, C_END)
    p_qa, p_ka, p_qc, p_kc = proj(C_QA, C_KA), proj(C_KA, C_VA), proj(C_QC, C_KC), proj(C_KC, C_VC)
    p_va, p_vc = proj(C_VA, C_CQ), proj(C_VC, C_KR)

    cq = _rms(p_cq, bcqg_ref[...]).astype(BF16)
    ckv = _rms(p_ckv, bckvg_ref[...])
    ckv16 = ckv.astype(BF16)
    u_q = _dot(cq, wuq_ref[...])
    u_k = _dot(ckv16, wuk_ref[...]) + _dot(p_kr.astype(BF16), plc_ref[...])
    u_v = _dot(ckv16, wuv_ref[...])
    ss_qa, ss_ka = seg_sum(p_qa, seg64), seg_sum(p_ka, seg64h)
    ss_qc, ss_kc = seg_sum(p_qc, seg64), seg_sum(p_kc, seg64h)
    ss_qb, ss_kb = seg_sum(u_q, seg96), seg_sum(u_k, seg96)

    def store_state(ref, t):
        for s in range(SEQ_PER_STEP):
            ref[s] = t[s * CTX_LEN:(s + 1) * CTX_LEN]

    def store_kv_state(ref, t):
        for s in range(SEQ_PER_STEP):
            ref[s] = t[s * CTX_LEN:(s + 1) * CTX_LEN].reshape(CTX_LEN, 2, HD)

    if not is_lat:
        store_kv_state(vas_ref, p_va)
        store_kv_state(vcs_ref, p_vc)
        store_state(ckvs_ref, ckv)
        store_state(krs_ref, p_kr[:, 0:B_ROPE])
    va_ref[...] = p_va.astype(BF16)
    vc_ref[...] = p_vc.astype(BF16)
    vb_ref[...] = u_v.astype(BF16)

    t = seg_finish(p_qa, ss_qa, aqg_ref[...], HD)
    qa_ref[...] = (rope64(t) * (HD ** -0.5 * LOG2E)).astype(BF16)
    t = seg_finish(p_ka, ss_ka, akg_ref[...], HD)
    if not is_lat:
        store_kv_state(kas_ref, t)
    ka_ref[...] = rope64(t).astype(BF16)
    t = seg_finish(p_qc, ss_qc, cqg_ref[...], HD)
    qc_ref[...] = (rope64(t) * (HD ** -0.5 * LOG2E)).astype(BF16)
    t = seg_finish(p_kc, ss_kc, ckg_ref[...], HD)
    if not is_lat:
        store_kv_state(kcs_ref, t)
    kc_ref[...] = rope64(t).astype(BF16)
    t = seg_finish(u_q, ss_qb, bqg_ref[...], B_QK)
    qb_ref[...] = (rope96(t) * (B_QK ** -0.5 * LOG2E)).astype(BF16)
    t = seg_finish(u_k, ss_kb, bkg_ref[...], B_QK)
    kb_ref[...] = rope96(t).astype(BF16)


def _full(shape):
    nd = len(shape)
    return pl.BlockSpec(shape, lambda i: (0,) * nd)


STATE_TAILS = ((2, HD), (2, HD), (B_KV_LORA,), (B_ROPE,), (2, HD), (2, HD))


def _layer_spec(a, l):
    nd = a.ndim - 1
    return pl.BlockSpec((None,) + a.shape[1:], lambda *_: (l,) + (0,) * nd)


VEC_ROWS = dict(n1g=(0, D), n2g=(1, D), aqg=(2, 4 * HD), akg=(3, 2 * HD), cqg=(4, 4 * HD), ckg=(5, 2 * HD),
                bcqg=(6, B_Q_LORA), bckvg=(7, B_KV_LORA), bqg=(8, 4 * BP), bkg=(9, 4 * BP))


def _vec_spec(l, name):
    row, width = VEC_ROWS[name]
    return pl.BlockSpec((None, None, 1, width), lambda *_: (l, row, 0, 0))


def _projection(is_lat, l, x_src, x_off, mod, lw, consts, prev_states=()):
    assert is_lat or bool(prev_states) == (l > 0)
    t_rows = T_LAT if is_lat else T_CTX
    TQ = TQ_LAT if is_lat else TQ_CTX
    n_tiles = t_rows // TQ
    if is_lat:
        mod_map = lambda i: (l, 1 + i // (LAT_LEN // TQ), 0, 0)
    else:
        mod_map = lambda i: (l, 0, 0, 0)
    vec = lw["vec"]
    gain = lambda name: (vec, _vec_spec(l, name))
    whole = lambda a: (a, _full(a.shape))
    layer = lambda a: (a, _layer_spec(a, l))
    pairs = [(x_src, pl.BlockSpec((TQ, D), lambda i: (i + x_off, 0))),
             (mod, pl.BlockSpec((None, None, 1, 6 * D), mod_map)),
             gain("n1g"), layer(lw["w_in"]), whole(consts["seg64"]), whole(consts["seg96"]),
             gain("aqg"), gain("akg"), gain("cqg"), gain("ckg"), gain("bcqg"), gain("bckvg"), gain("bqg"), gain("bkg"),
             layer(lw["w_uq"]), layer(lw["w_uk"]), layer(lw["w_uv"]), whole(consts["place"])]
    ins = [a for a, _ in pairs]
    in_specs = [s for _, s in pairs]
    n_plain = len(ins)
    ins += list(prev_states)
    in_specs += [pl.BlockSpec(memory_space=pl.ANY) for _ in prev_states]
    if is_lat:
        tabs = [consts["cos64"], consts["sin64"], consts["cos96"], consts["sin96"]]
        ins += tabs + [consts["first64"], consts["first96"]]
        in_specs += [pl.BlockSpec((TQ, a.shape[1]), lambda i: (i % (LAT_LEN // TQ), 0)) for a in tabs]
        in_specs += [_full(consts["first64"].shape), _full(consts["first96"].shape)]
    widths = [4 * HD, 2 * HD, 2 * HD, 4 * BP, 4 * BP, 4 * B_V, 4 * HD, 2 * HD, 2 * HD]
    out_shape = [jax.ShapeDtypeStruct((t_rows, w), BF16) for w in widths]
    out_specs = [pl.BlockSpec((TQ, w), lambda i: (i, 0)) for w in widths]
    if not is_lat:
        for tail in STATE_TAILS:
            zeros = (0,) * len(tail)
            out_shape.append(jax.ShapeDtypeStruct((N_CTX_SEQ, DEPTH, CTX_LEN) + tail, F32))
            if prev_states:
                out_specs.append(pl.BlockSpec((SEQ_PER_STEP, None, CTX_LEN) + tail, lambda i, z=zeros: (i, l, 0) + z))
            else:
                out_specs.append(pl.BlockSpec((SEQ_PER_STEP, DEPTH, CTX_LEN) + tail, lambda i, z=zeros: (i, 0, 0) + z))
    aliases = {n_plain + k: len(widths) + k for k in range(len(prev_states))}
    return pl.pallas_call(
        functools.partial(_proj_kernel, is_lat, len(prev_states)),
        grid=(n_tiles,),
        in_specs=in_specs,
        out_specs=out_specs,
        out_shape=out_shape,
        input_output_aliases=aliases,
        compiler_params=pltpu.CompilerParams(
            dimension_semantics=("arbitrary",), vmem_limit_bytes=VMEM_LIMIT),
        name="proj_lat" if is_lat else "proj_ctx",
    )(*ins)


def _cache_kv_kernel(ckv_ref, kr_ref, seg96_ref, bkg_ref, wuk_ref, wuv_ref, plc_ref, kb_ref, vb_ref):
    ckv16 = ckv_ref[...].astype(BF16)
    vb_ref[...] = _dot(ckv16, wuv_ref[...]).astype(BF16)
    kpre = _dot(ckv16, wuk_ref[...]) + _dot(kr_ref[...].astype(BF16), plc_ref[0:B_ROPE, :])
    kb_ref[...] = _seg_norm(kpre, seg96_ref[...], bkg_ref[...], B_QK).astype(BF16)


def _cache_kv(cache_b_ckv, cache_b_krope, lw, consts):
    def c_map(l, r):
        return (r // (PAST // TM), l, r % (PAST // TM), 0)

    def w_spec(a):
        return pl.BlockSpec((None,) + a.shape[1:], lambda l, r: (l,) + (0,) * (a.ndim - 1))

    def s_spec(a):
        return pl.BlockSpec(a.shape, lambda l, r: (0,) * a.ndim)

    rows = N_LAT_SEQ * PAST
    return pl.pallas_call(
        _cache_kv_kernel,
        grid=(DEPTH, rows // TM),
        in_specs=[pl.BlockSpec((None, None, TM, B_KV_LORA), c_map), pl.BlockSpec((None, None, TM, B_ROPE), c_map),
                  s_spec(consts["seg96"]),
                  pl.BlockSpec((None, None, 1, 4 * BP), lambda l, r: (l, VEC_ROWS["bkg"][0], 0, 0)),
                  w_spec(lw["w_uk"]), w_spec(lw["w_uv"]),
                  s_spec(consts["place"])],
        out_specs=[pl.BlockSpec((None, TM, 4 * BP), lambda l, r: (l, r, 0)),
                   pl.BlockSpec((None, TM, 4 * B_V), lambda l, r: (l, r, 0))],
        out_shape=[jax.ShapeDtypeStruct((DEPTH, rows, 4 * BP), BF16),
                   jax.ShapeDtypeStruct((DEPTH, rows, 4 * B_V), BF16)],
        compiler_params=pltpu.CompilerParams(
            dimension_semantics=("arbitrary", "arbitrary"), vmem_limit_bytes=VMEM_LIMIT),
        name="cache_kv",
    )(cache_b_ckv, cache_b_krope, consts["seg96"], lw["vec"], lw["w_uk"], lw["w_uv"], consts["place"])


def _scores(q, segs):
    scores = []
    for k, _, mask in segs:
        s = _dot_t(q, k)
        if mask is not None:
            s = jnp.where(mask, s, NEG_INF)
        scores.append(s)
    return scores


def _softmax_pv(scores, segs, sink, num_cols, den_col):
    m = None
    for s in scores:
        sm = jnp.max(s, axis=-1, keepdims=True)
        m = sm if m is None else jnp.maximum(m, sm)
    if sink is not None:
        m = jnp.maximum(m, sink)
    acc = None
    den = None
    for s, (_, v, _) in zip(scores, segs):
        e = jnp.exp2(s - m)
        if den_col is None:
            d = jnp.sum(e, axis=-1, keepdims=True)
            den = d if den is None else den + d
        o = _dot(e.astype(BF16), v)
        acc = o if acc is None else acc + o
    if den_col is not None:
        den = acc[:, den_col:den_col + 1]
        acc = acc[:, num_cols]
    if sink is not None:
        den = den + jnp.exp2(sink - m)
    return acc / den


HEAD_LOOKAHEAD = 1


def _run_heads(jobs, o_ref):
    ready = [_scores(job[0](), job[1]()) for job in jobs[:HEAD_LOOKAHEAD]]
    for n, (_, load_segs, sink, col, num_cols, den_col) in enumerate(jobs):
        scores = ready.pop(0)
        if n + HEAD_LOOKAHEAD < len(jobs):
            ahead = jobs[n + HEAD_LOOKAHEAD]
            ready.append(_scores(ahead[0](), ahead[1]()))
        o = _softmax_pv(scores, load_segs(values=True), sink, num_cols, den_col)
        o_ref[:, col:col + o.shape[1]] = o.astype(BF16)


OUT_B = 4 * HD
OUT_C = 4 * HD + 4 * B_V
LOG2E = 1.4426950408889634


def _head_job(q_ref, q_cols, seg_refs, kv_cols, v_cols, sink, col, rows=None, masks=None, mxu_sums=True):
    width = v_cols.stop - v_cols.start
    if not mxu_sums:
        load_cols, num_cols, den_col = v_cols, None, None
    elif width == HD:
        pair = v_cols.start // LANE * LANE
        load_cols = slice(pair, pair + LANE)
        lo = v_cols.start - pair
        num_cols, den_col = slice(lo, lo + HD), (lo + HD) % LANE
    else:
        load_cols, num_cols, den_col = v_cols, slice(0, width), width
    if sink is not None:
        sink = sink * LOG2E

    def with_ones(v):
        if not mxu_sums:
            return v
        if width != HD:
            return jnp.concatenate([v, jnp.ones_like(v)], axis=1)
        lane = lax.broadcasted_iota(jnp.int32, (1, LANE), 1)
        keep = jnp.where((lane >= num_cols.start) & (lane < num_cols.stop), 1.0, 0.0).astype(BF16)
        return v * keep + (1.0 - keep)

    def load_segs(values=False):
        segs = []
        for n, (k_ref, v_ref) in enumerate(seg_refs):
            r = rows if (rows is not None and n == 0) else slice(None)
            if values:
                segs.append((None, with_ones(v_ref[r, load_cols]), None))
            else:
                segs.append((k_ref[r, kv_cols], None, None if masks is None else masks[n]))
        return segs

    return (lambda: q_ref[:, q_cols]), load_segs, sink, col, num_cols, den_col


def _attn_ctx_kernel(l, sink_ref, qa_ref, ka_ref, va_ref, qb_ref, kb_ref, vb_ref, qc_ref, kc_ref, vc_ref, o_ref):
    jobs = []
    for h in range(4):
        cs = slice(h // 2 * HD, (h // 2 + 1) * HD)
        jobs.append(_head_job(qa_ref, slice(h * HD, (h + 1) * HD), [(ka_ref, va_ref)], cs, cs, sink_ref[l, h],
                              h * HD, mxu_sums=False))
    for h in range(4):
        ks, vs = slice(h * BP, (h + 1) * BP), slice(h * B_V, (h + 1) * B_V)
        jobs.append(_head_job(qb_ref, ks, [(kb_ref, vb_ref)], ks, vs, None, OUT_B + h * B_V, mxu_sums=False))
    for h in range(4):
        cs = slice(h // 2 * HD, (h // 2 + 1) * HD)
        jobs.append(_head_job(qc_ref, slice(h * HD, (h + 1) * HD), [(kc_ref, vc_ref)], cs, cs, None, OUT_C + h * HD,
                              mxu_sums=False))
    _run_heads(jobs, o_ref)


def _attention_ctx(l, sink, p):
    names = ["qa", "ka", "va", "qb", "kb", "vb", "qc", "kc", "vc"]
    ins = [p[n] for n in names]
    in_specs = [pl.BlockSpec(memory_space=pltpu.SMEM)]
    in_specs += [pl.BlockSpec((CTX_LEN, a.shape[1]), lambda i: (i, 0)) for a in ins]
    return pl.pallas_call(
        functools.partial(_attn_ctx_kernel, l),
        grid=(N_CTX_SEQ,),
        in_specs=in_specs,
        out_specs=pl.BlockSpec((CTX_LEN, D), lambda i: (i, 0)),
        out_shape=jax.ShapeDtypeStruct((T_CTX, D), BF16),
        compiler_params=pltpu.CompilerParams(
            dimension_semantics=("arbitrary",), vmem_limit_bytes=VMEM_LIMIT),
        name="attn_ctx",
    )(sink, *ins)


WIN_SPAN = TM + 2 * WINDOW


def _attn_lat_kernel(l, sink_ref, qa_ref, qb_ref, qc_ref, ka_ref, va_ref, kb_ref, vb_ref, kc_ref, vc_ref,
                     cka_ref, cva_ref, ckb_ref, cvb_ref, ckc_ref, cvc_ref, o_ref):
    qi = pl.program_id(1)
    ws = pl.multiple_of(jnp.clip(qi * TM - WINDOW, 0, LAT_LEN - WIN_SPAN), WINDOW)
    qpos = qi * TM + lax.broadcasted_iota(jnp.int32, (TM, WIN_SPAN), 0)
    kpos = ws + lax.broadcasted_iota(jnp.int32, (TM, WIN_SPAN), 1)
    band = jnp.abs(qpos - kpos) <= WINDOW
    jobs = []
    for h in range(4):
        cs = slice(h // 2 * HD, (h // 2 + 1) * HD)
        jobs.append(_head_job(qa_ref, slice(h * HD, (h + 1) * HD), [(ka_ref, va_ref), (cka_ref, cva_ref)], cs, cs,
                              sink_ref[l, h], h * HD, rows=pl.ds(ws, WIN_SPAN), masks=(band, None)))
    for h in range(4):
        ks, vs = slice(h * BP, (h + 1) * BP), slice(h * B_V, (h + 1) * B_V)
        jobs.append(_head_job(qb_ref, ks, [(kb_ref, vb_ref), (ckb_ref, cvb_ref)], ks, vs, None, OUT_B + h * B_V))
    for h in range(4):
        cs = slice(h // 2 * HD, (h // 2 + 1) * HD)
        jobs.append(_head_job(qc_ref, slice(h * HD, (h + 1) * HD), [(kc_ref, vc_ref), (ckc_ref, cvc_ref)], cs, cs,
                              None, OUT_C + h * HD))
    _run_heads(jobs, o_ref)


def _attention_lat(l, sink, p, cache):
    q_ins = [p["qa"], p["qb"], p["qc"]]
    kv_ins = [p[n] for n in ("ka", "va", "kb", "vb", "kc", "vc")]
    c_names = ("ka", "va", "kb", "vb", "kc", "vc")
    c_ins = [cache[n] for n in c_names]
    in_specs = [pl.BlockSpec(memory_space=pltpu.SMEM)]
    in_specs += [pl.BlockSpec((TM, a.shape[1]), lambda b, i: (b * LAT_TILES_PER_SEQ + i, 0)) for a in q_ins]
    in_specs += [pl.BlockSpec((LAT_LEN, a.shape[1]), lambda b, i: (b, 0)) for a in kv_ins]
    for n, a in zip(c_names, c_ins):
        if n in ("kb", "vb"):
            in_specs.append(pl.BlockSpec((None, PAST, a.shape[-1]), lambda b, i: (l, b, 0)))
        else:
            in_specs.append(pl.BlockSpec((None, None, PAST, a.shape[-1]), lambda b, i: (b, l, 0, 0)))
    return pl.pallas_call(
        functools.partial(_attn_lat_kernel, l),
        grid=(N_LAT_SEQ, LAT_TILES_PER_SEQ),
        in_specs=in_specs,
        out_specs=pl.BlockSpec((TM, D), lambda b, i: (b * LAT_TILES_PER_SEQ + i, 0)),
        out_shape=jax.ShapeDtypeStruct((T_LAT, D), BF16),
        compiler_params=pltpu.CompilerParams(
            dimension_semantics=("arbitrary", "arbitrary"), vmem_limit_bytes=VMEM_LIMIT),
        name="attn_lat",
    )(sink, *q_ins, *kv_ins, *c_ins)


def _pack_rows(x):
    half = x.shape[1] // 2
    r = x.astype(BF16).astype(F32)
    hi = lax.bitcast_convert_type(r[:, :half], jnp.int32)
    lo = lax.bitcast_convert_type(r[:, half:], jnp.int32)
    return jnp.bitwise_or(hi, lax.shift_right_logical(lo, 16))


def _unpack_rows(p):
    a = lax.bitcast_convert_type(jnp.bitwise_and(p, -65536), F32)
    b = lax.bitcast_convert_type(lax.shift_left(p, 16), F32)
    return jnp.concatenate([a, b], axis=1).astype(BF16)


ROUTE_ROWS = 16


def _post_kernel(mc_ref, ml_ref, xc_ref, xl_ref, mod_ref, n2g_ref, wout_ref, rwt_ref, rb_ref, tri_ref,
                 xo_ref, h2_ref, gate_ref, route_ref, cnt_ref, run_ref):
    i = pl.program_id(0)

    @pl.when(i == 0)
    def _():
        run_ref[...] = jnp.zeros_like(run_ref)
        cnt_ref[...] = jnp.zeros_like(cnt_ref)

    is_ctx = i < T_CTX // TP
    m = mod_ref[...]
    g1, sh2, sc2 = m[:, 2 * D:3 * D], m[:, 3 * D:4 * D], m[:, 4 * D:5 * D]

    mix = jnp.where(is_ctx, mc_ref[...], ml_ref[...])
    x = jnp.where(is_ctx, xc_ref[...], xl_ref[...]) + g1 * _dot(mix, wout_ref[...])
    xo_ref[...] = x
    h2 = _rms(x, n2g_ref[...]) * (1.0 + sc2) + sh2
    hi = h2.astype(BF16)
    h2_ref[...] = _pack_rows(h2)
    lo = (h2 - hi.astype(F32)).astype(BF16)
    a = _dot_t(rwt_ref[...], hi)
    b = _dot_t(rwt_ref[0:N_EXPERTS, :], lo)
    logits = a[0:N_EXPERTS] + a[N_EXPERTS:] + b + rb_ref[...]
    eidx = lax.broadcasted_iota(jnp.int32, logits.shape, 0).astype(F32)
    work = logits
    vals, idxs, hots = [], [], []
    for _ in range(TOP_K):
        v = jnp.max(work, axis=0, keepdims=True)
        idx = jnp.min(jnp.where(work == v, eidx, float(N_EXPERTS)), axis=0, keepdims=True)
        hot = eidx == idx
        vals.append(v)
        idxs.append(idx)
        hots.append(hot)
        work = jnp.where(hot, -jnp.inf, work)
    es = [jnp.exp(v - vals[0]) for v in vals]
    den = es[0] + es[1] + es[2] + es[3]
    gates = [e / den for e in es]
    sel = jnp.where(hots[0] | hots[1] | hots[2] | hots[3], 1.0, 0.0)
    run = run_ref[:, 0:1]
    before = _dot(sel.astype(BF16), tri_ref[...]) + run
    run_new = jnp.broadcast_to(run + jnp.sum(sel, axis=1, keepdims=True), run_ref.shape)
    run_ref[...] = run_new
    cnt_ref[...] = run_new
    ranks = [jnp.sum(jnp.where(hots[k], before, 0.0), axis=0, keepdims=True) for k in range(TOP_K)]
    pad = [jnp.zeros((ROUTE_ROWS - 3 * TOP_K, TP), F32)]
    route_ref[...] = jnp.concatenate(gates + idxs + ranks + pad, axis=0)
    gate_ref[...] = jnp.concatenate(gates + [jnp.zeros((LANE - TOP_K, TP), F32)], axis=0).T


def _mod_row(i, tm=TM):
    n_ctx = T_CTX // tm
    return jnp.where(i < n_ctx, 0, 1 + (i - n_ctx) // (LAT_LEN // tm))


def _ctx_tile(i):
    return jnp.minimum(i, T_CTX // TP - 1)


def _lat_tile(i):
    return jnp.maximum(i - T_CTX // TP, 0)


def _post_attention(l, mix_ctx, mix_lat, x_ctx, x_lat, x_lat_off, mod, lw, consts):
    ins = [mix_ctx, mix_lat, x_ctx, x_lat, mod, lw["vec"], lw["w_out"], lw["rwt"], lw["rb"], consts["tri"]]
    in_specs = [
        pl.BlockSpec((TP, D), lambda i: (_ctx_tile(i), 0)),
        pl.BlockSpec((TP, D), lambda i: (_lat_tile(i), 0)),
        pl.BlockSpec((TP, D), lambda i: (_ctx_tile(i), 0)),
        pl.BlockSpec((TP, D), lambda i: (_lat_tile(i) + x_lat_off, 0)),
        pl.BlockSpec((None, None, 1, 6 * D), lambda i: (l, _mod_row(i, TP), 0, 0)),
        _vec_spec(l, "n2g"),
    ] + [_layer_spec(a, l) for a in ins[6:9]] + [_full(consts["tri"].shape)]
    return pl.pallas_call(
        _post_kernel,
        grid=(T_ALL // TP,),
        in_specs=in_specs,
        out_specs=[pl.BlockSpec((TP, D), lambda i: (i, 0)), pl.BlockSpec((TP, DW), lambda i: (i, 0)),
                   pl.BlockSpec((TP, LANE), lambda i: (i, 0)), pl.BlockSpec((ROUTE_ROWS, TP), lambda i: (0, i)),
                   pl.BlockSpec((N_EXPERTS, LANE), lambda i: (0, 0))],
        out_shape=[jax.ShapeDtypeStruct((T_ALL, D), F32), jax.ShapeDtypeStruct((T_ALL, DW), jnp.int32),
                   jax.ShapeDtypeStruct((T_ALL, LANE), F32), jax.ShapeDtypeStruct((ROUTE_ROWS, T_ALL), F32),
                   jax.ShapeDtypeStruct((N_EXPERTS, LANE), F32)],
        scratch_shapes=[pltpu.VMEM((N_EXPERTS, LANE), F32)],
        compiler_params=pltpu.CompilerParams(
            dimension_semantics=("arbitrary",), vmem_limit_bytes=VMEM_LIMIT),
        name="post_attn",
    )(*ins)


def _expert_kernel(l, te_ref, nu_ref, nxt_ref, nv_ref, slot_ref, x_ref, wgu_hbm, bgu_ref, wdn_hbm, bdn_ref, y_ref,
                   wgu32, wdn32, wgu16, wdn16, sem):
    i = pl.program_id(0)
    prev = te_ref[jnp.maximum(i - 1, 0)]
    new_expert = jnp.logical_or(i == 0, te_ref[i] != prev)
    slot = slot_ref[i]

    def weight_copies(e, s):
        return (pltpu.make_async_copy(wgu_hbm.at[l, e], wgu32.at[s], sem.at[s, 0]),
                pltpu.make_async_copy(wdn_hbm.at[l, e], wdn32.at[s], sem.at[s, 1]))

    @pl.when(i == 0)
    def _():
        for cp in weight_copies(te_ref[0], slot):
            cp.start()

    @pl.when(new_expert)
    def _():
        for cp in weight_copies(te_ref[i], slot):
            cp.wait()

        @pl.when(nxt_ref[i] >= 0)
        def _():
            for cp in weight_copies(nxt_ref[i], 1 - slot):
                cp.start()

    def compute(rows, convert):
        x = _unpack_rows(x_ref[0:rows, :])

        def gate_up(a):
            if convert:
                for c in (slice(a, a + FF_CHUNK), slice(D_FF + a, D_FF + a + FF_CHUNK)):
                    wgu16[:, c] = wgu32[slot, :, c].astype(BF16)
                wdn16[a:a + FF_CHUNK, :] = wdn32[slot, a:a + FF_CHUNK, :].astype(BF16)
            b = a + FF_CHUNK
            g = _dot(x, wgu16[:, a:b]) + bgu_ref[:, a:b]
            up = _dot(x, wgu16[:, D_FF + a:D_FF + b]) + bgu_ref[:, D_FF + a:D_FF + b]
            return g, up

        def activate(g, up):
            g = jnp.minimum(g, SWIGLU_LIMIT)
            up = jnp.clip(up, -SWIGLU_LIMIT, SWIGLU_LIMIT)
            return ((up + 1.0) * (g * jax.nn.sigmoid(SWIGLU_ALPHA * g))).astype(BF16)

        starts = list(range(0, D_FF, FF_CHUNK))
        acc = None
        pre = gate_up(starts[0])
        for n, a in enumerate(starts):
            hid = activate(*pre)
            if n + 1 < len(starts):
                pre = gate_up(starts[n + 1])
            o = _dot(hid, wdn16[a:a + FF_CHUNK, :])
            acc = o if acc is None else acc + o
        y_ref[0:rows, :] = _pack_rows(acc + bdn_ref[...])

    used = i < nu_ref[0]
    for n_valid in range(1, MOE_STEP // MOE_TM + 1):
        for convert in (False, True):
            @pl.when(used & (nv_ref[i] == n_valid) & (new_expert == convert))
            def _():
                compute(n_valid * MOE_TM, convert)


def _experts(l, tile_expert, n_used, next_expert, n_valid, slot, x_sorted, w_gu, b_gu, w_dn, b_dn):
    def row_map(i, te, nu, nxt, nv, sl):
        return (jnp.minimum(i, nu[0] - 1), 0)

    def b_map(i, te, nu, nxt, nv, sl):
        return (l, te[i], 0, 0)

    grid_spec = pltpu.PrefetchScalarGridSpec(
        num_scalar_prefetch=5,
        grid=(MOE_STEPS,),
        in_specs=[
            pl.BlockSpec((MOE_STEP, DW), row_map),
            pl.BlockSpec(memory_space=pl.ANY),
            pl.BlockSpec((None, None, 1, 2 * D_FF), b_map),
            pl.BlockSpec(memory_space=pl.ANY),
            pl.BlockSpec((None, None, 1, D), b_map),
        ],
        out_specs=pl.BlockSpec((MOE_STEP, DW), row_map),
        scratch_shapes=[pltpu.VMEM((2, D, 2 * D_FF), F32), pltpu.VMEM((2, D_FF, D), F32),
                        pltpu.VMEM((D, 2 * D_FF), BF16), pltpu.VMEM((D_FF, D), BF16),
                        pltpu.SemaphoreType.DMA((2, 2))],
    )
    return pl.pallas_call(
        functools.partial(_expert_kernel, l),
        grid_spec=grid_spec,
        out_shape=jax.ShapeDtypeStruct((MOE_ROWS, DW), jnp.int32),
        compiler_params=pltpu.CompilerParams(
            dimension_semantics=("arbitrary",), vmem_limit_bytes=VMEM_LIMIT),
        name="experts",
    )(tile_expert, n_used, next_expert, n_valid, slot, x_sorted, w_gu, b_gu.reshape(DEPTH, N_EXPERTS, 1, 2 * D_FF),
      w_dn, b_dn.reshape(DEPTH, N_EXPERTS, 1, D))


def _combine_kernel(x_ref, y_ref, route_ref, mod_ref, o_ref):
    g2 = mod_ref[...][:, 5 * D:6 * D]
    r = route_ref[...]
    acc = r[:, 0:1] * _unpack_rows(y_ref[0]).astype(F32)
    for k in range(1, TOP_K):
        acc = acc + r[:, k:k + 1] * _unpack_rows(y_ref[k]).astype(F32)
    o_ref[...] = x_ref[...] + g2 * acc


def _combine(l, first_row, n_rows, x_all, y_tok, gates, mod):
    first_tile, n_tiles = first_row // TC, n_rows // TC
    return pl.pallas_call(
        _combine_kernel,
        grid=(n_tiles,),
        in_specs=[
            pl.BlockSpec((TC, D), lambda i: (i + first_tile, 0)),
            pl.BlockSpec((TOP_K, TC, DW), lambda i: (0, i, 0)),
            pl.BlockSpec((TC, LANE), lambda i: (i + first_tile, 0)),
            pl.BlockSpec((None, None, 1, 6 * D), lambda i: (l, _mod_row(i + first_tile, TC), 0, 0)),
        ],
        out_specs=pl.BlockSpec((TC, D), lambda i: (i, 0)),
        out_shape=jax.ShapeDtypeStruct((n_rows, D), F32),
        compiler_params=pltpu.CompilerParams(
            dimension_semantics=("arbitrary",), vmem_limit_bytes=VMEM_LIMIT),
        name="combine",
    )(x_all, y_tok, gates, mod)


SC_CORES, SC_SUBCORES = 2, 16
SC_WORKERS = SC_CORES * SC_SUBCORES
SC_CHUNK = 128
SC_CHUNKS_PER_WORKER = T_ALL // SC_CHUNK // SC_WORKERS


def _sc_mesh():
    return plsc.VectorSubcoreMesh(core_axis_name="c", subcore_axis_name="s")


def _sc_scratch():
    return [pltpu.VMEM((TOP_K, SC_CHUNK), jnp.int32), pltpu.VMEM((SC_CHUNK, DW), jnp.int32),
            pltpu.SemaphoreType.DMA]


def _dispatch_rows(h2p, dest):
    @functools.partial(pl.kernel, mesh=_sc_mesh(), out_type=jax.ShapeDtypeStruct((MOE_ROWS, DW), jnp.int32),
                       scratch_types=_sc_scratch(), name="dispatch_rows")
    def run(h_hbm, d_hbm, o_hbm, idx_v, rows_v, sem):
        wid = lax.axis_index("s") * SC_CORES + lax.axis_index("c")
        for j in range(SC_CHUNKS_PER_WORKER):
            c = wid * SC_CHUNKS_PER_WORKER + j
            pltpu.sync_copy(d_hbm.at[c], idx_v)
            pltpu.sync_copy(h_hbm.at[pl.ds(c * SC_CHUNK, SC_CHUNK)], rows_v)
            copies = [pltpu.async_copy(rows_v, o_hbm.at[idx_v.at[k]], sem) for k in range(TOP_K)]
            for cp in copies:
                cp.wait()

    return run(h2p, dest)


def _gather_rows(y, dest, first_chunk, n_chunks, *after):
    per_worker = n_chunks // SC_WORKERS
    assert per_worker * SC_WORKERS == n_chunks

    @functools.partial(pl.kernel, mesh=_sc_mesh(),
                       out_type=jax.ShapeDtypeStruct((TOP_K, n_chunks * SC_CHUNK, DW), jnp.int32),
                       scratch_types=_sc_scratch(), name="gather_rows")
    def run(y_hbm, d_hbm, *rest):
        o_hbm, idx_v, rows_v, sem = rest[len(after):]
        wid = lax.axis_index("s") * SC_CORES + lax.axis_index("c")
        for j in range(per_worker):
            c = wid * per_worker + j
            pltpu.sync_copy(d_hbm.at[first_chunk + c], idx_v)
            for k in range(TOP_K):
                pltpu.async_copy(y_hbm.at[idx_v.at[k]], rows_v, sem).wait()
                pltpu.sync_copy(rows_v, o_hbm.at[k, pl.ds(c * SC_CHUNK, SC_CHUNK)])

    return run(y, dest, *after)


def _constants():
    lane64 = np.arange(4 * HD)
    seg64 = (lane64[:, None] // HD == lane64[None, :] // HD).astype(np.float32)
    lane96 = np.arange(4 * BP)
    real = lane96 % BP < B_QK
    seg96 = ((lane96[:, None] // BP == lane96[None, :] // BP) & real[:, None] & real[None, :]).astype(np.float32)
    place = np.zeros((LANE, 4 * BP), np.float32)
    for hh in range(4):
        place[np.arange(B_ROPE), hh * BP + B_NOPE + np.arange(B_ROPE)] = 1.0

    def angles(rot_dim):
        pos = np.arange(LAT_LEN)
        rows = (pos // GRID_W).astype(np.float32)
        cols = (pos % GRID_W).astype(np.float32)
        axis_dim = rot_dim // 2
        inv = np.power(np.float32(ROPE_THETA), -(np.arange(0, axis_dim, 2, dtype=np.float32) / np.float32(axis_dim)))
        ang = np.concatenate([rows[:, None] * inv, cols[:, None] * inv], axis=-1).astype(np.float32)
        return np.cos(ang), np.sin(ang)

    def head_tables(rot_dim):
        cos, sin = angles(rot_dim)
        q = rot_dim // 4
        cr, cc, sr, sc = cos[:, :q], cos[:, q:], sin[:, :q], sin[:, q:]
        return (np.concatenate([cr, cr, cc, cc], axis=-1), np.concatenate([-sr, sr, -sc, sc], axis=-1))

    c64, s64 = head_tables(HD)
    cos64 = np.tile(c64, (1, 4))
    sin64 = np.tile(s64, (1, 4))
    c32, s32 = head_tables(B_ROPE)
    ones = np.ones((LAT_LEN, B_NOPE), np.float32)
    zeros = np.zeros((LAT_LEN, B_NOPE), np.float32)
    padz = np.zeros((LAT_LEN, BP - B_QK), np.float32)
    cos96 = np.tile(np.concatenate([ones, c32, padz], axis=-1), (1, 4))
    sin96 = np.tile(np.concatenate([zeros, s32, padz], axis=-1), (1, 4))
    first64 = ((lane64 % 32) < 16).astype(np.float32)[None, :]
    first96 = (((lane96 % BP) % 16) < 8).astype(np.float32)[None, :]
    tri = (np.arange(TP)[:, None] < np.arange(TP)[None, :]).astype(np.float32)
    f32 = lambda a: jnp.asarray(a, F32)
    return dict(seg64=jnp.asarray(seg64, BF16), seg96=jnp.asarray(seg96, BF16), place=jnp.asarray(place, BF16),
                tri=jnp.asarray(tri, BF16),
                cos64=f32(cos64), sin64=f32(sin64), cos96=f32(cos96), sin96=f32(sin96),
                first64=f32(first64), first96=f32(first96))


def _pad_heads(w, per_head, width):
    lead = w.shape[:-1]
    w = w.reshape(lead + (4, per_head))
    return jnp.pad(w, ((0, 0),) * (len(lead) + 1) + ((0, width - per_head),)).reshape(lead + (4 * width,))


def _weights(norm1_g, norm2_g, w_in, a_q_g, a_k_g, b_cq_g, b_ckv_g, w_uq, w_ukv, b_q_g, b_k_g,
             c_q_g, c_k_g, w_out, router_w, router_b):
    o = np.cumsum((0, 256, 128, 128, 384, 256, 32, 256, 128, 128))
    seg = lambda k: w_in[:, :, o[k]:o[k + 1]]
    w_in_r = jnp.concatenate([seg(0), seg(1), seg(2), seg(3), seg(4), seg(6), seg(7), seg(8), seg(5),
                              jnp.zeros((DEPTH, D, LANE - B_ROPE), F32)], axis=-1).astype(BF16)
    ukv = w_ukv.reshape(DEPTH, B_KV_LORA, 4, B_NOPE + B_V)
    w_uk = _pad_heads(ukv[..., :B_NOPE].reshape(DEPTH, B_KV_LORA, 4 * B_NOPE), B_NOPE, BP)
    w_uv = ukv[..., B_NOPE:].reshape(DEPTH, B_KV_LORA, 4 * B_V)
    rw_hi = router_w.astype(BF16)
    rw_lo = (router_w - rw_hi.astype(F32)).astype(BF16)
    tile = lambda v, n: jnp.tile(v, (1, n))
    rows = dict(n1g=norm1_g, n2g=norm2_g, aqg=tile(a_q_g, 4), akg=tile(a_k_g, 2), cqg=tile(c_q_g, 4),
                ckg=tile(c_k_g, 2), bcqg=b_cq_g, bckvg=b_ckv_g,
                bqg=_pad_heads(tile(b_q_g, 4), B_QK, BP), bkg=_pad_heads(tile(b_k_g, 4), B_QK, BP))
    order = sorted(VEC_ROWS, key=lambda n: VEC_ROWS[n][0])
    assert all(rows[n].shape == (DEPTH, VEC_ROWS[n][1]) for n in order)
    vec = jnp.stack([jnp.pad(rows[n], ((0, 0), (0, D - rows[n].shape[1]))) for n in order], axis=1)
    return dict(
        vec=vec[:, :, None, :], w_in=w_in_r,
        w_uq=_pad_heads(w_uq, B_QK, BP).astype(BF16), w_uk=w_uk.astype(BF16), w_uv=w_uv.astype(BF16),
        w_out=w_out.astype(BF16),
        rwt=jnp.concatenate([jnp.swapaxes(rw_hi, 1, 2), jnp.swapaxes(rw_lo, 1, 2)], axis=1),
        rb=router_b[:, :, None])


def _moe(l, x_all, h2p, gate_slab, route, counts, mod, w_gu, b_gu, w_dn, b_dn):
    counts = counts[:, 0].astype(jnp.int32)
    padded = (counts + MOE_STEP - 1) // MOE_STEP * MOE_STEP
    pend = jnp.cumsum(padded)
    pstart = pend - padded
    n_used = (pend[-1] // MOE_STEP).astype(jnp.int32)
    steps = jnp.minimum(jnp.arange(MOE_STEPS, dtype=jnp.int32), n_used - 1)
    tile_expert = jnp.sum((pend[None, :] <= steps[:, None] * MOE_STEP).astype(jnp.int32), axis=1)
    tile_expert = jnp.minimum(tile_expert, N_EXPERTS - 1)
    of_expert = lambda table: jnp.sum(
        jnp.where(tile_expert[:, None] == jnp.arange(N_EXPERTS)[None, :], table[None, :], 0), axis=1)
    group_end_step = of_expert(pend) // MOE_STEP
    after = jnp.sum((pend[None, :] <= group_end_step[:, None] * MOE_STEP).astype(jnp.int32), axis=1)
    next_expert = jnp.where(group_end_step < n_used, jnp.minimum(after, N_EXPERTS - 1), -1).astype(jnp.int32)
    rows_left = of_expert(pstart + counts) - steps * MOE_STEP
    n_valid = jnp.clip((rows_left + MOE_TM - 1) // MOE_TM, 1, MOE_STEP // MOE_TM).astype(jnp.int32)
    e = route[TOP_K:2 * TOP_K].astype(jnp.int32)
    rank = route[2 * TOP_K:3 * TOP_K].astype(jnp.int32)
    start = jnp.sum(jnp.where(e[:, :, None] == jnp.arange(N_EXPERTS)[None, None, :], pstart[None, None, :], 0), axis=-1)
    dest = (start + rank).reshape(TOP_K, T_ALL // SC_CHUNK, SC_CHUNK).transpose(1, 0, 2)
    x_sorted = _dispatch_rows(h2p, dest)
    changed = jnp.concatenate([jnp.zeros((1,), jnp.int32), (tile_expert[1:] != tile_expert[:-1]).astype(jnp.int32)])
    slot = jnp.cumsum(changed) % 2
    y = _experts(l, tile_expert, n_used.reshape(1), next_expert, n_valid, slot.astype(jnp.int32), x_sorted,
                 w_gu, b_gu, w_dn, b_dn)
    ctx_chunks, lat_chunks = T_CTX // SC_CHUNK, T_LAT // SC_CHUNK
    y_ctx = _gather_rows(y, dest, 0, ctx_chunks)
    out_ctx = _combine(l, 0, T_CTX, x_all, y_ctx, gate_slab, mod)
    y_lat = _gather_rows(y, dest, ctx_chunks, lat_chunks, out_ctx)
    return out_ctx, _combine(l, T_CTX, T_LAT, x_all, y_lat, gate_slab, mod)


def kernel(x_prompt, x_sample, cache_a_k, cache_a_v, cache_b_ckv, cache_b_krope, cache_c_k, cache_c_v, c, c_ctx,
           norm1_g, norm2_g, w_mod, b_mod, w_in, a_q_g, a_k_g, a_sink, b_cq_g, b_ckv_g, w_uq, w_ukv, b_q_g, b_k_g,
           c_q_g, c_k_g, w_out, router_w, router_b, w_gu, b_gu, w_dn, b_dn):
    consts = _constants()
    cond = jnp.concatenate([c_ctx[None, :], c, jnp.zeros((3, D), F32)], axis=0)
    mod = _modulation(cond, w_mod, b_mod).reshape(DEPTH, 8, 1, 6 * D)
    x_ctx, x_lat, x_lat_off = x_prompt.reshape(T_CTX, D), x_sample.reshape(T_LAT, D), 0

    lw = _weights(norm1_g, norm2_g, w_in, a_q_g, a_k_g, b_cq_g, b_ckv_g, w_uq, w_ukv, b_q_g, b_k_g,
                  c_q_g, c_k_g, w_out, router_w, router_b)
    ckb, cvb = _cache_kv(cache_b_ckv, cache_b_krope, lw, consts)
    merge_heads = lambda a: a.reshape(N_LAT_SEQ, DEPTH, PAST, 2 * HD).astype(BF16)
    cache = dict(ka=merge_heads(cache_a_k), va=merge_heads(cache_a_v), kb=ckb, vb=cvb,
                 kc=merge_heads(cache_c_k), vc=merge_heads(cache_c_v))

    states = ()
    names = ["qa", "ka", "va", "qb", "kb", "vb", "qc", "kc", "vc"]
    for l in range(DEPTH):
        outs = _projection(False, l, x_ctx, 0, mod, lw, consts, states)
        p_ctx = dict(zip(names, outs[:9]))
        states = tuple(outs[9:])
        p_lat = dict(zip(names, _projection(True, l, x_lat, x_lat_off, mod, lw, consts)))
        mix_ctx = _attention_ctx(l, a_sink, p_ctx)
        mix_lat = _attention_lat(l, a_sink, p_lat, cache)
        x_mid, h2p, gate_slab, route, counts = _post_attention(l, mix_ctx, mix_lat, x_ctx, x_lat, x_lat_off, mod, lw,
                                                               consts)
        x_ctx, x_lat = _moe(l, x_mid, h2p, gate_slab, route, counts, mod, w_gu, b_gu, w_dn, b_dn)
    y_ctx, y_lat = x_ctx, x_lat

    return (y_ctx.reshape(N_CTX_SEQ, CTX_LEN, D), y_lat.reshape(N_LAT_SEQ, LAT_LEN, D)) + states
```

```python
import functools

import jax
import jax.numpy as jnp
import numpy as np
from jax import lax
from jax.experimental import pallas as pl
from jax.experimental.pallas import tpu as pltpu
from jax.experimental.pallas import tpu_sc as plsc

F32 = jnp.float32
BF16 = jnp.bfloat16

D = 1024
DEPTH = 2
N_CTX_SEQ, CTX_LEN = 16, 256
N_LAT_SEQ, LAT_LEN = 4, 2048
PAST = 512
T_CTX = N_CTX_SEQ * CTX_LEN
T_LAT = N_LAT_SEQ * LAT_LEN
T_ALL = T_CTX + T_LAT
GRID_W = 64
HD = 64
WINDOW = 128
B_NOPE, B_ROPE, B_V = 64, 32, 128
B_QK = B_NOPE + B_ROPE
B_Q_LORA, B_KV_LORA = 384, 256
N_EXPERTS, TOP_K = 32, 4
D_FF = 1024
SWIGLU_LIMIT = 7.0
SWIGLU_ALPHA = 1.702
ROPE_THETA = 10000.0
EPS = 1e-6
NEG_INF = -1e30

TM = 256
TP = 1024
TQ_LAT = 1024
TQ_CTX = 256
TC = 1024
SEQ_PER_STEP = TQ_CTX // CTX_LEN
LANE = 128
BP = 128
N_CTX_TILES = T_CTX // TM
N_LAT_TILES = T_LAT // TM
N_TILES = T_ALL // TM
LAT_TILES_PER_SEQ = LAT_LEN // TM
MOE_TM = 256
MOE_STEP = 4 * MOE_TM
FF_CHUNK = 256
MOE_STEPS = T_ALL * TOP_K // MOE_STEP + N_EXPERTS
MOE_ROWS = MOE_STEPS * MOE_STEP
DW = D // 2
VMEM_LIMIT = 56 * 1024 * 1024

C_QA, C_KA, C_VA, C_CQ, C_CKV, C_QC, C_KC, C_VC, C_KR, C_END = (
    0, 256, 384, 512, 896, 1152, 1408, 1536, 1664, 1792)


def _dot(a, b):
    return jnp.dot(a, b, preferred_element_type=F32)


def _dot_t(a, b):
    return lax.dot_general(a, b, (((1,), (1,)), ((), ())), preferred_element_type=F32)


def _rms(x, g):
    return x * lax.rsqrt(jnp.mean(x * x, axis=-1, keepdims=True) + EPS) * g


def _seg_norm(x, seg, g, n):
    ss = _dot((x * x).astype(BF16), seg)
    return x * lax.rsqrt(ss * (1.0 / n) + EPS) * g


def _rope(x, cos, sin, first, sh):
    w = x.shape[1]
    fwd = pltpu.roll(x, w - sh, 1)
    bwd = pltpu.roll(x, sh, 1)
    return x * cos + jnp.where(first > 0.5, fwd, bwd) * sin


MOD_BN = 1536


def _mod_kernel(c_ref, w_ref, b_ref, o_ref):
    c = c_ref[...]
    s = (c * jax.nn.sigmoid(c)).astype(BF16)
    o_ref[...] = _dot(s, w_ref[...].astype(BF16)) + b_ref[...]


def _modulation(cond, w_mod, b_mod):
    return pl.pallas_call(
        _mod_kernel,
        grid=(DEPTH, 6 * D // MOD_BN),
        in_specs=[
            pl.BlockSpec((8, D), lambda l, j: (0, 0)),
            pl.BlockSpec((None, D, MOD_BN), lambda l, j: (l, 0, j)),
            pl.BlockSpec((None, 1, MOD_BN), lambda l, j: (l, 0, j)),
        ],
        out_specs=pl.BlockSpec((None, 8, MOD_BN), lambda l, j: (l, 0, j)),
        out_shape=jax.ShapeDtypeStruct((DEPTH, 8, 6 * D), F32),
        compiler_params=pltpu.CompilerParams(
            dimension_semantics=("arbitrary", "arbitrary"), vmem_limit_bytes=VMEM_LIMIT),
        name="modulation",
    )(cond, w_mod, b_mod.reshape(DEPTH, 1, 6 * D))


def _proj_kernel(is_lat, n_aliased, *refs):
    (x_ref, mod_ref, n1g_ref, win_ref, seg64_ref, seg96_ref, aqg_ref, akg_ref, cqg_ref, ckg_ref,
     bcqg_ref, bckvg_ref, bqg_ref, bkg_ref, wuq_ref, wuk_ref, wuv_ref, plc_ref) = refs[:18]
    refs = refs[18 + n_aliased:]
    if is_lat:
        (cos64_ref, sin64_ref, cos96_ref, sin96_ref, f64_ref, f96_ref) = refs[:6]
        refs = refs[6:]
    (qa_ref, ka_ref, va_ref, qb_ref, kb_ref, vb_ref, qc_ref, kc_ref, vc_ref) = refs[:9]
    refs = refs[9:]
    if not is_lat:
        if n_aliased == 0:
            for r in refs:
                r[:, 1:] = jnp.zeros((SEQ_PER_STEP, DEPTH - 1) + r.shape[2:], F32)
            refs = [r.at[:, 0] for r in refs]
        (kas_ref, vas_ref, ckvs_ref, krs_ref, kcs_ref, vcs_ref) = refs

    x = x_ref[...]
    m = mod_ref[...]
    sh1, sc1 = m[:, 0:D], m[:, D:2 * D]
    h = (_rms(x, n1g_ref[...]) * (1.0 + sc1) + sh1).astype(BF16)

    def proj(a, b):
        return _dot(h, win_ref[:, a:b])

    def rope64(t):
        wd = t.shape[1]
        return _rope(t, cos64_ref[:, :wd], sin64_ref[:, :wd], f64_ref[:, :wd], 16) if is_lat else t

    def rope96(t):
        return _rope(t, cos96_ref[...], sin96_ref[...], f96_ref[...], 8) if is_lat else t

    seg64 = seg64_ref[...]
    seg64h = seg64_ref[0:2 * HD, 0:2 * HD]
    seg96 = seg96_ref[...]

    def seg_sum(t, seg):
        return _dot((t * t).astype(BF16), seg)

    def seg_finish(t, ss, g, n):
        return t * lax.rsqrt(ss * (1.0 / n) + EPS) * g

    p_cq, p_ckv, p_kr = proj(C_CQ, C_CKV), proj(C_CKV, C_QC), proj(C_KR, C_END)
    p_qa, p_ka, p_qc, p_kc = proj(C_QA, C_KA), proj(C_KA, C_VA), proj(C_QC, C_KC), proj(C_KC, C_VC)
    p_va, p_vc = proj(C_VA, C_CQ), proj(C_VC, C_KR)

    cq = _rms(p_cq, bcqg_ref[...]).astype(BF16)
    ckv = _rms(p_ckv, bckvg_ref[...])
    ckv16 = ckv.astype(BF16)
    u_q = _dot(cq, wuq_ref[...])
    u_k = _dot(ckv16, wuk_ref[...]) + _dot(p_kr.astype(BF16), plc_ref[...])
    u_v = _dot(ckv16, wuv_ref[...])
    ss_qa, ss_ka = seg_sum(p_qa, seg64), seg_sum(p_ka, seg64h)
    ss_qc, ss_kc = seg_sum(p_qc, seg64), seg_sum(p_kc, seg64h)
    ss_qb, ss_kb = seg_sum(u_q, seg96), seg_sum(u_k, seg96)

    def store_state(ref, t):
        for s in range(SEQ_PER_STEP):
            ref[s] = t[s * CTX_LEN:(s + 1) * CTX_LEN]

    def store_kv_state(ref, t):
        for s in range(SEQ_PER_STEP):
            ref[s] = t[s * CTX_LEN:(s + 1) * CTX_LEN].reshape(CTX_LEN, 2, HD)

    if not is_lat:
        store_kv_state(vas_ref, p_va)
        store_kv_state(vcs_ref, p_vc)
        store_state(ckvs_ref, ckv)
        store_state(krs_ref, p_kr[:, 0:B_ROPE])
    va_ref[...] = p_va.astype(BF16)
    vc_ref[...] = p_vc.astype(BF16)
    vb_ref[...] = u_v.astype(BF16)

    t = seg_finish(p_qa, ss_qa, aqg_ref[...], HD)
    qa_ref[...] = (rope64(t) * (HD ** -0.5 * LOG2E)).astype(BF16)
    t = seg_finish(p_ka, ss_ka, akg_ref[...], HD)
    if not is_lat:
        store_kv_state(kas_ref, t)
    ka_ref[...] = rope64(t).astype(BF16)
    t = seg_finish(p_qc, ss_qc, cqg_ref[...], HD)
    qc_ref[...] = (rope64(t) * (HD ** -0.5 * LOG2E)).astype(BF16)
    t = seg_finish(p_kc, ss_kc, ckg_ref[...], HD)
    if not is_lat:
        store_kv_state(kcs_ref, t)
    kc_ref[...] = rope64(t).astype(BF16)
    t = seg_finish(u_q, ss_qb, bqg_ref[...], B_QK)
    qb_ref[...] = (rope96(t) * (B_QK ** -0.5 * LOG2E)).astype(BF16)
    t = seg_finish(u_k, ss_kb, bkg_ref[...], B_QK)
    kb_ref[...] = rope96(t).astype(BF16)


def _full(shape):
    nd = len(shape)
    return pl.BlockSpec(shape, lambda i: (0,) * nd)


STATE_TAILS = ((2, HD), (2, HD), (B_KV_LORA,), (B_ROPE,), (2, HD), (2, HD))


def _layer_spec(a, l):
    nd = a.ndim - 1
    return pl.BlockSpec((None,) + a.shape[1:], lambda *_: (l,) + (0,) * nd)


VEC_ROWS = dict(n1g=(0, D), n2g=(1, D), aqg=(2, 4 * HD), akg=(3, 2 * HD), cqg=(4, 4 * HD), ckg=(5, 2 * HD),
                bcqg=(6, B_Q_LORA), bckvg=(7, B_KV_LORA), bqg=(8, 4 * BP), bkg=(9, 4 * BP))


def _vec_spec(l, name):
    row, width = VEC_ROWS[name]
    return pl.BlockSpec((None, None, 1, width), lambda *_: (l, row, 0, 0))


def _projection(is_lat, l, x_src, x_off, mod, lw, consts, prev_states=()):
    assert is_lat or bool(prev_states) == (l > 0)
    t_rows = T_LAT if is_lat else T_CTX
    TQ = TQ_LAT if is_lat else TQ_CTX
    n_tiles = t_rows // TQ
    if is_lat:
        mod_map = lambda i: (l, 1 + i // (LAT_LEN // TQ), 0, 0)
    else:
        mod_map = lambda i: (l, 0, 0, 0)
    vec = lw["vec"]
    gain = lambda name: (vec, _vec_spec(l, name))
    whole = lambda a: (a, _full(a.shape))
    layer = lambda a: (a, _layer_spec(a, l))
    pairs = [(x_src, pl.BlockSpec((TQ, D), lambda i: (i + x_off, 0))),
             (mod, pl.BlockSpec((None, None, 1, 6 * D), mod_map)),
             gain("n1g"), layer(lw["w_in"]), whole(consts["seg64"]), whole(consts["seg96"]),
             gain("aqg"), gain("akg"), gain("cqg"), gain("ckg"), gain("bcqg"), gain("bckvg"), gain("bqg"), gain("bkg"),
             layer(lw["w_uq"]), layer(lw["w_uk"]), layer(lw["w_uv"]), whole(consts["place"])]
    ins = [a for a, _ in pairs]
    in_specs = [s for _, s in pairs]
    n_plain = len(ins)
    ins += list(prev_states)
    in_specs += [pl.BlockSpec(memory_space=pl.ANY) for _ in prev_states]
    if is_lat:
        tabs = [consts["cos64"], consts["sin64"], consts["cos96"], consts["sin96"]]
        ins += tabs + [consts["first64"], consts["first96"]]
        in_specs += [pl.BlockSpec((TQ, a.shape[1]), lambda i: (i % (LAT_LEN // TQ), 0)) for a in tabs]
        in_specs += [_full(consts["first64"].shape), _full(consts["first96"].shape)]
    widths = [4 * HD, 2 * HD, 2 * HD, 4 * BP, 4 * BP, 4 * B_V, 4 * HD, 2 * HD, 2 * HD]
    out_shape = [jax.ShapeDtypeStruct((t_rows, w), BF16) for w in widths]
    out_specs = [pl.BlockSpec((TQ, w), lambda i: (i, 0)) for w in widths]
    if not is_lat:
        for tail in STATE_TAILS:
            zeros = (0,) * len(tail)
            out_shape.append(jax.ShapeDtypeStruct((N_CTX_SEQ, DEPTH, CTX_LEN) + tail, F32))
            if prev_states:
                out_specs.append(pl.BlockSpec((SEQ_PER_STEP, None, CTX_LEN) + tail, lambda i, z=zeros: (i, l, 0) + z))
            else:
                out_specs.append(pl.BlockSpec((SEQ_PER_STEP, DEPTH, CTX_LEN) + tail, lambda i, z=zeros: (i, 0, 0) + z))
    aliases = {n_plain + k: len(widths) + k for k in range(len(prev_states))}
    return pl.pallas_call(
        functools.partial(_proj_kernel, is_lat, len(prev_states)),
        grid=(n_tiles,),
        in_specs=in_specs,
        out_specs=out_specs,
        out_shape=out_shape,
        input_output_aliases=aliases,
        compiler_params=pltpu.CompilerParams(
            dimension_semantics=("arbitrary",), vmem_limit_bytes=VMEM_LIMIT),
        name="proj_lat" if is_lat else "proj_ctx",
    )(*ins)


def _cache_kv_kernel(ckv_ref, kr_ref, seg96_ref, bkg_ref, wuk_ref, wuv_ref, plc_ref, kb_ref, vb_ref):
    ckv16 = ckv_ref[...].astype(BF16)
    vb_ref[...] = _dot(ckv16, wuv_ref[...]).astype(BF16)
    kpre = _dot(ckv16, wuk_ref[...]) + _dot(kr_ref[...].astype(BF16), plc_ref[0:B_ROPE, :])
    kb_ref[...] = _seg_norm(kpre, seg96_ref[...], bkg_ref[...], B_QK).astype(BF16)


def _cache_kv(cache_b_ckv, cache_b_krope, lw, consts):
    def c_map(l, r):
        return (r // (PAST // TM), l, r % (PAST // TM), 0)

    def w_spec(a):
        return pl.BlockSpec((None,) + a.shape[1:], lambda l, r: (l,) + (0,) * (a.ndim - 1))

    def s_spec(a):
        return pl.BlockSpec(a.shape, lambda l, r: (0,) * a.ndim)

    rows = N_LAT_SEQ * PAST
    return pl.pallas_call(
        _cache_kv_kernel,
        grid=(DEPTH, rows // TM),
        in_specs=[pl.BlockSpec((None, None, TM, B_KV_LORA), c_map), pl.BlockSpec((None, None, TM, B_ROPE), c_map),
                  s_spec(consts["seg96"]),
                  pl.BlockSpec((None, None, 1, 4 * BP), lambda l, r: (l, VEC_ROWS["bkg"][0], 0, 0)),
                  w_spec(lw["w_uk"]), w_spec(lw["w_uv"]),
                  s_spec(consts["place"])],
        out_specs=[pl.BlockSpec((None, TM, 4 * BP), lambda l, r: (l, r, 0)),
                   pl.BlockSpec((None, TM, 4 * B_V), lambda l, r: (l, r, 0))],
        out_shape=[jax.ShapeDtypeStruct((DEPTH, rows, 4 * BP), BF16),
                   jax.ShapeDtypeStruct((DEPTH, rows, 4 * B_V), BF16)],
        compiler_params=pltpu.CompilerParams(
            dimension_semantics=("arbitrary", "arbitrary"), vmem_limit_bytes=VMEM_LIMIT),
        name="cache_kv",
    )(cache_b_ckv, cache_b_krope, consts["seg96"], lw["vec"], lw["w_uk"], lw["w_uv"], consts["place"])


def _scores(q, segs):
    scores = []
    for k, _, mask in segs:
        s = _dot_t(q, k)
        if mask is not None:
            s = jnp.where(mask, s, NEG_INF)
        scores.append(s)
    return scores


def _softmax_pv(scores, segs, sink, num_cols, den_col):
    m = None
    for s in scores:
        sm = jnp.max(s, axis=-1, keepdims=True)
        m = sm if m is None else jnp.maximum(m, sm)
    if sink is not None:
        m = jnp.maximum(m, sink)
    acc = None
    den = None
    for s, (_, v, _) in zip(scores, segs):
        e = jnp.exp2(s - m)
        if den_col is None:
            d = jnp.sum(e, axis=-1, keepdims=True)
            den = d if den is None else den + d
        o = _dot(e.astype(BF16), v)
        acc = o if acc is None else acc + o
    if den_col is not None:
        den = acc[:, den_col:den_col + 1]
        acc = acc[:, num_cols]
    if sink is not None:
        den = den + jnp.exp2(sink - m)
    return acc / den


HEAD_LOOKAHEAD = 1


def _run_heads(jobs, o_ref):
    ready = [_scores(job[0](), job[1]()) for job in jobs[:HEAD_LOOKAHEAD]]
    for n, (_, load_segs, sink, col, num_cols, den_col) in enumerate(jobs):
        scores = ready.pop(0)
        if n + HEAD_LOOKAHEAD < len(jobs):
            ahead = jobs[n + HEAD_LOOKAHEAD]
            ready.append(_scores(ahead[0](), ahead[1]()))
        o = _softmax_pv(scores, load_segs(values=True), sink, num_cols, den_col)
        o_ref[:, col:col + o.shape[1]] = o.astype(BF16)


OUT_B = 4 * HD
OUT_C = 4 * HD + 4 * B_V
LOG2E = 1.4426950408889634


def _head_job(q_ref, q_cols, seg_refs, kv_cols, v_cols, sink, col, rows=None, masks=None, mxu_sums=True):
    width = v_cols.stop - v_cols.start
    if not mxu_sums:
        load_cols, num_cols, den_col = v_cols, None, None
    elif width == HD:
        pair = v_cols.start // LANE * LANE
        load_cols = slice(pair, pair + LANE)
        lo = v_cols.start - pair
        num_cols, den_col = slice(lo, lo + HD), (lo + HD) % LANE
    else:
        load_cols, num_cols, den_col = v_cols, slice(0, width), width
    if sink is not None:
        sink = sink * LOG2E

    def with_ones(v):
        if not mxu_sums:
            return v
        if width != HD:
            return jnp.concatenate([v, jnp.ones_like(v)], axis=1)
        lane = lax.broadcasted_iota(jnp.int32, (1, LANE), 1)
        keep = jnp.where((lane >= num_cols.start) & (lane < num_cols.stop), 1.0, 0.0).astype(BF16)
        return v * keep + (1.0 - keep)

    def load_segs(values=False):
        segs = []
        for n, (k_ref, v_ref) in enumerate(seg_refs):
            r = rows if (rows is not None and n == 0) else slice(None)
            if values:
                segs.append((None, with_ones(v_ref[r, load_cols]), None))
            else:
                segs.append((k_ref[r, kv_cols], None, None if masks is None else masks[n]))
        return segs

    return (lambda: q_ref[:, q_cols]), load_segs, sink, col, num_cols, den_col


def _attn_ctx_kernel(l, sink_ref, qa_ref, ka_ref, va_ref, qb_ref, kb_ref, vb_ref, qc_ref, kc_ref, vc_ref, o_ref):
    jobs = []
    for h in range(4):
        cs = slice(h // 2 * HD, (h // 2 + 1) * HD)
        jobs.append(_head_job(qa_ref, slice(h * HD, (h + 1) * HD), [(ka_ref, va_ref)], cs, cs, sink_ref[l, h],
                              h * HD, mxu_sums=False))
    for h in range(4):
        ks, vs = slice(h * BP, (h + 1) * BP), slice(h * B_V, (h + 1) * B_V)
        jobs.append(_head_job(qb_ref, ks, [(kb_ref, vb_ref)], ks, vs, None, OUT_B + h * B_V, mxu_sums=False))
    for h in range(4):
        cs = slice(h // 2 * HD, (h // 2 + 1) * HD)
        jobs.append(_head_job(qc_ref, slice(h * HD, (h + 1) * HD), [(kc_ref, vc_ref)], cs, cs, None, OUT_C + h * HD,
                              mxu_sums=False))
    _run_heads(jobs, o_ref)


def _attention_ctx(l, sink, p):
    names = ["qa", "ka", "va", "qb", "kb", "vb", "qc", "kc", "vc"]
    ins = [p[n] for n in names]
    in_specs = [pl.BlockSpec(memory_space=pltpu.SMEM)]
    in_specs += [pl.BlockSpec((CTX_LEN, a.shape[1]), lambda i: (i, 0)) for a in ins]
    return pl.pallas_call(
        functools.partial(_attn_ctx_kernel, l),
        grid=(N_CTX_SEQ,),
        in_specs=in_specs,
        out_specs=pl.BlockSpec((CTX_LEN, D), lambda i: (i, 0)),
        out_shape=jax.ShapeDtypeStruct((T_CTX, D), BF16),
        compiler_params=pltpu.CompilerParams(
            dimension_semantics=("arbitrary",), vmem_limit_bytes=VMEM_LIMIT),
        name="attn_ctx",
    )(sink, *ins)


WIN_SPAN = TM + 2 * WINDOW


def _attn_lat_kernel(l, sink_ref, qa_ref, qb_ref, qc_ref, ka_ref, va_ref, kb_ref, vb_ref, kc_ref, vc_ref,
                     cka_ref, cva_ref, ckb_ref, cvb_ref, ckc_ref, cvc_ref, o_ref):
    qi = pl.program_id(1)
    ws = pl.multiple_of(jnp.clip(qi * TM - WINDOW, 0, LAT_LEN - WIN_SPAN), WINDOW)
    qpos = qi * TM + lax.broadcasted_iota(jnp.int32, (TM, WIN_SPAN), 0)
    kpos = ws + lax.broadcasted_iota(jnp.int32, (TM, WIN_SPAN), 1)
    band = jnp.abs(qpos - kpos) <= WINDOW
    jobs = []
    for h in range(4):
        cs = slice(h // 2 * HD, (h // 2 + 1) * HD)
        jobs.append(_head_job(qa_ref, slice(h * HD, (h + 1) * HD), [(ka_ref, va_ref), (cka_ref, cva_ref)], cs, cs,
                              sink_ref[l, h], h * HD, rows=pl.ds(ws, WIN_SPAN), masks=(band, None)))
    for h in range(4):
        ks, vs = slice(h * BP, (h + 1) * BP), slice(h * B_V, (h + 1) * B_V)
        jobs.append(_head_job(qb_ref, ks, [(kb_ref, vb_ref), (ckb_ref, cvb_ref)], ks, vs, None, OUT_B + h * B_V))
    for h in range(4):
        cs = slice(h // 2 * HD, (h // 2 + 1) * HD)
        jobs.append(_head_job(qc_ref, slice(h * HD, (h + 1) * HD), [(kc_ref, vc_ref), (ckc_ref, cvc_ref)], cs, cs,
                              None, OUT_C + h * HD))
    _run_heads(jobs, o_ref)


def _attention_lat(l, sink, p, cache):
    q_ins = [p["qa"], p["qb"], p["qc"]]
    kv_ins = [p[n] for n in ("ka", "va", "kb", "vb", "kc", "vc")]
    c_names = ("ka", "va", "kb", "vb", "kc", "vc")
    c_ins = [cache[n] for n in c_names]
    in_specs = [pl.BlockSpec(memory_space=pltpu.SMEM)]
    in_specs += [pl.BlockSpec((TM, a.shape[1]), lambda b, i: (b * LAT_TILES_PER_SEQ + i, 0)) for a in q_ins]
    in_specs += [pl.BlockSpec((LAT_LEN, a.shape[1]), lambda b, i: (b, 0)) for a in kv_ins]
    for n, a in zip(c_names, c_ins):
        if n in ("kb", "vb"):
            in_specs.append(pl.BlockSpec((None, PAST, a.shape[-1]), lambda b, i: (l, b, 0)))
        else:
            in_specs.append(pl.BlockSpec((None, None, PAST, a.shape[-1]), lambda b, i: (b, l, 0, 0)))
    return pl.pallas_call(
        functools.partial(_attn_lat_kernel, l),
        grid=(N_LAT_SEQ, LAT_TILES_PER_SEQ),
        in_specs=in_specs,
        out_specs=pl.BlockSpec((TM, D), lambda b, i: (b * LAT_TILES_PER_SEQ + i, 0)),
        out_shape=jax.ShapeDtypeStruct((T_LAT, D), BF16),
        compiler_params=pltpu.CompilerParams(
            dimension_semantics=("arbitrary", "arbitrary"), vmem_limit_bytes=VMEM_LIMIT),
        name="attn_lat",
    )(sink, *q_ins, *kv_ins, *c_ins)


def _pack_rows(x):
    half = x.shape[1] // 2
    r = x.astype(BF16).astype(F32)
    hi = lax.bitcast_convert_type(r[:, :half], jnp.int32)
    lo = lax.bitcast_convert_type(r[:, half:], jnp.int32)
    return jnp.bitwise_or(hi, lax.shift_right_logical(lo, 16))


def _unpack_rows(p):
    a = lax.bitcast_convert_type(jnp.bitwise_and(p, -65536), F32)
    b = lax.bitcast_convert_type(lax.shift_left(p, 16), F32)
    return jnp.concatenate([a, b], axis=1).astype(BF16)


ROUTE_ROWS = 16


def _post_kernel(mc_ref, ml_ref, xc_ref, xl_ref, mod_ref, n2g_ref, wout_ref, rwt_ref, rb_ref, tri_ref,
                 xo_ref, h2_ref, gate_ref, route_ref, cnt_ref, run_ref):
    i = pl.program_id(0)

    @pl.when(i == 0)
    def _():
        run_ref[...] = jnp.zeros_like(run_ref)
        cnt_ref[...] = jnp.zeros_like(cnt_ref)

    is_ctx = i < T_CTX // TP
    m = mod_ref[...]
    g1, sh2, sc2 = m[:, 2 * D:3 * D], m[:, 3 * D:4 * D], m[:, 4 * D:5 * D]

    mix = jnp.where(is_ctx, mc_ref[...], ml_ref[...])
    x = jnp.where(is_ctx, xc_ref[...], xl_ref[...]) + g1 * _dot(mix, wout_ref[...])
    xo_ref[...] = x
    h2 = _rms(x, n2g_ref[...]) * (1.0 + sc2) + sh2
    hi = h2.astype(BF16)
    h2_ref[...] = _pack_rows(h2)
    lo = (h2 - hi.astype(F32)).astype(BF16)
    a = _dot_t(rwt_ref[...], hi)
    b = _dot_t(rwt_ref[0:N_EXPERTS, :], lo)
    logits = a[0:N_EXPERTS] + a[N_EXPERTS:] + b + rb_ref[...]
    eidx = lax.broadcasted_iota(jnp.int32, logits.shape, 0).astype(F32)
    work = logits
    vals, idxs, hots = [], [], []
    for _ in range(TOP_K):
        v = jnp.max(work, axis=0, keepdims=True)
        idx = jnp.min(jnp.where(work == v, eidx, float(N_EXPERTS)), axis=0, keepdims=True)
        hot = eidx == idx
        vals.append(v)
        idxs.append(idx)
        hots.append(hot)
        work = jnp.where(hot, -jnp.inf, work)
    es = [jnp.exp(v - vals[0]) for v in vals]
    den = es[0] + es[1] + es[2] + es[3]
    gates = [e / den for e in es]
    sel = jnp.where(hots[0] | hots[1] | hots[2] | hots[3], 1.0, 0.0)
    run = run_ref[:, 0:1]
    before = _dot(sel.astype(BF16), tri_ref[...]) + run
    run_new = jnp.broadcast_to(run + jnp.sum(sel, axis=1, keepdims=True), run_ref.shape)
    run_ref[...] = run_new
    cnt_ref[...] = run_new
    ranks = [jnp.sum(jnp.where(hots[k], before, 0.0), axis=0, keepdims=True) for k in range(TOP_K)]
    pad = [jnp.zeros((ROUTE_ROWS - 3 * TOP_K, TP), F32)]
    route_ref[...] = jnp.concatenate(gates + idxs + ranks + pad, axis=0)
    gate_ref[...] = jnp.concatenate(gates + [jnp.zeros((LANE - TOP_K, TP), F32)], axis=0).T


def _mod_row(i, tm=TM):
    n_ctx = T_CTX // tm
    return jnp.where(i < n_ctx, 0, 1 + (i - n_ctx) // (LAT_LEN // tm))


def _ctx_tile(i):
    return jnp.minimum(i, T_CTX // TP - 1)


def _lat_tile(i):
    return jnp.maximum(i - T_CTX // TP, 0)


def _post_attention(l, mix_ctx, mix_lat, x_ctx, x_lat, x_lat_off, mod, lw, consts):
    ins = [mix_ctx, mix_lat, x_ctx, x_lat, mod, lw["vec"], lw["w_out"], lw["rwt"], lw["rb"], consts["tri"]]
    in_specs = [
        pl.BlockSpec((TP, D), lambda i: (_ctx_tile(i), 0)),
        pl.BlockSpec((TP, D), lambda i: (_lat_tile(i), 0)),
        pl.BlockSpec((TP, D), lambda i: (_ctx_tile(i), 0)),
        pl.BlockSpec((TP, D), lambda i: (_lat_tile(i) + x_lat_off, 0)),
        pl.BlockSpec((None, None, 1, 6 * D), lambda i: (l, _mod_row(i, TP), 0, 0)),
        _vec_spec(l, "n2g"),
    ] + [_layer_spec(a, l) for a in ins[6:9]] + [_full(consts["tri"].shape)]
    return pl.pallas_call(
        _post_kernel,
        grid=(T_ALL // TP,),
        in_specs=in_specs,
        out_specs=[pl.BlockSpec((TP, D), lambda i: (i, 0)), pl.BlockSpec((TP, DW), lambda i: (i, 0)),
                   pl.BlockSpec((TP, LANE), lambda i: (i, 0)), pl.BlockSpec((ROUTE_ROWS, TP), lambda i: (0, i)),
                   pl.BlockSpec((N_EXPERTS, LANE), lambda i: (0, 0))],
        out_shape=[jax.ShapeDtypeStruct((T_ALL, D), F32), jax.ShapeDtypeStruct((T_ALL, DW), jnp.int32),
                   jax.ShapeDtypeStruct((T_ALL, LANE), F32), jax.ShapeDtypeStruct((ROUTE_ROWS, T_ALL), F32),
                   jax.ShapeDtypeStruct((N_EXPERTS, LANE), F32)],
        scratch_shapes=[pltpu.VMEM((N_EXPERTS, LANE), F32)],
        compiler_params=pltpu.CompilerParams(
            dimension_semantics=("arbitrary",), vmem_limit_bytes=VMEM_LIMIT),
        name="post_attn",
    )(*ins)


def _expert_kernel(l, te_ref, nu_ref, nxt_ref, nv_ref, slot_ref, x_ref, wgu_hbm, bgu_ref, wdn_hbm, bdn_ref, y_ref,
                   wgu32, wdn32, wgu16, wdn16, sem):
    i = pl.program_id(0)
    prev = te_ref[jnp.maximum(i - 1, 0)]
    new_expert = jnp.logical_or(i == 0, te_ref[i] != prev)
    slot = slot_ref[i]

    def weight_copies(e, s):
        return (pltpu.make_async_copy(wgu_hbm.at[l, e], wgu32.at[s], sem.at[s, 0]),
                pltpu.make_async_copy(wdn_hbm.at[l, e], wdn32.at[s], sem.at[s, 1]))

    @pl.when(i == 0)
    def _():
        for cp in weight_copies(te_ref[0], slot):
            cp.start()

    @pl.when(new_expert)
    def _():
        for cp in weight_copies(te_ref[i], slot):
            cp.wait()

        @pl.when(nxt_ref[i] >= 0)
        def _():
            for cp in weight_copies(nxt_ref[i], 1 - slot):
                cp.start()

    def compute(rows, convert):
        x = _unpack_rows(x_ref[0:rows, :])

        def gate_up(a):
            if convert:
                for c in (slice(a, a + FF_CHUNK), slice(D_FF + a, D_FF + a + FF_CHUNK)):
                    wgu16[:, c] = wgu32[slot, :, c].astype(BF16)
                wdn16[a:a + FF_CHUNK, :] = wdn32[slot, a:a + FF_CHUNK, :].astype(BF16)
            b = a + FF_CHUNK
            g = _dot(x, wgu16[:, a:b]) + bgu_ref[:, a:b]
            up = _dot(x, wgu16[:, D_FF + a:D_FF + b]) + bgu_ref[:, D_FF + a:D_FF + b]
            return g, up

        def activate(g, up):
            g = jnp.minimum(g, SWIGLU_LIMIT)
            up = jnp.clip(up, -SWIGLU_LIMIT, SWIGLU_LIMIT)
            return ((up + 1.0) * (g * jax.nn.sigmoid(SWIGLU_ALPHA * g))).astype(BF16)

        starts = list(range(0, D_FF, FF_CHUNK))
        acc = None
        pre = gate_up(starts[0])
        for n, a in enumerate(starts):
            hid = activate(*pre)
            if n + 1 < len(starts):
                pre = gate_up(starts[n + 1])
            o = _dot(hid, wdn16[a:a + FF_CHUNK, :])
            acc = o if acc is None else acc + o
        y_ref[0:rows, :] = _pack_rows(acc + bdn_ref[...])

    used = i < nu_ref[0]
    for n_valid in range(1, MOE_STEP // MOE_TM + 1):
        for convert in (False, True):
            @pl.when(used & (nv_ref[i] == n_valid) & (new_expert == convert))
            def _():
                compute(n_valid * MOE_TM, convert)


def _experts(l, tile_expert, n_used, next_expert, n_valid, slot, x_sorted, w_gu, b_gu, w_dn, b_dn):
    def row_map(i, te, nu, nxt, nv, sl):
        return (jnp.minimum(i, nu[0] - 1), 0)

    def b_map(i, te, nu, nxt, nv, sl):
        return (l, te[i], 0, 0)

    grid_spec = pltpu.PrefetchScalarGridSpec(
        num_scalar_prefetch=5,
        grid=(MOE_STEPS,),
        in_specs=[
            pl.BlockSpec((MOE_STEP, DW), row_map),
            pl.BlockSpec(memory_space=pl.ANY),
            pl.BlockSpec((None, None, 1, 2 * D_FF), b_map),
            pl.BlockSpec(memory_space=pl.ANY),
            pl.BlockSpec((None, None, 1, D), b_map),
        ],
        out_specs=pl.BlockSpec((MOE_STEP, DW), row_map),
        scratch_shapes=[pltpu.VMEM((2, D, 2 * D_FF), F32), pltpu.VMEM((2, D_FF, D), F32),
                        pltpu.VMEM((D, 2 * D_FF), BF16), pltpu.VMEM((D_FF, D), BF16),
                        pltpu.SemaphoreType.DMA((2, 2))],
    )
    return pl.pallas_call(
        functools.partial(_expert_kernel, l),
        grid_spec=grid_spec,
        out_shape=jax.ShapeDtypeStruct((MOE_ROWS, DW), jnp.int32),
        compiler_params=pltpu.CompilerParams(
            dimension_semantics=("arbitrary",), vmem_limit_bytes=VMEM_LIMIT),
        name="experts",
    )(tile_expert, n_used, next_expert, n_valid, slot, x_sorted, w_gu, b_gu.reshape(DEPTH, N_EXPERTS, 1, 2 * D_FF),
      w_dn, b_dn.reshape(DEPTH, N_EXPERTS, 1, D))


def _combine_kernel(x_ref, y_ref, route_ref, mod_ref, o_ref):
    g2 = mod_ref[...][:, 5 * D:6 * D]
    r = route_ref[...]
    acc = r[:, 0:1] * _unpack_rows(y_ref[0]).astype(F32)
    for k in range(1, TOP_K):
        acc = acc + r[:, k:k + 1] * _unpack_rows(y_ref[k]).astype(F32)
    o_ref[...] = x_ref[...] + g2 * acc


def _combine(l, first_row, n_rows, x_all, y_tok, gates, mod):
    first_tile, n_tiles = first_row // TC, n_rows // TC
    return pl.pallas_call(
        _combine_kernel,
        grid=(n_tiles,),
        in_specs=[
            pl.BlockSpec((TC, D), lambda i: (i + first_tile, 0)),
            pl.BlockSpec((TOP_K, TC, DW), lambda i: (0, i, 0)),
            pl.BlockSpec((TC, LANE), lambda i: (i + first_tile, 0)),
            pl.BlockSpec((None, None, 1, 6 * D), lambda i: (l, _mod_row(i + first_tile, TC), 0, 0)),
        ],
        out_specs=pl.BlockSpec((TC, D), lambda i: (i, 0)),
        out_shape=jax.ShapeDtypeStruct((n_rows, D), F32),
        compiler_params=pltpu.CompilerParams(
            dimension_semantics=("arbitrary",), vmem_limit_bytes=VMEM_LIMIT),
        name="combine",
    )(x_all, y_tok, gates, mod)


SC_CORES, SC_SUBCORES = 2, 16
SC_WORKERS = SC_CORES * SC_SUBCORES
SC_CHUNK = 128
SC_CHUNKS_PER_WORKER = T_ALL // SC_CHUNK // SC_WORKERS


def _sc_mesh():
    return plsc.VectorSubcoreMesh(core_axis_name="c", subcore_axis_name="s")


def _sc_scratch():
    return [pltpu.VMEM((TOP_K, SC_CHUNK), jnp.int32), pltpu.VMEM((SC_CHUNK, DW), jnp.int32),
            pltpu.SemaphoreType.DMA]


def _dispatch_rows(h2p, dest):
    @functools.partial(pl.kernel, mesh=_sc_mesh(), out_type=jax.ShapeDtypeStruct((MOE_ROWS, DW), jnp.int32),
                       scratch_types=_sc_scratch(), name="dispatch_rows")
    def run(h_hbm, d_hbm, o_hbm, idx_v, rows_v, sem):
        wid = lax.axis_index("s") * SC_CORES + lax.axis_index("c")
        for j in range(SC_CHUNKS_PER_WORKER):
            c = wid * SC_CHUNKS_PER_WORKER + j
            pltpu.sync_copy(d_hbm.at[c], idx_v)
            pltpu.sync_copy(h_hbm.at[pl.ds(c * SC_CHUNK, SC_CHUNK)], rows_v)
            copies = [pltpu.async_copy(rows_v, o_hbm.at[idx_v.at[k]], sem) for k in range(TOP_K)]
            for cp in copies:
                cp.wait()

    return run(h2p, dest)


def _gather_rows(y, dest, first_chunk, n_chunks, *after):
    per_worker = n_chunks // SC_WORKERS
    assert per_worker * SC_WORKERS == n_chunks

    @functools.partial(pl.kernel, mesh=_sc_mesh(),
                       out_type=jax.ShapeDtypeStruct((TOP_K, n_chunks * SC_CHUNK, DW), jnp.int32),
                       scratch_types=_sc_scratch(), name="gather_rows")
    def run(y_hbm, d_hbm, *rest):
        o_hbm, idx_v, rows_v, sem = rest[len(after):]
        wid = lax.axis_index("s") * SC_CORES + lax.axis_index("c")
        for j in range(per_worker):
            c = wid * per_worker + j
            pltpu.sync_copy(d_hbm.at[first_chunk + c], idx_v)
            for k in range(TOP_K):
                pltpu.async_copy(y_hbm.at[idx_v.at[k]], rows_v, sem).wait()
                pltpu.sync_copy(rows_v, o_hbm.at[k, pl.ds(c * SC_CHUNK, SC_CHUNK)])

    return run(y, dest, *after)


def _constants():
    lane64 = np.arange(4 * HD)
    seg64 = (lane64[:, None] // HD == lane64[None, :] // HD).astype(np.float32)
    lane96 = np.arange(4 * BP)
    real = lane96 % BP < B_QK
    seg96 = ((lane96[:, None] // BP == lane96[None, :] // BP) & real[:, None] & real[None, :]).astype(np.float32)
    place = np.zeros((LANE, 4 * BP), np.float32)
    for hh in range(4):
        place[np.arange(B_ROPE), hh * BP + B_NOPE + np.arange(B_ROPE)] = 1.0

    def angles(rot_dim):
        pos = np.arange(LAT_LEN)
        rows = (pos // GRID_W).astype(np.float32)
        cols = (pos % GRID_W).astype(np.float32)
        axis_dim = rot_dim // 2
        inv = np.power(np.float32(ROPE_THETA), -(np.arange(0, axis_dim, 2, dtype=np.float32) / np.float32(axis_dim)))
        ang = np.concatenate([rows[:, None] * inv, cols[:, None] * inv], axis=-1).astype(np.float32)
        return np.cos(ang), np.sin(ang)

    def head_tables(rot_dim):
        cos, sin = angles(rot_dim)
        q = rot_dim // 4
        cr, cc, sr, sc = cos[:, :q], cos[:, q:], sin[:, :q], sin[:, q:]
        return (np.concatenate([cr, cr, cc, cc], axis=-1), np.concatenate([-sr, sr, -sc, sc], axis=-1))

    c64, s64 = head_tables(HD)
    cos64 = np.tile(c64, (1, 4))
    sin64 = np.tile(s64, (1, 4))
    c32, s32 = head_tables(B_ROPE)
    ones = np.ones((LAT_LEN, B_NOPE), np.float32)
    zeros = np.zeros((LAT_LEN, B_NOPE), np.float32)
    padz = np.zeros((LAT_LEN, BP - B_QK), np.float32)
    cos96 = np.tile(np.concatenate([ones, c32, padz], axis=-1), (1, 4))
    sin96 = np.tile(np.concatenate([zeros, s32, padz], axis=-1), (1, 4))
    first64 = ((lane64 % 32) < 16).astype(np.float32)[None, :]
    first96 = (((lane96 % BP) % 16) < 8).astype(np.float32)[None, :]
    tri = (np.arange(TP)[:, None] < np.arange(TP)[None, :]).astype(np.float32)
    f32 = lambda a: jnp.asarray(a, F32)
    return dict(seg64=jnp.asarray(seg64, BF16), seg96=jnp.asarray(seg96, BF16), place=jnp.asarray(place, BF16),
                tri=jnp.asarray(tri, BF16),
                cos64=f32(cos64), sin64=f32(sin64), cos96=f32(cos96), sin96=f32(sin96),
                first64=f32(first64), first96=f32(first96))


def _pad_heads(w, per_head, width):
    lead = w.shape[:-1]
    w = w.reshape(lead + (4, per_head))
    return jnp.pad(w, ((0, 0),) * (len(lead) + 1) + ((0, width - per_head),)).reshape(lead + (4 * width,))


def _weights(norm1_g, norm2_g, w_in, a_q_g, a_k_g, b_cq_g, b_ckv_g, w_uq, w_ukv, b_q_g, b_k_g,
             c_q_g, c_k_g, w_out, router_w, router_b):
    o = np.cumsum((0, 256, 128, 128, 384, 256, 32, 256, 128, 128))
    seg = lambda k: w_in[:, :, o[k]:o[k + 1]]
    w_in_r = jnp.concatenate([seg(0), seg(1), seg(2), seg(3), seg(4), seg(6), seg(7), seg(8), seg(5),
                              jnp.zeros((DEPTH, D, LANE - B_ROPE), F32)], axis=-1).astype(BF16)
    ukv = w_ukv.reshape(DEPTH, B_KV_LORA, 4, B_NOPE + B_V)
    w_uk = _pad_heads(ukv[..., :B_NOPE].reshape(DEPTH, B_KV_LORA, 4 * B_NOPE), B_NOPE, BP)
    w_uv = ukv[..., B_NOPE:].reshape(DEPTH, B_KV_LORA, 4 * B_V)
    rw_hi = router_w.astype(BF16)
    rw_lo = (router_w - rw_hi.astype(F32)).astype(BF16)
    tile = lambda v, n: jnp.tile(v, (1, n))
    rows = dict(n1g=norm1_g, n2g=norm2_g, aqg=tile(a_q_g, 4), akg=tile(a_k_g, 2), cqg=tile(c_q_g, 4),
                ckg=tile(c_k_g, 2), bcqg=b_cq_g, bckvg=b_ckv_g,
                bqg=_pad_heads(tile(b_q_g, 4), B_QK, BP), bkg=_pad_heads(tile(b_k_g, 4), B_QK, BP))
    order = sorted(VEC_ROWS, key=lambda n: VEC_ROWS[n][0])
    assert all(rows[n].shape == (DEPTH, VEC_ROWS[n][1]) for n in order)
    vec = jnp.stack([jnp.pad(rows[n], ((0, 0), (0, D - rows[n].shape[1]))) for n in order], axis=1)
    return dict(
        vec=vec[:, :, None, :], w_in=w_in_r,
        w_uq=_pad_heads(w_uq, B_QK, BP).astype(BF16), w_uk=w_uk.astype(BF16), w_uv=w_uv.astype(BF16),
        w_out=w_out.astype(BF16),
        rwt=jnp.concatenate([jnp.swapaxes(rw_hi, 1, 2), jnp.swapaxes(rw_lo, 1, 2)], axis=1),
        rb=router_b[:, :, None])


def _moe(l, x_all, h2p, gate_slab, route, counts, mod, w_gu, b_gu, w_dn, b_dn):
    counts = counts[:, 0].astype(jnp.int32)
    padded = (counts + MOE_STEP - 1) // MOE_STEP * MOE_STEP
    pend = jnp.cumsum(padded)
    pstart = pend - padded
    n_used = (pend[-1] // MOE_STEP).astype(jnp.int32)
    steps = jnp.minimum(jnp.arange(MOE_STEPS, dtype=jnp.int32), n_used - 1)
    tile_expert = jnp.sum((pend[None, :] <= steps[:, None] * MOE_STEP).astype(jnp.int32), axis=1)
    tile_expert = jnp.minimum(tile_expert, N_EXPERTS - 1)
    of_expert = lambda table: jnp.sum(
        jnp.where(tile_expert[:, None] == jnp.arange(N_EXPERTS)[None, :], table[None, :], 0), axis=1)
    group_end_step = of_expert(pend) // MOE_STEP
    after = jnp.sum((pend[None, :] <= group_end_step[:, None] * MOE_STEP).astype(jnp.int32), axis=1)
    next_expert = jnp.where(group_end_step < n_used, jnp.minimum(after, N_EXPERTS - 1), -1).astype(jnp.int32)
    rows_left = of_expert(pstart + counts) - steps * MOE_STEP
    n_valid = jnp.clip((rows_left + MOE_TM - 1) // MOE_TM, 1, MOE_STEP // MOE_TM).astype(jnp.int32)
    e = route[TOP_K:2 * TOP_K].astype(jnp.int32)
    rank = route[2 * TOP_K:3 * TOP_K].astype(jnp.int32)
    start = jnp.sum(jnp.where(e[:, :, None] == jnp.arange(N_EXPERTS)[None, None, :], pstart[None, None, :], 0), axis=-1)
    dest = (start + rank).reshape(TOP_K, T_ALL // SC_CHUNK, SC_CHUNK).transpose(1, 0, 2)
    x_sorted = _dispatch_rows(h2p, dest)
    changed = jnp.concatenate([jnp.zeros((1,), jnp.int32), (tile_expert[1:] != tile_expert[:-1]).astype(jnp.int32)])
    slot = jnp.cumsum(changed) % 2
    y = _experts(l, tile_expert, n_used.reshape(1), next_expert, n_valid, slot.astype(jnp.int32), x_sorted,
                 w_gu, b_gu, w_dn, b_dn)
    ctx_chunks, lat_chunks = T_CTX // SC_CHUNK, T_LAT // SC_CHUNK
    y_ctx = _gather_rows(y, dest, 0, ctx_chunks)
    out_ctx = _combine(l, 0, T_CTX, x_all, y_ctx, gate_slab, mod)
    y_lat = _gather_rows(y, dest, ctx_chunks, lat_chunks, out_ctx)
    return out_ctx, _combine(l, T_CTX, T_LAT, x_all, y_lat, gate_slab, mod)


def kernel(x_prompt, x_sample, cache_a_k, cache_a_v, cache_b_ckv, cache_b_krope, cache_c_k, cache_c_v, c, c_ctx,
           norm1_g, norm2_g, w_mod, b_mod, w_in, a_q_g, a_k_g, a_sink, b_cq_g, b_ckv_g, w_uq, w_ukv, b_q_g, b_k_g,
           c_q_g, c_k_g, w_out, router_w, router_b, w_gu, b_gu, w_dn, b_dn):
    consts = _constants()
    cond = jnp.concatenate([c_ctx[None, :], c, jnp.zeros((3, D), F32)], axis=0)
    mod = _modulation(cond, w_mod, b_mod).reshape(DEPTH, 8, 1, 6 * D)
    x_ctx, x_lat, x_lat_off = x_prompt.reshape(T_CTX, D), x_sample.reshape(T_LAT, D), 0

    lw = _weights(norm1_g, norm2_g, w_in, a_q_g, a_k_g, b_cq_g, b_ckv_g, w_uq, w_ukv, b_q_g, b_k_g,
                  c_q_g, c_k_g, w_out, router_w, router_b)
    ckb, cvb = _cache_kv(cache_b_ckv, cache_b_krope, lw, consts)
    merge_heads = lambda a: a.reshape(N_LAT_SEQ, DEPTH, PAST, 2 * HD).astype(BF16)
    cache = dict(ka=merge_heads(cache_a_k), va=merge_heads(cache_a_v), kb=ckb, vb=cvb,
                 kc=merge_heads(cache_c_k), vc=merge_heads(cache_c_v))

    states = ()
    names = ["qa", "ka", "va", "qb", "kb", "vb", "qc", "kc", "vc"]
    for l in range(DEPTH):
        outs = _projection(False, l, x_ctx, 0, mod, lw, consts, states)
        p_ctx = dict(zip(names, outs[:9]))
        states = tuple(outs[9:])
        p_lat = dict(zip(names, _projection(True, l, x_lat, x_lat_off, mod, lw, consts)))
        mix_ctx = _attention_ctx(l, a_sink, p_ctx)
        mix_lat = _attention_lat(l, a_sink, p_lat, cache)
        x_mid, h2p, gate_slab, route, counts = _post_attention(l, mix_ctx, mix_lat, x_ctx, x_lat, x_lat_off, mod, lw,
                                                               consts)
        x_ctx, x_lat = _moe(l, x_mid, h2p, gate_slab, route, counts, mod, w_gu, b_gu, w_dn, b_dn)
    y_ctx, y_lat = x_ctx, x_lat

    return (y_ctx.reshape(N_CTX_SEQ, CTX_LEN, D), y_lat.reshape(N_LAT_SEQ, LAT_LEN, D)) + states
```

```python
import functools

import jax
import jax.numpy as jnp
import numpy as np
from jax import lax
from jax.experimental import pallas as pl
from jax.experimental.pallas import tpu as pltpu
from jax.experimental.pallas import tpu_sc as plsc

F32 = jnp.float32
BF16 = jnp.bfloat16

D = 1024
DEPTH = 2
N_CTX_SEQ, CTX_LEN = 16, 256
N_LAT_SEQ, LAT_LEN = 4, 2048
PAST = 512
T_CTX = N_CTX_SEQ * CTX_LEN
T_LAT = N_LAT_SEQ * LAT_LEN
T_ALL = T_CTX + T_LAT
GRID_W = 64
HD = 64
WINDOW = 128
B_NOPE, B_ROPE, B_V = 64, 32, 128
B_QK = B_NOPE + B_ROPE
B_Q_LORA, B_KV_LORA = 384, 256
N_EXPERTS, TOP_K = 32, 4
D_FF = 1024
SWIGLU_LIMIT = 7.0
SWIGLU_ALPHA = 1.702
ROPE_THETA = 10000.0
EPS = 1e-6
NEG_INF = -1e30

TM = 256
TP = 1024
TQ_LAT = 1024
TQ_CTX = 256
TC = 512
SEQ_PER_STEP = TQ_CTX // CTX_LEN
LANE = 128
BP = 128
N_CTX_TILES = T_CTX // TM
N_LAT_TILES = T_LAT // TM
N_TILES = T_ALL // TM
LAT_TILES_PER_SEQ = LAT_LEN // TM
MOE_TM = 256
MOE_STEP = 4 * MOE_TM
FF_CHUNK = 256
MOE_STEPS = T_ALL * TOP_K // MOE_STEP + N_EXPERTS
MOE_ROWS = MOE_STEPS * MOE_STEP
DW = D // 2
VMEM_LIMIT = 56 * 1024 * 1024

C_QA, C_KA, C_VA, C_CQ, C_CKV, C_QC, C_KC, C_VC, C_KR, C_END = (
    0, 256, 384, 512, 896, 1152, 1408, 1536, 1664, 1792)


def _dot(a, b):
    return jnp.dot(a, b, preferred_element_type=F32)


def _dot_t(a, b):
    return lax.dot_general(a, b, (((1,), (1,)), ((), ())), preferred_element_type=F32)


def _rms(x, g):
    return x * lax.rsqrt(jnp.mean(x * x, axis=-1, keepdims=True) + EPS) * g


def _seg_norm(x, seg, g, n):
    ss = _dot((x * x).astype(BF16), seg)
    return x * lax.rsqrt(ss * (1.0 / n) + EPS) * g


def _rope(x, cos, sin, first, sh):
    w = x.shape[1]
    fwd = pltpu.roll(x, w - sh, 1)
    bwd = pltpu.roll(x, sh, 1)
    return x * cos + jnp.where(first > 0.5, fwd, bwd) * sin


MOD_BN = 1536


def _mod_kernel(c_ref, w_ref, b_ref, o_ref):
    c = c_ref[...]
    s = (c * jax.nn.sigmoid(c)).astype(BF16)
    o_ref[...] = _dot(s, w_ref[...].astype(BF16)) + b_ref[...]


def _modulation(cond, w_mod, b_mod):
    return pl.pallas_call(
        _mod_kernel,
        grid=(DEPTH, 6 * D // MOD_BN),
        in_specs=[
            pl.BlockSpec((8, D), lambda l, j: (0, 0)),
            pl.BlockSpec((None, D, MOD_BN), lambda l, j: (l, 0, j)),
            pl.BlockSpec((None, 1, MOD_BN), lambda l, j: (l, 0, j)),
        ],
        out_specs=pl.BlockSpec((None, 8, MOD_BN), lambda l, j: (l, 0, j)),
        out_shape=jax.ShapeDtypeStruct((DEPTH, 8, 6 * D), F32),
        compiler_params=pltpu.CompilerParams(
            dimension_semantics=("arbitrary", "arbitrary"), vmem_limit_bytes=VMEM_LIMIT),
        name="modulation",
    )(cond, w_mod, b_mod.reshape(DEPTH, 1, 6 * D))


def _proj_kernel(is_lat, l, n_aliased, *refs):
    (x_ref, mod_ref, n1g_ref, win_ref, seg64_ref, seg96_ref, aqg_ref, akg_ref, cqg_ref, ckg_ref,
     bcqg_ref, bckvg_ref, bqg_ref, bkg_ref, wuq_ref, wuk_ref, wuv_ref, plc_ref) = refs[:18]
    refs = refs[18 + n_aliased:]
    if is_lat:
        (cos64_ref, sin64_ref, cos96_ref, sin96_ref, f64_ref, f96_ref) = refs[:6]
        (qa_ref, ka_ref, va_ref, qb_ref, kb_ref, vb_ref, qc_ref, kc_ref, vc_ref) = refs[6:15]
    else:
        sink_ref, mix_ref = refs[0], refs[1]
        (qa_ref, ka_ref, va_ref, qb_ref, kb_ref, vb_ref, qc_ref, kc_ref, vc_ref) = refs[8:17]
        refs = refs[2:8]
    if not is_lat:
        if n_aliased == 0:
            for r in refs:
                r[:, 1:] = jnp.zeros((SEQ_PER_STEP, DEPTH - 1) + r.shape[2:], F32)
            refs = [r.at[:, 0] for r in refs]
        (kas_ref, vas_ref, ckvs_ref, krs_ref, kcs_ref, vcs_ref) = refs

    x = x_ref[...]
    m = mod_ref[...]
    sh1, sc1 = m[:, 0:D], m[:, D:2 * D]
    h = (_rms(x, n1g_ref[...]) * (1.0 + sc1) + sh1).astype(BF16)

    def proj(a, b):
        return _dot(h, win_ref[:, a:b])

    def rope64(t):
        wd = t.shape[1]
        return _rope(t, cos64_ref[:, :wd], sin64_ref[:, :wd], f64_ref[:, :wd], 16) if is_lat else t

    def rope96(t):
        return _rope(t, cos96_ref[...], sin96_ref[...], f96_ref[...], 8) if is_lat else t

    seg64 = seg64_ref[...]
    seg64h = seg64_ref[0:2 * HD, 0:2 * HD]
    seg96 = seg96_ref[...]

    def seg_sum(t, seg):
        return _dot((t * t).astype(BF16), seg)

    def seg_finish(t, ss, g, n):
        return t * lax.rsqrt(ss * (1.0 / n) + EPS) * g

    p_cq, p_ckv, p_kr = proj(C_CQ, C_CKV), proj(C_CKV, C_QC), proj(C_KR, C_END)
    p_qa, p_ka, p_qc, p_kc = proj(C_QA, C_KA), proj(C_KA, C_VA), proj(C_QC, C_KC), proj(C_KC, C_VC)
    p_va, p_vc = proj(C_VA, C_CQ), proj(C_VC, C_KR)

    cq = _rms(p_cq, bcqg_ref[...]).astype(BF16)
    ckv = _rms(p_ckv, bckvg_ref[...])
    ckv16 = ckv.astype(BF16)
    u_q = _dot(cq, wuq_ref[...])
    u_k = _dot(ckv16, wuk_ref[...]) + _dot(p_kr.astype(BF16), plc_ref[...])
    u_v = _dot(ckv16, wuv_ref[...])
    ss_qa, ss_ka = seg_sum(p_qa, seg64), seg_sum(p_ka, seg64h)
    ss_qc, ss_kc = seg_sum(p_qc, seg64), seg_sum(p_kc, seg64h)
    ss_qb, ss_kb = seg_sum(u_q, seg96), seg_sum(u_k, seg96)

    def store_state(ref, t):
        for s in range(SEQ_PER_STEP):
            ref[s] = t[s * CTX_LEN:(s + 1) * CTX_LEN]

    def store_kv_state(ref, t):
        for s in range(SEQ_PER_STEP):
            ref[s] = t[s * CTX_LEN:(s + 1) * CTX_LEN].reshape(CTX_LEN, 2, HD)

    if not is_lat:
        store_kv_state(vas_ref, p_va)
        store_kv_state(vcs_ref, p_vc)
        store_state(ckvs_ref, ckv)
        store_state(krs_ref, p_kr[:, 0:B_ROPE])
    va_ref[...] = p_va.astype(BF16)
    vc_ref[...] = p_vc.astype(BF16)
    vb_ref[...] = u_v.astype(BF16)

    t = seg_finish(p_qa, ss_qa, aqg_ref[...], HD)
    qa_ref[...] = (rope64(t) * (HD ** -0.5 * LOG2E)).astype(BF16)
    t = seg_finish(p_ka, ss_ka, akg_ref[...], HD)
    if not is_lat:
        store_kv_state(kas_ref, t)
    ka_ref[...] = rope64(t).astype(BF16)
    t = seg_finish(p_qc, ss_qc, cqg_ref[...], HD)
    qc_ref[...] = (rope64(t) * (HD ** -0.5 * LOG2E)).astype(BF16)
    t = seg_finish(p_kc, ss_kc, ckg_ref[...], HD)
    if not is_lat:
        store_kv_state(kcs_ref, t)
    kc_ref[...] = rope64(t).astype(BF16)
    t = seg_finish(u_q, ss_qb, bqg_ref[...], B_QK)
    qb_ref[...] = (rope96(t) * (B_QK ** -0.5 * LOG2E)).astype(BF16)
    t = seg_finish(u_k, ss_kb, bkg_ref[...], B_QK)
    kb_ref[...] = rope96(t).astype(BF16)
    if not is_lat:
        _attn_ctx_kernel(l, sink_ref, qa_ref, ka_ref, va_ref, qb_ref, kb_ref, vb_ref, qc_ref, kc_ref, vc_ref, mix_ref)


def _full(shape):
    nd = len(shape)
    return pl.BlockSpec(shape, lambda i: (0,) * nd)


STATE_TAILS = ((2, HD), (2, HD), (B_KV_LORA,), (B_ROPE,), (2, HD), (2, HD))


def _layer_spec(a, l):
    nd = a.ndim - 1
    return pl.BlockSpec((None,) + a.shape[1:], lambda *_: (l,) + (0,) * nd)


VEC_ROWS = dict(n1g=(0, D), n2g=(1, D), aqg=(2, 4 * HD), akg=(3, 2 * HD), cqg=(4, 4 * HD), ckg=(5, 2 * HD),
                bcqg=(6, B_Q_LORA), bckvg=(7, B_KV_LORA), bqg=(8, 4 * BP), bkg=(9, 4 * BP))


def _vec_spec(l, name):
    row, width = VEC_ROWS[name]
    return pl.BlockSpec((None, None, 1, width), lambda *_: (l, row, 0, 0))


def _projection(is_lat, l, x_src, x_off, mod, lw, consts, prev_states=(), a_sink=None):
    assert is_lat or bool(prev_states) == (l > 0)
    t_rows = T_LAT if is_lat else T_CTX
    TQ = TQ_LAT if is_lat else TQ_CTX
    n_tiles = t_rows // TQ
    if is_lat:
        mod_map = lambda i: (l, 1 + i // (LAT_LEN // TQ), 0, 0)
    else:
        mod_map = lambda i: (l, 0, 0, 0)
    vec = lw["vec"]
    gain = lambda name: (vec, _vec_spec(l, name))
    whole = lambda a: (a, _full(a.shape))
    layer = lambda a: (a, _layer_spec(a, l))
    pairs = [(x_src, pl.BlockSpec((TQ, D), lambda i: (i + x_off, 0))),
             (mod, pl.BlockSpec((None, None, 1, 6 * D), mod_map)),
             gain("n1g"), layer(lw["w_in"]), whole(consts["seg64"]), whole(consts["seg96"]),
             gain("aqg"), gain("akg"), gain("cqg"), gain("ckg"), gain("bcqg"), gain("bckvg"), gain("bqg"), gain("bkg"),
             layer(lw["w_uq"]), layer(lw["w_uk"]), layer(lw["w_uv"]), whole(consts["place"])]
    ins = [a for a, _ in pairs]
    in_specs = [s for _, s in pairs]
    n_plain = len(ins)
    ins += list(prev_states)
    in_specs += [pl.BlockSpec(memory_space=pl.ANY) for _ in prev_states]
    if is_lat:
        tabs = [consts["cos64"], consts["sin64"], consts["cos96"], consts["sin96"]]
        ins += tabs + [consts["first64"], consts["first96"]]
        in_specs += [pl.BlockSpec((TQ, a.shape[1]), lambda i: (i % (LAT_LEN // TQ), 0)) for a in tabs]
        in_specs += [_full(consts["first64"].shape), _full(consts["first96"].shape)]
    widths = [4 * HD, 2 * HD, 2 * HD, 4 * BP, 4 * BP, 4 * B_V, 4 * HD, 2 * HD, 2 * HD]
    scratch = []
    if is_lat:
        out_shape = [jax.ShapeDtypeStruct((t_rows, w), BF16) for w in widths]
        out_specs = [pl.BlockSpec((TQ, w), lambda i: (i, 0)) for w in widths]
    else:
        assert TQ == CTX_LEN
        ins.append(a_sink)
        in_specs.append(pl.BlockSpec(memory_space=pltpu.SMEM))
        out_shape = [jax.ShapeDtypeStruct((t_rows, D), BF16)]
        out_specs = [pl.BlockSpec((TQ, D), lambda i: (i, 0))]
        scratch = [pltpu.VMEM((TQ, w), BF16) for w in widths]
        for tail in STATE_TAILS:
            zeros = (0,) * len(tail)
            out_shape.append(jax.ShapeDtypeStruct((N_CTX_SEQ, DEPTH, CTX_LEN) + tail, F32))
            if prev_states:
                out_specs.append(pl.BlockSpec((SEQ_PER_STEP, None, CTX_LEN) + tail, lambda i, z=zeros: (i, l, 0) + z))
            else:
                out_specs.append(pl.BlockSpec((SEQ_PER_STEP, DEPTH, CTX_LEN) + tail, lambda i, z=zeros: (i, 0, 0) + z))
    aliases = {n_plain + k: 1 + k for k in range(len(prev_states))}
    return pl.pallas_call(
        functools.partial(_proj_kernel, is_lat, l, len(prev_states)),
        grid=(n_tiles,),
        in_specs=in_specs,
        out_specs=out_specs,
        out_shape=out_shape,
        scratch_shapes=scratch,
        input_output_aliases=aliases,
        compiler_params=pltpu.CompilerParams(
            dimension_semantics=("arbitrary",), vmem_limit_bytes=VMEM_LIMIT),
        name="proj_lat" if is_lat else "proj_ctx",
    )(*ins)


def _cache_kv_kernel(ckv_ref, kr_ref, seg96_ref, bkg_ref, wuk_ref, wuv_ref, plc_ref, kb_ref, vb_ref):
    ckv16 = ckv_ref[...].astype(BF16)
    vb_ref[...] = _dot(ckv16, wuv_ref[...]).astype(BF16)
    kpre = _dot(ckv16, wuk_ref[...]) + _dot(kr_ref[...].astype(BF16), plc_ref[0:B_ROPE, :])
    kb_ref[...] = _seg_norm(kpre, seg96_ref[...], bkg_ref[...], B_QK).astype(BF16)


def _cache_kv(cache_b_ckv, cache_b_krope, lw, consts):
    def c_map(l, r):
        return (r // (PAST // TM), l, r % (PAST // TM), 0)

    def w_spec(a):
        return pl.BlockSpec((None,) + a.shape[1:], lambda l, r: (l,) + (0,) * (a.ndim - 1))

    def s_spec(a):
        return pl.BlockSpec(a.shape, lambda l, r: (0,) * a.ndim)

    rows = N_LAT_SEQ * PAST
    return pl.pallas_call(
        _cache_kv_kernel,
        grid=(DEPTH, rows // TM),
        in_specs=[pl.BlockSpec((None, None, TM, B_KV_LORA), c_map), pl.BlockSpec((None, None, TM, B_ROPE), c_map),
                  s_spec(consts["seg96"]),
                  pl.BlockSpec((None, None, 1, 4 * BP), lambda l, r: (l, VEC_ROWS["bkg"][0], 0, 0)),
                  w_spec(lw["w_uk"]), w_spec(lw["w_uv"]),
                  s_spec(consts["place"])],
        out_specs=[pl.BlockSpec((None, TM, 4 * BP), lambda l, r: (l, r, 0)),
                   pl.BlockSpec((None, TM, 4 * B_V), lambda l, r: (l, r, 0))],
        out_shape=[jax.ShapeDtypeStruct((DEPTH, rows, 4 * BP), BF16),
                   jax.ShapeDtypeStruct((DEPTH, rows, 4 * B_V), BF16)],
        compiler_params=pltpu.CompilerParams(
            dimension_semantics=("arbitrary", "arbitrary"), vmem_limit_bytes=VMEM_LIMIT),
        name="cache_kv",
    )(cache_b_ckv, cache_b_krope, consts["seg96"], lw["vec"], lw["w_uk"], lw["w_uv"], consts["place"])


def _scores(q, segs):
    scores = []
    for k, _, mask in segs:
        s = _dot_t(q, k)
        if mask is not None:
            s = jnp.where(mask, s, NEG_INF)
        scores.append(s)
    return scores


def _softmax_pv(scores, segs, sink, num_cols, den_col):
    m = None
    for s in scores:
        sm = jnp.max(s, axis=-1, keepdims=True)
        m = sm if m is None else jnp.maximum(m, sm)
    if sink is not None:
        m = jnp.maximum(m, sink)
    acc = None
    den = None
    for s, (_, v, _) in zip(scores, segs):
        e = jnp.exp2(s - m)
        if den_col is None:
            d = jnp.sum(e, axis=-1, keepdims=True)
            den = d if den is None else den + d
        o = _dot(e.astype(BF16), v)
        acc = o if acc is None else acc + o
    if den_col is not None:
        den = acc[:, den_col:den_col + 1]
        acc = acc[:, num_cols]
    if sink is not None:
        den = den + jnp.exp2(sink - m)
    return acc / den


HEAD_LOOKAHEAD = 1


def _run_heads(jobs, o_ref):
    ready = [_scores(job[0](), job[1]()) for job in jobs[:HEAD_LOOKAHEAD]]
    for n, (_, load_segs, sink, col, num_cols, den_col) in enumerate(jobs):
        scores = ready.pop(0)
        if n + HEAD_LOOKAHEAD < len(jobs):
            ahead = jobs[n + HEAD_LOOKAHEAD]
            ready.append(_scores(ahead[0](), ahead[1]()))
        o = _softmax_pv(scores, load_segs(values=True), sink, num_cols, den_col)
        o_ref[:, col:col + o.shape[1]] = o.astype(BF16)


OUT_B = 4 * HD
OUT_C = 4 * HD + 4 * B_V
LOG2E = 1.4426950408889634


def _head_job(q_ref, q_cols, seg_refs, kv_cols, v_cols, sink, col, rows=None, masks=None, mxu_sums=True):
    width = v_cols.stop - v_cols.start
    if not mxu_sums:
        load_cols, num_cols, den_col = v_cols, None, None
    elif width == HD:
        pair = v_cols.start // LANE * LANE
        load_cols = slice(pair, pair + LANE)
        lo = v_cols.start - pair
        num_cols, den_col = slice(lo, lo + HD), (lo + HD) % LANE
    else:
        load_cols, num_cols, den_col = v_cols, slice(0, width), width
    if sink is not None:
        sink = sink * LOG2E

    def with_ones(v):
        if not mxu_sums:
            return v
        if width != HD:
            return jnp.concatenate([v, jnp.ones_like(v)], axis=1)
        lane = lax.broadcasted_iota(jnp.int32, (1, LANE), 1)
        keep = jnp.where((lane >= num_cols.start) & (lane < num_cols.stop), 1.0, 0.0).astype(BF16)
        return v * keep + (1.0 - keep)

    def load_segs(values=False):
        segs = []
        for n, (k_ref, v_ref) in enumerate(seg_refs):
            r = rows if (rows is not None and n == 0) else slice(None)
            if values:
                segs.append((None, with_ones(v_ref[r, load_cols]), None))
            else:
                segs.append((k_ref[r, kv_cols], None, None if masks is None else masks[n]))
        return segs

    return (lambda: q_ref[:, q_cols]), load_segs, sink, col, num_cols, den_col


def _attn_ctx_kernel(l, sink_ref, qa_ref, ka_ref, va_ref, qb_ref, kb_ref, vb_ref, qc_ref, kc_ref, vc_ref, o_ref):
    jobs = []
    for h in range(4):
        cs = slice(h // 2 * HD, (h // 2 + 1) * HD)
        jobs.append(_head_job(qa_ref, slice(h * HD, (h + 1) * HD), [(ka_ref, va_ref)], cs, cs, sink_ref[l, h],
                              h * HD, mxu_sums=False))
    for h in range(4):
        ks, vs = slice(h * BP, (h + 1) * BP), slice(h * B_V, (h + 1) * B_V)
        jobs.append(_head_job(qb_ref, ks, [(kb_ref, vb_ref)], ks, vs, None, OUT_B + h * B_V, mxu_sums=False))
    for h in range(4):
        cs = slice(h // 2 * HD, (h // 2 + 1) * HD)
        jobs.append(_head_job(qc_ref, slice(h * HD, (h + 1) * HD), [(kc_ref, vc_ref)], cs, cs, None, OUT_C + h * HD,
                              mxu_sums=False))
    _run_heads(jobs, o_ref)


def _attention_ctx(l, sink, p):
    names = ["qa", "ka", "va", "qb", "kb", "vb", "qc", "kc", "vc"]
    ins = [p[n] for n in names]
    in_specs = [pl.BlockSpec(memory_space=pltpu.SMEM)]
    in_specs += [pl.BlockSpec((CTX_LEN, a.shape[1]), lambda i: (i, 0)) for a in ins]
    return pl.pallas_call(
        functools.partial(_attn_ctx_kernel, l),
        grid=(N_CTX_SEQ,),
        in_specs=in_specs,
        out_specs=pl.BlockSpec((CTX_LEN, D), lambda i: (i, 0)),
        out_shape=jax.ShapeDtypeStruct((T_CTX, D), BF16),
        compiler_params=pltpu.CompilerParams(
            dimension_semantics=("arbitrary",), vmem_limit_bytes=VMEM_LIMIT),
        name="attn_ctx",
    )(sink, *ins)


WIN_SPAN = TM + 2 * WINDOW


def _attn_lat_kernel(l, sink_ref, qa_ref, qb_ref, qc_ref, ka_ref, va_ref, kb_ref, vb_ref, kc_ref, vc_ref,
                     cka_ref, cva_ref, ckb_ref, cvb_ref, ckc_ref, cvc_ref, o_ref):
    qi = pl.program_id(1)
    ws = pl.multiple_of(jnp.clip(qi * TM - WINDOW, 0, LAT_LEN - WIN_SPAN), WINDOW)
    qpos = qi * TM + lax.broadcasted_iota(jnp.int32, (TM, WIN_SPAN), 0)
    kpos = ws + lax.broadcasted_iota(jnp.int32, (TM, WIN_SPAN), 1)
    band = jnp.abs(qpos - kpos) <= WINDOW
    jobs = []
    for h in range(4):
        cs = slice(h // 2 * HD, (h // 2 + 1) * HD)
        jobs.append(_head_job(qa_ref, slice(h * HD, (h + 1) * HD), [(ka_ref, va_ref), (cka_ref, cva_ref)], cs, cs,
                              sink_ref[l, h], h * HD, rows=pl.ds(ws, WIN_SPAN), masks=(band, None)))
    for h in range(4):
        ks, vs = slice(h * BP, (h + 1) * BP), slice(h * B_V, (h + 1) * B_V)
        jobs.append(_head_job(qb_ref, ks, [(kb_ref, vb_ref), (ckb_ref, cvb_ref)], ks, vs, None, OUT_B + h * B_V))
    for h in range(4):
        cs = slice(h // 2 * HD, (h // 2 + 1) * HD)
        jobs.append(_head_job(qc_ref, slice(h * HD, (h + 1) * HD), [(kc_ref, vc_ref), (ckc_ref, cvc_ref)], cs, cs,
                              None, OUT_C + h * HD))
    _run_heads(jobs, o_ref)


def _attention_lat(l, sink, p, cache):
    q_ins = [p["qa"], p["qb"], p["qc"]]
    kv_ins = [p[n] for n in ("ka", "va", "kb", "vb", "kc", "vc")]
    c_names = ("ka", "va", "kb", "vb", "kc", "vc")
    c_ins = [cache[n] for n in c_names]
    in_specs = [pl.BlockSpec(memory_space=pltpu.SMEM)]
    in_specs += [pl.BlockSpec((TM, a.shape[1]), lambda b, i: (b * LAT_TILES_PER_SEQ + i, 0)) for a in q_ins]
    in_specs += [pl.BlockSpec((LAT_LEN, a.shape[1]), lambda b, i: (b, 0)) for a in kv_ins]
    for n, a in zip(c_names, c_ins):
        if n in ("kb", "vb"):
            in_specs.append(pl.BlockSpec((None, PAST, a.shape[-1]), lambda b, i: (l, b, 0)))
        else:
            in_specs.append(pl.BlockSpec((None, None, PAST, a.shape[-1]), lambda b, i: (b, l, 0, 0)))
    return pl.pallas_call(
        functools.partial(_attn_lat_kernel, l),
        grid=(N_LAT_SEQ, LAT_TILES_PER_SEQ),
        in_specs=in_specs,
        out_specs=pl.BlockSpec((TM, D), lambda b, i: (b * LAT_TILES_PER_SEQ + i, 0)),
        out_shape=jax.ShapeDtypeStruct((T_LAT, D), BF16),
        compiler_params=pltpu.CompilerParams(
            dimension_semantics=("arbitrary", "arbitrary"), vmem_limit_bytes=VMEM_LIMIT),
        name="attn_lat",
    )(sink, *q_ins, *kv_ins, *c_ins)


def _pack_rows(x):
    half = x.shape[1] // 2
    r = x.astype(BF16).astype(F32)
    hi = lax.bitcast_convert_type(r[:, :half], jnp.int32)
    lo = lax.bitcast_convert_type(r[:, half:], jnp.int32)
    return jnp.bitwise_or(hi, lax.shift_right_logical(lo, 16))


def _unpack_rows(p):
    a = lax.bitcast_convert_type(jnp.bitwise_and(p, -65536), F32)
    b = lax.bitcast_convert_type(lax.shift_left(p, 16), F32)
    return jnp.concatenate([a, b], axis=1).astype(BF16)


ROUTE_ROWS = 16


def _post_kernel(mc_ref, ml_ref, xc_ref, xl_ref, mod_ref, n2g_ref, wout_ref, rwt_ref, rb_ref, tri_ref,
                 xo_ref, h2_ref, gate_ref, route_ref, cnt_ref, run_ref):
    i = pl.program_id(0)

    @pl.when(i == 0)
    def _():
        run_ref[...] = jnp.zeros_like(run_ref)
        cnt_ref[...] = jnp.zeros_like(cnt_ref)

    is_ctx = i < T_CTX // TP
    m = mod_ref[...]
    g1, sh2, sc2 = m[:, 2 * D:3 * D], m[:, 3 * D:4 * D], m[:, 4 * D:5 * D]

    mix = jnp.where(is_ctx, mc_ref[...], ml_ref[...])
    x = jnp.where(is_ctx, xc_ref[...], xl_ref[...]) + g1 * _dot(mix, wout_ref[...])
    xo_ref[...] = x
    h2 = _rms(x, n2g_ref[...]) * (1.0 + sc2) + sh2
    hi = h2.astype(BF16)
    h2_ref[...] = _pack_rows(h2)
    lo = (h2 - hi.astype(F32)).astype(BF16)
    a = _dot_t(rwt_ref[...], hi)
    b = _dot_t(rwt_ref[0:N_EXPERTS, :], lo)
    logits = a[0:N_EXPERTS] + a[N_EXPERTS:] + b + rb_ref[...]
    eidx = lax.broadcasted_iota(jnp.int32, logits.shape, 0).astype(F32)
    work = logits
    vals, idxs, hots = [], [], []
    for _ in range(TOP_K):
        v = jnp.max(work, axis=0, keepdims=True)
        idx = jnp.min(jnp.where(work == v, eidx, float(N_EXPERTS)), axis=0, keepdims=True)
        hot = eidx == idx
        vals.append(v)
        idxs.append(idx)
        hots.append(hot)
        work = jnp.where(hot, -jnp.inf, work)
    es = [jnp.exp(v - vals[0]) for v in vals]
    den = es[0] + es[1] + es[2] + es[3]
    gates = [e / den for e in es]
    sel = jnp.where(hots[0] | hots[1] | hots[2] | hots[3], 1.0, 0.0)
    run = run_ref[:, 0:1]
    before = _dot(sel.astype(BF16), tri_ref[...]) + run
    run_new = jnp.broadcast_to(run + jnp.sum(sel, axis=1, keepdims=True), run_ref.shape)
    run_ref[...] = run_new
    cnt_ref[...] = run_new
    ranks = [jnp.sum(jnp.where(hots[k], before, 0.0), axis=0, keepdims=True) for k in range(TOP_K)]
    pad = [jnp.zeros((ROUTE_ROWS - 3 * TOP_K, TP), F32)]
    route_ref[...] = jnp.concatenate(gates + idxs + ranks + pad, axis=0)
    gate_ref[...] = jnp.concatenate(gates + [jnp.zeros((LANE - TOP_K, TP), F32)], axis=0).T


def _mod_row(i, tm=TM):
    n_ctx = T_CTX // tm
    return jnp.where(i < n_ctx, 0, 1 + (i - n_ctx) // (LAT_LEN // tm))


def _ctx_tile(i):
    return jnp.minimum(i, T_CTX // TP - 1)


def _lat_tile(i):
    return jnp.maximum(i - T_CTX // TP, 0)


def _post_attention(l, mix_ctx, mix_lat, x_ctx, x_lat, x_lat_off, mod, lw, consts):
    ins = [mix_ctx, mix_lat, x_ctx, x_lat, mod, lw["vec"], lw["w_out"], lw["rwt"], lw["rb"], consts["tri"]]
    in_specs = [
        pl.BlockSpec((TP, D), lambda i: (_ctx_tile(i), 0)),
        pl.BlockSpec((TP, D), lambda i: (_lat_tile(i), 0)),
        pl.BlockSpec((TP, D), lambda i: (_ctx_tile(i), 0)),
        pl.BlockSpec((TP, D), lambda i: (_lat_tile(i) + x_lat_off, 0)),
        pl.BlockSpec((None, None, 1, 6 * D), lambda i: (l, _mod_row(i, TP), 0, 0)),
        _vec_spec(l, "n2g"),
    ] + [_layer_spec(a, l) for a in ins[6:9]] + [_full(consts["tri"].shape)]
    return pl.pallas_call(
        _post_kernel,
        grid=(T_ALL // TP,),
        in_specs=in_specs,
        out_specs=[pl.BlockSpec((TP, D), lambda i: (i, 0)), pl.BlockSpec((TP, DW), lambda i: (i, 0)),
                   pl.BlockSpec((TP, LANE), lambda i: (i, 0)), pl.BlockSpec((ROUTE_ROWS, TP), lambda i: (0, i)),
                   pl.BlockSpec((N_EXPERTS, LANE), lambda i: (0, 0))],
        out_shape=[jax.ShapeDtypeStruct((T_ALL, D), F32), jax.ShapeDtypeStruct((T_ALL, DW), jnp.int32),
                   jax.ShapeDtypeStruct((T_ALL, LANE), F32), jax.ShapeDtypeStruct((ROUTE_ROWS, T_ALL), F32),
                   jax.ShapeDtypeStruct((N_EXPERTS, LANE), F32)],
        scratch_shapes=[pltpu.VMEM((N_EXPERTS, LANE), F32)],
        compiler_params=pltpu.CompilerParams(
            dimension_semantics=("arbitrary",), vmem_limit_bytes=VMEM_LIMIT),
        name="post_attn",
    )(*ins)


def _expert_kernel(l, te_ref, nu_ref, nxt_ref, nv_ref, slot_ref, x_ref, wgu_hbm, bgu_ref, wdn_hbm, bdn_ref, y_ref,
                   wgu32, wdn32, wgu16, wdn16, sem):
    i = pl.program_id(0)
    prev = te_ref[jnp.maximum(i - 1, 0)]
    new_expert = jnp.logical_or(i == 0, te_ref[i] != prev)
    slot = slot_ref[i]

    def weight_copies(e, s):
        return (pltpu.make_async_copy(wgu_hbm.at[l, e], wgu32.at[s], sem.at[s, 0]),
                pltpu.make_async_copy(wdn_hbm.at[l, e], wdn32.at[s], sem.at[s, 1]))

    @pl.when(i == 0)
    def _():
        for cp in weight_copies(te_ref[0], slot):
            cp.start()

    @pl.when(new_expert)
    def _():
        for cp in weight_copies(te_ref[i], slot):
            cp.wait()

        @pl.when(nxt_ref[i] >= 0)
        def _():
            for cp in weight_copies(nxt_ref[i], 1 - slot):
                cp.start()

    def compute(rows, convert):
        x = _unpack_rows(x_ref[0:rows, :])

        def gate_up(a):
            if convert:
                for c in (slice(a, a + FF_CHUNK), slice(D_FF + a, D_FF + a + FF_CHUNK)):
                    wgu16[:, c] = wgu32[slot, :, c].astype(BF16)
                wdn16[a:a + FF_CHUNK, :] = wdn32[slot, a:a + FF_CHUNK, :].astype(BF16)
            b = a + FF_CHUNK
            g = _dot(x, wgu16[:, a:b]) + bgu_ref[:, a:b]
            up = _dot(x, wgu16[:, D_FF + a:D_FF + b]) + bgu_ref[:, D_FF + a:D_FF + b]
            return g, up

        def activate(g, up):
            g = jnp.minimum(g, SWIGLU_LIMIT)
            up = jnp.clip(up, -SWIGLU_LIMIT, SWIGLU_LIMIT)
            return ((up + 1.0) * (g * jax.nn.sigmoid(SWIGLU_ALPHA * g))).astype(BF16)

        starts = list(range(0, D_FF, FF_CHUNK))
        acc = None
        pre = gate_up(starts[0])
        for n, a in enumerate(starts):
            hid = activate(*pre)
            if n + 1 < len(starts):
                pre = gate_up(starts[n + 1])
            o = _dot(hid, wdn16[a:a + FF_CHUNK, :])
            acc = o if acc is None else acc + o
        y_ref[0:rows, :] = _pack_rows(acc + bdn_ref[...])

    used = i < nu_ref[0]
    for n_valid in range(1, MOE_STEP // MOE_TM + 1):
        for convert in (False, True):
            @pl.when(used & (nv_ref[i] == n_valid) & (new_expert == convert))
            def _():
                compute(n_valid * MOE_TM, convert)


def _experts(l, tile_expert, n_used, next_expert, n_valid, slot, x_sorted, w_gu, b_gu, w_dn, b_dn):
    def row_map(i, te, nu, nxt, nv, sl):
        return (jnp.minimum(i, nu[0] - 1), 0)

    def b_map(i, te, nu, nxt, nv, sl):
        return (l, te[i], 0, 0)

    grid_spec = pltpu.PrefetchScalarGridSpec(
        num_scalar_prefetch=5,
        grid=(MOE_STEPS,),
        in_specs=[
            pl.BlockSpec((MOE_STEP, DW), row_map),
            pl.BlockSpec(memory_space=pl.ANY),
            pl.BlockSpec((None, None, 1, 2 * D_FF), b_map),
            pl.BlockSpec(memory_space=pl.ANY),
            pl.BlockSpec((None, None, 1, D), b_map),
        ],
        out_specs=pl.BlockSpec((MOE_STEP, DW), row_map),
        scratch_shapes=[pltpu.VMEM((2, D, 2 * D_FF), F32), pltpu.VMEM((2, D_FF, D), F32),
                        pltpu.VMEM((D, 2 * D_FF), BF16), pltpu.VMEM((D_FF, D), BF16),
                        pltpu.SemaphoreType.DMA((2, 2))],
    )
    return pl.pallas_call(
        functools.partial(_expert_kernel, l),
        grid_spec=grid_spec,
        out_shape=jax.ShapeDtypeStruct((MOE_ROWS, DW), jnp.int32),
        compiler_params=pltpu.CompilerParams(
            dimension_semantics=("arbitrary",), vmem_limit_bytes=VMEM_LIMIT),
        name="experts",
    )(tile_expert, n_used, next_expert, n_valid, slot, x_sorted, w_gu, b_gu.reshape(DEPTH, N_EXPERTS, 1, 2 * D_FF),
      w_dn, b_dn.reshape(DEPTH, N_EXPERTS, 1, D))


def _combine_kernel(x_ref, y_ref, route_ref, mod_ref, o_ref):
    g2 = mod_ref[...][:, 5 * D:6 * D]
    r = route_ref[...]
    acc = r[:, 0:1] * _unpack_rows(y_ref[0]).astype(F32)
    for k in range(1, TOP_K):
        acc = acc + r[:, k:k + 1] * _unpack_rows(y_ref[k]).astype(F32)
    o_ref[...] = x_ref[...] + g2 * acc


def _combine(l, first_row, n_rows, x_all, y_tok, gates, mod):
    first_tile, n_tiles = first_row // TC, n_rows // TC
    return pl.pallas_call(
        _combine_kernel,
        grid=(n_tiles,),
        in_specs=[
            pl.BlockSpec((TC, D), lambda i: (i + first_tile, 0)),
            pl.BlockSpec((TOP_K, TC, DW), lambda i: (0, i, 0)),
            pl.BlockSpec((TC, LANE), lambda i: (i + first_tile, 0)),
            pl.BlockSpec((None, None, 1, 6 * D), lambda i: (l, _mod_row(i + first_tile, TC), 0, 0)),
        ],
        out_specs=pl.BlockSpec((TC, D), lambda i: (i, 0)),
        out_shape=jax.ShapeDtypeStruct((n_rows, D), F32),
        compiler_params=pltpu.CompilerParams(
            dimension_semantics=("arbitrary",), vmem_limit_bytes=VMEM_LIMIT),
        name="combine",
    )(x_all, y_tok, gates, mod)


SC_CORES, SC_SUBCORES = 2, 16
SC_WORKERS = SC_CORES * SC_SUBCORES
SC_CHUNK = 128
SC_CHUNKS_PER_WORKER = T_ALL // SC_CHUNK // SC_WORKERS


def _sc_mesh():
    return plsc.VectorSubcoreMesh(core_axis_name="c", subcore_axis_name="s")


def _sc_scratch():
    return [pltpu.VMEM((TOP_K, SC_CHUNK), jnp.int32), pltpu.VMEM((SC_CHUNK, DW), jnp.int32),
            pltpu.SemaphoreType.DMA]


def _dispatch_rows(h2p, dest):
    @functools.partial(pl.kernel, mesh=_sc_mesh(), out_type=jax.ShapeDtypeStruct((MOE_ROWS, DW), jnp.int32),
                       scratch_types=_sc_scratch(), name="dispatch_rows")
    def run(h_hbm, d_hbm, o_hbm, idx_v, rows_v, sem):
        wid = lax.axis_index("s") * SC_CORES + lax.axis_index("c")
        for j in range(SC_CHUNKS_PER_WORKER):
            c = wid * SC_CHUNKS_PER_WORKER + j
            pltpu.sync_copy(d_hbm.at[c], idx_v)
            pltpu.sync_copy(h_hbm.at[pl.ds(c * SC_CHUNK, SC_CHUNK)], rows_v)
            copies = [pltpu.async_copy(rows_v, o_hbm.at[idx_v.at[k]], sem) for k in range(TOP_K)]
            for cp in copies:
                cp.wait()

    return run(h2p, dest)


def _gather_rows(y, dest, first_chunk, n_chunks, *after):
    per_worker = n_chunks // SC_WORKERS
    assert per_worker * SC_WORKERS == n_chunks

    @functools.partial(pl.kernel, mesh=_sc_mesh(),
                       out_type=jax.ShapeDtypeStruct((TOP_K, n_chunks * SC_CHUNK, DW), jnp.int32),
                       scratch_types=_sc_scratch(), name="gather_rows")
    def run(y_hbm, d_hbm, *rest):
        o_hbm, idx_v, rows_v, sem = rest[len(after):]
        wid = lax.axis_index("s") * SC_CORES + lax.axis_index("c")
        for j in range(per_worker):
            c = wid * per_worker + j
            pltpu.sync_copy(d_hbm.at[first_chunk + c], idx_v)
            for k in range(TOP_K):
                pltpu.async_copy(y_hbm.at[idx_v.at[k]], rows_v, sem).wait()
                pltpu.sync_copy(rows_v, o_hbm.at[k, pl.ds(c * SC_CHUNK, SC_CHUNK)])

    return run(y, dest, *after)


def _constants():
    lane64 = np.arange(4 * HD)
    seg64 = (lane64[:, None] // HD == lane64[None, :] // HD).astype(np.float32)
    lane96 = np.arange(4 * BP)
    real = lane96 % BP < B_QK
    seg96 = ((lane96[:, None] // BP == lane96[None, :] // BP) & real[:, None] & real[None, :]).astype(np.float32)
    place = np.zeros((LANE, 4 * BP), np.float32)
    for hh in range(4):
        place[np.arange(B_ROPE), hh * BP + B_NOPE + np.arange(B_ROPE)] = 1.0

    def angles(rot_dim):
        pos = np.arange(LAT_LEN)
        rows = (pos // GRID_W).astype(np.float32)
        cols = (pos % GRID_W).astype(np.float32)
        axis_dim = rot_dim // 2
        inv = np.power(np.float32(ROPE_THETA), -(np.arange(0, axis_dim, 2, dtype=np.float32) / np.float32(axis_dim)))
        ang = np.concatenate([rows[:, None] * inv, cols[:, None] * inv], axis=-1).astype(np.float32)
        return np.cos(ang), np.sin(ang)

    def head_tables(rot_dim):
        cos, sin = angles(rot_dim)
        q = rot_dim // 4
        cr, cc, sr, sc = cos[:, :q], cos[:, q:], sin[:, :q], sin[:, q:]
        return (np.concatenate([cr, cr, cc, cc], axis=-1), np.concatenate([-sr, sr, -sc, sc], axis=-1))

    c64, s64 = head_tables(HD)
    cos64 = np.tile(c64, (1, 4))
    sin64 = np.tile(s64, (1, 4))
    c32, s32 = head_tables(B_ROPE)
    ones = np.ones((LAT_LEN, B_NOPE), np.float32)
    zeros = np.zeros((LAT_LEN, B_NOPE), np.float32)
    padz = np.zeros((LAT_LEN, BP - B_QK), np.float32)
    cos96 = np.tile(np.concatenate([ones, c32, padz], axis=-1), (1, 4))
    sin96 = np.tile(np.concatenate([zeros, s32, padz], axis=-1), (1, 4))
    first64 = ((lane64 % 32) < 16).astype(np.float32)[None, :]
    first96 = (((lane96 % BP) % 16) < 8).astype(np.float32)[None, :]
    tri = (np.arange(TP)[:, None] < np.arange(TP)[None, :]).astype(np.float32)
    f32 = lambda a: jnp.asarray(a, F32)
    return dict(seg64=jnp.asarray(seg64, BF16), seg96=jnp.asarray(seg96, BF16), place=jnp.asarray(place, BF16),
                tri=jnp.asarray(tri, BF16),
                cos64=f32(cos64), sin64=f32(sin64), cos96=f32(cos96), sin96=f32(sin96),
                first64=f32(first64), first96=f32(first96))


def _pad_heads(w, per_head, width):
    lead = w.shape[:-1]
    w = w.reshape(lead + (4, per_head))
    return jnp.pad(w, ((0, 0),) * (len(lead) + 1) + ((0, width - per_head),)).reshape(lead + (4 * width,))


def _weights(norm1_g, norm2_g, w_in, a_q_g, a_k_g, b_cq_g, b_ckv_g, w_uq, w_ukv, b_q_g, b_k_g,
             c_q_g, c_k_g, w_out, router_w, router_b):
    o = np.cumsum((0, 256, 128, 128, 384, 256, 32, 256, 128, 128))
    seg = lambda k: w_in[:, :, o[k]:o[k + 1]]
    w_in_r = jnp.concatenate([seg(0), seg(1), seg(2), seg(3), seg(4), seg(6), seg(7), seg(8), seg(5),
                              jnp.zeros((DEPTH, D, LANE - B_ROPE), F32)], axis=-1).astype(BF16)
    ukv = w_ukv.reshape(DEPTH, B_KV_LORA, 4, B_NOPE + B_V)
    w_uk = _pad_heads(ukv[..., :B_NOPE].reshape(DEPTH, B_KV_LORA, 4 * B_NOPE), B_NOPE, BP)
    w_uv = ukv[..., B_NOPE:].reshape(DEPTH, B_KV_LORA, 4 * B_V)
    rw_hi = router_w.astype(BF16)
    rw_lo = (router_w - rw_hi.astype(F32)).astype(BF16)
    tile = lambda v, n: jnp.tile(v, (1, n))
    rows = dict(n1g=norm1_g, n2g=norm2_g, aqg=tile(a_q_g, 4), akg=tile(a_k_g, 2), cqg=tile(c_q_g, 4),
                ckg=tile(c_k_g, 2), bcqg=b_cq_g, bckvg=b_ckv_g,
                bqg=_pad_heads(tile(b_q_g, 4), B_QK, BP), bkg=_pad_heads(tile(b_k_g, 4), B_QK, BP))
    order = sorted(VEC_ROWS, key=lambda n: VEC_ROWS[n][0])
    assert all(rows[n].shape == (DEPTH, VEC_ROWS[n][1]) for n in order)
    vec = jnp.stack([jnp.pad(rows[n], ((0, 0), (0, D - rows[n].shape[1]))) for n in order], axis=1)
    return dict(
        vec=vec[:, :, None, :], w_in=w_in_r,
        w_uq=_pad_heads(w_uq, B_QK, BP).astype(BF16), w_uk=w_uk.astype(BF16), w_uv=w_uv.astype(BF16),
        w_out=w_out.astype(BF16),
        rwt=jnp.concatenate([jnp.swapaxes(rw_hi, 1, 2), jnp.swapaxes(rw_lo, 1, 2)], axis=1),
        rb=router_b[:, :, None])


def _moe(l, x_all, h2p, gate_slab, route, counts, mod, w_gu, b_gu, w_dn, b_dn):
    counts = counts[:, 0].astype(jnp.int32)
    padded = (counts + MOE_STEP - 1) // MOE_STEP * MOE_STEP
    pend = jnp.cumsum(padded)
    pstart = pend - padded
    n_used = (pend[-1] // MOE_STEP).astype(jnp.int32)
    steps = jnp.minimum(jnp.arange(MOE_STEPS, dtype=jnp.int32), n_used - 1)
    tile_expert = jnp.sum((pend[None, :] <= steps[:, None] * MOE_STEP).astype(jnp.int32), axis=1)
    tile_expert = jnp.minimum(tile_expert, N_EXPERTS - 1)
    of_expert = lambda table: jnp.sum(
        jnp.where(tile_expert[:, None] == jnp.arange(N_EXPERTS)[None, :], table[None, :], 0), axis=1)
    group_end_step = of_expert(pend) // MOE_STEP
    after = jnp.sum((pend[None, :] <= group_end_step[:, None] * MOE_STEP).astype(jnp.int32), axis=1)
    next_expert = jnp.where(group_end_step < n_used, jnp.minimum(after, N_EXPERTS - 1), -1).astype(jnp.int32)
    rows_left = of_expert(pstart + counts) - steps * MOE_STEP
    n_valid = jnp.clip((rows_left + MOE_TM - 1) // MOE_TM, 1, MOE_STEP // MOE_TM).astype(jnp.int32)
    e = route[TOP_K:2 * TOP_K].astype(jnp.int32)
    rank = route[2 * TOP_K:3 * TOP_K].astype(jnp.int32)
    start = jnp.sum(jnp.where(e[:, :, None] == jnp.arange(N_EXPERTS)[None, None, :], pstart[None, None, :], 0), axis=-1)
    dest = (start + rank).reshape(TOP_K, T_ALL // SC_CHUNK, SC_CHUNK).transpose(1, 0, 2)
    x_sorted = _dispatch_rows(h2p, dest)
    changed = jnp.concatenate([jnp.zeros((1,), jnp.int32), (tile_expert[1:] != tile_expert[:-1]).astype(jnp.int32)])
    slot = jnp.cumsum(changed) % 2
    y = _experts(l, tile_expert, n_used.reshape(1), next_expert, n_valid, slot.astype(jnp.int32), x_sorted,
                 w_gu, b_gu, w_dn, b_dn)
    ctx_chunks, lat_chunks = T_CTX // SC_CHUNK, T_LAT // SC_CHUNK
    y_ctx = _gather_rows(y, dest, 0, ctx_chunks)
    out_ctx = _combine(l, 0, T_CTX, x_all, y_ctx, gate_slab, mod)
    y_lat = _gather_rows(y, dest, ctx_chunks, lat_chunks, out_ctx)
    return out_ctx, _combine(l, T_CTX, T_LAT, x_all, y_lat, gate_slab, mod)


def kernel(x_prompt, x_sample, cache_a_k, cache_a_v, cache_b_ckv, cache_b_krope, cache_c_k, cache_c_v, c, c_ctx,
           norm1_g, norm2_g, w_mod, b_mod, w_in, a_q_g, a_k_g, a_sink, b_cq_g, b_ckv_g, w_uq, w_ukv, b_q_g, b_k_g,
           c_q_g, c_k_g, w_out, router_w, router_b, w_gu, b_gu, w_dn, b_dn):
    consts = _constants()
    cond = jnp.concatenate([c_ctx[None, :], c, jnp.zeros((3, D), F32)], axis=0)
    mod = _modulation(cond, w_mod, b_mod).reshape(DEPTH, 8, 1, 6 * D)
    x_ctx, x_lat, x_lat_off = x_prompt.reshape(T_CTX, D), x_sample.reshape(T_LAT, D), 0

    lw = _weights(norm1_g, norm2_g, w_in, a_q_g, a_k_g, b_cq_g, b_ckv_g, w_uq, w_ukv, b_q_g, b_k_g,
                  c_q_g, c_k_g, w_out, router_w, router_b)
    ckb, cvb = _cache_kv(cache_b_ckv, cache_b_krope, lw, consts)
    merge_heads = lambda a: a.reshape(N_LAT_SEQ, DEPTH, PAST, 2 * HD).astype(BF16)
    cache = dict(ka=merge_heads(cache_a_k), va=merge_heads(cache_a_v), kb=ckb, vb=cvb,
                 kc=merge_heads(cache_c_k), vc=merge_heads(cache_c_v))

    states = ()
    names = ["qa", "ka", "va", "qb", "kb", "vb", "qc", "kc", "vc"]
    for l in range(DEPTH):
        outs = _projection(False, l, x_ctx, 0, mod, lw, consts, states, a_sink)
        mix_ctx, states = outs[0], tuple(outs[1:])
        p_lat = dict(zip(names, _projection(True, l, x_lat, x_lat_off, mod, lw, consts)))
        mix_lat = _attention_lat(l, a_sink, p_lat, cache)
        x_mid, h2p, gate_slab, route, counts = _post_attention(l, mix_ctx, mix_lat, x_ctx, x_lat, x_lat_off, mod, lw,
                                                               consts)
        x_ctx, x_lat = _moe(l, x_mid, h2p, gate_slab, route, counts, mod, w_gu, b_gu, w_dn, b_dn)
    y_ctx, y_lat = x_ctx, x_lat

    return (y_ctx.reshape(N_CTX_SEQ, CTX_LEN, D), y_lat.reshape(N_LAT_SEQ, LAT_LEN, D)) + states
```

```python
import functools

import jax
import jax.numpy as jnp
import numpy as np
from jax import lax
from jax.experimental import pallas as pl
from jax.experimental.pallas import tpu as pltpu
from jax.experimental.pallas import tpu_sc as plsc

F32 = jnp.float32
BF16 = jnp.bfloat16

D = 1024
DEPTH = 2
N_CTX_SEQ, CTX_LEN = 16, 256
N_LAT_SEQ, LAT_LEN = 4, 2048
PAST = 512
T_CTX = N_CTX_SEQ * CTX_LEN
T_LAT = N_LAT_SEQ * LAT_LEN
T_ALL = T_CTX + T_LAT
GRID_W = 64
HD = 64
WINDOW = 128
B_NOPE, B_ROPE, B_V = 64, 32, 128
B_QK = B_NOPE + B_ROPE
B_Q_LORA, B_KV_LORA = 384, 256
N_EXPERTS, TOP_K = 32, 4
D_FF = 1024
SWIGLU_LIMIT = 7.0
SWIGLU_ALPHA = 1.702
ROPE_THETA = 10000.0
EPS = 1e-6
NEG_INF = -1e30

TM = 256
TP = 1024
TQ_LAT = 1024
TQ_CTX = 256
TC = 512
SEQ_PER_STEP = TQ_CTX // CTX_LEN
LANE = 128
BP = 128
N_CTX_TILES = T_CTX // TM
N_LAT_TILES = T_LAT // TM
N_TILES = T_ALL // TM
LAT_TILES_PER_SEQ = LAT_LEN // TM
MOE_TM = 256
MOE_STEP = 4 * MOE_TM
FF_CHUNK = 256
MOE_STEPS = T_ALL * TOP_K // MOE_STEP + N_EXPERTS
MOE_ROWS = MOE_STEPS * MOE_STEP
DW = D // 2
VMEM_LIMIT = 56 * 1024 * 1024

C_QA, C_KA, C_VA, C_CQ, C_CKV, C_QC, C_KC, C_VC, C_KR, C_END = (
    0, 256, 384, 512, 896, 1152, 1408, 1536, 1664, 1792)


def _dot(a, b):
    return jnp.dot(a, b, preferred_element_type=F32)


def _dot_t(a, b):
    return lax.dot_general(a, b, (((1,), (1,)), ((), ())), preferred_element_type=F32)


def _rms(x, g):
    return x * lax.rsqrt(jnp.mean(x * x, axis=-1, keepdims=True) + EPS) * g


def _seg_norm(x, seg, g, n):
    ss = _dot((x * x).astype(BF16), seg)
    return x * lax.rsqrt(ss * (1.0 / n) + EPS) * g


def _rope(x, cos, sin, first, sh):
    w = x.shape[1]
    fwd = pltpu.roll(x, w - sh, 1)
    bwd = pltpu.roll(x, sh, 1)
    return x * cos + jnp.where(first > 0.5, fwd, bwd) * sin


MOD_BN = 1536


def _mod_kernel(c_ref, w_ref, b_ref, o_ref):
    c = c_ref[...]
    s = (c * jax.nn.sigmoid(c)).astype(BF16)
    o_ref[...] = _dot(s, w_ref[...].astype(BF16)) + b_ref[...]


def _modulation(cond, w_mod, b_mod):
    return pl.pallas_call(
        _mod_kernel,
        grid=(DEPTH, 6 * D // MOD_BN),
        in_specs=[
            pl.BlockSpec((8, D), lambda l, j: (0, 0)),
            pl.BlockSpec((None, D, MOD_BN), lambda l, j: (l, 0, j)),
            pl.BlockSpec((None, 1, MOD_BN), lambda l, j: (l, 0, j)),
        ],
        out_specs=pl.BlockSpec((None, 8, MOD_BN), lambda l, j: (l, 0, j)),
        out_shape=jax.ShapeDtypeStruct((DEPTH, 8, 6 * D), F32),
        compiler_params=pltpu.CompilerParams(
            dimension_semantics=("arbitrary", "arbitrary"), vmem_limit_bytes=VMEM_LIMIT),
        name="modulation",
    )(cond, w_mod, b_mod.reshape(DEPTH, 1, 6 * D))


def _proj_kernel(is_lat, l, n_aliased, fused, *refs):
    (x_ref, mod_ref, n1g_ref, win_ref, seg64_ref, seg96_ref, aqg_ref, akg_ref, cqg_ref, ckg_ref,
     bcqg_ref, bckvg_ref, bqg_ref, bkg_ref, wuq_ref, wuk_ref, wuv_ref, plc_ref) = refs[:18]
    refs = refs[18 + n_aliased:]
    if is_lat:
        (cos64_ref, sin64_ref, cos96_ref, sin96_ref, f64_ref, f96_ref) = refs[:6]
        (qa_ref, ka_ref, va_ref, qb_ref, kb_ref, vb_ref, qc_ref, kc_ref, vc_ref) = refs[6:15]
    else:
        sink_ref, refs = refs[0], refs[1:]
        if fused:
            (y_ref, gate_ref, modp_ref, mix_ref, xnew_ref), refs = refs[:5], refs[5:]
        else:
            mix_ref, refs = refs[0], refs[1:]
        (qa_ref, ka_ref, va_ref, qb_ref, kb_ref, vb_ref, qc_ref, kc_ref, vc_ref) = refs[6:15]
        refs = refs[0:6]
    if not is_lat:
        if n_aliased == 0:
            for r in refs:
                r[:, 1:] = jnp.zeros((SEQ_PER_STEP, DEPTH - 1) + r.shape[2:], F32)
            refs = [r.at[:, 0] for r in refs]
        (kas_ref, vas_ref, ckvs_ref, krs_ref, kcs_ref, vcs_ref) = refs

    x = x_ref[...]
    if fused:
        gate = gate_ref[...]
        acc = gate[:, 0:1] * _unpack_rows(y_ref[0]).astype(F32)
        for k in range(1, TOP_K):
            acc = acc + gate[:, k:k + 1] * _unpack_rows(y_ref[k]).astype(F32)
        x = x + modp_ref[...][:, 5 * D:6 * D] * acc
        xnew_ref[...] = x
    m = mod_ref[...]
    sh1, sc1 = m[:, 0:D], m[:, D:2 * D]
    h = (_rms(x, n1g_ref[...]) * (1.0 + sc1) + sh1).astype(BF16)

    def proj(a, b):
        return _dot(h, win_ref[:, a:b])

    def rope64(t):
        wd = t.shape[1]
        return _rope(t, cos64_ref[:, :wd], sin64_ref[:, :wd], f64_ref[:, :wd], 16) if is_lat else t

    def rope96(t):
        return _rope(t, cos96_ref[...], sin96_ref[...], f96_ref[...], 8) if is_lat else t

    seg64 = seg64_ref[...]
    seg64h = seg64_ref[0:2 * HD, 0:2 * HD]
    seg96 = seg96_ref[...]

    def seg_sum(t, seg):
        return _dot((t * t).astype(BF16), seg)

    def seg_finish(t, ss, g, n):
        return t * lax.rsqrt(ss * (1.0 / n) + EPS) * g

    p_cq, p_ckv, p_kr = proj(C_CQ, C_CKV), proj(C_CKV, C_QC), proj(C_KR, C_END)
    p_qa, p_ka, p_qc, p_kc = proj(C_QA, C_KA), proj(C_KA, C_VA), proj(C_QC, C_KC), proj(C_KC, C_VC)
    p_va, p_vc = proj(C_VA, C_CQ), proj(C_VC, C_KR)

    cq = _rms(p_cq, bcqg_ref[...]).astype(BF16)
    ckv = _rms(p_ckv, bckvg_ref[...])
    ckv16 = ckv.astype(BF16)
    u_q = _dot(cq, wuq_ref[...])
    u_k = _dot(ckv16, wuk_ref[...]) + _dot(p_kr.astype(BF16), plc_ref[...])
    u_v = _dot(ckv16, wuv_ref[...])
    ss_qa, ss_ka = seg_sum(p_qa, seg64), seg_sum(p_ka, seg64h)
    ss_qc, ss_kc = seg_sum(p_qc, seg64), seg_sum(p_kc, seg64h)
    ss_qb, ss_kb = seg_sum(u_q, seg96), seg_sum(u_k, seg96)

    def store_state(ref, t):
        for s in range(SEQ_PER_STEP):
            ref[s] = t[s * CTX_LEN:(s + 1) * CTX_LEN]

    def store_kv_state(ref, t):
        for s in range(SEQ_PER_STEP):
            ref[s] = t[s * CTX_LEN:(s + 1) * CTX_LEN].reshape(CTX_LEN, 2, HD)

    if not is_lat:
        store_kv_state(vas_ref, p_va)
        store_kv_state(vcs_ref, p_vc)
        store_state(ckvs_ref, ckv)
        store_state(krs_ref, p_kr[:, 0:B_ROPE])
    va_ref[...] = p_va.astype(BF16)
    vc_ref[...] = p_vc.astype(BF16)
    vb_ref[...] = u_v.astype(BF16)

    t = seg_finish(p_qa, ss_qa, aqg_ref[...], HD)
    qa_ref[...] = (rope64(t) * (HD ** -0.5 * LOG2E)).astype(BF16)
    t = seg_finish(p_ka, ss_ka, akg_ref[...], HD)
    if not is_lat:
        store_kv_state(kas_ref, t)
    ka_ref[...] = rope64(t).astype(BF16)
    t = seg_finish(p_qc, ss_qc, cqg_ref[...], HD)
    qc_ref[...] = (rope64(t) * (HD ** -0.5 * LOG2E)).astype(BF16)
    t = seg_finish(p_kc, ss_kc, ckg_ref[...], HD)
    if not is_lat:
        store_kv_state(kcs_ref, t)
    kc_ref[...] = rope64(t).astype(BF16)
    t = seg_finish(u_q, ss_qb, bqg_ref[...], B_QK)
    qb_ref[...] = (rope96(t) * (B_QK ** -0.5 * LOG2E)).astype(BF16)
    t = seg_finish(u_k, ss_kb, bkg_ref[...], B_QK)
    kb_ref[...] = rope96(t).astype(BF16)
    if not is_lat:
        _attn_ctx_kernel(l, sink_ref, qa_ref, ka_ref, va_ref, qb_ref, kb_ref, vb_ref, qc_ref, kc_ref, vc_ref, mix_ref)


def _full(shape):
    nd = len(shape)
    return pl.BlockSpec(shape, lambda i: (0,) * nd)


STATE_TAILS = ((2, HD), (2, HD), (B_KV_LORA,), (B_ROPE,), (2, HD), (2, HD))


def _layer_spec(a, l):
    nd = a.ndim - 1
    return pl.BlockSpec((None,) + a.shape[1:], lambda *_: (l,) + (0,) * nd)


VEC_ROWS = dict(n1g=(0, D), n2g=(1, D), aqg=(2, 4 * HD), akg=(3, 2 * HD), cqg=(4, 4 * HD), ckg=(5, 2 * HD),
                bcqg=(6, B_Q_LORA), bckvg=(7, B_KV_LORA), bqg=(8, 4 * BP), bkg=(9, 4 * BP))


def _vec_spec(l, name):
    row, width = VEC_ROWS[name]
    return pl.BlockSpec((None, None, 1, width), lambda *_: (l, row, 0, 0))


def _projection(is_lat, l, x_src, x_off, mod, lw, consts, prev_states=(), a_sink=None, pending=None):
    assert is_lat or bool(prev_states) == (l > 0)
    t_rows = T_LAT if is_lat else T_CTX
    TQ = TQ_LAT if is_lat else TQ_CTX
    n_tiles = t_rows // TQ
    if is_lat:
        mod_map = lambda i: (l, 1 + i // (LAT_LEN // TQ), 0, 0)
    else:
        mod_map = lambda i: (l, 0, 0, 0)
    vec = lw["vec"]
    gain = lambda name: (vec, _vec_spec(l, name))
    whole = lambda a: (a, _full(a.shape))
    layer = lambda a: (a, _layer_spec(a, l))
    pairs = [(x_src, pl.BlockSpec((TQ, D), lambda i: (i + x_off, 0))),
             (mod, pl.BlockSpec((None, None, 1, 6 * D), mod_map)),
             gain("n1g"), layer(lw["w_in"]), whole(consts["seg64"]), whole(consts["seg96"]),
             gain("aqg"), gain("akg"), gain("cqg"), gain("ckg"), gain("bcqg"), gain("bckvg"), gain("bqg"), gain("bkg"),
             layer(lw["w_uq"]), layer(lw["w_uk"]), layer(lw["w_uv"]), whole(consts["place"])]
    ins = [a for a, _ in pairs]
    in_specs = [s for _, s in pairs]
    n_plain = len(ins)
    ins += list(prev_states)
    in_specs += [pl.BlockSpec(memory_space=pl.ANY) for _ in prev_states]
    if is_lat:
        tabs = [consts["cos64"], consts["sin64"], consts["cos96"], consts["sin96"]]
        ins += tabs + [consts["first64"], consts["first96"]]
        in_specs += [pl.BlockSpec((TQ, a.shape[1]), lambda i: (i % (LAT_LEN // TQ), 0)) for a in tabs]
        in_specs += [_full(consts["first64"].shape), _full(consts["first96"].shape)]
    widths = [4 * HD, 2 * HD, 2 * HD, 4 * BP, 4 * BP, 4 * B_V, 4 * HD, 2 * HD, 2 * HD]
    scratch = []
    if is_lat:
        out_shape = [jax.ShapeDtypeStruct((t_rows, w), BF16) for w in widths]
        out_specs = [pl.BlockSpec((TQ, w), lambda i: (i, 0)) for w in widths]
    else:
        assert TQ == CTX_LEN
        ins.append(a_sink)
        in_specs.append(pl.BlockSpec(memory_space=pltpu.SMEM))
        out_shape = [jax.ShapeDtypeStruct((t_rows, D), BF16)]
        out_specs = [pl.BlockSpec((TQ, D), lambda i: (i, 0))]
        if pending is not None:
            y_tok, gates = pending
            ins += [y_tok, gates, mod]
            in_specs += [pl.BlockSpec((TOP_K, TQ, DW), lambda i: (0, i, 0)),
                         pl.BlockSpec((TQ, LANE), lambda i: (i, 0)),
                         pl.BlockSpec((None, None, 1, 6 * D), lambda i: (l - 1, 0, 0, 0))]
            out_shape.append(jax.ShapeDtypeStruct((t_rows, D), F32))
            out_specs.append(pl.BlockSpec((TQ, D), lambda i: (i, 0)))
        scratch = [pltpu.VMEM((TQ, w), BF16) for w in widths]
        for tail in STATE_TAILS:
            zeros = (0,) * len(tail)
            out_shape.append(jax.ShapeDtypeStruct((N_CTX_SEQ, DEPTH, CTX_LEN) + tail, F32))
            if prev_states:
                out_specs.append(pl.BlockSpec((SEQ_PER_STEP, None, CTX_LEN) + tail, lambda i, z=zeros: (i, l, 0) + z))
            else:
                out_specs.append(pl.BlockSpec((SEQ_PER_STEP, DEPTH, CTX_LEN) + tail, lambda i, z=zeros: (i, 0, 0) + z))
    n_lead = 1 if pending is None else 2
    aliases = {n_plain + k: n_lead + k for k in range(len(prev_states))}
    return pl.pallas_call(
        functools.partial(_proj_kernel, is_lat, l, len(prev_states), pending is not None),
        grid=(n_tiles,),
        in_specs=in_specs,
        out_specs=out_specs,
        out_shape=out_shape,
        scratch_shapes=scratch,
        input_output_aliases=aliases,
        compiler_params=pltpu.CompilerParams(
            dimension_semantics=("arbitrary",), vmem_limit_bytes=VMEM_LIMIT),
        name="proj_lat" if is_lat else "proj_ctx",
    )(*ins)


def _cache_kv_kernel(ckv_ref, kr_ref, seg96_ref, bkg_ref, wuk_ref, wuv_ref, plc_ref, kb_ref, vb_ref):
    ckv16 = ckv_ref[...].astype(BF16)
    vb_ref[...] = _dot(ckv16, wuv_ref[...]).astype(BF16)
    kpre = _dot(ckv16, wuk_ref[...]) + _dot(kr_ref[...].astype(BF16), plc_ref[0:B_ROPE, :])
    kb_ref[...] = _seg_norm(kpre, seg96_ref[...], bkg_ref[...], B_QK).astype(BF16)


def _cache_kv(cache_b_ckv, cache_b_krope, lw, consts):
    def c_map(l, r):
        return (r // (PAST // TM), l, r % (PAST // TM), 0)

    def w_spec(a):
        return pl.BlockSpec((None,) + a.shape[1:], lambda l, r: (l,) + (0,) * (a.ndim - 1))

    def s_spec(a):
        return pl.BlockSpec(a.shape, lambda l, r: (0,) * a.ndim)

    rows = N_LAT_SEQ * PAST
    return pl.pallas_call(
        _cache_kv_kernel,
        grid=(DEPTH, rows // TM),
        in_specs=[pl.BlockSpec((None, None, TM, B_KV_LORA), c_map), pl.BlockSpec((None, None, TM, B_ROPE), c_map),
                  s_spec(consts["seg96"]),
                  pl.BlockSpec((None, None, 1, 4 * BP), lambda l, r: (l, VEC_ROWS["bkg"][0], 0, 0)),
                  w_spec(lw["w_uk"]), w_spec(lw["w_uv"]),
                  s_spec(consts["place"])],
        out_specs=[pl.BlockSpec((None, TM, 4 * BP), lambda l, r: (l, r, 0)),
                   pl.BlockSpec((None, TM, 4 * B_V), lambda l, r: (l, r, 0))],
        out_shape=[jax.ShapeDtypeStruct((DEPTH, rows, 4 * BP), BF16),
                   jax.ShapeDtypeStruct((DEPTH, rows, 4 * B_V), BF16)],
        compiler_params=pltpu.CompilerParams(
            dimension_semantics=("arbitrary", "arbitrary"), vmem_limit_bytes=VMEM_LIMIT),
        name="cache_kv",
    )(cache_b_ckv, cache_b_krope, consts["seg96"], lw["vec"], lw["w_uk"], lw["w_uv"], consts["place"])


def _scores(q, segs):
    scores = []
    for k, _, mask in segs:
        s = _dot_t(q, k)
        if mask is not None:
            s = jnp.where(mask, s, NEG_INF)
        scores.append(s)
    return scores


def _softmax_pv(scores, segs, sink, num_cols, den_col):
    m = None
    for s in scores:
        sm = jnp.max(s, axis=-1, keepdims=True)
        m = sm if m is None else jnp.maximum(m, sm)
    if sink is not None:
        m = jnp.maximum(m, sink)
    acc = None
    den = None
    for s, (_, v, _) in zip(scores, segs):
        e = jnp.exp2(s - m)
        if den_col is None:
            d = jnp.sum(e, axis=-1, keepdims=True)
            den = d if den is None else den + d
        o = _dot(e.astype(BF16), v)
        acc = o if acc is None else acc + o
    if den_col is not None:
        den = acc[:, den_col:den_col + 1]
        acc = acc[:, num_cols]
    if sink is not None:
        den = den + jnp.exp2(sink - m)
    return acc / den


HEAD_LOOKAHEAD = 1


def _run_heads(jobs, o_ref):
    ready = [_scores(job[0](), job[1]()) for job in jobs[:HEAD_LOOKAHEAD]]
    for n, (_, load_segs, sink, col, num_cols, den_col) in enumerate(jobs):
        scores = ready.pop(0)
        if n + HEAD_LOOKAHEAD < len(jobs):
            ahead = jobs[n + HEAD_LOOKAHEAD]
            ready.append(_scores(ahead[0](), ahead[1]()))
        o = _softmax_pv(scores, load_segs(values=True), sink, num_cols, den_col)
        o_ref[:, col:col + o.shape[1]] = o.astype(BF16)


OUT_B = 4 * HD
OUT_C = 4 * HD + 4 * B_V
LOG2E = 1.4426950408889634


def _head_job(q_ref, q_cols, seg_refs, kv_cols, v_cols, sink, col, rows=None, masks=None, mxu_sums=True):
    width = v_cols.stop - v_cols.start
    if not mxu_sums:
        load_cols, num_cols, den_col = v_cols, None, None
    elif width == HD:
        pair = v_cols.start // LANE * LANE
        load_cols = slice(pair, pair + LANE)
        lo = v_cols.start - pair
        num_cols, den_col = slice(lo, lo + HD), (lo + HD) % LANE
    else:
        load_cols, num_cols, den_col = v_cols, slice(0, width), width
    if sink is not None:
        sink = sink * LOG2E

    def with_ones(v):
        if not mxu_sums:
            return v
        if width != HD:
            return jnp.concatenate([v, jnp.ones_like(v)], axis=1)
        lane = lax.broadcasted_iota(jnp.int32, (1, LANE), 1)
        keep = jnp.where((lane >= num_cols.start) & (lane < num_cols.stop), 1.0, 0.0).astype(BF16)
        return v * keep + (1.0 - keep)

    def load_segs(values=False):
        segs = []
        for n, (k_ref, v_ref) in enumerate(seg_refs):
            r = rows if (rows is not None and n == 0) else slice(None)
            if values:
                segs.append((None, with_ones(v_ref[r, load_cols]), None))
            else:
                segs.append((k_ref[r, kv_cols], None, None if masks is None else masks[n]))
        return segs

    return (lambda: q_ref[:, q_cols]), load_segs, sink, col, num_cols, den_col


def _attn_ctx_kernel(l, sink_ref, qa_ref, ka_ref, va_ref, qb_ref, kb_ref, vb_ref, qc_ref, kc_ref, vc_ref, o_ref):
    jobs = []
    for h in range(4):
        cs = slice(h // 2 * HD, (h // 2 + 1) * HD)
        jobs.append(_head_job(qa_ref, slice(h * HD, (h + 1) * HD), [(ka_ref, va_ref)], cs, cs, sink_ref[l, h],
                              h * HD, mxu_sums=False))
    for h in range(4):
        ks, vs = slice(h * BP, (h + 1) * BP), slice(h * B_V, (h + 1) * B_V)
        jobs.append(_head_job(qb_ref, ks, [(kb_ref, vb_ref)], ks, vs, None, OUT_B + h * B_V, mxu_sums=False))
    for h in range(4):
        cs = slice(h // 2 * HD, (h // 2 + 1) * HD)
        jobs.append(_head_job(qc_ref, slice(h * HD, (h + 1) * HD), [(kc_ref, vc_ref)], cs, cs, None, OUT_C + h * HD,
                              mxu_sums=False))
    _run_heads(jobs, o_ref)


def _attention_ctx(l, sink, p):
    names = ["qa", "ka", "va", "qb", "kb", "vb", "qc", "kc", "vc"]
    ins = [p[n] for n in names]
    in_specs = [pl.BlockSpec(memory_space=pltpu.SMEM)]
    in_specs += [pl.BlockSpec((CTX_LEN, a.shape[1]), lambda i: (i, 0)) for a in ins]
    return pl.pallas_call(
        functools.partial(_attn_ctx_kernel, l),
        grid=(N_CTX_SEQ,),
        in_specs=in_specs,
        out_specs=pl.BlockSpec((CTX_LEN, D), lambda i: (i, 0)),
        out_shape=jax.ShapeDtypeStruct((T_CTX, D), BF16),
        compiler_params=pltpu.CompilerParams(
            dimension_semantics=("arbitrary",), vmem_limit_bytes=VMEM_LIMIT),
        name="attn_ctx",
    )(sink, *ins)


WIN_SPAN = TM + 2 * WINDOW


def _attn_lat_kernel(l, sink_ref, qa_ref, qb_ref, qc_ref, ka_ref, va_ref, kb_ref, vb_ref, kc_ref, vc_ref,
                     cka_ref, cva_ref, ckb_ref, cvb_ref, ckc_ref, cvc_ref, o_ref):
    qi = pl.program_id(1)
    ws = pl.multiple_of(jnp.clip(qi * TM - WINDOW, 0, LAT_LEN - WIN_SPAN), WINDOW)
    qpos = qi * TM + lax.broadcasted_iota(jnp.int32, (TM, WIN_SPAN), 0)
    kpos = ws + lax.broadcasted_iota(jnp.int32, (TM, WIN_SPAN), 1)
    band = jnp.abs(qpos - kpos) <= WINDOW
    jobs = []
    for h in range(4):
        cs = slice(h // 2 * HD, (h // 2 + 1) * HD)
        jobs.append(_head_job(qa_ref, slice(h * HD, (h + 1) * HD), [(ka_ref, va_ref), (cka_ref, cva_ref)], cs, cs,
                              sink_ref[l, h], h * HD, rows=pl.ds(ws, WIN_SPAN), masks=(band, None)))
    for h in range(4):
        ks, vs = slice(h * BP, (h + 1) * BP), slice(h * B_V, (h + 1) * B_V)
        jobs.append(_head_job(qb_ref, ks, [(kb_ref, vb_ref), (ckb_ref, cvb_ref)], ks, vs, None, OUT_B + h * B_V))
    for h in range(4):
        cs = slice(h // 2 * HD, (h // 2 + 1) * HD)
        jobs.append(_head_job(qc_ref, slice(h * HD, (h + 1) * HD), [(kc_ref, vc_ref), (ckc_ref, cvc_ref)], cs, cs,
                              None, OUT_C + h * HD))
    _run_heads(jobs, o_ref)


def _attention_lat(l, sink, p, cache):
    q_ins = [p["qa"], p["qb"], p["qc"]]
    kv_ins = [p[n] for n in ("ka", "va", "kb", "vb", "kc", "vc")]
    c_names = ("ka", "va", "kb", "vb", "kc", "vc")
    c_ins = [cache[n] for n in c_names]
    in_specs = [pl.BlockSpec(memory_space=pltpu.SMEM)]
    in_specs += [pl.BlockSpec((TM, a.shape[1]), lambda b, i: (b * LAT_TILES_PER_SEQ + i, 0)) for a in q_ins]
    in_specs += [pl.BlockSpec((LAT_LEN, a.shape[1]), lambda b, i: (b, 0)) for a in kv_ins]
    for n, a in zip(c_names, c_ins):
        if n in ("kb", "vb"):
            in_specs.append(pl.BlockSpec((None, PAST, a.shape[-1]), lambda b, i: (l, b, 0)))
        else:
            in_specs.append(pl.BlockSpec((None, None, PAST, a.shape[-1]), lambda b, i: (b, l, 0, 0)))
    return pl.pallas_call(
        functools.partial(_attn_lat_kernel, l),
        grid=(N_LAT_SEQ, LAT_TILES_PER_SEQ),
        in_specs=in_specs,
        out_specs=pl.BlockSpec((TM, D), lambda b, i: (b * LAT_TILES_PER_SEQ + i, 0)),
        out_shape=jax.ShapeDtypeStruct((T_LAT, D), BF16),
        compiler_params=pltpu.CompilerParams(
            dimension_semantics=("arbitrary", "arbitrary"), vmem_limit_bytes=VMEM_LIMIT),
        name="attn_lat",
    )(sink, *q_ins, *kv_ins, *c_ins)


def _pack_rows(x):
    half = x.shape[1] // 2
    r = x.astype(BF16).astype(F32)
    hi = lax.bitcast_convert_type(r[:, :half], jnp.int32)
    lo = lax.bitcast_convert_type(r[:, half:], jnp.int32)
    return jnp.bitwise_or(hi, lax.shift_right_logical(lo, 16))


def _unpack_rows(p):
    a = lax.bitcast_convert_type(jnp.bitwise_and(p, -65536), F32)
    b = lax.bitcast_convert_type(lax.shift_left(p, 16), F32)
    return jnp.concatenate([a, b], axis=1).astype(BF16)


ROUTE_ROWS = 16


def _post_kernel(mc_ref, ml_ref, xc_ref, xl_ref, mod_ref, n2g_ref, wout_ref, rwt_ref, rb_ref, tri_ref,
                 xo_ref, h2_ref, gate_ref, route_ref, cnt_ref, run_ref):
    i = pl.program_id(0)

    @pl.when(i == 0)
    def _():
        run_ref[...] = jnp.zeros_like(run_ref)
        cnt_ref[...] = jnp.zeros_like(cnt_ref)

    is_ctx = i < T_CTX // TP
    m = mod_ref[...]
    g1, sh2, sc2 = m[:, 2 * D:3 * D], m[:, 3 * D:4 * D], m[:, 4 * D:5 * D]

    mix = jnp.where(is_ctx, mc_ref[...], ml_ref[...])
    x = jnp.where(is_ctx, xc_ref[...], xl_ref[...]) + g1 * _dot(mix, wout_ref[...])
    xo_ref[...] = x
    h2 = _rms(x, n2g_ref[...]) * (1.0 + sc2) + sh2
    hi = h2.astype(BF16)
    h2_ref[...] = _pack_rows(h2)
    lo = (h2 - hi.astype(F32)).astype(BF16)
    a = _dot_t(rwt_ref[...], hi)
    b = _dot_t(rwt_ref[0:N_EXPERTS, :], lo)
    logits = a[0:N_EXPERTS] + a[N_EXPERTS:] + b + rb_ref[...]
    eidx = lax.broadcasted_iota(jnp.int32, logits.shape, 0).astype(F32)
    work = logits
    vals, idxs, hots = [], [], []
    for _ in range(TOP_K):
        v = jnp.max(work, axis=0, keepdims=True)
        idx = jnp.min(jnp.where(work == v, eidx, float(N_EXPERTS)), axis=0, keepdims=True)
        hot = eidx == idx
        vals.append(v)
        idxs.append(idx)
        hots.append(hot)
        work = jnp.where(hot, -jnp.inf, work)
    es = [jnp.exp(v - vals[0]) for v in vals]
    den = es[0] + es[1] + es[2] + es[3]
    gates = [e / den for e in es]
    sel = jnp.where(hots[0] | hots[1] | hots[2] | hots[3], 1.0, 0.0)
    run = run_ref[:, 0:1]
    before = _dot(sel.astype(BF16), tri_ref[...]) + run
    run_new = jnp.broadcast_to(run + jnp.sum(sel, axis=1, keepdims=True), run_ref.shape)
    run_ref[...] = run_new
    cnt_ref[...] = run_new
    ranks = [jnp.sum(jnp.where(hots[k], before, 0.0), axis=0, keepdims=True) for k in range(TOP_K)]
    pad = [jnp.zeros((ROUTE_ROWS - 3 * TOP_K, TP), F32)]
    route_ref[...] = jnp.concatenate(gates + idxs + ranks + pad, axis=0)
    gate_ref[...] = jnp.concatenate(gates + [jnp.zeros((LANE - TOP_K, TP), F32)], axis=0).T


def _mod_row(i, tm=TM):
    n_ctx = T_CTX // tm
    return jnp.where(i < n_ctx, 0, 1 + (i - n_ctx) // (LAT_LEN // tm))


def _ctx_tile(i):
    return jnp.minimum(i, T_CTX // TP - 1)


def _lat_tile(i):
    return jnp.maximum(i - T_CTX // TP, 0)


def _post_attention(l, mix_ctx, mix_lat, x_ctx, x_lat, x_lat_off, mod, lw, consts):
    ins = [mix_ctx, mix_lat, x_ctx, x_lat, mod, lw["vec"], lw["w_out"], lw["rwt"], lw["rb"], consts["tri"]]
    in_specs = [
        pl.BlockSpec((TP, D), lambda i: (_ctx_tile(i), 0)),
        pl.BlockSpec((TP, D), lambda i: (_lat_tile(i), 0)),
        pl.BlockSpec((TP, D), lambda i: (_ctx_tile(i), 0)),
        pl.BlockSpec((TP, D), lambda i: (_lat_tile(i) + x_lat_off, 0)),
        pl.BlockSpec((None, None, 1, 6 * D), lambda i: (l, _mod_row(i, TP), 0, 0)),
        _vec_spec(l, "n2g"),
    ] + [_layer_spec(a, l) for a in ins[6:9]] + [_full(consts["tri"].shape)]
    return pl.pallas_call(
        _post_kernel,
        grid=(T_ALL // TP,),
        in_specs=in_specs,
        out_specs=[pl.BlockSpec((TP, D), lambda i: (i, 0)), pl.BlockSpec((TP, DW), lambda i: (i, 0)),
                   pl.BlockSpec((TP, LANE), lambda i: (i, 0)), pl.BlockSpec((ROUTE_ROWS, TP), lambda i: (0, i)),
                   pl.BlockSpec((N_EXPERTS, LANE), lambda i: (0, 0))],
        out_shape=[jax.ShapeDtypeStruct((T_ALL, D), F32), jax.ShapeDtypeStruct((T_ALL, DW), jnp.int32),
                   jax.ShapeDtypeStruct((T_ALL, LANE), F32), jax.ShapeDtypeStruct((ROUTE_ROWS, T_ALL), F32),
                   jax.ShapeDtypeStruct((N_EXPERTS, LANE), F32)],
        scratch_shapes=[pltpu.VMEM((N_EXPERTS, LANE), F32)],
        compiler_params=pltpu.CompilerParams(
            dimension_semantics=("arbitrary",), vmem_limit_bytes=VMEM_LIMIT),
        name="post_attn",
    )(*ins)


def _expert_kernel(l, te_ref, nu_ref, nxt_ref, nv_ref, slot_ref, x_ref, wgu_hbm, bgu_ref, wdn_hbm, bdn_ref, y_ref,
                   wgu32, wdn32, wgu16, wdn16, sem):
    i = pl.program_id(0)
    prev = te_ref[jnp.maximum(i - 1, 0)]
    new_expert = jnp.logical_or(i == 0, te_ref[i] != prev)
    slot = slot_ref[i]

    def weight_copies(e, s):
        return (pltpu.make_async_copy(wgu_hbm.at[l, e], wgu32.at[s], sem.at[s, 0]),
                pltpu.make_async_copy(wdn_hbm.at[l, e], wdn32.at[s], sem.at[s, 1]))

    @pl.when(i == 0)
    def _():
        for cp in weight_copies(te_ref[0], slot):
            cp.start()

    @pl.when(new_expert)
    def _():
        for cp in weight_copies(te_ref[i], slot):
            cp.wait()

        @pl.when(nxt_ref[i] >= 0)
        def _():
            for cp in weight_copies(nxt_ref[i], 1 - slot):
                cp.start()

    def compute(rows, convert):
        x = _unpack_rows(x_ref[0:rows, :])

        def gate_up(a):
            if convert:
                for c in (slice(a, a + FF_CHUNK), slice(D_FF + a, D_FF + a + FF_CHUNK)):
                    wgu16[:, c] = wgu32[slot, :, c].astype(BF16)
                wdn16[a:a + FF_CHUNK, :] = wdn32[slot, a:a + FF_CHUNK, :].astype(BF16)
            b = a + FF_CHUNK
            g = _dot(x, wgu16[:, a:b]) + bgu_ref[:, a:b]
            up = _dot(x, wgu16[:, D_FF + a:D_FF + b]) + bgu_ref[:, D_FF + a:D_FF + b]
            return g, up

        def activate(g, up):
            g = jnp.minimum(g, SWIGLU_LIMIT)
            up = jnp.clip(up, -SWIGLU_LIMIT, SWIGLU_LIMIT)
            return ((up + 1.0) * (g * jax.nn.sigmoid(SWIGLU_ALPHA * g))).astype(BF16)

        starts = list(range(0, D_FF, FF_CHUNK))
        acc = None
        pre = gate_up(starts[0])
        for n, a in enumerate(starts):
            hid = activate(*pre)
            if n + 1 < len(starts):
                pre = gate_up(starts[n + 1])
            o = _dot(hid, wdn16[a:a + FF_CHUNK, :])
            acc = o if acc is None else acc + o
        y_ref[0:rows, :] = _pack_rows(acc + bdn_ref[...])

    used = i < nu_ref[0]
    for n_valid in range(1, MOE_STEP // MOE_TM + 1):
        for convert in (False, True):
            @pl.when(used & (nv_ref[i] == n_valid) & (new_expert == convert))
            def _():
                compute(n_valid * MOE_TM, convert)


def _experts(l, tile_expert, n_used, next_expert, n_valid, slot, x_sorted, w_gu, b_gu, w_dn, b_dn):
    def row_map(i, te, nu, nxt, nv, sl):
        return (jnp.minimum(i, nu[0] - 1), 0)

    def b_map(i, te, nu, nxt, nv, sl):
        return (l, te[i], 0, 0)

    grid_spec = pltpu.PrefetchScalarGridSpec(
        num_scalar_prefetch=5,
        grid=(MOE_STEPS,),
        in_specs=[
            pl.BlockSpec((MOE_STEP, DW), row_map),
            pl.BlockSpec(memory_space=pl.ANY),
            pl.BlockSpec((None, None, 1, 2 * D_FF), b_map),
            pl.BlockSpec(memory_space=pl.ANY),
            pl.BlockSpec((None, None, 1, D), b_map),
        ],
        out_specs=pl.BlockSpec((MOE_STEP, DW), row_map),
        scratch_shapes=[pltpu.VMEM((2, D, 2 * D_FF), F32), pltpu.VMEM((2, D_FF, D), F32),
                        pltpu.VMEM((D, 2 * D_FF), BF16), pltpu.VMEM((D_FF, D), BF16),
                        pltpu.SemaphoreType.DMA((2, 2))],
    )
    return pl.pallas_call(
        functools.partial(_expert_kernel, l),
        grid_spec=grid_spec,
        out_shape=jax.ShapeDtypeStruct((MOE_ROWS, DW), jnp.int32),
        compiler_params=pltpu.CompilerParams(
            dimension_semantics=("arbitrary",), vmem_limit_bytes=VMEM_LIMIT),
        name="experts",
    )(tile_expert, n_used, next_expert, n_valid, slot, x_sorted, w_gu, b_gu.reshape(DEPTH, N_EXPERTS, 1, 2 * D_FF),
      w_dn, b_dn.reshape(DEPTH, N_EXPERTS, 1, D))


def _combine_kernel(x_ref, y_ref, route_ref, mod_ref, o_ref):
    g2 = mod_ref[...][:, 5 * D:6 * D]
    r = route_ref[...]
    acc = r[:, 0:1] * _unpack_rows(y_ref[0]).astype(F32)
    for k in range(1, TOP_K):
        acc = acc + r[:, k:k + 1] * _unpack_rows(y_ref[k]).astype(F32)
    o_ref[...] = x_ref[...] + g2 * acc


def _combine(l, first_row, n_rows, x_all, y_tok, gates, mod):
    first_tile, n_tiles = first_row // TC, n_rows // TC
    return pl.pallas_call(
        _combine_kernel,
        grid=(n_tiles,),
        in_specs=[
            pl.BlockSpec((TC, D), lambda i: (i + first_tile, 0)),
            pl.BlockSpec((TOP_K, TC, DW), lambda i: (0, i, 0)),
            pl.BlockSpec((TC, LANE), lambda i: (i + first_tile, 0)),
            pl.BlockSpec((None, None, 1, 6 * D), lambda i: (l, _mod_row(i + first_tile, TC), 0, 0)),
        ],
        out_specs=pl.BlockSpec((TC, D), lambda i: (i, 0)),
        out_shape=jax.ShapeDtypeStruct((n_rows, D), F32),
        compiler_params=pltpu.CompilerParams(
            dimension_semantics=("arbitrary",), vmem_limit_bytes=VMEM_LIMIT),
        name="combine",
    )(x_all, y_tok, gates, mod)


SC_CORES, SC_SUBCORES = 2, 16
SC_WORKERS = SC_CORES * SC_SUBCORES
SC_CHUNK = 128
SC_CHUNKS_PER_WORKER = T_ALL // SC_CHUNK // SC_WORKERS


def _sc_mesh():
    return plsc.VectorSubcoreMesh(core_axis_name="c", subcore_axis_name="s")


def _sc_scratch():
    return [pltpu.VMEM((TOP_K, SC_CHUNK), jnp.int32), pltpu.VMEM((SC_CHUNK, DW), jnp.int32),
            pltpu.SemaphoreType.DMA]


def _dispatch_rows(h2p, dest):
    @functools.partial(pl.kernel, mesh=_sc_mesh(), out_type=jax.ShapeDtypeStruct((MOE_ROWS, DW), jnp.int32),
                       scratch_types=_sc_scratch(), name="dispatch_rows")
    def run(h_hbm, d_hbm, o_hbm, idx_v, rows_v, sem):
        wid = lax.axis_index("s") * SC_CORES + lax.axis_index("c")
        for j in range(SC_CHUNKS_PER_WORKER):
            c = wid * SC_CHUNKS_PER_WORKER + j
            pltpu.sync_copy(d_hbm.at[c], idx_v)
            pltpu.sync_copy(h_hbm.at[pl.ds(c * SC_CHUNK, SC_CHUNK)], rows_v)
            copies = [pltpu.async_copy(rows_v, o_hbm.at[idx_v.at[k]], sem) for k in range(TOP_K)]
            for cp in copies:
                cp.wait()

    return run(h2p, dest)


def _gather_rows(y, dest, first_chunk, n_chunks, *after):
    per_worker = n_chunks // SC_WORKERS
    assert per_worker * SC_WORKERS == n_chunks

    @functools.partial(pl.kernel, mesh=_sc_mesh(),
                       out_type=jax.ShapeDtypeStruct((TOP_K, n_chunks * SC_CHUNK, DW), jnp.int32),
                       scratch_types=_sc_scratch(), name="gather_rows")
    def run(y_hbm, d_hbm, *rest):
        o_hbm, idx_v, rows_v, sem = rest[len(after):]
        wid = lax.axis_index("s") * SC_CORES + lax.axis_index("c")
        for j in range(per_worker):
            c = wid * per_worker + j
            pltpu.sync_copy(d_hbm.at[first_chunk + c], idx_v)
            for k in range(TOP_K):
                pltpu.async_copy(y_hbm.at[idx_v.at[k]], rows_v, sem).wait()
                pltpu.sync_copy(rows_v, o_hbm.at[k, pl.ds(c * SC_CHUNK, SC_CHUNK)])

    return run(y, dest, *after)


def _constants():
    lane64 = np.arange(4 * HD)
    seg64 = (lane64[:, None] // HD == lane64[None, :] // HD).astype(np.float32)
    lane96 = np.arange(4 * BP)
    real = lane96 % BP < B_QK
    seg96 = ((lane96[:, None] // BP == lane96[None, :] // BP) & real[:, None] & real[None, :]).astype(np.float32)
    place = np.zeros((LANE, 4 * BP), np.float32)
    for hh in range(4):
        place[np.arange(B_ROPE), hh * BP + B_NOPE + np.arange(B_ROPE)] = 1.0

    def angles(rot_dim):
        pos = np.arange(LAT_LEN)
        rows = (pos // GRID_W).astype(np.float32)
        cols = (pos % GRID_W).astype(np.float32)
        axis_dim = rot_dim // 2
        inv = np.power(np.float32(ROPE_THETA), -(np.arange(0, axis_dim, 2, dtype=np.float32) / np.float32(axis_dim)))
        ang = np.concatenate([rows[:, None] * inv, cols[:, None] * inv], axis=-1).astype(np.float32)
        return np.cos(ang), np.sin(ang)

    def head_tables(rot_dim):
        cos, sin = angles(rot_dim)
        q = rot_dim // 4
        cr, cc, sr, sc = cos[:, :q], cos[:, q:], sin[:, :q], sin[:, q:]
        return (np.concatenate([cr, cr, cc, cc], axis=-1), np.concatenate([-sr, sr, -sc, sc], axis=-1))

    c64, s64 = head_tables(HD)
    cos64 = np.tile(c64, (1, 4))
    sin64 = np.tile(s64, (1, 4))
    c32, s32 = head_tables(B_ROPE)
    ones = np.ones((LAT_LEN, B_NOPE), np.float32)
    zeros = np.zeros((LAT_LEN, B_NOPE), np.float32)
    padz = np.zeros((LAT_LEN, BP - B_QK), np.float32)
    cos96 = np.tile(np.concatenate([ones, c32, padz], axis=-1), (1, 4))
    sin96 = np.tile(np.concatenate([zeros, s32, padz], axis=-1), (1, 4))
    first64 = ((lane64 % 32) < 16).astype(np.float32)[None, :]
    first96 = (((lane96 % BP) % 16) < 8).astype(np.float32)[None, :]
    tri = (np.arange(TP)[:, None] < np.arange(TP)[None, :]).astype(np.float32)
    f32 = lambda a: jnp.asarray(a, F32)
    return dict(seg64=jnp.asarray(seg64, BF16), seg96=jnp.asarray(seg96, BF16), place=jnp.asarray(place, BF16),
                tri=jnp.asarray(tri, BF16),
                cos64=f32(cos64), sin64=f32(sin64), cos96=f32(cos96), sin96=f32(sin96),
                first64=f32(first64), first96=f32(first96))


def _pad_heads(w, per_head, width):
    lead = w.shape[:-1]
    w = w.reshape(lead + (4, per_head))
    return jnp.pad(w, ((0, 0),) * (len(lead) + 1) + ((0, width - per_head),)).reshape(lead + (4 * width,))


def _weights(norm1_g, norm2_g, w_in, a_q_g, a_k_g, b_cq_g, b_ckv_g, w_uq, w_ukv, b_q_g, b_k_g,
             c_q_g, c_k_g, w_out, router_w, router_b):
    o = np.cumsum((0, 256, 128, 128, 384, 256, 32, 256, 128, 128))
    seg = lambda k: w_in[:, :, o[k]:o[k + 1]]
    w_in_r = jnp.concatenate([seg(0), seg(1), seg(2), seg(3), seg(4), seg(6), seg(7), seg(8), seg(5),
                              jnp.zeros((DEPTH, D, LANE - B_ROPE), F32)], axis=-1).astype(BF16)
    ukv = w_ukv.reshape(DEPTH, B_KV_LORA, 4, B_NOPE + B_V)
    w_uk = _pad_heads(ukv[..., :B_NOPE].reshape(DEPTH, B_KV_LORA, 4 * B_NOPE), B_NOPE, BP)
    w_uv = ukv[..., B_NOPE:].reshape(DEPTH, B_KV_LORA, 4 * B_V)
    rw_hi = router_w.astype(BF16)
    rw_lo = (router_w - rw_hi.astype(F32)).astype(BF16)
    tile = lambda v, n: jnp.tile(v, (1, n))
    rows = dict(n1g=norm1_g, n2g=norm2_g, aqg=tile(a_q_g, 4), akg=tile(a_k_g, 2), cqg=tile(c_q_g, 4),
                ckg=tile(c_k_g, 2), bcqg=b_cq_g, bckvg=b_ckv_g,
                bqg=_pad_heads(tile(b_q_g, 4), B_QK, BP), bkg=_pad_heads(tile(b_k_g, 4), B_QK, BP))
    order = sorted(VEC_ROWS, key=lambda n: VEC_ROWS[n][0])
    assert all(rows[n].shape == (DEPTH, VEC_ROWS[n][1]) for n in order)
    vec = jnp.stack([jnp.pad(rows[n], ((0, 0), (0, D - rows[n].shape[1]))) for n in order], axis=1)
    return dict(
        vec=vec[:, :, None, :], w_in=w_in_r,
        w_uq=_pad_heads(w_uq, B_QK, BP).astype(BF16), w_uk=w_uk.astype(BF16), w_uv=w_uv.astype(BF16),
        w_out=w_out.astype(BF16),
        rwt=jnp.concatenate([jnp.swapaxes(rw_hi, 1, 2), jnp.swapaxes(rw_lo, 1, 2)], axis=1),
        rb=router_b[:, :, None])


def _moe(l, x_all, h2p, gate_slab, route, counts, mod, w_gu, b_gu, w_dn, b_dn):
    counts = counts[:, 0].astype(jnp.int32)
    padded = (counts + MOE_STEP - 1) // MOE_STEP * MOE_STEP
    pend = jnp.cumsum(padded)
    pstart = pend - padded
    n_used = (pend[-1] // MOE_STEP).astype(jnp.int32)
    steps = jnp.minimum(jnp.arange(MOE_STEPS, dtype=jnp.int32), n_used - 1)
    tile_expert = jnp.sum((pend[None, :] <= steps[:, None] * MOE_STEP).astype(jnp.int32), axis=1)
    tile_expert = jnp.minimum(tile_expert, N_EXPERTS - 1)
    of_expert = lambda table: jnp.sum(
        jnp.where(tile_expert[:, None] == jnp.arange(N_EXPERTS)[None, :], table[None, :], 0), axis=1)
    group_end_step = of_expert(pend) // MOE_STEP
    after = jnp.sum((pend[None, :] <= group_end_step[:, None] * MOE_STEP).astype(jnp.int32), axis=1)
    next_expert = jnp.where(group_end_step < n_used, jnp.minimum(after, N_EXPERTS - 1), -1).astype(jnp.int32)
    rows_left = of_expert(pstart + counts) - steps * MOE_STEP
    n_valid = jnp.clip((rows_left + MOE_TM - 1) // MOE_TM, 1, MOE_STEP // MOE_TM).astype(jnp.int32)
    e = route[TOP_K:2 * TOP_K].astype(jnp.int32)
    rank = route[2 * TOP_K:3 * TOP_K].astype(jnp.int32)
    start = jnp.sum(jnp.where(e[:, :, None] == jnp.arange(N_EXPERTS)[None, None, :], pstart[None, None, :], 0), axis=-1)
    dest = (start + rank).reshape(TOP_K, T_ALL // SC_CHUNK, SC_CHUNK).transpose(1, 0, 2)
    x_sorted = _dispatch_rows(h2p, dest)
    changed = jnp.concatenate([jnp.zeros((1,), jnp.int32), (tile_expert[1:] != tile_expert[:-1]).astype(jnp.int32)])
    slot = jnp.cumsum(changed) % 2
    y = _experts(l, tile_expert, n_used.reshape(1), next_expert, n_valid, slot.astype(jnp.int32), x_sorted,
                 w_gu, b_gu, w_dn, b_dn)
    ctx_chunks, lat_chunks = T_CTX // SC_CHUNK, T_LAT // SC_CHUNK
    y_ctx = _gather_rows(y, dest, 0, ctx_chunks)
    if l < DEPTH - 1:
        y_lat = _gather_rows(y, dest, ctx_chunks, lat_chunks, y_ctx)
        return (y_ctx, gate_slab), _combine(l, T_CTX, T_LAT, x_all, y_lat, gate_slab, mod)
    out_ctx = _combine(l, 0, T_CTX, x_all, y_ctx, gate_slab, mod)
    y_lat = _gather_rows(y, dest, ctx_chunks, lat_chunks, out_ctx)
    return out_ctx, _combine(l, T_CTX, T_LAT, x_all, y_lat, gate_slab, mod)


def kernel(x_prompt, x_sample, cache_a_k, cache_a_v, cache_b_ckv, cache_b_krope, cache_c_k, cache_c_v, c, c_ctx,
           norm1_g, norm2_g, w_mod, b_mod, w_in, a_q_g, a_k_g, a_sink, b_cq_g, b_ckv_g, w_uq, w_ukv, b_q_g, b_k_g,
           c_q_g, c_k_g, w_out, router_w, router_b, w_gu, b_gu, w_dn, b_dn):
    consts = _constants()
    cond = jnp.concatenate([c_ctx[None, :], c, jnp.zeros((3, D), F32)], axis=0)
    mod = _modulation(cond, w_mod, b_mod).reshape(DEPTH, 8, 1, 6 * D)
    x_ctx, x_lat, x_lat_off = x_prompt.reshape(T_CTX, D), x_sample.reshape(T_LAT, D), 0

    lw = _weights(norm1_g, norm2_g, w_in, a_q_g, a_k_g, b_cq_g, b_ckv_g, w_uq, w_ukv, b_q_g, b_k_g,
                  c_q_g, c_k_g, w_out, router_w, router_b)
    ckb, cvb = _cache_kv(cache_b_ckv, cache_b_krope, lw, consts)
    merge_heads = lambda a: a.reshape(N_LAT_SEQ, DEPTH, PAST, 2 * HD).astype(BF16)
    cache = dict(ka=merge_heads(cache_a_k), va=merge_heads(cache_a_v), kb=ckb, vb=cvb,
                 kc=merge_heads(cache_c_k), vc=merge_heads(cache_c_v))

    states = ()
    names = ["qa", "ka", "va", "qb", "kb", "vb", "qc", "kc", "vc"]
    pending = None
    for l in range(DEPTH):
        outs = _projection(False, l, x_ctx, 0, mod, lw, consts, states, a_sink, pending)
        if pending is None:
            mix_ctx, states = outs[0], tuple(outs[1:])
        else:
            mix_ctx, x_ctx, states = outs[0], outs[1], tuple(outs[2:])
        p_lat = dict(zip(names, _projection(True, l, x_lat, x_lat_off, mod, lw, consts)))
        mix_lat = _attention_lat(l, a_sink, p_lat, cache)
        x_mid, h2p, gate_slab, route, counts = _post_attention(l, mix_ctx, mix_lat, x_ctx, x_lat, x_lat_off, mod, lw,
                                                               consts)
        ctx_out, x_lat = _moe(l, x_mid, h2p, gate_slab, route, counts, mod, w_gu, b_gu, w_dn, b_dn)
        if l < DEPTH - 1:
            x_ctx, pending = x_mid, ctx_out
        else:
            x_ctx = ctx_out
    y_ctx, y_lat = x_ctx, x_lat

    return (y_ctx.reshape(N_CTX_SEQ, CTX_LEN, D), y_lat.reshape(N_LAT_SEQ, LAT_LEN, D)) + states
```
